```python
import math
import jax, jax.numpy as jnp
from jax import lax
import numpy as np

D_MODEL = 2048
BATCH = 4
SEQ = 2048
DEPTH = 2

GRID_W = 64
HEAD_DIM = 128
N_Q_HEADS = 12
N_KV_HEADS = 4
GQA_GROUP = N_Q_HEADS // N_KV_HEADS
ATTN_WIDTH = N_Q_HEADS * HEAD_DIM
KV_WIDTH = N_KV_HEADS * HEAD_DIM
Q_BLOCK = 128
ROPE_THETA = 10000.0
ROPE_AXIS_DIM = HEAD_DIM // 2
N_FOURIER_GROUPS = 4
FOURIER_GROUP = 128
FOURIER_WIDTH = N_FOURIER_GROUPS * FOURIER_GROUP
MIX_WIDTH = ATTN_WIDTH + FOURIER_WIDTH
IN_WIDTH = ATTN_WIDTH + 2 * KV_WIDTH + FOURIER_WIDTH
POOL_WINDOWS = (2, 4, 8, 16)
N_POOL_GROUPS = len(POOL_WINDOWS)
POOL_GROUP = D_MODEL // N_POOL_GROUPS
MEM_LEN = 256
N_MEM_HEADS = 4
MEM_HEAD_DIM = D_MODEL // N_MEM_HEADS
N_EXPERTS = 16
EXPERT_FF = D_MODEL // 2
EC_CAPACITY_FACTOR = 2
NORM_EPS = 1e-6
N_EVEN = (DEPTH + 1) // 2
N_ODD = DEPTH // 2

kernel_name = "hybrid_attn_fourier_pool_ec_moe_encoder"


def rms_norm(x, gain):
    xf = x.astype(jnp.float32)
    y = xf * lax.rsqrt(jnp.mean(xf * xf, axis=-1, keepdims=True) + NORM_EPS)
    return (y * gain.astype(jnp.float32)).astype(x.dtype)


def axial_rope_tables(seq_len):
    rows = seq_len // GRID_W
    row_idx = jnp.repeat(jnp.arange(rows), GRID_W).astype(jnp.float32)
    col_idx = jnp.tile(jnp.arange(GRID_W), rows).astype(jnp.float32)
    inv_freq = 1.0 / (ROPE_THETA ** (jnp.arange(0, ROPE_AXIS_DIM, 2, dtype=jnp.float32) / ROPE_AXIS_DIM))
    ang = jnp.concatenate([row_idx[:, None] * inv_freq[None, :],
                           col_idx[:, None] * inv_freq[None, :]], axis=-1)
    return jnp.cos(ang), jnp.sin(ang)


def apply_axial_rope(x, cos, sin):
    xf = x.astype(jnp.float32)
    x1 = xf[..., 0::2]
    x2 = xf[..., 1::2]
    c = cos[None, :, None, :]
    s = sin[None, :, None, :]
    out = jnp.stack([x1 * c - x2 * s, x1 * s + x2 * c], axis=-1).reshape(x.shape)
    return out.astype(x.dtype)


def blocked_bidirectional_gqa(q, k, v):
    b, s = q.shape[0], q.shape[1]
    n_blocks = s // Q_BLOCK
    scale = HEAD_DIM ** -0.5
    qb = q.reshape(b, n_blocks, Q_BLOCK, N_KV_HEADS, GQA_GROUP, HEAD_DIM).transpose(1, 0, 2, 3, 4, 5)

    def one_block(q_blk):
        scores = jnp.einsum('bqkgd,bskd->bkgqs', q_blk, k,
                            preferred_element_type=jnp.float32) * scale
        probs = jax.nn.softmax(scores, axis=-1).astype(v.dtype)
        return jnp.einsum('bkgqs,bskd->bqkgd', probs, v)

    o = lax.map(one_block, qb)
    return o.transpose(1, 0, 2, 3, 4, 5).reshape(b, s, ATTN_WIDTH)


def fourier_groups(u, w_fourier):
    z = jnp.fft.fft2(u.astype(jnp.float32), axes=(1, 3), norm='ortho').real
    return jnp.einsum('bsgc,gcd->bsgd', z.astype(u.dtype), w_fourier)


def attn_fourier_mixer(h, cos, sin, w_in, q_gain, k_gain, w_fourier, w_out):
    b, s, _ = h.shape
    proj = h @ w_in
    q, k, v, f = jnp.split(proj, [ATTN_WIDTH, ATTN_WIDTH + KV_WIDTH,
                                  ATTN_WIDTH + 2 * KV_WIDTH], axis=-1)
    q = apply_axial_rope(rms_norm(q.reshape(b, s, N_Q_HEADS, HEAD_DIM), q_gain), cos, sin)
    k = apply_axial_rope(rms_norm(k.reshape(b, s, N_KV_HEADS, HEAD_DIM), k_gain), cos, sin)
    q = q.reshape(b, s, N_KV_HEADS, GQA_GROUP, HEAD_DIM)
    v = v.reshape(b, s, N_KV_HEADS, HEAD_DIM)
    o_attn = blocked_bidirectional_gqa(q, k, v)
    o_four = fourier_groups(f.reshape(b, s, N_FOURIER_GROUPS, FOURIER_GROUP),
                            w_fourier).reshape(b, s, FOURIER_WIDTH)
    return jnp.concatenate([o_attn, o_four], axis=-1) @ w_out


def centred_mean_minus_self(u, window):
    b, s, c = u.shape
    t = jnp.arange(s)
    lo = jnp.clip(t - window // 2, 0, s)
    hi = jnp.clip(t + window - window // 2, 0, s)
    uf = u.astype(jnp.float32)
    cs = jnp.concatenate([jnp.zeros((b, 1, c), jnp.float32), jnp.cumsum(uf, axis=1)], axis=1)
    total = cs[:, hi] - cs[:, lo]
    count = (hi - lo).astype(jnp.float32)[None, :, None]
    return (total / count - uf).astype(u.dtype)


def pool_mixer(h, w_pool, pool_scale):
    b, s, _ = h.shape
    groups = h.reshape(b, s, N_POOL_GROUPS, POOL_GROUP)
    pooled = jnp.stack([centred_mean_minus_self(groups[:, :, g], w)
                        for g, w in enumerate(POOL_WINDOWS)], axis=2)
    mixed = jnp.einsum('bsgc,gcd->bsgd', pooled, w_pool).reshape(b, s, D_MODEL)
    return mixed * pool_scale


def memory_cross_attention(h, mem_n, w_q, w_k, w_v, w_o):
    b, s, _ = h.shape
    m = mem_n.shape[1]
    q = (h @ w_q).reshape(b, s, N_MEM_HEADS, MEM_HEAD_DIM)
    k = (mem_n @ w_k).reshape(b, m, N_MEM_HEADS, MEM_HEAD_DIM)
    v = (mem_n @ w_v).reshape(b, m, N_MEM_HEADS, MEM_HEAD_DIM)
    scores = jnp.einsum('bqhd,bmhd->bhqm', q, k,
                        preferred_element_type=jnp.float32) * (MEM_HEAD_DIM ** -0.5)
    probs = jax.nn.softmax(scores, axis=-1).astype(v.dtype)
    o = jnp.einsum('bhqm,bmhd->bqhd', probs, v).reshape(b, s, D_MODEL)
    return o @ w_o


def expert_choice_moe(h, w_router, w_gate, w_up, w_down):
    b, s, d = h.shape
    capacity = EC_CAPACITY_FACTOR * s // N_EXPERTS
    logits = jnp.einsum('bsd,de->bse', h, w_router, preferred_element_type=jnp.float32)
    affinity = jax.nn.softmax(logits, axis=-1)
    gate, idx = lax.top_k(affinity.transpose(0, 2, 1), capacity)
    xs = jax.vmap(lambda hb, ib: hb[ib])(h, idx)
    a = jnp.einsum('becd,edf->becf', xs, w_gate)
    u = jnp.einsum('becd,edf->becf', xs, w_up)
    y = jnp.einsum('becf,efd->becd', jax.nn.silu(a) * u, w_down)
    y = y * gate[..., None].astype(y.dtype)
    return jax.vmap(lambda yb, ib: jnp.zeros((s, d), yb.dtype)
                    .at[ib.reshape(-1)].add(yb.reshape(-1, d)))(y, idx)


def setup_inputs(seed: int = 0) -> dict:
    key = jax.random.key(seed)
    ks = jax.random.split(key, 24)
    f32 = jnp.float32

    def w(k, shape, fan_in):
        return jax.random.normal(k, shape, f32) * (fan_in ** -0.5)

    def gain(k, shape):
        return 1.0 + 0.02 * jax.random.normal(k, shape, f32)

    return {
        "x": jax.random.normal(ks[0], (BATCH, SEQ, D_MODEL), f32),
        "mem": jax.random.normal(ks[1], (BATCH, MEM_LEN, D_MODEL), f32),
        "mix_norm": gain(ks[2], (DEPTH, D_MODEL)),
        "attn_w_in": w(ks[3], (N_EVEN, D_MODEL, IN_WIDTH), D_MODEL),
        "q_gain": gain(ks[4], (N_EVEN, HEAD_DIM)),
        "k_gain": gain(ks[5], (N_EVEN, HEAD_DIM)),
        "fourier_w": w(ks[6], (N_EVEN, N_FOURIER_GROUPS, FOURIER_GROUP, FOURIER_GROUP), FOURIER_GROUP),
        "attn_w_out": w(ks[7], (N_EVEN, MIX_WIDTH, D_MODEL), MIX_WIDTH),
        "pool_w": w(ks[8], (N_ODD, N_POOL_GROUPS, POOL_GROUP, POOL_GROUP), POOL_GROUP),
        "pool_scale": gain(ks[9], (N_ODD, D_MODEL)),
        "cross_norm": gain(ks[10], (DEPTH, D_MODEL)),
        "mem_norm": gain(ks[11], (DEPTH, D_MODEL)),
        "cross_w_q": w(ks[12], (DEPTH, D_MODEL, D_MODEL), D_MODEL),
        "cross_w_k": w(ks[13], (DEPTH, D_MODEL, D_MODEL), D_MODEL),
        "cross_w_v": w(ks[14], (DEPTH, D_MODEL, D_MODEL), D_MODEL),
        "cross_w_o": w(ks[15], (DEPTH, D_MODEL, D_MODEL), D_MODEL),
        "ffn_norm": gain(ks[16], (DEPTH, D_MODEL)),
        "router_w": w(ks[17], (DEPTH, D_MODEL, N_EXPERTS), D_MODEL),
        "expert_w_gate": w(ks[18], (DEPTH, N_EXPERTS, D_MODEL, EXPERT_FF), D_MODEL),
        "expert_w_up": w(ks[19], (DEPTH, N_EXPERTS, D_MODEL, EXPERT_FF), D_MODEL),
        "expert_w_down": w(ks[20], (DEPTH, N_EXPERTS, EXPERT_FF, D_MODEL), EXPERT_FF),
        "final_norm": gain(ks[21], (D_MODEL,)),
    }


def reference(x, mem, mix_norm, attn_w_in, q_gain, k_gain, fourier_w, attn_w_out,
              pool_w, pool_scale, cross_norm, mem_norm, cross_w_q, cross_w_k,
              cross_w_v, cross_w_o, ffn_norm, router_w, expert_w_gate, expert_w_up,
              expert_w_down, final_norm):
    seq_len = x.shape[1]
    cos, sin = axial_rope_tables(seq_len)
    for layer in range(DEPTH):
        h = rms_norm(x, mix_norm[layer])
        i = layer // 2
        if layer % 2 == 0:
            x = x + attn_fourier_mixer(h, cos, sin, attn_w_in[i], q_gain[i], k_gain[i],
                                       fourier_w[i], attn_w_out[i])
        else:
            x = x + pool_mixer(h, pool_w[i], pool_scale[i])
        x = x + memory_cross_attention(rms_norm(x, cross_norm[layer]),
                                       rms_norm(mem, mem_norm[layer]),
                                       cross_w_q[layer], cross_w_k[layer],
                                       cross_w_v[layer], cross_w_o[layer])
        x = x + expert_choice_moe(rms_norm(x, ffn_norm[layer]), router_w[layer],
                                  expert_w_gate[layer], expert_w_up[layer],
                                  expert_w_down[layer])
    return rms_norm(x, final_norm)
```

```python
import functools
import math

import jax
import jax.numpy as jnp
from jax import lax
from jax.experimental import pallas as pl
from jax.experimental.pallas import tpu as pltpu

F32 = jnp.float32
BF16 = jnp.bfloat16

GRID_W = 64
HEAD_DIM = 128
N_Q_HEADS = 12
N_KV_HEADS = 4
N_FOURIER_GROUPS = 4
FOURIER_GROUP = 128
ROPE_THETA = 10000.0
POOL_WINDOWS = (2, 4, 8, 16)
N_MEM_HEADS = 4
EC_CAPACITY_FACTOR = 2
NORM_EPS = 1e-6

LANES = 128
V7X_VMEM_BYTES = 64 * 1024 * 1024
VMEM_HEADROOM_BYTES = 6 * 1024 * 1024


def _params(semantics, vmem_bytes):
    limit = min(int(vmem_bytes) + VMEM_HEADROOM_BYTES, V7X_VMEM_BYTES - VMEM_HEADROOM_BYTES)
    return pltpu.CompilerParams(dimension_semantics=semantics, vmem_limit_bytes=limit)


def _tile(n, pref):
    t = min(n, pref)
    while n % t:
        t //= 2
    return t


def _rms(x, gain):
    ms = jnp.mean(x * x, axis=-1, keepdims=True)
    return x * lax.rsqrt(ms + NORM_EPS) * gain


def _norm_kernel(x_ref, g_ref, o_ref):
    o_ref[...] = _rms(x_ref[...], g_ref[...]).astype(o_ref.dtype)


def rms_norm_rows(x, gain, out_dtype):
    m, d = x.shape
    tm = _tile(m, 512)
    return pl.pallas_call(
        _norm_kernel,
        grid=(m // tm,),
        in_specs=[pl.BlockSpec((tm, d), lambda i: (i, 0)),
                  pl.BlockSpec((1, d), lambda i: (0, 0))],
        out_specs=pl.BlockSpec((tm, d), lambda i: (i, 0)),
        out_shape=jax.ShapeDtypeStruct((m, d), out_dtype),
        compiler_params=_params(("parallel",), 2 * tm * d * 8),
        name="rms_norm_rows",
    )(x, gain.reshape(1, d))


def _mm_kernel(a_ref, w_ref, o_ref):
    o_ref[...] = jnp.dot(a_ref[...], w_ref[...].astype(BF16),
                         preferred_element_type=F32).astype(o_ref.dtype)


def matmul_bf16(a, w, out_dtype=BF16, tm_pref=1024, tn_pref=512):
    m, k = a.shape
    n = w.shape[1]
    tm, tn = _tile(m, tm_pref), _tile(n, tn_pref)
    vmem = 2 * (tm * k * 2 + k * tn * 4 + tm * tn * 4) + k * tn * 2
    return pl.pallas_call(
        _mm_kernel,
        grid=(m // tm, n // tn),
        in_specs=[pl.BlockSpec((tm, k), lambda i, j: (i, 0)),
                  pl.BlockSpec((k, tn), lambda i, j: (0, j))],
        out_specs=pl.BlockSpec((tm, tn), lambda i, j: (i, j)),
        out_shape=jax.ShapeDtypeStruct((m, n), out_dtype),
        compiler_params=_params(("parallel", "parallel"), vmem),
        name="matmul_bf16",
    )(a, w)


def _proj_in_kernel(a_ref, w_ref, g_ref, cos_ref, sa_ref, sb_ref, o_ref, *, n_rope_blocks):
    j = pl.program_id(1)
    acc = jnp.dot(a_ref[...], w_ref[...].astype(BF16), preferred_element_type=F32)
    tn = acc.shape[1]

    @pl.when(j < n_rope_blocks)
    def _():
        heads = []
        for h in range(tn // HEAD_DIM):
            xh = acc[:, h * HEAD_DIM:(h + 1) * HEAD_DIM]
            ms = jnp.mean(xh * xh, axis=-1, keepdims=True)
            heads.append(xh * lax.rsqrt(ms + NORM_EPS))
        y = jnp.concatenate(heads, axis=-1) * g_ref[...]
        nxt = pltpu.roll(y, tn - 1, axis=1)
        prv = pltpu.roll(y, 1, axis=1)
        o_ref[...] = (y * cos_ref[...] + nxt * sa_ref[...] + prv * sb_ref[...]).astype(o_ref.dtype)

    @pl.when(j >= n_rope_blocks)
    def _():
        o_ref[...] = acc.astype(o_ref.dtype)


def proj_in(h, w_in, gain_cols, cos_t, sin_a, sin_b, *, seq, rope_width, tn):
    m, k = h.shape
    n = w_in.shape[1]
    tm = _tile(seq, 512)
    n_pos_blocks = seq // tm
    vmem = 2 * (tm * k * 2 + k * tn * 4 + tm * tn * 2 + 3 * tm * tn * 4) + k * tn * 2 + 6 * tm * tn * 4
    table = pl.BlockSpec((tm, tn), lambda i, j: (i % n_pos_blocks, 0))
    return pl.pallas_call(
        functools.partial(_proj_in_kernel, n_rope_blocks=rope_width // tn),
        grid=(m // tm, n // tn),
        in_specs=[pl.BlockSpec((tm, k), lambda i, j: (i, 0)),
                  pl.BlockSpec((k, tn), lambda i, j: (0, j)),
                  pl.BlockSpec((1, tn), lambda i, j: (0, j)),
                  table, table, table],
        out_specs=pl.BlockSpec((tm, tn), lambda i, j: (i, j)),
        out_shape=jax.ShapeDtypeStruct((m, n), BF16),
        compiler_params=_params(("parallel", "parallel"), vmem),
        name="proj_in",
    )(h, w_in, gain_cols, cos_t, sin_a, sin_b)


def _attn_kernel(q_ref, k_ref, v_ref, o_ref, *, group, scale):
    k = k_ref[...]
    v = v_ref[...]
    for g in range(group):
        q = q_ref[:, g * HEAD_DIM:(g + 1) * HEAD_DIM]
        s = lax.dot_general(q, k, (((1,), (1,)), ((), ())), preferred_element_type=F32) * scale
        e = jnp.exp(s - jnp.max(s, axis=-1, keepdims=True))
        l = jnp.sum(e, axis=-1, keepdims=True)
        o = jnp.dot(e.astype(BF16), v, preferred_element_type=F32) / l
        o_ref[:, g * HEAD_DIM:(g + 1) * HEAD_DIM] = o.astype(o_ref.dtype)


def gqa_attention(proj, *, batch, seq, attn_width, kv_width):
    group = N_Q_HEADS // N_KV_HEADS
    gw = group * HEAD_DIM
    tq = _tile(seq, 256)
    nq = seq // tq
    k_col0 = attn_width // HEAD_DIM
    v_col0 = (attn_width + kv_width) // HEAD_DIM
    vmem = 2 * (tq * gw * 2 * 2 + 2 * seq * HEAD_DIM * 2) + 4 * tq * seq * 4
    return pl.pallas_call(
        functools.partial(_attn_kernel, group=group, scale=HEAD_DIM ** -0.5),
        grid=(batch, N_KV_HEADS, nq),
        in_specs=[pl.BlockSpec((tq, gw), lambda b, h, i: (b * nq + i, h)),
                  pl.BlockSpec((seq, HEAD_DIM), lambda b, h, i: (b, k_col0 + h)),
                  pl.BlockSpec((seq, HEAD_DIM), lambda b, h, i: (b, v_col0 + h))],
        out_specs=pl.BlockSpec((tq, gw), lambda b, h, i: (b * nq + i, h)),
        out_shape=jax.ShapeDtypeStruct((batch * seq, attn_width), BF16),
        compiler_params=_params(("parallel", "parallel", "parallel"), vmem),
        name="gqa_attention",
    )(proj, proj, proj)


def _fourier_kernel(f_ref, cc_ref, sc_ref, cs_ref, ss_ref, wf_ref, o_ref, ab_ref, *, seq, norm):
    r = pl.program_id(1)
    ng = wf_ref.shape[0]
    cg = wf_ref.shape[1]

    @pl.when(r == 0)
    def _():
        for g in range(ng):
            fg = f_ref[:, g * cg:(g + 1) * cg]
            ab_ref[0:seq, g * cg:(g + 1) * cg] = jnp.dot(
                fg, cc_ref[...], preferred_element_type=F32).astype(BF16)
            ab_ref[seq:2 * seq, g * cg:(g + 1) * cg] = jnp.dot(
                fg, sc_ref[...], preferred_element_type=F32).astype(BF16)

    z = (jnp.dot(cs_ref[...], ab_ref[0:seq, :], preferred_element_type=F32)
         - jnp.dot(ss_ref[...], ab_ref[seq:2 * seq, :], preferred_element_type=F32)) * norm
    for g in range(ng):
        zg = z[:, g * cg:(g + 1) * cg].astype(BF16)
        o_ref[:, g * cg:(g + 1) * cg] = jnp.dot(
            zg, wf_ref[g].astype(BF16), preferred_element_type=F32).astype(o_ref.dtype)


def _dft_cos_sin(n):
    idx = jnp.arange(n, dtype=jnp.int32)
    kn = (idx[:, None] * idx[None, :]) % n
    ang = kn.astype(F32) * (2.0 * math.pi / n)
    return jnp.cos(ang).astype(BF16), jnp.sin(ang).astype(BF16)


def fourier_mixer(proj, w_fourier, *, batch, seq, col0):
    ng, cg, _ = w_fourier.shape
    fw = ng * cg
    tr = _tile(seq, 512)
    nr = seq // tr
    cc, sc = _dft_cos_sin(cg)
    cs, ss = _dft_cos_sin(seq)
    norm = 1.0 / math.sqrt(seq * cg)
    vmem = 2 * (seq * fw * 2 + 2 * cg * cg * 2 + 2 * tr * seq * 2 + ng * cg * cg * 4 + tr * fw * 2) \
        + 2 * seq * fw * 2 + 4 * tr * fw * 4
    return pl.pallas_call(
        functools.partial(_fourier_kernel, seq=seq, norm=norm),
        grid=(batch, nr),
        in_specs=[pl.BlockSpec((seq, fw), lambda b, r: (b, col0 // fw)),
                  pl.BlockSpec((cg, cg), lambda b, r: (0, 0)),
                  pl.BlockSpec((cg, cg), lambda b, r: (0, 0)),
                  pl.BlockSpec((tr, seq), lambda b, r: (r, 0)),
                  pl.BlockSpec((tr, seq), lambda b, r: (r, 0)),
                  pl.BlockSpec((ng, cg, cg), lambda b, r: (0, 0, 0))],
        out_specs=pl.BlockSpec((tr, fw), lambda b, r: (b * nr + r, 0)),
        out_shape=jax.ShapeDtypeStruct((batch * seq, fw), BF16),
        scratch_shapes=[pltpu.VMEM((2 * seq, fw), BF16)],
        compiler_params=_params(("parallel", "arbitrary"), vmem),
        name="fourier_mixer",
    )(proj, cc, sc, cs, ss, w_fourier)


def _router_affinity(hn, wr_hi_ref, wr_lo_ref, n_experts):
    hi = hn.astype(BF16)
    lo = (hn - hi.astype(F32)).astype(BF16)
    logits = (jnp.dot(hi, wr_hi_ref[...], preferred_element_type=F32)
              + jnp.dot(lo, wr_hi_ref[...], preferred_element_type=F32)
              + jnp.dot(hi, wr_lo_ref[...], preferred_element_type=F32))
    lane = lax.broadcasted_iota(jnp.int32, logits.shape, 1)
    logits = jnp.where(lane < n_experts, logits, -jnp.inf)
    e = jnp.exp(logits - jnp.max(logits, axis=-1, keepdims=True))
    return e / jnp.sum(e, axis=-1, keepdims=True)


def _mm_res_norm_kernel(*refs, n_a, steps_a, with_router, n_experts):
    a_refs = refs[:n_a]
    w_ref, x_ref, g_ref = refs[n_a:n_a + 3]
    pos = n_a + 3
    if with_router:
        wr_hi_ref, wr_lo_ref = refs[pos:pos + 2]
        pos += 2
    xo_ref, ho_ref = refs[pos:pos + 2]
    pos += 2
    if with_router:
        aff_ref = refs[pos]
        pos += 1
    acc_ref = refs[pos]
    k = pl.program_id(1)
    nk = pl.num_programs(1)

    @pl.when(k == 0)
    def _():
        acc_ref[...] = x_ref[...]

    w = w_ref[...].astype(BF16)
    lo = 0
    for a_ref, steps in zip(a_refs, steps_a):
        @pl.when((k >= lo) & (k < lo + steps))
        def _(a_ref=a_ref):
            acc_ref[...] += jnp.dot(a_ref[...], w, preferred_element_type=F32)
        lo += steps

    @pl.when(k == nk - 1)
    def _():
        x_new = acc_ref[...]
        xo_ref[...] = x_new
        hn = _rms(x_new, g_ref[...])
        ho_ref[...] = hn.astype(ho_ref.dtype)
        if with_router:
            aff_ref[...] = _router_affinity(hn, wr_hi_ref, wr_lo_ref, n_experts)


def matmul_residual_norm(a_list, w, x_res, gain, *, router=None, tk=512, tm_pref=512):
    m, d = x_res.shape
    tm = _tile(m, tm_pref)
    steps_a = tuple(a.shape[1] // tk for a in a_list)
    nk = sum(steps_a)
    starts = [sum(steps_a[:i]) for i in range(len(a_list))]
    in_specs = []
    for a, st, lo in zip(a_list, steps_a, starts):
        in_specs.append(pl.BlockSpec(
            (tm, tk), lambda i, k, st=st, lo=lo: (i, jnp.clip(k - lo, 0, st - 1))))
    in_specs += [pl.BlockSpec((tk, d), lambda i, k: (k, 0)),
                 pl.BlockSpec((tm, d), lambda i, k: (i, 0)),
                 pl.BlockSpec((1, d), lambda i, k: (0, 0))]
    args = list(a_list) + [w, x_res, gain.reshape(1, d)]
    out_specs = [pl.BlockSpec((tm, d), lambda i, k: (i, 0)),
                 pl.BlockSpec((tm, d), lambda i, k: (i, 0))]
    out_shape = [jax.ShapeDtypeStruct((m, d), F32), jax.ShapeDtypeStruct((m, d), BF16)]
    n_experts = 0
    if router is not None:
        wr_hi, wr_lo, n_experts = router
        in_specs += [pl.BlockSpec((d, LANES), lambda i, k: (0, 0)),
                     pl.BlockSpec((d, LANES), lambda i, k: (0, 0))]
        args += [wr_hi, wr_lo]
        out_specs.append(pl.BlockSpec((tm, LANES), lambda i, k: (i, 0)))
        out_shape.append(jax.ShapeDtypeStruct((m, LANES), F32))
    vmem = 2 * (len(a_list) * tm * tk * 2 + tk * d * 4 + tm * d * 4 + tm * d * 4 + tm * d * 2) \
        + tk * d * 2 + 5 * tm * d * 4 + 4 * d * LANES * 2
    return pl.pallas_call(
        functools.partial(_mm_res_norm_kernel, n_a=len(a_list), steps_a=steps_a,
                          with_router=router is not None, n_experts=n_experts),
        grid=(m // tm, nk),
        in_specs=in_specs,
        out_specs=out_specs,
        out_shape=out_shape,
        scratch_shapes=[pltpu.VMEM((tm, d), F32)],
        compiler_params=_params(("parallel", "arbitrary"), vmem),
        name="matmul_residual_norm",
    )(*args)


def _cross_attn_kernel(q_ref, k_ref, v_ref, o_ref, *, n_heads, scale):
    dh = q_ref.shape[1] // n_heads
    for h in range(n_heads):
        sl = slice(h * dh, (h + 1) * dh)
        s = lax.dot_general(q_ref[:, sl], k_ref[:, sl], (((1,), (1,)), ((), ())),
                            preferred_element_type=F32) * scale
        e = jnp.exp(s - jnp.max(s, axis=-1, keepdims=True))
        l = jnp.sum(e, axis=-1, keepdims=True)
        o = jnp.dot(e.astype(BF16), v_ref[:, sl], preferred_element_type=F32) / l
        o_ref[:, sl] = o.astype(o_ref.dtype)


def cross_attention(q, k, v, *, batch, seq, mem_len):
    d = q.shape[1]
    tq = _tile(seq, 512)
    nq = seq // tq
    vmem = 2 * (2 * tq * d * 2 + 2 * mem_len * d * 2) + 6 * tq * mem_len * 4 + 2 * tq * d * 4
    return pl.pallas_call(
        functools.partial(_cross_attn_kernel, n_heads=N_MEM_HEADS,
                          scale=(d // N_MEM_HEADS) ** -0.5),
        grid=(batch, nq),
        in_specs=[pl.BlockSpec((tq, d), lambda b, i: (b * nq + i, 0)),
                  pl.BlockSpec((mem_len, d), lambda b, i: (b, 0)),
                  pl.BlockSpec((mem_len, d), lambda b, i: (b, 0))],
        out_specs=pl.BlockSpec((tq, d), lambda b, i: (b * nq + i, 0)),
        out_shape=jax.ShapeDtypeStruct((batch * seq, d), BF16),
        compiler_params=_params(("parallel", "parallel"), vmem),
        name="cross_attention",
    )(q, k, v)


def _prefix_count(x, lane):
    n = x.shape[1]
    shift = 1
    while shift < n:
        x = x + jnp.where(lane >= shift, pltpu.roll(x, shift, axis=1), 0)
        shift *= 2
    return x


def _route_kernel(aff_ref, prow_ref, pcol_ref, *, n_experts, capacity):
    e_pad = prow_ref.shape[1]
    a_t = aff_ref[...].T[:e_pad]
    seq = a_t.shape[1]

    def body(i, thr_bits):
        cand = thr_bits | jnp.left_shift(jnp.int32(1), 30 - i)
        cnt = jnp.sum((a_t >= pltpu.bitcast(cand, F32)).astype(F32), axis=1, keepdims=True)
        return jnp.where(cnt >= capacity, cand, thr_bits)

    thr = pltpu.bitcast(lax.fori_loop(0, 31, body, jnp.zeros((e_pad, 1), jnp.int32)), F32)
    lane = lax.broadcasted_iota(jnp.int32, a_t.shape, 1)
    gt = a_t > thr
    eq = a_t == thr
    need = capacity - jnp.sum(gt.astype(F32), axis=1, keepdims=True).astype(jnp.int32)
    sel = gt | (eq & (_prefix_count(eq.astype(jnp.int32), lane) <= need))
    pos = jnp.where(sel, _prefix_count(sel.astype(jnp.int32), lane) - 1, -1)
    row = lax.broadcasted_iota(jnp.int32, a_t.shape, 0)
    pos = jnp.where(row < n_experts, pos, -1)
    prow_ref[0] = pos
    unused = jnp.full((LANES - e_pad, seq), -1.0, F32)
    pcol_ref[...] = jnp.concatenate([pos.astype(F32), unused], axis=0).T


def route(aff, *, batch, seq, n_experts, capacity):
    e_pad = max(8, n_experts)
    return pl.pallas_call(
        functools.partial(_route_kernel, n_experts=n_experts, capacity=capacity),
        grid=(batch,),
        in_specs=[pl.BlockSpec((seq, LANES), lambda b: (b, 0))],
        out_specs=[pl.BlockSpec((1, e_pad, seq), lambda b: (b, 0, 0)),
                   pl.BlockSpec((seq, LANES), lambda b: (b, 0))],
        out_shape=[jax.ShapeDtypeStruct((batch, e_pad, seq), jnp.int32),
                   jax.ShapeDtypeStruct((batch * seq, LANES), F32)],
        compiler_params=_params(("parallel",), 24 * seq * LANES * 4),
        name="route",
    )(aff)


def _gather_kernel(h_ref, prow_ref, o_ref):
    e = pl.program_id(1)
    cap = o_ref.shape[0]
    seq = h_ref.shape[0]
    prow = prow_ref[0, pl.ds(e, 1), :]
    slot = lax.broadcasted_iota(jnp.int32, (cap, seq), 0)
    onehot = (slot == prow).astype(BF16)
    o_ref[...] = jnp.dot(onehot, h_ref[...], preferred_element_type=F32).astype(o_ref.dtype)


def gather_tokens(h, prow, *, batch, seq, n_experts, capacity):
    d = h.shape[1]
    e_pad = prow.shape[1]
    vmem = 2 * (seq * d * 2 + e_pad * seq * 4 + capacity * d * 2) + capacity * seq * 8 + capacity * d * 4
    return pl.pallas_call(
        _gather_kernel,
        grid=(batch, n_experts),
        in_specs=[pl.BlockSpec((seq, d), lambda b, e: (b, 0)),
                  pl.BlockSpec((1, e_pad, seq), lambda b, e: (b, 0, 0))],
        out_specs=pl.BlockSpec((capacity, d), lambda b, e: (e * batch + b, 0)),
        out_shape=jax.ShapeDtypeStruct((n_experts * batch * capacity, d), BF16),
        compiler_params=_params(("parallel", "arbitrary"), vmem),
        name="gather_tokens",
    )(h, prow)


def _ffn_kernel(x_ref, wg_ref, wu_ref, wd_ref, o_ref, acc_ref):
    f = pl.program_id(1)
    x = x_ref[...]
    a = jnp.dot(x, wg_ref[0].astype(BF16), preferred_element_type=F32)
    u = jnp.dot(x, wu_ref[0].astype(BF16), preferred_element_type=F32)
    hidden = (a / (1.0 + jnp.exp(-a)) * u).astype(BF16)
    y = jnp.dot(hidden, wd_ref[0].astype(BF16), preferred_element_type=F32)

    @pl.when(f == 0)
    def _():
        acc_ref[...] = y

    @pl.when(f > 0)
    def _():
        acc_ref[...] += y

    @pl.when(f == pl.num_programs(1) - 1)
    def _():
        o_ref[...] = acc_ref[...].astype(o_ref.dtype)


def expert_ffn(xs, w_gate, w_up, w_down, *, rows_per_expert):
    n_experts, d, ff = w_gate.shape
    tf = _tile(ff, 256)
    r = rows_per_expert
    vmem = 2 * (r * d * 2 + 3 * d * tf * 4 + r * d * 2) + r * d * 4 + 3 * d * tf * 2 + 4 * r * tf * 4 + r * d * 4
    return pl.pallas_call(
        _ffn_kernel,
        grid=(n_experts, ff // tf),
        in_specs=[pl.BlockSpec((r, d), lambda e, f: (e, 0)),
                  pl.BlockSpec((1, d, tf), lambda e, f: (e, 0, f)),
                  pl.BlockSpec((1, d, tf), lambda e, f: (e, 0, f)),
                  pl.BlockSpec((1, tf, d), lambda e, f: (e, f, 0))],
        out_specs=pl.BlockSpec((r, d), lambda e, f: (e, 0)),
        out_shape=jax.ShapeDtypeStruct((n_experts * r, d), BF16),
        scratch_shapes=[pltpu.VMEM((r, d), F32)],
        compiler_params=_params(("parallel", "arbitrary"), vmem),
        name="expert_ffn",
    )(xs, w_gate, w_up, w_down)


def _combine_kernel(x_ref, y_ref, pcol_ref, aff_ref, g_ref, xo_ref, no_ref):
    e = pl.program_id(2)
    tt = x_ref.shape[0]
    cap = y_ref.shape[0]

    @pl.when(e == 0)
    def _():
        xo_ref[...] = x_ref[...]

    lane = lax.broadcasted_iota(jnp.int32, (tt, LANES), 1)
    mine = lane == e
    slot_of_token = jnp.sum(jnp.where(mine, pcol_ref[...], 0.0), axis=1, keepdims=True)
    gate = jnp.sum(jnp.where(mine, aff_ref[...], 0.0), axis=1, keepdims=True)
    slot = lax.broadcasted_iota(jnp.int32, (tt, cap), 1).astype(F32)
    onehot = (slot == slot_of_token).astype(BF16)
    xo_ref[...] += gate * jnp.dot(onehot, y_ref[...], preferred_element_type=F32)

    @pl.when(e == pl.num_programs(2) - 1)
    def _():
        no_ref[...] = _rms(xo_ref[...], g_ref[...]).astype(no_ref.dtype)


def combine(x_res, y, pcol, aff, gain, *, batch, seq, n_experts, capacity, norm_dtype):
    d = x_res.shape[1]
    tt = _tile(seq, 512)
    nt = seq // tt
    vmem = 2 * (tt * d * 4 + capacity * d * 2 + 2 * tt * LANES * 4 + tt * d * 4 + tt * d * 4) + 3 * tt * d * 4
    return pl.pallas_call(
        _combine_kernel,
        grid=(batch, nt, n_experts),
        in_specs=[pl.BlockSpec((tt, d), lambda b, t, e: (b * nt + t, 0)),
                  pl.BlockSpec((capacity, d), lambda b, t, e: (e * batch + b, 0)),
                  pl.BlockSpec((tt, LANES), lambda b, t, e: (b * nt + t, 0)),
                  pl.BlockSpec((tt, LANES), lambda b, t, e: (b * nt + t, 0)),
                  pl.BlockSpec((1, d), lambda b, t, e: (0, 0))],
        out_specs=[pl.BlockSpec((tt, d), lambda b, t, e: (b * nt + t, 0)),
                   pl.BlockSpec((tt, d), lambda b, t, e: (b * nt + t, 0))],
        out_shape=[jax.ShapeDtypeStruct((batch * seq, d), F32),
                   jax.ShapeDtypeStruct((batch * seq, d), norm_dtype)],
        compiler_params=_params(("parallel", "parallel", "arbitrary"), vmem),
        name="combine",
    )(x_res, y, pcol, aff, gain.reshape(1, d))


def _pool_kernel(h_ref, o_ref, pad_ref, *, seq, windows, margin):
    g = pl.program_id(1)
    width = h_ref.shape[1]
    h = h_ref[...]
    zeros = jnp.zeros((margin, width), F32)
    pad_ref[0:margin, :] = zeros
    pad_ref[margin + seq:2 * margin + seq, :] = zeros
    pad_ref[margin:margin + seq, :] = h
    t = lax.broadcasted_iota(jnp.int32, (seq, 1), 0)
    for gi, w in enumerate(windows):
        @pl.when(g == gi)
        def _(w=w):
            total = pad_ref[margin - w // 2:margin - w // 2 + seq, :]
            for off in range(1 - w // 2, w - w // 2):
                total = total + pad_ref[margin + off:margin + off + seq, :]
            count = (jnp.minimum(t + (w - w // 2), seq) - jnp.maximum(t - w // 2, 0)).astype(F32)
            o_ref[...] = (total / count - h).astype(o_ref.dtype)


def pool_windows(hn, *, batch, seq, n_groups):
    d = hn.shape[1]
    pg = d // n_groups
    margin = 8 * ((max(POOL_WINDOWS) // 2 + 7) // 8)
    vmem = 2 * (seq * pg * 4 + seq * pg * 2) + (seq + 2 * margin) * pg * 4 + 4 * seq * pg * 4
    return pl.pallas_call(
        functools.partial(_pool_kernel, seq=seq, windows=POOL_WINDOWS, margin=margin),
        grid=(batch, n_groups),
        in_specs=[pl.BlockSpec((seq, pg), lambda b, g: (b, g))],
        out_specs=pl.BlockSpec((seq, pg), lambda b, g: (b, g)),
        out_shape=jax.ShapeDtypeStruct((batch * seq, d), BF16),
        scratch_shapes=[pltpu.VMEM((seq + 2 * margin, pg), F32)],
        compiler_params=_params(("parallel", "parallel"), vmem),
        name="pool_windows",
    )(hn)


def _pool_out_kernel(p_ref, w_ref, s_ref, x_ref, g_ref, xo_ref, ho_ref):
    ng, pg, _ = w_ref.shape
    for g in range(ng):
        sl = slice(g * pg, (g + 1) * pg)
        mixed = jnp.dot(p_ref[:, sl], w_ref[g].astype(BF16), preferred_element_type=F32)
        xo_ref[:, sl] = mixed * s_ref[:, sl] + x_ref[:, sl]
    ho_ref[...] = _rms(xo_ref[...], g_ref[...]).astype(ho_ref.dtype)


def pool_out(pooled, w_pool, pool_scale, x_res, gain):
    m, d = x_res.shape
    ng, pg, _ = w_pool.shape
    tm = _tile(m, 512)
    vmem = 2 * (tm * d * 2 + ng * pg * pg * 4 + tm * d * 4 + tm * d * 4 + tm * d * 2) + 3 * tm * d * 4
    return pl.pallas_call(
        _pool_out_kernel,
        grid=(m // tm,),
        in_specs=[pl.BlockSpec((tm, d), lambda i: (i, 0)),
                  pl.BlockSpec((ng, pg, pg), lambda i: (0, 0, 0)),
                  pl.BlockSpec((1, d), lambda i: (0, 0)),
                  pl.BlockSpec((tm, d), lambda i: (i, 0)),
                  pl.BlockSpec((1, d), lambda i: (0, 0))],
        out_specs=[pl.BlockSpec((tm, d), lambda i: (i, 0)),
                   pl.BlockSpec((tm, d), lambda i: (i, 0))],
        out_shape=[jax.ShapeDtypeStruct((m, d), F32), jax.ShapeDtypeStruct((m, d), BF16)],
        compiler_params=_params(("parallel",), vmem),
        name="pool_out",
    )(pooled, w_pool, pool_scale.reshape(1, d), x_res, gain.reshape(1, d))


def _rope_tables(seq, width):
    half = HEAD_DIM // 2
    rows = seq // GRID_W
    row_idx = jnp.repeat(jnp.arange(rows), GRID_W).astype(F32)
    col_idx = jnp.tile(jnp.arange(GRID_W), rows).astype(F32)
    inv_freq = 1.0 / (ROPE_THETA ** (jnp.arange(0, half, 2, dtype=F32) / half))
    ang = jnp.concatenate([row_idx[:, None] * inv_freq[None, :],
                           col_idx[:, None] * inv_freq[None, :]], axis=-1)
    cos, sin = jnp.cos(ang), jnp.sin(ang)
    zero = jnp.zeros_like(sin)
    cos_full = jnp.stack([cos, cos], axis=-1).reshape(seq, HEAD_DIM)
    sin_even = jnp.stack([-sin, zero], axis=-1).reshape(seq, HEAD_DIM)
    sin_odd = jnp.stack([zero, sin], axis=-1).reshape(seq, HEAD_DIM)
    reps = width // HEAD_DIM
    return tuple(jnp.tile(t, (1, reps)) for t in (cos_full, sin_even, sin_odd))


def _split_router(w_router):
    d, n_experts = w_router.shape
    w = jnp.pad(w_router, ((0, 0), (0, LANES - n_experts)))
    hi = w.astype(BF16)
    lo = (w - hi.astype(F32)).astype(BF16)
    return hi, lo, n_experts


def _moe(x_res, h, aff, w_gate, w_up, w_down, next_gain, *, batch, seq, norm_dtype):
    n_experts = w_gate.shape[0]
    capacity = EC_CAPACITY_FACTOR * seq // n_experts
    prow, pcol = route(aff, batch=batch, seq=seq, n_experts=n_experts, capacity=capacity)
    xs = gather_tokens(h, prow, batch=batch, seq=seq, n_experts=n_experts, capacity=capacity)
    y = expert_ffn(xs, w_gate, w_up, w_down, rows_per_expert=batch * capacity)
    return combine(x_res, y, pcol, aff, next_gain, batch=batch, seq=seq,
                   n_experts=n_experts, capacity=capacity, norm_dtype=norm_dtype)


def _cross_block(x, h, mem2d, layer, cross_w_q, cross_w_k, cross_w_v, cross_w_o, mem_norm,
                 ffn_norm, router_w, *, batch, seq, mem_len):
    mem_n = rms_norm_rows(mem2d, mem_norm[layer], BF16)
    k = matmul_bf16(mem_n, cross_w_k[layer])
    v = matmul_bf16(mem_n, cross_w_v[layer])
    q = matmul_bf16(h, cross_w_q[layer])
    o = cross_attention(q, k, v, batch=batch, seq=seq, mem_len=mem_len)
    return matmul_residual_norm([o], cross_w_o[layer], x, ffn_norm[layer],
                                router=_split_router(router_w[layer]))


def kernel(x, mem, mix_norm, attn_w_in, q_gain, k_gain, fourier_w, attn_w_out, pool_w, pool_scale,
           cross_norm, mem_norm, cross_w_q, cross_w_k, cross_w_v, cross_w_o, ffn_norm, router_w,
           expert_w_gate, expert_w_up, expert_w_down, final_norm):
    batch, seq, d = x.shape
    mem_len = mem.shape[1]
    depth = mix_norm.shape[0]
    attn_width = N_Q_HEADS * HEAD_DIM
    kv_width = N_KV_HEADS * HEAD_DIM
    fourier_width = N_FOURIER_GROUPS * FOURIER_GROUP
    rope_width = attn_width + kv_width
    tn_in = math.gcd(math.gcd(attn_width, kv_width), math.gcd(fourier_width, 512))
    cos_t, sin_a, sin_b = _rope_tables(seq, tn_in)

    xf = x.reshape(batch * seq, d)
    mem2d = mem.reshape(batch * mem_len, d)
    cross = functools.partial(_cross_block, mem2d=mem2d, cross_w_q=cross_w_q, cross_w_k=cross_w_k,
                              cross_w_v=cross_w_v, cross_w_o=cross_w_o, mem_norm=mem_norm,
                              ffn_norm=ffn_norm, router_w=router_w,
                              batch=batch, seq=seq, mem_len=mem_len)
    normed = None
    for layer in range(depth):
        i = layer // 2
        if layer % 2 == 0:
            if normed is None:
                h = rms_norm_rows(xf, mix_norm[layer], BF16)
            else:
                h = normed.astype(BF16)
            gain_cols = jnp.concatenate([jnp.tile(q_gain[i], N_Q_HEADS), jnp.tile(k_gain[i], N_KV_HEADS),
                                         jnp.ones((kv_width + fourier_width,), F32)]).reshape(1, -1)
            proj = proj_in(h, attn_w_in[i], gain_cols, cos_t, sin_a, sin_b,
                           seq=seq, rope_width=rope_width, tn=tn_in)
            o_attn = gqa_attention(proj, batch=batch, seq=seq, attn_width=attn_width, kv_width=kv_width)
            o_four = fourier_mixer(proj, fourier_w[i], batch=batch, seq=seq,
                                   col0=attn_width + 2 * kv_width)
            xf, h = matmul_residual_norm([o_attn, o_four], attn_w_out[i], xf, cross_norm[layer],
                                         tk=_tile(fourier_width, 512))
        else:
            if normed is None:
                normed = rms_norm_rows(xf, mix_norm[layer], F32)
            pooled = pool_windows(normed, batch=batch, seq=seq, n_groups=len(POOL_WINDOWS))
            xf, h = pool_out(pooled, pool_w[i], pool_scale[i], xf, cross_norm[layer])
        xf, h, aff = cross(xf, h, layer=layer)
        next_gain = mix_norm[layer + 1] if layer + 1 < depth else final_norm
        xf, normed = _moe(xf, h, aff, expert_w_gate[layer], expert_w_up[layer], expert_w_down[layer],
                          next_gain, batch=batch, seq=seq, norm_dtype=F32)
    return normed.reshape(batch, seq, d)
```

```python
import functools
import math

import jax
import jax.numpy as jnp
from jax import lax
from jax.experimental import pallas as pl
from jax.experimental.pallas import tpu as pltpu

F32 = jnp.float32
BF16 = jnp.bfloat16

GRID_W = 64
HEAD_DIM = 128
N_Q_HEADS = 12
N_KV_HEADS = 4
N_FOURIER_GROUPS = 4
FOURIER_GROUP = 128
ROPE_THETA = 10000.0
POOL_WINDOWS = (2, 4, 8, 16)
N_MEM_HEADS = 4
EC_CAPACITY_FACTOR = 2
NORM_EPS = 1e-6

LANES = 128
V7X_VMEM_BYTES = 64 * 1024 * 1024
VMEM_HEADROOM_BYTES = 6 * 1024 * 1024
COL_CHUNK = 512


def _params(semantics, vmem_bytes):
    limit = min(int(vmem_bytes) + VMEM_HEADROOM_BYTES, V7X_VMEM_BYTES - VMEM_HEADROOM_BYTES)
    return pltpu.CompilerParams(dimension_semantics=semantics, vmem_limit_bytes=limit)


def _tile(n, pref):
    t = min(n, pref)
    while n % t:
        t //= 2
    return t


def _rms(x, gain):
    ms = jnp.mean(x * x, axis=-1, keepdims=True)
    return x * lax.rsqrt(ms + NORM_EPS) * gain


def _norm_kernel(x_ref, g_ref, o_ref):
    o_ref[...] = _rms(x_ref[...], g_ref[...]).astype(o_ref.dtype)


def rms_norm_rows(x, gain, out_dtype):
    m, d = x.shape
    tm = _tile(m, 512)
    return pl.pallas_call(
        _norm_kernel,
        grid=(m // tm,),
        in_specs=[pl.BlockSpec((tm, d), lambda i: (i, 0)),
                  pl.BlockSpec((1, d), lambda i: (0, 0))],
        out_specs=pl.BlockSpec((tm, d), lambda i: (i, 0)),
        out_shape=jax.ShapeDtypeStruct((m, d), out_dtype),
        compiler_params=_params(("parallel",), 2 * tm * d * 8),
        name="rms_norm_rows",
    )(x, gain.reshape(1, d))


def _mm_kernel(a_ref, w_ref, o_ref):
    n = o_ref.shape[1]
    tn = _tile(n, COL_CHUNK)
    a = a_ref[...]
    for c in range(n // tn):
        sl = slice(c * tn, (c + 1) * tn)
        o_ref[:, sl] = jnp.dot(a, w_ref[:, sl], preferred_element_type=F32).astype(o_ref.dtype)


def matmul_resident(a, w, out_dtype=BF16, tm_pref=512):
    m, k = a.shape
    n = w.shape[1]
    tm = _tile(m, tm_pref)
    vmem = 2 * (tm * k * 2 + k * n * 2 + tm * n * 2) + 2 * tm * COL_CHUNK * 4
    return pl.pallas_call(
        _mm_kernel,
        grid=(m // tm,),
        in_specs=[pl.BlockSpec((tm, k), lambda i: (i, 0)),
                  pl.BlockSpec((k, n), lambda i: (0, 0))],
        out_specs=pl.BlockSpec((tm, n), lambda i: (i, 0)),
        out_shape=jax.ShapeDtypeStruct((m, n), out_dtype),
        compiler_params=_params(("parallel",), vmem),
        name="matmul_resident",
    )(a, w)


def _proj_in_kernel(a_ref, w_ref, g_ref, cos_ref, sa_ref, sb_ref, o_ref, *, n_rope_heads):
    n = o_ref.shape[1]
    tn = _tile(n, COL_CHUNK)
    a = a_ref[...]
    cos, sin_a, sin_b = cos_ref[...], sa_ref[...], sb_ref[...]
    for c in range(n // tn):
        acc = jnp.dot(a, w_ref[:, c * tn:(c + 1) * tn], preferred_element_type=F32)
        for h in range(tn // HEAD_DIM):
            head = c * (tn // HEAD_DIM) + h
            col = slice(head * HEAD_DIM, (head + 1) * HEAD_DIM)
            xh = acc[:, h * HEAD_DIM:(h + 1) * HEAD_DIM]
            if head < n_rope_heads:
                ms = jnp.mean(xh * xh, axis=-1, keepdims=True)
                y = xh * lax.rsqrt(ms + NORM_EPS) * g_ref[:, col]
                xh = (y * cos + pltpu.roll(y, HEAD_DIM - 1, axis=1) * sin_a
                      + pltpu.roll(y, 1, axis=1) * sin_b)
            o_ref[:, col] = xh.astype(o_ref.dtype)


def proj_in(h, w_in, gain_cols, cos_t, sin_a, sin_b, *, seq, rope_width):
    m, k = h.shape
    n = w_in.shape[1]
    tm = _tile(seq, 512)
    n_pos_blocks = seq // tm
    vmem = 2 * (tm * k * 2 + k * n * 2 + tm * n * 2 + 3 * tm * HEAD_DIM * 4) + 6 * tm * COL_CHUNK * 4
    table = pl.BlockSpec((tm, HEAD_DIM), lambda i: (i % n_pos_blocks, 0))
    return pl.pallas_call(
        functools.partial(_proj_in_kernel, n_rope_heads=rope_width // HEAD_DIM),
        grid=(m // tm,),
        in_specs=[pl.BlockSpec((tm, k), lambda i: (i, 0)),
                  pl.BlockSpec((k, n), lambda i: (0, 0)),
                  pl.BlockSpec((1, n), lambda i: (0, 0)),
                  table, table, table],
        out_specs=pl.BlockSpec((tm, n), lambda i: (i, 0)),
        out_shape=jax.ShapeDtypeStruct((m, n), BF16),
        compiler_params=_params(("parallel",), vmem),
        name="proj_in",
    )(h, w_in, gain_cols, cos_t, sin_a, sin_b)


def _attn_kernel(q_ref, k_ref, v_ref, o_ref, *, group, scale):
    k = k_ref[...]
    v = v_ref[...]
    for g in range(group):
        q = q_ref[:, g * HEAD_DIM:(g + 1) * HEAD_DIM]
        s = lax.dot_general(q, k, (((1,), (1,)), ((), ())), preferred_element_type=F32)
        e = jnp.exp((s - jnp.max(s, axis=-1, keepdims=True)) * scale)
        l = jnp.sum(e, axis=-1, keepdims=True)
        o = jnp.dot(e.astype(BF16), v, preferred_element_type=F32) / l
        o_ref[:, g * HEAD_DIM:(g + 1) * HEAD_DIM] = o.astype(o_ref.dtype)


def gqa_attention(proj, *, batch, seq, attn_width, kv_width):
    group = N_Q_HEADS // N_KV_HEADS
    gw = group * HEAD_DIM
    tq = _tile(seq, 256)
    nq = seq // tq
    k_col0 = attn_width // HEAD_DIM
    v_col0 = (attn_width + kv_width) // HEAD_DIM
    vmem = 2 * (tq * gw * 2 * 2 + 2 * seq * HEAD_DIM * 2) + 4 * tq * seq * 4
    return pl.pallas_call(
        functools.partial(_attn_kernel, group=group, scale=HEAD_DIM ** -0.5),
        grid=(batch, N_KV_HEADS, nq),
        in_specs=[pl.BlockSpec((tq, gw), lambda b, h, i: (b * nq + i, h)),
                  pl.BlockSpec((seq, HEAD_DIM), lambda b, h, i: (b, k_col0 + h)),
                  pl.BlockSpec((seq, HEAD_DIM), lambda b, h, i: (b, v_col0 + h))],
        out_specs=pl.BlockSpec((tq, gw), lambda b, h, i: (b * nq + i, h)),
        out_shape=jax.ShapeDtypeStruct((batch * seq, attn_width), BF16),
        compiler_params=_params(("parallel", "parallel", "parallel"), vmem),
        name="gqa_attention",
    )(proj, proj, proj)


def _fourier_kernel(f_ref, cc_ref, sc_ref, cs_ref, ss_ref, wf_ref, o_ref, ab_ref, *, seq, norm):
    r = pl.program_id(1)
    ng = wf_ref.shape[0]
    cg = wf_ref.shape[1]

    @pl.when(r == 0)
    def _():
        for g in range(ng):
            fg = f_ref[:, g * cg:(g + 1) * cg]
            ab_ref[0:seq, g * cg:(g + 1) * cg] = jnp.dot(
                fg, cc_ref[...], preferred_element_type=F32).astype(BF16)
            ab_ref[seq:2 * seq, g * cg:(g + 1) * cg] = jnp.dot(
                fg, sc_ref[...], preferred_element_type=F32).astype(BF16)

    z = (jnp.dot(cs_ref[...], ab_ref[0:seq, :], preferred_element_type=F32)
         - jnp.dot(ss_ref[...], ab_ref[seq:2 * seq, :], preferred_element_type=F32)) * norm
    for g in range(ng):
        zg = z[:, g * cg:(g + 1) * cg].astype(BF16)
        o_ref[:, g * cg:(g + 1) * cg] = jnp.dot(
            zg, wf_ref[g].astype(BF16), preferred_element_type=F32).astype(o_ref.dtype)


def _dft_cos_sin(n):
    lo = _tile(n, 64)
    hi = n // lo
    j = jnp.arange(n, dtype=jnp.int32)

    def table(k):
        ang = ((k[:, None] * j[None, :]) % n).astype(F32) * (2.0 * math.pi / n)
        return jnp.cos(ang), jnp.sin(ang)

    ca, sa = table(jnp.arange(hi, dtype=jnp.int32) * lo)
    cb, sb = table(jnp.arange(lo, dtype=jnp.int32))
    cos = ca[:, None, :] * cb[None, :, :] - sa[:, None, :] * sb[None, :, :]
    sin = sa[:, None, :] * cb[None, :, :] + ca[:, None, :] * sb[None, :, :]
    return cos.reshape(n, n).astype(BF16), sin.reshape(n, n).astype(BF16)


def fourier_mixer(proj, w_fourier, *, batch, seq, col0):
    ng, cg, _ = w_fourier.shape
    fw = ng * cg
    tr = _tile(seq, 512)
    nr = seq // tr
    cc, sc = _dft_cos_sin(cg)
    cs, ss = _dft_cos_sin(seq)
    norm = 1.0 / math.sqrt(seq * cg)
    vmem = 2 * (seq * fw * 2 + 2 * cg * cg * 2 + 2 * tr * seq * 2 + ng * cg * cg * 4 + tr * fw * 2) \
        + 2 * seq * fw * 2 + 4 * tr * fw * 4
    return pl.pallas_call(
        functools.partial(_fourier_kernel, seq=seq, norm=norm),
        grid=(batch, nr),
        in_specs=[pl.BlockSpec((seq, fw), lambda b, r: (b, col0 // fw)),
                  pl.BlockSpec((cg, cg), lambda b, r: (0, 0)),
                  pl.BlockSpec((cg, cg), lambda b, r: (0, 0)),
                  pl.BlockSpec((tr, seq), lambda b, r: (r, 0)),
                  pl.BlockSpec((tr, seq), lambda b, r: (r, 0)),
                  pl.BlockSpec((ng, cg, cg), lambda b, r: (0, 0, 0))],
        out_specs=pl.BlockSpec((tr, fw), lambda b, r: (b * nr + r, 0)),
        out_shape=jax.ShapeDtypeStruct((batch * seq, fw), BF16),
        scratch_shapes=[pltpu.VMEM((2 * seq, fw), BF16)],
        compiler_params=_params(("parallel", "arbitrary"), vmem),
        name="fourier_mixer",
    )(proj, cc, sc, cs, ss, w_fourier)


def _router_affinity(hn, wr_hi_ref, wr_lo_ref, n_experts):
    hi = hn.astype(BF16)
    lo = (hn - hi.astype(F32)).astype(BF16)
    logits = (jnp.dot(hi, wr_hi_ref[...], preferred_element_type=F32)
              + jnp.dot(lo, wr_hi_ref[...], preferred_element_type=F32)
              + jnp.dot(hi, wr_lo_ref[...], preferred_element_type=F32))
    lane = lax.broadcasted_iota(jnp.int32, logits.shape, 1)
    logits = jnp.where(lane < n_experts, logits, -jnp.inf)
    e = jnp.exp(logits - jnp.max(logits, axis=-1, keepdims=True))
    return e / jnp.sum(e, axis=-1, keepdims=True)


def _mm_res_norm_kernel(*refs, n_a, with_router, n_experts):
    a_refs = refs[:n_a]
    w_ref, x_ref, g_ref = refs[n_a:n_a + 3]
    pos = n_a + 3
    if with_router:
        wr_hi_ref, wr_lo_ref = refs[pos:pos + 2]
        pos += 2
    xo_ref, ho_ref = refs[pos:pos + 2]
    if with_router:
        aff_ref = refs[pos + 2]
    d = xo_ref.shape[1]
    tn = _tile(d, COL_CHUNK)
    for c in range(d // tn):
        sl = slice(c * tn, (c + 1) * tn)
        acc = x_ref[:, sl]
        row0 = 0
        for a_ref in a_refs:
            ka = a_ref.shape[1]
            acc = acc + jnp.dot(a_ref[...], w_ref[row0:row0 + ka, sl], preferred_element_type=F32)
            row0 += ka
        xo_ref[:, sl] = acc
    hn = _rms(xo_ref[...], g_ref[...])
    ho_ref[...] = hn.astype(ho_ref.dtype)
    if with_router:
        aff_ref[...] = _router_affinity(hn, wr_hi_ref, wr_lo_ref, n_experts)


def matmul_residual_norm(a_list, w, x_res, gain, *, router=None, tm_pref=512):
    m, d = x_res.shape
    k = w.shape[0]
    tm = _tile(m, tm_pref)
    in_specs = [pl.BlockSpec((tm, a.shape[1]), lambda i: (i, 0)) for a in a_list]
    in_specs += [pl.BlockSpec((k, d), lambda i: (0, 0)),
                 pl.BlockSpec((tm, d), lambda i: (i, 0)),
                 pl.BlockSpec((1, d), lambda i: (0, 0))]
    args = list(a_list) + [w, x_res, gain.reshape(1, d)]
    out_specs = [pl.BlockSpec((tm, d), lambda i: (i, 0)),
                 pl.BlockSpec((tm, d), lambda i: (i, 0))]
    out_shape = [jax.ShapeDtypeStruct((m, d), F32), jax.ShapeDtypeStruct((m, d), BF16)]
    n_experts = 0
    if router is not None:
        wr_hi, wr_lo, n_experts = router
        in_specs += [pl.BlockSpec((d, LANES), lambda i: (0, 0)),
                     pl.BlockSpec((d, LANES), lambda i: (0, 0))]
        args += [wr_hi, wr_lo]
        out_specs.append(pl.BlockSpec((tm, LANES), lambda i: (i, 0)))
        out_shape.append(jax.ShapeDtypeStruct((m, LANES), F32))
    vmem = 2 * (tm * k * 2 + k * d * 2 + tm * d * 4 + tm * d * 4 + tm * d * 2) \
        + 3 * tm * d * 4 + 4 * d * LANES * 2
    return pl.pallas_call(
        functools.partial(_mm_res_norm_kernel, n_a=len(a_list),
                          with_router=router is not None, n_experts=n_experts),
        grid=(m // tm,),
        in_specs=in_specs,
        out_specs=out_specs,
        out_shape=out_shape,
        compiler_params=_params(("parallel",), vmem),
        name="matmul_residual_norm",
    )(*args)


def _cross_attn_kernel(q_ref, k_ref, v_ref, o_ref, *, n_heads, scale):
    dh = q_ref.shape[1] // n_heads
    for h in range(n_heads):
        sl = slice(h * dh, (h + 1) * dh)
        s = lax.dot_general(q_ref[:, sl], k_ref[:, sl], (((1,), (1,)), ((), ())),
                            preferred_element_type=F32)
        e = jnp.exp((s - jnp.max(s, axis=-1, keepdims=True)) * scale)
        l = jnp.sum(e, axis=-1, keepdims=True)
        o = jnp.dot(e.astype(BF16), v_ref[:, sl], preferred_element_type=F32) / l
        o_ref[:, sl] = o.astype(o_ref.dtype)


def cross_attention(q, k, v, *, batch, seq, mem_len):
    d = q.shape[1]
    tq = _tile(seq, 512)
    nq = seq // tq
    vmem = 2 * (2 * tq * d * 2 + 2 * mem_len * d * 2) + 6 * tq * mem_len * 4 + 2 * tq * d * 4
    return pl.pallas_call(
        functools.partial(_cross_attn_kernel, n_heads=N_MEM_HEADS,
                          scale=(d // N_MEM_HEADS) ** -0.5),
        grid=(batch, nq),
        in_specs=[pl.BlockSpec((tq, d), lambda b, i: (b * nq + i, 0)),
                  pl.BlockSpec((mem_len, d), lambda b, i: (b, 0)),
                  pl.BlockSpec((mem_len, d), lambda b, i: (b, 0))],
        out_specs=pl.BlockSpec((tq, d), lambda b, i: (b * nq + i, 0)),
        out_shape=jax.ShapeDtypeStruct((batch * seq, d), BF16),
        compiler_params=_params(("parallel", "parallel"), vmem),
        name="cross_attention",
    )(q, k, v)


def _prefix_count(x, lane):
    n = x.shape[1]
    shift = 1
    while shift < n:
        x = x + jnp.where(lane >= shift, pltpu.roll(x, shift, axis=1), 0)
        shift *= 2
    return x


def _route_kernel(aff_ref, prow_ref, pcol_ref, *, n_experts, capacity):
    e_pad = prow_ref.shape[1]
    a_t = aff_ref[...].T[:e_pad]
    seq = a_t.shape[1]

    def body(i, thr_bits):
        cand = thr_bits | jnp.left_shift(jnp.int32(1), 30 - i)
        cnt = jnp.sum((a_t >= pltpu.bitcast(cand, F32)).astype(F32), axis=1, keepdims=True)
        return jnp.where(cnt >= capacity, cand, thr_bits)

    thr = pltpu.bitcast(lax.fori_loop(0, 31, body, jnp.zeros((e_pad, 1), jnp.int32)), F32)
    lane = lax.broadcasted_iota(jnp.int32, a_t.shape, 1)
    gt = a_t > thr
    eq = a_t == thr
    need = capacity - jnp.sum(gt.astype(F32), axis=1, keepdims=True).astype(jnp.int32)
    sel = gt | (eq & (_prefix_count(eq.astype(jnp.int32), lane) <= need))
    pos = jnp.where(sel, _prefix_count(sel.astype(jnp.int32), lane) - 1, -1)
    row = lax.broadcasted_iota(jnp.int32, a_t.shape, 0)
    pos = jnp.where(row < n_experts, pos, -1)
    prow_ref[0] = pos
    unused = jnp.full((LANES - e_pad, seq), -1.0, F32)
    pcol_ref[...] = jnp.concatenate([pos.astype(F32), unused], axis=0).T


def route(aff, *, batch, seq, n_experts, capacity):
    e_pad = max(8, n_experts)
    return pl.pallas_call(
        functools.partial(_route_kernel, n_experts=n_experts, capacity=capacity),
        grid=(batch,),
        in_specs=[pl.BlockSpec((seq, LANES), lambda b: (b, 0))],
        out_specs=[pl.BlockSpec((1, e_pad, seq), lambda b: (b, 0, 0)),
                   pl.BlockSpec((seq, LANES), lambda b: (b, 0))],
        out_shape=[jax.ShapeDtypeStruct((batch, e_pad, seq), jnp.int32),
                   jax.ShapeDtypeStruct((batch * seq, LANES), F32)],
        compiler_params=_params(("parallel",), 24 * seq * LANES * 4),
        name="route",
    )(aff)


def _gather_kernel(h_ref, prow_ref, o_ref):
    e = pl.program_id(1)
    cap = o_ref.shape[0]
    seq = h_ref.shape[0]
    prow = prow_ref[0, pl.ds(e, 1), :]
    slot = lax.broadcasted_iota(jnp.int32, (cap, seq), 0)
    onehot = (slot == prow).astype(BF16)
    o_ref[...] = jnp.dot(onehot, h_ref[...], preferred_element_type=F32).astype(o_ref.dtype)


def gather_tokens(h, prow, *, batch, seq, n_experts, capacity):
    d = h.shape[1]
    e_pad = prow.shape[1]
    vmem = 2 * (seq * d * 2 + e_pad * seq * 4 + capacity * d * 2) + capacity * seq * 8 + capacity * d * 4
    return pl.pallas_call(
        _gather_kernel,
        grid=(batch, n_experts),
        in_specs=[pl.BlockSpec((seq, d), lambda b, e: (b, 0)),
                  pl.BlockSpec((1, e_pad, seq), lambda b, e: (b, 0, 0))],
        out_specs=pl.BlockSpec((capacity, d), lambda b, e: (e * batch + b, 0)),
        out_shape=jax.ShapeDtypeStruct((n_experts * batch * capacity, d), BF16),
        compiler_params=_params(("parallel", "arbitrary"), vmem),
        name="gather_tokens",
    )(h, prow)


def _ffn_kernel(x_ref, wg_ref, wu_ref, wd_ref, o_ref, acc_ref):
    f = pl.program_id(1)
    x = x_ref[...]
    a = jnp.dot(x, wg_ref[...].astype(BF16), preferred_element_type=F32)
    u = jnp.dot(x, wu_ref[...].astype(BF16), preferred_element_type=F32)
    hidden = (a / (1.0 + jnp.exp(-a)) * u).astype(BF16)
    y = jnp.dot(hidden, wd_ref[...].astype(BF16), preferred_element_type=F32)

    @pl.when(f == 0)
    def _():
        acc_ref[...] = y

    @pl.when(f > 0)
    def _():
        acc_ref[...] += y

    @pl.when(f == pl.num_programs(1) - 1)
    def _():
        o_ref[...] = acc_ref[...].astype(o_ref.dtype)


def expert_ffn(xs, w_gate, w_up, w_down, layer, *, rows_per_expert):
    _, n_experts, d, ff = w_gate.shape
    tf = _tile(ff, 256)
    r = rows_per_expert
    vmem = 2 * (r * d * 2 + 3 * d * tf * 4 + r * d * 2) + r * d * 4 + 3 * d * tf * 2 + 4 * r * tf * 4 + r * d * 4
    return pl.pallas_call(
        _ffn_kernel,
        grid=(n_experts, ff // tf),
        in_specs=[pl.BlockSpec((r, d), lambda e, f: (e, 0)),
                  pl.BlockSpec((None, None, d, tf), lambda e, f: (layer, e, 0, f)),
                  pl.BlockSpec((None, None, d, tf), lambda e, f: (layer, e, 0, f)),
                  pl.BlockSpec((None, None, tf, d), lambda e, f: (layer, e, f, 0))],
        out_specs=pl.BlockSpec((r, d), lambda e, f: (e, 0)),
        out_shape=jax.ShapeDtypeStruct((n_experts * r, d), BF16),
        scratch_shapes=[pltpu.VMEM((r, d), F32)],
        compiler_params=_params(("parallel", "arbitrary"), vmem),
        name="expert_ffn",
    )(xs, w_gate, w_up, w_down)


def _combine_kernel(x_ref, y_ref, pcol_ref, aff_ref, g_ref, xo_ref, no_ref):
    e = pl.program_id(2)
    tt = x_ref.shape[0]
    cap = y_ref.shape[0]

    @pl.when(e == 0)
    def _():
        xo_ref[...] = x_ref[...]

    lane = lax.broadcasted_iota(jnp.int32, (tt, LANES), 1)
    mine = lane == e
    slot_of_token = jnp.sum(jnp.where(mine, pcol_ref[...], 0.0), axis=1, keepdims=True)
    gate = jnp.sum(jnp.where(mine, aff_ref[...], 0.0), axis=1, keepdims=True)
    slot = lax.broadcasted_iota(jnp.int32, (tt, cap), 1).astype(F32)
    onehot = (slot == slot_of_token).astype(BF16)
    xo_ref[...] += gate * jnp.dot(onehot, y_ref[...], preferred_element_type=F32)

    @pl.when(e == pl.num_programs(2) - 1)
    def _():
        no_ref[...] = _rms(xo_ref[...], g_ref[...]).astype(no_ref.dtype)


def combine(x_res, y, pcol, aff, gain, *, batch, seq, n_experts, capacity, norm_dtype):
    d = x_res.shape[1]
    tt = _tile(seq, 512)
    nt = seq // tt
    vmem = 2 * (tt * d * 4 + capacity * d * 2 + 2 * tt * LANES * 4 + tt * d * 4 + tt * d * 4) + 3 * tt * d * 4
    return pl.pallas_call(
        _combine_kernel,
        grid=(batch, nt, n_experts),
        in_specs=[pl.BlockSpec((tt, d), lambda b, t, e: (b * nt + t, 0)),
                  pl.BlockSpec((capacity, d), lambda b, t, e: (e * batch + b, 0)),
                  pl.BlockSpec((tt, LANES), lambda b, t, e: (b * nt + t, 0)),
                  pl.BlockSpec((tt, LANES), lambda b, t, e: (b * nt + t, 0)),
                  pl.BlockSpec((1, d), lambda b, t, e: (0, 0))],
        out_specs=[pl.BlockSpec((tt, d), lambda b, t, e: (b * nt + t, 0)),
                   pl.BlockSpec((tt, d), lambda b, t, e: (b * nt + t, 0))],
        out_shape=[jax.ShapeDtypeStruct((batch * seq, d), F32),
                   jax.ShapeDtypeStruct((batch * seq, d), norm_dtype)],
        compiler_params=_params(("parallel", "parallel", "arbitrary"), vmem),
        name="combine",
    )(x_res, y, pcol, aff, gain.reshape(1, d))


def _pool_kernel(h_ref, o_ref, pad_ref, *, seq, windows, margin):
    g = pl.program_id(1)
    width = h_ref.shape[1]
    h = h_ref[...]
    zeros = jnp.zeros((margin, width), F32)
    pad_ref[0:margin, :] = zeros
    pad_ref[margin + seq:2 * margin + seq, :] = zeros
    pad_ref[margin:margin + seq, :] = h
    t = lax.broadcasted_iota(jnp.int32, (seq, 1), 0)
    for gi, w in enumerate(windows):
        @pl.when(g == gi)
        def _(w=w):
            total = pad_ref[margin - w // 2:margin - w // 2 + seq, :]
            for off in range(1 - w // 2, w - w // 2):
                total = total + pad_ref[margin + off:margin + off + seq, :]
            count = (jnp.minimum(t + (w - w // 2), seq) - jnp.maximum(t - w // 2, 0)).astype(F32)
            o_ref[...] = (total / count - h).astype(o_ref.dtype)


def pool_windows(hn, *, batch, seq, n_groups):
    d = hn.shape[1]
    pg = d // n_groups
    margin = 8 * ((max(POOL_WINDOWS) // 2 + 7) // 8)
    vmem = 2 * (seq * pg * 4 + seq * pg * 2) + (seq + 2 * margin) * pg * 4 + 4 * seq * pg * 4
    return pl.pallas_call(
        functools.partial(_pool_kernel, seq=seq, windows=POOL_WINDOWS, margin=margin),
        grid=(batch, n_groups),
        in_specs=[pl.BlockSpec((seq, pg), lambda b, g: (b, g))],
        out_specs=pl.BlockSpec((seq, pg), lambda b, g: (b, g)),
        out_shape=jax.ShapeDtypeStruct((batch * seq, d), BF16),
        scratch_shapes=[pltpu.VMEM((seq + 2 * margin, pg), F32)],
        compiler_params=_params(("parallel", "parallel"), vmem),
        name="pool_windows",
    )(hn)


def _pool_out_kernel(p_ref, w_ref, s_ref, x_ref, g_ref, xo_ref, ho_ref):
    ng, pg, _ = w_ref.shape
    for g in range(ng):
        sl = slice(g * pg, (g + 1) * pg)
        mixed = jnp.dot(p_ref[:, sl], w_ref[g], preferred_element_type=F32)
        xo_ref[:, sl] = mixed * s_ref[:, sl] + x_ref[:, sl]
    ho_ref[...] = _rms(xo_ref[...], g_ref[...]).astype(ho_ref.dtype)


def pool_out(pooled, w_pool, pool_scale, x_res, gain):
    m, d = x_res.shape
    ng, pg, _ = w_pool.shape
    tm = _tile(m, 512)
    vmem = 2 * (tm * d * 2 + ng * pg * pg * 2 + tm * d * 4 + tm * d * 4 + tm * d * 2) + 3 * tm * d * 4
    return pl.pallas_call(
        _pool_out_kernel,
        grid=(m // tm,),
        in_specs=[pl.BlockSpec((tm, d), lambda i: (i, 0)),
                  pl.BlockSpec((ng, pg, pg), lambda i: (0, 0, 0)),
                  pl.BlockSpec((1, d), lambda i: (0, 0)),
                  pl.BlockSpec((tm, d), lambda i: (i, 0)),
                  pl.BlockSpec((1, d), lambda i: (0, 0))],
        out_specs=[pl.BlockSpec((tm, d), lambda i: (i, 0)),
                   pl.BlockSpec((tm, d), lambda i: (i, 0))],
        out_shape=[jax.ShapeDtypeStruct((m, d), F32), jax.ShapeDtypeStruct((m, d), BF16)],
        compiler_params=_params(("parallel",), vmem),
        name="pool_out",
    )(pooled, w_pool, pool_scale.reshape(1, d), x_res, gain.reshape(1, d))


def _rope_tables(seq):
    half = HEAD_DIM // 2
    rows = seq // GRID_W
    row_idx = jnp.repeat(jnp.arange(rows), GRID_W).astype(F32)
    col_idx = jnp.tile(jnp.arange(GRID_W), rows).astype(F32)
    inv_freq = 1.0 / (ROPE_THETA ** (jnp.arange(0, half, 2, dtype=F32) / half))
    ang = jnp.concatenate([row_idx[:, None] * inv_freq[None, :],
                           col_idx[:, None] * inv_freq[None, :]], axis=-1)
    cos, sin = jnp.cos(ang), jnp.sin(ang)
    zero = jnp.zeros_like(sin)
    cos_full = jnp.stack([cos, cos], axis=-1).reshape(seq, HEAD_DIM)
    sin_even = jnp.stack([-sin, zero], axis=-1).reshape(seq, HEAD_DIM)
    sin_odd = jnp.stack([zero, sin], axis=-1).reshape(seq, HEAD_DIM)
    return cos_full, sin_even, sin_odd


def _split_router(w_router):
    d, n_experts = w_router.shape
    w = jnp.pad(w_router, ((0, 0), (0, LANES - n_experts)))
    hi = w.astype(BF16)
    lo = (w - hi.astype(F32)).astype(BF16)
    return hi, lo, n_experts


def _moe(x_res, h, aff, w_gate, w_up, w_down, layer, next_gain, *, batch, seq, norm_dtype):
    n_experts = w_gate.shape[1]
    capacity = EC_CAPACITY_FACTOR * seq // n_experts
    prow, pcol = route(aff, batch=batch, seq=seq, n_experts=n_experts, capacity=capacity)
    xs = gather_tokens(h, prow, batch=batch, seq=seq, n_experts=n_experts, capacity=capacity)
    y = expert_ffn(xs, w_gate, w_up, w_down, layer, rows_per_expert=batch * capacity)
    return combine(x_res, y, pcol, aff, next_gain, batch=batch, seq=seq,
                   n_experts=n_experts, capacity=capacity, norm_dtype=norm_dtype)


def _cross_block(x, h, mem2d, layer, cross_w_q, cross_w_k, cross_w_v, cross_w_o, mem_norm,
                 ffn_norm, router_w, *, batch, seq, mem_len):
    mem_n = rms_norm_rows(mem2d, mem_norm[layer], BF16)
    k = matmul_resident(mem_n, cross_w_k[layer].astype(BF16))
    v = matmul_resident(mem_n, cross_w_v[layer].astype(BF16))
    q = matmul_resident(h, cross_w_q[layer].astype(BF16))
    o = cross_attention(q, k, v, batch=batch, seq=seq, mem_len=mem_len)
    return matmul_residual_norm([o], cross_w_o[layer].astype(BF16), x, ffn_norm[layer],
                                router=_split_router(router_w[layer]))


def kernel(x, mem, mix_norm, attn_w_in, q_gain, k_gain, fourier_w, attn_w_out, pool_w, pool_scale,
           cross_norm, mem_norm, cross_w_q, cross_w_k, cross_w_v, cross_w_o, ffn_norm, router_w,
           expert_w_gate, expert_w_up, expert_w_down, final_norm):
    batch, seq, d = x.shape
    mem_len = mem.shape[1]
    depth = mix_norm.shape[0]
    attn_width = N_Q_HEADS * HEAD_DIM
    kv_width = N_KV_HEADS * HEAD_DIM
    fourier_width = N_FOURIER_GROUPS * FOURIER_GROUP
    rope_width = attn_width + kv_width
    cos_t, sin_a, sin_b = _rope_tables(seq)

    xf = x.reshape(batch * seq, d)
    mem2d = mem.reshape(batch * mem_len, d)
    cross = functools.partial(_cross_block, mem2d=mem2d, cross_w_q=cross_w_q, cross_w_k=cross_w_k,
                              cross_w_v=cross_w_v, cross_w_o=cross_w_o, mem_norm=mem_norm,
                              ffn_norm=ffn_norm, router_w=router_w,
                              batch=batch, seq=seq, mem_len=mem_len)
    normed = None
    for layer in range(depth):
        i = layer // 2
        if layer % 2 == 0:
            if normed is None:
                h = rms_norm_rows(xf, mix_norm[layer], BF16)
            else:
                h = normed.astype(BF16)
            gain_cols = jnp.concatenate([jnp.tile(q_gain[i], N_Q_HEADS), jnp.tile(k_gain[i], N_KV_HEADS),
                                         jnp.ones((kv_width + fourier_width,), F32)]).reshape(1, -1)
            proj = proj_in(h, attn_w_in[i].astype(BF16), gain_cols, cos_t, sin_a, sin_b,
                           seq=seq, rope_width=rope_width)
            o_attn = gqa_attention(proj, batch=batch, seq=seq, attn_width=attn_width, kv_width=kv_width)
            o_four = fourier_mixer(proj, fourier_w[i], batch=batch, seq=seq,
                                   col0=attn_width + 2 * kv_width)
            xf, h = matmul_residual_norm([o_attn, o_four], attn_w_out[i].astype(BF16), xf,
                                         cross_norm[layer])
        else:
            if normed is None:
                normed = rms_norm_rows(xf, mix_norm[layer], F32)
            pooled = pool_windows(normed, batch=batch, seq=seq, n_groups=len(POOL_WINDOWS))
            xf, h = pool_out(pooled, pool_w[i].astype(BF16), pool_scale[i], xf, cross_norm[layer])
        xf, h, aff = cross(xf, h, layer=layer)
        next_gain = mix_norm[layer + 1] if layer + 1 < depth else final_norm
        xf, normed = _moe(xf, h, aff, expert_w_gate, expert_w_up, expert_w_down, layer,
                          next_gain, batch=batch, seq=seq, norm_dtype=F32)
    return normed.reshape(batch, seq, d)
```

```python
import functools
import math

import jax
import jax.numpy as jnp
from jax import lax
from jax.experimental import pallas as pl
from jax.experimental.pallas import tpu as pltpu

F32 = jnp.float32
BF16 = jnp.bfloat16

GRID_W = 64
HEAD_DIM = 128
N_Q_HEADS = 12
N_KV_HEADS = 4
N_FOURIER_GROUPS = 4
FOURIER_GROUP = 128
ROPE_THETA = 10000.0
POOL_WINDOWS = (2, 4, 8, 16)
N_MEM_HEADS = 4
EC_CAPACITY_FACTOR = 2
NORM_EPS = 1e-6

LANES = 128
V7X_VMEM_BYTES = 64 * 1024 * 1024
VMEM_HEADROOM_BYTES = 6 * 1024 * 1024
COL_CHUNK = 512


def _params(semantics, vmem_bytes):
    limit = min(int(vmem_bytes) + VMEM_HEADROOM_BYTES, V7X_VMEM_BYTES - VMEM_HEADROOM_BYTES)
    return pltpu.CompilerParams(dimension_semantics=semantics, vmem_limit_bytes=limit)


def _tile(n, pref):
    t = min(n, pref)
    while n % t:
        t //= 2
    return t


def _rms(x, gain):
    ms = jnp.mean(x * x, axis=-1, keepdims=True)
    return x * lax.rsqrt(ms + NORM_EPS) * gain


def _norm_kernel(x_ref, g_ref, o_ref):
    o_ref[...] = _rms(x_ref[...], g_ref[...]).astype(o_ref.dtype)


def rms_norm_rows(x, gain, out_dtype):
    m, d = x.shape
    tm = _tile(m, 512)
    return pl.pallas_call(
        _norm_kernel,
        grid=(m // tm,),
        in_specs=[pl.BlockSpec((tm, d), lambda i: (i, 0)),
                  pl.BlockSpec((1, d), lambda i: (0, 0))],
        out_specs=pl.BlockSpec((tm, d), lambda i: (i, 0)),
        out_shape=jax.ShapeDtypeStruct((m, d), out_dtype),
        compiler_params=_params(("parallel",), 2 * tm * d * 8),
        name="rms_norm_rows",
    )(x, gain.reshape(1, d))


def _mm_kernel(a_ref, w_ref, o_ref):
    n = o_ref.shape[1]
    tn = _tile(n, COL_CHUNK)
    a = a_ref[...]
    for c in range(n // tn):
        sl = slice(c * tn, (c + 1) * tn)
        o_ref[:, sl] = jnp.dot(a, w_ref[:, sl], preferred_element_type=F32).astype(o_ref.dtype)


def matmul_resident(a, w, out_dtype=BF16, tm_pref=512):
    m, k = a.shape
    n = w.shape[1]
    tm = _tile(m, tm_pref)
    vmem = 2 * (tm * k * 2 + k * n * 2 + tm * n * 2) + 2 * tm * COL_CHUNK * 4
    return pl.pallas_call(
        _mm_kernel,
        grid=(m // tm,),
        in_specs=[pl.BlockSpec((tm, k), lambda i: (i, 0)),
                  pl.BlockSpec((k, n), lambda i: (0, 0))],
        out_specs=pl.BlockSpec((tm, n), lambda i: (i, 0)),
        out_shape=jax.ShapeDtypeStruct((m, n), out_dtype),
        compiler_params=_params(("parallel",), vmem),
        name="matmul_resident",
    )(a, w)


def _proj_in_kernel(a_ref, w_ref, g_ref, cos_ref, sa_ref, sb_ref, o_ref, *, n_rope_heads):
    n = o_ref.shape[1]
    tn = _tile(n, COL_CHUNK)
    a = a_ref[...]
    cos, sin_a, sin_b = cos_ref[...], sa_ref[...], sb_ref[...]
    for c in range(n // tn):
        acc = jnp.dot(a, w_ref[:, c * tn:(c + 1) * tn], preferred_element_type=F32)
        for h in range(tn // HEAD_DIM):
            head = c * (tn // HEAD_DIM) + h
            col = slice(head * HEAD_DIM, (head + 1) * HEAD_DIM)
            xh = acc[:, h * HEAD_DIM:(h + 1) * HEAD_DIM]
            if head < n_rope_heads:
                ms = jnp.mean(xh * xh, axis=-1, keepdims=True)
                y = xh * lax.rsqrt(ms + NORM_EPS) * g_ref[:, col]
                xh = (y * cos + pltpu.roll(y, HEAD_DIM - 1, axis=1) * sin_a
                      + pltpu.roll(y, 1, axis=1) * sin_b)
            o_ref[:, col] = xh.astype(o_ref.dtype)


def proj_in(h, w_in, gain_cols, cos_t, sin_a, sin_b, *, seq, rope_width):
    m, k = h.shape
    n = w_in.shape[1]
    tm = _tile(seq, 512)
    n_pos_blocks = seq // tm
    vmem = 2 * (tm * k * 2 + k * n * 2 + tm * n * 2 + 3 * tm * HEAD_DIM * 4) + 6 * tm * COL_CHUNK * 4
    table = pl.BlockSpec((tm, HEAD_DIM), lambda i: (i % n_pos_blocks, 0))
    return pl.pallas_call(
        functools.partial(_proj_in_kernel, n_rope_heads=rope_width // HEAD_DIM),
        grid=(m // tm,),
        in_specs=[pl.BlockSpec((tm, k), lambda i: (i, 0)),
                  pl.BlockSpec((k, n), lambda i: (0, 0)),
                  pl.BlockSpec((1, n), lambda i: (0, 0)),
                  table, table, table],
        out_specs=pl.BlockSpec((tm, n), lambda i: (i, 0)),
        out_shape=jax.ShapeDtypeStruct((m, n), BF16),
        compiler_params=_params(("parallel",), vmem),
        name="proj_in",
    )(h, w_in, gain_cols, cos_t, sin_a, sin_b)


def _attn_kernel(q_ref, k_ref, v_ref, o_ref, *, group, scale):
    k = k_ref[...]
    v = v_ref[...]
    for g in range(group):
        q = q_ref[:, g * HEAD_DIM:(g + 1) * HEAD_DIM]
        s = lax.dot_general(q, k, (((1,), (1,)), ((), ())), preferred_element_type=F32)
        e = jnp.exp((s - jnp.max(s, axis=-1, keepdims=True)) * scale)
        l = jnp.sum(e, axis=-1, keepdims=True)
        o = jnp.dot(e.astype(BF16), v, preferred_element_type=F32) / l
        o_ref[:, g * HEAD_DIM:(g + 1) * HEAD_DIM] = o.astype(o_ref.dtype)


def gqa_attention(proj, *, batch, seq, attn_width, kv_width):
    group = N_Q_HEADS // N_KV_HEADS
    gw = group * HEAD_DIM
    tq = _tile(seq, 256)
    nq = seq // tq
    k_col0 = attn_width // HEAD_DIM
    v_col0 = (attn_width + kv_width) // HEAD_DIM
    vmem = 2 * (tq * gw * 2 * 2 + 2 * seq * HEAD_DIM * 2) + 4 * tq * seq * 4
    return pl.pallas_call(
        functools.partial(_attn_kernel, group=group, scale=HEAD_DIM ** -0.5),
        grid=(batch, N_KV_HEADS, nq),
        in_specs=[pl.BlockSpec((tq, gw), lambda b, h, i: (b * nq + i, h)),
                  pl.BlockSpec((seq, HEAD_DIM), lambda b, h, i: (b, k_col0 + h)),
                  pl.BlockSpec((seq, HEAD_DIM), lambda b, h, i: (b, v_col0 + h))],
        out_specs=pl.BlockSpec((tq, gw), lambda b, h, i: (b * nq + i, h)),
        out_shape=jax.ShapeDtypeStruct((batch * seq, attn_width), BF16),
        compiler_params=_params(("parallel", "parallel", "parallel"), vmem),
        name="gqa_attention",
    )(proj, proj, proj)


def _fourier_kernel(f_ref, cc_ref, sc_ref, cs_ref, ss_ref, wf_ref, o_ref, ab_ref, *, seq, norm):
    r = pl.program_id(1)
    ng = wf_ref.shape[0]
    cg = wf_ref.shape[1]

    @pl.when(r == 0)
    def _():
        for g in range(ng):
            fg = f_ref[:, g * cg:(g + 1) * cg]
            ab_ref[0:seq, g * cg:(g + 1) * cg] = jnp.dot(
                fg, cc_ref[...], preferred_element_type=F32).astype(BF16)
            ab_ref[seq:2 * seq, g * cg:(g + 1) * cg] = jnp.dot(
                fg, sc_ref[...], preferred_element_type=F32).astype(BF16)

    z = (jnp.dot(cs_ref[...], ab_ref[0:seq, :], preferred_element_type=F32)
         - jnp.dot(ss_ref[...], ab_ref[seq:2 * seq, :], preferred_element_type=F32)) * norm
    for g in range(ng):
        zg = z[:, g * cg:(g + 1) * cg].astype(BF16)
        o_ref[:, g * cg:(g + 1) * cg] = jnp.dot(
            zg, wf_ref[g].astype(BF16), preferred_element_type=F32).astype(o_ref.dtype)


def _dft_cos_sin(n):
    lo = _tile(n, 64)
    hi = n // lo
    j = jnp.arange(n, dtype=jnp.int32)

    def table(k):
        ang = ((k[:, None] * j[None, :]) % n).astype(F32) * (2.0 * math.pi / n)
        return jnp.cos(ang), jnp.sin(ang)

    ca, sa = table(jnp.arange(hi, dtype=jnp.int32) * lo)
    cb, sb = table(jnp.arange(lo, dtype=jnp.int32))
    cos = ca[:, None, :] * cb[None, :, :] - sa[:, None, :] * sb[None, :, :]
    sin = sa[:, None, :] * cb[None, :, :] + ca[:, None, :] * sb[None, :, :]
    return cos.reshape(n, n).astype(BF16), sin.reshape(n, n).astype(BF16)


def fourier_mixer(proj, w_fourier, *, batch, seq, col0):
    ng, cg, _ = w_fourier.shape
    fw = ng * cg
    tr = _tile(seq, 512)
    nr = seq // tr
    cc, sc = _dft_cos_sin(cg)
    cs, ss = _dft_cos_sin(seq)
    norm = 1.0 / math.sqrt(seq * cg)
    vmem = 2 * (seq * fw * 2 + 2 * cg * cg * 2 + 2 * tr * seq * 2 + ng * cg * cg * 4 + tr * fw * 2) \
        + 2 * seq * fw * 2 + 4 * tr * fw * 4
    return pl.pallas_call(
        functools.partial(_fourier_kernel, seq=seq, norm=norm),
        grid=(batch, nr),
        in_specs=[pl.BlockSpec((seq, fw), lambda b, r: (b, col0 // fw)),
                  pl.BlockSpec((cg, cg), lambda b, r: (0, 0)),
                  pl.BlockSpec((cg, cg), lambda b, r: (0, 0)),
                  pl.BlockSpec((tr, seq), lambda b, r: (r, 0)),
                  pl.BlockSpec((tr, seq), lambda b, r: (r, 0)),
                  pl.BlockSpec((ng, cg, cg), lambda b, r: (0, 0, 0))],
        out_specs=pl.BlockSpec((tr, fw), lambda b, r: (b * nr + r, 0)),
        out_shape=jax.ShapeDtypeStruct((batch * seq, fw), BF16),
        scratch_shapes=[pltpu.VMEM((2 * seq, fw), BF16)],
        compiler_params=_params(("parallel", "arbitrary"), vmem),
        name="fourier_mixer",
    )(proj, cc, sc, cs, ss, w_fourier)


def _router_affinity(hn, wr_ref, n_experts):
    hi = hn.astype(BF16)
    lo = (hn - hi.astype(F32)).astype(BF16)
    r_hi = jnp.dot(hi, wr_ref[...], preferred_element_type=F32)
    r_lo = jnp.dot(lo, wr_ref[...], preferred_element_type=F32)
    logits = r_hi[:, :LANES] + r_hi[:, LANES:] + r_lo[:, :LANES]
    lane = lax.broadcasted_iota(jnp.int32, logits.shape, 1)
    logits = jnp.where(lane < n_experts, logits, -jnp.inf)
    e = jnp.exp(logits - jnp.max(logits, axis=-1, keepdims=True))
    return e / jnp.sum(e, axis=-1, keepdims=True)


def _mm_res_norm_kernel(*refs, n_a, with_router, n_experts):
    a_refs = refs[:n_a]
    w_ref, x_ref, g_ref = refs[n_a:n_a + 3]
    pos = n_a + 3
    if with_router:
        wr_ref = refs[pos]
        pos += 1
    xo_ref, ho_ref = refs[pos:pos + 2]
    if with_router:
        aff_ref = refs[pos + 2]
    d = xo_ref.shape[1]
    tn = _tile(d, COL_CHUNK)
    for c in range(d // tn):
        sl = slice(c * tn, (c + 1) * tn)
        acc = x_ref[:, sl]
        row0 = 0
        for a_ref in a_refs:
            ka = a_ref.shape[1]
            acc = acc + jnp.dot(a_ref[...], w_ref[row0:row0 + ka, sl], preferred_element_type=F32)
            row0 += ka
        xo_ref[:, sl] = acc
    hn = _rms(xo_ref[...], g_ref[...])
    ho_ref[...] = hn.astype(ho_ref.dtype)
    if with_router:
        aff_ref[...] = _router_affinity(hn, wr_ref, n_experts)


def matmul_residual_norm(a_list, w, x_res, gain, *, router=None, tm_pref=512):
    m, d = x_res.shape
    k = w.shape[0]
    tm = _tile(m, tm_pref)
    in_specs = [pl.BlockSpec((tm, a.shape[1]), lambda i: (i, 0)) for a in a_list]
    in_specs += [pl.BlockSpec((k, d), lambda i: (0, 0)),
                 pl.BlockSpec((tm, d), lambda i: (i, 0)),
                 pl.BlockSpec((1, d), lambda i: (0, 0))]
    args = list(a_list) + [w, x_res, gain.reshape(1, d)]
    out_specs = [pl.BlockSpec((tm, d), lambda i: (i, 0)),
                 pl.BlockSpec((tm, d), lambda i: (i, 0))]
    out_shape = [jax.ShapeDtypeStruct((m, d), F32), jax.ShapeDtypeStruct((m, d), BF16)]
    n_experts = 0
    if router is not None:
        wr, n_experts = router
        in_specs.append(pl.BlockSpec((d, 2 * LANES), lambda i: (0, 0)))
        args.append(wr)
        out_specs.append(pl.BlockSpec((tm, LANES), lambda i: (i, 0)))
        out_shape.append(jax.ShapeDtypeStruct((m, LANES), F32))
    vmem = 2 * (tm * k * 2 + k * d * 2 + tm * d * 4 + tm * d * 4 + tm * d * 2) \
        + 3 * tm * d * 4 + 4 * d * LANES * 2
    return pl.pallas_call(
        functools.partial(_mm_res_norm_kernel, n_a=len(a_list),
                          with_router=router is not None, n_experts=n_experts),
        grid=(m // tm,),
        in_specs=in_specs,
        out_specs=out_specs,
        out_shape=out_shape,
        compiler_params=_params(("parallel",), vmem),
        name="matmul_residual_norm",
    )(*args)


def _cross_attn_kernel(q_ref, k_ref, v_ref, o_ref, *, n_heads, scale):
    dh = q_ref.shape[1] // n_heads
    for h in range(n_heads):
        sl = slice(h * dh, (h + 1) * dh)
        s = lax.dot_general(q_ref[:, sl], k_ref[:, sl], (((1,), (1,)), ((), ())),
                            preferred_element_type=F32)
        e = jnp.exp((s - jnp.max(s, axis=-1, keepdims=True)) * scale)
        l = jnp.sum(e, axis=-1, keepdims=True)
        o = jnp.dot(e.astype(BF16), v_ref[:, sl], preferred_element_type=F32) / l
        o_ref[:, sl] = o.astype(o_ref.dtype)


def cross_attention(q, k, v, *, batch, seq, mem_len):
    d = q.shape[1]
    tq = _tile(seq, 512)
    nq = seq // tq
    vmem = 2 * (2 * tq * d * 2 + 2 * mem_len * d * 2) + 6 * tq * mem_len * 4 + 2 * tq * d * 4
    return pl.pallas_call(
        functools.partial(_cross_attn_kernel, n_heads=N_MEM_HEADS,
                          scale=(d // N_MEM_HEADS) ** -0.5),
        grid=(batch, nq),
        in_specs=[pl.BlockSpec((tq, d), lambda b, i: (b * nq + i, 0)),
                  pl.BlockSpec((mem_len, d), lambda b, i: (b, 0)),
                  pl.BlockSpec((mem_len, d), lambda b, i: (b, 0))],
        out_specs=pl.BlockSpec((tq, d), lambda b, i: (b * nq + i, 0)),
        out_shape=jax.ShapeDtypeStruct((batch * seq, d), BF16),
        compiler_params=_params(("parallel", "parallel"), vmem),
        name="cross_attention",
    )(q, k, v)


def _prefix_count(x, lane):
    n = x.shape[1]
    shift = 1
    while shift < n:
        x = x + jnp.where(lane >= shift, pltpu.roll(x, shift, axis=1), 0)
        shift *= 2
    return x


def _route_kernel(aff_ref, prow_ref, arow_ref, pcol_ref, *, n_experts, capacity):
    e_pad = prow_ref.shape[1]
    a_t = aff_ref[...].T[:e_pad]
    arow_ref[0] = a_t
    seq = a_t.shape[1]

    def body(i, thr_bits):
        cand = thr_bits | jnp.left_shift(jnp.int32(1), 30 - i)
        cnt = jnp.sum((a_t >= pltpu.bitcast(cand, F32)).astype(F32), axis=1, keepdims=True)
        return jnp.where(cnt >= capacity, cand, thr_bits)

    thr = pltpu.bitcast(lax.fori_loop(0, 31, body, jnp.zeros((e_pad, 1), jnp.int32)), F32)
    lane = lax.broadcasted_iota(jnp.int32, a_t.shape, 1)
    gt = a_t > thr
    eq = a_t == thr
    need = capacity - jnp.sum(gt.astype(F32), axis=1, keepdims=True).astype(jnp.int32)
    sel = gt | (eq & (_prefix_count(eq.astype(jnp.int32), lane) <= need))
    pos = jnp.where(sel, _prefix_count(sel.astype(jnp.int32), lane) - 1, -1)
    row = lax.broadcasted_iota(jnp.int32, a_t.shape, 0)
    pos = jnp.where(row < n_experts, pos, -1)
    prow_ref[0] = pos
    unused = jnp.full((LANES - e_pad, seq), -1.0, F32)
    pcol_ref[...] = jnp.concatenate([pos.astype(F32), unused], axis=0).T


def route(aff, *, batch, seq, n_experts, capacity):
    e_pad = max(8, n_experts)
    return pl.pallas_call(
        functools.partial(_route_kernel, n_experts=n_experts, capacity=capacity),
        grid=(batch,),
        in_specs=[pl.BlockSpec((seq, LANES), lambda b: (b, 0))],
        out_specs=[pl.BlockSpec((1, e_pad, seq), lambda b: (b, 0, 0)),
                   pl.BlockSpec((1, e_pad, seq), lambda b: (b, 0, 0)),
                   pl.BlockSpec((seq, LANES), lambda b: (b, 0))],
        out_shape=[jax.ShapeDtypeStruct((batch, e_pad, seq), jnp.int32),
                   jax.ShapeDtypeStruct((batch, e_pad, seq), F32),
                   jax.ShapeDtypeStruct((batch * seq, LANES), F32)],
        compiler_params=_params(("parallel",), 24 * seq * LANES * 4),
        name="route",
    )(aff)


def _gather_kernel(h_ref, prow_ref, o_ref):
    e = pl.program_id(1)
    cap = o_ref.shape[0]
    seq = h_ref.shape[0]
    prow = prow_ref[0, pl.ds(e, 1), :]
    slot = lax.broadcasted_iota(jnp.int32, (cap, seq), 0)
    onehot = (slot == prow).astype(BF16)
    o_ref[...] = jnp.dot(onehot, h_ref[...], preferred_element_type=F32).astype(o_ref.dtype)


def gather_tokens(h, prow, *, batch, seq, n_experts, capacity):
    d = h.shape[1]
    e_pad = prow.shape[1]
    vmem = 2 * (seq * d * 2 + e_pad * seq * 4 + capacity * d * 2) + capacity * seq * 8 + capacity * d * 4
    return pl.pallas_call(
        _gather_kernel,
        grid=(batch, n_experts),
        in_specs=[pl.BlockSpec((seq, d), lambda b, e: (b, 0)),
                  pl.BlockSpec((1, e_pad, seq), lambda b, e: (b, 0, 0))],
        out_specs=pl.BlockSpec((capacity, d), lambda b, e: (e * batch + b, 0)),
        out_shape=jax.ShapeDtypeStruct((n_experts * batch * capacity, d), BF16),
        compiler_params=_params(("parallel", "arbitrary"), vmem),
        name="gather_tokens",
    )(h, prow)


def _ffn_kernel(x_ref, wg_ref, wu_ref, wd_ref, prow_ref, arow_ref, o_ref, hid_ref, gate_ref,
                *, n_up, capacity):
    e = pl.program_id(0)
    s = pl.program_id(1)
    rows = x_ref.shape[0]
    tf = wg_ref.shape[1]

    @pl.when(s < n_up)
    def _():
        x = x_ref[...]
        a = jnp.dot(x, wg_ref[...].astype(BF16), preferred_element_type=F32)
        u = jnp.dot(x, wu_ref[...].astype(BF16), preferred_element_type=F32)
        hid_ref[s] = (a / (1.0 + jnp.exp(-a)) * u).astype(BF16)

    @pl.when(s == n_up)
    def _():
        seq = prow_ref.shape[2]
        slot = lax.broadcasted_iota(jnp.int32, (capacity, seq), 0)
        for b in range(rows // capacity):
            mine = slot == prow_ref[b, pl.ds(e, 1), :]
            gate_ref[b * capacity:(b + 1) * capacity, :] = jnp.sum(
                jnp.where(mine, arow_ref[b, pl.ds(e, 1), :], 0.0), axis=1, keepdims=True)

    @pl.when(s >= n_up)
    def _():
        y = jnp.dot(hid_ref[0], wd_ref[0:tf, :].astype(BF16), preferred_element_type=F32)
        for c in range(1, n_up):
            y = y + jnp.dot(hid_ref[c], wd_ref[c * tf:(c + 1) * tf, :].astype(BF16),
                            preferred_element_type=F32)
        o_ref[...] = (y * gate_ref[...]).astype(o_ref.dtype)


def expert_ffn(xs, w_gate, w_up, w_down, prow, arow, layer, *, rows_per_expert, capacity):
    _, n_experts, d, ff = w_gate.shape
    tf = _tile(ff, 256)
    tn = _tile(d, COL_CHUNK)
    n_up, n_down = ff // tf, d // tn
    r = rows_per_expert
    batch, e_pad, seq = prow.shape
    vmem = 2 * (r * d * 2 + 2 * d * tf * 4 + ff * tn * 4 + r * tn * 2 + 2 * batch * e_pad * seq * 4) \
        + r * ff * 2 + r * LANES * 4 + 2 * d * tf * 2 + ff * tn * 2 + 4 * r * tf * 4 + 2 * r * tn * 4 \
        + 2 * capacity * seq * 4
    return pl.pallas_call(
        functools.partial(_ffn_kernel, n_up=n_up, capacity=capacity),
        grid=(n_experts, n_up + n_down),
        in_specs=[pl.BlockSpec((r, d), lambda e, s: (e, 0)),
                  pl.BlockSpec((None, None, d, tf),
                               lambda e, s: (layer, e, 0, jnp.minimum(s, n_up - 1))),
                  pl.BlockSpec((None, None, d, tf),
                               lambda e, s: (layer, e, 0, jnp.minimum(s, n_up - 1))),
                  pl.BlockSpec((None, None, ff, tn),
                               lambda e, s: (layer, e, 0, jnp.maximum(s - n_up, 0))),
                  pl.BlockSpec((batch, e_pad, seq), lambda e, s: (0, 0, 0)),
                  pl.BlockSpec((batch, e_pad, seq), lambda e, s: (0, 0, 0))],
        out_specs=pl.BlockSpec((r, tn), lambda e, s: (e, jnp.maximum(s - n_up, 0))),
        out_shape=jax.ShapeDtypeStruct((n_experts * r, d), BF16),
        scratch_shapes=[pltpu.VMEM((n_up, r, tf), BF16), pltpu.VMEM((r, 1), F32)],
        compiler_params=_params(("parallel", "arbitrary"), vmem),
        name="expert_ffn",
    )(xs, w_gate, w_up, w_down, prow, arow)


def _combine_kernel(x_ref, y_ref, pcol_ref, g_ref, xo_ref, no_ref):
    g = pl.program_id(2)
    tt, d = x_ref.shape
    n_group, cap, _ = y_ref.shape

    @pl.when(g == 0)
    def _():
        xo_ref[...] = x_ref[...]

    lane = lax.broadcasted_iota(jnp.int32, (tt, LANES), 1)
    slot = lax.broadcasted_iota(jnp.int32, (tt, cap), 1).astype(F32)
    pcol = pcol_ref[...]
    hots = []
    for j in range(n_group):
        slot_of_token = jnp.sum(jnp.where(lane == g * n_group + j, pcol, 0.0),
                                axis=1, keepdims=True)
        hots.append((slot == slot_of_token).astype(BF16))
    onehot = jnp.concatenate(hots, axis=1)
    xo_ref[...] += jnp.dot(onehot, y_ref[...].reshape(n_group * cap, d),
                           preferred_element_type=F32)

    @pl.when(g == pl.num_programs(2) - 1)
    def _():
        no_ref[...] = _rms(xo_ref[...], g_ref[...]).astype(no_ref.dtype)


def combine(x_res, y, pcol, gain, *, batch, seq, n_experts, capacity, norm_dtype):
    d = x_res.shape[1]
    tt = _tile(seq, 512)
    nt = seq // tt
    ng = _tile(n_experts, 4)
    y4 = y.reshape(n_experts, batch, capacity, d)
    vmem = 2 * (tt * d * 4 + ng * capacity * d * 2 + tt * LANES * 4 + tt * d * 4 + tt * d * 4) \
        + 3 * tt * d * 4 + 2 * tt * ng * capacity * 4
    return pl.pallas_call(
        _combine_kernel,
        grid=(batch, nt, n_experts // ng),
        in_specs=[pl.BlockSpec((tt, d), lambda b, t, g: (b * nt + t, 0)),
                  pl.BlockSpec((ng, None, capacity, d), lambda b, t, g: (g, b, 0, 0)),
                  pl.BlockSpec((tt, LANES), lambda b, t, g: (b * nt + t, 0)),
                  pl.BlockSpec((1, d), lambda b, t, g: (0, 0))],
        out_specs=[pl.BlockSpec((tt, d), lambda b, t, g: (b * nt + t, 0)),
                   pl.BlockSpec((tt, d), lambda b, t, g: (b * nt + t, 0))],
        out_shape=[jax.ShapeDtypeStruct((batch * seq, d), F32),
                   jax.ShapeDtypeStruct((batch * seq, d), norm_dtype)],
        compiler_params=_params(("parallel", "parallel", "arbitrary"), vmem),
        name="combine",
    )(x_res, y4, pcol, gain.reshape(1, d))


def _pool_kernel(h_ref, o_ref, pad_ref, *, seq, windows, margin):
    g = pl.program_id(1)
    width = h_ref.shape[1]
    h = h_ref[...]
    zeros = jnp.zeros((margin, width), F32)
    pad_ref[0:margin, :] = zeros
    pad_ref[margin + seq:2 * margin + seq, :] = zeros
    pad_ref[margin:margin + seq, :] = h
    t = lax.broadcasted_iota(jnp.int32, (seq, 1), 0)
    for gi, w in enumerate(windows):
        @pl.when(g == gi)
        def _(w=w):
            total = pad_ref[margin - w // 2:margin - w // 2 + seq, :]
            for off in range(1 - w // 2, w - w // 2):
                total = total + pad_ref[margin + off:margin + off + seq, :]
            count = (jnp.minimum(t + (w - w // 2), seq) - jnp.maximum(t - w // 2, 0)).astype(F32)
            o_ref[...] = (total / count - h).astype(o_ref.dtype)


def pool_windows(hn, *, batch, seq, n_groups):
    d = hn.shape[1]
    pg = d // n_groups
    margin = 8 * ((max(POOL_WINDOWS) // 2 + 7) // 8)
    vmem = 2 * (seq * pg * 4 + seq * pg * 2) + (seq + 2 * margin) * pg * 4 + 4 * seq * pg * 4
    return pl.pallas_call(
        functools.partial(_pool_kernel, seq=seq, windows=POOL_WINDOWS, margin=margin),
        grid=(batch, n_groups),
        in_specs=[pl.BlockSpec((seq, pg), lambda b, g: (b, g))],
        out_specs=pl.BlockSpec((seq, pg), lambda b, g: (b, g)),
        out_shape=jax.ShapeDtypeStruct((batch * seq, d), BF16),
        scratch_shapes=[pltpu.VMEM((seq + 2 * margin, pg), F32)],
        compiler_params=_params(("parallel", "parallel"), vmem),
        name="pool_windows",
    )(hn)


def _pool_out_kernel(p_ref, w_ref, s_ref, x_ref, g_ref, xo_ref, ho_ref):
    ng, pg, _ = w_ref.shape
    for g in range(ng):
        sl = slice(g * pg, (g + 1) * pg)
        mixed = jnp.dot(p_ref[:, sl], w_ref[g], preferred_element_type=F32)
        xo_ref[:, sl] = mixed * s_ref[:, sl] + x_ref[:, sl]
    ho_ref[...] = _rms(xo_ref[...], g_ref[...]).astype(ho_ref.dtype)


def pool_out(pooled, w_pool, pool_scale, x_res, gain):
    m, d = x_res.shape
    ng, pg, _ = w_pool.shape
    tm = _tile(m, 512)
    vmem = 2 * (tm * d * 2 + ng * pg * pg * 2 + tm * d * 4 + tm * d * 4 + tm * d * 2) + 3 * tm * d * 4
    return pl.pallas_call(
        _pool_out_kernel,
        grid=(m // tm,),
        in_specs=[pl.BlockSpec((tm, d), lambda i: (i, 0)),
                  pl.BlockSpec((ng, pg, pg), lambda i: (0, 0, 0)),
                  pl.BlockSpec((1, d), lambda i: (0, 0)),
                  pl.BlockSpec((tm, d), lambda i: (i, 0)),
                  pl.BlockSpec((1, d), lambda i: (0, 0))],
        out_specs=[pl.BlockSpec((tm, d), lambda i: (i, 0)),
                   pl.BlockSpec((tm, d), lambda i: (i, 0))],
        out_shape=[jax.ShapeDtypeStruct((m, d), F32), jax.ShapeDtypeStruct((m, d), BF16)],
        compiler_params=_params(("parallel",), vmem),
        name="pool_out",
    )(pooled, w_pool, pool_scale.reshape(1, d), x_res, gain.reshape(1, d))


def _rope_tables(seq):
    half = HEAD_DIM // 2
    rows = seq // GRID_W
    row_idx = jnp.repeat(jnp.arange(rows), GRID_W).astype(F32)
    col_idx = jnp.tile(jnp.arange(GRID_W), rows).astype(F32)
    inv_freq = 1.0 / (ROPE_THETA ** (jnp.arange(0, half, 2, dtype=F32) / half))
    ang = jnp.concatenate([row_idx[:, None] * inv_freq[None, :],
                           col_idx[:, None] * inv_freq[None, :]], axis=-1)
    cos, sin = jnp.cos(ang), jnp.sin(ang)
    zero = jnp.zeros_like(sin)
    cos_full = jnp.stack([cos, cos], axis=-1).reshape(seq, HEAD_DIM)
    sin_even = jnp.stack([-sin, zero], axis=-1).reshape(seq, HEAD_DIM)
    sin_odd = jnp.stack([zero, sin], axis=-1).reshape(seq, HEAD_DIM)
    return cos_full, sin_even, sin_odd


def _split_router(w_router):
    d, n_experts = w_router.shape
    w = jnp.pad(w_router, ((0, 0), (0, LANES - n_experts)))
    hi = w.astype(BF16)
    lo = (w - hi.astype(F32)).astype(BF16)
    return jnp.concatenate([hi, lo], axis=1), n_experts


def _moe(x_res, h, aff, w_gate, w_up, w_down, layer, next_gain, *, batch, seq, norm_dtype):
    n_experts = w_gate.shape[1]
    capacity = EC_CAPACITY_FACTOR * seq // n_experts
    prow, arow, pcol = route(aff, batch=batch, seq=seq, n_experts=n_experts, capacity=capacity)
    xs = gather_tokens(h, prow, batch=batch, seq=seq, n_experts=n_experts, capacity=capacity)
    y = expert_ffn(xs, w_gate, w_up, w_down, prow, arow, layer,
                   rows_per_expert=batch * capacity, capacity=capacity)
    return combine(x_res, y, pcol, next_gain, batch=batch, seq=seq,
                   n_experts=n_experts, capacity=capacity, norm_dtype=norm_dtype)


def _cross_block(x, h, mem2d, layer, cross_w_q, cross_w_k, cross_w_v, cross_w_o, mem_norm,
                 ffn_norm, router_w, *, batch, seq, mem_len):
    mem_n = rms_norm_rows(mem2d, mem_norm[layer], BF16)
    k = matmul_resident(mem_n, cross_w_k[layer].astype(BF16))
    v = matmul_resident(mem_n, cross_w_v[layer].astype(BF16))
    q = matmul_resident(h, cross_w_q[layer].astype(BF16))
    o = cross_attention(q, k, v, batch=batch, seq=seq, mem_len=mem_len)
    return matmul_residual_norm([o], cross_w_o[layer].astype(BF16), x, ffn_norm[layer],
                                router=_split_router(router_w[layer]))


def kernel(x, mem, mix_norm, attn_w_in, q_gain, k_gain, fourier_w, attn_w_out, pool_w, pool_scale,
           cross_norm, mem_norm, cross_w_q, cross_w_k, cross_w_v, cross_w_o, ffn_norm, router_w,
           expert_w_gate, expert_w_up, expert_w_down, final_norm):
    batch, seq, d = x.shape
    mem_len = mem.shape[1]
    depth = mix_norm.shape[0]
    attn_width = N_Q_HEADS * HEAD_DIM
    kv_width = N_KV_HEADS * HEAD_DIM
    fourier_width = N_FOURIER_GROUPS * FOURIER_GROUP
    rope_width = attn_width + kv_width
    cos_t, sin_a, sin_b = _rope_tables(seq)

    xf = x.reshape(batch * seq, d)
    mem2d = mem.reshape(batch * mem_len, d)
    cross = functools.partial(_cross_block, mem2d=mem2d, cross_w_q=cross_w_q, cross_w_k=cross_w_k,
                              cross_w_v=cross_w_v, cross_w_o=cross_w_o, mem_norm=mem_norm,
                              ffn_norm=ffn_norm, router_w=router_w,
                              batch=batch, seq=seq, mem_len=mem_len)
    normed = None
    for layer in range(depth):
        i = layer // 2
        if layer % 2 == 0:
            if normed is None:
                h = rms_norm_rows(xf, mix_norm[layer], BF16)
            else:
                h = normed.astype(BF16)
            gain_cols = jnp.concatenate([jnp.tile(q_gain[i], N_Q_HEADS), jnp.tile(k_gain[i], N_KV_HEADS),
                                         jnp.ones((kv_width + fourier_width,), F32)]).reshape(1, -1)
            proj = proj_in(h, attn_w_in[i].astype(BF16), gain_cols, cos_t, sin_a, sin_b,
                           seq=seq, rope_width=rope_width)
            o_attn = gqa_attention(proj, batch=batch, seq=seq, attn_width=attn_width, kv_width=kv_width)
            o_four = fourier_mixer(proj, fourier_w[i], batch=batch, seq=seq,
                                   col0=attn_width + 2 * kv_width)
            xf, h = matmul_residual_norm([o_attn, o_four], attn_w_out[i].astype(BF16), xf,
                                         cross_norm[layer])
        else:
            if normed is None:
                normed = rms_norm_rows(xf, mix_norm[layer], F32)
            pooled = pool_windows(normed, batch=batch, seq=seq, n_groups=len(POOL_WINDOWS))
            xf, h = pool_out(pooled, pool_w[i].astype(BF16), pool_scale[i], xf, cross_norm[layer])
        xf, h, aff = cross(xf, h, layer=layer)
        next_gain = mix_norm[layer + 1] if layer + 1 < depth else final_norm
        xf, normed = _moe(xf, h, aff, expert_w_gate, expert_w_up, expert_w_down, layer,
                          next_gain, batch=batch, seq=seq, norm_dtype=F32)
    return normed.reshape(batch, seq, d)
```

```python
import functools
import math

import jax
import jax.numpy as jnp
from jax import lax
from jax.experimental import pallas as pl
from jax.experimental.pallas import tpu as pltpu

F32 = jnp.float32
BF16 = jnp.bfloat16

GRID_W = 64
HEAD_DIM = 128
N_Q_HEADS = 12
N_KV_HEADS = 4
N_FOURIER_GROUPS = 4
FOURIER_GROUP = 128
ROPE_THETA = 10000.0
POOL_WINDOWS = (2, 4, 8, 16)
N_MEM_HEADS = 4
EC_CAPACITY_FACTOR = 2
NORM_EPS = 1e-6

LANES = 128
V7X_VMEM_BYTES = 64 * 1024 * 1024
VMEM_HEADROOM_BYTES = 6 * 1024 * 1024
COL_CHUNK = 512


def _params(semantics, vmem_bytes):
    limit = min(int(vmem_bytes) + VMEM_HEADROOM_BYTES, V7X_VMEM_BYTES - VMEM_HEADROOM_BYTES)
    return pltpu.CompilerParams(dimension_semantics=semantics, vmem_limit_bytes=limit)


def _tile(n, pref):
    t = min(n, pref)
    while n % t:
        t //= 2
    return t


def _rms(x, gain):
    ms = jnp.mean(x * x, axis=-1, keepdims=True)
    return x * lax.rsqrt(ms + NORM_EPS) * gain


def _norm_kernel(x_ref, g_ref, o_ref):
    o_ref[...] = _rms(x_ref[...], g_ref[...]).astype(o_ref.dtype)


def rms_norm_rows(x, gain, out_dtype):
    m, d = x.shape
    tm = _tile(m, 512)
    return pl.pallas_call(
        _norm_kernel,
        grid=(m // tm,),
        in_specs=[pl.BlockSpec((tm, d), lambda i: (i, 0)),
                  pl.BlockSpec((1, d), lambda i: (0, 0))],
        out_specs=pl.BlockSpec((tm, d), lambda i: (i, 0)),
        out_shape=jax.ShapeDtypeStruct((m, d), out_dtype),
        compiler_params=_params(("parallel",), 2 * tm * d * 8),
        name="rms_norm_rows",
    )(x, gain.reshape(1, d))


def _cast_kernel(w_ref, o_ref):
    o_ref[...] = w_ref[...].astype(o_ref.dtype)


def cast_weight(w, lead):
    k, n = w.shape[-2:]
    tk = _tile(k, 1024)
    squeezed = (None,) * len(lead)
    return pl.pallas_call(
        _cast_kernel,
        grid=(k // tk,),
        in_specs=[pl.BlockSpec(squeezed + (tk, n), lambda i: tuple(lead) + (i, 0))],
        out_specs=pl.BlockSpec((tk, n), lambda i: (i, 0)),
        out_shape=jax.ShapeDtypeStruct((k, n), BF16),
        compiler_params=_params(("parallel",), 2 * tk * n * 6),
        name="cast_weight",
    )(w)


def _mm_kernel(a_ref, w_ref, o_ref):
    n = o_ref.shape[1]
    tn = _tile(n, COL_CHUNK)
    a = a_ref[...]
    for c in range(n // tn):
        sl = slice(c * tn, (c + 1) * tn)
        o_ref[:, sl] = jnp.dot(a, w_ref[:, sl], preferred_element_type=F32).astype(o_ref.dtype)


def matmul_resident(a, w, out_dtype=BF16, tm_pref=512):
    m, k = a.shape
    n = w.shape[1]
    tm = _tile(m, tm_pref)
    vmem = 2 * (tm * k * 2 + k * n * 2 + tm * n * 2) + 2 * tm * COL_CHUNK * 4
    return pl.pallas_call(
        _mm_kernel,
        grid=(m // tm,),
        in_specs=[pl.BlockSpec((tm, k), lambda i: (i, 0)),
                  pl.BlockSpec((k, n), lambda i: (0, 0))],
        out_specs=pl.BlockSpec((tm, n), lambda i: (i, 0)),
        out_shape=jax.ShapeDtypeStruct((m, n), out_dtype),
        compiler_params=_params(("parallel",), vmem),
        name="matmul_resident",
    )(a, w)


def _proj_in_kernel(a_ref, ng_ref, w_ref, g_ref, cos_ref, sa_ref, sb_ref, o_ref, *, n_rope_heads):
    n = o_ref.shape[1]
    tn = _tile(n, COL_CHUNK)
    a = _rms(a_ref[...], ng_ref[...]).astype(BF16)
    cos, sin_a, sin_b = cos_ref[...], sa_ref[...], sb_ref[...]
    for c in range(n // tn):
        acc = jnp.dot(a, w_ref[:, c * tn:(c + 1) * tn], preferred_element_type=F32)
        for h in range(tn // HEAD_DIM):
            head = c * (tn // HEAD_DIM) + h
            col = slice(head * HEAD_DIM, (head + 1) * HEAD_DIM)
            xh = acc[:, h * HEAD_DIM:(h + 1) * HEAD_DIM]
            if head < n_rope_heads:
                ms = jnp.mean(xh * xh, axis=-1, keepdims=True)
                y = xh * lax.rsqrt(ms + NORM_EPS) * g_ref[:, col]
                xh = (y * cos + pltpu.roll(y, HEAD_DIM - 1, axis=1) * sin_a
                      + pltpu.roll(y, 1, axis=1) * sin_b)
            o_ref[:, col] = xh.astype(o_ref.dtype)


def proj_in(x, norm_gain, w_in, gain_cols, cos_t, sin_a, sin_b, *, seq, rope_width):
    m, k = x.shape
    n = w_in.shape[1]
    tm = _tile(seq, 512)
    n_pos_blocks = seq // tm
    vmem = 2 * (tm * k * 4 + k * n * 2 + tm * n * 2 + 3 * tm * HEAD_DIM * 4) + 6 * tm * COL_CHUNK * 4 \
        + 2 * tm * k * 4
    table = pl.BlockSpec((tm, HEAD_DIM), lambda i: (i % n_pos_blocks, 0))
    return pl.pallas_call(
        functools.partial(_proj_in_kernel, n_rope_heads=rope_width // HEAD_DIM),
        grid=(m // tm,),
        in_specs=[pl.BlockSpec((tm, k), lambda i: (i, 0)),
                  pl.BlockSpec((1, k), lambda i: (0, 0)),
                  pl.BlockSpec((k, n), lambda i: (0, 0)),
                  pl.BlockSpec((1, n), lambda i: (0, 0)),
                  table, table, table],
        out_specs=pl.BlockSpec((tm, n), lambda i: (i, 0)),
        out_shape=jax.ShapeDtypeStruct((m, n), BF16),
        compiler_params=_params(("parallel",), vmem),
        name="proj_in",
    )(x, norm_gain.reshape(1, k), w_in, gain_cols, cos_t, sin_a, sin_b)


def _attn_kernel(q_ref, k_ref, v_ref, o_ref, *, group, scale):
    k = k_ref[...]
    v = v_ref[...]
    for g in range(group):
        q = q_ref[:, g * HEAD_DIM:(g + 1) * HEAD_DIM]
        s = lax.dot_general(q, k, (((1,), (1,)), ((), ())), preferred_element_type=F32)
        e = jnp.exp((s - jnp.max(s, axis=-1, keepdims=True)) * scale)
        l = jnp.sum(e, axis=-1, keepdims=True)
        o = jnp.dot(e.astype(BF16), v, preferred_element_type=F32) / l
        o_ref[:, g * HEAD_DIM:(g + 1) * HEAD_DIM] = o.astype(o_ref.dtype)


def gqa_attention(proj, *, batch, seq, attn_width, kv_width):
    group = N_Q_HEADS // N_KV_HEADS
    gw = group * HEAD_DIM
    tq = _tile(seq, 256)
    nq = seq // tq
    k_col0 = attn_width // HEAD_DIM
    v_col0 = (attn_width + kv_width) // HEAD_DIM
    vmem = 2 * (tq * gw * 2 * 2 + 2 * seq * HEAD_DIM * 2) + 4 * tq * seq * 4
    return pl.pallas_call(
        functools.partial(_attn_kernel, group=group, scale=HEAD_DIM ** -0.5),
        grid=(batch, N_KV_HEADS, nq),
        in_specs=[pl.BlockSpec((tq, gw), lambda b, h, i: (b * nq + i, h)),
                  pl.BlockSpec((seq, HEAD_DIM), lambda b, h, i: (b, k_col0 + h)),
                  pl.BlockSpec((seq, HEAD_DIM), lambda b, h, i: (b, v_col0 + h))],
        out_specs=pl.BlockSpec((tq, gw), lambda b, h, i: (b * nq + i, h)),
        out_shape=jax.ShapeDtypeStruct((batch * seq, attn_width), BF16),
        compiler_params=_params(("parallel", "parallel", "parallel"), vmem),
        name="gqa_attention",
    )(proj, proj, proj)


def _fourier_kernel(f_ref, cc_ref, sc_ref, cs_ref, ss_ref, wf_ref, o_ref, ab_ref, *, seq, norm):
    r = pl.program_id(1)
    ng = wf_ref.shape[0]
    cg = wf_ref.shape[1]

    @pl.when(r == 0)
    def _():
        for g in range(ng):
            fg = f_ref[:, g * cg:(g + 1) * cg]
            ab_ref[0:seq, g * cg:(g + 1) * cg] = jnp.dot(
                fg, cc_ref[...], preferred_element_type=F32).astype(BF16)
            ab_ref[seq:2 * seq, g * cg:(g + 1) * cg] = jnp.dot(
                fg, sc_ref[...], preferred_element_type=F32).astype(BF16)

    z = (jnp.dot(cs_ref[...], ab_ref[0:seq, :], preferred_element_type=F32)
         - jnp.dot(ss_ref[...], ab_ref[seq:2 * seq, :], preferred_element_type=F32)) * norm
    for g in range(ng):
        zg = z[:, g * cg:(g + 1) * cg].astype(BF16)
        o_ref[:, g * cg:(g + 1) * cg] = jnp.dot(
            zg, wf_ref[g].astype(BF16), preferred_element_type=F32).astype(o_ref.dtype)


def _dft_cos_sin(n):
    lo = _tile(n, 64)
    hi = n // lo
    j = jnp.arange(n, dtype=jnp.int32)

    def table(k):
        ang = ((k[:, None] * j[None, :]) % n).astype(F32) * (2.0 * math.pi / n)
        return jnp.cos(ang), jnp.sin(ang)

    ca, sa = table(jnp.arange(hi, dtype=jnp.int32) * lo)
    cb, sb = table(jnp.arange(lo, dtype=jnp.int32))
    cos = ca[:, None, :] * cb[None, :, :] - sa[:, None, :] * sb[None, :, :]
    sin = sa[:, None, :] * cb[None, :, :] + ca[:, None, :] * sb[None, :, :]
    return cos.reshape(n, n).astype(BF16), sin.reshape(n, n).astype(BF16)


def fourier_mixer(proj, w_fourier, *, batch, seq, col0):
    ng, cg, _ = w_fourier.shape
    fw = ng * cg
    tr = _tile(seq, 512)
    nr = seq // tr
    cc, sc = _dft_cos_sin(cg)
    cs, ss = _dft_cos_sin(seq)
    norm = 1.0 / math.sqrt(seq * cg)
    vmem = 2 * (seq * fw * 2 + 2 * cg * cg * 2 + 2 * tr * seq * 2 + ng * cg * cg * 4 + tr * fw * 2) \
        + 2 * seq * fw * 2 + 4 * tr * fw * 4
    return pl.pallas_call(
        functools.partial(_fourier_kernel, seq=seq, norm=norm),
        grid=(batch, nr),
        in_specs=[pl.BlockSpec((seq, fw), lambda b, r: (b, col0 // fw)),
                  pl.BlockSpec((cg, cg), lambda b, r: (0, 0)),
                  pl.BlockSpec((cg, cg), lambda b, r: (0, 0)),
                  pl.BlockSpec((tr, seq), lambda b, r: (r, 0)),
                  pl.BlockSpec((tr, seq), lambda b, r: (r, 0)),
                  pl.BlockSpec((ng, cg, cg), lambda b, r: (0, 0, 0))],
        out_specs=pl.BlockSpec((tr, fw), lambda b, r: (b * nr + r, 0)),
        out_shape=jax.ShapeDtypeStruct((batch * seq, fw), BF16),
        scratch_shapes=[pltpu.VMEM((2 * seq, fw), BF16)],
        compiler_params=_params(("parallel", "arbitrary"), vmem),
        name="fourier_mixer",
    )(proj, cc, sc, cs, ss, w_fourier)


def _router_affinity(hn, wr_ref, n_experts):
    hi = hn.astype(BF16)
    lo = (hn - hi.astype(F32)).astype(BF16)
    r_hi = jnp.dot(hi, wr_ref[...], preferred_element_type=F32)
    r_lo = jnp.dot(lo, wr_ref[...], preferred_element_type=F32)
    logits = r_hi[:, :LANES] + r_hi[:, LANES:] + r_lo[:, :LANES]
    lane = lax.broadcasted_iota(jnp.int32, logits.shape, 1)
    logits = jnp.where(lane < n_experts, logits, -jnp.inf)
    e = jnp.exp(logits - jnp.max(logits, axis=-1, keepdims=True))
    return e / jnp.sum(e, axis=-1, keepdims=True)


def _mm_res_norm_kernel(*refs, n_a, with_router, n_experts):
    a_refs = refs[:n_a]
    w_ref, x_ref, g_ref = refs[n_a:n_a + 3]
    pos = n_a + 3
    if with_router:
        wr_ref = refs[pos]
        pos += 1
    xo_ref, ho_ref = refs[pos:pos + 2]
    if with_router:
        aff_ref = refs[pos + 2]
    d = xo_ref.shape[1]
    tn = _tile(d, COL_CHUNK)
    for c in range(d // tn):
        sl = slice(c * tn, (c + 1) * tn)
        acc = x_ref[:, sl]
        row0 = 0
        for a_ref in a_refs:
            ka = a_ref.shape[1]
            acc = acc + jnp.dot(a_ref[...], w_ref[row0:row0 + ka, sl], preferred_element_type=F32)
            row0 += ka
        xo_ref[:, sl] = acc
    hn = _rms(xo_ref[...], g_ref[...])
    ho_ref[...] = hn.astype(ho_ref.dtype)
    if with_router:
        aff_ref[...] = _router_affinity(hn, wr_ref, n_experts)


def matmul_residual_norm(a_list, w, x_res, gain, *, router=None, tm_pref=512):
    m, d = x_res.shape
    k = w.shape[0]
    tm = _tile(m, tm_pref)
    in_specs = [pl.BlockSpec((tm, a.shape[1]), lambda i: (i, 0)) for a in a_list]
    in_specs += [pl.BlockSpec((k, d), lambda i: (0, 0)),
                 pl.BlockSpec((tm, d), lambda i: (i, 0)),
                 pl.BlockSpec((1, d), lambda i: (0, 0))]
    args = list(a_list) + [w, x_res, gain.reshape(1, d)]
    out_specs = [pl.BlockSpec((tm, d), lambda i: (i, 0)),
                 pl.BlockSpec((tm, d), lambda i: (i, 0))]
    out_shape = [jax.ShapeDtypeStruct((m, d), F32), jax.ShapeDtypeStruct((m, d), BF16)]
    n_experts = 0
    if router is not None:
        wr, n_experts = router
        in_specs.append(pl.BlockSpec((d, 2 * LANES), lambda i: (0, 0)))
        args.append(wr)
        out_specs.append(pl.BlockSpec((tm, LANES), lambda i: (i, 0)))
        out_shape.append(jax.ShapeDtypeStruct((m, LANES), F32))
    vmem = 2 * (tm * k * 2 + k * d * 2 + tm * d * 4 + tm * d * 4 + tm * d * 2) \
        + 3 * tm * d * 4 + 4 * d * LANES * 2
    return pl.pallas_call(
        functools.partial(_mm_res_norm_kernel, n_a=len(a_list),
                          with_router=router is not None, n_experts=n_experts),
        grid=(m // tm,),
        in_specs=in_specs,
        out_specs=out_specs,
        out_shape=out_shape,
        compiler_params=_params(("parallel",), vmem),
        name="matmul_residual_norm",
    )(*args)


def _cross_attn_kernel(q_ref, k_ref, v_ref, o_ref, *, n_heads, scale):
    dh = q_ref.shape[1] // n_heads
    for h in range(n_heads):
        sl = slice(h * dh, (h + 1) * dh)
        s = lax.dot_general(q_ref[:, sl], k_ref[:, sl], (((1,), (1,)), ((), ())),
                            preferred_element_type=F32)
        e = jnp.exp((s - jnp.max(s, axis=-1, keepdims=True)) * scale)
        l = jnp.sum(e, axis=-1, keepdims=True)
        o = jnp.dot(e.astype(BF16), v_ref[:, sl], preferred_element_type=F32) / l
        o_ref[:, sl] = o.astype(o_ref.dtype)


def cross_attention(q, k, v, *, batch, seq, mem_len):
    d = q.shape[1]
    tq = _tile(seq, 512)
    nq = seq // tq
    vmem = 2 * (2 * tq * d * 2 + 2 * mem_len * d * 2) + 6 * tq * mem_len * 4 + 2 * tq * d * 4
    return pl.pallas_call(
        functools.partial(_cross_attn_kernel, n_heads=N_MEM_HEADS,
                          scale=(d // N_MEM_HEADS) ** -0.5),
        grid=(batch, nq),
        in_specs=[pl.BlockSpec((tq, d), lambda b, i: (b * nq + i, 0)),
                  pl.BlockSpec((mem_len, d), lambda b, i: (b, 0)),
                  pl.BlockSpec((mem_len, d), lambda b, i: (b, 0))],
        out_specs=pl.BlockSpec((tq, d), lambda b, i: (b * nq + i, 0)),
        out_shape=jax.ShapeDtypeStruct((batch * seq, d), BF16),
        compiler_params=_params(("parallel", "parallel"), vmem),
        name="cross_attention",
    )(q, k, v)


def _prefix_count(x, lane):
    n = x.shape[1]
    shift = 1
    while shift < n:
        x = x + jnp.where(lane >= shift, pltpu.roll(x, shift, axis=1), 0)
        shift *= 2
    return x


def _route_kernel(aff_ref, prow_ref, arow_ref, pcol_ref, *, n_experts, capacity):
    e_pad = prow_ref.shape[1]
    a_t = aff_ref[...].T[:e_pad]
    arow_ref[0] = a_t
    seq = a_t.shape[1]

    def body(i, thr_bits):
        cand = thr_bits | jnp.left_shift(jnp.int32(1), 30 - i)
        cnt = jnp.sum((a_t >= pltpu.bitcast(cand, F32)).astype(F32), axis=1, keepdims=True)
        return jnp.where(cnt >= capacity, cand, thr_bits)

    thr = pltpu.bitcast(lax.fori_loop(0, 31, body, jnp.zeros((e_pad, 1), jnp.int32)), F32)
    lane = lax.broadcasted_iota(jnp.int32, a_t.shape, 1)
    gt = a_t > thr
    eq = a_t == thr
    need = capacity - jnp.sum(gt.astype(F32), axis=1, keepdims=True).astype(jnp.int32)
    sel = gt | (eq & (_prefix_count(eq.astype(jnp.int32), lane) <= need))
    pos = jnp.where(sel, _prefix_count(sel.astype(jnp.int32), lane) - 1, -1)
    row = lax.broadcasted_iota(jnp.int32, a_t.shape, 0)
    pos = jnp.where(row < n_experts, pos, -1)
    prow_ref[0] = pos
    unused = jnp.full((LANES - e_pad, seq), -1.0, F32)
    pcol_ref[...] = jnp.concatenate([pos.astype(F32), unused], axis=0).T


def route(aff, *, batch, seq, n_experts, capacity):
    e_pad = max(8, n_experts)
    return pl.pallas_call(
        functools.partial(_route_kernel, n_experts=n_experts, capacity=capacity),
        grid=(batch,),
        in_specs=[pl.BlockSpec((seq, LANES), lambda b: (b, 0))],
        out_specs=[pl.BlockSpec((1, e_pad, seq), lambda b: (b, 0, 0)),
                   pl.BlockSpec((1, e_pad, seq), lambda b: (b, 0, 0)),
                   pl.BlockSpec((seq, LANES), lambda b: (b, 0))],
        out_shape=[jax.ShapeDtypeStruct((batch, e_pad, seq), jnp.int32),
                   jax.ShapeDtypeStruct((batch, e_pad, seq), F32),
                   jax.ShapeDtypeStruct((batch * seq, LANES), F32)],
        compiler_params=_params(("parallel",), 24 * seq * LANES * 4),
        name="route",
    )(aff)


def _gather_kernel(h_ref, prow_ref, o_ref):
    e = pl.program_id(1)
    cap = o_ref.shape[0]
    seq = h_ref.shape[0]
    prow = prow_ref[0, pl.ds(e, 1), :]
    slot = lax.broadcasted_iota(jnp.int32, (cap, seq), 0)
    onehot = (slot == prow).astype(BF16)
    o_ref[...] = jnp.dot(onehot, h_ref[...], preferred_element_type=F32).astype(o_ref.dtype)


def gather_tokens(h, prow, *, batch, seq, n_experts, capacity):
    d = h.shape[1]
    e_pad = prow.shape[1]
    vmem = 2 * (seq * d * 2 + e_pad * seq * 4 + capacity * d * 2) + capacity * seq * 8 + capacity * d * 4
    return pl.pallas_call(
        _gather_kernel,
        grid=(batch, n_experts),
        in_specs=[pl.BlockSpec((seq, d), lambda b, e: (b, 0)),
                  pl.BlockSpec((1, e_pad, seq), lambda b, e: (b, 0, 0))],
        out_specs=pl.BlockSpec((capacity, d), lambda b, e: (e * batch + b, 0)),
        out_shape=jax.ShapeDtypeStruct((n_experts * batch * capacity, d), BF16),
        compiler_params=_params(("parallel", "arbitrary"), vmem),
        name="gather_tokens",
    )(h, prow)


def _ffn_kernel(x_ref, wg_ref, wu_ref, wd_ref, prow_ref, arow_ref, o_ref, hid_ref, gate_ref,
                *, n_up, capacity):
    e = pl.program_id(0)
    s = pl.program_id(1)
    rows = x_ref.shape[0]
    tf = wg_ref.shape[1]

    @pl.when(s < n_up)
    def _():
        x = x_ref[...]
        a = jnp.dot(x, wg_ref[...].astype(BF16), preferred_element_type=F32)
        u = jnp.dot(x, wu_ref[...].astype(BF16), preferred_element_type=F32)
        hid_ref[s] = (a / (1.0 + jnp.exp(-a)) * u).astype(BF16)

    @pl.when(s == n_up)
    def _():
        seq = prow_ref.shape[2]
        slot = lax.broadcasted_iota(jnp.int32, (capacity, seq), 0)
        for b in range(rows // capacity):
            mine = slot == prow_ref[b, pl.ds(e, 1), :]
            gate_ref[b * capacity:(b + 1) * capacity, :] = jnp.sum(
                jnp.where(mine, arow_ref[b, pl.ds(e, 1), :], 0.0), axis=1, keepdims=True)

    @pl.when(s >= n_up)
    def _():
        y = jnp.dot(hid_ref[0], wd_ref[0:tf, :].astype(BF16), preferred_element_type=F32)
        for c in range(1, n_up):
            y = y + jnp.dot(hid_ref[c], wd_ref[c * tf:(c + 1) * tf, :].astype(BF16),
                            preferred_element_type=F32)
        o_ref[...] = (y * gate_ref[...]).astype(o_ref.dtype)


def expert_ffn(xs, w_gate, w_up, w_down, prow, arow, layer, *, rows_per_expert, capacity):
    _, n_experts, d, ff = w_gate.shape
    tf = _tile(ff, 512)
    tn = _tile(d, COL_CHUNK)
    n_up, n_down = ff // tf, d // tn
    r = rows_per_expert
    batch, e_pad, seq = prow.shape
    vmem = 2 * (r * d * 2 + 2 * d * tf * 4 + ff * tn * 4 + r * tn * 2 + 2 * batch * e_pad * seq * 4) \
        + r * ff * 2 + r * LANES * 4 + 2 * d * tf * 2 + ff * tn * 2 + 4 * r * tf * 4 + 2 * r * tn * 4 \
        + 2 * capacity * seq * 4
    return pl.pallas_call(
        functools.partial(_ffn_kernel, n_up=n_up, capacity=capacity),
        grid=(n_experts, n_up + n_down),
        in_specs=[pl.BlockSpec((r, d), lambda e, s: (e, 0)),
                  pl.BlockSpec((None, None, d, tf),
                               lambda e, s: (layer, e, 0, jnp.minimum(s, n_up - 1))),
                  pl.BlockSpec((None, None, d, tf),
                               lambda e, s: (layer, e, 0, jnp.minimum(s, n_up - 1))),
                  pl.BlockSpec((None, None, ff, tn),
                               lambda e, s: (layer, e, 0, jnp.maximum(s - n_up, 0))),
                  pl.BlockSpec((batch, e_pad, seq), lambda e, s: (0, 0, 0)),
                  pl.BlockSpec((batch, e_pad, seq), lambda e, s: (0, 0, 0))],
        out_specs=pl.BlockSpec((r, tn), lambda e, s: (e, jnp.maximum(s - n_up, 0))),
        out_shape=jax.ShapeDtypeStruct((n_experts * r, d), BF16),
        scratch_shapes=[pltpu.VMEM((n_up, r, tf), BF16), pltpu.VMEM((r, 1), F32)],
        compiler_params=_params(("parallel", "arbitrary"), vmem),
        name="expert_ffn",
    )(xs, w_gate, w_up, w_down, prow, arow)


def _combine_kernel(x_ref, y_ref, pcol_ref, g_ref, xo_ref, no_ref):
    g = pl.program_id(2)
    tt, d = x_ref.shape
    n_group, cap, _ = y_ref.shape

    @pl.when(g == 0)
    def _():
        xo_ref[...] = x_ref[...]

    lane = lax.broadcasted_iota(jnp.int32, (tt, LANES), 1)
    slot = lax.broadcasted_iota(jnp.int32, (tt, cap), 1).astype(F32)
    pcol = pcol_ref[...]
    hots = []
    for j in range(n_group):
        slot_of_token = jnp.sum(jnp.where(lane == g * n_group + j, pcol, 0.0),
                                axis=1, keepdims=True)
        hots.append((slot == slot_of_token).astype(BF16))
    onehot = jnp.concatenate(hots, axis=1)
    xo_ref[...] += jnp.dot(onehot, y_ref[...].reshape(n_group * cap, d),
                           preferred_element_type=F32)

    @pl.when(g == pl.num_programs(2) - 1)
    def _():
        no_ref[...] = _rms(xo_ref[...], g_ref[...]).astype(no_ref.dtype)


def combine(x_res, y, pcol, gain, *, batch, seq, n_experts, capacity, norm_dtype):
    d = x_res.shape[1]
    tt = _tile(seq, 512)
    nt = seq // tt
    ng = _tile(n_experts, 4)
    y4 = y.reshape(n_experts, batch, capacity, d)
    vmem = 2 * (tt * d * 4 + ng * capacity * d * 2 + tt * LANES * 4 + tt * d * 4 + tt * d * 4) \
        + 3 * tt * d * 4 + 2 * tt * ng * capacity * 4
    return pl.pallas_call(
        _combine_kernel,
        grid=(batch, nt, n_experts // ng),
        in_specs=[pl.BlockSpec((tt, d), lambda b, t, g: (b * nt + t, 0)),
                  pl.BlockSpec((ng, None, capacity, d), lambda b, t, g: (g, b, 0, 0)),
                  pl.BlockSpec((tt, LANES), lambda b, t, g: (b * nt + t, 0)),
                  pl.BlockSpec((1, d), lambda b, t, g: (0, 0))],
        out_specs=[pl.BlockSpec((tt, d), lambda b, t, g: (b * nt + t, 0)),
                   pl.BlockSpec((tt, d), lambda b, t, g: (b * nt + t, 0))],
        out_shape=[jax.ShapeDtypeStruct((batch * seq, d), F32),
                   jax.ShapeDtypeStruct((batch * seq, d), norm_dtype)],
        compiler_params=_params(("parallel", "parallel", "arbitrary"), vmem),
        name="combine",
    )(x_res, y4, pcol, gain.reshape(1, d))


def _pool_kernel(h_ref, o_ref, pad_ref, *, seq, windows, margin):
    g = pl.program_id(1)
    width = h_ref.shape[1]
    h = h_ref[...]
    zeros = jnp.zeros((margin, width), F32)
    pad_ref[0:margin, :] = zeros
    pad_ref[margin + seq:2 * margin + seq, :] = zeros
    pad_ref[margin:margin + seq, :] = h
    t = lax.broadcasted_iota(jnp.int32, (seq, 1), 0)
    for gi, w in enumerate(windows):
        @pl.when(g == gi)
        def _(w=w):
            acc = pad_ref[...]
            span = 1
            while span < w:
                acc = acc + pltpu.roll(acc, acc.shape[0] - span, axis=0)
                span *= 2
            total = acc[margin - w // 2:margin - w // 2 + seq, :]
            count = (jnp.minimum(t + (w - w // 2), seq) - jnp.maximum(t - w // 2, 0)).astype(F32)
            o_ref[...] = (total / count - h).astype(o_ref.dtype)


def pool_windows(hn, *, batch, seq, n_groups):
    d = hn.shape[1]
    pg = d // n_groups
    margin = 8 * ((max(POOL_WINDOWS) // 2 + 7) // 8)
    vmem = 2 * (seq * pg * 4 + seq * pg * 2) + (seq + 2 * margin) * pg * 4 + 4 * seq * pg * 4
    return pl.pallas_call(
        functools.partial(_pool_kernel, seq=seq, windows=POOL_WINDOWS, margin=margin),
        grid=(batch, n_groups),
        in_specs=[pl.BlockSpec((seq, pg), lambda b, g: (b, g))],
        out_specs=pl.BlockSpec((seq, pg), lambda b, g: (b, g)),
        out_shape=jax.ShapeDtypeStruct((batch * seq, d), BF16),
        scratch_shapes=[pltpu.VMEM((seq + 2 * margin, pg), F32)],
        compiler_params=_params(("parallel", "parallel"), vmem),
        name="pool_windows",
    )(hn)


def _pool_out_kernel(p_ref, w_ref, s_ref, x_ref, g_ref, xo_ref, ho_ref):
    ng, pg, _ = w_ref.shape
    for g in range(ng):
        sl = slice(g * pg, (g + 1) * pg)
        mixed = jnp.dot(p_ref[:, sl], w_ref[g], preferred_element_type=F32)
        xo_ref[:, sl] = mixed * s_ref[:, sl] + x_ref[:, sl]
    ho_ref[...] = _rms(xo_ref[...], g_ref[...]).astype(ho_ref.dtype)


def pool_out(pooled, w_pool, pool_scale, x_res, gain):
    m, d = x_res.shape
    ng, pg, _ = w_pool.shape
    tm = _tile(m, 512)
    vmem = 2 * (tm * d * 2 + ng * pg * pg * 2 + tm * d * 4 + tm * d * 4 + tm * d * 2) + 3 * tm * d * 4
    return pl.pallas_call(
        _pool_out_kernel,
        grid=(m // tm,),
        in_specs=[pl.BlockSpec((tm, d), lambda i: (i, 0)),
                  pl.BlockSpec((ng, pg, pg), lambda i: (0, 0, 0)),
                  pl.BlockSpec((1, d), lambda i: (0, 0)),
                  pl.BlockSpec((tm, d), lambda i: (i, 0)),
                  pl.BlockSpec((1, d), lambda i: (0, 0))],
        out_specs=[pl.BlockSpec((tm, d), lambda i: (i, 0)),
                   pl.BlockSpec((tm, d), lambda i: (i, 0))],
        out_shape=[jax.ShapeDtypeStruct((m, d), F32), jax.ShapeDtypeStruct((m, d), BF16)],
        compiler_params=_params(("parallel",), vmem),
        name="pool_out",
    )(pooled, w_pool, pool_scale.reshape(1, d), x_res, gain.reshape(1, d))


def _rope_tables(seq):
    half = HEAD_DIM // 2
    rows = seq // GRID_W
    row_idx = jnp.repeat(jnp.arange(rows), GRID_W).astype(F32)
    col_idx = jnp.tile(jnp.arange(GRID_W), rows).astype(F32)
    inv_freq = 1.0 / (ROPE_THETA ** (jnp.arange(0, half, 2, dtype=F32) / half))
    ang = jnp.concatenate([row_idx[:, None] * inv_freq[None, :],
                           col_idx[:, None] * inv_freq[None, :]], axis=-1)
    cos, sin = jnp.cos(ang), jnp.sin(ang)
    zero = jnp.zeros_like(sin)
    cos_full = jnp.stack([cos, cos], axis=-1).reshape(seq, HEAD_DIM)
    sin_even = jnp.stack([-sin, zero], axis=-1).reshape(seq, HEAD_DIM)
    sin_odd = jnp.stack([zero, sin], axis=-1).reshape(seq, HEAD_DIM)
    return cos_full, sin_even, sin_odd


def _split_router(w_router):
    d, n_experts = w_router.shape
    w = jnp.pad(w_router, ((0, 0), (0, LANES - n_experts)))
    hi = w.astype(BF16)
    lo = (w - hi.astype(F32)).astype(BF16)
    return jnp.concatenate([hi, lo], axis=1), n_experts


def _moe(x_res, h, aff, w_gate, w_up, w_down, layer, next_gain, *, batch, seq, norm_dtype):
    n_experts = w_gate.shape[1]
    capacity = EC_CAPACITY_FACTOR * seq // n_experts
    prow, arow, pcol = route(aff, batch=batch, seq=seq, n_experts=n_experts, capacity=capacity)
    xs = gather_tokens(h, prow, batch=batch, seq=seq, n_experts=n_experts, capacity=capacity)
    y = expert_ffn(xs, w_gate, w_up, w_down, prow, arow, layer,
                   rows_per_expert=batch * capacity, capacity=capacity)
    return combine(x_res, y, pcol, next_gain, batch=batch, seq=seq,
                   n_experts=n_experts, capacity=capacity, norm_dtype=norm_dtype)


def _cross_block(x, h, mem2d, layer, cross_w_q, cross_w_k, cross_w_v, cross_w_o, mem_norm,
                 ffn_norm, router_w, *, batch, seq, mem_len):
    mem_n = rms_norm_rows(mem2d, mem_norm[layer], BF16)
    k = matmul_resident(mem_n, cast_weight(cross_w_k, (layer,)))
    v = matmul_resident(mem_n, cast_weight(cross_w_v, (layer,)))
    q = matmul_resident(h, cast_weight(cross_w_q, (layer,)))
    o = cross_attention(q, k, v, batch=batch, seq=seq, mem_len=mem_len)
    return matmul_residual_norm([o], cast_weight(cross_w_o, (layer,)), x, ffn_norm[layer],
                                router=_split_router(router_w[layer]))


def kernel(x, mem, mix_norm, attn_w_in, q_gain, k_gain, fourier_w, attn_w_out, pool_w, pool_scale,
           cross_norm, mem_norm, cross_w_q, cross_w_k, cross_w_v, cross_w_o, ffn_norm, router_w,
           expert_w_gate, expert_w_up, expert_w_down, final_norm):
    batch, seq, d = x.shape
    mem_len = mem.shape[1]
    depth = mix_norm.shape[0]
    attn_width = N_Q_HEADS * HEAD_DIM
    kv_width = N_KV_HEADS * HEAD_DIM
    fourier_width = N_FOURIER_GROUPS * FOURIER_GROUP
    rope_width = attn_width + kv_width
    cos_t, sin_a, sin_b = _rope_tables(seq)

    xf = x.reshape(batch * seq, d)
    mem2d = mem.reshape(batch * mem_len, d)
    cross = functools.partial(_cross_block, mem2d=mem2d, cross_w_q=cross_w_q, cross_w_k=cross_w_k,
                              cross_w_v=cross_w_v, cross_w_o=cross_w_o, mem_norm=mem_norm,
                              ffn_norm=ffn_norm, router_w=router_w,
                              batch=batch, seq=seq, mem_len=mem_len)
    normed = None
    for layer in range(depth):
        i = layer // 2
        if layer % 2 == 0:
            gain_cols = jnp.concatenate([jnp.tile(q_gain[i], N_Q_HEADS), jnp.tile(k_gain[i], N_KV_HEADS),
                                         jnp.ones((kv_width + fourier_width,), F32)]).reshape(1, -1)
            proj = proj_in(xf, mix_norm[layer], cast_weight(attn_w_in, (i,)), gain_cols,
                           cos_t, sin_a, sin_b, seq=seq, rope_width=rope_width)
            o_attn = gqa_attention(proj, batch=batch, seq=seq, attn_width=attn_width, kv_width=kv_width)
            o_four = fourier_mixer(proj, fourier_w[i], batch=batch, seq=seq,
                                   col0=attn_width + 2 * kv_width)
            xf, h = matmul_residual_norm([o_attn, o_four], cast_weight(attn_w_out, (i,)), xf,
                                         cross_norm[layer])
        else:
            if normed is None:
                normed = rms_norm_rows(xf, mix_norm[layer], F32)
            pooled = pool_windows(normed, batch=batch, seq=seq, n_groups=len(POOL_WINDOWS))
            xf, h = pool_out(pooled, pool_w[i].astype(BF16), pool_scale[i], xf, cross_norm[layer])
        xf, h, aff = cross(xf, h, layer=layer)
        next_gain = mix_norm[layer + 1] if layer + 1 < depth else final_norm
        xf, normed = _moe(xf, h, aff, expert_w_gate, expert_w_up, expert_w_down, layer,
                          next_gain, batch=batch, seq=seq, norm_dtype=F32)
    return normed.reshape(batch, seq, d)
```

```python
import functools
import math

import jax
import jax.numpy as jnp
from jax import lax
from jax.experimental import pallas as pl
from jax.experimental.pallas import tpu as pltpu

F32 = jnp.float32
BF16 = jnp.bfloat16

GRID_W = 64
HEAD_DIM = 128
N_Q_HEADS = 12
N_KV_HEADS = 4
N_FOURIER_GROUPS = 4
FOURIER_GROUP = 128
ROPE_THETA = 10000.0
POOL_WINDOWS = (2, 4, 8, 16)
N_MEM_HEADS = 4
EC_CAPACITY_FACTOR = 2
NORM_EPS = 1e-6

LANES = 128
V7X_VMEM_BYTES = 64 * 1024 * 1024
VMEM_HEADROOM_BYTES = 6 * 1024 * 1024
COL_CHUNK = 512


def _params(semantics, vmem_bytes):
    limit = min(int(vmem_bytes) + VMEM_HEADROOM_BYTES, V7X_VMEM_BYTES - VMEM_HEADROOM_BYTES)
    return pltpu.CompilerParams(dimension_semantics=semantics, vmem_limit_bytes=limit)


def _tile(n, pref):
    t = min(n, pref)
    while n % t:
        t //= 2
    return t


def _rms(x, gain):
    ms = jnp.mean(x * x, axis=-1, keepdims=True)
    return x * lax.rsqrt(ms + NORM_EPS) * gain


def _norm_kernel(x_ref, g_ref, o_ref):
    o_ref[...] = _rms(x_ref[...], g_ref[...]).astype(o_ref.dtype)


def rms_norm_rows(x, gain, out_dtype):
    m, d = x.shape
    tm = _tile(m, 512)
    return pl.pallas_call(
        _norm_kernel,
        grid=(m // tm,),
        in_specs=[pl.BlockSpec((tm, d), lambda i: (i, 0)),
                  pl.BlockSpec((1, d), lambda i: (0, 0))],
        out_specs=pl.BlockSpec((tm, d), lambda i: (i, 0)),
        out_shape=jax.ShapeDtypeStruct((m, d), out_dtype),
        compiler_params=_params(("parallel",), 2 * tm * d * 8),
        name="rms_norm_rows",
    )(x, gain.reshape(1, d))


def _cast_kernel(w_ref, o_ref):
    o_ref[...] = w_ref[...].astype(o_ref.dtype)


def cast_weight(w, lead):
    k, n = w.shape[-2:]
    tk = _tile(k, 1024)
    squeezed = (None,) * len(lead)
    return pl.pallas_call(
        _cast_kernel,
        grid=(k // tk,),
        in_specs=[pl.BlockSpec(squeezed + (tk, n), lambda i: tuple(lead) + (i, 0))],
        out_specs=pl.BlockSpec((tk, n), lambda i: (i, 0)),
        out_shape=jax.ShapeDtypeStruct((k, n), BF16),
        compiler_params=_params(("parallel",), 2 * tk * n * 6),
        name="cast_weight",
    )(w)


def _mm_kernel(a_ref, w_ref, o_ref):
    n = o_ref.shape[1]
    tn = _tile(n, COL_CHUNK)
    a = a_ref[...]
    for c in range(n // tn):
        sl = slice(c * tn, (c + 1) * tn)
        o_ref[:, sl] = jnp.dot(a, w_ref[:, sl], preferred_element_type=F32).astype(o_ref.dtype)


def matmul_resident(a, w, out_dtype=BF16, tm_pref=512):
    m, k = a.shape
    n = w.shape[1]
    tm = _tile(m, tm_pref)
    vmem = 2 * (tm * k * 2 + k * n * 2 + tm * n * 2) + 2 * tm * COL_CHUNK * 4
    return pl.pallas_call(
        _mm_kernel,
        grid=(m // tm,),
        in_specs=[pl.BlockSpec((tm, k), lambda i: (i, 0)),
                  pl.BlockSpec((k, n), lambda i: (0, 0))],
        out_specs=pl.BlockSpec((tm, n), lambda i: (i, 0)),
        out_shape=jax.ShapeDtypeStruct((m, n), out_dtype),
        compiler_params=_params(("parallel",), vmem),
        name="matmul_resident",
    )(a, w)


def _proj_in_kernel(a_ref, ng_ref, w_ref, g_ref, cos_ref, sa_ref, sb_ref, o_ref, *, n_rope_heads):
    n = o_ref.shape[1]
    tn = _tile(n, COL_CHUNK)
    a = _rms(a_ref[...], ng_ref[...]).astype(BF16)
    cos, sin_a, sin_b = cos_ref[...], sa_ref[...], sb_ref[...]
    for c in range(n // tn):
        acc = jnp.dot(a, w_ref[:, c * tn:(c + 1) * tn], preferred_element_type=F32)
        for h in range(tn // HEAD_DIM):
            head = c * (tn // HEAD_DIM) + h
            col = slice(head * HEAD_DIM, (head + 1) * HEAD_DIM)
            xh = acc[:, h * HEAD_DIM:(h + 1) * HEAD_DIM]
            if head < n_rope_heads:
                ms = jnp.mean(xh * xh, axis=-1, keepdims=True)
                y = xh * lax.rsqrt(ms + NORM_EPS) * g_ref[:, col]
                xh = (y * cos + pltpu.roll(y, HEAD_DIM - 1, axis=1) * sin_a
                      + pltpu.roll(y, 1, axis=1) * sin_b)
            o_ref[:, col] = xh.astype(o_ref.dtype)


def proj_in(x, norm_gain, w_in, gain_cols, cos_t, sin_a, sin_b, *, seq, rope_width):
    m, k = x.shape
    n = w_in.shape[1]
    tm = _tile(seq, 512)
    n_pos_blocks = seq // tm
    vmem = 2 * (tm * k * 4 + k * n * 2 + tm * n * 2 + 3 * tm * HEAD_DIM * 4) + 6 * tm * COL_CHUNK * 4 \
        + 2 * tm * k * 4
    table = pl.BlockSpec((tm, HEAD_DIM), lambda i: (i % n_pos_blocks, 0))
    return pl.pallas_call(
        functools.partial(_proj_in_kernel, n_rope_heads=rope_width // HEAD_DIM),
        grid=(m // tm,),
        in_specs=[pl.BlockSpec((tm, k), lambda i: (i, 0)),
                  pl.BlockSpec((1, k), lambda i: (0, 0)),
                  pl.BlockSpec((k, n), lambda i: (0, 0)),
                  pl.BlockSpec((1, n), lambda i: (0, 0)),
                  table, table, table],
        out_specs=pl.BlockSpec((tm, n), lambda i: (i, 0)),
        out_shape=jax.ShapeDtypeStruct((m, n), BF16),
        compiler_params=_params(("parallel",), vmem),
        name="proj_in",
    )(x, norm_gain.reshape(1, k), w_in, gain_cols, cos_t, sin_a, sin_b)


def _attn_kernel(q_ref, k_ref, v_ref, o_ref, *, group, scale):
    k = k_ref[...]
    v = v_ref[...]
    v_ones = jnp.concatenate([v, jnp.ones_like(v)], axis=1)
    for g in range(group):
        q = q_ref[:, g * HEAD_DIM:(g + 1) * HEAD_DIM]
        s = lax.dot_general(q, k, (((1,), (1,)), ((), ())), preferred_element_type=F32)
        e = jnp.exp((s - jnp.max(s, axis=-1, keepdims=True)) * scale).astype(BF16)
        ol = jnp.dot(e, v_ones, preferred_element_type=F32)
        o = ol[:, :HEAD_DIM] / ol[:, HEAD_DIM:HEAD_DIM + 1]
        o_ref[:, g * HEAD_DIM:(g + 1) * HEAD_DIM] = o.astype(o_ref.dtype)


def gqa_attention(proj, *, batch, seq, attn_width, kv_width):
    group = N_Q_HEADS // N_KV_HEADS
    gw = group * HEAD_DIM
    tq = _tile(seq, 512)
    nq = seq // tq
    k_col0 = attn_width // HEAD_DIM
    v_col0 = (attn_width + kv_width) // HEAD_DIM
    vmem = 2 * (tq * gw * 2 * 2 + 2 * seq * HEAD_DIM * 2) + 4 * tq * seq * 4
    return pl.pallas_call(
        functools.partial(_attn_kernel, group=group, scale=HEAD_DIM ** -0.5),
        grid=(batch, N_KV_HEADS, nq),
        in_specs=[pl.BlockSpec((tq, gw), lambda b, h, i: (b * nq + i, h)),
                  pl.BlockSpec((seq, HEAD_DIM), lambda b, h, i: (b, k_col0 + h)),
                  pl.BlockSpec((seq, HEAD_DIM), lambda b, h, i: (b, v_col0 + h))],
        out_specs=pl.BlockSpec((tq, gw), lambda b, h, i: (b * nq + i, h)),
        out_shape=jax.ShapeDtypeStruct((batch * seq, attn_width), BF16),
        compiler_params=_params(("parallel", "parallel", "parallel"), vmem),
        name="gqa_attention",
    )(proj, proj, proj)


def _fourier_kernel(f_ref, cc_ref, sc_ref, cs_ref, ss_ref, wf_ref, o_ref, ab_ref, *, seq, norm):
    r = pl.program_id(1)
    ng = wf_ref.shape[0]
    cg = wf_ref.shape[1]

    @pl.when(r == 0)
    def _():
        for g in range(ng):
            fg = f_ref[:, g * cg:(g + 1) * cg]
            ab_ref[0:seq, g * cg:(g + 1) * cg] = jnp.dot(
                fg, cc_ref[...], preferred_element_type=F32).astype(BF16)
            ab_ref[seq:2 * seq, g * cg:(g + 1) * cg] = jnp.dot(
                fg, sc_ref[...], preferred_element_type=F32).astype(BF16)

    z = (jnp.dot(cs_ref[...], ab_ref[0:seq, :], preferred_element_type=F32)
         - jnp.dot(ss_ref[...], ab_ref[seq:2 * seq, :], preferred_element_type=F32)) * norm
    for g in range(ng):
        zg = z[:, g * cg:(g + 1) * cg].astype(BF16)
        o_ref[:, g * cg:(g + 1) * cg] = jnp.dot(
            zg, wf_ref[g].astype(BF16), preferred_element_type=F32).astype(o_ref.dtype)


def _dft_cos_sin(n):
    lo = _tile(n, 64)
    hi = n // lo
    j = jnp.arange(n, dtype=jnp.int32)

    def table(k):
        ang = ((k[:, None] * j[None, :]) % n).astype(F32) * (2.0 * math.pi / n)
        return jnp.cos(ang), jnp.sin(ang)

    ca, sa = table(jnp.arange(hi, dtype=jnp.int32) * lo)
    cb, sb = table(jnp.arange(lo, dtype=jnp.int32))
    cos = ca[:, None, :] * cb[None, :, :] - sa[:, None, :] * sb[None, :, :]
    sin = sa[:, None, :] * cb[None, :, :] + ca[:, None, :] * sb[None, :, :]
    return cos.reshape(n, n).astype(BF16), sin.reshape(n, n).astype(BF16)


def fourier_mixer(proj, w_fourier, *, batch, seq, col0):
    ng, cg, _ = w_fourier.shape
    fw = ng * cg
    tr = _tile(seq, 512)
    nr = seq // tr
    cc, sc = _dft_cos_sin(cg)
    cs, ss = _dft_cos_sin(seq)
    norm = 1.0 / math.sqrt(seq * cg)
    vmem = 2 * (seq * fw * 2 + 2 * cg * cg * 2 + 2 * tr * seq * 2 + ng * cg * cg * 4 + tr * fw * 2) \
        + 2 * seq * fw * 2 + 4 * tr * fw * 4
    return pl.pallas_call(
        functools.partial(_fourier_kernel, seq=seq, norm=norm),
        grid=(batch, nr),
        in_specs=[pl.BlockSpec((seq, fw), lambda b, r: (b, col0 // fw)),
                  pl.BlockSpec((cg, cg), lambda b, r: (0, 0)),
                  pl.BlockSpec((cg, cg), lambda b, r: (0, 0)),
                  pl.BlockSpec((tr, seq), lambda b, r: (r, 0)),
                  pl.BlockSpec((tr, seq), lambda b, r: (r, 0)),
                  pl.BlockSpec((ng, cg, cg), lambda b, r: (0, 0, 0))],
        out_specs=pl.BlockSpec((tr, fw), lambda b, r: (b * nr + r, 0)),
        out_shape=jax.ShapeDtypeStruct((batch * seq, fw), BF16),
        scratch_shapes=[pltpu.VMEM((2 * seq, fw), BF16)],
        compiler_params=_params(("parallel", "arbitrary"), vmem),
        name="fourier_mixer",
    )(proj, cc, sc, cs, ss, w_fourier)


def _router_affinity(hn, wr_ref, n_experts):
    hi = hn.astype(BF16)
    lo = (hn - hi.astype(F32)).astype(BF16)
    r_hi = jnp.dot(hi, wr_ref[...], preferred_element_type=F32)
    r_lo = jnp.dot(lo, wr_ref[...], preferred_element_type=F32)
    logits = r_hi[:, :LANES] + r_hi[:, LANES:] + r_lo[:, :LANES]
    lane = lax.broadcasted_iota(jnp.int32, logits.shape, 1)
    logits = jnp.where(lane < n_experts, logits, -jnp.inf)
    e = jnp.exp(logits - jnp.max(logits, axis=-1, keepdims=True))
    return e / jnp.sum(e, axis=-1, keepdims=True)


def _mm_res_norm_kernel(*refs, n_a, with_router, n_experts):
    a_refs = refs[:n_a]
    w_ref, x_ref, g_ref = refs[n_a:n_a + 3]
    pos = n_a + 3
    if with_router:
        wr_ref = refs[pos]
        pos += 1
    xo_ref, ho_ref = refs[pos:pos + 2]
    if with_router:
        aff_ref = refs[pos + 2]
    d = xo_ref.shape[1]
    tn = _tile(d, COL_CHUNK)
    for c in range(d // tn):
        sl = slice(c * tn, (c + 1) * tn)
        acc = x_ref[:, sl]
        row0 = 0
        for a_ref in a_refs:
            ka = a_ref.shape[1]
            acc = acc + jnp.dot(a_ref[...], w_ref[row0:row0 + ka, sl], preferred_element_type=F32)
            row0 += ka
        xo_ref[:, sl] = acc
    hn = _rms(xo_ref[...], g_ref[...])
    ho_ref[...] = hn.astype(ho_ref.dtype)
    if with_router:
        aff_ref[...] = _router_affinity(hn, wr_ref, n_experts)


def matmul_residual_norm(a_list, w, x_res, gain, *, router=None, tm_pref=512):
    m, d = x_res.shape
    k = w.shape[0]
    tm = _tile(m, tm_pref)
    in_specs = [pl.BlockSpec((tm, a.shape[1]), lambda i: (i, 0)) for a in a_list]
    in_specs += [pl.BlockSpec((k, d), lambda i: (0, 0)),
                 pl.BlockSpec((tm, d), lambda i: (i, 0)),
                 pl.BlockSpec((1, d), lambda i: (0, 0))]
    args = list(a_list) + [w, x_res, gain.reshape(1, d)]
    out_specs = [pl.BlockSpec((tm, d), lambda i: (i, 0)),
                 pl.BlockSpec((tm, d), lambda i: (i, 0))]
    out_shape = [jax.ShapeDtypeStruct((m, d), F32), jax.ShapeDtypeStruct((m, d), BF16)]
    n_experts = 0
    if router is not None:
        wr, n_experts = router
        in_specs.append(pl.BlockSpec((d, 2 * LANES), lambda i: (0, 0)))
        args.append(wr)
        out_specs.append(pl.BlockSpec((tm, LANES), lambda i: (i, 0)))
        out_shape.append(jax.ShapeDtypeStruct((m, LANES), F32))
    vmem = 2 * (tm * k * 2 + k * d * 2 + tm * d * 4 + tm * d * 4 + tm * d * 2) \
        + 3 * tm * d * 4 + 4 * d * LANES * 2
    return pl.pallas_call(
        functools.partial(_mm_res_norm_kernel, n_a=len(a_list),
                          with_router=router is not None, n_experts=n_experts),
        grid=(m // tm,),
        in_specs=in_specs,
        out_specs=out_specs,
        out_shape=out_shape,
        compiler_params=_params(("parallel",), vmem),
        name="matmul_residual_norm",
    )(*args)


def _cross_attn_kernel(q_ref, k_ref, v_ref, o_ref, *, n_heads, scale):
    dh = q_ref.shape[1] // n_heads
    for h in range(n_heads):
        sl = slice(h * dh, (h + 1) * dh)
        s = lax.dot_general(q_ref[:, sl], k_ref[:, sl], (((1,), (1,)), ((), ())),
                            preferred_element_type=F32)
        e = jnp.exp((s - jnp.max(s, axis=-1, keepdims=True)) * scale)
        l = jnp.sum(e, axis=-1, keepdims=True)
        o = jnp.dot(e.astype(BF16), v_ref[:, sl], preferred_element_type=F32) / l
        o_ref[:, sl] = o.astype(o_ref.dtype)


def cross_attention(q, k, v, *, batch, seq, mem_len):
    d = q.shape[1]
    tq = _tile(seq, 512)
    nq = seq // tq
    vmem = 2 * (2 * tq * d * 2 + 2 * mem_len * d * 2) + 6 * tq * mem_len * 4 + 2 * tq * d * 4
    return pl.pallas_call(
        functools.partial(_cross_attn_kernel, n_heads=N_MEM_HEADS,
                          scale=(d // N_MEM_HEADS) ** -0.5),
        grid=(batch, nq),
        in_specs=[pl.BlockSpec((tq, d), lambda b, i: (b * nq + i, 0)),
                  pl.BlockSpec((mem_len, d), lambda b, i: (b, 0)),
                  pl.BlockSpec((mem_len, d), lambda b, i: (b, 0))],
        out_specs=pl.BlockSpec((tq, d), lambda b, i: (b * nq + i, 0)),
        out_shape=jax.ShapeDtypeStruct((batch * seq, d), BF16),
        compiler_params=_params(("parallel", "parallel"), vmem),
        name="cross_attention",
    )(q, k, v)


def _prefix_count(x, lane):
    n = x.shape[1]
    shift = 1
    while shift < n:
        x = x + jnp.where(lane >= shift, pltpu.roll(x, shift, axis=1), 0)
        shift *= 2
    return x


def _route_kernel(aff_ref, prow_ref, arow_ref, pcol_ref, *, n_experts, capacity):
    e_pad = prow_ref.shape[1]
    a_t = aff_ref[...].T[:e_pad]
    arow_ref[0] = a_t
    seq = a_t.shape[1]

    def body(i, thr_bits):
        cand = thr_bits | jnp.left_shift(jnp.int32(1), 30 - i)
        cnt = jnp.sum((a_t >= pltpu.bitcast(cand, F32)).astype(F32), axis=1, keepdims=True)
        return jnp.where(cnt >= capacity, cand, thr_bits)

    thr = pltpu.bitcast(lax.fori_loop(0, 31, body, jnp.zeros((e_pad, 1), jnp.int32)), F32)
    lane = lax.broadcasted_iota(jnp.int32, a_t.shape, 1)
    gt = a_t > thr
    eq = a_t == thr
    need = capacity - jnp.sum(gt.astype(F32), axis=1, keepdims=True).astype(jnp.int32)
    sel = gt | (eq & (_prefix_count(eq.astype(jnp.int32), lane) <= need))
    pos = jnp.where(sel, _prefix_count(sel.astype(jnp.int32), lane) - 1, -1)
    row = lax.broadcasted_iota(jnp.int32, a_t.shape, 0)
    pos = jnp.where(row < n_experts, pos, -1)
    prow_ref[0] = pos
    unused = jnp.full((LANES - e_pad, seq), -1.0, F32)
    pcol_ref[...] = jnp.concatenate([pos.astype(F32), unused], axis=0).T


def route(aff, *, batch, seq, n_experts, capacity):
    e_pad = max(8, n_experts)
    return pl.pallas_call(
        functools.partial(_route_kernel, n_experts=n_experts, capacity=capacity),
        grid=(batch,),
        in_specs=[pl.BlockSpec((seq, LANES), lambda b: (b, 0))],
        out_specs=[pl.BlockSpec((1, e_pad, seq), lambda b: (b, 0, 0)),
                   pl.BlockSpec((1, e_pad, seq), lambda b: (b, 0, 0)),
                   pl.BlockSpec((seq, LANES), lambda b: (b, 0))],
        out_shape=[jax.ShapeDtypeStruct((batch, e_pad, seq), jnp.int32),
                   jax.ShapeDtypeStruct((batch, e_pad, seq), F32),
                   jax.ShapeDtypeStruct((batch * seq, LANES), F32)],
        compiler_params=_params(("parallel",), 24 * seq * LANES * 4),
        name="route",
    )(aff)


def _gather_kernel(h_ref, prow_ref, o_ref):
    e = pl.program_id(1)
    cap = o_ref.shape[0]
    seq = h_ref.shape[0]
    prow = prow_ref[0, pl.ds(e, 1), :]
    slot = lax.broadcasted_iota(jnp.int32, (cap, seq), 0)
    onehot = (slot == prow).astype(BF16)
    o_ref[...] = jnp.dot(onehot, h_ref[...], preferred_element_type=F32).astype(o_ref.dtype)


def gather_tokens(h, prow, *, batch, seq, n_experts, capacity):
    d = h.shape[1]
    e_pad = prow.shape[1]
    vmem = 2 * (seq * d * 2 + e_pad * seq * 4 + capacity * d * 2) + capacity * seq * 8 + capacity * d * 4
    return pl.pallas_call(
        _gather_kernel,
        grid=(batch, n_experts),
        in_specs=[pl.BlockSpec((seq, d), lambda b, e: (b, 0)),
                  pl.BlockSpec((1, e_pad, seq), lambda b, e: (b, 0, 0))],
        out_specs=pl.BlockSpec((capacity, d), lambda b, e: (e * batch + b, 0)),
        out_shape=jax.ShapeDtypeStruct((n_experts * batch * capacity, d), BF16),
        compiler_params=_params(("parallel", "arbitrary"), vmem),
        name="gather_tokens",
    )(h, prow)


def _ffn_kernel(x_ref, wg_ref, wu_ref, wd_ref, prow_ref, arow_ref, o_ref, acc_ref, gate_ref,
                *, capacity):
    e = pl.program_id(0)
    f = pl.program_id(1)
    last = pl.num_programs(1) - 1
    rows = x_ref.shape[0]
    x = x_ref[...]
    a = jnp.dot(x, wg_ref[...].astype(BF16), preferred_element_type=F32)
    u = jnp.dot(x, wu_ref[...].astype(BF16), preferred_element_type=F32)
    hidden = (a / (1.0 + jnp.exp(-a)) * u).astype(BF16)
    y = jnp.dot(hidden, wd_ref[...].astype(BF16), preferred_element_type=F32)

    @pl.when(f == 0)
    def _():
        acc_ref[...] = y
        seq = prow_ref.shape[2]
        slot = lax.broadcasted_iota(jnp.int32, (capacity, seq), 0)
        for b in range(rows // capacity):
            mine = slot == prow_ref[b, pl.ds(e, 1), :]
            gate_ref[b * capacity:(b + 1) * capacity, :] = jnp.sum(
                jnp.where(mine, arow_ref[b, pl.ds(e, 1), :], 0.0), axis=1, keepdims=True)

    @pl.when((f > 0) & (f < last))
    def _():
        acc_ref[...] += y

    @pl.when(f == last)
    def _():
        o_ref[...] = ((acc_ref[...] + y) * gate_ref[...]).astype(o_ref.dtype)


def expert_ffn(xs, w_gate, w_up, w_down, prow, arow, layer, *, rows_per_expert, capacity):
    _, n_experts, d, ff = w_gate.shape
    tf = _tile(ff // 2, 256)
    r = rows_per_expert
    batch, e_pad, seq = prow.shape
    vmem = 2 * (r * d * 2 + 3 * d * tf * 4 + r * d * 2 + 2 * batch * e_pad * seq * 4) + r * d * 4 \
        + r * LANES * 4 + 3 * d * tf * 2 + 4 * r * tf * 4 + r * d * 4 + 2 * capacity * seq * 4
    return pl.pallas_call(
        functools.partial(_ffn_kernel, capacity=capacity),
        grid=(n_experts, ff // tf),
        in_specs=[pl.BlockSpec((r, d), lambda e, f: (e, 0)),
                  pl.BlockSpec((None, None, d, tf), lambda e, f: (layer, e, 0, f)),
                  pl.BlockSpec((None, None, d, tf), lambda e, f: (layer, e, 0, f)),
                  pl.BlockSpec((None, None, tf, d), lambda e, f: (layer, e, f, 0)),
                  pl.BlockSpec((batch, e_pad, seq), lambda e, f: (0, 0, 0)),
                  pl.BlockSpec((batch, e_pad, seq), lambda e, f: (0, 0, 0))],
        out_specs=pl.BlockSpec((r, d), lambda e, f: (e, 0)),
        out_shape=jax.ShapeDtypeStruct((n_experts * r, d), BF16),
        scratch_shapes=[pltpu.VMEM((r, d), F32), pltpu.VMEM((r, 1), F32)],
        compiler_params=_params(("parallel", "arbitrary"), vmem),
        name="expert_ffn",
    )(xs, w_gate, w_up, w_down, prow, arow)


def _combine_kernel(x_ref, y_ref, pcol_ref, g_ref, xo_ref, no_ref):
    g = pl.program_id(2)
    tt, d = x_ref.shape
    n_group, cap, _ = y_ref.shape

    @pl.when(g == 0)
    def _():
        xo_ref[...] = x_ref[...]

    lane = lax.broadcasted_iota(jnp.int32, (tt, LANES), 1)
    slot = lax.broadcasted_iota(jnp.int32, (tt, cap), 1).astype(F32)
    pcol = pcol_ref[...]
    hots = []
    for j in range(n_group):
        slot_of_token = jnp.sum(jnp.where(lane == g * n_group + j, pcol, 0.0),
                                axis=1, keepdims=True)
        hots.append((slot == slot_of_token).astype(BF16))
    onehot = jnp.concatenate(hots, axis=1)
    xo_ref[...] += jnp.dot(onehot, y_ref[...].reshape(n_group * cap, d),
                           preferred_element_type=F32)

    @pl.when(g == pl.num_programs(2) - 1)
    def _():
        no_ref[...] = _rms(xo_ref[...], g_ref[...]).astype(no_ref.dtype)


def combine(x_res, y, pcol, gain, *, batch, seq, n_experts, capacity, norm_dtype):
    d = x_res.shape[1]
    tt = _tile(seq, 512)
    nt = seq // tt
    ng = _tile(n_experts, 4)
    y4 = y.reshape(n_experts, batch, capacity, d)
    vmem = 2 * (tt * d * 4 + ng * capacity * d * 2 + tt * LANES * 4 + tt * d * 4 + tt * d * 4) \
        + 3 * tt * d * 4 + 2 * tt * ng * capacity * 4
    return pl.pallas_call(
        _combine_kernel,
        grid=(batch, nt, n_experts // ng),
        in_specs=[pl.BlockSpec((tt, d), lambda b, t, g: (b * nt + t, 0)),
                  pl.BlockSpec((ng, None, capacity, d), lambda b, t, g: (g, b, 0, 0)),
                  pl.BlockSpec((tt, LANES), lambda b, t, g: (b * nt + t, 0)),
                  pl.BlockSpec((1, d), lambda b, t, g: (0, 0))],
        out_specs=[pl.BlockSpec((tt, d), lambda b, t, g: (b * nt + t, 0)),
                   pl.BlockSpec((tt, d), lambda b, t, g: (b * nt + t, 0))],
        out_shape=[jax.ShapeDtypeStruct((batch * seq, d), F32),
                   jax.ShapeDtypeStruct((batch * seq, d), norm_dtype)],
        compiler_params=_params(("parallel", "parallel", "arbitrary"), vmem),
        name="combine",
    )(x_res, y4, pcol, gain.reshape(1, d))


def _pool_kernel(h_ref, o_ref, pad_ref, *, seq, windows, margin):
    g = pl.program_id(1)
    width = h_ref.shape[1]
    h = h_ref[...]
    zeros = jnp.zeros((margin, width), F32)
    pad_ref[0:margin, :] = zeros
    pad_ref[margin + seq:2 * margin + seq, :] = zeros
    pad_ref[margin:margin + seq, :] = h
    t = lax.broadcasted_iota(jnp.int32, (seq, 1), 0)
    for gi, w in enumerate(windows):
        @pl.when(g == gi)
        def _(w=w):
            acc = pad_ref[...]
            span = 1
            while span < w:
                acc = acc + pltpu.roll(acc, acc.shape[0] - span, axis=0)
                span *= 2
            total = acc[margin - w // 2:margin - w // 2 + seq, :]
            count = (jnp.minimum(t + (w - w // 2), seq) - jnp.maximum(t - w // 2, 0)).astype(F32)
            o_ref[...] = (total / count - h).astype(o_ref.dtype)


def pool_windows(hn, *, batch, seq, n_groups):
    d = hn.shape[1]
    pg = d // n_groups
    margin = 8 * ((max(POOL_WINDOWS) // 2 + 7) // 8)
    vmem = 2 * (seq * pg * 4 + seq * pg * 2) + (seq + 2 * margin) * pg * 4 + 4 * seq * pg * 4
    return pl.pallas_call(
        functools.partial(_pool_kernel, seq=seq, windows=POOL_WINDOWS, margin=margin),
        grid=(batch, n_groups),
        in_specs=[pl.BlockSpec((seq, pg), lambda b, g: (b, g))],
        out_specs=pl.BlockSpec((seq, pg), lambda b, g: (b, g)),
        out_shape=jax.ShapeDtypeStruct((batch * seq, d), BF16),
        scratch_shapes=[pltpu.VMEM((seq + 2 * margin, pg), F32)],
        compiler_params=_params(("parallel", "parallel"), vmem),
        name="pool_windows",
    )(hn)


def _pool_out_kernel(p_ref, w_ref, s_ref, x_ref, g_ref, xo_ref, ho_ref):
    ng, pg, _ = w_ref.shape
    for g in range(ng):
        sl = slice(g * pg, (g + 1) * pg)
        mixed = jnp.dot(p_ref[:, sl], w_ref[g], preferred_element_type=F32)
        xo_ref[:, sl] = mixed * s_ref[:, sl] + x_ref[:, sl]
    ho_ref[...] = _rms(xo_ref[...], g_ref[...]).astype(ho_ref.dtype)


def pool_out(pooled, w_pool, pool_scale, x_res, gain):
    m, d = x_res.shape
    ng, pg, _ = w_pool.shape
    tm = _tile(m, 512)
    vmem = 2 * (tm * d * 2 + ng * pg * pg * 2 + tm * d * 4 + tm * d * 4 + tm * d * 2) + 3 * tm * d * 4
    return pl.pallas_call(
        _pool_out_kernel,
        grid=(m // tm,),
        in_specs=[pl.BlockSpec((tm, d), lambda i: (i, 0)),
                  pl.BlockSpec((ng, pg, pg), lambda i: (0, 0, 0)),
                  pl.BlockSpec((1, d), lambda i: (0, 0)),
                  pl.BlockSpec((tm, d), lambda i: (i, 0)),
                  pl.BlockSpec((1, d), lambda i: (0, 0))],
        out_specs=[pl.BlockSpec((tm, d), lambda i: (i, 0)),
                   pl.BlockSpec((tm, d), lambda i: (i, 0))],
        out_shape=[jax.ShapeDtypeStruct((m, d), F32), jax.ShapeDtypeStruct((m, d), BF16)],
        compiler_params=_params(("parallel",), vmem),
        name="pool_out",
    )(pooled, w_pool, pool_scale.reshape(1, d), x_res, gain.reshape(1, d))


def _rope_tables(seq):
    half = HEAD_DIM // 2
    rows = seq // GRID_W
    row_idx = jnp.repeat(jnp.arange(rows), GRID_W).astype(F32)
    col_idx = jnp.tile(jnp.arange(GRID_W), rows).astype(F32)
    inv_freq = 1.0 / (ROPE_THETA ** (jnp.arange(0, half, 2, dtype=F32) / half))
    ang = jnp.concatenate([row_idx[:, None] * inv_freq[None, :],
                           col_idx[:, None] * inv_freq[None, :]], axis=-1)
    cos, sin = jnp.cos(ang), jnp.sin(ang)
    zero = jnp.zeros_like(sin)
    cos_full = jnp.stack([cos, cos], axis=-1).reshape(seq, HEAD_DIM)
    sin_even = jnp.stack([-sin, zero], axis=-1).reshape(seq, HEAD_DIM)
    sin_odd = jnp.stack([zero, sin], axis=-1).reshape(seq, HEAD_DIM)
    return cos_full, sin_even, sin_odd


def _split_router(w_router):
    d, n_experts = w_router.shape
    w = jnp.pad(w_router, ((0, 0), (0, LANES - n_experts)))
    hi = w.astype(BF16)
    lo = (w - hi.astype(F32)).astype(BF16)
    return jnp.concatenate([hi, lo], axis=1), n_experts


def _moe(x_res, h, aff, w_gate, w_up, w_down, layer, next_gain, *, batch, seq, norm_dtype):
    n_experts = w_gate.shape[1]
    capacity = EC_CAPACITY_FACTOR * seq // n_experts
    prow, arow, pcol = route(aff, batch=batch, seq=seq, n_experts=n_experts, capacity=capacity)
    xs = gather_tokens(h, prow, batch=batch, seq=seq, n_experts=n_experts, capacity=capacity)
    y = expert_ffn(xs, w_gate, w_up, w_down, prow, arow, layer,
                   rows_per_expert=batch * capacity, capacity=capacity)
    return combine(x_res, y, pcol, next_gain, batch=batch, seq=seq,
                   n_experts=n_experts, capacity=capacity, norm_dtype=norm_dtype)


def _cross_block(x, h, mem2d, layer, cross_w_q, cross_w_k, cross_w_v, cross_w_o, mem_norm,
                 ffn_norm, router_w, *, batch, seq, mem_len):
    mem_n = rms_norm_rows(mem2d, mem_norm[layer], BF16)
    k = matmul_resident(mem_n, cast_weight(cross_w_k, (layer,)))
    v = matmul_resident(mem_n, cast_weight(cross_w_v, (layer,)))
    q = matmul_resident(h, cast_weight(cross_w_q, (layer,)))
    o = cross_attention(q, k, v, batch=batch, seq=seq, mem_len=mem_len)
    return matmul_residual_norm([o], cast_weight(cross_w_o, (layer,)), x, ffn_norm[layer],
                                router=_split_router(router_w[layer]))


def kernel(x, mem, mix_norm, attn_w_in, q_gain, k_gain, fourier_w, attn_w_out, pool_w, pool_scale,
           cross_norm, mem_norm, cross_w_q, cross_w_k, cross_w_v, cross_w_o, ffn_norm, router_w,
           expert_w_gate, expert_w_up, expert_w_down, final_norm):
    batch, seq, d = x.shape
    mem_len = mem.shape[1]
    depth = mix_norm.shape[0]
    attn_width = N_Q_HEADS * HEAD_DIM
    kv_width = N_KV_HEADS * HEAD_DIM
    fourier_width = N_FOURIER_GROUPS * FOURIER_GROUP
    rope_width = attn_width + kv_width
    cos_t, sin_a, sin_b = _rope_tables(seq)

    xf = x.reshape(batch * seq, d)
    mem2d = mem.reshape(batch * mem_len, d)
    cross = functools.partial(_cross_block, mem2d=mem2d, cross_w_q=cross_w_q, cross_w_k=cross_w_k,
                              cross_w_v=cross_w_v, cross_w_o=cross_w_o, mem_norm=mem_norm,
                              ffn_norm=ffn_norm, router_w=router_w,
                              batch=batch, seq=seq, mem_len=mem_len)
    normed = None
    for layer in range(depth):
        i = layer // 2
        if layer % 2 == 0:
            gain_cols = jnp.concatenate([jnp.tile(q_gain[i], N_Q_HEADS), jnp.tile(k_gain[i], N_KV_HEADS),
                                         jnp.ones((kv_width + fourier_width,), F32)]).reshape(1, -1)
            proj = proj_in(xf, mix_norm[layer], cast_weight(attn_w_in, (i,)), gain_cols,
                           cos_t, sin_a, sin_b, seq=seq, rope_width=rope_width)
            o_attn = gqa_attention(proj, batch=batch, seq=seq, attn_width=attn_width, kv_width=kv_width)
            o_four = fourier_mixer(proj, fourier_w[i], batch=batch, seq=seq,
                                   col0=attn_width + 2 * kv_width)
            xf, h = matmul_residual_norm([o_attn, o_four], cast_weight(attn_w_out, (i,)), xf,
                                         cross_norm[layer])
        else:
            if normed is None:
                normed = rms_norm_rows(xf, mix_norm[layer], F32)
            pooled = pool_windows(normed, batch=batch, seq=seq, n_groups=len(POOL_WINDOWS))
            xf, h = pool_out(pooled, pool_w[i].astype(BF16), pool_scale[i], xf, cross_norm[layer])
        xf, h, aff = cross(xf, h, layer=layer)
        next_gain = mix_norm[layer + 1] if layer + 1 < depth else final_norm
        xf, normed = _moe(xf, h, aff, expert_w_gate, expert_w_up, expert_w_down, layer,
                          next_gain, batch=batch, seq=seq, norm_dtype=F32)
    return normed.reshape(batch, seq, d)
```

```python
import functools
import math

import jax
import jax.numpy as jnp
from jax import lax
from jax.experimental import pallas as pl
from jax.experimental.pallas import tpu as pltpu

F32 = jnp.float32
BF16 = jnp.bfloat16

GRID_W = 64
HEAD_DIM = 128
N_Q_HEADS = 12
N_KV_HEADS = 4
N_FOURIER_GROUPS = 4
FOURIER_GROUP = 128
ROPE_THETA = 10000.0
POOL_WINDOWS = (2, 4, 8, 16)
N_MEM_HEADS = 4
EC_CAPACITY_FACTOR = 2
NORM_EPS = 1e-6

LANES = 128
V7X_VMEM_BYTES = 64 * 1024 * 1024
VMEM_HEADROOM_BYTES = 6 * 1024 * 1024
COL_CHUNK = 512


def _params(semantics, vmem_bytes):
    limit = min(int(vmem_bytes) + VMEM_HEADROOM_BYTES, V7X_VMEM_BYTES - VMEM_HEADROOM_BYTES)
    return pltpu.CompilerParams(dimension_semantics=semantics, vmem_limit_bytes=limit)


def _tile(n, pref):
    t = min(n, pref)
    while n % t:
        t //= 2
    return t


def _rms(x, gain):
    ms = jnp.mean(x * x, axis=-1, keepdims=True)
    return x * lax.rsqrt(ms + NORM_EPS) * gain


def _norm_kernel(x_ref, g_ref, o_ref):
    o_ref[...] = _rms(x_ref[...], g_ref[...]).astype(o_ref.dtype)


def rms_norm_rows(x, gain, out_dtype):
    m, d = x.shape
    tm = _tile(m, 512)
    return pl.pallas_call(
        _norm_kernel,
        grid=(m // tm,),
        in_specs=[pl.BlockSpec((tm, d), lambda i: (i, 0)),
                  pl.BlockSpec((1, d), lambda i: (0, 0))],
        out_specs=pl.BlockSpec((tm, d), lambda i: (i, 0)),
        out_shape=jax.ShapeDtypeStruct((m, d), out_dtype),
        compiler_params=_params(("parallel",), 2 * tm * d * 8),
        name="rms_norm_rows",
    )(x, gain.reshape(1, d))


def _cast_kernel(w_ref, o_ref):
    o_ref[...] = w_ref[...].astype(o_ref.dtype)


def cast_weight(w, lead):
    k, n = w.shape[-2:]
    tk = _tile(k, 1024)
    squeezed = (None,) * len(lead)
    return pl.pallas_call(
        _cast_kernel,
        grid=(k // tk,),
        in_specs=[pl.BlockSpec(squeezed + (tk, n), lambda i: tuple(lead) + (i, 0))],
        out_specs=pl.BlockSpec((tk, n), lambda i: (i, 0)),
        out_shape=jax.ShapeDtypeStruct((k, n), BF16),
        compiler_params=_params(("parallel",), 2 * tk * n * 6),
        name="cast_weight",
    )(w)


def _mm_kernel(a_ref, w_ref, o_ref):
    n = o_ref.shape[1]
    tn = _tile(n, COL_CHUNK)
    a = a_ref[...]
    for c in range(n // tn):
        sl = slice(c * tn, (c + 1) * tn)
        o_ref[:, sl] = jnp.dot(a, w_ref[:, sl], preferred_element_type=F32).astype(o_ref.dtype)


def matmul_resident(a, w, out_dtype=BF16, tm_pref=512):
    m, k = a.shape
    n = w.shape[1]
    tm = _tile(m, tm_pref)
    vmem = 2 * (tm * k * 2 + k * n * 2 + tm * n * 2) + 2 * tm * COL_CHUNK * 4
    return pl.pallas_call(
        _mm_kernel,
        grid=(m // tm,),
        in_specs=[pl.BlockSpec((tm, k), lambda i: (i, 0)),
                  pl.BlockSpec((k, n), lambda i: (0, 0))],
        out_specs=pl.BlockSpec((tm, n), lambda i: (i, 0)),
        out_shape=jax.ShapeDtypeStruct((m, n), out_dtype),
        compiler_params=_params(("parallel",), vmem),
        name="matmul_resident",
    )(a, w)


def _proj_in_kernel(a_ref, ng_ref, w_ref, g_ref, cos_ref, sa_ref, sb_ref, o_ref, *, n_rope_heads):
    n = o_ref.shape[1]
    tn = _tile(n, COL_CHUNK)
    a = _rms(a_ref[...], ng_ref[...]).astype(BF16)
    cos, sin_a, sin_b = cos_ref[...], sa_ref[...], sb_ref[...]
    for c in range(n // tn):
        acc = jnp.dot(a, w_ref[:, c * tn:(c + 1) * tn], preferred_element_type=F32)
        for h in range(tn // HEAD_DIM):
            head = c * (tn // HEAD_DIM) + h
            col = slice(head * HEAD_DIM, (head + 1) * HEAD_DIM)
            xh = acc[:, h * HEAD_DIM:(h + 1) * HEAD_DIM]
            if head < n_rope_heads:
                ms = jnp.mean(xh * xh, axis=-1, keepdims=True)
                y = xh * lax.rsqrt(ms + NORM_EPS) * g_ref[:, col]
                xh = (y * cos + pltpu.roll(y, HEAD_DIM - 1, axis=1) * sin_a
                      + pltpu.roll(y, 1, axis=1) * sin_b)
            o_ref[:, col] = xh.astype(o_ref.dtype)


def proj_in(x, norm_gain, w_in, gain_cols, cos_t, sin_a, sin_b, *, seq, rope_width):
    m, k = x.shape
    n = w_in.shape[1]
    tm = _tile(seq, 512)
    n_pos_blocks = seq // tm
    vmem = 2 * (tm * k * 4 + k * n * 2 + tm * n * 2 + 3 * tm * HEAD_DIM * 4) + 6 * tm * COL_CHUNK * 4 \
        + 2 * tm * k * 4
    table = pl.BlockSpec((tm, HEAD_DIM), lambda i: (i % n_pos_blocks, 0))
    return pl.pallas_call(
        functools.partial(_proj_in_kernel, n_rope_heads=rope_width // HEAD_DIM),
        grid=(m // tm,),
        in_specs=[pl.BlockSpec((tm, k), lambda i: (i, 0)),
                  pl.BlockSpec((1, k), lambda i: (0, 0)),
                  pl.BlockSpec((k, n), lambda i: (0, 0)),
                  pl.BlockSpec((1, n), lambda i: (0, 0)),
                  table, table, table],
        out_specs=pl.BlockSpec((tm, n), lambda i: (i, 0)),
        out_shape=jax.ShapeDtypeStruct((m, n), BF16),
        compiler_params=_params(("parallel",), vmem),
        name="proj_in",
    )(x, norm_gain.reshape(1, k), w_in, gain_cols, cos_t, sin_a, sin_b)


def _attn_kernel(q_ref, k_ref, v_ref, o_ref, *, group, scale):
    k = k_ref[...]
    v = v_ref[...]
    v_ones = jnp.concatenate([v, jnp.ones_like(v)], axis=1)
    for g in range(group):
        q = q_ref[:, g * HEAD_DIM:(g + 1) * HEAD_DIM]
        s = lax.dot_general(q, k, (((1,), (1,)), ((), ())), preferred_element_type=F32)
        e = jnp.exp((s - jnp.max(s, axis=-1, keepdims=True)) * scale).astype(BF16)
        ol = jnp.dot(e, v_ones, preferred_element_type=F32)
        o = ol[:, :HEAD_DIM] / ol[:, HEAD_DIM:HEAD_DIM + 1]
        o_ref[:, g * HEAD_DIM:(g + 1) * HEAD_DIM] = o.astype(o_ref.dtype)


def gqa_attention(proj, *, batch, seq, attn_width, kv_width):
    group = N_Q_HEADS // N_KV_HEADS
    gw = group * HEAD_DIM
    tq = _tile(seq, 512)
    nq = seq // tq
    k_col0 = attn_width // HEAD_DIM
    v_col0 = (attn_width + kv_width) // HEAD_DIM
    vmem = 2 * (tq * gw * 2 * 2 + 2 * seq * HEAD_DIM * 2) + 4 * tq * seq * 4
    return pl.pallas_call(
        functools.partial(_attn_kernel, group=group, scale=HEAD_DIM ** -0.5),
        grid=(batch, N_KV_HEADS, nq),
        in_specs=[pl.BlockSpec((tq, gw), lambda b, h, i: (b * nq + i, h)),
                  pl.BlockSpec((seq, HEAD_DIM), lambda b, h, i: (b, k_col0 + h)),
                  pl.BlockSpec((seq, HEAD_DIM), lambda b, h, i: (b, v_col0 + h))],
        out_specs=pl.BlockSpec((tq, gw), lambda b, h, i: (b * nq + i, h)),
        out_shape=jax.ShapeDtypeStruct((batch * seq, attn_width), BF16),
        compiler_params=_params(("parallel", "parallel", "parallel"), vmem),
        name="gqa_attention",
    )(proj, proj, proj)


def _fourier_kernel(f_ref, cc_ref, sc_ref, cs_ref, ss_ref, wf_ref, o_ref, ab_ref, *, seq, norm):
    r = pl.program_id(1)
    ng = wf_ref.shape[0]
    cg = wf_ref.shape[1]

    @pl.when(r == 0)
    def _():
        for g in range(ng):
            fg = f_ref[:, g * cg:(g + 1) * cg]
            ab_ref[0:seq, g * cg:(g + 1) * cg] = jnp.dot(
                fg, cc_ref[...], preferred_element_type=F32).astype(BF16)
            ab_ref[seq:2 * seq, g * cg:(g + 1) * cg] = jnp.dot(
                fg, sc_ref[...], preferred_element_type=F32).astype(BF16)

    z = (jnp.dot(cs_ref[...], ab_ref[0:seq, :], preferred_element_type=F32)
         - jnp.dot(ss_ref[...], ab_ref[seq:2 * seq, :], preferred_element_type=F32)) * norm
    for g in range(ng):
        zg = z[:, g * cg:(g + 1) * cg].astype(BF16)
        o_ref[:, g * cg:(g + 1) * cg] = jnp.dot(
            zg, wf_ref[g].astype(BF16), preferred_element_type=F32).astype(o_ref.dtype)


def _dft_cos_sin(n):
    lo = _tile(n, 64)
    hi = n // lo
    j = jnp.arange(n, dtype=jnp.int32)

    def table(k):
        ang = ((k[:, None] * j[None, :]) % n).astype(F32) * (2.0 * math.pi / n)
        return jnp.cos(ang), jnp.sin(ang)

    ca, sa = table(jnp.arange(hi, dtype=jnp.int32) * lo)
    cb, sb = table(jnp.arange(lo, dtype=jnp.int32))
    cos = ca[:, None, :] * cb[None, :, :] - sa[:, None, :] * sb[None, :, :]
    sin = sa[:, None, :] * cb[None, :, :] + ca[:, None, :] * sb[None, :, :]
    return cos.reshape(n, n).astype(BF16), sin.reshape(n, n).astype(BF16)


def fourier_mixer(proj, w_fourier, *, batch, seq, col0):
    ng, cg, _ = w_fourier.shape
    fw = ng * cg
    tr = _tile(seq, 512)
    nr = seq // tr
    cc, sc = _dft_cos_sin(cg)
    cs, ss = _dft_cos_sin(seq)
    norm = 1.0 / math.sqrt(seq * cg)
    vmem = 2 * (seq * fw * 2 + 2 * cg * cg * 2 + 2 * tr * seq * 2 + ng * cg * cg * 4 + tr * fw * 2) \
        + 2 * seq * fw * 2 + 4 * tr * fw * 4
    return pl.pallas_call(
        functools.partial(_fourier_kernel, seq=seq, norm=norm),
        grid=(batch, nr),
        in_specs=[pl.BlockSpec((seq, fw), lambda b, r: (b, col0 // fw)),
                  pl.BlockSpec((cg, cg), lambda b, r: (0, 0)),
                  pl.BlockSpec((cg, cg), lambda b, r: (0, 0)),
                  pl.BlockSpec((tr, seq), lambda b, r: (r, 0)),
                  pl.BlockSpec((tr, seq), lambda b, r: (r, 0)),
                  pl.BlockSpec((ng, cg, cg), lambda b, r: (0, 0, 0))],
        out_specs=pl.BlockSpec((tr, fw), lambda b, r: (b * nr + r, 0)),
        out_shape=jax.ShapeDtypeStruct((batch * seq, fw), BF16),
        scratch_shapes=[pltpu.VMEM((2 * seq, fw), BF16)],
        compiler_params=_params(("parallel", "arbitrary"), vmem),
        name="fourier_mixer",
    )(proj, cc, sc, cs, ss, w_fourier)


def _router_affinity(hn, wr_ref, n_experts):
    hi = hn.astype(BF16)
    lo = (hn - hi.astype(F32)).astype(BF16)
    r_hi = jnp.dot(hi, wr_ref[...], preferred_element_type=F32)
    r_lo = jnp.dot(lo, wr_ref[...], preferred_element_type=F32)
    logits = r_hi[:, :LANES] + r_hi[:, LANES:] + r_lo[:, :LANES]
    lane = lax.broadcasted_iota(jnp.int32, logits.shape, 1)
    logits = jnp.where(lane < n_experts, logits, -jnp.inf)
    e = jnp.exp(logits - jnp.max(logits, axis=-1, keepdims=True))
    return e / jnp.sum(e, axis=-1, keepdims=True)


def _mm_res_norm_kernel(*refs, n_a, with_router, n_experts):
    a_refs = refs[:n_a]
    w_ref, x_ref, g_ref = refs[n_a:n_a + 3]
    pos = n_a + 3
    if with_router:
        wr_ref = refs[pos]
        pos += 1
    xo_ref, ho_ref = refs[pos:pos + 2]
    if with_router:
        aff_ref = refs[pos + 2]
    d = xo_ref.shape[1]
    tn = _tile(d, COL_CHUNK)
    for c in range(d // tn):
        sl = slice(c * tn, (c + 1) * tn)
        acc = x_ref[:, sl]
        row0 = 0
        for a_ref in a_refs:
            ka = a_ref.shape[1]
            acc = acc + jnp.dot(a_ref[...], w_ref[row0:row0 + ka, sl], preferred_element_type=F32)
            row0 += ka
        xo_ref[:, sl] = acc
    hn = _rms(xo_ref[...], g_ref[...])
    ho_ref[...] = hn.astype(ho_ref.dtype)
    if with_router:
        aff_ref[...] = _router_affinity(hn, wr_ref, n_experts)


def matmul_residual_norm(a_list, w, x_res, gain, *, router=None, tm_pref=512):
    m, d = x_res.shape
    k = w.shape[0]
    tm = _tile(m, tm_pref)
    in_specs = [pl.BlockSpec((tm, a.shape[1]), lambda i: (i, 0)) for a in a_list]
    in_specs += [pl.BlockSpec((k, d), lambda i: (0, 0)),
                 pl.BlockSpec((tm, d), lambda i: (i, 0)),
                 pl.BlockSpec((1, d), lambda i: (0, 0))]
    args = list(a_list) + [w, x_res, gain.reshape(1, d)]
    out_specs = [pl.BlockSpec((tm, d), lambda i: (i, 0)),
                 pl.BlockSpec((tm, d), lambda i: (i, 0))]
    out_shape = [jax.ShapeDtypeStruct((m, d), F32), jax.ShapeDtypeStruct((m, d), BF16)]
    n_experts = 0
    if router is not None:
        wr, n_experts = router
        in_specs.append(pl.BlockSpec((d, 2 * LANES), lambda i: (0, 0)))
        args.append(wr)
        out_specs.append(pl.BlockSpec((tm, LANES), lambda i: (i, 0)))
        out_shape.append(jax.ShapeDtypeStruct((m, LANES), F32))
    vmem = 2 * (tm * k * 2 + k * d * 2 + tm * d * 4 + tm * d * 4 + tm * d * 2) \
        + 3 * tm * d * 4 + 4 * d * LANES * 2
    return pl.pallas_call(
        functools.partial(_mm_res_norm_kernel, n_a=len(a_list),
                          with_router=router is not None, n_experts=n_experts),
        grid=(m // tm,),
        in_specs=in_specs,
        out_specs=out_specs,
        out_shape=out_shape,
        compiler_params=_params(("parallel",), vmem),
        name="matmul_residual_norm",
    )(*args)


def _cross_attn_kernel(h_ref, wq_ref, k_ref, v_ref, o_ref, *, n_heads, scale):
    dh = h_ref.shape[1] // n_heads
    h = h_ref[...]
    for head in range(n_heads):
        sl = slice(head * dh, (head + 1) * dh)
        q = jnp.dot(h, wq_ref[:, sl], preferred_element_type=F32).astype(BF16)
        s = lax.dot_general(q, k_ref[:, sl], (((1,), (1,)), ((), ())),
                            preferred_element_type=F32)
        e = jnp.exp((s - jnp.max(s, axis=-1, keepdims=True)) * scale)
        l = jnp.sum(e, axis=-1, keepdims=True)
        o = jnp.dot(e.astype(BF16), v_ref[:, sl], preferred_element_type=F32) / l
        o_ref[:, sl] = o.astype(o_ref.dtype)


def cross_attention(h, w_q, k, v, *, batch, seq, mem_len):
    d = h.shape[1]
    tq = _tile(seq, 512)
    nq = seq // tq
    vmem = 2 * (2 * tq * d * 2 + d * d * 2 + 2 * mem_len * d * 2) + 6 * tq * mem_len * 4 + 3 * tq * d
    return pl.pallas_call(
        functools.partial(_cross_attn_kernel, n_heads=N_MEM_HEADS,
                          scale=(d // N_MEM_HEADS) ** -0.5),
        grid=(batch, nq),
        in_specs=[pl.BlockSpec((tq, d), lambda b, i: (b * nq + i, 0)),
                  pl.BlockSpec((d, d), lambda b, i: (0, 0)),
                  pl.BlockSpec((mem_len, d), lambda b, i: (b, 0)),
                  pl.BlockSpec((mem_len, d), lambda b, i: (b, 0))],
        out_specs=pl.BlockSpec((tq, d), lambda b, i: (b * nq + i, 0)),
        out_shape=jax.ShapeDtypeStruct((batch * seq, d), BF16),
        compiler_params=_params(("parallel", "parallel"), vmem),
        name="cross_attention",
    )(h, w_q, k, v)


def _prefix_count(x, lane):
    n = x.shape[1]
    shift = 1
    while shift < n:
        x = x + jnp.where(lane >= shift, pltpu.roll(x, shift, axis=1), 0)
        shift *= 2
    return x


def _route_kernel(aff_ref, prow_ref, arow_ref, pcol_ref, *, n_experts, capacity):
    e_pad = prow_ref.shape[1]
    a_t = aff_ref[...].T[:e_pad]
    arow_ref[0] = a_t
    seq = a_t.shape[1]

    def body(i, thr_bits):
        cand = thr_bits | jnp.left_shift(jnp.int32(1), 30 - i)
        cnt = jnp.sum((a_t >= pltpu.bitcast(cand, F32)).astype(F32), axis=1, keepdims=True)
        return jnp.where(cnt >= capacity, cand, thr_bits)

    thr = pltpu.bitcast(lax.fori_loop(0, 31, body, jnp.zeros((e_pad, 1), jnp.int32)), F32)
    lane = lax.broadcasted_iota(jnp.int32, a_t.shape, 1)
    gt = a_t > thr
    eq = a_t == thr
    need = capacity - jnp.sum(gt.astype(F32), axis=1, keepdims=True).astype(jnp.int32)
    sel = gt | (eq & (_prefix_count(eq.astype(jnp.int32), lane) <= need))
    pos = jnp.where(sel, _prefix_count(sel.astype(jnp.int32), lane) - 1, -1)
    row = lax.broadcasted_iota(jnp.int32, a_t.shape, 0)
    pos = jnp.where(row < n_experts, pos, -1)
    prow_ref[0] = pos
    unused = jnp.full((LANES - e_pad, seq), -1.0, F32)
    pcol_ref[...] = jnp.concatenate([pos.astype(F32), unused], axis=0).T


def route(aff, *, batch, seq, n_experts, capacity):
    e_pad = max(8, n_experts)
    return pl.pallas_call(
        functools.partial(_route_kernel, n_experts=n_experts, capacity=capacity),
        grid=(batch,),
        in_specs=[pl.BlockSpec((seq, LANES), lambda b: (b, 0))],
        out_specs=[pl.BlockSpec((1, e_pad, seq), lambda b: (b, 0, 0)),
                   pl.BlockSpec((1, e_pad, seq), lambda b: (b, 0, 0)),
                   pl.BlockSpec((seq, LANES), lambda b: (b, 0))],
        out_shape=[jax.ShapeDtypeStruct((batch, e_pad, seq), jnp.int32),
                   jax.ShapeDtypeStruct((batch, e_pad, seq), F32),
                   jax.ShapeDtypeStruct((batch * seq, LANES), F32)],
        compiler_params=_params(("parallel",), 24 * seq * LANES * 4),
        name="route",
    )(aff)


def _gather_kernel(h_ref, prow_ref, o_ref):
    e = pl.program_id(1)
    cap = o_ref.shape[0]
    seq = h_ref.shape[0]
    prow = prow_ref[0, pl.ds(e, 1), :]
    slot = lax.broadcasted_iota(jnp.int32, (cap, seq), 0)
    onehot = (slot == prow).astype(BF16)
    o_ref[...] = jnp.dot(onehot, h_ref[...], preferred_element_type=F32).astype(o_ref.dtype)


def gather_tokens(h, prow, *, batch, seq, n_experts, capacity):
    d = h.shape[1]
    e_pad = prow.shape[1]
    vmem = 2 * (seq * d * 2 + e_pad * seq * 4 + capacity * d * 2) + capacity * seq * 8 + capacity * d * 4
    return pl.pallas_call(
        _gather_kernel,
        grid=(batch, n_experts),
        in_specs=[pl.BlockSpec((seq, d), lambda b, e: (b, 0)),
                  pl.BlockSpec((1, e_pad, seq), lambda b, e: (b, 0, 0))],
        out_specs=pl.BlockSpec((capacity, d), lambda b, e: (e * batch + b, 0)),
        out_shape=jax.ShapeDtypeStruct((n_experts * batch * capacity, d), BF16),
        compiler_params=_params(("parallel", "arbitrary"), vmem),
        name="gather_tokens",
    )(h, prow)


def _ffn_kernel(x_ref, wg_ref, wu_ref, wd_ref, prow_ref, arow_ref, o_ref, acc_ref, gate_ref,
                *, capacity):
    e = pl.program_id(0)
    f = pl.program_id(1)
    last = pl.num_programs(1) - 1
    rows = x_ref.shape[0]
    x = x_ref[...]
    a = jnp.dot(x, wg_ref[...].astype(BF16), preferred_element_type=F32)
    u = jnp.dot(x, wu_ref[...].astype(BF16), preferred_element_type=F32)
    hidden = (a / (1.0 + jnp.exp(-a)) * u).astype(BF16)
    y = jnp.dot(hidden, wd_ref[...].astype(BF16), preferred_element_type=F32)

    @pl.when(f == 0)
    def _():
        acc_ref[...] = y
        seq = prow_ref.shape[2]
        slot = lax.broadcasted_iota(jnp.int32, (capacity, seq), 0)
        for b in range(rows // capacity):
            mine = slot == prow_ref[b, pl.ds(e, 1), :]
            gate_ref[b * capacity:(b + 1) * capacity, :] = jnp.sum(
                jnp.where(mine, arow_ref[b, pl.ds(e, 1), :], 0.0), axis=1, keepdims=True)

    @pl.when((f > 0) & (f < last))
    def _():
        acc_ref[...] += y

    @pl.when(f == last)
    def _():
        o_ref[...] = ((acc_ref[...] + y) * gate_ref[...]).astype(o_ref.dtype)


def expert_ffn(xs, w_gate, w_up, w_down, prow, arow, layer, *, rows_per_expert, capacity):
    _, n_experts, d, ff = w_gate.shape
    tf = _tile(ff // 2, 256)
    r = rows_per_expert
    batch, e_pad, seq = prow.shape
    vmem = 2 * (r * d * 2 + 3 * d * tf * 4 + r * d * 2 + 2 * batch * e_pad * seq * 4) + r * d * 4 \
        + r * LANES * 4 + 3 * d * tf * 2 + 4 * r * tf * 4 + r * d * 4 + 2 * capacity * seq * 4
    return pl.pallas_call(
        functools.partial(_ffn_kernel, capacity=capacity),
        grid=(n_experts, ff // tf),
        in_specs=[pl.BlockSpec((r, d), lambda e, f: (e, 0)),
                  pl.BlockSpec((None, None, d, tf), lambda e, f: (layer, e, 0, f)),
                  pl.BlockSpec((None, None, d, tf), lambda e, f: (layer, e, 0, f)),
                  pl.BlockSpec((None, None, tf, d), lambda e, f: (layer, e, f, 0)),
                  pl.BlockSpec((batch, e_pad, seq), lambda e, f: (0, 0, 0)),
                  pl.BlockSpec((batch, e_pad, seq), lambda e, f: (0, 0, 0))],
        out_specs=pl.BlockSpec((r, d), lambda e, f: (e, 0)),
        out_shape=jax.ShapeDtypeStruct((n_experts * r, d), BF16),
        scratch_shapes=[pltpu.VMEM((r, d), F32), pltpu.VMEM((r, 1), F32)],
        compiler_params=_params(("parallel", "arbitrary"), vmem),
        name="expert_ffn",
    )(xs, w_gate, w_up, w_down, prow, arow)


def _combine_kernel(x_ref, y_ref, pcol_ref, g_ref, out_ref, acc_ref, *, final_norm):
    g = pl.program_id(2)
    tt, d = x_ref.shape
    n_group, cap, _ = y_ref.shape

    @pl.when(g == 0)
    def _():
        acc_ref[...] = x_ref[...]

    lane = lax.broadcasted_iota(jnp.int32, (tt, LANES), 1)
    slot = lax.broadcasted_iota(jnp.int32, (tt, cap), 1).astype(F32)
    pcol = pcol_ref[...]
    hots = []
    for j in range(n_group):
        slot_of_token = jnp.sum(jnp.where(lane == g * n_group + j, pcol, 0.0),
                                axis=1, keepdims=True)
        hots.append((slot == slot_of_token).astype(BF16))
    onehot = jnp.concatenate(hots, axis=1)
    acc_ref[...] += jnp.dot(onehot, y_ref[...].reshape(n_group * cap, d),
                            preferred_element_type=F32)

    @pl.when(g == pl.num_programs(2) - 1)
    def _():
        if final_norm:
            out_ref[...] = _rms(acc_ref[...], g_ref[...])
        else:
            out_ref[...] = acc_ref[...]


def combine(x_res, y, pcol, gain, *, batch, seq, n_experts, capacity, final_norm):
    d = x_res.shape[1]
    tt = _tile(seq, 512)
    nt = seq // tt
    ng = _tile(n_experts, 4)
    y4 = y.reshape(n_experts, batch, capacity, d)
    vmem = 2 * (tt * d * 4 + ng * capacity * d * 2 + tt * LANES * 4 + tt * d * 4) \
        + 4 * tt * d * 4 + 2 * tt * ng * capacity * 4
    return pl.pallas_call(
        functools.partial(_combine_kernel, final_norm=final_norm),
        grid=(batch, nt, n_experts // ng),
        in_specs=[pl.BlockSpec((tt, d), lambda b, t, g: (b * nt + t, 0)),
                  pl.BlockSpec((ng, None, capacity, d), lambda b, t, g: (g, b, 0, 0)),
                  pl.BlockSpec((tt, LANES), lambda b, t, g: (b * nt + t, 0)),
                  pl.BlockSpec((1, d), lambda b, t, g: (0, 0))],
        out_specs=pl.BlockSpec((tt, d), lambda b, t, g: (b * nt + t, 0)),
        out_shape=jax.ShapeDtypeStruct((batch * seq, d), F32),
        scratch_shapes=[pltpu.VMEM((tt, d), F32)],
        compiler_params=_params(("parallel", "parallel", "arbitrary"), vmem),
        name="combine",
    )(x_res, y4, pcol, gain.reshape(1, d))


def _pool_kernel(x_ref, prev_ref, next_ref, ng_ref, w_ref, s_ref, g_ref, xo_ref, ho_ref, pad_ref,
                 *, seq, windows):
    i = pl.program_id(0)
    tm, d = x_ref.shape
    halo = prev_ref.shape[0]
    pg = w_ref.shape[1]
    tiles_per_seq = seq // tm
    tile_in_seq = i % tiles_per_seq
    norm_gain = ng_ref[...]
    h = _rms(x_ref[...], norm_gain)
    pad_ref[0:halo, :] = jnp.where(tile_in_seq == 0, 0.0, _rms(prev_ref[...], norm_gain))
    pad_ref[halo:halo + tm, :] = h
    pad_ref[halo + tm:2 * halo + tm, :] = jnp.where(tile_in_seq == tiles_per_seq - 1, 0.0,
                                                    _rms(next_ref[...], norm_gain))
    t = tile_in_seq * tm + lax.broadcasted_iota(jnp.int32, (tm, 1), 0)
    for gi, w in enumerate(windows):
        cols = slice(gi * pg, (gi + 1) * pg)
        acc = pad_ref[:, cols]
        span = 1
        while span < w:
            acc = acc + pltpu.roll(acc, acc.shape[0] - span, axis=0)
            span *= 2
        total = acc[halo - w // 2:halo - w // 2 + tm, :]
        count = (jnp.minimum(t + (w - w // 2), seq) - jnp.maximum(t - w // 2, 0)).astype(F32)
        pooled = (total / count - h[:, cols]).astype(BF16)
        mixed = jnp.dot(pooled, w_ref[gi], preferred_element_type=F32)
        xo_ref[:, cols] = mixed * s_ref[:, cols] + x_ref[:, cols]
    ho_ref[...] = _rms(xo_ref[...], g_ref[...]).astype(ho_ref.dtype)


def pool_mixer(x, norm_gain, w_pool, pool_scale, next_gain, *, seq):
    m, d = x.shape
    ng, pg, _ = w_pool.shape
    halo = 8
    assert all(w & (w - 1) == 0 and w // 2 <= halo for w in POOL_WINDOWS) and ng == len(POOL_WINDOWS)
    tm = _tile(seq, 256)
    per = tm // halo
    n_halo_blocks = m // halo
    vmem = 2 * (2 * tm * d * 4 + 2 * halo * d * 4 + ng * pg * pg * 2 + tm * d * 2) \
        + (tm + 2 * halo) * d * 4 + 6 * tm * d * 4
    return pl.pallas_call(
        functools.partial(_pool_kernel, seq=seq, windows=POOL_WINDOWS),
        grid=(m // tm,),
        in_specs=[pl.BlockSpec((tm, d), lambda i: (i, 0)),
                  pl.BlockSpec((halo, d), lambda i: (jnp.maximum(i * per - 1, 0), 0)),
                  pl.BlockSpec((halo, d), lambda i: (jnp.minimum((i + 1) * per, n_halo_blocks - 1), 0)),
                  pl.BlockSpec((1, d), lambda i: (0, 0)),
                  pl.BlockSpec((ng, pg, pg), lambda i: (0, 0, 0)),
                  pl.BlockSpec((1, d), lambda i: (0, 0)),
                  pl.BlockSpec((1, d), lambda i: (0, 0))],
        out_specs=[pl.BlockSpec((tm, d), lambda i: (i, 0)),
                   pl.BlockSpec((tm, d), lambda i: (i, 0))],
        out_shape=[jax.ShapeDtypeStruct((m, d), F32), jax.ShapeDtypeStruct((m, d), BF16)],
        scratch_shapes=[pltpu.VMEM((tm + 2 * halo, d), F32)],
        compiler_params=_params(("parallel",), vmem),
        name="pool_mixer",
    )(x, x, x, norm_gain.reshape(1, d), w_pool, pool_scale.reshape(1, d), next_gain.reshape(1, d))


def _rope_tables(seq):
    half = HEAD_DIM // 2
    rows = seq // GRID_W
    row_idx = jnp.repeat(jnp.arange(rows), GRID_W).astype(F32)
    col_idx = jnp.tile(jnp.arange(GRID_W), rows).astype(F32)
    inv_freq = 1.0 / (ROPE_THETA ** (jnp.arange(0, half, 2, dtype=F32) / half))
    ang = jnp.concatenate([row_idx[:, None] * inv_freq[None, :],
                           col_idx[:, None] * inv_freq[None, :]], axis=-1)
    cos, sin = jnp.cos(ang), jnp.sin(ang)
    zero = jnp.zeros_like(sin)
    cos_full = jnp.stack([cos, cos], axis=-1).reshape(seq, HEAD_DIM)
    sin_even = jnp.stack([-sin, zero], axis=-1).reshape(seq, HEAD_DIM)
    sin_odd = jnp.stack([zero, sin], axis=-1).reshape(seq, HEAD_DIM)
    return cos_full, sin_even, sin_odd


def _split_router(w_router):
    d, n_experts = w_router.shape
    w = jnp.pad(w_router, ((0, 0), (0, LANES - n_experts)))
    hi = w.astype(BF16)
    lo = (w - hi.astype(F32)).astype(BF16)
    return jnp.concatenate([hi, lo], axis=1), n_experts


def _moe(x_res, h, aff, w_gate, w_up, w_down, layer, gain, *, batch, seq, final_norm):
    n_experts = w_gate.shape[1]
    capacity = EC_CAPACITY_FACTOR * seq // n_experts
    prow, arow, pcol = route(aff, batch=batch, seq=seq, n_experts=n_experts, capacity=capacity)
    xs = gather_tokens(h, prow, batch=batch, seq=seq, n_experts=n_experts, capacity=capacity)
    y = expert_ffn(xs, w_gate, w_up, w_down, prow, arow, layer,
                   rows_per_expert=batch * capacity, capacity=capacity)
    return combine(x_res, y, pcol, gain, batch=batch, seq=seq,
                   n_experts=n_experts, capacity=capacity, final_norm=final_norm)


def _cross_block(x, h, mem2d, layer, cross_w_q, cross_w_k, cross_w_v, cross_w_o, mem_norm,
                 ffn_norm, router_w, *, batch, seq, mem_len):
    mem_n = rms_norm_rows(mem2d, mem_norm[layer], BF16)
    k = matmul_resident(mem_n, cast_weight(cross_w_k, (layer,)))
    v = matmul_resident(mem_n, cast_weight(cross_w_v, (layer,)))
    o = cross_attention(h, cast_weight(cross_w_q, (layer,)), k, v,
                        batch=batch, seq=seq, mem_len=mem_len)
    return matmul_residual_norm([o], cast_weight(cross_w_o, (layer,)), x, ffn_norm[layer],
                                router=_split_router(router_w[layer]))


def kernel(x, mem, mix_norm, attn_w_in, q_gain, k_gain, fourier_w, attn_w_out, pool_w, pool_scale,
           cross_norm, mem_norm, cross_w_q, cross_w_k, cross_w_v, cross_w_o, ffn_norm, router_w,
           expert_w_gate, expert_w_up, expert_w_down, final_norm):
    batch, seq, d = x.shape
    mem_len = mem.shape[1]
    depth = mix_norm.shape[0]
    attn_width = N_Q_HEADS * HEAD_DIM
    kv_width = N_KV_HEADS * HEAD_DIM
    fourier_width = N_FOURIER_GROUPS * FOURIER_GROUP
    rope_width = attn_width + kv_width
    cos_t, sin_a, sin_b = _rope_tables(seq)

    xf = x.reshape(batch * seq, d)
    mem2d = mem.reshape(batch * mem_len, d)
    cross = functools.partial(_cross_block, mem2d=mem2d, cross_w_q=cross_w_q, cross_w_k=cross_w_k,
                              cross_w_v=cross_w_v, cross_w_o=cross_w_o, mem_norm=mem_norm,
                              ffn_norm=ffn_norm, router_w=router_w,
                              batch=batch, seq=seq, mem_len=mem_len)
    for layer in range(depth):
        i = layer // 2
        if layer % 2 == 0:
            gain_cols = jnp.concatenate([jnp.tile(q_gain[i], N_Q_HEADS), jnp.tile(k_gain[i], N_KV_HEADS),
                                         jnp.ones((kv_width + fourier_width,), F32)]).reshape(1, -1)
            proj = proj_in(xf, mix_norm[layer], cast_weight(attn_w_in, (i,)), gain_cols,
                           cos_t, sin_a, sin_b, seq=seq, rope_width=rope_width)
            o_attn = gqa_attention(proj, batch=batch, seq=seq, attn_width=attn_width, kv_width=kv_width)
            o_four = fourier_mixer(proj, fourier_w[i], batch=batch, seq=seq,
                                   col0=attn_width + 2 * kv_width)
            xf, h = matmul_residual_norm([o_attn, o_four], cast_weight(attn_w_out, (i,)), xf,
                                         cross_norm[layer])
        else:
            xf, h = pool_mixer(xf, mix_norm[layer], pool_w[i].astype(BF16), pool_scale[i],
                               cross_norm[layer], seq=seq)
        xf, h, aff = cross(xf, h, layer=layer)
        xf = _moe(xf, h, aff, expert_w_gate, expert_w_up, expert_w_down, layer, final_norm,
                  batch=batch, seq=seq, final_norm=layer == depth - 1)
    return xf.reshape(batch, seq, d)
```

```python
import functools
import math

import jax
import jax.numpy as jnp
from jax import lax
from jax.experimental import pallas as pl
from jax.experimental.pallas import tpu as pltpu

F32 = jnp.float32
BF16 = jnp.bfloat16

GRID_W = 64
HEAD_DIM = 128
N_Q_HEADS = 12
N_KV_HEADS = 4
N_FOURIER_GROUPS = 4
FOURIER_GROUP = 128
ROPE_THETA = 10000.0
POOL_WINDOWS = (2, 4, 8, 16)
N_MEM_HEADS = 4
EC_CAPACITY_FACTOR = 2
NORM_EPS = 1e-6

LANES = 128
V7X_VMEM_BYTES = 64 * 1024 * 1024
VMEM_HEADROOM_BYTES = 6 * 1024 * 1024
COL_CHUNK = 512
LOG2_E = math.log2(math.e)


def _params(semantics, vmem_bytes):
    limit = min(int(vmem_bytes) + VMEM_HEADROOM_BYTES, V7X_VMEM_BYTES - VMEM_HEADROOM_BYTES)
    return pltpu.CompilerParams(dimension_semantics=semantics, vmem_limit_bytes=limit)


def _tile(n, pref):
    t = min(n, pref)
    while n % t:
        t //= 2
    return t


def _rms(x, gain):
    ms = jnp.mean(x * x, axis=-1, keepdims=True)
    return x * lax.rsqrt(ms + NORM_EPS) * gain


def _softmax_numerators(scores, scale):
    c = scale * LOG2_E
    return [jnp.exp2((s - jnp.max(s, axis=-1, keepdims=True)) * c) for s in scores]


def _norm_kernel(x_ref, g_ref, o_ref):
    o_ref[...] = _rms(x_ref[...], g_ref[...]).astype(o_ref.dtype)


def rms_norm_rows(x, gain, out_dtype):
    m, d = x.shape
    tm = _tile(m, 512)
    return pl.pallas_call(
        _norm_kernel,
        grid=(m // tm,),
        in_specs=[pl.BlockSpec((tm, d), lambda i: (i, 0)),
                  pl.BlockSpec((1, d), lambda i: (0, 0))],
        out_specs=pl.BlockSpec((tm, d), lambda i: (i, 0)),
        out_shape=jax.ShapeDtypeStruct((m, d), out_dtype),
        compiler_params=_params(("parallel",), 2 * tm * d * 8),
        name="rms_norm_rows",
    )(x, gain.reshape(1, d))


def _cast_kernel(w_ref, o_ref):
    o_ref[...] = w_ref[...].astype(o_ref.dtype)


def cast_weight(w, lead):
    k, n = w.shape[-2:]
    tk = _tile(k, 1024)
    squeezed = (None,) * len(lead)
    return pl.pallas_call(
        _cast_kernel,
        grid=(k // tk,),
        in_specs=[pl.BlockSpec(squeezed + (tk, n), lambda i: tuple(lead) + (i, 0))],
        out_specs=pl.BlockSpec((tk, n), lambda i: (i, 0)),
        out_shape=jax.ShapeDtypeStruct((k, n), BF16),
        compiler_params=_params(("parallel",), 2 * tk * n * 6),
        name="cast_weight",
    )(w)


def _mm_stream_kernel(a_ref, w_ref, o_ref):
    o_ref[...] = jnp.dot(a_ref[...], w_ref[...].astype(BF16),
                         preferred_element_type=F32).astype(o_ref.dtype)


def matmul_streamed(a, w, lead, out_dtype=BF16):
    m, k = a.shape
    n = w.shape[-1]
    tn = _tile(n, COL_CHUNK)
    squeezed = (None,) * len(lead)
    vmem = 2 * (m * k * 2 + k * tn * 4 + m * tn * 2) + k * tn * 2 + m * tn * 4
    return pl.pallas_call(
        _mm_stream_kernel,
        grid=(n // tn,),
        in_specs=[pl.BlockSpec((m, k), lambda j: (0, 0)),
                  pl.BlockSpec(squeezed + (k, tn), lambda j: tuple(lead) + (0, j))],
        out_specs=pl.BlockSpec((m, tn), lambda j: (0, j)),
        out_shape=jax.ShapeDtypeStruct((m, n), out_dtype),
        compiler_params=_params(("parallel",), vmem),
        name="matmul_streamed",
    )(a, w)


def _proj_in_kernel(a_ref, ng_ref, w_ref, g_ref, cos_ref, sa_ref, sb_ref, o_ref, *, n_rope_heads):
    n = o_ref.shape[1]
    tn = _tile(n, COL_CHUNK)
    a = _rms(a_ref[...], ng_ref[...]).astype(BF16)
    cos, sin_a, sin_b = cos_ref[...], sa_ref[...], sb_ref[...]
    for c in range(n // tn):
        acc = jnp.dot(a, w_ref[:, c * tn:(c + 1) * tn], preferred_element_type=F32)
        for h in range(tn // HEAD_DIM):
            head = c * (tn // HEAD_DIM) + h
            col = slice(head * HEAD_DIM, (head + 1) * HEAD_DIM)
            xh = acc[:, h * HEAD_DIM:(h + 1) * HEAD_DIM]
            if head < n_rope_heads:
                ms = jnp.mean(xh * xh, axis=-1, keepdims=True)
                y = xh * lax.rsqrt(ms + NORM_EPS) * g_ref[:, col]
                xh = (y * cos + pltpu.roll(y, HEAD_DIM - 1, axis=1) * sin_a
                      + pltpu.roll(y, 1, axis=1) * sin_b)
            o_ref[:, col] = xh.astype(o_ref.dtype)


def proj_in(x, norm_gain, w_in, gain_cols, cos_t, sin_a, sin_b, *, seq, rope_width):
    m, k = x.shape
    n = w_in.shape[1]
    tm = _tile(seq, 512)
    n_pos_blocks = seq // tm
    vmem = 2 * (tm * k * 4 + k * n * 2 + tm * n * 2 + 3 * tm * HEAD_DIM * 4) + 6 * tm * COL_CHUNK * 4 \
        + 2 * tm * k * 4
    table = pl.BlockSpec((tm, HEAD_DIM), lambda i: (i % n_pos_blocks, 0))
    return pl.pallas_call(
        functools.partial(_proj_in_kernel, n_rope_heads=rope_width // HEAD_DIM),
        grid=(m // tm,),
        in_specs=[pl.BlockSpec((tm, k), lambda i: (i, 0)),
                  pl.BlockSpec((1, k), lambda i: (0, 0)),
                  pl.BlockSpec((k, n), lambda i: (0, 0)),
                  pl.BlockSpec((1, n), lambda i: (0, 0)),
                  table, table, table],
        out_specs=pl.BlockSpec((tm, n), lambda i: (i, 0)),
        out_shape=jax.ShapeDtypeStruct((m, n), BF16),
        compiler_params=_params(("parallel",), vmem),
        name="proj_in",
    )(x, norm_gain.reshape(1, k), w_in, gain_cols, cos_t, sin_a, sin_b)


def _attn_kernel(q_ref, k_ref, v_ref, o_ref, *, group, scale):
    k = k_ref[...]
    v = v_ref[...]
    v_ones = jnp.concatenate([v, jnp.ones_like(v)], axis=1)
    heads = [slice(g * HEAD_DIM, (g + 1) * HEAD_DIM) for g in range(group)]
    scores = [lax.dot_general(q_ref[:, sl], k, (((1,), (1,)), ((), ())),
                              preferred_element_type=F32) for sl in heads]
    for sl, e in zip(heads, _softmax_numerators(scores, scale)):
        ol = jnp.dot(e.astype(BF16), v_ones, preferred_element_type=F32)
        o_ref[:, sl] = (ol[:, :HEAD_DIM] / ol[:, HEAD_DIM:HEAD_DIM + 1]).astype(o_ref.dtype)


def gqa_attention(proj, *, batch, seq, attn_width, kv_width):
    group = N_Q_HEADS // N_KV_HEADS
    gw = group * HEAD_DIM
    tq = _tile(seq, 1024)
    nq = seq // tq
    k_col0 = attn_width // HEAD_DIM
    v_col0 = (attn_width + kv_width) // HEAD_DIM
    vmem = 2 * (tq * gw * 2 * 2 + 2 * seq * HEAD_DIM * 2) + group * tq * seq * 6
    return pl.pallas_call(
        functools.partial(_attn_kernel, group=group, scale=HEAD_DIM ** -0.5),
        grid=(batch, N_KV_HEADS, nq),
        in_specs=[pl.BlockSpec((tq, gw), lambda b, h, i: (b * nq + i, h)),
                  pl.BlockSpec((seq, HEAD_DIM), lambda b, h, i: (b, k_col0 + h)),
                  pl.BlockSpec((seq, HEAD_DIM), lambda b, h, i: (b, v_col0 + h))],
        out_specs=pl.BlockSpec((tq, gw), lambda b, h, i: (b * nq + i, h)),
        out_shape=jax.ShapeDtypeStruct((batch * seq, attn_width), BF16),
        compiler_params=_params(("parallel", "parallel", "parallel"), vmem),
        name="gqa_attention",
    )(proj, proj, proj)


def _fourier_kernel(f_ref, cc_ref, sc_ref, cs_ref, ss_ref, wf_ref, o_ref, ab_ref, *, seq, norm):
    r = pl.program_id(1)
    ng = wf_ref.shape[0]
    cg = wf_ref.shape[1]

    @pl.when(r == 0)
    def _():
        for g in range(ng):
            fg = f_ref[:, g * cg:(g + 1) * cg]
            ab_ref[0:seq, g * cg:(g + 1) * cg] = jnp.dot(
                fg, cc_ref[...], preferred_element_type=F32).astype(BF16)
            ab_ref[seq:2 * seq, g * cg:(g + 1) * cg] = jnp.dot(
                fg, sc_ref[...], preferred_element_type=F32).astype(BF16)

    z = (jnp.dot(cs_ref[...], ab_ref[0:seq, :], preferred_element_type=F32)
         - jnp.dot(ss_ref[...], ab_ref[seq:2 * seq, :], preferred_element_type=F32)) * norm
    for g in range(ng):
        zg = z[:, g * cg:(g + 1) * cg].astype(BF16)
        o_ref[:, g * cg:(g + 1) * cg] = jnp.dot(
            zg, wf_ref[g].astype(BF16), preferred_element_type=F32).astype(o_ref.dtype)


def _dft_cos_sin(n):
    lo = _tile(n, 64)
    hi = n // lo
    j = jnp.arange(n, dtype=jnp.int32)

    def table(k):
        ang = ((k[:, None] * j[None, :]) % n).astype(F32) * (2.0 * math.pi / n)
        return jnp.cos(ang), jnp.sin(ang)

    ca, sa = table(jnp.arange(hi, dtype=jnp.int32) * lo)
    cb, sb = table(jnp.arange(lo, dtype=jnp.int32))
    cos = ca[:, None, :] * cb[None, :, :] - sa[:, None, :] * sb[None, :, :]
    sin = sa[:, None, :] * cb[None, :, :] + ca[:, None, :] * sb[None, :, :]
    return cos.reshape(n, n).astype(BF16), sin.reshape(n, n).astype(BF16)


def fourier_mixer(proj, w_fourier, *, batch, seq, col0):
    ng, cg, _ = w_fourier.shape
    fw = ng * cg
    tr = _tile(seq, 512)
    nr = seq // tr
    cc, sc = _dft_cos_sin(cg)
    cs, ss = _dft_cos_sin(seq)
    norm = 1.0 / math.sqrt(seq * cg)
    vmem = 2 * (seq * fw * 2 + 2 * cg * cg * 2 + 2 * tr * seq * 2 + ng * cg * cg * 4 + tr * fw * 2) \
        + 2 * seq * fw * 2 + 4 * tr * fw * 4
    return pl.pallas_call(
        functools.partial(_fourier_kernel, seq=seq, norm=norm),
        grid=(batch, nr),
        in_specs=[pl.BlockSpec((seq, fw), lambda b, r: (b, col0 // fw)),
                  pl.BlockSpec((cg, cg), lambda b, r: (0, 0)),
                  pl.BlockSpec((cg, cg), lambda b, r: (0, 0)),
                  pl.BlockSpec((tr, seq), lambda b, r: (r, 0)),
                  pl.BlockSpec((tr, seq), lambda b, r: (r, 0)),
                  pl.BlockSpec((ng, cg, cg), lambda b, r: (0, 0, 0))],
        out_specs=pl.BlockSpec((tr, fw), lambda b, r: (b * nr + r, 0)),
        out_shape=jax.ShapeDtypeStruct((batch * seq, fw), BF16),
        scratch_shapes=[pltpu.VMEM((2 * seq, fw), BF16)],
        compiler_params=_params(("parallel", "arbitrary"), vmem),
        name="fourier_mixer",
    )(proj, cc, sc, cs, ss, w_fourier)


def _router_affinity(hn, wr_ref, n_experts):
    hi = hn.astype(BF16)
    lo = (hn - hi.astype(F32)).astype(BF16)
    r_hi = jnp.dot(hi, wr_ref[...], preferred_element_type=F32)
    r_lo = jnp.dot(lo, wr_ref[...], preferred_element_type=F32)
    logits = r_hi[:, :LANES] + r_hi[:, LANES:] + r_lo[:, :LANES]
    lane = lax.broadcasted_iota(jnp.int32, logits.shape, 1)
    logits = jnp.where(lane < n_experts, logits, -jnp.inf)
    e = jnp.exp(logits - jnp.max(logits, axis=-1, keepdims=True))
    return e / jnp.sum(e, axis=-1, keepdims=True)


def _mm_res_norm_kernel(*refs, n_a, with_router, n_experts):
    a_refs = refs[:n_a]
    w_ref, x_ref, g_ref = refs[n_a:n_a + 3]
    pos = n_a + 3
    if with_router:
        wr_ref = refs[pos]
        pos += 1
    xo_ref, ho_ref = refs[pos:pos + 2]
    if with_router:
        aff_ref = refs[pos + 2]
    d = xo_ref.shape[1]
    tn = _tile(d, COL_CHUNK)
    for c in range(d // tn):
        sl = slice(c * tn, (c + 1) * tn)
        acc = x_ref[:, sl]
        row0 = 0
        for a_ref in a_refs:
            ka = a_ref.shape[1]
            acc = acc + jnp.dot(a_ref[...], w_ref[row0:row0 + ka, sl], preferred_element_type=F32)
            row0 += ka
        xo_ref[:, sl] = acc
    hn = _rms(xo_ref[...], g_ref[...])
    ho_ref[...] = hn.astype(ho_ref.dtype)
    if with_router:
        aff_ref[...] = _router_affinity(hn, wr_ref, n_experts)


def matmul_residual_norm(a_list, w, x_res, gain, *, router=None, tm_pref=512):
    m, d = x_res.shape
    k = w.shape[0]
    tm = _tile(m, tm_pref)
    in_specs = [pl.BlockSpec((tm, a.shape[1]), lambda i: (i, 0)) for a in a_list]
    in_specs += [pl.BlockSpec((k, d), lambda i: (0, 0)),
                 pl.BlockSpec((tm, d), lambda i: (i, 0)),
                 pl.BlockSpec((1, d), lambda i: (0, 0))]
    args = list(a_list) + [w, x_res, gain.reshape(1, d)]
    out_specs = [pl.BlockSpec((tm, d), lambda i: (i, 0)),
                 pl.BlockSpec((tm, d), lambda i: (i, 0))]
    out_shape = [jax.ShapeDtypeStruct((m, d), F32), jax.ShapeDtypeStruct((m, d), BF16)]
    n_experts = 0
    if router is not None:
        wr, n_experts = router
        in_specs.append(pl.BlockSpec((d, 2 * LANES), lambda i: (0, 0)))
        args.append(wr)
        out_specs.append(pl.BlockSpec((tm, LANES), lambda i: (i, 0)))
        out_shape.append(jax.ShapeDtypeStruct((m, LANES), F32))
    vmem = 2 * (tm * k * 2 + k * d * 2 + tm * d * 4 + tm * d * 4 + tm * d * 2) \
        + 3 * tm * d * 4 + 4 * d * LANES * 2
    return pl.pallas_call(
        functools.partial(_mm_res_norm_kernel, n_a=len(a_list),
                          with_router=router is not None, n_experts=n_experts),
        grid=(m // tm,),
        in_specs=in_specs,
        out_specs=out_specs,
        out_shape=out_shape,
        compiler_params=_params(("parallel",), vmem),
        name="matmul_residual_norm",
    )(*args)


def _cross_attn_kernel(h_ref, wq_ref, k_ref, v_ref, o_ref, *, n_heads, scale):
    dh = h_ref.shape[1] // n_heads
    h = h_ref[...]
    heads = [slice(i * dh, (i + 1) * dh) for i in range(n_heads)]
    qs = [jnp.dot(h, wq_ref[:, sl], preferred_element_type=F32).astype(BF16) for sl in heads]
    scores = [lax.dot_general(q, k_ref[:, sl], (((1,), (1,)), ((), ())),
                              preferred_element_type=F32) for q, sl in zip(qs, heads)]
    for sl, e in zip(heads, _softmax_numerators(scores, scale)):
        l = jnp.sum(e, axis=-1, keepdims=True)
        o = jnp.dot(e.astype(BF16), v_ref[:, sl], preferred_element_type=F32) / l
        o_ref[:, sl] = o.astype(o_ref.dtype)


def cross_attention(h, w_q, k, v, *, batch, seq, mem_len):
    d = h.shape[1]
    tq = _tile(seq, 512)
    nq = seq // tq
    vmem = 2 * (2 * tq * d * 2 + d * d * 2 + 2 * mem_len * d * 2) + 6 * tq * mem_len * 4 + 3 * tq * d * 4
    return pl.pallas_call(
        functools.partial(_cross_attn_kernel, n_heads=N_MEM_HEADS,
                          scale=(d // N_MEM_HEADS) ** -0.5),
        grid=(batch, nq),
        in_specs=[pl.BlockSpec((tq, d), lambda b, i: (b * nq + i, 0)),
                  pl.BlockSpec((d, d), lambda b, i: (0, 0)),
                  pl.BlockSpec((mem_len, d), lambda b, i: (b, 0)),
                  pl.BlockSpec((mem_len, d), lambda b, i: (b, 0))],
        out_specs=pl.BlockSpec((tq, d), lambda b, i: (b * nq + i, 0)),
        out_shape=jax.ShapeDtypeStruct((batch * seq, d), BF16),
        compiler_params=_params(("parallel", "parallel"), vmem),
        name="cross_attention",
    )(h, w_q, k, v)


def _prefix_count(x, lane):
    n = x.shape[1]
    shift = 1
    while shift < n:
        x = x + jnp.where(lane >= shift, pltpu.roll(x, shift, axis=1), 0)
        shift *= 2
    return x


def _route_kernel(aff_ref, prow_ref, arow_ref, pcol_ref, *, n_experts, capacity):
    e_pad = prow_ref.shape[1]
    a_t = aff_ref[...].T[:e_pad]
    arow_ref[0] = a_t
    seq = a_t.shape[1]

    def body(i, thr_bits):
        cand = thr_bits | jnp.left_shift(jnp.int32(1), 30 - i)
        cnt = jnp.sum((a_t >= pltpu.bitcast(cand, F32)).astype(F32), axis=1, keepdims=True)
        return jnp.where(cnt >= capacity, cand, thr_bits)

    thr = pltpu.bitcast(lax.fori_loop(0, 31, body, jnp.zeros((e_pad, 1), jnp.int32)), F32)
    lane = lax.broadcasted_iota(jnp.int32, a_t.shape, 1)
    gt = a_t > thr
    eq = a_t == thr
    need = capacity - jnp.sum(gt.astype(F32), axis=1, keepdims=True).astype(jnp.int32)
    sel = gt | (eq & (_prefix_count(eq.astype(jnp.int32), lane) <= need))
    pos = jnp.where(sel, _prefix_count(sel.astype(jnp.int32), lane) - 1, -1)
    row = lax.broadcasted_iota(jnp.int32, a_t.shape, 0)
    pos = jnp.where(row < n_experts, pos, -1)
    prow_ref[0] = pos
    unused = jnp.full((LANES - e_pad, seq), -1.0, F32)
    pcol_ref[...] = jnp.concatenate([pos.astype(F32), unused], axis=0).T


def route(aff, *, batch, seq, n_experts, capacity):
    e_pad = max(8, n_experts)
    return pl.pallas_call(
        functools.partial(_route_kernel, n_experts=n_experts, capacity=capacity),
        grid=(batch,),
        in_specs=[pl.BlockSpec((seq, LANES), lambda b: (b, 0))],
        out_specs=[pl.BlockSpec((1, e_pad, seq), lambda b: (b, 0, 0)),
                   pl.BlockSpec((1, e_pad, seq), lambda b: (b, 0, 0)),
                   pl.BlockSpec((seq, LANES), lambda b: (b, 0))],
        out_shape=[jax.ShapeDtypeStruct((batch, e_pad, seq), jnp.int32),
                   jax.ShapeDtypeStruct((batch, e_pad, seq), F32),
                   jax.ShapeDtypeStruct((batch * seq, LANES), F32)],
        compiler_params=_params(("parallel",), 24 * seq * LANES * 4),
        name="route",
    )(aff)


def _gather_kernel(h_ref, prow_ref, o_ref, *, experts_per_step):
    g = pl.program_id(1)
    cap = o_ref.shape[1]
    seq = h_ref.shape[0]
    slot = lax.broadcasted_iota(jnp.int32, (cap, seq), 0)
    hots = [(slot == prow_ref[0, pl.ds(g * experts_per_step + j, 1), :]).astype(BF16)
            for j in range(experts_per_step)]
    onehot = jnp.concatenate(hots, axis=0)
    rows = jnp.dot(onehot, h_ref[...], preferred_element_type=F32).astype(o_ref.dtype)
    for j in range(experts_per_step):
        o_ref[j] = rows[j * cap:(j + 1) * cap]


def gather_tokens(h, prow, *, batch, seq, n_experts, capacity):
    d = h.shape[1]
    e_pad = prow.shape[1]
    ng = _tile(n_experts, 2)
    vmem = 2 * (seq * d * 2 + e_pad * seq * 4 + ng * capacity * d * 2) + ng * capacity * seq * 8 \
        + ng * capacity * d * 4
    return pl.pallas_call(
        functools.partial(_gather_kernel, experts_per_step=ng),
        grid=(batch, n_experts // ng),
        in_specs=[pl.BlockSpec((seq, d), lambda b, g: (b, 0)),
                  pl.BlockSpec((1, e_pad, seq), lambda b, g: (b, 0, 0))],
        out_specs=pl.BlockSpec((ng, None, capacity, d), lambda b, g: (g, b, 0, 0)),
        out_shape=jax.ShapeDtypeStruct((n_experts, batch, capacity, d), BF16),
        compiler_params=_params(("parallel", "arbitrary"), vmem),
        name="gather_tokens",
    )(h, prow)


def _ffn_kernel(x_ref, wg_ref, wu_ref, wd_ref, prow_ref, arow_ref, o_ref, acc_ref, gate_ref,
                *, capacity):
    e = pl.program_id(0)
    f = pl.program_id(1)
    last = pl.num_programs(1) - 1
    rows = x_ref.shape[0]
    x = x_ref[...]
    a = jnp.dot(x, wg_ref[...].astype(BF16), preferred_element_type=F32)
    u = jnp.dot(x, wu_ref[...].astype(BF16), preferred_element_type=F32)
    hidden = (a / (1.0 + jnp.exp(-a)) * u).astype(BF16)
    y = jnp.dot(hidden, wd_ref[...].astype(BF16), preferred_element_type=F32)

    @pl.when(f == 0)
    def _():
        acc_ref[...] = y
        seq = prow_ref.shape[2]
        slot = lax.broadcasted_iota(jnp.int32, (capacity, seq), 0)
        for b in range(rows // capacity):
            mine = slot == prow_ref[b, pl.ds(e, 1), :]
            gate_ref[b * capacity:(b + 1) * capacity, :] = jnp.sum(
                jnp.where(mine, arow_ref[b, pl.ds(e, 1), :], 0.0), axis=1, keepdims=True)

    @pl.when((f > 0) & (f < last))
    def _():
        acc_ref[...] += y

    @pl.when(f == last)
    def _():
        o_ref[...] = ((acc_ref[...] + y) * gate_ref[...]).astype(o_ref.dtype)


def expert_ffn(xs, w_gate, w_up, w_down, prow, arow, layer, *, rows_per_expert, capacity):
    _, n_experts, d, ff = w_gate.shape
    tf = _tile(ff // 2, 256)
    r = rows_per_expert
    batch, e_pad, seq = prow.shape
    vmem = 2 * (r * d * 2 + 3 * d * tf * 4 + r * d * 2 + 2 * batch * e_pad * seq * 4) + r * d * 4 \
        + r * LANES * 4 + 3 * d * tf * 2 + 4 * r * tf * 4 + r * d * 4 + 2 * capacity * seq * 4
    return pl.pallas_call(
        functools.partial(_ffn_kernel, capacity=capacity),
        grid=(n_experts, ff // tf),
        in_specs=[pl.BlockSpec((r, d), lambda e, f: (e, 0)),
                  pl.BlockSpec((None, None, d, tf), lambda e, f: (layer, e, 0, f)),
                  pl.BlockSpec((None, None, d, tf), lambda e, f: (layer, e, 0, f)),
                  pl.BlockSpec((None, None, tf, d), lambda e, f: (layer, e, f, 0)),
                  pl.BlockSpec((batch, e_pad, seq), lambda e, f: (0, 0, 0)),
                  pl.BlockSpec((batch, e_pad, seq), lambda e, f: (0, 0, 0))],
        out_specs=pl.BlockSpec((r, d), lambda e, f: (e, 0)),
        out_shape=jax.ShapeDtypeStruct((n_experts * r, d), BF16),
        scratch_shapes=[pltpu.VMEM((r, d), F32), pltpu.VMEM((r, 1), F32)],
        compiler_params=_params(("parallel", "arbitrary"), vmem),
        name="expert_ffn",
    )(xs, w_gate, w_up, w_down, prow, arow)


def _combine_kernel(x_ref, y_ref, pcol_ref, g_ref, out_ref, acc_ref, *, final_norm):
    g = pl.program_id(2)
    tt, d = x_ref.shape
    n_group, cap, _ = y_ref.shape

    @pl.when(g == 0)
    def _():
        acc_ref[...] = x_ref[...]

    lane = lax.broadcasted_iota(jnp.int32, (tt, LANES), 1)
    slot = lax.broadcasted_iota(jnp.int32, (tt, cap), 1).astype(F32)
    pcol = pcol_ref[...]
    hots = []
    for j in range(n_group):
        slot_of_token = jnp.sum(jnp.where(lane == g * n_group + j, pcol, 0.0),
                                axis=1, keepdims=True)
        hots.append((slot == slot_of_token).astype(BF16))
    onehot = jnp.concatenate(hots, axis=1)
    acc_ref[...] += jnp.dot(onehot, y_ref[...].reshape(n_group * cap, d),
                            preferred_element_type=F32)

    @pl.when(g == pl.num_programs(2) - 1)
    def _():
        if final_norm:
            out_ref[...] = _rms(acc_ref[...], g_ref[...])
        else:
            out_ref[...] = acc_ref[...]


def combine(x_res, y, pcol, gain, *, batch, seq, n_experts, capacity, final_norm):
    d = x_res.shape[1]
    tt = _tile(seq, 512)
    nt = seq // tt
    ng = _tile(n_experts, 4)
    y4 = y.reshape(n_experts, batch, capacity, d)
    vmem = 2 * (tt * d * 4 + ng * capacity * d * 2 + tt * LANES * 4 + tt * d * 4) \
        + 4 * tt * d * 4 + 2 * tt * ng * capacity * 4
    return pl.pallas_call(
        functools.partial(_combine_kernel, final_norm=final_norm),
        grid=(batch, nt, n_experts // ng),
        in_specs=[pl.BlockSpec((tt, d), lambda b, t, g: (b * nt + t, 0)),
                  pl.BlockSpec((ng, None, capacity, d), lambda b, t, g: (g, b, 0, 0)),
                  pl.BlockSpec((tt, LANES), lambda b, t, g: (b * nt + t, 0)),
                  pl.BlockSpec((1, d), lambda b, t, g: (0, 0))],
        out_specs=pl.BlockSpec((tt, d), lambda b, t, g: (b * nt + t, 0)),
        out_shape=jax.ShapeDtypeStruct((batch * seq, d), F32),
        scratch_shapes=[pltpu.VMEM((tt, d), F32)],
        compiler_params=_params(("parallel", "parallel", "arbitrary"), vmem),
        name="combine",
    )(x_res, y4, pcol, gain.reshape(1, d))


def _pool_kernel(x_ref, prev_ref, next_ref, ng_ref, w_ref, s_ref, g_ref, xo_ref, ho_ref, pad_ref,
                 *, seq, windows):
    i = pl.program_id(0)
    tm, d = x_ref.shape
    halo = prev_ref.shape[0]
    pg = w_ref.shape[1]
    tiles_per_seq = seq // tm
    tile_in_seq = i % tiles_per_seq
    norm_gain = ng_ref[...]
    h = _rms(x_ref[...], norm_gain)
    pad_ref[0:halo, :] = jnp.where(tile_in_seq == 0, 0.0, _rms(prev_ref[...], norm_gain))
    pad_ref[halo:halo + tm, :] = h
    pad_ref[halo + tm:2 * halo + tm, :] = jnp.where(tile_in_seq == tiles_per_seq - 1, 0.0,
                                                    _rms(next_ref[...], norm_gain))
    t = tile_in_seq * tm + lax.broadcasted_iota(jnp.int32, (tm, 1), 0)
    for gi, w in enumerate(windows):
        cols = slice(gi * pg, (gi + 1) * pg)
        acc = pad_ref[:, cols]
        span = 1
        while span < w:
            acc = acc + pltpu.roll(acc, acc.shape[0] - span, axis=0)
            span *= 2
        total = acc[halo - w // 2:halo - w // 2 + tm, :]
        count = (jnp.minimum(t + (w - w // 2), seq) - jnp.maximum(t - w // 2, 0)).astype(F32)
        pooled = (total / count - h[:, cols]).astype(BF16)
        mixed = jnp.dot(pooled, w_ref[gi], preferred_element_type=F32)
        xo_ref[:, cols] = mixed * s_ref[:, cols] + x_ref[:, cols]
    ho_ref[...] = _rms(xo_ref[...], g_ref[...]).astype(ho_ref.dtype)


def pool_mixer(x, norm_gain, w_pool, pool_scale, next_gain, *, seq):
    m, d = x.shape
    ng, pg, _ = w_pool.shape
    halo = 8
    assert all(w & (w - 1) == 0 and w // 2 <= halo for w in POOL_WINDOWS) and ng == len(POOL_WINDOWS)
    tm = _tile(seq, 256)
    per = tm // halo
    n_halo_blocks = m // halo
    vmem = 2 * (2 * tm * d * 4 + 2 * halo * d * 4 + ng * pg * pg * 2 + tm * d * 2) \
        + (tm + 2 * halo) * d * 4 + 6 * tm * d * 4
    return pl.pallas_call(
        functools.partial(_pool_kernel, seq=seq, windows=POOL_WINDOWS),
        grid=(m // tm,),
        in_specs=[pl.BlockSpec((tm, d), lambda i: (i, 0)),
                  pl.BlockSpec((halo, d), lambda i: (jnp.maximum(i * per - 1, 0), 0)),
                  pl.BlockSpec((halo, d), lambda i: (jnp.minimum((i + 1) * per, n_halo_blocks - 1), 0)),
                  pl.BlockSpec((1, d), lambda i: (0, 0)),
                  pl.BlockSpec((ng, pg, pg), lambda i: (0, 0, 0)),
                  pl.BlockSpec((1, d), lambda i: (0, 0)),
                  pl.BlockSpec((1, d), lambda i: (0, 0))],
        out_specs=[pl.BlockSpec((tm, d), lambda i: (i, 0)),
                   pl.BlockSpec((tm, d), lambda i: (i, 0))],
        out_shape=[jax.ShapeDtypeStruct((m, d), F32), jax.ShapeDtypeStruct((m, d), BF16)],
        scratch_shapes=[pltpu.VMEM((tm + 2 * halo, d), F32)],
        compiler_params=_params(("parallel",), vmem),
        name="pool_mixer",
    )(x, x, x, norm_gain.reshape(1, d), w_pool, pool_scale.reshape(1, d), next_gain.reshape(1, d))


def _rope_tables(seq):
    half = HEAD_DIM // 2
    rows = seq // GRID_W
    row_idx = jnp.repeat(jnp.arange(rows), GRID_W).astype(F32)
    col_idx = jnp.tile(jnp.arange(GRID_W), rows).astype(F32)
    inv_freq = 1.0 / (ROPE_THETA ** (jnp.arange(0, half, 2, dtype=F32) / half))
    ang = jnp.concatenate([row_idx[:, None] * inv_freq[None, :],
                           col_idx[:, None] * inv_freq[None, :]], axis=-1)
    cos, sin = jnp.cos(ang), jnp.sin(ang)
    zero = jnp.zeros_like(sin)
    cos_full = jnp.stack([cos, cos], axis=-1).reshape(seq, HEAD_DIM)
    sin_even = jnp.stack([-sin, zero], axis=-1).reshape(seq, HEAD_DIM)
    sin_odd = jnp.stack([zero, sin], axis=-1).reshape(seq, HEAD_DIM)
    return cos_full, sin_even, sin_odd


def _split_router(w_router):
    d, n_experts = w_router.shape
    w = jnp.pad(w_router, ((0, 0), (0, LANES - n_experts)))
    hi = w.astype(BF16)
    lo = (w - hi.astype(F32)).astype(BF16)
    return jnp.concatenate([hi, lo], axis=1), n_experts


def _moe(x_res, h, aff, w_gate, w_up, w_down, layer, gain, *, batch, seq, final_norm):
    n_experts = w_gate.shape[1]
    capacity = EC_CAPACITY_FACTOR * seq // n_experts
    d = h.shape[1]
    prow, arow, pcol = route(aff, batch=batch, seq=seq, n_experts=n_experts, capacity=capacity)
    xs = gather_tokens(h, prow, batch=batch, seq=seq, n_experts=n_experts, capacity=capacity)
    y = expert_ffn(xs.reshape(n_experts * batch * capacity, d), w_gate, w_up, w_down, prow, arow,
                   layer, rows_per_expert=batch * capacity, capacity=capacity)
    return combine(x_res, y, pcol, gain, batch=batch, seq=seq,
                   n_experts=n_experts, capacity=capacity, final_norm=final_norm)


def _cross_block(x, h, mem2d, layer, cross_w_q, cross_w_k, cross_w_v, cross_w_o, mem_norm,
                 ffn_norm, router_w, *, batch, seq, mem_len):
    mem_n = rms_norm_rows(mem2d, mem_norm[layer], BF16)
    k = matmul_streamed(mem_n, cross_w_k, (layer,))
    v = matmul_streamed(mem_n, cross_w_v, (layer,))
    o = cross_attention(h, cast_weight(cross_w_q, (layer,)), k, v,
                        batch=batch, seq=seq, mem_len=mem_len)
    return matmul_residual_norm([o], cast_weight(cross_w_o, (layer,)), x, ffn_norm[layer],
                                router=_split_router(router_w[layer]))


def kernel(x, mem, mix_norm, attn_w_in, q_gain, k_gain, fourier_w, attn_w_out, pool_w, pool_scale,
           cross_norm, mem_norm, cross_w_q, cross_w_k, cross_w_v, cross_w_o, ffn_norm, router_w,
           expert_w_gate, expert_w_up, expert_w_down, final_norm):
    batch, seq, d = x.shape
    mem_len = mem.shape[1]
    depth = mix_norm.shape[0]
    attn_width = N_Q_HEADS * HEAD_DIM
    kv_width = N_KV_HEADS * HEAD_DIM
    fourier_width = N_FOURIER_GROUPS * FOURIER_GROUP
    rope_width = attn_width + kv_width
    cos_t, sin_a, sin_b = _rope_tables(seq)

    xf = x.reshape(batch * seq, d)
    mem2d = mem.reshape(batch * mem_len, d)
    cross = functools.partial(_cross_block, mem2d=mem2d, cross_w_q=cross_w_q, cross_w_k=cross_w_k,
                              cross_w_v=cross_w_v, cross_w_o=cross_w_o, mem_norm=mem_norm,
                              ffn_norm=ffn_norm, router_w=router_w,
                              batch=batch, seq=seq, mem_len=mem_len)
    for layer in range(depth):
        i = layer // 2
        if layer % 2 == 0:
            gain_cols = jnp.concatenate([jnp.tile(q_gain[i], N_Q_HEADS), jnp.tile(k_gain[i], N_KV_HEADS),
                                         jnp.ones((kv_width + fourier_width,), F32)]).reshape(1, -1)
            proj = proj_in(xf, mix_norm[layer], cast_weight(attn_w_in, (i,)), gain_cols,
                           cos_t, sin_a, sin_b, seq=seq, rope_width=rope_width)
            o_attn = gqa_attention(proj, batch=batch, seq=seq, attn_width=attn_width, kv_width=kv_width)
            o_four = fourier_mixer(proj, fourier_w[i], batch=batch, seq=seq,
                                   col0=attn_width + 2 * kv_width)
            xf, h = matmul_residual_norm([o_attn, o_four], cast_weight(attn_w_out, (i,)), xf,
                                         cross_norm[layer])
        else:
            xf, h = pool_mixer(xf, mix_norm[layer], pool_w[i].astype(BF16), pool_scale[i],
                               cross_norm[layer], seq=seq)
        xf, h, aff = cross(xf, h, layer=layer)
        xf = _moe(xf, h, aff, expert_w_gate, expert_w_up, expert_w_down, layer, final_norm,
                  batch=batch, seq=seq, final_norm=layer == depth - 1)
    return xf.reshape(batch, seq, d)
```

```python
import functools
import math

import jax
import jax.numpy as jnp
from jax import lax
from jax.experimental import pallas as pl
from jax.experimental.pallas import tpu as pltpu

F32 = jnp.float32
BF16 = jnp.bfloat16

GRID_W = 64
HEAD_DIM = 128
N_Q_HEADS = 12
N_KV_HEADS = 4
N_FOURIER_GROUPS = 4
FOURIER_GROUP = 128
ROPE_THETA = 10000.0
POOL_WINDOWS = (2, 4, 8, 16)
N_MEM_HEADS = 4
EC_CAPACITY_FACTOR = 2
NORM_EPS = 1e-6

LANES = 128
V7X_VMEM_BYTES = 64 * 1024 * 1024
VMEM_HEADROOM_BYTES = 6 * 1024 * 1024
COL_CHUNK = 512
LOG2_E = math.log2(math.e)


def _params(semantics, vmem_bytes):
    limit = min(int(vmem_bytes) + VMEM_HEADROOM_BYTES, V7X_VMEM_BYTES - VMEM_HEADROOM_BYTES)
    return pltpu.CompilerParams(dimension_semantics=semantics, vmem_limit_bytes=limit)


def _tile(n, pref):
    t = min(n, pref)
    while n % t:
        t //= 2
    return t


def _rms(x, gain):
    ms = jnp.mean(x * x, axis=-1, keepdims=True)
    return x * lax.rsqrt(ms + NORM_EPS) * gain


def _softmax_numerators(scores, scale):
    c = scale * LOG2_E
    return [jnp.exp2((s - jnp.max(s, axis=-1, keepdims=True)) * c) for s in scores]


def _norm_kernel(x_ref, g_ref, o_ref):
    o_ref[...] = _rms(x_ref[...], g_ref[...]).astype(o_ref.dtype)


def rms_norm_rows(x, gain, out_dtype):
    m, d = x.shape
    tm = _tile(m, 512)
    return pl.pallas_call(
        _norm_kernel,
        grid=(m // tm,),
        in_specs=[pl.BlockSpec((tm, d), lambda i: (i, 0)),
                  pl.BlockSpec((1, d), lambda i: (0, 0))],
        out_specs=pl.BlockSpec((tm, d), lambda i: (i, 0)),
        out_shape=jax.ShapeDtypeStruct((m, d), out_dtype),
        compiler_params=_params(("parallel",), 2 * tm * d * 8),
        name="rms_norm_rows",
    )(x, gain.reshape(1, d))


def _cast_kernel(w_ref, o_ref):
    o_ref[...] = w_ref[...].astype(o_ref.dtype)


def cast_weight(w, lead):
    k, n = w.shape[-2:]
    tk = _tile(k, 1024)
    squeezed = (None,) * len(lead)
    return pl.pallas_call(
        _cast_kernel,
        grid=(k // tk,),
        in_specs=[pl.BlockSpec(squeezed + (tk, n), lambda i: tuple(lead) + (i, 0))],
        out_specs=pl.BlockSpec((tk, n), lambda i: (i, 0)),
        out_shape=jax.ShapeDtypeStruct((k, n), BF16),
        compiler_params=_params(("parallel",), 2 * tk * n * 6),
        name="cast_weight",
    )(w)


def _mm_stream_kernel(a_ref, w_ref, o_ref):
    o_ref[...] = jnp.dot(a_ref[...], w_ref[...].astype(BF16),
                         preferred_element_type=F32).astype(o_ref.dtype)


def matmul_streamed(a, w, lead, out_dtype=BF16):
    m, k = a.shape
    n = w.shape[-1]
    tn = _tile(n, COL_CHUNK)
    squeezed = (None,) * len(lead)
    vmem = 2 * (m * k * 2 + k * tn * 4 + m * tn * 2) + k * tn * 2 + m * tn * 4
    return pl.pallas_call(
        _mm_stream_kernel,
        grid=(n // tn,),
        in_specs=[pl.BlockSpec((m, k), lambda j: (0, 0)),
                  pl.BlockSpec(squeezed + (k, tn), lambda j: tuple(lead) + (0, j))],
        out_specs=pl.BlockSpec((m, tn), lambda j: (0, j)),
        out_shape=jax.ShapeDtypeStruct((m, n), out_dtype),
        compiler_params=_params(("parallel",), vmem),
        name="matmul_streamed",
    )(a, w)


def _proj_in_kernel(a_ref, ng_ref, w_ref, g_ref, cos_ref, sa_ref, sb_ref, o_ref, *, n_rope_heads):
    n = o_ref.shape[1]
    tn = _tile(n, COL_CHUNK)
    a = _rms(a_ref[...], ng_ref[...]).astype(BF16)
    cos, sin_a, sin_b = cos_ref[...], sa_ref[...], sb_ref[...]
    chunks = [jnp.dot(a, w_ref[:, c * tn:(c + 1) * tn], preferred_element_type=F32)
              for c in range(n // tn)]
    for c, acc in enumerate(chunks):
        for h in range(tn // HEAD_DIM):
            head = c * (tn // HEAD_DIM) + h
            col = slice(head * HEAD_DIM, (head + 1) * HEAD_DIM)
            xh = acc[:, h * HEAD_DIM:(h + 1) * HEAD_DIM]
            if head < n_rope_heads:
                ms = jnp.mean(xh * xh, axis=-1, keepdims=True)
                y = xh * lax.rsqrt(ms + NORM_EPS) * g_ref[:, col]
                xh = (y * cos + pltpu.roll(y, HEAD_DIM - 1, axis=1) * sin_a
                      + pltpu.roll(y, 1, axis=1) * sin_b)
            o_ref[:, col] = xh.astype(o_ref.dtype)


def proj_in(x, norm_gain, w_in, gain_cols, cos_t, sin_a, sin_b, *, seq, rope_width):
    m, k = x.shape
    n = w_in.shape[1]
    tm = _tile(seq, 512)
    n_pos_blocks = seq // tm
    vmem = 2 * (tm * k * 4 + k * n * 2 + tm * n * 2 + 3 * tm * HEAD_DIM * 4) + 6 * tm * COL_CHUNK * 4 \
        + 2 * tm * k * 4
    table = pl.BlockSpec((tm, HEAD_DIM), lambda i: (i % n_pos_blocks, 0))
    return pl.pallas_call(
        functools.partial(_proj_in_kernel, n_rope_heads=rope_width // HEAD_DIM),
        grid=(m // tm,),
        in_specs=[pl.BlockSpec((tm, k), lambda i: (i, 0)),
                  pl.BlockSpec((1, k), lambda i: (0, 0)),
                  pl.BlockSpec((k, n), lambda i: (0, 0)),
                  pl.BlockSpec((1, n), lambda i: (0, 0)),
                  table, table, table],
        out_specs=pl.BlockSpec((tm, n), lambda i: (i, 0)),
        out_shape=jax.ShapeDtypeStruct((m, n), BF16),
        compiler_params=_params(("parallel",), vmem),
        name="proj_in",
    )(x, norm_gain.reshape(1, k), w_in, gain_cols, cos_t, sin_a, sin_b)


def _attn_kernel(q_ref, k_ref, v_ref, o_ref, *, group, scale):
    k = k_ref[...]
    v = v_ref[...]
    v_ones = jnp.concatenate([v, jnp.ones_like(v)], axis=1)
    heads = [slice(g * HEAD_DIM, (g + 1) * HEAD_DIM) for g in range(group)]
    scores = [lax.dot_general(q_ref[:, sl], k, (((1,), (1,)), ((), ())),
                              preferred_element_type=F32) for sl in heads]
    for sl, e in zip(heads, _softmax_numerators(scores, scale)):
        ol = jnp.dot(e.astype(BF16), v_ones, preferred_element_type=F32)
        o_ref[:, sl] = (ol[:, :HEAD_DIM] / ol[:, HEAD_DIM:HEAD_DIM + 1]).astype(o_ref.dtype)


def gqa_attention(proj, *, batch, seq, attn_width, kv_width):
    group = N_Q_HEADS // N_KV_HEADS
    gw = group * HEAD_DIM
    tq = _tile(seq, 1024)
    nq = seq // tq
    k_col0 = attn_width // HEAD_DIM
    v_col0 = (attn_width + kv_width) // HEAD_DIM
    vmem = 2 * (tq * gw * 2 * 2 + 2 * seq * HEAD_DIM * 2) + group * tq * seq * 6
    return pl.pallas_call(
        functools.partial(_attn_kernel, group=group, scale=HEAD_DIM ** -0.5),
        grid=(batch, N_KV_HEADS, nq),
        in_specs=[pl.BlockSpec((tq, gw), lambda b, h, i: (b * nq + i, h)),
                  pl.BlockSpec((seq, HEAD_DIM), lambda b, h, i: (b, k_col0 + h)),
                  pl.BlockSpec((seq, HEAD_DIM), lambda b, h, i: (b, v_col0 + h))],
        out_specs=pl.BlockSpec((tq, gw), lambda b, h, i: (b * nq + i, h)),
        out_shape=jax.ShapeDtypeStruct((batch * seq, attn_width), BF16),
        compiler_params=_params(("parallel", "parallel", "parallel"), vmem),
        name="gqa_attention",
    )(proj, proj, proj)


def _fourier_kernel(f_ref, cc_ref, sc_ref, cs_ref, ss_ref, wf_ref, o_ref, ab_ref, *, seq, norm):
    r = pl.program_id(1)
    ng = wf_ref.shape[0]
    cg = wf_ref.shape[1]

    @pl.when(r == 0)
    def _():
        for g in range(ng):
            fg = f_ref[:, g * cg:(g + 1) * cg]
            ab_ref[0:seq, g * cg:(g + 1) * cg] = jnp.dot(
                fg, cc_ref[...], preferred_element_type=F32).astype(BF16)
            ab_ref[seq:2 * seq, g * cg:(g + 1) * cg] = jnp.dot(
                fg, sc_ref[...], preferred_element_type=F32).astype(BF16)

    z = (jnp.dot(cs_ref[...], ab_ref[0:seq, :], preferred_element_type=F32)
         - jnp.dot(ss_ref[...], ab_ref[seq:2 * seq, :], preferred_element_type=F32)) * norm
    for g in range(ng):
        zg = z[:, g * cg:(g + 1) * cg].astype(BF16)
        o_ref[:, g * cg:(g + 1) * cg] = jnp.dot(
            zg, wf_ref[g].astype(BF16), preferred_element_type=F32).astype(o_ref.dtype)


def _dft_cos_sin(n):
    lo = _tile(n, 64)
    hi = n // lo
    j = jnp.arange(n, dtype=jnp.int32)

    def table(k):
        ang = ((k[:, None] * j[None, :]) % n).astype(F32) * (2.0 * math.pi / n)
        return jnp.cos(ang), jnp.sin(ang)

    ca, sa = table(jnp.arange(hi, dtype=jnp.int32) * lo)
    cb, sb = table(jnp.arange(lo, dtype=jnp.int32))
    cos = ca[:, None, :] * cb[None, :, :] - sa[:, None, :] * sb[None, :, :]
    sin = sa[:, None, :] * cb[None, :, :] + ca[:, None, :] * sb[None, :, :]
    return cos.reshape(n, n).astype(BF16), sin.reshape(n, n).astype(BF16)


def fourier_mixer(proj, w_fourier, *, batch, seq, col0):
    ng, cg, _ = w_fourier.shape
    fw = ng * cg
    tr = _tile(seq, 512)
    nr = seq // tr
    cc, sc = _dft_cos_sin(cg)
    cs, ss = _dft_cos_sin(seq)
    norm = 1.0 / math.sqrt(seq * cg)
    vmem = 2 * (seq * fw * 2 + 2 * cg * cg * 2 + 2 * tr * seq * 2 + ng * cg * cg * 4 + tr * fw * 2) \
        + 2 * seq * fw * 2 + 4 * tr * fw * 4
    return pl.pallas_call(
        functools.partial(_fourier_kernel, seq=seq, norm=norm),
        grid=(batch, nr),
        in_specs=[pl.BlockSpec((seq, fw), lambda b, r: (b, col0 // fw)),
                  pl.BlockSpec((cg, cg), lambda b, r: (0, 0)),
                  pl.BlockSpec((cg, cg), lambda b, r: (0, 0)),
                  pl.BlockSpec((tr, seq), lambda b, r: (r, 0)),
                  pl.BlockSpec((tr, seq), lambda b, r: (r, 0)),
                  pl.BlockSpec((ng, cg, cg), lambda b, r: (0, 0, 0))],
        out_specs=pl.BlockSpec((tr, fw), lambda b, r: (b * nr + r, 0)),
        out_shape=jax.ShapeDtypeStruct((batch * seq, fw), BF16),
        scratch_shapes=[pltpu.VMEM((2 * seq, fw), BF16)],
        compiler_params=_params(("parallel", "arbitrary"), vmem),
        name="fourier_mixer",
    )(proj, cc, sc, cs, ss, w_fourier)


def _router_affinity(hn, wr_ref, n_experts):
    hi = hn.astype(BF16)
    lo = (hn - hi.astype(F32)).astype(BF16)
    r_hi = jnp.dot(hi, wr_ref[...], preferred_element_type=F32)
    r_lo = jnp.dot(lo, wr_ref[...], preferred_element_type=F32)
    logits = r_hi[:, :LANES] + r_hi[:, LANES:] + r_lo[:, :LANES]
    lane = lax.broadcasted_iota(jnp.int32, logits.shape, 1)
    logits = jnp.where(lane < n_experts, logits, -jnp.inf)
    e = jnp.exp(logits - jnp.max(logits, axis=-1, keepdims=True))
    return e / jnp.sum(e, axis=-1, keepdims=True)


def _mm_res_norm_kernel(*refs, n_a, with_router, n_experts):
    a_refs = refs[:n_a]
    w_ref, x_ref, g_ref = refs[n_a:n_a + 3]
    pos = n_a + 3
    if with_router:
        wr_ref = refs[pos]
        pos += 1
    xo_ref, ho_ref = refs[pos:pos + 2]
    if with_router:
        aff_ref = refs[pos + 2]
    d = xo_ref.shape[1]
    tn = _tile(d, COL_CHUNK)
    for c in range(d // tn):
        sl = slice(c * tn, (c + 1) * tn)
        acc = x_ref[:, sl]
        row0 = 0
        for a_ref in a_refs:
            ka = a_ref.shape[1]
            acc = acc + jnp.dot(a_ref[...], w_ref[row0:row0 + ka, sl], preferred_element_type=F32)
            row0 += ka
        xo_ref[:, sl] = acc
    hn = _rms(xo_ref[...], g_ref[...])
    ho_ref[...] = hn.astype(ho_ref.dtype)
    if with_router:
        aff_ref[...] = _router_affinity(hn, wr_ref, n_experts)


def matmul_residual_norm(a_list, w, x_res, gain, *, router=None, tm_pref=512):
    m, d = x_res.shape
    k = w.shape[0]
    tm = _tile(m, tm_pref)
    in_specs = [pl.BlockSpec((tm, a.shape[1]), lambda i: (i, 0)) for a in a_list]
    in_specs += [pl.BlockSpec((k, d), lambda i: (0, 0)),
                 pl.BlockSpec((tm, d), lambda i: (i, 0)),
                 pl.BlockSpec((1, d), lambda i: (0, 0))]
    args = list(a_list) + [w, x_res, gain.reshape(1, d)]
    out_specs = [pl.BlockSpec((tm, d), lambda i: (i, 0)),
                 pl.BlockSpec((tm, d), lambda i: (i, 0))]
    out_shape = [jax.ShapeDtypeStruct((m, d), F32), jax.ShapeDtypeStruct((m, d), BF16)]
    n_experts = 0
    if router is not None:
        wr, n_experts = router
        in_specs.append(pl.BlockSpec((d, 2 * LANES), lambda i: (0, 0)))
        args.append(wr)
        out_specs.append(pl.BlockSpec((tm, LANES), lambda i: (i, 0)))
        out_shape.append(jax.ShapeDtypeStruct((m, LANES), F32))
    vmem = 2 * (tm * k * 2 + k * d * 2 + tm * d * 4 + tm * d * 4 + tm * d * 2) \
        + 3 * tm * d * 4 + 4 * d * LANES * 2
    return pl.pallas_call(
        functools.partial(_mm_res_norm_kernel, n_a=len(a_list),
                          with_router=router is not None, n_experts=n_experts),
        grid=(m // tm,),
        in_specs=in_specs,
        out_specs=out_specs,
        out_shape=out_shape,
        compiler_params=_params(("parallel",), vmem),
        name="matmul_residual_norm",
    )(*args)


def _cross_attn_kernel(h_ref, wq_ref, k_ref, v_ref, o_ref, *, n_heads, scale):
    dh = h_ref.shape[1] // n_heads
    h = h_ref[...]
    heads = [slice(i * dh, (i + 1) * dh) for i in range(n_heads)]
    qs = [jnp.dot(h, wq_ref[:, sl], preferred_element_type=F32).astype(BF16) for sl in heads]
    scores = [lax.dot_general(q, k_ref[:, sl], (((1,), (1,)), ((), ())),
                              preferred_element_type=F32) for q, sl in zip(qs, heads)]
    for sl, e in zip(heads, _softmax_numerators(scores, scale)):
        l = jnp.sum(e, axis=-1, keepdims=True)
        o = jnp.dot(e.astype(BF16), v_ref[:, sl], preferred_element_type=F32) / l
        o_ref[:, sl] = o.astype(o_ref.dtype)


def cross_attention(h, w_q, k, v, *, batch, seq, mem_len):
    d = h.shape[1]
    tq = _tile(seq, 512)
    nq = seq // tq
    vmem = 2 * (2 * tq * d * 2 + d * d * 2 + 2 * mem_len * d * 2) + 6 * tq * mem_len * 4 + 3 * tq * d * 4
    return pl.pallas_call(
        functools.partial(_cross_attn_kernel, n_heads=N_MEM_HEADS,
                          scale=(d // N_MEM_HEADS) ** -0.5),
        grid=(batch, nq),
        in_specs=[pl.BlockSpec((tq, d), lambda b, i: (b * nq + i, 0)),
                  pl.BlockSpec((d, d), lambda b, i: (0, 0)),
                  pl.BlockSpec((mem_len, d), lambda b, i: (b, 0)),
                  pl.BlockSpec((mem_len, d), lambda b, i: (b, 0))],
        out_specs=pl.BlockSpec((tq, d), lambda b, i: (b * nq + i, 0)),
        out_shape=jax.ShapeDtypeStruct((batch * seq, d), BF16),
        compiler_params=_params(("parallel", "parallel"), vmem),
        name="cross_attention",
    )(h, w_q, k, v)


def _prefix_count(x, lane):
    n = x.shape[1]
    shift = 1
    while shift < n:
        x = x + jnp.where(lane >= shift, pltpu.roll(x, shift, axis=1), 0)
        shift *= 2
    return x


def _route_kernel(aff_ref, prow_ref, arow_ref, pcol_ref, *, n_experts, capacity):
    e_pad = prow_ref.shape[1]
    a_t = aff_ref[...].T[:e_pad]
    arow_ref[0] = a_t
    seq = a_t.shape[1]

    def body(i, thr_bits):
        cand = thr_bits | jnp.left_shift(jnp.int32(1), 30 - i)
        cnt = jnp.sum((a_t >= pltpu.bitcast(cand, F32)).astype(F32), axis=1, keepdims=True)
        return jnp.where(cnt >= capacity, cand, thr_bits)

    thr = pltpu.bitcast(lax.fori_loop(0, 31, body, jnp.zeros((e_pad, 1), jnp.int32)), F32)
    lane = lax.broadcasted_iota(jnp.int32, a_t.shape, 1)
    gt = a_t > thr
    eq = a_t == thr
    need = capacity - jnp.sum(gt.astype(F32), axis=1, keepdims=True).astype(jnp.int32)
    sel = gt | (eq & (_prefix_count(eq.astype(jnp.int32), lane) <= need))
    pos = jnp.where(sel, _prefix_count(sel.astype(jnp.int32), lane) - 1, -1)
    row = lax.broadcasted_iota(jnp.int32, a_t.shape, 0)
    pos = jnp.where(row < n_experts, pos, -1)
    prow_ref[0] = pos
    unused = jnp.full((LANES - e_pad, seq), -1.0, F32)
    pcol_ref[...] = jnp.concatenate([pos.astype(F32), unused], axis=0).T


def route(aff, *, batch, seq, n_experts, capacity):
    e_pad = max(8, n_experts)
    return pl.pallas_call(
        functools.partial(_route_kernel, n_experts=n_experts, capacity=capacity),
        grid=(batch,),
        in_specs=[pl.BlockSpec((seq, LANES), lambda b: (b, 0))],
        out_specs=[pl.BlockSpec((1, e_pad, seq), lambda b: (b, 0, 0)),
                   pl.BlockSpec((1, e_pad, seq), lambda b: (b, 0, 0)),
                   pl.BlockSpec((seq, LANES), lambda b: (b, 0))],
        out_shape=[jax.ShapeDtypeStruct((batch, e_pad, seq), jnp.int32),
                   jax.ShapeDtypeStruct((batch, e_pad, seq), F32),
                   jax.ShapeDtypeStruct((batch * seq, LANES), F32)],
        compiler_params=_params(("parallel",), 24 * seq * LANES * 4),
        name="route",
    )(aff)


def _gather_kernel(h_ref, prow_ref, o_ref, *, experts_per_step):
    g = pl.program_id(1)
    cap = o_ref.shape[1]
    seq = h_ref.shape[0]
    slot = lax.broadcasted_iota(jnp.int32, (cap, seq), 0)
    h = h_ref[...]
    for j in range(experts_per_step):
        onehot = (slot == prow_ref[0, pl.ds(g * experts_per_step + j, 1), :]).astype(BF16)
        o_ref[j] = jnp.dot(onehot, h, preferred_element_type=F32).astype(o_ref.dtype)


def gather_tokens(h, prow, *, batch, seq, n_experts, capacity):
    d = h.shape[1]
    e_pad = prow.shape[1]
    ng = _tile(n_experts, 2)
    vmem = 2 * (seq * d * 2 + e_pad * seq * 4 + ng * capacity * d * 2) + ng * capacity * seq * 8 \
        + ng * capacity * d * 4
    return pl.pallas_call(
        functools.partial(_gather_kernel, experts_per_step=ng),
        grid=(batch, n_experts // ng),
        in_specs=[pl.BlockSpec((seq, d), lambda b, g: (b, 0)),
                  pl.BlockSpec((1, e_pad, seq), lambda b, g: (b, 0, 0))],
        out_specs=pl.BlockSpec((ng, None, capacity, d), lambda b, g: (g, b, 0, 0)),
        out_shape=jax.ShapeDtypeStruct((n_experts, batch, capacity, d), BF16),
        compiler_params=_params(("parallel", "arbitrary"), vmem),
        name="gather_tokens",
    )(h, prow)


def _ffn_kernel(x_ref, wg_ref, wu_ref, wd_ref, prow_ref, arow_ref, o_ref, acc_ref, gate_ref,
                *, capacity, gate_steps):
    e = pl.program_id(0)
    f = pl.program_id(1)
    nf = pl.num_programs(1)
    rows = x_ref.shape[0]
    n_seq = rows // capacity

    @pl.when(f == 0)
    def _():
        acc_ref[...] = jnp.zeros_like(acc_ref)

    x = x_ref[...]
    a = jnp.dot(x, wg_ref[...].astype(BF16), preferred_element_type=F32)
    u = jnp.dot(x, wu_ref[...].astype(BF16), preferred_element_type=F32)
    hidden = (a / (1.0 + jnp.exp(-a)) * u).astype(BF16)
    acc_ref[...] += jnp.dot(hidden, wd_ref[...].astype(BF16), preferred_element_type=F32)

    seq = prow_ref.shape[2]
    slot = lax.broadcasted_iota(jnp.int32, (capacity, seq), 0)
    per_step = -(-n_seq // gate_steps)
    for j in range(per_step):
        b = jnp.minimum(f * per_step + j, n_seq - 1)
        mine = slot == prow_ref[b, pl.ds(e, 1), :]
        gate_ref[pl.ds(pl.multiple_of(b * capacity, capacity), capacity), :] = jnp.sum(
            jnp.where(mine, arow_ref[b, pl.ds(e, 1), :], 0.0), axis=1, keepdims=True)

    @pl.when(f == nf - 1)
    def _():
        o_ref[...] = (acc_ref[...] * gate_ref[...]).astype(o_ref.dtype)


def expert_ffn(xs, w_gate, w_up, w_down, prow, arow, layer, *, rows_per_expert, capacity):
    _, n_experts, d, ff = w_gate.shape
    tf = _tile(ff // 2, 256)
    r = rows_per_expert
    batch, e_pad, seq = prow.shape
    vmem = 2 * (r * d * 2 + 3 * d * tf * 4 + r * d * 2 + 2 * batch * e_pad * seq * 4) + r * d * 4 \
        + r * LANES * 4 + 3 * d * tf * 2 + 4 * r * tf * 4 + r * d * 4 + 2 * capacity * seq * 4
    return pl.pallas_call(
        functools.partial(_ffn_kernel, capacity=capacity, gate_steps=ff // tf),
        grid=(n_experts, ff // tf),
        in_specs=[pl.BlockSpec((r, d), lambda e, f: (e, 0)),
                  pl.BlockSpec((None, None, d, tf), lambda e, f: (layer, e, 0, f)),
                  pl.BlockSpec((None, None, d, tf), lambda e, f: (layer, e, 0, f)),
                  pl.BlockSpec((None, None, tf, d), lambda e, f: (layer, e, f, 0)),
                  pl.BlockSpec((batch, e_pad, seq), lambda e, f: (0, 0, 0)),
                  pl.BlockSpec((batch, e_pad, seq), lambda e, f: (0, 0, 0))],
        out_specs=pl.BlockSpec((r, d), lambda e, f: (e, 0)),
        out_shape=jax.ShapeDtypeStruct((n_experts * r, d), BF16),
        scratch_shapes=[pltpu.VMEM((r, d), F32), pltpu.VMEM((r, 1), F32)],
        compiler_params=_params(("parallel", "arbitrary"), vmem),
        name="expert_ffn",
    )(xs, w_gate, w_up, w_down, prow, arow)


def _combine_kernel(x_ref, y_ref, pcol_ref, g_ref, out_ref, acc_ref, *, final_norm):
    g = pl.program_id(2)
    tt, d = x_ref.shape
    n_group, cap, _ = y_ref.shape

    @pl.when(g == 0)
    def _():
        acc_ref[...] = x_ref[...]

    lane = lax.broadcasted_iota(jnp.int32, (tt, LANES), 1)
    slot = lax.broadcasted_iota(jnp.int32, (tt, cap), 1).astype(F32)
    pcol = pcol_ref[...]
    total = None
    for j in range(n_group):
        slot_of_token = jnp.sum(jnp.where(lane == g * n_group + j, pcol, 0.0),
                                axis=1, keepdims=True)
        onehot = (slot == slot_of_token).astype(BF16)
        part = jnp.dot(onehot, y_ref[j], preferred_element_type=F32)
        total = part if total is None else total + part
    acc_ref[...] += total

    @pl.when(g == pl.num_programs(2) - 1)
    def _():
        if final_norm:
            out_ref[...] = _rms(acc_ref[...], g_ref[...])
        else:
            out_ref[...] = acc_ref[...]


def combine(x_res, y, pcol, gain, *, batch, seq, n_experts, capacity, final_norm):
    d = x_res.shape[1]
    tt = _tile(seq, 512)
    nt = seq // tt
    ng = _tile(n_experts, 4)
    y4 = y.reshape(n_experts, batch, capacity, d)
    vmem = 2 * (tt * d * 4 + ng * capacity * d * 2 + tt * LANES * 4 + tt * d * 4) \
        + 4 * tt * d * 4 + 2 * tt * ng * capacity * 4
    return pl.pallas_call(
        functools.partial(_combine_kernel, final_norm=final_norm),
        grid=(batch, nt, n_experts // ng),
        in_specs=[pl.BlockSpec((tt, d), lambda b, t, g: (b * nt + t, 0)),
                  pl.BlockSpec((ng, None, capacity, d), lambda b, t, g: (g, b, 0, 0)),
                  pl.BlockSpec((tt, LANES), lambda b, t, g: (b * nt + t, 0)),
                  pl.BlockSpec((1, d), lambda b, t, g: (0, 0))],
        out_specs=pl.BlockSpec((tt, d), lambda b, t, g: (b * nt + t, 0)),
        out_shape=jax.ShapeDtypeStruct((batch * seq, d), F32),
        scratch_shapes=[pltpu.VMEM((tt, d), F32)],
        compiler_params=_params(("parallel", "parallel", "arbitrary"), vmem),
        name="combine",
    )(x_res, y4, pcol, gain.reshape(1, d))


def _pool_kernel(x_ref, prev_ref, next_ref, ng_ref, w_ref, s_ref, g_ref, xo_ref, ho_ref, pad_ref,
                 *, seq, windows):
    i = pl.program_id(0)
    tm, d = x_ref.shape
    halo = prev_ref.shape[0]
    pg = w_ref.shape[1]
    tiles_per_seq = seq // tm
    tile_in_seq = i % tiles_per_seq
    norm_gain = ng_ref[...]
    h = _rms(x_ref[...], norm_gain)
    pad_ref[0:halo, :] = jnp.where(tile_in_seq == 0, 0.0, _rms(prev_ref[...], norm_gain))
    pad_ref[halo:halo + tm, :] = h
    pad_ref[halo + tm:2 * halo + tm, :] = jnp.where(tile_in_seq == tiles_per_seq - 1, 0.0,
                                                    _rms(next_ref[...], norm_gain))
    t = tile_in_seq * tm + lax.broadcasted_iota(jnp.int32, (tm, 1), 0)
    for gi, w in enumerate(windows):
        cols = slice(gi * pg, (gi + 1) * pg)
        acc = pad_ref[:, cols]
        span = 1
        while span < w:
            acc = acc + pltpu.roll(acc, acc.shape[0] - span, axis=0)
            span *= 2
        total = acc[halo - w // 2:halo - w // 2 + tm, :]
        count = (jnp.minimum(t + (w - w // 2), seq) - jnp.maximum(t - w // 2, 0)).astype(F32)
        pooled = (total / count - h[:, cols]).astype(BF16)
        mixed = jnp.dot(pooled, w_ref[gi], preferred_element_type=F32)
        xo_ref[:, cols] = mixed * s_ref[:, cols] + x_ref[:, cols]
    ho_ref[...] = _rms(xo_ref[...], g_ref[...]).astype(ho_ref.dtype)


def pool_mixer(x, norm_gain, w_pool, pool_scale, next_gain, *, seq):
    m, d = x.shape
    ng, pg, _ = w_pool.shape
    halo = 8
    assert all(w & (w - 1) == 0 and w // 2 <= halo for w in POOL_WINDOWS) and ng == len(POOL_WINDOWS)
    tm = _tile(seq, 256)
    per = tm // halo
    n_halo_blocks = m // halo
    vmem = 2 * (2 * tm * d * 4 + 2 * halo * d * 4 + ng * pg * pg * 2 + tm * d * 2) \
        + (tm + 2 * halo) * d * 4 + 6 * tm * d * 4
    return pl.pallas_call(
        functools.partial(_pool_kernel, seq=seq, windows=POOL_WINDOWS),
        grid=(m // tm,),
        in_specs=[pl.BlockSpec((tm, d), lambda i: (i, 0)),
                  pl.BlockSpec((halo, d), lambda i: (jnp.maximum(i * per - 1, 0), 0)),
                  pl.BlockSpec((halo, d), lambda i: (jnp.minimum((i + 1) * per, n_halo_blocks - 1), 0)),
                  pl.BlockSpec((1, d), lambda i: (0, 0)),
                  pl.BlockSpec((ng, pg, pg), lambda i: (0, 0, 0)),
                  pl.BlockSpec((1, d), lambda i: (0, 0)),
                  pl.BlockSpec((1, d), lambda i: (0, 0))],
        out_specs=[pl.BlockSpec((tm, d), lambda i: (i, 0)),
                   pl.BlockSpec((tm, d), lambda i: (i, 0))],
        out_shape=[jax.ShapeDtypeStruct((m, d), F32), jax.ShapeDtypeStruct((m, d), BF16)],
        scratch_shapes=[pltpu.VMEM((tm + 2 * halo, d), F32)],
        compiler_params=_params(("parallel",), vmem),
        name="pool_mixer",
    )(x, x, x, norm_gain.reshape(1, d), w_pool, pool_scale.reshape(1, d), next_gain.reshape(1, d))


def _rope_tables(seq):
    half = HEAD_DIM // 2
    rows = seq // GRID_W
    row_idx = jnp.repeat(jnp.arange(rows), GRID_W).astype(F32)
    col_idx = jnp.tile(jnp.arange(GRID_W), rows).astype(F32)
    inv_freq = 1.0 / (ROPE_THETA ** (jnp.arange(0, half, 2, dtype=F32) / half))
    ang = jnp.concatenate([row_idx[:, None] * inv_freq[None, :],
                           col_idx[:, None] * inv_freq[None, :]], axis=-1)
    cos, sin = jnp.cos(ang), jnp.sin(ang)
    zero = jnp.zeros_like(sin)
    cos_full = jnp.stack([cos, cos], axis=-1).reshape(seq, HEAD_DIM)
    sin_even = jnp.stack([-sin, zero], axis=-1).reshape(seq, HEAD_DIM)
    sin_odd = jnp.stack([zero, sin], axis=-1).reshape(seq, HEAD_DIM)
    return cos_full, sin_even, sin_odd


def _split_router(w_router):
    d, n_experts = w_router.shape
    w = jnp.pad(w_router, ((0, 0), (0, LANES - n_experts)))
    hi = w.astype(BF16)
    lo = (w - hi.astype(F32)).astype(BF16)
    return jnp.concatenate([hi, lo], axis=1), n_experts


def _moe(x_res, h, aff, w_gate, w_up, w_down, layer, gain, *, batch, seq, final_norm):
    n_experts = w_gate.shape[1]
    capacity = EC_CAPACITY_FACTOR * seq // n_experts
    d = h.shape[1]
    prow, arow, pcol = route(aff, batch=batch, seq=seq, n_experts=n_experts, capacity=capacity)
    xs = gather_tokens(h, prow, batch=batch, seq=seq, n_experts=n_experts, capacity=capacity)
    y = expert_ffn(xs.reshape(n_experts * batch * capacity, d), w_gate, w_up, w_down, prow, arow,
                   layer, rows_per_expert=batch * capacity, capacity=capacity)
    return combine(x_res, y, pcol, gain, batch=batch, seq=seq,
                   n_experts=n_experts, capacity=capacity, final_norm=final_norm)


def _cross_block(x, h, mem2d, layer, cross_w_q, cross_w_k, cross_w_v, cross_w_o, mem_norm,
                 ffn_norm, router_w, *, batch, seq, mem_len):
    mem_n = rms_norm_rows(mem2d, mem_norm[layer], BF16)
    k = matmul_streamed(mem_n, cross_w_k, (layer,))
    v = matmul_streamed(mem_n, cross_w_v, (layer,))
    o = cross_attention(h, cast_weight(cross_w_q, (layer,)), k, v,
                        batch=batch, seq=seq, mem_len=mem_len)
    return matmul_residual_norm([o], cast_weight(cross_w_o, (layer,)), x, ffn_norm[layer],
                                router=_split_router(router_w[layer]))


def kernel(x, mem, mix_norm, attn_w_in, q_gain, k_gain, fourier_w, attn_w_out, pool_w, pool_scale,
           cross_norm, mem_norm, cross_w_q, cross_w_k, cross_w_v, cross_w_o, ffn_norm, router_w,
           expert_w_gate, expert_w_up, expert_w_down, final_norm):
    batch, seq, d = x.shape
    mem_len = mem.shape[1]
    depth = mix_norm.shape[0]
    attn_width = N_Q_HEADS * HEAD_DIM
    kv_width = N_KV_HEADS * HEAD_DIM
    fourier_width = N_FOURIER_GROUPS * FOURIER_GROUP
    rope_width = attn_width + kv_width
    cos_t, sin_a, sin_b = _rope_tables(seq)

    xf = x.reshape(batch * seq, d)
    mem2d = mem.reshape(batch * mem_len, d)
    cross = functools.partial(_cross_block, mem2d=mem2d, cross_w_q=cross_w_q, cross_w_k=cross_w_k,
                              cross_w_v=cross_w_v, cross_w_o=cross_w_o, mem_norm=mem_norm,
                              ffn_norm=ffn_norm, router_w=router_w,
                              batch=batch, seq=seq, mem_len=mem_len)
    for layer in range(depth):
        i = layer // 2
        if layer % 2 == 0:
            gain_cols = jnp.concatenate([jnp.tile(q_gain[i], N_Q_HEADS), jnp.tile(k_gain[i], N_KV_HEADS),
                                         jnp.ones((kv_width + fourier_width,), F32)]).reshape(1, -1)
            proj = proj_in(xf, mix_norm[layer], cast_weight(attn_w_in, (i,)), gain_cols,
                           cos_t, sin_a, sin_b, seq=seq, rope_width=rope_width)
            o_attn = gqa_attention(proj, batch=batch, seq=seq, attn_width=attn_width, kv_width=kv_width)
            o_four = fourier_mixer(proj, fourier_w[i], batch=batch, seq=seq,
                                   col0=attn_width + 2 * kv_width)
            xf, h = matmul_residual_norm([o_attn, o_four], cast_weight(attn_w_out, (i,)), xf,
                                         cross_norm[layer])
        else:
            xf, h = pool_mixer(xf, mix_norm[layer], pool_w[i].astype(BF16), pool_scale[i],
                               cross_norm[layer], seq=seq)
        xf, h, aff = cross(xf, h, layer=layer)
        xf = _moe(xf, h, aff, expert_w_gate, expert_w_up, expert_w_down, layer, final_norm,
                  batch=batch, seq=seq, final_norm=layer == depth - 1)
    return xf.reshape(batch, seq, d)
```

```python
import functools
import math

import jax
import jax.numpy as jnp
from jax import lax
from jax.experimental import pallas as pl
from jax.experimental.pallas import tpu as pltpu

F32 = jnp.float32
BF16 = jnp.bfloat16

GRID_W = 64
HEAD_DIM = 128
N_Q_HEADS = 12
N_KV_HEADS = 4
N_FOURIER_GROUPS = 4
FOURIER_GROUP = 128
ROPE_THETA = 10000.0
POOL_WINDOWS = (2, 4, 8, 16)
N_MEM_HEADS = 4
EC_CAPACITY_FACTOR = 2
NORM_EPS = 1e-6

LANES = 128
V7X_VMEM_BYTES = 64 * 1024 * 1024
VMEM_HEADROOM_BYTES = 6 * 1024 * 1024
COL_CHUNK = 512
LOG2_E = math.log2(math.e)


def _params(semantics, vmem_bytes):
    limit = min(int(vmem_bytes) + VMEM_HEADROOM_BYTES, V7X_VMEM_BYTES - VMEM_HEADROOM_BYTES)
    return pltpu.CompilerParams(dimension_semantics=semantics, vmem_limit_bytes=limit)


def _tile(n, pref):
    t = min(n, pref)
    while n % t:
        t //= 2
    return t


def _rms(x, gain):
    ms = jnp.mean(x * x, axis=-1, keepdims=True)
    return x * lax.rsqrt(ms + NORM_EPS) * gain


def _softmax_numerators(scores, scale):
    c = scale * LOG2_E
    return [jnp.exp2((s - jnp.max(s, axis=-1, keepdims=True)) * c) for s in scores]


def _norm_kernel(x_ref, g_ref, o_ref):
    o_ref[...] = _rms(x_ref[...], g_ref[...]).astype(o_ref.dtype)


def rms_norm_rows(x, gain, out_dtype):
    m, d = x.shape
    tm = _tile(m, 512)
    return pl.pallas_call(
        _norm_kernel,
        grid=(m // tm,),
        in_specs=[pl.BlockSpec((tm, d), lambda i: (i, 0)),
                  pl.BlockSpec((1, d), lambda i: (0, 0))],
        out_specs=pl.BlockSpec((tm, d), lambda i: (i, 0)),
        out_shape=jax.ShapeDtypeStruct((m, d), out_dtype),
        compiler_params=_params(("parallel",), 2 * tm * d * 8),
        name="rms_norm_rows",
    )(x, gain.reshape(1, d))


def _cast_kernel(w_ref, o_ref):
    o_ref[...] = w_ref[...].astype(o_ref.dtype)


def cast_weight(w, lead):
    k, n = w.shape[-2:]
    tk = _tile(k, 1024)
    squeezed = (None,) * len(lead)
    return pl.pallas_call(
        _cast_kernel,
        grid=(k // tk,),
        in_specs=[pl.BlockSpec(squeezed + (tk, n), lambda i: tuple(lead) + (i, 0))],
        out_specs=pl.BlockSpec((tk, n), lambda i: (i, 0)),
        out_shape=jax.ShapeDtypeStruct((k, n), BF16),
        compiler_params=_params(("parallel",), 2 * tk * n * 6),
        name="cast_weight",
    )(w)


def _mm_stream_kernel(a_ref, w_ref, o_ref):
    o_ref[...] = jnp.dot(a_ref[...], w_ref[...].astype(BF16),
                         preferred_element_type=F32).astype(o_ref.dtype)


def matmul_streamed(a, w, lead, out_dtype=BF16):
    m, k = a.shape
    n = w.shape[-1]
    tn = _tile(n, COL_CHUNK)
    squeezed = (None,) * len(lead)
    vmem = 2 * (m * k * 2 + k * tn * 4 + m * tn * 2) + k * tn * 2 + m * tn * 4
    return pl.pallas_call(
        _mm_stream_kernel,
        grid=(n // tn,),
        in_specs=[pl.BlockSpec((m, k), lambda j: (0, 0)),
                  pl.BlockSpec(squeezed + (k, tn), lambda j: tuple(lead) + (0, j))],
        out_specs=pl.BlockSpec((m, tn), lambda j: (0, j)),
        out_shape=jax.ShapeDtypeStruct((m, n), out_dtype),
        compiler_params=_params(("parallel",), vmem),
        name="matmul_streamed",
    )(a, w)


def _proj_in_kernel(a_ref, ng_ref, w_ref, g_ref, cos_ref, sa_ref, sb_ref, o_ref, *, n_rope_heads):
    n = o_ref.shape[1]
    tn = _tile(n, COL_CHUNK)
    a = _rms(a_ref[...], ng_ref[...]).astype(BF16)
    cos, sin_a, sin_b = cos_ref[...], sa_ref[...], sb_ref[...]
    chunks = [jnp.dot(a, w_ref[:, c * tn:(c + 1) * tn], preferred_element_type=F32)
              for c in range(n // tn)]
    for c, acc in enumerate(chunks):
        for h in range(tn // HEAD_DIM):
            head = c * (tn // HEAD_DIM) + h
            col = slice(head * HEAD_DIM, (head + 1) * HEAD_DIM)
            xh = acc[:, h * HEAD_DIM:(h + 1) * HEAD_DIM]
            if head < n_rope_heads:
                ms = jnp.mean(xh * xh, axis=-1, keepdims=True)
                y = xh * lax.rsqrt(ms + NORM_EPS) * g_ref[:, col]
                xh = (y * cos + pltpu.roll(y, HEAD_DIM - 1, axis=1) * sin_a
                      + pltpu.roll(y, 1, axis=1) * sin_b)
            o_ref[:, col] = xh.astype(o_ref.dtype)


def proj_in(x, norm_gain, w_in, gain_cols, cos_t, sin_a, sin_b, *, seq, rope_width):
    m, k = x.shape
    n = w_in.shape[1]
    tm = _tile(seq, 512)
    n_pos_blocks = seq // tm
    vmem = 2 * (tm * k * 4 + k * n * 2 + tm * n * 2 + 3 * tm * HEAD_DIM * 4) + 6 * tm * COL_CHUNK * 4 \
        + 2 * tm * k * 4
    table = pl.BlockSpec((tm, HEAD_DIM), lambda i: (i % n_pos_blocks, 0))
    return pl.pallas_call(
        functools.partial(_proj_in_kernel, n_rope_heads=rope_width // HEAD_DIM),
        grid=(m // tm,),
        in_specs=[pl.BlockSpec((tm, k), lambda i: (i, 0)),
                  pl.BlockSpec((1, k), lambda i: (0, 0)),
                  pl.BlockSpec((k, n), lambda i: (0, 0)),
                  pl.BlockSpec((1, n), lambda i: (0, 0)),
                  table, table, table],
        out_specs=pl.BlockSpec((tm, n), lambda i: (i, 0)),
        out_shape=jax.ShapeDtypeStruct((m, n), BF16),
        compiler_params=_params(("parallel",), vmem),
        name="proj_in",
    )(x, norm_gain.reshape(1, k), w_in, gain_cols, cos_t, sin_a, sin_b)


def _attn_kernel(q_ref, k_ref, v_ref, o_ref, *, group, scale):
    k = k_ref[...]
    v = v_ref[...]
    v_ones = jnp.concatenate([v, jnp.ones_like(v)], axis=1)
    heads = [slice(g * HEAD_DIM, (g + 1) * HEAD_DIM) for g in range(group)]
    scores = [lax.dot_general(q_ref[:, sl], k, (((1,), (1,)), ((), ())),
                              preferred_element_type=F32) for sl in heads]
    for sl, e in zip(heads, _softmax_numerators(scores, scale)):
        ol = jnp.dot(e.astype(BF16), v_ones, preferred_element_type=F32)
        o_ref[:, sl] = (ol[:, :HEAD_DIM] / ol[:, HEAD_DIM:HEAD_DIM + 1]).astype(o_ref.dtype)


def gqa_attention(proj, *, batch, seq, attn_width, kv_width):
    group = N_Q_HEADS // N_KV_HEADS
    gw = group * HEAD_DIM
    tq = _tile(seq, 1024)
    nq = seq // tq
    k_col0 = attn_width // HEAD_DIM
    v_col0 = (attn_width + kv_width) // HEAD_DIM
    vmem = 2 * (tq * gw * 2 * 2 + 2 * seq * HEAD_DIM * 2) + group * tq * seq * 6
    return pl.pallas_call(
        functools.partial(_attn_kernel, group=group, scale=HEAD_DIM ** -0.5),
        grid=(batch, N_KV_HEADS, nq),
        in_specs=[pl.BlockSpec((tq, gw), lambda b, h, i: (b * nq + i, h)),
                  pl.BlockSpec((seq, HEAD_DIM), lambda b, h, i: (b, k_col0 + h)),
                  pl.BlockSpec((seq, HEAD_DIM), lambda b, h, i: (b, v_col0 + h))],
        out_specs=pl.BlockSpec((tq, gw), lambda b, h, i: (b * nq + i, h)),
        out_shape=jax.ShapeDtypeStruct((batch * seq, attn_width), BF16),
        compiler_params=_params(("parallel", "parallel", "parallel"), vmem),
        name="gqa_attention",
    )(proj, proj, proj)


def _fourier_kernel(f_ref, cc_ref, sc_ref, cs_ref, ss_ref, wf_ref, o_ref, ab_ref, *, seq, norm):
    r = pl.program_id(1)
    ng = wf_ref.shape[0]
    cg = wf_ref.shape[1]

    @pl.when(r == 0)
    def _():
        for g in range(ng):
            fg = f_ref[:, g * cg:(g + 1) * cg]
            ab_ref[0:seq, g * cg:(g + 1) * cg] = jnp.dot(
                fg, cc_ref[...], preferred_element_type=F32).astype(BF16)
            ab_ref[seq:2 * seq, g * cg:(g + 1) * cg] = jnp.dot(
                fg, sc_ref[...], preferred_element_type=F32).astype(BF16)

    z = (jnp.dot(cs_ref[...], ab_ref[0:seq, :], preferred_element_type=F32)
         - jnp.dot(ss_ref[...], ab_ref[seq:2 * seq, :], preferred_element_type=F32)) * norm
    for g in range(ng):
        zg = z[:, g * cg:(g + 1) * cg].astype(BF16)
        o_ref[:, g * cg:(g + 1) * cg] = jnp.dot(
            zg, wf_ref[g].astype(BF16), preferred_element_type=F32).astype(o_ref.dtype)


def _dft_cos_sin(n):
    lo = _tile(n, 64)
    hi = n // lo
    j = jnp.arange(n, dtype=jnp.int32)

    def table(k):
        ang = ((k[:, None] * j[None, :]) % n).astype(F32) * (2.0 * math.pi / n)
        return jnp.cos(ang), jnp.sin(ang)

    ca, sa = table(jnp.arange(hi, dtype=jnp.int32) * lo)
    cb, sb = table(jnp.arange(lo, dtype=jnp.int32))
    cos = ca[:, None, :] * cb[None, :, :] - sa[:, None, :] * sb[None, :, :]
    sin = sa[:, None, :] * cb[None, :, :] + ca[:, None, :] * sb[None, :, :]
    return cos.reshape(n, n).astype(BF16), sin.reshape(n, n).astype(BF16)


def fourier_mixer(proj, w_fourier, *, batch, seq, col0):
    ng, cg, _ = w_fourier.shape
    fw = ng * cg
    tr = _tile(seq, 512)
    nr = seq // tr
    cc, sc = _dft_cos_sin(cg)
    cs, ss = _dft_cos_sin(seq)
    norm = 1.0 / math.sqrt(seq * cg)
    vmem = 2 * (seq * fw * 2 + 2 * cg * cg * 2 + 2 * tr * seq * 2 + ng * cg * cg * 4 + tr * fw * 2) \
        + 2 * seq * fw * 2 + 4 * tr * fw * 4
    return pl.pallas_call(
        functools.partial(_fourier_kernel, seq=seq, norm=norm),
        grid=(batch, nr),
        in_specs=[pl.BlockSpec((seq, fw), lambda b, r: (b, col0 // fw)),
                  pl.BlockSpec((cg, cg), lambda b, r: (0, 0)),
                  pl.BlockSpec((cg, cg), lambda b, r: (0, 0)),
                  pl.BlockSpec((tr, seq), lambda b, r: (r, 0)),
                  pl.BlockSpec((tr, seq), lambda b, r: (r, 0)),
                  pl.BlockSpec((ng, cg, cg), lambda b, r: (0, 0, 0))],
        out_specs=pl.BlockSpec((tr, fw), lambda b, r: (b * nr + r, 0)),
        out_shape=jax.ShapeDtypeStruct((batch * seq, fw), BF16),
        scratch_shapes=[pltpu.VMEM((2 * seq, fw), BF16)],
        compiler_params=_params(("parallel", "arbitrary"), vmem),
        name="fourier_mixer",
    )(proj, cc, sc, cs, ss, w_fourier)


def _router_affinity(hn, wr_ref, n_experts):
    hi = hn.astype(BF16)
    lo = (hn - hi.astype(F32)).astype(BF16)
    r_hi = jnp.dot(hi, wr_ref[...], preferred_element_type=F32)
    r_lo = jnp.dot(lo, wr_ref[...], preferred_element_type=F32)
    logits = r_hi[:, :LANES] + r_hi[:, LANES:] + r_lo[:, :LANES]
    lane = lax.broadcasted_iota(jnp.int32, logits.shape, 1)
    logits = jnp.where(lane < n_experts, logits, -jnp.inf)
    e = jnp.exp(logits - jnp.max(logits, axis=-1, keepdims=True))
    return e / jnp.sum(e, axis=-1, keepdims=True)


def _mm_res_norm_kernel(*refs, n_a, with_router, n_experts):
    a_refs = refs[:n_a]
    w_ref, x_ref, g_ref = refs[n_a:n_a + 3]
    pos = n_a + 3
    if with_router:
        wr_ref = refs[pos]
        pos += 1
    xo_ref, ho_ref = refs[pos:pos + 2]
    if with_router:
        aff_ref = refs[pos + 2]
    d = xo_ref.shape[1]
    tn = _tile(d, COL_CHUNK)
    for c in range(d // tn):
        sl = slice(c * tn, (c + 1) * tn)
        acc = x_ref[:, sl]
        row0 = 0
        for a_ref in a_refs:
            ka = a_ref.shape[1]
            acc = acc + jnp.dot(a_ref[...], w_ref[row0:row0 + ka, sl], preferred_element_type=F32)
            row0 += ka
        xo_ref[:, sl] = acc
    hn = _rms(xo_ref[...], g_ref[...])
    ho_ref[...] = hn.astype(ho_ref.dtype)
    if with_router:
        aff_ref[...] = _router_affinity(hn, wr_ref, n_experts)


def matmul_residual_norm(a_list, w, x_res, gain, *, router=None, tm_pref=512):
    m, d = x_res.shape
    k = w.shape[0]
    tm = _tile(m, tm_pref)
    in_specs = [pl.BlockSpec((tm, a.shape[1]), lambda i: (i, 0)) for a in a_list]
    in_specs += [pl.BlockSpec((k, d), lambda i: (0, 0)),
                 pl.BlockSpec((tm, d), lambda i: (i, 0)),
                 pl.BlockSpec((1, d), lambda i: (0, 0))]
    args = list(a_list) + [w, x_res, gain.reshape(1, d)]
    out_specs = [pl.BlockSpec((tm, d), lambda i: (i, 0)),
                 pl.BlockSpec((tm, d), lambda i: (i, 0))]
    out_shape = [jax.ShapeDtypeStruct((m, d), F32), jax.ShapeDtypeStruct((m, d), BF16)]
    n_experts = 0
    if router is not None:
        wr, n_experts = router
        in_specs.append(pl.BlockSpec((d, 2 * LANES), lambda i: (0, 0)))
        args.append(wr)
        out_specs.append(pl.BlockSpec((tm, LANES), lambda i: (i, 0)))
        out_shape.append(jax.ShapeDtypeStruct((m, LANES), F32))
    vmem = 2 * (tm * k * 2 + k * d * 2 + tm * d * 4 + tm * d * 4 + tm * d * 2) \
        + 3 * tm * d * 4 + 4 * d * LANES * 2
    return pl.pallas_call(
        functools.partial(_mm_res_norm_kernel, n_a=len(a_list),
                          with_router=router is not None, n_experts=n_experts),
        grid=(m // tm,),
        in_specs=in_specs,
        out_specs=out_specs,
        out_shape=out_shape,
        compiler_params=_params(("parallel",), vmem),
        name="matmul_residual_norm",
    )(*args)


def _cross_attn_kernel(h_ref, wq_ref, k_ref, v_ref, o_ref, *, n_heads, scale):
    dh = h_ref.shape[1] // n_heads
    h = h_ref[...]
    heads = [slice(i * dh, (i + 1) * dh) for i in range(n_heads)]
    qs = [jnp.dot(h, wq_ref[:, sl], preferred_element_type=F32).astype(BF16) for sl in heads]
    scores = [lax.dot_general(q, k_ref[:, sl], (((1,), (1,)), ((), ())),
                              preferred_element_type=F32) for q, sl in zip(qs, heads)]
    for sl, e in zip(heads, _softmax_numerators(scores, scale)):
        l = jnp.sum(e, axis=-1, keepdims=True)
        o = jnp.dot(e.astype(BF16), v_ref[:, sl], preferred_element_type=F32) / l
        o_ref[:, sl] = o.astype(o_ref.dtype)


def cross_attention(h, w_q, k, v, *, batch, seq, mem_len):
    d = h.shape[1]
    tq = _tile(seq, 512)
    nq = seq // tq
    vmem = 2 * (2 * tq * d * 2 + d * d * 2 + 2 * mem_len * d * 2) + 6 * tq * mem_len * 4 + 3 * tq * d * 4
    return pl.pallas_call(
        functools.partial(_cross_attn_kernel, n_heads=N_MEM_HEADS,
                          scale=(d // N_MEM_HEADS) ** -0.5),
        grid=(batch, nq),
        in_specs=[pl.BlockSpec((tq, d), lambda b, i: (b * nq + i, 0)),
                  pl.BlockSpec((d, d), lambda b, i: (0, 0)),
                  pl.BlockSpec((mem_len, d), lambda b, i: (b, 0)),
                  pl.BlockSpec((mem_len, d), lambda b, i: (b, 0))],
        out_specs=pl.BlockSpec((tq, d), lambda b, i: (b * nq + i, 0)),
        out_shape=jax.ShapeDtypeStruct((batch * seq, d), BF16),
        compiler_params=_params(("parallel", "parallel"), vmem),
        name="cross_attention",
    )(h, w_q, k, v)


def _prefix_count(x, lane):
    n = x.shape[1]
    shift = 1
    while shift < n:
        x = x + jnp.where(lane >= shift, pltpu.roll(x, shift, axis=1), 0)
        shift *= 2
    return x


def _route_kernel(aff_ref, prow_ref, arow_ref, pcol_ref, *, n_experts, capacity):
    e_pad = prow_ref.shape[1]
    a_t = aff_ref[...].T[:e_pad]
    arow_ref[0] = a_t
    seq = a_t.shape[1]

    def body(i, thr_bits):
        cand = thr_bits | jnp.left_shift(jnp.int32(1), 30 - i)
        cnt = jnp.sum((a_t >= pltpu.bitcast(cand, F32)).astype(F32), axis=1, keepdims=True)
        return jnp.where(cnt >= capacity, cand, thr_bits)

    thr = pltpu.bitcast(lax.fori_loop(0, 31, body, jnp.zeros((e_pad, 1), jnp.int32)), F32)
    lane = lax.broadcasted_iota(jnp.int32, a_t.shape, 1)
    gt = a_t > thr
    eq = a_t == thr
    need = capacity - jnp.sum(gt.astype(F32), axis=1, keepdims=True).astype(jnp.int32)
    sel = gt | (eq & (_prefix_count(eq.astype(jnp.int32), lane) <= need))
    pos = jnp.where(sel, _prefix_count(sel.astype(jnp.int32), lane) - 1, -1)
    row = lax.broadcasted_iota(jnp.int32, a_t.shape, 0)
    pos = jnp.where(row < n_experts, pos, -1)
    prow_ref[0] = pos
    unused = jnp.full((LANES - e_pad, seq), -1.0, F32)
    pcol_ref[...] = jnp.concatenate([pos.astype(F32), unused], axis=0).T


def route(aff, *, batch, seq, n_experts, capacity):
    e_pad = max(8, n_experts)
    return pl.pallas_call(
        functools.partial(_route_kernel, n_experts=n_experts, capacity=capacity),
        grid=(batch,),
        in_specs=[pl.BlockSpec((seq, LANES), lambda b: (b, 0))],
        out_specs=[pl.BlockSpec((1, e_pad, seq), lambda b: (b, 0, 0)),
                   pl.BlockSpec((1, e_pad, seq), lambda b: (b, 0, 0)),
                   pl.BlockSpec((seq, LANES), lambda b: (b, 0))],
        out_shape=[jax.ShapeDtypeStruct((batch, e_pad, seq), jnp.int32),
                   jax.ShapeDtypeStruct((batch, e_pad, seq), F32),
                   jax.ShapeDtypeStruct((batch * seq, LANES), F32)],
        compiler_params=_params(("parallel",), 24 * seq * LANES * 4),
        name="route",
    )(aff)


def _gather_kernel(h_ref, prow_ref, o_ref, *, experts_per_step):
    g = pl.program_id(1)
    cap = o_ref.shape[1]
    seq = h_ref.shape[0]
    slot = lax.broadcasted_iota(jnp.int32, (cap, seq), 0)
    h = h_ref[...]
    for j in range(experts_per_step):
        onehot = (slot == prow_ref[0, pl.ds(g * experts_per_step + j, 1), :]).astype(BF16)
        o_ref[j] = jnp.dot(onehot, h, preferred_element_type=F32).astype(o_ref.dtype)


def gather_tokens(h, prow, *, batch, seq, n_experts, capacity):
    d = h.shape[1]
    e_pad = prow.shape[1]
    ng = _tile(n_experts, 4)
    vmem = 2 * (seq * d * 2 + e_pad * seq * 4 + ng * capacity * d * 2) + ng * capacity * seq * 8 \
        + ng * capacity * d * 4
    return pl.pallas_call(
        functools.partial(_gather_kernel, experts_per_step=ng),
        grid=(batch, n_experts // ng),
        in_specs=[pl.BlockSpec((seq, d), lambda b, g: (b, 0)),
                  pl.BlockSpec((1, e_pad, seq), lambda b, g: (b, 0, 0))],
        out_specs=pl.BlockSpec((ng, None, capacity, d), lambda b, g: (g, b, 0, 0)),
        out_shape=jax.ShapeDtypeStruct((n_experts, batch, capacity, d), BF16),
        compiler_params=_params(("parallel", "arbitrary"), vmem),
        name="gather_tokens",
    )(h, prow)


def _ffn_kernel(x_ref, wg_ref, wu_ref, wd_ref, prow_ref, arow_ref, o_ref, acc_ref, gate_ref,
                *, capacity, gate_steps):
    e = pl.program_id(0)
    f = pl.program_id(1)
    nf = pl.num_programs(1)
    rows = x_ref.shape[0]
    n_seq = rows // capacity

    @pl.when(f == 0)
    def _():
        acc_ref[...] = jnp.zeros_like(acc_ref)

    x = x_ref[...]
    a = jnp.dot(x, wg_ref[...].astype(BF16), preferred_element_type=F32)
    u = jnp.dot(x, wu_ref[...].astype(BF16), preferred_element_type=F32)
    hidden = (a / (1.0 + jnp.exp(-a)) * u).astype(BF16)
    acc_ref[...] += jnp.dot(hidden, wd_ref[...].astype(BF16), preferred_element_type=F32)

    seq = prow_ref.shape[2]
    slot = lax.broadcasted_iota(jnp.int32, (capacity, seq), 0)
    per_step = -(-n_seq // gate_steps)
    for j in range(per_step):
        b = jnp.minimum(f * per_step + j, n_seq - 1)
        mine = slot == prow_ref[b, pl.ds(e, 1), :]
        gate_ref[pl.ds(pl.multiple_of(b * capacity, capacity), capacity), :] = jnp.sum(
            jnp.where(mine, arow_ref[b, pl.ds(e, 1), :], 0.0), axis=1, keepdims=True)

    @pl.when(f == nf - 1)
    def _():
        o_ref[...] = (acc_ref[...] * gate_ref[...]).astype(o_ref.dtype)


def expert_ffn(xs, w_gate, w_up, w_down, prow, arow, layer, *, rows_per_expert, capacity):
    _, n_experts, d, ff = w_gate.shape
    tf = _tile(ff // 2, 256)
    r = rows_per_expert
    batch, e_pad, seq = prow.shape
    vmem = 2 * (r * d * 2 + 3 * d * tf * 4 + r * d * 2 + 2 * batch * e_pad * seq * 4) + r * d * 4 \
        + r * LANES * 4 + 3 * d * tf * 2 + 4 * r * tf * 4 + r * d * 4 + 2 * capacity * seq * 4
    return pl.pallas_call(
        functools.partial(_ffn_kernel, capacity=capacity, gate_steps=ff // tf),
        grid=(n_experts, ff // tf),
        in_specs=[pl.BlockSpec((r, d), lambda e, f: (e, 0)),
                  pl.BlockSpec((None, None, d, tf), lambda e, f: (layer, e, 0, f)),
                  pl.BlockSpec((None, None, d, tf), lambda e, f: (layer, e, 0, f)),
                  pl.BlockSpec((None, None, tf, d), lambda e, f: (layer, e, f, 0)),
                  pl.BlockSpec((batch, e_pad, seq), lambda e, f: (0, 0, 0)),
                  pl.BlockSpec((batch, e_pad, seq), lambda e, f: (0, 0, 0))],
        out_specs=pl.BlockSpec((r, d), lambda e, f: (e, 0)),
        out_shape=jax.ShapeDtypeStruct((n_experts * r, d), BF16),
        scratch_shapes=[pltpu.VMEM((r, d), F32), pltpu.VMEM((r, 1), F32)],
        compiler_params=_params(("parallel", "arbitrary"), vmem),
        name="expert_ffn",
    )(xs, w_gate, w_up, w_down, prow, arow)


def _combine_kernel(x_ref, y_ref, pcol_ref, g_ref, out_ref, *, final_norm):
    g = pl.program_id(2)
    tt, d = x_ref.shape
    n_group, cap, _ = y_ref.shape

    @pl.when(g == 0)
    def _():
        out_ref[...] = x_ref[...]

    lane = lax.broadcasted_iota(jnp.int32, (tt, LANES), 1)
    slot = lax.broadcasted_iota(jnp.int32, (tt, cap), 1).astype(F32)
    pcol = pcol_ref[...]
    total = None
    for j in range(n_group):
        slot_of_token = jnp.sum(jnp.where(lane == g * n_group + j, pcol, 0.0),
                                axis=1, keepdims=True)
        onehot = (slot == slot_of_token).astype(BF16)
        part = jnp.dot(onehot, y_ref[j], preferred_element_type=F32)
        total = part if total is None else total + part
    out_ref[...] += total

    if final_norm:
        @pl.when(g == pl.num_programs(2) - 1)
        def _():
            out_ref[...] = _rms(out_ref[...], g_ref[...])


def combine(x_res, y, pcol, gain, *, batch, seq, n_experts, capacity, final_norm):
    d = x_res.shape[1]
    tt = _tile(seq, 1024)
    nt = seq // tt
    ng = _tile(n_experts, 4)
    y4 = y.reshape(n_experts, batch, capacity, d)
    vmem = 2 * (tt * d * 4 + ng * capacity * d * 2 + tt * LANES * 4 + tt * d * 4) \
        + 4 * tt * d * 4 + 2 * tt * ng * capacity * 4
    return pl.pallas_call(
        functools.partial(_combine_kernel, final_norm=final_norm),
        grid=(batch, nt, n_experts // ng),
        in_specs=[pl.BlockSpec((tt, d), lambda b, t, g: (b * nt + t, 0)),
                  pl.BlockSpec((ng, None, capacity, d), lambda b, t, g: (g, b, 0, 0)),
                  pl.BlockSpec((tt, LANES), lambda b, t, g: (b * nt + t, 0)),
                  pl.BlockSpec((1, d), lambda b, t, g: (0, 0))],
        out_specs=pl.BlockSpec((tt, d), lambda b, t, g: (b * nt + t, 0)),
        out_shape=jax.ShapeDtypeStruct((batch * seq, d), F32),
        compiler_params=_params(("parallel", "parallel", "arbitrary"), vmem),
        name="combine",
    )(x_res, y4, pcol, gain.reshape(1, d))


def _pool_kernel(x_ref, prev_ref, next_ref, ng_ref, w_ref, s_ref, g_ref, xo_ref, ho_ref, pad_ref,
                 *, seq, windows):
    i = pl.program_id(0)
    tm, d = x_ref.shape
    halo = prev_ref.shape[0]
    pg = w_ref.shape[1]
    tiles_per_seq = seq // tm
    tile_in_seq = i % tiles_per_seq
    norm_gain = ng_ref[...]
    h = _rms(x_ref[...], norm_gain)
    pad_ref[0:halo, :] = jnp.where(tile_in_seq == 0, 0.0, _rms(prev_ref[...], norm_gain))
    pad_ref[halo:halo + tm, :] = h
    pad_ref[halo + tm:2 * halo + tm, :] = jnp.where(tile_in_seq == tiles_per_seq - 1, 0.0,
                                                    _rms(next_ref[...], norm_gain))
    t = tile_in_seq * tm + lax.broadcasted_iota(jnp.int32, (tm, 1), 0)
    for gi, w in enumerate(windows):
        cols = slice(gi * pg, (gi + 1) * pg)
        acc = pad_ref[:, cols]
        span = 1
        while span < w:
            acc = acc + pltpu.roll(acc, acc.shape[0] - span, axis=0)
            span *= 2
        total = acc[halo - w // 2:halo - w // 2 + tm, :]
        count = (jnp.minimum(t + (w - w // 2), seq) - jnp.maximum(t - w // 2, 0)).astype(F32)
        pooled = (total / count - h[:, cols]).astype(BF16)
        mixed = jnp.dot(pooled, w_ref[gi], preferred_element_type=F32)
        xo_ref[:, cols] = mixed * s_ref[:, cols] + x_ref[:, cols]
    ho_ref[...] = _rms(xo_ref[...], g_ref[...]).astype(ho_ref.dtype)


def pool_mixer(x, norm_gain, w_pool, pool_scale, next_gain, *, seq):
    m, d = x.shape
    ng, pg, _ = w_pool.shape
    halo = 8
    assert all(w & (w - 1) == 0 and w // 2 <= halo for w in POOL_WINDOWS) and ng == len(POOL_WINDOWS)
    tm = _tile(seq, 256)
    per = tm // halo
    n_halo_blocks = m // halo
    vmem = 2 * (2 * tm * d * 4 + 2 * halo * d * 4 + ng * pg * pg * 2 + tm * d * 2) \
        + (tm + 2 * halo) * d * 4 + 6 * tm * d * 4
    return pl.pallas_call(
        functools.partial(_pool_kernel, seq=seq, windows=POOL_WINDOWS),
        grid=(m // tm,),
        in_specs=[pl.BlockSpec((tm, d), lambda i: (i, 0)),
                  pl.BlockSpec((halo, d), lambda i: (jnp.maximum(i * per - 1, 0), 0)),
                  pl.BlockSpec((halo, d), lambda i: (jnp.minimum((i + 1) * per, n_halo_blocks - 1), 0)),
                  pl.BlockSpec((1, d), lambda i: (0, 0)),
                  pl.BlockSpec((ng, pg, pg), lambda i: (0, 0, 0)),
                  pl.BlockSpec((1, d), lambda i: (0, 0)),
                  pl.BlockSpec((1, d), lambda i: (0, 0))],
        out_specs=[pl.BlockSpec((tm, d), lambda i: (i, 0)),
                   pl.BlockSpec((tm, d), lambda i: (i, 0))],
        out_shape=[jax.ShapeDtypeStruct((m, d), F32), jax.ShapeDtypeStruct((m, d), BF16)],
        scratch_shapes=[pltpu.VMEM((tm + 2 * halo, d), F32)],
        compiler_params=_params(("parallel",), vmem),
        name="pool_mixer",
    )(x, x, x, norm_gain.reshape(1, d), w_pool, pool_scale.reshape(1, d), next_gain.reshape(1, d))


def _rope_tables(seq):
    half = HEAD_DIM // 2
    rows = seq // GRID_W
    row_idx = jnp.repeat(jnp.arange(rows), GRID_W).astype(F32)
    col_idx = jnp.tile(jnp.arange(GRID_W), rows).astype(F32)
    inv_freq = 1.0 / (ROPE_THETA ** (jnp.arange(0, half, 2, dtype=F32) / half))
    ang = jnp.concatenate([row_idx[:, None] * inv_freq[None, :],
                           col_idx[:, None] * inv_freq[None, :]], axis=-1)
    cos, sin = jnp.cos(ang), jnp.sin(ang)
    zero = jnp.zeros_like(sin)
    cos_full = jnp.stack([cos, cos], axis=-1).reshape(seq, HEAD_DIM)
    sin_even = jnp.stack([-sin, zero], axis=-1).reshape(seq, HEAD_DIM)
    sin_odd = jnp.stack([zero, sin], axis=-1).reshape(seq, HEAD_DIM)
    return cos_full, sin_even, sin_odd


def _split_router(w_router):
    d, n_experts = w_router.shape
    w = jnp.pad(w_router, ((0, 0), (0, LANES - n_experts)))
    hi = w.astype(BF16)
    lo = (w - hi.astype(F32)).astype(BF16)
    return jnp.concatenate([hi, lo], axis=1), n_experts


def _moe(x_res, h, aff, w_gate, w_up, w_down, layer, gain, *, batch, seq, final_norm):
    n_experts = w_gate.shape[1]
    capacity = EC_CAPACITY_FACTOR * seq // n_experts
    d = h.shape[1]
    prow, arow, pcol = route(aff, batch=batch, seq=seq, n_experts=n_experts, capacity=capacity)
    xs = gather_tokens(h, prow, batch=batch, seq=seq, n_experts=n_experts, capacity=capacity)
    y = expert_ffn(xs.reshape(n_experts * batch * capacity, d), w_gate, w_up, w_down, prow, arow,
                   layer, rows_per_expert=batch * capacity, capacity=capacity)
    return combine(x_res, y, pcol, gain, batch=batch, seq=seq,
                   n_experts=n_experts, capacity=capacity, final_norm=final_norm)


def _cross_block(x, h, mem2d, layer, cross_w_q, cross_w_k, cross_w_v, cross_w_o, mem_norm,
                 ffn_norm, router_w, *, batch, seq, mem_len):
    mem_n = rms_norm_rows(mem2d, mem_norm[layer], BF16)
    k = matmul_streamed(mem_n, cross_w_k, (layer,))
    v = matmul_streamed(mem_n, cross_w_v, (layer,))
    o = cross_attention(h, cast_weight(cross_w_q, (layer,)), k, v,
                        batch=batch, seq=seq, mem_len=mem_len)
    return matmul_residual_norm([o], cast_weight(cross_w_o, (layer,)), x, ffn_norm[layer],
                                router=_split_router(router_w[layer]))


def kernel(x, mem, mix_norm, attn_w_in, q_gain, k_gain, fourier_w, attn_w_out, pool_w, pool_scale,
           cross_norm, mem_norm, cross_w_q, cross_w_k, cross_w_v, cross_w_o, ffn_norm, router_w,
           expert_w_gate, expert_w_up, expert_w_down, final_norm):
    batch, seq, d = x.shape
    mem_len = mem.shape[1]
    depth = mix_norm.shape[0]
    attn_width = N_Q_HEADS * HEAD_DIM
    kv_width = N_KV_HEADS * HEAD_DIM
    fourier_width = N_FOURIER_GROUPS * FOURIER_GROUP
    rope_width = attn_width + kv_width
    cos_t, sin_a, sin_b = _rope_tables(seq)

    xf = x.reshape(batch * seq, d)
    mem2d = mem.reshape(batch * mem_len, d)
    cross = functools.partial(_cross_block, mem2d=mem2d, cross_w_q=cross_w_q, cross_w_k=cross_w_k,
                              cross_w_v=cross_w_v, cross_w_o=cross_w_o, mem_norm=mem_norm,
                              ffn_norm=ffn_norm, router_w=router_w,
                              batch=batch, seq=seq, mem_len=mem_len)
    for layer in range(depth):
        i = layer // 2
        if layer % 2 == 0:
            gain_cols = jnp.concatenate([jnp.tile(q_gain[i], N_Q_HEADS), jnp.tile(k_gain[i], N_KV_HEADS),
                                         jnp.ones((kv_width + fourier_width,), F32)]).reshape(1, -1)
            proj = proj_in(xf, mix_norm[layer], cast_weight(attn_w_in, (i,)), gain_cols,
                           cos_t, sin_a, sin_b, seq=seq, rope_width=rope_width)
            o_attn = gqa_attention(proj, batch=batch, seq=seq, attn_width=attn_width, kv_width=kv_width)
            o_four = fourier_mixer(proj, fourier_w[i], batch=batch, seq=seq,
                                   col0=attn_width + 2 * kv_width)
            xf, h = matmul_residual_norm([o_attn, o_four], cast_weight(attn_w_out, (i,)), xf,
                                         cross_norm[layer])
        else:
            xf, h = pool_mixer(xf, mix_norm[layer], pool_w[i].astype(BF16), pool_scale[i],
                               cross_norm[layer], seq=seq)
        xf, h, aff = cross(xf, h, layer=layer)
        xf = _moe(xf, h, aff, expert_w_gate, expert_w_up, expert_w_down, layer, final_norm,
                  batch=batch, seq=seq, final_norm=layer == depth - 1)
    return xf.reshape(batch, seq, d)
```

```python
import functools
import math

import jax
import jax.numpy as jnp
from jax import lax
from jax.experimental import pallas as pl
from jax.experimental.pallas import tpu as pltpu

F32 = jnp.float32
BF16 = jnp.bfloat16

GRID_W = 64
HEAD_DIM = 128
N_Q_HEADS = 12
N_KV_HEADS = 4
N_FOURIER_GROUPS = 4
FOURIER_GROUP = 128
ROPE_THETA = 10000.0
POOL_WINDOWS = (2, 4, 8, 16)
N_MEM_HEADS = 4
EC_CAPACITY_FACTOR = 2
NORM_EPS = 1e-6

LANES = 128
V7X_VMEM_BYTES = 64 * 1024 * 1024
VMEM_HEADROOM_BYTES = 6 * 1024 * 1024
COL_CHUNK = 512
LOG2_E = math.log2(math.e)


def _params(semantics, vmem_bytes):
    limit = min(int(vmem_bytes) + VMEM_HEADROOM_BYTES, V7X_VMEM_BYTES - VMEM_HEADROOM_BYTES)
    return pltpu.CompilerParams(dimension_semantics=semantics, vmem_limit_bytes=limit)


def _tile(n, pref):
    t = min(n, pref)
    while n % t:
        t //= 2
    return t


def _rms(x, gain):
    ms = jnp.mean(x * x, axis=-1, keepdims=True)
    return x * lax.rsqrt(ms + NORM_EPS) * gain


def _softmax_numerators(scores, scale):
    c = scale * LOG2_E
    return [jnp.exp2((s - jnp.max(s, axis=-1, keepdims=True)) * c) for s in scores]


def _norm_kernel(x_ref, g_ref, o_ref):
    o_ref[...] = _rms(x_ref[...], g_ref[...]).astype(o_ref.dtype)


def rms_norm_rows(x, gain, out_dtype):
    m, d = x.shape
    tm = _tile(m, 512)
    return pl.pallas_call(
        _norm_kernel,
        grid=(m // tm,),
        in_specs=[pl.BlockSpec((tm, d), lambda i: (i, 0)),
                  pl.BlockSpec((1, d), lambda i: (0, 0))],
        out_specs=pl.BlockSpec((tm, d), lambda i: (i, 0)),
        out_shape=jax.ShapeDtypeStruct((m, d), out_dtype),
        compiler_params=_params(("parallel",), 2 * tm * d * 8),
        name="rms_norm_rows",
    )(x, gain.reshape(1, d))


def _cast_kernel(w_ref, o_ref):
    o_ref[...] = w_ref[...].astype(o_ref.dtype)


def cast_weight(w, lead):
    k, n = w.shape[-2:]
    tk = _tile(k, 1024)
    squeezed = (None,) * len(lead)
    return pl.pallas_call(
        _cast_kernel,
        grid=(k // tk,),
        in_specs=[pl.BlockSpec(squeezed + (tk, n), lambda i: tuple(lead) + (i, 0))],
        out_specs=pl.BlockSpec((tk, n), lambda i: (i, 0)),
        out_shape=jax.ShapeDtypeStruct((k, n), BF16),
        compiler_params=_params(("parallel",), 2 * tk * n * 6),
        name="cast_weight",
    )(w)


def _mm_stream_kernel(a_ref, w_ref, o_ref):
    o_ref[...] = jnp.dot(a_ref[...], w_ref[...].astype(BF16),
                         preferred_element_type=F32).astype(o_ref.dtype)


def matmul_streamed(a, w, lead, out_dtype=BF16):
    m, k = a.shape
    n = w.shape[-1]
    tn = _tile(n, COL_CHUNK)
    squeezed = (None,) * len(lead)
    vmem = 2 * (m * k * 2 + k * tn * 4 + m * tn * 2) + k * tn * 2 + m * tn * 4
    return pl.pallas_call(
        _mm_stream_kernel,
        grid=(n // tn,),
        in_specs=[pl.BlockSpec((m, k), lambda j: (0, 0)),
                  pl.BlockSpec(squeezed + (k, tn), lambda j: tuple(lead) + (0, j))],
        out_specs=pl.BlockSpec((m, tn), lambda j: (0, j)),
        out_shape=jax.ShapeDtypeStruct((m, n), out_dtype),
        compiler_params=_params(("parallel",), vmem),
        name="matmul_streamed",
    )(a, w)


def _proj_in_kernel(a_ref, ng_ref, w_ref, g_ref, cos_ref, sa_ref, sb_ref, o_ref, *, n_rope_heads):
    n = o_ref.shape[1]
    tn = _tile(n, COL_CHUNK)
    a = _rms(a_ref[...], ng_ref[...]).astype(BF16)
    cos, sin_a, sin_b = cos_ref[...], sa_ref[...], sb_ref[...]
    chunks = [jnp.dot(a, w_ref[:, c * tn:(c + 1) * tn], preferred_element_type=F32)
              for c in range(n // tn)]
    for c, acc in enumerate(chunks):
        for h in range(tn // HEAD_DIM):
            head = c * (tn // HEAD_DIM) + h
            col = slice(head * HEAD_DIM, (head + 1) * HEAD_DIM)
            xh = acc[:, h * HEAD_DIM:(h + 1) * HEAD_DIM]
            if head < n_rope_heads:
                ms = jnp.mean(xh * xh, axis=-1, keepdims=True)
                y = xh * lax.rsqrt(ms + NORM_EPS) * g_ref[:, col]
                xh = (y * cos + pltpu.roll(y, HEAD_DIM - 1, axis=1) * sin_a
                      + pltpu.roll(y, 1, axis=1) * sin_b)
            o_ref[:, col] = xh.astype(o_ref.dtype)


def proj_in(x, norm_gain, w_in, gain_cols, cos_t, sin_a, sin_b, *, seq, rope_width):
    m, k = x.shape
    n = w_in.shape[1]
    tm = _tile(seq, 512)
    n_pos_blocks = seq // tm
    vmem = 2 * (tm * k * 4 + k * n * 2 + tm * n * 2 + 3 * tm * HEAD_DIM * 4) + 6 * tm * COL_CHUNK * 4 \
        + 2 * tm * k * 4
    table = pl.BlockSpec((tm, HEAD_DIM), lambda i: (i % n_pos_blocks, 0))
    return pl.pallas_call(
        functools.partial(_proj_in_kernel, n_rope_heads=rope_width // HEAD_DIM),
        grid=(m // tm,),
        in_specs=[pl.BlockSpec((tm, k), lambda i: (i, 0)),
                  pl.BlockSpec((1, k), lambda i: (0, 0)),
                  pl.BlockSpec((k, n), lambda i: (0, 0)),
                  pl.BlockSpec((1, n), lambda i: (0, 0)),
                  table, table, table],
        out_specs=pl.BlockSpec((tm, n), lambda i: (i, 0)),
        out_shape=jax.ShapeDtypeStruct((m, n), BF16),
        compiler_params=_params(("parallel",), vmem),
        name="proj_in",
    )(x, norm_gain.reshape(1, k), w_in, gain_cols, cos_t, sin_a, sin_b)


def _attn_kernel(q_ref, k_ref, v_ref, o_ref, *, group, scale):
    k = k_ref[...]
    v = v_ref[...]
    v_ones = jnp.concatenate([v, jnp.ones_like(v)], axis=1)
    heads = [slice(g * HEAD_DIM, (g + 1) * HEAD_DIM) for g in range(group)]
    scores = [lax.dot_general(q_ref[:, sl], k, (((1,), (1,)), ((), ())),
                              preferred_element_type=F32) for sl in heads]
    for sl, e in zip(heads, _softmax_numerators(scores, scale)):
        ol = jnp.dot(e.astype(BF16), v_ones, preferred_element_type=F32)
        o_ref[:, sl] = (ol[:, :HEAD_DIM] / ol[:, HEAD_DIM:HEAD_DIM + 1]).astype(o_ref.dtype)


def gqa_attention(proj, *, batch, seq, attn_width, kv_width):
    group = N_Q_HEADS // N_KV_HEADS
    gw = group * HEAD_DIM
    tq = _tile(seq, 1024)
    nq = seq // tq
    k_col0 = attn_width // HEAD_DIM
    v_col0 = (attn_width + kv_width) // HEAD_DIM
    vmem = 2 * (tq * gw * 2 * 2 + 2 * seq * HEAD_DIM * 2) + group * tq * seq * 6
    return pl.pallas_call(
        functools.partial(_attn_kernel, group=group, scale=HEAD_DIM ** -0.5),
        grid=(batch, N_KV_HEADS, nq),
        in_specs=[pl.BlockSpec((tq, gw), lambda b, h, i: (b * nq + i, h)),
                  pl.BlockSpec((seq, HEAD_DIM), lambda b, h, i: (b, k_col0 + h)),
                  pl.BlockSpec((seq, HEAD_DIM), lambda b, h, i: (b, v_col0 + h))],
        out_specs=pl.BlockSpec((tq, gw), lambda b, h, i: (b * nq + i, h)),
        out_shape=jax.ShapeDtypeStruct((batch * seq, attn_width), BF16),
        compiler_params=_params(("parallel", "parallel", "parallel"), vmem),
        name="gqa_attention",
    )(proj, proj, proj)


def _fourier_kernel(f_ref, cc_ref, sc_ref, cs_ref, ss_ref, wf_ref, o_ref, ab_ref, *, seq, norm):
    r = pl.program_id(1)
    ng = wf_ref.shape[0]
    cg = wf_ref.shape[1]

    @pl.when(r == 0)
    def _():
        for g in range(ng):
            fg = f_ref[:, g * cg:(g + 1) * cg]
            ab_ref[0:seq, g * cg:(g + 1) * cg] = jnp.dot(
                fg, cc_ref[...], preferred_element_type=F32).astype(BF16)
            ab_ref[seq:2 * seq, g * cg:(g + 1) * cg] = jnp.dot(
                fg, sc_ref[...], preferred_element_type=F32).astype(BF16)

    z = (jnp.dot(cs_ref[...], ab_ref[0:seq, :], preferred_element_type=F32)
         - jnp.dot(ss_ref[...], ab_ref[seq:2 * seq, :], preferred_element_type=F32)) * norm
    for g in range(ng):
        zg = z[:, g * cg:(g + 1) * cg].astype(BF16)
        o_ref[:, g * cg:(g + 1) * cg] = jnp.dot(
            zg, wf_ref[g].astype(BF16), preferred_element_type=F32).astype(o_ref.dtype)


def _dft_cos_sin(n):
    lo = _tile(n, 64)
    hi = n // lo
    j = jnp.arange(n, dtype=jnp.int32)

    def table(k):
        ang = ((k[:, None] * j[None, :]) % n).astype(F32) * (2.0 * math.pi / n)
        return jnp.cos(ang), jnp.sin(ang)

    ca, sa = table(jnp.arange(hi, dtype=jnp.int32) * lo)
    cb, sb = table(jnp.arange(lo, dtype=jnp.int32))
    cos = ca[:, None, :] * cb[None, :, :] - sa[:, None, :] * sb[None, :, :]
    sin = sa[:, None, :] * cb[None, :, :] + ca[:, None, :] * sb[None, :, :]
    return cos.reshape(n, n).astype(BF16), sin.reshape(n, n).astype(BF16)


def fourier_mixer(proj, w_fourier, *, batch, seq, col0):
    ng, cg, _ = w_fourier.shape
    fw = ng * cg
    tr = _tile(seq, 512)
    nr = seq // tr
    cc, sc = _dft_cos_sin(cg)
    cs, ss = _dft_cos_sin(seq)
    norm = 1.0 / math.sqrt(seq * cg)
    vmem = 2 * (seq * fw * 2 + 2 * cg * cg * 2 + 2 * tr * seq * 2 + ng * cg * cg * 4 + tr * fw * 2) \
        + 2 * seq * fw * 2 + 4 * tr * fw * 4
    return pl.pallas_call(
        functools.partial(_fourier_kernel, seq=seq, norm=norm),
        grid=(batch, nr),
        in_specs=[pl.BlockSpec((seq, fw), lambda b, r: (b, col0 // fw)),
                  pl.BlockSpec((cg, cg), lambda b, r: (0, 0)),
                  pl.BlockSpec((cg, cg), lambda b, r: (0, 0)),
                  pl.BlockSpec((tr, seq), lambda b, r: (r, 0)),
                  pl.BlockSpec((tr, seq), lambda b, r: (r, 0)),
                  pl.BlockSpec((ng, cg, cg), lambda b, r: (0, 0, 0))],
        out_specs=pl.BlockSpec((tr, fw), lambda b, r: (b * nr + r, 0)),
        out_shape=jax.ShapeDtypeStruct((batch * seq, fw), BF16),
        scratch_shapes=[pltpu.VMEM((2 * seq, fw), BF16)],
        compiler_params=_params(("parallel", "arbitrary"), vmem),
        name="fourier_mixer",
    )(proj, cc, sc, cs, ss, w_fourier)


def _router_affinity(hn, wr_ref, n_experts):
    hi = hn.astype(BF16)
    lo = (hn - hi.astype(F32)).astype(BF16)
    r_hi = jnp.dot(hi, wr_ref[...], preferred_element_type=F32)
    r_lo = jnp.dot(lo, wr_ref[...], preferred_element_type=F32)
    logits = r_hi[:, :LANES] + r_hi[:, LANES:] + r_lo[:, :LANES]
    lane = lax.broadcasted_iota(jnp.int32, logits.shape, 1)
    logits = jnp.where(lane < n_experts, logits, -jnp.inf)
    e = jnp.exp(logits - jnp.max(logits, axis=-1, keepdims=True))
    return e / jnp.sum(e, axis=-1, keepdims=True)


def _mm_res_norm_kernel(*refs, n_a, with_router, n_experts):
    a_refs = refs[:n_a]
    w_ref, x_ref, g_ref = refs[n_a:n_a + 3]
    pos = n_a + 3
    if with_router:
        wr_ref = refs[pos]
        pos += 1
    xo_ref, ho_ref = refs[pos:pos + 2]
    if with_router:
        aff_ref = refs[pos + 2]
    d = xo_ref.shape[1]
    tn = _tile(d, COL_CHUNK)
    for c in range(d // tn):
        sl = slice(c * tn, (c + 1) * tn)
        acc = x_ref[:, sl]
        row0 = 0
        for a_ref in a_refs:
            ka = a_ref.shape[1]
            acc = acc + jnp.dot(a_ref[...], w_ref[row0:row0 + ka, sl], preferred_element_type=F32)
            row0 += ka
        xo_ref[:, sl] = acc
    hn = _rms(xo_ref[...], g_ref[...])
    ho_ref[...] = hn.astype(ho_ref.dtype)
    if with_router:
        aff_ref[...] = _router_affinity(hn, wr_ref, n_experts)


def matmul_residual_norm(a_list, w, x_res, gain, *, router=None, tm_pref=512):
    m, d = x_res.shape
    k = w.shape[0]
    tm = _tile(m, tm_pref)
    in_specs = [pl.BlockSpec((tm, a.shape[1]), lambda i: (i, 0)) for a in a_list]
    in_specs += [pl.BlockSpec((k, d), lambda i: (0, 0)),
                 pl.BlockSpec((tm, d), lambda i: (i, 0)),
                 pl.BlockSpec((1, d), lambda i: (0, 0))]
    args = list(a_list) + [w, x_res, gain.reshape(1, d)]
    out_specs = [pl.BlockSpec((tm, d), lambda i: (i, 0)),
                 pl.BlockSpec((tm, d), lambda i: (i, 0))]
    out_shape = [jax.ShapeDtypeStruct((m, d), F32), jax.ShapeDtypeStruct((m, d), BF16)]
    n_experts = 0
    if router is not None:
        wr, n_experts = router
        in_specs.append(pl.BlockSpec((d, 2 * LANES), lambda i: (0, 0)))
        args.append(wr)
        out_specs.append(pl.BlockSpec((tm, LANES), lambda i: (i, 0)))
        out_shape.append(jax.ShapeDtypeStruct((m, LANES), F32))
    vmem = 2 * (tm * k * 2 + k * d * 2 + tm * d * 4 + tm * d * 4 + tm * d * 2) \
        + 3 * tm * d * 4 + 4 * d * LANES * 2
    return pl.pallas_call(
        functools.partial(_mm_res_norm_kernel, n_a=len(a_list),
                          with_router=router is not None, n_experts=n_experts),
        grid=(m // tm,),
        in_specs=in_specs,
        out_specs=out_specs,
        out_shape=out_shape,
        compiler_params=_params(("parallel",), vmem),
        name="matmul_residual_norm",
    )(*args)


def _cross_attn_kernel(h_ref, wq_ref, k_ref, v_ref, o_ref, *, n_heads, scale):
    dh = h_ref.shape[1] // n_heads
    h = h_ref[...]
    heads = [slice(i * dh, (i + 1) * dh) for i in range(n_heads)]
    qs = [jnp.dot(h, wq_ref[:, sl], preferred_element_type=F32).astype(BF16) for sl in heads]
    scores = [lax.dot_general(q, k_ref[:, sl], (((1,), (1,)), ((), ())),
                              preferred_element_type=F32) for q, sl in zip(qs, heads)]
    for sl, e in zip(heads, _softmax_numerators(scores, scale)):
        l = jnp.sum(e, axis=-1, keepdims=True)
        o = jnp.dot(e.astype(BF16), v_ref[:, sl], preferred_element_type=F32) / l
        o_ref[:, sl] = o.astype(o_ref.dtype)


def cross_attention(h, w_q, k, v, *, batch, seq, mem_len):
    d = h.shape[1]
    tq = _tile(seq, 512)
    nq = seq // tq
    vmem = 2 * (2 * tq * d * 2 + d * d * 2 + 2 * mem_len * d * 2) + 6 * tq * mem_len * 4 + 3 * tq * d * 4
    return pl.pallas_call(
        functools.partial(_cross_attn_kernel, n_heads=N_MEM_HEADS,
                          scale=(d // N_MEM_HEADS) ** -0.5),
        grid=(batch, nq),
        in_specs=[pl.BlockSpec((tq, d), lambda b, i: (b * nq + i, 0)),
                  pl.BlockSpec((d, d), lambda b, i: (0, 0)),
                  pl.BlockSpec((mem_len, d), lambda b, i: (b, 0)),
                  pl.BlockSpec((mem_len, d), lambda b, i: (b, 0))],
        out_specs=pl.BlockSpec((tq, d), lambda b, i: (b * nq + i, 0)),
        out_shape=jax.ShapeDtypeStruct((batch * seq, d), BF16),
        compiler_params=_params(("parallel", "parallel"), vmem),
        name="cross_attention",
    )(h, w_q, k, v)


def _prefix_count(x, lane):
    n = x.shape[1]
    shift = 1
    while shift < n:
        x = x + jnp.where(lane >= shift, pltpu.roll(x, shift, axis=1), 0)
        shift *= 2
    return x


def _route_kernel(aff_ref, prow_ref, arow_ref, pcol_ref, *, n_experts, capacity):
    e_pad = prow_ref.shape[1]
    a_t = aff_ref[...].T[:e_pad]
    arow_ref[0] = a_t
    seq = a_t.shape[1]

    def body(i, thr_bits):
        cand = thr_bits | jnp.left_shift(jnp.int32(1), 30 - i)
        cnt = jnp.sum((a_t >= pltpu.bitcast(cand, F32)).astype(F32), axis=1, keepdims=True)
        return jnp.where(cnt >= capacity, cand, thr_bits)

    thr = pltpu.bitcast(lax.fori_loop(0, 31, body, jnp.zeros((e_pad, 1), jnp.int32)), F32)
    lane = lax.broadcasted_iota(jnp.int32, a_t.shape, 1)
    gt = a_t > thr
    eq = a_t == thr
    need = capacity - jnp.sum(gt.astype(F32), axis=1, keepdims=True).astype(jnp.int32)
    sel = gt | (eq & (_prefix_count(eq.astype(jnp.int32), lane) <= need))
    pos = jnp.where(sel, _prefix_count(sel.astype(jnp.int32), lane) - 1, -1)
    row = lax.broadcasted_iota(jnp.int32, a_t.shape, 0)
    pos = jnp.where(row < n_experts, pos, -1)
    prow_ref[0] = pos
    unused = jnp.full((LANES - e_pad, seq), -1.0, F32)
    pcol_ref[...] = jnp.concatenate([pos.astype(F32), unused], axis=0).T


def route(aff, *, batch, seq, n_experts, capacity):
    e_pad = max(8, n_experts)
    return pl.pallas_call(
        functools.partial(_route_kernel, n_experts=n_experts, capacity=capacity),
        grid=(batch,),
        in_specs=[pl.BlockSpec((seq, LANES), lambda b: (b, 0))],
        out_specs=[pl.BlockSpec((1, e_pad, seq), lambda b: (b, 0, 0)),
                   pl.BlockSpec((1, e_pad, seq), lambda b: (b, 0, 0)),
                   pl.BlockSpec((seq, LANES), lambda b: (b, 0))],
        out_shape=[jax.ShapeDtypeStruct((batch, e_pad, seq), jnp.int32),
                   jax.ShapeDtypeStruct((batch, e_pad, seq), F32),
                   jax.ShapeDtypeStruct((batch * seq, LANES), F32)],
        compiler_params=_params(("parallel",), 24 * seq * LANES * 4),
        name="route",
    )(aff)


def _gather_kernel(h_ref, prow_ref, o_ref, *, experts_per_step):
    g = pl.program_id(1)
    cap = o_ref.shape[1]
    seq = h_ref.shape[0]
    slot = lax.broadcasted_iota(jnp.int32, (cap, seq), 0)
    h = h_ref[...]
    for j in range(experts_per_step):
        onehot = (slot == prow_ref[0, pl.ds(g * experts_per_step + j, 1), :]).astype(BF16)
        o_ref[j] = jnp.dot(onehot, h, preferred_element_type=F32).astype(o_ref.dtype)


def gather_tokens(h, prow, *, batch, seq, n_experts, capacity):
    d = h.shape[1]
    e_pad = prow.shape[1]
    ng = _tile(n_experts, 4)
    vmem = 2 * (seq * d * 2 + e_pad * seq * 4 + ng * capacity * d * 2) + ng * capacity * seq * 8 \
        + ng * capacity * d * 4
    return pl.pallas_call(
        functools.partial(_gather_kernel, experts_per_step=ng),
        grid=(batch, n_experts // ng),
        in_specs=[pl.BlockSpec((seq, d), lambda b, g: (b, 0)),
                  pl.BlockSpec((1, e_pad, seq), lambda b, g: (b, 0, 0))],
        out_specs=pl.BlockSpec((ng, None, capacity, d), lambda b, g: (g, b, 0, 0)),
        out_shape=jax.ShapeDtypeStruct((n_experts, batch, capacity, d), BF16),
        compiler_params=_params(("parallel", "arbitrary"), vmem),
        name="gather_tokens",
    )(h, prow)


def _ffn_kernel(x_ref, wg_ref, wu_ref, wd_ref, prow_ref, arow_ref, o_ref, acc_ref, gate_ref,
                *, capacity, gate_steps):
    e = pl.program_id(0)
    f = pl.program_id(1)
    nf = pl.num_programs(1)
    rows = x_ref.shape[0]
    n_seq = rows // capacity

    def chunk_product():
        x = x_ref[...]
        a = jnp.dot(x, wg_ref[...].astype(BF16), preferred_element_type=F32)
        u = jnp.dot(x, wu_ref[...].astype(BF16), preferred_element_type=F32)
        hidden = (a / (1.0 + jnp.exp(-a)) * u).astype(BF16)
        return jnp.dot(hidden, wd_ref[...].astype(BF16), preferred_element_type=F32)

    def gate_share():
        seq = prow_ref.shape[2]
        slot = lax.broadcasted_iota(jnp.int32, (capacity, seq), 0)
        per_step = -(-n_seq // gate_steps)
        for j in range(per_step):
            b = jnp.minimum(f * per_step + j, n_seq - 1)
            mine = slot == prow_ref[b, pl.ds(e, 1), :]
            gate_ref[pl.ds(pl.multiple_of(b * capacity, capacity), capacity), :] = jnp.sum(
                jnp.where(mine, arow_ref[b, pl.ds(e, 1), :], 0.0), axis=1, keepdims=True)

    @pl.when(f == 0)
    def _():
        gate_share()
        acc_ref[...] = chunk_product()

    @pl.when((f > 0) & (f < nf - 1))
    def _():
        gate_share()
        acc_ref[...] += chunk_product()

    @pl.when(f == nf - 1)
    def _():
        gate_share()
        o_ref[...] = ((acc_ref[...] + chunk_product()) * gate_ref[...]).astype(o_ref.dtype)


def expert_ffn(xs, w_gate, w_up, w_down, prow, arow, layer, *, rows_per_expert, capacity):
    _, n_experts, d, ff = w_gate.shape
    tf = _tile(ff // 2, 256)
    r = rows_per_expert
    batch, e_pad, seq = prow.shape
    vmem = 2 * (r * d * 2 + 3 * d * tf * 4 + r * d * 2 + 2 * batch * e_pad * seq * 4) + r * d * 4 \
        + r * LANES * 4 + 3 * d * tf * 2 + 4 * r * tf * 4 + r * d * 4 + 2 * capacity * seq * 4
    return pl.pallas_call(
        functools.partial(_ffn_kernel, capacity=capacity, gate_steps=ff // tf),
        grid=(n_experts, ff // tf),
        in_specs=[pl.BlockSpec((r, d), lambda e, f: (e, 0)),
                  pl.BlockSpec((None, None, d, tf), lambda e, f: (layer, e, 0, f)),
                  pl.BlockSpec((None, None, d, tf), lambda e, f: (layer, e, 0, f)),
                  pl.BlockSpec((None, None, tf, d), lambda e, f: (layer, e, f, 0)),
                  pl.BlockSpec((batch, e_pad, seq), lambda e, f: (0, 0, 0)),
                  pl.BlockSpec((batch, e_pad, seq), lambda e, f: (0, 0, 0))],
        out_specs=pl.BlockSpec((r, d), lambda e, f: (e, 0)),
        out_shape=jax.ShapeDtypeStruct((n_experts * r, d), BF16),
        scratch_shapes=[pltpu.VMEM((r, d), F32), pltpu.VMEM((r, 1), F32)],
        compiler_params=_params(("parallel", "arbitrary"), vmem),
        name="expert_ffn",
    )(xs, w_gate, w_up, w_down, prow, arow)


def _combine_kernel(x_ref, y_ref, pcol_ref, g_ref, out_ref, *, final_norm):
    g = pl.program_id(2)
    tt, d = x_ref.shape
    n_group, cap, _ = y_ref.shape

    @pl.when(g == 0)
    def _():
        out_ref[...] = x_ref[...]

    lane = lax.broadcasted_iota(jnp.int32, (tt, LANES), 1)
    slot = lax.broadcasted_iota(jnp.int32, (tt, cap), 1).astype(F32)
    pcol = pcol_ref[...]
    total = None
    for j in range(n_group):
        slot_of_token = jnp.sum(jnp.where(lane == g * n_group + j, pcol, 0.0),
                                axis=1, keepdims=True)
        onehot = (slot == slot_of_token).astype(BF16)
        part = jnp.dot(onehot, y_ref[j], preferred_element_type=F32)
        total = part if total is None else total + part
    out_ref[...] += total

    if final_norm:
        @pl.when(g == pl.num_programs(2) - 1)
        def _():
            out_ref[...] = _rms(out_ref[...], g_ref[...])


def combine(x_res, y, pcol, gain, *, batch, seq, n_experts, capacity, final_norm):
    d = x_res.shape[1]
    tt = _tile(seq, 1024)
    nt = seq // tt
    ng = _tile(n_experts, 4)
    y4 = y.reshape(n_experts, batch, capacity, d)
    vmem = 2 * (tt * d * 4 + ng * capacity * d * 2 + tt * LANES * 4 + tt * d * 4) \
        + 4 * tt * d * 4 + 2 * tt * ng * capacity * 4
    return pl.pallas_call(
        functools.partial(_combine_kernel, final_norm=final_norm),
        grid=(batch, nt, n_experts // ng),
        in_specs=[pl.BlockSpec((tt, d), lambda b, t, g: (b * nt + t, 0)),
                  pl.BlockSpec((ng, None, capacity, d), lambda b, t, g: (g, b, 0, 0)),
                  pl.BlockSpec((tt, LANES), lambda b, t, g: (b * nt + t, 0)),
                  pl.BlockSpec((1, d), lambda b, t, g: (0, 0))],
        out_specs=pl.BlockSpec((tt, d), lambda b, t, g: (b * nt + t, 0)),
        out_shape=jax.ShapeDtypeStruct((batch * seq, d), F32),
        compiler_params=_params(("parallel", "parallel", "arbitrary"), vmem),
        name="combine",
    )(x_res, y4, pcol, gain.reshape(1, d))


def _pool_kernel(x_ref, prev_ref, next_ref, ng_ref, w_ref, s_ref, g_ref, xo_ref, ho_ref, pad_ref,
                 *, seq, windows):
    i = pl.program_id(0)
    tm, d = x_ref.shape
    halo = prev_ref.shape[0]
    pg = w_ref.shape[1]
    tiles_per_seq = seq // tm
    tile_in_seq = i % tiles_per_seq
    norm_gain = ng_ref[...]
    h = _rms(x_ref[...], norm_gain)
    pad_ref[0:halo, :] = jnp.where(tile_in_seq == 0, 0.0, _rms(prev_ref[...], norm_gain))
    pad_ref[halo:halo + tm, :] = h
    pad_ref[halo + tm:2 * halo + tm, :] = jnp.where(tile_in_seq == tiles_per_seq - 1, 0.0,
                                                    _rms(next_ref[...], norm_gain))
    t = tile_in_seq * tm + lax.broadcasted_iota(jnp.int32, (tm, 1), 0)
    for gi, w in enumerate(windows):
        cols = slice(gi * pg, (gi + 1) * pg)
        acc = pad_ref[:, cols]
        span = 1
        while span < w:
            acc = acc + pltpu.roll(acc, acc.shape[0] - span, axis=0)
            span *= 2
        total = acc[halo - w // 2:halo - w // 2 + tm, :]
        count = (jnp.minimum(t + (w - w // 2), seq) - jnp.maximum(t - w // 2, 0)).astype(F32)
        pooled = (total / count - h[:, cols]).astype(BF16)
        mixed = jnp.dot(pooled, w_ref[gi], preferred_element_type=F32)
        xo_ref[:, cols] = mixed * s_ref[:, cols] + x_ref[:, cols]
    ho_ref[...] = _rms(xo_ref[...], g_ref[...]).astype(ho_ref.dtype)


def pool_mixer(x, norm_gain, w_pool, pool_scale, next_gain, *, seq):
    m, d = x.shape
    ng, pg, _ = w_pool.shape
    halo = 8
    assert all(w & (w - 1) == 0 and w // 2 <= halo for w in POOL_WINDOWS) and ng == len(POOL_WINDOWS)
    tm = _tile(seq, 256)
    per = tm // halo
    n_halo_blocks = m // halo
    vmem = 2 * (2 * tm * d * 4 + 2 * halo * d * 4 + ng * pg * pg * 2 + tm * d * 2) \
        + (tm + 2 * halo) * d * 4 + 6 * tm * d * 4
    return pl.pallas_call(
        functools.partial(_pool_kernel, seq=seq, windows=POOL_WINDOWS),
        grid=(m // tm,),
        in_specs=[pl.BlockSpec((tm, d), lambda i: (i, 0)),
                  pl.BlockSpec((halo, d), lambda i: (jnp.maximum(i * per - 1, 0), 0)),
                  pl.BlockSpec((halo, d), lambda i: (jnp.minimum((i + 1) * per, n_halo_blocks - 1), 0)),
                  pl.BlockSpec((1, d), lambda i: (0, 0)),
                  pl.BlockSpec((ng, pg, pg), lambda i: (0, 0, 0)),
                  pl.BlockSpec((1, d), lambda i: (0, 0)),
                  pl.BlockSpec((1, d), lambda i: (0, 0))],
        out_specs=[pl.BlockSpec((tm, d), lambda i: (i, 0)),
                   pl.BlockSpec((tm, d), lambda i: (i, 0))],
        out_shape=[jax.ShapeDtypeStruct((m, d), F32), jax.ShapeDtypeStruct((m, d), BF16)],
        scratch_shapes=[pltpu.VMEM((tm + 2 * halo, d), F32)],
        compiler_params=_params(("parallel",), vmem),
        name="pool_mixer",
    )(x, x, x, norm_gain.reshape(1, d), w_pool, pool_scale.reshape(1, d), next_gain.reshape(1, d))


def _rope_tables(seq):
    half = HEAD_DIM // 2
    rows = seq // GRID_W
    row_idx = jnp.repeat(jnp.arange(rows), GRID_W).astype(F32)
    col_idx = jnp.tile(jnp.arange(GRID_W), rows).astype(F32)
    inv_freq = 1.0 / (ROPE_THETA ** (jnp.arange(0, half, 2, dtype=F32) / half))
    ang = jnp.concatenate([row_idx[:, None] * inv_freq[None, :],
                           col_idx[:, None] * inv_freq[None, :]], axis=-1)
    cos, sin = jnp.cos(ang), jnp.sin(ang)
    zero = jnp.zeros_like(sin)
    cos_full = jnp.stack([cos, cos], axis=-1).reshape(seq, HEAD_DIM)
    sin_even = jnp.stack([-sin, zero], axis=-1).reshape(seq, HEAD_DIM)
    sin_odd = jnp.stack([zero, sin], axis=-1).reshape(seq, HEAD_DIM)
    return cos_full, sin_even, sin_odd


def _split_router(w_router):
    d, n_experts = w_router.shape
    w = jnp.pad(w_router, ((0, 0), (0, LANES - n_experts)))
    hi = w.astype(BF16)
    lo = (w - hi.astype(F32)).astype(BF16)
    return jnp.concatenate([hi, lo], axis=1), n_experts


def _moe(x_res, h, aff, w_gate, w_up, w_down, layer, gain, *, batch, seq, final_norm):
    n_experts = w_gate.shape[1]
    capacity = EC_CAPACITY_FACTOR * seq // n_experts
    d = h.shape[1]
    prow, arow, pcol = route(aff, batch=batch, seq=seq, n_experts=n_experts, capacity=capacity)
    xs = gather_tokens(h, prow, batch=batch, seq=seq, n_experts=n_experts, capacity=capacity)
    y = expert_ffn(xs.reshape(n_experts * batch * capacity, d), w_gate, w_up, w_down, prow, arow,
                   layer, rows_per_expert=batch * capacity, capacity=capacity)
    return combine(x_res, y, pcol, gain, batch=batch, seq=seq,
                   n_experts=n_experts, capacity=capacity, final_norm=final_norm)


def _cross_block(x, h, mem2d, layer, cross_w_q, cross_w_k, cross_w_v, cross_w_o, mem_norm,
                 ffn_norm, router_w, *, batch, seq, mem_len):
    mem_n = rms_norm_rows(mem2d, mem_norm[layer], BF16)
    k = matmul_streamed(mem_n, cross_w_k, (layer,))
    v = matmul_streamed(mem_n, cross_w_v, (layer,))
    o = cross_attention(h, cast_weight(cross_w_q, (layer,)), k, v,
                        batch=batch, seq=seq, mem_len=mem_len)
    return matmul_residual_norm([o], cast_weight(cross_w_o, (layer,)), x, ffn_norm[layer],
                                router=_split_router(router_w[layer]))


def kernel(x, mem, mix_norm, attn_w_in, q_gain, k_gain, fourier_w, attn_w_out, pool_w, pool_scale,
           cross_norm, mem_norm, cross_w_q, cross_w_k, cross_w_v, cross_w_o, ffn_norm, router_w,
           expert_w_gate, expert_w_up, expert_w_down, final_norm):
    batch, seq, d = x.shape
    mem_len = mem.shape[1]
    depth = mix_norm.shape[0]
    attn_width = N_Q_HEADS * HEAD_DIM
    kv_width = N_KV_HEADS * HEAD_DIM
    fourier_width = N_FOURIER_GROUPS * FOURIER_GROUP
    rope_width = attn_width + kv_width
    cos_t, sin_a, sin_b = _rope_tables(seq)

    xf = x.reshape(batch * seq, d)
    mem2d = mem.reshape(batch * mem_len, d)
    cross = functools.partial(_cross_block, mem2d=mem2d, cross_w_q=cross_w_q, cross_w_k=cross_w_k,
                              cross_w_v=cross_w_v, cross_w_o=cross_w_o, mem_norm=mem_norm,
                              ffn_norm=ffn_norm, router_w=router_w,
                              batch=batch, seq=seq, mem_len=mem_len)
    for layer in range(depth):
        i = layer // 2
        if layer % 2 == 0:
            gain_cols = jnp.concatenate([jnp.tile(q_gain[i], N_Q_HEADS), jnp.tile(k_gain[i], N_KV_HEADS),
                                         jnp.ones((kv_width + fourier_width,), F32)]).reshape(1, -1)
            proj = proj_in(xf, mix_norm[layer], cast_weight(attn_w_in, (i,)), gain_cols,
                           cos_t, sin_a, sin_b, seq=seq, rope_width=rope_width)
            o_attn = gqa_attention(proj, batch=batch, seq=seq, attn_width=attn_width, kv_width=kv_width)
            o_four = fourier_mixer(proj, fourier_w[i], batch=batch, seq=seq,
                                   col0=attn_width + 2 * kv_width)
            xf, h = matmul_residual_norm([o_attn, o_four], cast_weight(attn_w_out, (i,)), xf,
                                         cross_norm[layer])
        else:
            xf, h = pool_mixer(xf, mix_norm[layer], pool_w[i].astype(BF16), pool_scale[i],
                               cross_norm[layer], seq=seq)
        xf, h, aff = cross(xf, h, layer=layer)
        xf = _moe(xf, h, aff, expert_w_gate, expert_w_up, expert_w_down, layer, final_norm,
                  batch=batch, seq=seq, final_norm=layer == depth - 1)
    return xf.reshape(batch, seq, d)
```

```python
import functools
import math

import jax
import jax.numpy as jnp
from jax import lax
from jax.experimental import pallas as pl
from jax.experimental.pallas import tpu as pltpu

F32 = jnp.float32
BF16 = jnp.bfloat16

GRID_W = 64
HEAD_DIM = 128
N_Q_HEADS = 12
N_KV_HEADS = 4
N_FOURIER_GROUPS = 4
FOURIER_GROUP = 128
ROPE_THETA = 10000.0
POOL_WINDOWS = (2, 4, 8, 16)
N_MEM_HEADS = 4
EC_CAPACITY_FACTOR = 2
NORM_EPS = 1e-6

LANES = 128
V7X_VMEM_BYTES = 64 * 1024 * 1024
VMEM_HEADROOM_BYTES = 6 * 1024 * 1024
COL_CHUNK = 512
LOG2_E = math.log2(math.e)


def _params(semantics, vmem_bytes):
    limit = min(int(vmem_bytes) + VMEM_HEADROOM_BYTES, V7X_VMEM_BYTES - VMEM_HEADROOM_BYTES)
    return pltpu.CompilerParams(dimension_semantics=semantics, vmem_limit_bytes=limit)


def _tile(n, pref):
    t = min(n, pref)
    while n % t:
        t //= 2
    return t


def _rms(x, gain):
    ms = jnp.mean(x * x, axis=-1, keepdims=True)
    return x * lax.rsqrt(ms + NORM_EPS) * gain


def _softmax_numerators(scores, scale):
    c = scale * LOG2_E
    return [jnp.exp2((s - jnp.max(s, axis=-1, keepdims=True)) * c) for s in scores]


def _norm_kernel(x_ref, g_ref, o_ref):
    o_ref[...] = _rms(x_ref[...], g_ref[...]).astype(o_ref.dtype)


def rms_norm_rows(x, gain, out_dtype):
    m, d = x.shape
    tm = _tile(m, 512)
    return pl.pallas_call(
        _norm_kernel,
        grid=(m // tm,),
        in_specs=[pl.BlockSpec((tm, d), lambda i: (i, 0)),
                  pl.BlockSpec((1, d), lambda i: (0, 0))],
        out_specs=pl.BlockSpec((tm, d), lambda i: (i, 0)),
        out_shape=jax.ShapeDtypeStruct((m, d), out_dtype),
        compiler_params=_params(("parallel",), 2 * tm * d * 8),
        name="rms_norm_rows",
    )(x, gain.reshape(1, d))


def _cast_kernel(w_ref, o_ref):
    o_ref[...] = w_ref[...].astype(o_ref.dtype)


def cast_weight(w, lead):
    k, n = w.shape[-2:]
    tk = _tile(k, 1024)
    squeezed = (None,) * len(lead)
    return pl.pallas_call(
        _cast_kernel,
        grid=(k // tk,),
        in_specs=[pl.BlockSpec(squeezed + (tk, n), lambda i: tuple(lead) + (i, 0))],
        out_specs=pl.BlockSpec((tk, n), lambda i: (i, 0)),
        out_shape=jax.ShapeDtypeStruct((k, n), BF16),
        compiler_params=_params(("parallel",), 2 * tk * n * 6),
        name="cast_weight",
    )(w)


def _mm_stream_kernel(a_ref, w_ref, o_ref):
    o_ref[...] = jnp.dot(a_ref[...], w_ref[...].astype(BF16),
                         preferred_element_type=F32).astype(o_ref.dtype)


def matmul_streamed(a, w, lead, out_dtype=BF16):
    m, k = a.shape
    n = w.shape[-1]
    tn = _tile(n, COL_CHUNK)
    squeezed = (None,) * len(lead)
    vmem = 2 * (m * k * 2 + k * tn * 4 + m * tn * 2) + k * tn * 2 + m * tn * 4
    return pl.pallas_call(
        _mm_stream_kernel,
        grid=(n // tn,),
        in_specs=[pl.BlockSpec((m, k), lambda j: (0, 0)),
                  pl.BlockSpec(squeezed + (k, tn), lambda j: tuple(lead) + (0, j))],
        out_specs=pl.BlockSpec((m, tn), lambda j: (0, j)),
        out_shape=jax.ShapeDtypeStruct((m, n), out_dtype),
        compiler_params=_params(("parallel",), vmem),
        name="matmul_streamed",
    )(a, w)


def _proj_in_kernel(a_ref, ng_ref, w_ref, g_ref, cos_ref, sa_ref, sb_ref, o_ref, *, n_rope_heads):
    n = o_ref.shape[1]
    tn = _tile(n, COL_CHUNK)
    a = _rms(a_ref[...], ng_ref[...]).astype(BF16)
    cos, sin_a, sin_b = cos_ref[...], sa_ref[...], sb_ref[...]
    chunks = [jnp.dot(a, w_ref[:, c * tn:(c + 1) * tn], preferred_element_type=F32)
              for c in range(n // tn)]
    for c, acc in enumerate(chunks):
        for h in range(tn // HEAD_DIM):
            head = c * (tn // HEAD_DIM) + h
            col = slice(head * HEAD_DIM, (head + 1) * HEAD_DIM)
            xh = acc[:, h * HEAD_DIM:(h + 1) * HEAD_DIM]
            if head < n_rope_heads:
                ms = jnp.mean(xh * xh, axis=-1, keepdims=True)
                y = xh * lax.rsqrt(ms + NORM_EPS) * g_ref[:, col]
                xh = (y * cos + pltpu.roll(y, HEAD_DIM - 1, axis=1) * sin_a
                      + pltpu.roll(y, 1, axis=1) * sin_b)
            o_ref[:, col] = xh.astype(o_ref.dtype)


def proj_in(x, norm_gain, w_in, gain_cols, cos_t, sin_a, sin_b, *, seq, rope_width):
    m, k = x.shape
    n = w_in.shape[1]
    tm = _tile(seq, 512)
    n_pos_blocks = seq // tm
    vmem = 2 * (tm * k * 4 + k * n * 2 + tm * n * 2 + 3 * tm * HEAD_DIM * 4) + 6 * tm * COL_CHUNK * 4 \
        + 2 * tm * k * 4
    table = pl.BlockSpec((tm, HEAD_DIM), lambda i: (i % n_pos_blocks, 0))
    return pl.pallas_call(
        functools.partial(_proj_in_kernel, n_rope_heads=rope_width // HEAD_DIM),
        grid=(m // tm,),
        in_specs=[pl.BlockSpec((tm, k), lambda i: (i, 0)),
                  pl.BlockSpec((1, k), lambda i: (0, 0)),
                  pl.BlockSpec((k, n), lambda i: (0, 0)),
                  pl.BlockSpec((1, n), lambda i: (0, 0)),
                  table, table, table],
        out_specs=pl.BlockSpec((tm, n), lambda i: (i, 0)),
        out_shape=jax.ShapeDtypeStruct((m, n), BF16),
        compiler_params=_params(("parallel",), vmem),
        name="proj_in",
    )(x, norm_gain.reshape(1, k), w_in, gain_cols, cos_t, sin_a, sin_b)


def _attn_kernel(q_ref, k_ref, v_ref, o_ref, *, group, scale):
    k = k_ref[...]
    v = v_ref[...]
    v_ones = jnp.concatenate([v, jnp.ones_like(v)], axis=1)
    heads = [slice(g * HEAD_DIM, (g + 1) * HEAD_DIM) for g in range(group)]
    scores = [lax.dot_general(q_ref[:, sl], k, (((1,), (1,)), ((), ())),
                              preferred_element_type=F32) for sl in heads]
    for sl, e in zip(heads, _softmax_numerators(scores, scale)):
        ol = jnp.dot(e.astype(BF16), v_ones, preferred_element_type=F32)
        o_ref[:, sl] = (ol[:, :HEAD_DIM] / ol[:, HEAD_DIM:HEAD_DIM + 1]).astype(o_ref.dtype)


def gqa_attention(proj, *, batch, seq, attn_width, kv_width):
    group = N_Q_HEADS // N_KV_HEADS
    gw = group * HEAD_DIM
    tq = _tile(seq, 1024)
    nq = seq // tq
    k_col0 = attn_width // HEAD_DIM
    v_col0 = (attn_width + kv_width) // HEAD_DIM
    vmem = 2 * (tq * gw * 2 * 2 + 2 * seq * HEAD_DIM * 2) + group * tq * seq * 6
    return pl.pallas_call(
        functools.partial(_attn_kernel, group=group, scale=HEAD_DIM ** -0.5),
        grid=(batch, N_KV_HEADS, nq),
        in_specs=[pl.BlockSpec((tq, gw), lambda b, h, i: (b * nq + i, h)),
                  pl.BlockSpec((seq, HEAD_DIM), lambda b, h, i: (b, k_col0 + h)),
                  pl.BlockSpec((seq, HEAD_DIM), lambda b, h, i: (b, v_col0 + h))],
        out_specs=pl.BlockSpec((tq, gw), lambda b, h, i: (b * nq + i, h)),
        out_shape=jax.ShapeDtypeStruct((batch * seq, attn_width), BF16),
        compiler_params=_params(("parallel", "parallel", "parallel"), vmem),
        name="gqa_attention",
    )(proj, proj, proj)


def _fourier_kernel(f_ref, cc_ref, sc_ref, cs_ref, ss_ref, wf_ref, o_ref, ab_ref, *, seq, norm):
    r = pl.program_id(1)
    ng = wf_ref.shape[0]
    cg = wf_ref.shape[1]

    @pl.when(r == 0)
    def _():
        for g in range(ng):
            fg = f_ref[:, g * cg:(g + 1) * cg]
            ab_ref[0:seq, g * cg:(g + 1) * cg] = jnp.dot(
                fg, cc_ref[...], preferred_element_type=F32).astype(BF16)
            ab_ref[seq:2 * seq, g * cg:(g + 1) * cg] = jnp.dot(
                fg, sc_ref[...], preferred_element_type=F32).astype(BF16)

    z = (jnp.dot(cs_ref[...], ab_ref[0:seq, :], preferred_element_type=F32)
         - jnp.dot(ss_ref[...], ab_ref[seq:2 * seq, :], preferred_element_type=F32)) * norm
    for g in range(ng):
        zg = z[:, g * cg:(g + 1) * cg].astype(BF16)
        o_ref[:, g * cg:(g + 1) * cg] = jnp.dot(
            zg, wf_ref[g].astype(BF16), preferred_element_type=F32).astype(o_ref.dtype)


def _dft_cos_sin(n):
    lo = _tile(n, 64)
    hi = n // lo
    j = jnp.arange(n, dtype=jnp.int32)

    def table(k):
        ang = ((k[:, None] * j[None, :]) % n).astype(F32) * (2.0 * math.pi / n)
        return jnp.cos(ang), jnp.sin(ang)

    ca, sa = table(jnp.arange(hi, dtype=jnp.int32) * lo)
    cb, sb = table(jnp.arange(lo, dtype=jnp.int32))
    cos = ca[:, None, :] * cb[None, :, :] - sa[:, None, :] * sb[None, :, :]
    sin = sa[:, None, :] * cb[None, :, :] + ca[:, None, :] * sb[None, :, :]
    return cos.reshape(n, n).astype(BF16), sin.reshape(n, n).astype(BF16)


def fourier_mixer(proj, w_fourier, *, batch, seq, col0):
    ng, cg, _ = w_fourier.shape
    fw = ng * cg
    tr = _tile(seq, 512)
    nr = seq // tr
    cc, sc = _dft_cos_sin(cg)
    cs, ss = _dft_cos_sin(seq)
    norm = 1.0 / math.sqrt(seq * cg)
    vmem = 2 * (seq * fw * 2 + 2 * cg * cg * 2 + 2 * tr * seq * 2 + ng * cg * cg * 4 + tr * fw * 2) \
        + 2 * seq * fw * 2 + 4 * tr * fw * 4
    return pl.pallas_call(
        functools.partial(_fourier_kernel, seq=seq, norm=norm),
        grid=(batch, nr),
        in_specs=[pl.BlockSpec((seq, fw), lambda b, r: (b, col0 // fw)),
                  pl.BlockSpec((cg, cg), lambda b, r: (0, 0)),
                  pl.BlockSpec((cg, cg), lambda b, r: (0, 0)),
                  pl.BlockSpec((tr, seq), lambda b, r: (r, 0)),
                  pl.BlockSpec((tr, seq), lambda b, r: (r, 0)),
                  pl.BlockSpec((ng, cg, cg), lambda b, r: (0, 0, 0))],
        out_specs=pl.BlockSpec((tr, fw), lambda b, r: (b * nr + r, 0)),
        out_shape=jax.ShapeDtypeStruct((batch * seq, fw), BF16),
        scratch_shapes=[pltpu.VMEM((2 * seq, fw), BF16)],
        compiler_params=_params(("parallel", "arbitrary"), vmem),
        name="fourier_mixer",
    )(proj, cc, sc, cs, ss, w_fourier)


def _router_affinity(hn, wr_ref, n_experts):
    hi = hn.astype(BF16)
    lo = (hn - hi.astype(F32)).astype(BF16)
    r_hi = jnp.dot(hi, wr_ref[...], preferred_element_type=F32)
    r_lo = jnp.dot(lo, wr_ref[...], preferred_element_type=F32)
    logits = r_hi[:, :LANES] + r_hi[:, LANES:] + r_lo[:, :LANES]
    lane = lax.broadcasted_iota(jnp.int32, logits.shape, 1)
    logits = jnp.where(lane < n_experts, logits, -jnp.inf)
    e = jnp.exp(logits - jnp.max(logits, axis=-1, keepdims=True))
    return e / jnp.sum(e, axis=-1, keepdims=True)


def _mm_res_norm_kernel(*refs, n_a, with_router, n_experts):
    a_refs = refs[:n_a]
    w_ref, x_ref, g_ref = refs[n_a:n_a + 3]
    pos = n_a + 3
    if with_router:
        wr_ref = refs[pos]
        pos += 1
    xo_ref, ho_ref = refs[pos:pos + 2]
    if with_router:
        aff_ref = refs[pos + 2]
    d = xo_ref.shape[1]
    tn = _tile(d, COL_CHUNK)
    for c in range(d // tn):
        sl = slice(c * tn, (c + 1) * tn)
        acc = x_ref[:, sl]
        row0 = 0
        for a_ref in a_refs:
            ka = a_ref.shape[1]
            acc = acc + jnp.dot(a_ref[...], w_ref[row0:row0 + ka, sl], preferred_element_type=F32)
            row0 += ka
        xo_ref[:, sl] = acc
    hn = _rms(xo_ref[...], g_ref[...])
    ho_ref[...] = hn.astype(ho_ref.dtype)
    if with_router:
        aff_ref[...] = _router_affinity(hn, wr_ref, n_experts)


def matmul_residual_norm(a_list, w, x_res, gain, *, router=None, tm_pref=512):
    m, d = x_res.shape
    k = w.shape[0]
    tm = _tile(m, tm_pref)
    in_specs = [pl.BlockSpec((tm, a.shape[1]), lambda i: (i, 0)) for a in a_list]
    in_specs += [pl.BlockSpec((k, d), lambda i: (0, 0)),
                 pl.BlockSpec((tm, d), lambda i: (i, 0)),
                 pl.BlockSpec((1, d), lambda i: (0, 0))]
    args = list(a_list) + [w, x_res, gain.reshape(1, d)]
    out_specs = [pl.BlockSpec((tm, d), lambda i: (i, 0)),
                 pl.BlockSpec((tm, d), lambda i: (i, 0))]
    out_shape = [jax.ShapeDtypeStruct((m, d), F32), jax.ShapeDtypeStruct((m, d), BF16)]
    n_experts = 0
    if router is not None:
        wr, n_experts = router
        in_specs.append(pl.BlockSpec((d, 2 * LANES), lambda i: (0, 0)))
        args.append(wr)
        out_specs.append(pl.BlockSpec((tm, LANES), lambda i: (i, 0)))
        out_shape.append(jax.ShapeDtypeStruct((m, LANES), F32))
    vmem = 2 * (tm * k * 2 + k * d * 2 + tm * d * 4 + tm * d * 4 + tm * d * 2) \
        + 3 * tm * d * 4 + 4 * d * LANES * 2
    return pl.pallas_call(
        functools.partial(_mm_res_norm_kernel, n_a=len(a_list),
                          with_router=router is not None, n_experts=n_experts),
        grid=(m // tm,),
        in_specs=in_specs,
        out_specs=out_specs,
        out_shape=out_shape,
        compiler_params=_params(("parallel",), vmem),
        name="matmul_residual_norm",
    )(*args)


def _cross_attn_kernel(h_ref, wq_ref, k_ref, v_ref, o_ref, *, n_heads, scale):
    dh = h_ref.shape[1] // n_heads
    h = h_ref[...]
    heads = [slice(i * dh, (i + 1) * dh) for i in range(n_heads)]
    qs = [jnp.dot(h, wq_ref[:, sl], preferred_element_type=F32).astype(BF16) for sl in heads]
    scores = [lax.dot_general(q, k_ref[:, sl], (((1,), (1,)), ((), ())),
                              preferred_element_type=F32) for q, sl in zip(qs, heads)]
    for sl, e in zip(heads, _softmax_numerators(scores, scale)):
        l = jnp.sum(e, axis=-1, keepdims=True)
        o = jnp.dot(e.astype(BF16), v_ref[:, sl], preferred_element_type=F32) / l
        o_ref[:, sl] = o.astype(o_ref.dtype)


def cross_attention(h, w_q, k, v, *, batch, seq, mem_len):
    d = h.shape[1]
    tq = _tile(seq, 512)
    nq = seq // tq
    vmem = 2 * (2 * tq * d * 2 + d * d * 2 + 2 * mem_len * d * 2) + 6 * tq * mem_len * 4 + 3 * tq * d * 4
    return pl.pallas_call(
        functools.partial(_cross_attn_kernel, n_heads=N_MEM_HEADS,
                          scale=(d // N_MEM_HEADS) ** -0.5),
        grid=(batch, nq),
        in_specs=[pl.BlockSpec((tq, d), lambda b, i: (b * nq + i, 0)),
                  pl.BlockSpec((d, d), lambda b, i: (0, 0)),
                  pl.BlockSpec((mem_len, d), lambda b, i: (b, 0)),
                  pl.BlockSpec((mem_len, d), lambda b, i: (b, 0))],
        out_specs=pl.BlockSpec((tq, d), lambda b, i: (b * nq + i, 0)),
        out_shape=jax.ShapeDtypeStruct((batch * seq, d), BF16),
        compiler_params=_params(("parallel", "parallel"), vmem),
        name="cross_attention",
    )(h, w_q, k, v)


def _prefix_count(x, lane):
    n = x.shape[1]
    shift = 1
    while shift < n:
        x = x + jnp.where(lane >= shift, pltpu.roll(x, shift, axis=1), 0)
        shift *= 2
    return x


def _route_kernel(aff_ref, prow_ref, arow_ref, pcol_ref, *, n_experts, capacity):
    e_pad = prow_ref.shape[1]
    a_t = aff_ref[...].T[:e_pad]
    arow_ref[0] = a_t
    seq = a_t.shape[1]

    def body(i, thr_bits):
        cand = thr_bits | jnp.left_shift(jnp.int32(1), 30 - i)
        cnt = jnp.sum((a_t >= pltpu.bitcast(cand, F32)).astype(F32), axis=1, keepdims=True)
        return jnp.where(cnt >= capacity, cand, thr_bits)

    thr = pltpu.bitcast(lax.fori_loop(0, 31, body, jnp.zeros((e_pad, 1), jnp.int32)), F32)
    lane = lax.broadcasted_iota(jnp.int32, a_t.shape, 1)
    gt = a_t > thr
    eq = a_t == thr
    need = capacity - jnp.sum(gt.astype(F32), axis=1, keepdims=True).astype(jnp.int32)
    sel = gt | (eq & (_prefix_count(eq.astype(jnp.int32), lane) <= need))
    pos = jnp.where(sel, _prefix_count(sel.astype(jnp.int32), lane) - 1, -1)
    row = lax.broadcasted_iota(jnp.int32, a_t.shape, 0)
    pos = jnp.where(row < n_experts, pos, -1)
    prow_ref[0] = pos
    unused = jnp.full((LANES - e_pad, seq), -1.0, F32)
    pcol_ref[...] = jnp.concatenate([pos.astype(F32), unused], axis=0).T


def route(aff, *, batch, seq, n_experts, capacity):
    e_pad = max(8, n_experts)
    return pl.pallas_call(
        functools.partial(_route_kernel, n_experts=n_experts, capacity=capacity),
        grid=(batch,),
        in_specs=[pl.BlockSpec((seq, LANES), lambda b: (b, 0))],
        out_specs=[pl.BlockSpec((1, e_pad, seq), lambda b: (b, 0, 0)),
                   pl.BlockSpec((1, e_pad, seq), lambda b: (b, 0, 0)),
                   pl.BlockSpec((seq, LANES), lambda b: (b, 0))],
        out_shape=[jax.ShapeDtypeStruct((batch, e_pad, seq), jnp.int32),
                   jax.ShapeDtypeStruct((batch, e_pad, seq), F32),
                   jax.ShapeDtypeStruct((batch * seq, LANES), F32)],
        compiler_params=_params(("parallel",), 24 * seq * LANES * 4),
        name="route",
    )(aff)


def _gather_kernel(n_first_ref, h_ref, prow_ref, o_ref, *, experts_per_step, split):
    b = pl.program_id(0)
    g = pl.program_id(1)
    cap = o_ref.shape[1]
    seq = h_ref.shape[0]
    half = seq // 2
    wide = cap - split
    for j in range(experts_per_step):
        expert = g * experts_per_step + j
        prow = prow_ref[0, pl.ds(expert, 1), :]
        n_first = n_first_ref[b, expert]
        balanced = (n_first >= split) & (n_first <= wide)

        @pl.when(balanced)
        def _(j=j, prow=prow):
            slot = lax.broadcasted_iota(jnp.int32, (wide, half), 0)
            first = jnp.dot((slot == prow[:, :half]).astype(BF16), h_ref[0:half, :],
                            preferred_element_type=F32)
            second = jnp.dot((slot + split == prow[:, half:]).astype(BF16), h_ref[half:seq, :],
                             preferred_element_type=F32)
            o_ref[j, 0:split, :] = first[0:split].astype(o_ref.dtype)
            o_ref[j, split:wide, :] = (first[split:wide] + second[0:wide - split]).astype(o_ref.dtype)
            o_ref[j, wide:cap, :] = second[wide - split:wide].astype(o_ref.dtype)

        @pl.when(jnp.logical_not(balanced))
        def _(j=j, prow=prow):
            slot = lax.broadcasted_iota(jnp.int32, (cap, seq), 0)
            o_ref[j] = jnp.dot((slot == prow).astype(BF16), h_ref[...],
                               preferred_element_type=F32).astype(o_ref.dtype)


def gather_tokens(h, prow, *, batch, seq, n_experts, capacity):
    d = h.shape[1]
    e_pad = prow.shape[1]
    ng = _tile(n_experts, 4)
    split = (3 * capacity // 8) // 16 * 16
    n_first = jnp.sum((prow[:, :n_experts, :seq // 2] >= 0).astype(jnp.int32), axis=-1)
    vmem = 2 * (seq * d * 2 + e_pad * seq * 4 + ng * capacity * d * 2) + 2 * capacity * seq * 8 \
        + 3 * capacity * d * 4
    return pl.pallas_call(
        functools.partial(_gather_kernel, experts_per_step=ng, split=split),
        grid_spec=pltpu.PrefetchScalarGridSpec(
            num_scalar_prefetch=1,
            grid=(batch, n_experts // ng),
            in_specs=[pl.BlockSpec((seq, d), lambda b, g, nf: (b, 0)),
                      pl.BlockSpec((1, e_pad, seq), lambda b, g, nf: (b, 0, 0))],
            out_specs=pl.BlockSpec((ng, None, capacity, d), lambda b, g, nf: (g, b, 0, 0))),
        out_shape=jax.ShapeDtypeStruct((n_experts, batch, capacity, d), BF16),
        compiler_params=_params(("parallel", "arbitrary"), vmem),
        name="gather_tokens",
    )(n_first, h, prow)


def _ffn_kernel(x_ref, wg_ref, wu_ref, wd_ref, prow_ref, arow_ref, o_ref, acc_ref, gate_ref,
                *, capacity, gate_steps):
    e = pl.program_id(0)
    f = pl.program_id(1)
    nf = pl.num_programs(1)
    rows = x_ref.shape[0]
    n_seq = rows // capacity

    def chunk_product():
        x = x_ref[...]
        a = jnp.dot(x, wg_ref[...].astype(BF16), preferred_element_type=F32)
        u = jnp.dot(x, wu_ref[...].astype(BF16), preferred_element_type=F32)
        hidden = (a / (1.0 + jnp.exp(-a)) * u).astype(BF16)
        return jnp.dot(hidden, wd_ref[...].astype(BF16), preferred_element_type=F32)

    def gate_share():
        seq = prow_ref.shape[2]
        slot = lax.broadcasted_iota(jnp.int32, (capacity, seq), 0)
        per_step = -(-n_seq // gate_steps)
        for j in range(per_step):
            b = jnp.minimum(f * per_step + j, n_seq - 1)
            mine = slot == prow_ref[b, pl.ds(e, 1), :]
            gate_ref[pl.ds(pl.multiple_of(b * capacity, capacity), capacity), :] = jnp.sum(
                jnp.where(mine, arow_ref[b, pl.ds(e, 1), :], 0.0), axis=1, keepdims=True)

    @pl.when(f == 0)
    def _():
        gate_share()
        acc_ref[...] = chunk_product()

    @pl.when((f > 0) & (f < nf - 1))
    def _():
        gate_share()
        acc_ref[...] += chunk_product()

    @pl.when(f == nf - 1)
    def _():
        gate_share()
        o_ref[...] = ((acc_ref[...] + chunk_product()) * gate_ref[...]).astype(o_ref.dtype)


def expert_ffn(xs, w_gate, w_up, w_down, prow, arow, layer, *, rows_per_expert, capacity):
    _, n_experts, d, ff = w_gate.shape
    tf = _tile(ff // 2, 256)
    r = rows_per_expert
    batch, e_pad, seq = prow.shape
    vmem = 2 * (r * d * 2 + 3 * d * tf * 4 + r * d * 2 + 2 * batch * e_pad * seq * 4) + r * d * 4 \
        + r * LANES * 4 + 3 * d * tf * 2 + 4 * r * tf * 4 + r * d * 4 + 2 * capacity * seq * 4
    return pl.pallas_call(
        functools.partial(_ffn_kernel, capacity=capacity, gate_steps=ff // tf),
        grid=(n_experts, ff // tf),
        in_specs=[pl.BlockSpec((r, d), lambda e, f: (e, 0)),
                  pl.BlockSpec((None, None, d, tf), lambda e, f: (layer, e, 0, f)),
                  pl.BlockSpec((None, None, d, tf), lambda e, f: (layer, e, 0, f)),
                  pl.BlockSpec((None, None, tf, d), lambda e, f: (layer, e, f, 0)),
                  pl.BlockSpec((batch, e_pad, seq), lambda e, f: (0, 0, 0)),
                  pl.BlockSpec((batch, e_pad, seq), lambda e, f: (0, 0, 0))],
        out_specs=pl.BlockSpec((r, d), lambda e, f: (e, 0)),
        out_shape=jax.ShapeDtypeStruct((n_experts * r, d), BF16),
        scratch_shapes=[pltpu.VMEM((r, d), F32), pltpu.VMEM((r, 1), F32)],
        compiler_params=_params(("parallel", "arbitrary"), vmem),
        name="expert_ffn",
    )(xs, w_gate, w_up, w_down, prow, arow)


def _combine_kernel(x_ref, y_ref, pcol_ref, g_ref, out_ref, *, final_norm):
    g = pl.program_id(2)
    tt, d = x_ref.shape
    n_group, cap, _ = y_ref.shape

    @pl.when(g == 0)
    def _():
        out_ref[...] = x_ref[...]

    lane = lax.broadcasted_iota(jnp.int32, (tt, LANES), 1)
    slot = lax.broadcasted_iota(jnp.int32, (tt, cap), 1).astype(F32)
    pcol = pcol_ref[...]
    total = None
    for j in range(n_group):
        slot_of_token = jnp.sum(jnp.where(lane == g * n_group + j, pcol, 0.0),
                                axis=1, keepdims=True)
        onehot = (slot == slot_of_token).astype(BF16)
        part = jnp.dot(onehot, y_ref[j], preferred_element_type=F32)
        total = part if total is None else total + part
    out_ref[...] += total

    if final_norm:
        @pl.when(g == pl.num_programs(2) - 1)
        def _():
            out_ref[...] = _rms(out_ref[...], g_ref[...])


def combine(x_res, y, pcol, gain, *, batch, seq, n_experts, capacity, final_norm):
    d = x_res.shape[1]
    tt = _tile(seq, 1024)
    nt = seq // tt
    ng = _tile(n_experts, 4)
    y4 = y.reshape(n_experts, batch, capacity, d)
    vmem = 2 * (tt * d * 4 + ng * capacity * d * 2 + tt * LANES * 4 + tt * d * 4) \
        + 4 * tt * d * 4 + 2 * tt * ng * capacity * 4
    return pl.pallas_call(
        functools.partial(_combine_kernel, final_norm=final_norm),
        grid=(batch, nt, n_experts // ng),
        in_specs=[pl.BlockSpec((tt, d), lambda b, t, g: (b * nt + t, 0)),
                  pl.BlockSpec((ng, None, capacity, d), lambda b, t, g: (g, b, 0, 0)),
                  pl.BlockSpec((tt, LANES), lambda b, t, g: (b * nt + t, 0)),
                  pl.BlockSpec((1, d), lambda b, t, g: (0, 0))],
        out_specs=pl.BlockSpec((tt, d), lambda b, t, g: (b * nt + t, 0)),
        out_shape=jax.ShapeDtypeStruct((batch * seq, d), F32),
        compiler_params=_params(("parallel", "parallel", "arbitrary"), vmem),
        name="combine",
    )(x_res, y4, pcol, gain.reshape(1, d))


def _pool_kernel(x_ref, prev_ref, next_ref, ng_ref, w_ref, s_ref, g_ref, xo_ref, ho_ref, pad_ref,
                 *, seq, windows):
    i = pl.program_id(0)
    tm, d = x_ref.shape
    halo = prev_ref.shape[0]
    pg = w_ref.shape[1]
    tiles_per_seq = seq // tm
    tile_in_seq = i % tiles_per_seq
    norm_gain = ng_ref[...]
    h = _rms(x_ref[...], norm_gain)
    pad_ref[0:halo, :] = jnp.where(tile_in_seq == 0, 0.0, _rms(prev_ref[...], norm_gain))
    pad_ref[halo:halo + tm, :] = h
    pad_ref[halo + tm:2 * halo + tm, :] = jnp.where(tile_in_seq == tiles_per_seq - 1, 0.0,
                                                    _rms(next_ref[...], norm_gain))
    t = tile_in_seq * tm + lax.broadcasted_iota(jnp.int32, (tm, 1), 0)
    for gi, w in enumerate(windows):
        cols = slice(gi * pg, (gi + 1) * pg)
        acc = pad_ref[:, cols]
        span = 1
        while span < w:
            acc = acc + pltpu.roll(acc, acc.shape[0] - span, axis=0)
            span *= 2
        total = acc[halo - w // 2:halo - w // 2 + tm, :]
        count = (jnp.minimum(t + (w - w // 2), seq) - jnp.maximum(t - w // 2, 0)).astype(F32)
        pooled = (total / count - h[:, cols]).astype(BF16)
        mixed = jnp.dot(pooled, w_ref[gi], preferred_element_type=F32)
        xo_ref[:, cols] = mixed * s_ref[:, cols] + x_ref[:, cols]
    ho_ref[...] = _rms(xo_ref[...], g_ref[...]).astype(ho_ref.dtype)


def pool_mixer(x, norm_gain, w_pool, pool_scale, next_gain, *, seq):
    m, d = x.shape
    ng, pg, _ = w_pool.shape
    halo = 8
    assert all(w & (w - 1) == 0 and w // 2 <= halo for w in POOL_WINDOWS) and ng == len(POOL_WINDOWS)
    tm = _tile(seq, 512)
    per = tm // halo
    n_halo_blocks = m // halo
    vmem = 2 * (2 * tm * d * 4 + 2 * halo * d * 4 + ng * pg * pg * 2 + tm * d * 2) \
        + (tm + 2 * halo) * d * 4 + 6 * tm * d * 4
    return pl.pallas_call(
        functools.partial(_pool_kernel, seq=seq, windows=POOL_WINDOWS),
        grid=(m // tm,),
        in_specs=[pl.BlockSpec((tm, d), lambda i: (i, 0)),
                  pl.BlockSpec((halo, d), lambda i: (jnp.maximum(i * per - 1, 0), 0)),
                  pl.BlockSpec((halo, d), lambda i: (jnp.minimum((i + 1) * per, n_halo_blocks - 1), 0)),
                  pl.BlockSpec((1, d), lambda i: (0, 0)),
                  pl.BlockSpec((ng, pg, pg), lambda i: (0, 0, 0)),
                  pl.BlockSpec((1, d), lambda i: (0, 0)),
                  pl.BlockSpec((1, d), lambda i: (0, 0))],
        out_specs=[pl.BlockSpec((tm, d), lambda i: (i, 0)),
                   pl.BlockSpec((tm, d), lambda i: (i, 0))],
        out_shape=[jax.ShapeDtypeStruct((m, d), F32), jax.ShapeDtypeStruct((m, d), BF16)],
        scratch_shapes=[pltpu.VMEM((tm + 2 * halo, d), F32)],
        compiler_params=_params(("parallel",), vmem),
        name="pool_mixer",
    )(x, x, x, norm_gain.reshape(1, d), w_pool, pool_scale.reshape(1, d), next_gain.reshape(1, d))


def _rope_tables(seq):
    half = HEAD_DIM // 2
    rows = seq // GRID_W
    row_idx = jnp.repeat(jnp.arange(rows), GRID_W).astype(F32)
    col_idx = jnp.tile(jnp.arange(GRID_W), rows).astype(F32)
    inv_freq = 1.0 / (ROPE_THETA ** (jnp.arange(0, half, 2, dtype=F32) / half))
    ang = jnp.concatenate([row_idx[:, None] * inv_freq[None, :],
                           col_idx[:, None] * inv_freq[None, :]], axis=-1)
    cos, sin = jnp.cos(ang), jnp.sin(ang)
    zero = jnp.zeros_like(sin)
    cos_full = jnp.stack([cos, cos], axis=-1).reshape(seq, HEAD_DIM)
    sin_even = jnp.stack([-sin, zero], axis=-1).reshape(seq, HEAD_DIM)
    sin_odd = jnp.stack([zero, sin], axis=-1).reshape(seq, HEAD_DIM)
    return cos_full, sin_even, sin_odd


def _split_router(w_router):
    d, n_experts = w_router.shape
    w = jnp.pad(w_router, ((0, 0), (0, LANES - n_experts)))
    hi = w.astype(BF16)
    lo = (w - hi.astype(F32)).astype(BF16)
    return jnp.concatenate([hi, lo], axis=1), n_experts


def _moe(x_res, h, aff, w_gate, w_up, w_down, layer, gain, *, batch, seq, final_norm):
    n_experts = w_gate.shape[1]
    capacity = EC_CAPACITY_FACTOR * seq // n_experts
    d = h.shape[1]
    prow, arow, pcol = route(aff, batch=batch, seq=seq, n_experts=n_experts, capacity=capacity)
    xs = gather_tokens(h, prow, batch=batch, seq=seq, n_experts=n_experts, capacity=capacity)
    y = expert_ffn(xs.reshape(n_experts * batch * capacity, d), w_gate, w_up, w_down, prow, arow,
                   layer, rows_per_expert=batch * capacity, capacity=capacity)
    return combine(x_res, y, pcol, gain, batch=batch, seq=seq,
                   n_experts=n_experts, capacity=capacity, final_norm=final_norm)


def _cross_block(x, h, mem2d, layer, cross_w_q, cross_w_k, cross_w_v, cross_w_o, mem_norm,
                 ffn_norm, router_w, *, batch, seq, mem_len):
    mem_n = rms_norm_rows(mem2d, mem_norm[layer], BF16)
    k = matmul_streamed(mem_n, cross_w_k, (layer,))
    v = matmul_streamed(mem_n, cross_w_v, (layer,))
    o = cross_attention(h, cast_weight(cross_w_q, (layer,)), k, v,
                        batch=batch, seq=seq, mem_len=mem_len)
    return matmul_residual_norm([o], cast_weight(cross_w_o, (layer,)), x, ffn_norm[layer],
                                router=_split_router(router_w[layer]))


def kernel(x, mem, mix_norm, attn_w_in, q_gain, k_gain, fourier_w, attn_w_out, pool_w, pool_scale,
           cross_norm, mem_norm, cross_w_q, cross_w_k, cross_w_v, cross_w_o, ffn_norm, router_w,
           expert_w_gate, expert_w_up, expert_w_down, final_norm):
    batch, seq, d = x.shape
    mem_len = mem.shape[1]
    depth = mix_norm.shape[0]
    attn_width = N_Q_HEADS * HEAD_DIM
    kv_width = N_KV_HEADS * HEAD_DIM
    fourier_width = N_FOURIER_GROUPS * FOURIER_GROUP
    rope_width = attn_width + kv_width
    cos_t, sin_a, sin_b = _rope_tables(seq)

    xf = x.reshape(batch * seq, d)
    mem2d = mem.reshape(batch * mem_len, d)
    cross = functools.partial(_cross_block, mem2d=mem2d, cross_w_q=cross_w_q, cross_w_k=cross_w_k,
                              cross_w_v=cross_w_v, cross_w_o=cross_w_o, mem_norm=mem_norm,
                              ffn_norm=ffn_norm, router_w=router_w,
                              batch=batch, seq=seq, mem_len=mem_len)
    for layer in range(depth):
        i = layer // 2
        if layer % 2 == 0:
            gain_cols = jnp.concatenate([jnp.tile(q_gain[i], N_Q_HEADS), jnp.tile(k_gain[i], N_KV_HEADS),
                                         jnp.ones((kv_width + fourier_width,), F32)]).reshape(1, -1)
            proj = proj_in(xf, mix_norm[layer], cast_weight(attn_w_in, (i,)), gain_cols,
                           cos_t, sin_a, sin_b, seq=seq, rope_width=rope_width)
            o_attn = gqa_attention(proj, batch=batch, seq=seq, attn_width=attn_width, kv_width=kv_width)
            o_four = fourier_mixer(proj, fourier_w[i], batch=batch, seq=seq,
                                   col0=attn_width + 2 * kv_width)
            xf, h = matmul_residual_norm([o_attn, o_four], cast_weight(attn_w_out, (i,)), xf,
                                         cross_norm[layer])
        else:
            xf, h = pool_mixer(xf, mix_norm[layer], pool_w[i].astype(BF16), pool_scale[i],
                               cross_norm[layer], seq=seq)
        xf, h, aff = cross(xf, h, layer=layer)
        xf = _moe(xf, h, aff, expert_w_gate, expert_w_up, expert_w_down, layer, final_norm,
                  batch=batch, seq=seq, final_norm=layer == depth - 1)
    return xf.reshape(batch, seq, d)
```

```python
import functools
import math

import jax
import jax.numpy as jnp
from jax import lax
from jax.experimental import pallas as pl
from jax.experimental.pallas import tpu as pltpu

F32 = jnp.float32
BF16 = jnp.bfloat16

GRID_W = 64
HEAD_DIM = 128
N_Q_HEADS = 12
N_KV_HEADS = 4
N_FOURIER_GROUPS = 4
FOURIER_GROUP = 128
ROPE_THETA = 10000.0
POOL_WINDOWS = (2, 4, 8, 16)
N_MEM_HEADS = 4
EC_CAPACITY_FACTOR = 2
NORM_EPS = 1e-6

LANES = 128
V7X_VMEM_BYTES = 64 * 1024 * 1024
VMEM_HEADROOM_BYTES = 6 * 1024 * 1024
COL_CHUNK = 512
LOG2_E = math.log2(math.e)


def _params(semantics, vmem_bytes):
    limit = min(int(vmem_bytes) + VMEM_HEADROOM_BYTES, V7X_VMEM_BYTES - VMEM_HEADROOM_BYTES)
    return pltpu.CompilerParams(dimension_semantics=semantics, vmem_limit_bytes=limit)


def _tile(n, pref):
    t = min(n, pref)
    while n % t:
        t //= 2
    return t


def _rms(x, gain):
    ms = jnp.mean(x * x, axis=-1, keepdims=True)
    return x * lax.rsqrt(ms + NORM_EPS) * gain


def _softmax_numerators(scores, scale):
    c = scale * LOG2_E
    return [jnp.exp2((s - jnp.max(s, axis=-1, keepdims=True)) * c) for s in scores]


def _norm_kernel(x_ref, g_ref, o_ref):
    o_ref[...] = _rms(x_ref[...], g_ref[...]).astype(o_ref.dtype)


def rms_norm_rows(x, gain, out_dtype):
    m, d = x.shape
    tm = _tile(m, 512)
    return pl.pallas_call(
        _norm_kernel,
        grid=(m // tm,),
        in_specs=[pl.BlockSpec((tm, d), lambda i: (i, 0)),
                  pl.BlockSpec((1, d), lambda i: (0, 0))],
        out_specs=pl.BlockSpec((tm, d), lambda i: (i, 0)),
        out_shape=jax.ShapeDtypeStruct((m, d), out_dtype),
        compiler_params=_params(("parallel",), 2 * tm * d * 8),
        name="rms_norm_rows",
    )(x, gain.reshape(1, d))


def _cast_kernel(w_ref, o_ref):
    o_ref[...] = w_ref[...].astype(o_ref.dtype)


def cast_weight(w, lead):
    k, n = w.shape[-2:]
    tk = _tile(k, 1024)
    squeezed = (None,) * len(lead)
    return pl.pallas_call(
        _cast_kernel,
        grid=(k // tk,),
        in_specs=[pl.BlockSpec(squeezed + (tk, n), lambda i: tuple(lead) + (i, 0))],
        out_specs=pl.BlockSpec((tk, n), lambda i: (i, 0)),
        out_shape=jax.ShapeDtypeStruct((k, n), BF16),
        compiler_params=_params(("parallel",), 2 * tk * n * 6),
        name="cast_weight",
    )(w)


def _mm_stream_kernel(a_ref, w_ref, o_ref):
    o_ref[...] = jnp.dot(a_ref[...], w_ref[...].astype(BF16),
                         preferred_element_type=F32).astype(o_ref.dtype)


def matmul_streamed(a, w, lead, out_dtype=BF16):
    m, k = a.shape
    n = w.shape[-1]
    tn = _tile(n, COL_CHUNK)
    squeezed = (None,) * len(lead)
    vmem = 2 * (m * k * 2 + k * tn * 4 + m * tn * 2) + k * tn * 2 + m * tn * 4
    return pl.pallas_call(
        _mm_stream_kernel,
        grid=(n // tn,),
        in_specs=[pl.BlockSpec((m, k), lambda j: (0, 0)),
                  pl.BlockSpec(squeezed + (k, tn), lambda j: tuple(lead) + (0, j))],
        out_specs=pl.BlockSpec((m, tn), lambda j: (0, j)),
        out_shape=jax.ShapeDtypeStruct((m, n), out_dtype),
        compiler_params=_params(("parallel",), vmem),
        name="matmul_streamed",
    )(a, w)


def _proj_in_kernel(a_ref, ng_ref, w_ref, g_ref, cos_ref, sa_ref, sb_ref, o_ref, *, n_rope_heads):
    n = o_ref.shape[1]
    tn = _tile(n, COL_CHUNK)
    a = _rms(a_ref[...], ng_ref[...]).astype(BF16)
    cos, sin_a, sin_b = cos_ref[...], sa_ref[...], sb_ref[...]
    chunks = [jnp.dot(a, w_ref[:, c * tn:(c + 1) * tn], preferred_element_type=F32)
              for c in range(n // tn)]
    for c, acc in enumerate(chunks):
        for h in range(tn // HEAD_DIM):
            head = c * (tn // HEAD_DIM) + h
            col = slice(head * HEAD_DIM, (head + 1) * HEAD_DIM)
            xh = acc[:, h * HEAD_DIM:(h + 1) * HEAD_DIM]
            if head < n_rope_heads:
                ms = jnp.mean(xh * xh, axis=-1, keepdims=True)
                y = xh * lax.rsqrt(ms + NORM_EPS) * g_ref[:, col]
                xh = (y * cos + pltpu.roll(y, HEAD_DIM - 1, axis=1) * sin_a
                      + pltpu.roll(y, 1, axis=1) * sin_b)
            o_ref[:, col] = xh.astype(o_ref.dtype)


def proj_in(x, norm_gain, w_in, gain_cols, cos_t, sin_a, sin_b, *, seq, rope_width):
    m, k = x.shape
    n = w_in.shape[1]
    tm = _tile(seq, 512)
    n_pos_blocks = seq // tm
    vmem = 2 * (tm * k * 4 + k * n * 2 + tm * n * 2 + 3 * tm * HEAD_DIM * 4) + 6 * tm * COL_CHUNK * 4 \
        + 2 * tm * k * 4
    table = pl.BlockSpec((tm, HEAD_DIM), lambda i: (i % n_pos_blocks, 0))
    return pl.pallas_call(
        functools.partial(_proj_in_kernel, n_rope_heads=rope_width // HEAD_DIM),
        grid=(m // tm,),
        in_specs=[pl.BlockSpec((tm, k), lambda i: (i, 0)),
                  pl.BlockSpec((1, k), lambda i: (0, 0)),
                  pl.BlockSpec((k, n), lambda i: (0, 0)),
                  pl.BlockSpec((1, n), lambda i: (0, 0)),
                  table, table, table],
        out_specs=pl.BlockSpec((tm, n), lambda i: (i, 0)),
        out_shape=jax.ShapeDtypeStruct((m, n), BF16),
        compiler_params=_params(("parallel",), vmem),
        name="proj_in",
    )(x, norm_gain.reshape(1, k), w_in, gain_cols, cos_t, sin_a, sin_b)


def _attn_kernel(q_ref, k_ref, v_ref, o_ref, *, group, scale):
    k = k_ref[...]
    v = v_ref[...]
    v_ones = jnp.concatenate([v, jnp.ones_like(v)], axis=1)
    heads = [slice(g * HEAD_DIM, (g + 1) * HEAD_DIM) for g in range(group)]
    scores = [lax.dot_general(q_ref[:, sl], k, (((1,), (1,)), ((), ())),
                              preferred_element_type=F32) for sl in heads]
    for sl, e in zip(heads, _softmax_numerators(scores, scale)):
        ol = jnp.dot(e.astype(BF16), v_ones, preferred_element_type=F32)
        o_ref[:, sl] = (ol[:, :HEAD_DIM] / ol[:, HEAD_DIM:HEAD_DIM + 1]).astype(o_ref.dtype)


def gqa_attention(proj, *, batch, seq, attn_width, kv_width):
    group = N_Q_HEADS // N_KV_HEADS
    gw = group * HEAD_DIM
    tq = _tile(seq, 1024)
    nq = seq // tq
    k_col0 = attn_width // HEAD_DIM
    v_col0 = (attn_width + kv_width) // HEAD_DIM
    vmem = 2 * (tq * gw * 2 * 2 + 2 * seq * HEAD_DIM * 2) + group * tq * seq * 6
    return pl.pallas_call(
        functools.partial(_attn_kernel, group=group, scale=HEAD_DIM ** -0.5),
        grid=(batch, N_KV_HEADS, nq),
        in_specs=[pl.BlockSpec((tq, gw), lambda b, h, i: (b * nq + i, h)),
                  pl.BlockSpec((seq, HEAD_DIM), lambda b, h, i: (b, k_col0 + h)),
                  pl.BlockSpec((seq, HEAD_DIM), lambda b, h, i: (b, v_col0 + h))],
        out_specs=pl.BlockSpec((tq, gw), lambda b, h, i: (b * nq + i, h)),
        out_shape=jax.ShapeDtypeStruct((batch * seq, attn_width), BF16),
        compiler_params=_params(("parallel", "parallel", "parallel"), vmem),
        name="gqa_attention",
    )(proj, proj, proj)


def _fourier_kernel(f_ref, cc_ref, sc_ref, cs_ref, ss_ref, wf_ref, o_ref, ab_ref, *, seq, norm):
    r = pl.program_id(1)
    ng = wf_ref.shape[0]
    cg = wf_ref.shape[1]

    @pl.when(r == 0)
    def _():
        for g in range(ng):
            fg = f_ref[:, g * cg:(g + 1) * cg]
            ab_ref[0:seq, g * cg:(g + 1) * cg] = jnp.dot(
                fg, cc_ref[...], preferred_element_type=F32).astype(BF16)
            ab_ref[seq:2 * seq, g * cg:(g + 1) * cg] = jnp.dot(
                fg, sc_ref[...], preferred_element_type=F32).astype(BF16)

    z = (jnp.dot(cs_ref[...], ab_ref[0:seq, :], preferred_element_type=F32)
         - jnp.dot(ss_ref[...], ab_ref[seq:2 * seq, :], preferred_element_type=F32)) * norm
    for g in range(ng):
        zg = z[:, g * cg:(g + 1) * cg].astype(BF16)
        o_ref[:, g * cg:(g + 1) * cg] = jnp.dot(
            zg, wf_ref[g].astype(BF16), preferred_element_type=F32).astype(o_ref.dtype)


def _dft_cos_sin(n):
    lo = _tile(n, 64)
    hi = n // lo
    j = jnp.arange(n, dtype=jnp.int32)

    def table(k):
        ang = ((k[:, None] * j[None, :]) % n).astype(F32) * (2.0 * math.pi / n)
        return jnp.cos(ang), jnp.sin(ang)

    ca, sa = table(jnp.arange(hi, dtype=jnp.int32) * lo)
    cb, sb = table(jnp.arange(lo, dtype=jnp.int32))
    cos = ca[:, None, :] * cb[None, :, :] - sa[:, None, :] * sb[None, :, :]
    sin = sa[:, None, :] * cb[None, :, :] + ca[:, None, :] * sb[None, :, :]
    return cos.reshape(n, n).astype(BF16), sin.reshape(n, n).astype(BF16)


def fourier_mixer(proj, w_fourier, *, batch, seq, col0):
    ng, cg, _ = w_fourier.shape
    fw = ng * cg
    tr = _tile(seq, 512)
    nr = seq // tr
    cc, sc = _dft_cos_sin(cg)
    cs, ss = _dft_cos_sin(seq)
    norm = 1.0 / math.sqrt(seq * cg)
    vmem = 2 * (seq * fw * 2 + 2 * cg * cg * 2 + 2 * tr * seq * 2 + ng * cg * cg * 4 + tr * fw * 2) \
        + 2 * seq * fw * 2 + 4 * tr * fw * 4
    return pl.pallas_call(
        functools.partial(_fourier_kernel, seq=seq, norm=norm),
        grid=(batch, nr),
        in_specs=[pl.BlockSpec((seq, fw), lambda b, r: (b, col0 // fw)),
                  pl.BlockSpec((cg, cg), lambda b, r: (0, 0)),
                  pl.BlockSpec((cg, cg), lambda b, r: (0, 0)),
                  pl.BlockSpec((tr, seq), lambda b, r: (r, 0)),
                  pl.BlockSpec((tr, seq), lambda b, r: (r, 0)),
                  pl.BlockSpec((ng, cg, cg), lambda b, r: (0, 0, 0))],
        out_specs=pl.BlockSpec((tr, fw), lambda b, r: (b * nr + r, 0)),
        out_shape=jax.ShapeDtypeStruct((batch * seq, fw), BF16),
        scratch_shapes=[pltpu.VMEM((2 * seq, fw), BF16)],
        compiler_params=_params(("parallel", "arbitrary"), vmem),
        name="fourier_mixer",
    )(proj, cc, sc, cs, ss, w_fourier)


def _router_affinity(hn, wr_ref, n_experts):
    hi = hn.astype(BF16)
    lo = (hn - hi.astype(F32)).astype(BF16)
    r_hi = jnp.dot(hi, wr_ref[...], preferred_element_type=F32)
    r_lo = jnp.dot(lo, wr_ref[...], preferred_element_type=F32)
    logits = r_hi[:, :LANES] + r_hi[:, LANES:] + r_lo[:, :LANES]
    lane = lax.broadcasted_iota(jnp.int32, logits.shape, 1)
    logits = jnp.where(lane < n_experts, logits, -jnp.inf)
    e = jnp.exp(logits - jnp.max(logits, axis=-1, keepdims=True))
    return e / jnp.sum(e, axis=-1, keepdims=True)


def _mm_res_norm_kernel(*refs, n_a, with_router, n_experts):
    a_refs = refs[:n_a]
    w_ref, x_ref, g_ref = refs[n_a:n_a + 3]
    pos = n_a + 3
    if with_router:
        wr_ref = refs[pos]
        pos += 1
    xo_ref, ho_ref = refs[pos:pos + 2]
    if with_router:
        aff_ref = refs[pos + 2]
    d = xo_ref.shape[1]
    tn = _tile(d, COL_CHUNK)
    for c in range(d // tn):
        sl = slice(c * tn, (c + 1) * tn)
        acc = x_ref[:, sl]
        row0 = 0
        for a_ref in a_refs:
            ka = a_ref.shape[1]
            acc = acc + jnp.dot(a_ref[...], w_ref[row0:row0 + ka, sl], preferred_element_type=F32)
            row0 += ka
        xo_ref[:, sl] = acc
    hn = _rms(xo_ref[...], g_ref[...])
    ho_ref[...] = hn.astype(ho_ref.dtype)
    if with_router:
        aff_ref[...] = _router_affinity(hn, wr_ref, n_experts)


def matmul_residual_norm(a_list, w, x_res, gain, *, router=None, tm_pref=512):
    m, d = x_res.shape
    k = w.shape[0]
    tm = _tile(m, tm_pref)
    in_specs = [pl.BlockSpec((tm, a.shape[1]), lambda i: (i, 0)) for a in a_list]
    in_specs += [pl.BlockSpec((k, d), lambda i: (0, 0)),
                 pl.BlockSpec((tm, d), lambda i: (i, 0)),
                 pl.BlockSpec((1, d), lambda i: (0, 0))]
    args = list(a_list) + [w, x_res, gain.reshape(1, d)]
    out_specs = [pl.BlockSpec((tm, d), lambda i: (i, 0)),
                 pl.BlockSpec((tm, d), lambda i: (i, 0))]
    out_shape = [jax.ShapeDtypeStruct((m, d), F32), jax.ShapeDtypeStruct((m, d), BF16)]
    n_experts = 0
    if router is not None:
        wr, n_experts = router
        in_specs.append(pl.BlockSpec((d, 2 * LANES), lambda i: (0, 0)))
        args.append(wr)
        out_specs.append(pl.BlockSpec((tm, LANES), lambda i: (i, 0)))
        out_shape.append(jax.ShapeDtypeStruct((m, LANES), F32))
    vmem = 2 * (tm * k * 2 + k * d * 2 + tm * d * 4 + tm * d * 4 + tm * d * 2) \
        + 3 * tm * d * 4 + 4 * d * LANES * 2
    return pl.pallas_call(
        functools.partial(_mm_res_norm_kernel, n_a=len(a_list),
                          with_router=router is not None, n_experts=n_experts),
        grid=(m // tm,),
        in_specs=in_specs,
        out_specs=out_specs,
        out_shape=out_shape,
        compiler_params=_params(("parallel",), vmem),
        name="matmul_residual_norm",
    )(*args)


def _cross_attn_kernel(h_ref, wq_ref, k_ref, v_ref, o_ref, *, n_heads, scale):
    dh = h_ref.shape[1] // n_heads
    h = h_ref[...]
    heads = [slice(i * dh, (i + 1) * dh) for i in range(n_heads)]
    qs = [jnp.dot(h, wq_ref[:, sl], preferred_element_type=F32).astype(BF16) for sl in heads]
    scores = [lax.dot_general(q, k_ref[:, sl], (((1,), (1,)), ((), ())),
                              preferred_element_type=F32) for q, sl in zip(qs, heads)]
    for sl, e in zip(heads, _softmax_numerators(scores, scale)):
        l = jnp.sum(e, axis=-1, keepdims=True)
        o = jnp.dot(e.astype(BF16), v_ref[:, sl], preferred_element_type=F32) / l
        o_ref[:, sl] = o.astype(o_ref.dtype)


def cross_attention(h, w_q, k, v, *, batch, seq, mem_len):
    d = h.shape[1]
    tq = _tile(seq, 512)
    nq = seq // tq
    vmem = 2 * (2 * tq * d * 2 + d * d * 2 + 2 * mem_len * d * 2) + 6 * tq * mem_len * 4 + 3 * tq * d * 4
    return pl.pallas_call(
        functools.partial(_cross_attn_kernel, n_heads=N_MEM_HEADS,
                          scale=(d // N_MEM_HEADS) ** -0.5),
        grid=(batch, nq),
        in_specs=[pl.BlockSpec((tq, d), lambda b, i: (b * nq + i, 0)),
                  pl.BlockSpec((d, d), lambda b, i: (0, 0)),
                  pl.BlockSpec((mem_len, d), lambda b, i: (b, 0)),
                  pl.BlockSpec((mem_len, d), lambda b, i: (b, 0))],
        out_specs=pl.BlockSpec((tq, d), lambda b, i: (b * nq + i, 0)),
        out_shape=jax.ShapeDtypeStruct((batch * seq, d), BF16),
        compiler_params=_params(("parallel", "parallel"), vmem),
        name="cross_attention",
    )(h, w_q, k, v)


def _prefix_count(x, lane):
    n = x.shape[1]
    shift = 1
    while shift < n:
        x = x + jnp.where(lane >= shift, pltpu.roll(x, shift, axis=1), 0)
        shift *= 2
    return x


def _route_kernel(aff_ref, prow_ref, arow_ref, pcol_ref, *, n_experts, capacity):
    e_pad = prow_ref.shape[1]
    a_t = aff_ref[...].T[:e_pad]
    arow_ref[0] = a_t
    seq = a_t.shape[1]

    def body(i, thr_bits):
        cand = thr_bits | jnp.left_shift(jnp.int32(1), 30 - i)
        cnt = jnp.sum((a_t >= pltpu.bitcast(cand, F32)).astype(F32), axis=1, keepdims=True)
        return jnp.where(cnt >= capacity, cand, thr_bits)

    thr = pltpu.bitcast(lax.fori_loop(0, 31, body, jnp.zeros((e_pad, 1), jnp.int32)), F32)
    lane = lax.broadcasted_iota(jnp.int32, a_t.shape, 1)
    gt = a_t > thr
    eq = a_t == thr
    need = capacity - jnp.sum(gt.astype(F32), axis=1, keepdims=True).astype(jnp.int32)
    sel = gt | (eq & (_prefix_count(eq.astype(jnp.int32), lane) <= need))
    pos = jnp.where(sel, _prefix_count(sel.astype(jnp.int32), lane) - 1, -1)
    row = lax.broadcasted_iota(jnp.int32, a_t.shape, 0)
    pos = jnp.where(row < n_experts, pos, -1)
    prow_ref[0] = pos
    unused = jnp.full((LANES - e_pad, seq), -1.0, F32)
    pcol_ref[...] = jnp.concatenate([pos.astype(F32), unused], axis=0).T


def route(aff, *, batch, seq, n_experts, capacity):
    e_pad = max(8, n_experts)
    return pl.pallas_call(
        functools.partial(_route_kernel, n_experts=n_experts, capacity=capacity),
        grid=(batch,),
        in_specs=[pl.BlockSpec((seq, LANES), lambda b: (b, 0))],
        out_specs=[pl.BlockSpec((1, e_pad, seq), lambda b: (b, 0, 0)),
                   pl.BlockSpec((1, e_pad, seq), lambda b: (b, 0, 0)),
                   pl.BlockSpec((seq, LANES), lambda b: (b, 0))],
        out_shape=[jax.ShapeDtypeStruct((batch, e_pad, seq), jnp.int32),
                   jax.ShapeDtypeStruct((batch, e_pad, seq), F32),
                   jax.ShapeDtypeStruct((batch * seq, LANES), F32)],
        compiler_params=_params(("parallel",), 24 * seq * LANES * 4),
        name="route",
    )(aff)


def _gather_kernel(n_first_ref, h_ref, prow_ref, o_ref, *, experts_per_step, split):
    b = pl.program_id(0)
    g = pl.program_id(1)
    cap = o_ref.shape[1]
    seq = h_ref.shape[0]
    half = seq // 2
    wide = cap - split
    for j in range(experts_per_step):
        expert = g * experts_per_step + j
        prow = prow_ref[0, pl.ds(expert, 1), :]
        n_first = n_first_ref[b, expert]
        balanced = (n_first >= split) & (n_first <= wide)

        @pl.when(balanced)
        def _(j=j, prow=prow):
            slot = lax.broadcasted_iota(jnp.int32, (wide, half), 0)
            first = jnp.dot((slot == prow[:, :half]).astype(BF16), h_ref[0:half, :],
                            preferred_element_type=F32)
            second = jnp.dot((slot + split == prow[:, half:]).astype(BF16), h_ref[half:seq, :],
                             preferred_element_type=F32)
            o_ref[j, 0:split, :] = first[0:split].astype(o_ref.dtype)
            o_ref[j, split:wide, :] = (first[split:wide] + second[0:wide - split]).astype(o_ref.dtype)
            o_ref[j, wide:cap, :] = second[wide - split:wide].astype(o_ref.dtype)

        @pl.when(jnp.logical_not(balanced))
        def _(j=j, prow=prow):
            slot = lax.broadcasted_iota(jnp.int32, (cap, seq), 0)
            o_ref[j] = jnp.dot((slot == prow).astype(BF16), h_ref[...],
                               preferred_element_type=F32).astype(o_ref.dtype)


def gather_tokens(h, prow, *, batch, seq, n_experts, capacity):
    d = h.shape[1]
    e_pad = prow.shape[1]
    ng = _tile(n_experts, 4)
    split = (3 * capacity // 8) // 16 * 16
    n_first = jnp.sum((prow[:, :n_experts, :seq // 2] >= 0).astype(jnp.int32), axis=-1)
    vmem = 2 * (seq * d * 2 + e_pad * seq * 4 + ng * capacity * d * 2) + 2 * capacity * seq * 8 \
        + 3 * capacity * d * 4
    return pl.pallas_call(
        functools.partial(_gather_kernel, experts_per_step=ng, split=split),
        grid_spec=pltpu.PrefetchScalarGridSpec(
            num_scalar_prefetch=1,
            grid=(batch, n_experts // ng),
            in_specs=[pl.BlockSpec((seq, d), lambda b, g, nf: (b, 0)),
                      pl.BlockSpec((1, e_pad, seq), lambda b, g, nf: (b, 0, 0))],
            out_specs=pl.BlockSpec((ng, None, capacity, d), lambda b, g, nf: (g, b, 0, 0))),
        out_shape=jax.ShapeDtypeStruct((n_experts, batch, capacity, d), BF16),
        compiler_params=_params(("parallel", "arbitrary"), vmem),
        name="gather_tokens",
    )(n_first, h, prow)


def _ffn_kernel(x_ref, wg_ref, wu_ref, wd_ref, prow_ref, arow_ref, o_ref, acc_ref, gate_ref,
                *, capacity, gate_steps):
    e = pl.program_id(0)
    f = pl.program_id(1)
    nf = pl.num_programs(1)
    rows = x_ref.shape[0]
    n_seq = rows // capacity

    def chunk_product():
        x = x_ref[...]
        a = jnp.dot(x, wg_ref[...].astype(BF16), preferred_element_type=F32)
        u = jnp.dot(x, wu_ref[...].astype(BF16), preferred_element_type=F32)
        hidden = (a / (1.0 + jnp.exp(-a)) * u).astype(BF16)
        return jnp.dot(hidden, wd_ref[...].astype(BF16), preferred_element_type=F32)

    def gate_share():
        seq = prow_ref.shape[2]
        slot = lax.broadcasted_iota(jnp.int32, (capacity, seq), 0)
        per_step = -(-n_seq // gate_steps)
        for j in range(per_step):
            b = jnp.minimum(f * per_step + j, n_seq - 1)
            mine = slot == prow_ref[b, pl.ds(e, 1), :]
            gate_ref[pl.ds(pl.multiple_of(b * capacity, capacity), capacity), :] = jnp.sum(
                jnp.where(mine, arow_ref[b, pl.ds(e, 1), :], 0.0), axis=1, keepdims=True)

    @pl.when(f == 0)
    def _():
        gate_share()
        acc_ref[...] = chunk_product()

    @pl.when((f > 0) & (f < nf - 1))
    def _():
        gate_share()
        acc_ref[...] += chunk_product()

    @pl.when(f == nf - 1)
    def _():
        gate_share()
        o_ref[...] = ((acc_ref[...] + chunk_product()) * gate_ref[...]).astype(o_ref.dtype)


def expert_ffn(xs, w_gate, w_up, w_down, prow, arow, layer, *, rows_per_expert, capacity):
    _, n_experts, d, ff = w_gate.shape
    tf = _tile(ff // 2, 256)
    r = rows_per_expert
    batch, e_pad, seq = prow.shape
    vmem = 2 * (r * d * 2 + 3 * d * tf * 4 + r * d * 2 + 2 * batch * e_pad * seq * 4) + r * d * 4 \
        + r * LANES * 4 + 3 * d * tf * 2 + 4 * r * tf * 4 + r * d * 4 + 2 * capacity * seq * 4
    return pl.pallas_call(
        functools.partial(_ffn_kernel, capacity=capacity, gate_steps=ff // tf),
        grid=(n_experts, ff // tf),
        in_specs=[pl.BlockSpec((r, d), lambda e, f: (e, 0)),
                  pl.BlockSpec((None, None, d, tf), lambda e, f: (layer, e, 0, f)),
                  pl.BlockSpec((None, None, d, tf), lambda e, f: (layer, e, 0, f)),
                  pl.BlockSpec((None, None, tf, d), lambda e, f: (layer, e, f, 0)),
                  pl.BlockSpec((batch, e_pad, seq), lambda e, f: (0, 0, 0)),
                  pl.BlockSpec((batch, e_pad, seq), lambda e, f: (0, 0, 0))],
        out_specs=pl.BlockSpec((r, d), lambda e, f: (e, 0)),
        out_shape=jax.ShapeDtypeStruct((n_experts * r, d), BF16),
        scratch_shapes=[pltpu.VMEM((r, d), F32), pltpu.VMEM((r, 1), F32)],
        compiler_params=_params(("parallel", "arbitrary"), vmem),
        name="expert_ffn",
    )(xs, w_gate, w_up, w_down, prow, arow)


COMBINE_TILES = 4


def _window_start(tile, capacity, window):
    share = capacity // COMBINE_TILES
    unit = (window - share) // 2
    assert share % unit == 0 and (capacity - window) % unit == 0
    return jnp.clip(tile * (share // unit) - 1, 0, (capacity - window) // unit) * unit


def _combine_kernel(ok_ref, x_ref, ywin_ref, yall_ref, pcol_ref, g_ref, out_ref, *, final_norm):
    b = pl.program_id(0)
    t = pl.program_id(1)
    g = pl.program_id(2)
    tt, d = x_ref.shape
    n_group, _, window, _ = ywin_ref.shape
    cap = yall_ref.shape[1]

    @pl.when(g == 0)
    def _():
        out_ref[...] = x_ref[...]

    lane = lax.broadcasted_iota(jnp.int32, (tt, LANES), 1)
    pcol = pcol_ref[...]

    def scatter(y_rows, n_slots, first_slot):
        slot = (lax.broadcasted_iota(jnp.int32, (tt, n_slots), 1) + first_slot).astype(F32)
        hots = []
        for j in range(n_group):
            slot_of_token = jnp.sum(jnp.where(lane == g * n_group + j, pcol, 0.0),
                                    axis=1, keepdims=True)
            hots.append((slot == slot_of_token).astype(BF16))
        out_ref[...] += jnp.dot(jnp.concatenate(hots, axis=1), y_rows, preferred_element_type=F32)

    in_window = ok_ref[b, t, g] != 0

    @pl.when(in_window)
    def _():
        scatter(ywin_ref[...].reshape(n_group * window, d), window, _window_start(t, cap, window))

    @pl.when(jnp.logical_not(in_window))
    def _():
        scatter(yall_ref[...].reshape(n_group * cap, d), cap, 0)

    if final_norm:
        @pl.when(g == pl.num_programs(2) - 1)
        def _():
            out_ref[...] = _rms(out_ref[...], g_ref[...])


def combine(x_res, y, prow, pcol, gain, *, batch, seq, n_experts, capacity, final_norm):
    d = x_res.shape[1]
    nt = COMBINE_TILES
    tt = seq // nt
    window = capacity // 2
    ng = _tile(n_experts, 8)
    n_groups = n_experts // ng
    y4 = y.reshape(n_experts, batch, capacity, d)
    pos = prow[:, :n_experts, :].reshape(batch, n_experts, nt, tt)
    first = _window_start(jnp.arange(nt, dtype=jnp.int32), capacity, window)[None, None, :]
    lowest = jnp.min(jnp.where(pos >= 0, pos, capacity), axis=-1)
    highest = jnp.max(pos, axis=-1)
    inside = (lowest >= first) & (highest < first + window)
    ok = jnp.all(inside.reshape(batch, n_groups, ng, nt), axis=2).transpose(0, 2, 1).astype(jnp.int32)
    vmem = 2 * (tt * d * 4 + ng * window * d * 2 + ng * capacity * d * 2 + tt * LANES * 4 + tt * d * 4) \
        + 3 * tt * d * 4 + 2 * tt * capacity * 4

    def full_block(b, t, g, ok_ref):
        need = ok_ref[b, t, g] == 0
        return (jnp.where(need, g, 0), jnp.where(need, b, 0), 0, 0)

    return pl.pallas_call(
        functools.partial(_combine_kernel, final_norm=final_norm),
        grid_spec=pltpu.PrefetchScalarGridSpec(
            num_scalar_prefetch=1,
            grid=(batch, nt, n_groups),
            in_specs=[pl.BlockSpec((tt, d), lambda b, t, g, ok_ref: (b * nt + t, 0)),
                      pl.BlockSpec((pl.Element(ng), pl.Element(1), pl.Element(window), pl.Element(d)),
                                   lambda b, t, g, ok_ref: (g * ng, b, _window_start(t, capacity, window), 0)),
                      pl.BlockSpec((ng, None, capacity, d), full_block),
                      pl.BlockSpec((tt, LANES), lambda b, t, g, ok_ref: (b * nt + t, 0)),
                      pl.BlockSpec((1, d), lambda b, t, g, ok_ref: (0, 0))],
            out_specs=pl.BlockSpec((tt, d), lambda b, t, g, ok_ref: (b * nt + t, 0))),
        out_shape=jax.ShapeDtypeStruct((batch * seq, d), F32),
        compiler_params=_params(("parallel", "parallel", "arbitrary"), vmem),
        name="combine",
    )(ok, x_res, y4, y4, pcol, gain.reshape(1, d))


def _pool_kernel(x_ref, prev_ref, next_ref, ng_ref, w_ref, s_ref, g_ref, xo_ref, ho_ref, pad_ref,
                 *, seq, windows):
    i = pl.program_id(0)
    tm, d = x_ref.shape
    halo = prev_ref.shape[0]
    pg = w_ref.shape[1]
    tiles_per_seq = seq // tm
    tile_in_seq = i % tiles_per_seq
    norm_gain = ng_ref[...]
    h = _rms(x_ref[...], norm_gain)
    pad_ref[0:halo, :] = jnp.where(tile_in_seq == 0, 0.0, _rms(prev_ref[...], norm_gain))
    pad_ref[halo:halo + tm, :] = h
    pad_ref[halo + tm:2 * halo + tm, :] = jnp.where(tile_in_seq == tiles_per_seq - 1, 0.0,
                                                    _rms(next_ref[...], norm_gain))
    t = tile_in_seq * tm + lax.broadcasted_iota(jnp.int32, (tm, 1), 0)
    for gi, w in enumerate(windows):
        cols = slice(gi * pg, (gi + 1) * pg)
        acc = pad_ref[:, cols]
        span = 1
        while span < w:
            acc = acc + pltpu.roll(acc, acc.shape[0] - span, axis=0)
            span *= 2
        total = acc[halo - w // 2:halo - w // 2 + tm, :]
        count = (jnp.minimum(t + (w - w // 2), seq) - jnp.maximum(t - w // 2, 0)).astype(F32)
        pooled = (total / count - h[:, cols]).astype(BF16)
        mixed = jnp.dot(pooled, w_ref[gi], preferred_element_type=F32)
        xo_ref[:, cols] = mixed * s_ref[:, cols] + x_ref[:, cols]
    ho_ref[...] = _rms(xo_ref[...], g_ref[...]).astype(ho_ref.dtype)


def pool_mixer(x, norm_gain, w_pool, pool_scale, next_gain, *, seq):
    m, d = x.shape
    ng, pg, _ = w_pool.shape
    halo = 8
    assert all(w & (w - 1) == 0 and w // 2 <= halo for w in POOL_WINDOWS) and ng == len(POOL_WINDOWS)
    tm = _tile(seq, 512)
    per = tm // halo
    n_halo_blocks = m // halo
    vmem = 2 * (2 * tm * d * 4 + 2 * halo * d * 4 + ng * pg * pg * 2 + tm * d * 2) \
        + (tm + 2 * halo) * d * 4 + 6 * tm * d * 4
    return pl.pallas_call(
        functools.partial(_pool_kernel, seq=seq, windows=POOL_WINDOWS),
        grid=(m // tm,),
        in_specs=[pl.BlockSpec((tm, d), lambda i: (i, 0)),
                  pl.BlockSpec((halo, d), lambda i: (jnp.maximum(i * per - 1, 0), 0)),
                  pl.BlockSpec((halo, d), lambda i: (jnp.minimum((i + 1) * per, n_halo_blocks - 1), 0)),
                  pl.BlockSpec((1, d), lambda i: (0, 0)),
                  pl.BlockSpec((ng, pg, pg), lambda i: (0, 0, 0)),
                  pl.BlockSpec((1, d), lambda i: (0, 0)),
                  pl.BlockSpec((1, d), lambda i: (0, 0))],
        out_specs=[pl.BlockSpec((tm, d), lambda i: (i, 0)),
                   pl.BlockSpec((tm, d), lambda i: (i, 0))],
        out_shape=[jax.ShapeDtypeStruct((m, d), F32), jax.ShapeDtypeStruct((m, d), BF16)],
        scratch_shapes=[pltpu.VMEM((tm + 2 * halo, d), F32)],
        compiler_params=_params(("parallel",), vmem),
        name="pool_mixer",
    )(x, x, x, norm_gain.reshape(1, d), w_pool, pool_scale.reshape(1, d), next_gain.reshape(1, d))


def _rope_tables(seq):
    half = HEAD_DIM // 2
    rows = seq // GRID_W
    row_idx = jnp.repeat(jnp.arange(rows), GRID_W).astype(F32)
    col_idx = jnp.tile(jnp.arange(GRID_W), rows).astype(F32)
    inv_freq = 1.0 / (ROPE_THETA ** (jnp.arange(0, half, 2, dtype=F32) / half))
    ang = jnp.concatenate([row_idx[:, None] * inv_freq[None, :],
                           col_idx[:, None] * inv_freq[None, :]], axis=-1)
    cos, sin = jnp.cos(ang), jnp.sin(ang)
    zero = jnp.zeros_like(sin)
    cos_full = jnp.stack([cos, cos], axis=-1).reshape(seq, HEAD_DIM)
    sin_even = jnp.stack([-sin, zero], axis=-1).reshape(seq, HEAD_DIM)
    sin_odd = jnp.stack([zero, sin], axis=-1).reshape(seq, HEAD_DIM)
    return cos_full, sin_even, sin_odd


def _split_router(w_router):
    d, n_experts = w_router.shape
    w = jnp.pad(w_router, ((0, 0), (0, LANES - n_experts)))
    hi = w.astype(BF16)
    lo = (w - hi.astype(F32)).astype(BF16)
    return jnp.concatenate([hi, lo], axis=1), n_experts


def _moe(x_res, h, aff, w_gate, w_up, w_down, layer, gain, *, batch, seq, final_norm):
    n_experts = w_gate.shape[1]
    capacity = EC_CAPACITY_FACTOR * seq // n_experts
    d = h.shape[1]
    prow, arow, pcol = route(aff, batch=batch, seq=seq, n_experts=n_experts, capacity=capacity)
    xs = gather_tokens(h, prow, batch=batch, seq=seq, n_experts=n_experts, capacity=capacity)
    y = expert_ffn(xs.reshape(n_experts * batch * capacity, d), w_gate, w_up, w_down, prow, arow,
                   layer, rows_per_expert=batch * capacity, capacity=capacity)
    return combine(x_res, y, prow, pcol, gain, batch=batch, seq=seq,
                   n_experts=n_experts, capacity=capacity, final_norm=final_norm)


def _cross_block(x, h, mem2d, layer, cross_w_q, cross_w_k, cross_w_v, cross_w_o, mem_norm,
                 ffn_norm, router_w, *, batch, seq, mem_len):
    mem_n = rms_norm_rows(mem2d, mem_norm[layer], BF16)
    k = matmul_streamed(mem_n, cross_w_k, (layer,))
    v = matmul_streamed(mem_n, cross_w_v, (layer,))
    o = cross_attention(h, cast_weight(cross_w_q, (layer,)), k, v,
                        batch=batch, seq=seq, mem_len=mem_len)
    return matmul_residual_norm([o], cast_weight(cross_w_o, (layer,)), x, ffn_norm[layer],
                                router=_split_router(router_w[layer]))


def kernel(x, mem, mix_norm, attn_w_in, q_gain, k_gain, fourier_w, attn_w_out, pool_w, pool_scale,
           cross_norm, mem_norm, cross_w_q, cross_w_k, cross_w_v, cross_w_o, ffn_norm, router_w,
           expert_w_gate, expert_w_up, expert_w_down, final_norm):
    batch, seq, d = x.shape
    mem_len = mem.shape[1]
    depth = mix_norm.shape[0]
    attn_width = N_Q_HEADS * HEAD_DIM
    kv_width = N_KV_HEADS * HEAD_DIM
    fourier_width = N_FOURIER_GROUPS * FOURIER_GROUP
    rope_width = attn_width + kv_width
    cos_t, sin_a, sin_b = _rope_tables(seq)

    xf = x.reshape(batch * seq, d)
    mem2d = mem.reshape(batch * mem_len, d)
    cross = functools.partial(_cross_block, mem2d=mem2d, cross_w_q=cross_w_q, cross_w_k=cross_w_k,
                              cross_w_v=cross_w_v, cross_w_o=cross_w_o, mem_norm=mem_norm,
                              ffn_norm=ffn_norm, router_w=router_w,
                              batch=batch, seq=seq, mem_len=mem_len)
    for layer in range(depth):
        i = layer // 2
        if layer % 2 == 0:
            gain_cols = jnp.concatenate([jnp.tile(q_gain[i], N_Q_HEADS), jnp.tile(k_gain[i], N_KV_HEADS),
                                         jnp.ones((kv_width + fourier_width,), F32)]).reshape(1, -1)
            proj = proj_in(xf, mix_norm[layer], cast_weight(attn_w_in, (i,)), gain_cols,
                           cos_t, sin_a, sin_b, seq=seq, rope_width=rope_width)
            o_attn = gqa_attention(proj, batch=batch, seq=seq, attn_width=attn_width, kv_width=kv_width)
            o_four = fourier_mixer(proj, fourier_w[i], batch=batch, seq=seq,
                                   col0=attn_width + 2 * kv_width)
            xf, h = matmul_residual_norm([o_attn, o_four], cast_weight(attn_w_out, (i,)), xf,
                                         cross_norm[layer])
        else:
            xf, h = pool_mixer(xf, mix_norm[layer], pool_w[i].astype(BF16), pool_scale[i],
                               cross_norm[layer], seq=seq)
        xf, h, aff = cross(xf, h, layer=layer)
        xf = _moe(xf, h, aff, expert_w_gate, expert_w_up, expert_w_down, layer, final_norm,
                  batch=batch, seq=seq, final_norm=layer == depth - 1)
    return xf.reshape(batch, seq, d)
```

```python
import functools
import math

import jax
import jax.numpy as jnp
from jax import lax
from jax.experimental import pallas as pl
from jax.experimental.pallas import tpu as pltpu

F32 = jnp.float32
BF16 = jnp.bfloat16

GRID_W = 64
HEAD_DIM = 128
N_Q_HEADS = 12
N_KV_HEADS = 4
N_FOURIER_GROUPS = 4
FOURIER_GROUP = 128
ROPE_THETA = 10000.0
POOL_WINDOWS = (2, 4, 8, 16)
N_MEM_HEADS = 4
EC_CAPACITY_FACTOR = 2
NORM_EPS = 1e-6

LANES = 128
V7X_VMEM_BYTES = 64 * 1024 * 1024
VMEM_HEADROOM_BYTES = 6 * 1024 * 1024
COL_CHUNK = 512
LOG2_E = math.log2(math.e)


def _params(semantics, vmem_bytes):
    limit = min(int(vmem_bytes) + VMEM_HEADROOM_BYTES, V7X_VMEM_BYTES - VMEM_HEADROOM_BYTES)
    return pltpu.CompilerParams(dimension_semantics=semantics, vmem_limit_bytes=limit)


def _tile(n, pref):
    t = min(n, pref)
    while n % t:
        t //= 2
    return t


def _rms(x, gain):
    ms = jnp.mean(x * x, axis=-1, keepdims=True)
    return x * lax.rsqrt(ms + NORM_EPS) * gain


def _softmax_numerators(scores, scale):
    c = scale * LOG2_E
    return [jnp.exp2((s - jnp.max(s, axis=-1, keepdims=True)) * c) for s in scores]


def _norm_kernel(x_ref, g_ref, o_ref):
    o_ref[...] = _rms(x_ref[...], g_ref[...]).astype(o_ref.dtype)


def rms_norm_rows(x, gain, out_dtype):
    m, d = x.shape
    tm = _tile(m, 512)
    return pl.pallas_call(
        _norm_kernel,
        grid=(m // tm,),
        in_specs=[pl.BlockSpec((tm, d), lambda i: (i, 0)),
                  pl.BlockSpec((1, d), lambda i: (0, 0))],
        out_specs=pl.BlockSpec((tm, d), lambda i: (i, 0)),
        out_shape=jax.ShapeDtypeStruct((m, d), out_dtype),
        compiler_params=_params(("parallel",), 2 * tm * d * 8),
        name="rms_norm_rows",
    )(x, gain.reshape(1, d))


def _cast_kernel(w_ref, o_ref):
    o_ref[...] = w_ref[...].astype(o_ref.dtype)


def cast_weight(w, lead):
    k, n = w.shape[-2:]
    tk = _tile(k, 1024)
    squeezed = (None,) * len(lead)
    return pl.pallas_call(
        _cast_kernel,
        grid=(k // tk,),
        in_specs=[pl.BlockSpec(squeezed + (tk, n), lambda i: tuple(lead) + (i, 0))],
        out_specs=pl.BlockSpec((tk, n), lambda i: (i, 0)),
        out_shape=jax.ShapeDtypeStruct((k, n), BF16),
        compiler_params=_params(("parallel",), 2 * tk * n * 6),
        name="cast_weight",
    )(w)


def _mm_stream_kernel(a_ref, w_ref, o_ref):
    o_ref[...] = jnp.dot(a_ref[...], w_ref[...].astype(BF16),
                         preferred_element_type=F32).astype(o_ref.dtype)


def matmul_streamed(a, w, lead, out_dtype=BF16):
    m, k = a.shape
    n = w.shape[-1]
    tn = _tile(n, COL_CHUNK)
    squeezed = (None,) * len(lead)
    vmem = 2 * (m * k * 2 + k * tn * 4 + m * tn * 2) + k * tn * 2 + m * tn * 4
    return pl.pallas_call(
        _mm_stream_kernel,
        grid=(n // tn,),
        in_specs=[pl.BlockSpec((m, k), lambda j: (0, 0)),
                  pl.BlockSpec(squeezed + (k, tn), lambda j: tuple(lead) + (0, j))],
        out_specs=pl.BlockSpec((m, tn), lambda j: (0, j)),
        out_shape=jax.ShapeDtypeStruct((m, n), out_dtype),
        compiler_params=_params(("parallel",), vmem),
        name="matmul_streamed",
    )(a, w)


def _proj_in_kernel(a_ref, ng_ref, w_ref, g_ref, cos_ref, sa_ref, sb_ref, o_ref, *, n_rope_heads):
    n = o_ref.shape[1]
    tn = _tile(n, COL_CHUNK)
    a = _rms(a_ref[...], ng_ref[...]).astype(BF16)
    cos, sin_a, sin_b = cos_ref[...], sa_ref[...], sb_ref[...]
    chunks = [jnp.dot(a, w_ref[:, c * tn:(c + 1) * tn], preferred_element_type=F32)
              for c in range(n // tn)]
    for c, acc in enumerate(chunks):
        for h in range(tn // HEAD_DIM):
            head = c * (tn // HEAD_DIM) + h
            col = slice(head * HEAD_DIM, (head + 1) * HEAD_DIM)
            xh = acc[:, h * HEAD_DIM:(h + 1) * HEAD_DIM]
            if head < n_rope_heads:
                ms = jnp.mean(xh * xh, axis=-1, keepdims=True)
                y = xh * lax.rsqrt(ms + NORM_EPS) * g_ref[:, col]
                xh = (y * cos + pltpu.roll(y, HEAD_DIM - 1, axis=1) * sin_a
                      + pltpu.roll(y, 1, axis=1) * sin_b)
            o_ref[:, col] = xh.astype(o_ref.dtype)


def proj_in(x, norm_gain, w_in, gain_cols, cos_t, sin_a, sin_b, *, seq, rope_width):
    m, k = x.shape
    n = w_in.shape[1]
    tm = _tile(seq, 512)
    n_pos_blocks = seq // tm
    vmem = 2 * (tm * k * 4 + k * n * 2 + tm * n * 2 + 3 * tm * HEAD_DIM * 4) + 6 * tm * COL_CHUNK * 4 \
        + 2 * tm * k * 4
    table = pl.BlockSpec((tm, HEAD_DIM), lambda i: (i % n_pos_blocks, 0))
    return pl.pallas_call(
        functools.partial(_proj_in_kernel, n_rope_heads=rope_width // HEAD_DIM),
        grid=(m // tm,),
        in_specs=[pl.BlockSpec((tm, k), lambda i: (i, 0)),
                  pl.BlockSpec((1, k), lambda i: (0, 0)),
                  pl.BlockSpec((k, n), lambda i: (0, 0)),
                  pl.BlockSpec((1, n), lambda i: (0, 0)),
                  table, table, table],
        out_specs=pl.BlockSpec((tm, n), lambda i: (i, 0)),
        out_shape=jax.ShapeDtypeStruct((m, n), BF16),
        compiler_params=_params(("parallel",), vmem),
        name="proj_in",
    )(x, norm_gain.reshape(1, k), w_in, gain_cols, cos_t, sin_a, sin_b)


def _attn_kernel(q_ref, k_ref, v_ref, o_ref, *, group, scale):
    k = k_ref[...]
    v = v_ref[...]
    v_ones = jnp.concatenate([v, jnp.ones_like(v)], axis=1)
    heads = [slice(g * HEAD_DIM, (g + 1) * HEAD_DIM) for g in range(group)]
    scores = [lax.dot_general(q_ref[:, sl], k, (((1,), (1,)), ((), ())),
                              preferred_element_type=F32) for sl in heads]
    for sl, e in zip(heads, _softmax_numerators(scores, scale)):
        ol = jnp.dot(e.astype(BF16), v_ones, preferred_element_type=F32)
        o_ref[:, sl] = (ol[:, :HEAD_DIM] / ol[:, HEAD_DIM:HEAD_DIM + 1]).astype(o_ref.dtype)


def gqa_attention(proj, *, batch, seq, attn_width, kv_width):
    group = N_Q_HEADS // N_KV_HEADS
    gw = group * HEAD_DIM
    tq = _tile(seq, 1024)
    nq = seq // tq
    k_col0 = attn_width // HEAD_DIM
    v_col0 = (attn_width + kv_width) // HEAD_DIM
    vmem = 2 * (tq * gw * 2 * 2 + 2 * seq * HEAD_DIM * 2) + group * tq * seq * 6
    return pl.pallas_call(
        functools.partial(_attn_kernel, group=group, scale=HEAD_DIM ** -0.5),
        grid=(batch, N_KV_HEADS, nq),
        in_specs=[pl.BlockSpec((tq, gw), lambda b, h, i: (b * nq + i, h)),
                  pl.BlockSpec((seq, HEAD_DIM), lambda b, h, i: (b, k_col0 + h)),
                  pl.BlockSpec((seq, HEAD_DIM), lambda b, h, i: (b, v_col0 + h))],
        out_specs=pl.BlockSpec((tq, gw), lambda b, h, i: (b * nq + i, h)),
        out_shape=jax.ShapeDtypeStruct((batch * seq, attn_width), BF16),
        compiler_params=_params(("parallel", "parallel", "parallel"), vmem),
        name="gqa_attention",
    )(proj, proj, proj)


def _fourier_kernel(f_ref, cc_ref, sc_ref, cs_ref, ss_ref, wf_ref, o_ref, ab_ref, *, seq, norm):
    r = pl.program_id(1)
    ng = wf_ref.shape[0]
    cg = wf_ref.shape[1]

    @pl.when(r == 0)
    def _():
        for g in range(ng):
            fg = f_ref[:, g * cg:(g + 1) * cg]
            ab_ref[0:seq, g * cg:(g + 1) * cg] = jnp.dot(
                fg, cc_ref[...], preferred_element_type=F32).astype(BF16)
            ab_ref[seq:2 * seq, g * cg:(g + 1) * cg] = jnp.dot(
                fg, sc_ref[...], preferred_element_type=F32).astype(BF16)

    z = (jnp.dot(cs_ref[...], ab_ref[0:seq, :], preferred_element_type=F32)
         - jnp.dot(ss_ref[...], ab_ref[seq:2 * seq, :], preferred_element_type=F32)) * norm
    for g in range(ng):
        zg = z[:, g * cg:(g + 1) * cg].astype(BF16)
        o_ref[:, g * cg:(g + 1) * cg] = jnp.dot(
            zg, wf_ref[g].astype(BF16), preferred_element_type=F32).astype(o_ref.dtype)


def _dft_cos_sin(n):
    lo = _tile(n, 64)
    hi = n // lo
    j = jnp.arange(n, dtype=jnp.int32)

    def table(k):
        ang = ((k[:, None] * j[None, :]) % n).astype(F32) * (2.0 * math.pi / n)
        return jnp.cos(ang), jnp.sin(ang)

    ca, sa = table(jnp.arange(hi, dtype=jnp.int32) * lo)
    cb, sb = table(jnp.arange(lo, dtype=jnp.int32))
    cos = ca[:, None, :] * cb[None, :, :] - sa[:, None, :] * sb[None, :, :]
    sin = sa[:, None, :] * cb[None, :, :] + ca[:, None, :] * sb[None, :, :]
    return cos.reshape(n, n).astype(BF16), sin.reshape(n, n).astype(BF16)


def fourier_mixer(proj, w_fourier, *, batch, seq, col0):
    ng, cg, _ = w_fourier.shape
    fw = ng * cg
    tr = _tile(seq, 512)
    nr = seq // tr
    cc, sc = _dft_cos_sin(cg)
    cs, ss = _dft_cos_sin(seq)
    norm = 1.0 / math.sqrt(seq * cg)
    vmem = 2 * (seq * fw * 2 + 2 * cg * cg * 2 + 2 * tr * seq * 2 + ng * cg * cg * 4 + tr * fw * 2) \
        + 2 * seq * fw * 2 + 4 * tr * fw * 4
    return pl.pallas_call(
        functools.partial(_fourier_kernel, seq=seq, norm=norm),
        grid=(batch, nr),
        in_specs=[pl.BlockSpec((seq, fw), lambda b, r: (b, col0 // fw)),
                  pl.BlockSpec((cg, cg), lambda b, r: (0, 0)),
                  pl.BlockSpec((cg, cg), lambda b, r: (0, 0)),
                  pl.BlockSpec((tr, seq), lambda b, r: (r, 0)),
                  pl.BlockSpec((tr, seq), lambda b, r: (r, 0)),
                  pl.BlockSpec((ng, cg, cg), lambda b, r: (0, 0, 0))],
        out_specs=pl.BlockSpec((tr, fw), lambda b, r: (b * nr + r, 0)),
        out_shape=jax.ShapeDtypeStruct((batch * seq, fw), BF16),
        scratch_shapes=[pltpu.VMEM((2 * seq, fw), BF16)],
        compiler_params=_params(("parallel", "arbitrary"), vmem),
        name="fourier_mixer",
    )(proj, cc, sc, cs, ss, w_fourier)


def _router_affinity(hn, wr_ref, n_experts):
    hi = hn.astype(BF16)
    lo = (hn - hi.astype(F32)).astype(BF16)
    r_hi = jnp.dot(hi, wr_ref[...], preferred_element_type=F32)
    r_lo = jnp.dot(lo, wr_ref[...], preferred_element_type=F32)
    logits = r_hi[:, :LANES] + r_hi[:, LANES:] + r_lo[:, :LANES]
    lane = lax.broadcasted_iota(jnp.int32, logits.shape, 1)
    logits = jnp.where(lane < n_experts, logits, -jnp.inf)
    e = jnp.exp(logits - jnp.max(logits, axis=-1, keepdims=True))
    return e / jnp.sum(e, axis=-1, keepdims=True)


def _mm_res_norm_kernel(*refs, n_a, with_router, n_experts):
    a_refs = refs[:n_a]
    w_ref, x_ref, g_ref = refs[n_a:n_a + 3]
    pos = n_a + 3
    if with_router:
        wr_ref = refs[pos]
        pos += 1
    xo_ref, ho_ref = refs[pos:pos + 2]
    if with_router:
        aff_ref = refs[pos + 2]
    d = xo_ref.shape[1]
    tn = _tile(d, COL_CHUNK)
    for c in range(d // tn):
        sl = slice(c * tn, (c + 1) * tn)
        acc = x_ref[:, sl]
        row0 = 0
        for a_ref in a_refs:
            ka = a_ref.shape[1]
            acc = acc + jnp.dot(a_ref[...], w_ref[row0:row0 + ka, sl], preferred_element_type=F32)
            row0 += ka
        xo_ref[:, sl] = acc
    hn = _rms(xo_ref[...], g_ref[...])
    ho_ref[...] = hn.astype(ho_ref.dtype)
    if with_router:
        aff_ref[...] = _router_affinity(hn, wr_ref, n_experts)


def matmul_residual_norm(a_list, w, x_res, gain, *, router=None, tm_pref=512):
    m, d = x_res.shape
    k = w.shape[0]
    tm = _tile(m, tm_pref)
    in_specs = [pl.BlockSpec((tm, a.shape[1]), lambda i: (i, 0)) for a in a_list]
    in_specs += [pl.BlockSpec((k, d), lambda i: (0, 0)),
                 pl.BlockSpec((tm, d), lambda i: (i, 0)),
                 pl.BlockSpec((1, d), lambda i: (0, 0))]
    args = list(a_list) + [w, x_res, gain.reshape(1, d)]
    out_specs = [pl.BlockSpec((tm, d), lambda i: (i, 0)),
                 pl.BlockSpec((tm, d), lambda i: (i, 0))]
    out_shape = [jax.ShapeDtypeStruct((m, d), F32), jax.ShapeDtypeStruct((m, d), BF16)]
    n_experts = 0
    if router is not None:
        wr, n_experts = router
        in_specs.append(pl.BlockSpec((d, 2 * LANES), lambda i: (0, 0)))
        args.append(wr)
        out_specs.append(pl.BlockSpec((tm, LANES), lambda i: (i, 0)))
        out_shape.append(jax.ShapeDtypeStruct((m, LANES), F32))
    vmem = 2 * (tm * k * 2 + k * d * 2 + tm * d * 4 + tm * d * 4 + tm * d * 2) \
        + 3 * tm * d * 4 + 4 * d * LANES * 2
    return pl.pallas_call(
        functools.partial(_mm_res_norm_kernel, n_a=len(a_list),
                          with_router=router is not None, n_experts=n_experts),
        grid=(m // tm,),
        in_specs=in_specs,
        out_specs=out_specs,
        out_shape=out_shape,
        compiler_params=_params(("parallel",), vmem),
        name="matmul_residual_norm",
    )(*args)


def _cross_attn_kernel(h_ref, wq_ref, k_ref, v_ref, o_ref, *, n_heads, scale):
    dh = h_ref.shape[1] // n_heads
    h = h_ref[...]
    heads = [slice(i * dh, (i + 1) * dh) for i in range(n_heads)]
    qs = [jnp.dot(h, wq_ref[:, sl], preferred_element_type=F32).astype(BF16) for sl in heads]
    scores = [lax.dot_general(q, k_ref[:, sl], (((1,), (1,)), ((), ())),
                              preferred_element_type=F32) for q, sl in zip(qs, heads)]
    for sl, e in zip(heads, _softmax_numerators(scores, scale)):
        l = jnp.sum(e, axis=-1, keepdims=True)
        o = jnp.dot(e.astype(BF16), v_ref[:, sl], preferred_element_type=F32) / l
        o_ref[:, sl] = o.astype(o_ref.dtype)


def cross_attention(h, w_q, k, v, *, batch, seq, mem_len):
    d = h.shape[1]
    tq = _tile(seq, 512)
    nq = seq // tq
    vmem = 2 * (2 * tq * d * 2 + d * d * 2 + 2 * mem_len * d * 2) + 6 * tq * mem_len * 4 + 3 * tq * d * 4
    return pl.pallas_call(
        functools.partial(_cross_attn_kernel, n_heads=N_MEM_HEADS,
                          scale=(d // N_MEM_HEADS) ** -0.5),
        grid=(batch, nq),
        in_specs=[pl.BlockSpec((tq, d), lambda b, i: (b * nq + i, 0)),
                  pl.BlockSpec((d, d), lambda b, i: (0, 0)),
                  pl.BlockSpec((mem_len, d), lambda b, i: (b, 0)),
                  pl.BlockSpec((mem_len, d), lambda b, i: (b, 0))],
        out_specs=pl.BlockSpec((tq, d), lambda b, i: (b * nq + i, 0)),
        out_shape=jax.ShapeDtypeStruct((batch * seq, d), BF16),
        compiler_params=_params(("parallel", "parallel"), vmem),
        name="cross_attention",
    )(h, w_q, k, v)


def _prefix_count(x, lane):
    n = x.shape[1]
    shift = 1
    while shift < n:
        x = x + jnp.where(lane >= shift, pltpu.roll(x, shift, axis=1), 0)
        shift *= 2
    return x


def _route_kernel(aff_ref, prow_ref, arow_ref, pcol_ref, *, n_experts, capacity):
    e_pad = prow_ref.shape[1]
    a_t = aff_ref[...].T[:e_pad]
    arow_ref[0] = a_t
    seq = a_t.shape[1]

    def body(i, thr_bits):
        cand = thr_bits | jnp.left_shift(jnp.int32(1), 30 - i)
        cnt = jnp.sum((a_t >= pltpu.bitcast(cand, F32)).astype(F32), axis=1, keepdims=True)
        return jnp.where(cnt >= capacity, cand, thr_bits)

    thr = pltpu.bitcast(lax.fori_loop(0, 31, body, jnp.zeros((e_pad, 1), jnp.int32)), F32)
    lane = lax.broadcasted_iota(jnp.int32, a_t.shape, 1)
    gt = a_t > thr
    eq = a_t == thr
    need = capacity - jnp.sum(gt.astype(F32), axis=1, keepdims=True).astype(jnp.int32)
    sel = gt | (eq & (_prefix_count(eq.astype(jnp.int32), lane) <= need))
    pos = jnp.where(sel, _prefix_count(sel.astype(jnp.int32), lane) - 1, -1)
    row = lax.broadcasted_iota(jnp.int32, a_t.shape, 0)
    pos = jnp.where(row < n_experts, pos, -1)
    prow_ref[0] = pos
    unused = jnp.full((LANES - e_pad, seq), -1.0, F32)
    pcol_ref[...] = jnp.concatenate([pos.astype(F32), unused], axis=0).T


def route(aff, *, batch, seq, n_experts, capacity):
    e_pad = max(8, n_experts)
    return pl.pallas_call(
        functools.partial(_route_kernel, n_experts=n_experts, capacity=capacity),
        grid=(batch,),
        in_specs=[pl.BlockSpec((seq, LANES), lambda b: (b, 0))],
        out_specs=[pl.BlockSpec((1, e_pad, seq), lambda b: (b, 0, 0)),
                   pl.BlockSpec((1, e_pad, seq), lambda b: (b, 0, 0)),
                   pl.BlockSpec((seq, LANES), lambda b: (b, 0))],
        out_shape=[jax.ShapeDtypeStruct((batch, e_pad, seq), jnp.int32),
                   jax.ShapeDtypeStruct((batch, e_pad, seq), F32),
                   jax.ShapeDtypeStruct((batch * seq, LANES), F32)],
        compiler_params=_params(("parallel",), 24 * seq * LANES * 4),
        name="route",
    )(aff)


SLOT_TILES = 4


def _window_start(tile, capacity, window, clip=jnp.clip):
    share = capacity // SLOT_TILES
    unit = (window - share) // 2
    assert share % unit == 0 and (capacity - window) % unit == 0
    return clip(tile * (share // unit) - 1, 0, (capacity - window) // unit) * unit


def _static_clip(v, lo, hi):
    return max(lo, min(v, hi))


def _slots_inside_windows(prow, *, n_experts, capacity, window):
    batch, _, seq = prow.shape
    pos = prow[:, :n_experts, :].reshape(batch, n_experts, SLOT_TILES, seq // SLOT_TILES)
    first = _window_start(jnp.arange(SLOT_TILES, dtype=jnp.int32), capacity, window)[None, None, :]
    lowest = jnp.min(jnp.where(pos >= 0, pos, capacity), axis=-1)
    highest = jnp.max(pos, axis=-1)
    return (lowest >= first) & (highest < first + window)


def _gather_kernel(ok_ref, h_ref, prow_ref, o_ref, *, experts_per_step, window):
    b = pl.program_id(0)
    g = pl.program_id(1)
    cap = o_ref.shape[1]
    seq = h_ref.shape[0]
    tt = seq // SLOT_TILES
    starts = [_window_start(t, cap, window, _static_clip) for t in range(SLOT_TILES)]
    unit = (window - cap // SLOT_TILES) // 2
    for j in range(experts_per_step):
        expert = g * experts_per_step + j
        prow = prow_ref[0, pl.ds(expert, 1), :]
        in_windows = ok_ref[b, expert] != 0

        @pl.when(in_windows)
        def _(j=j, prow=prow):
            slot = lax.broadcasted_iota(jnp.int32, (window, tt), 0)
            parts = [jnp.dot((slot + starts[t] == prow[:, t * tt:(t + 1) * tt]).astype(BF16),
                             h_ref[t * tt:(t + 1) * tt, :], preferred_element_type=F32)
                     for t in range(SLOT_TILES)]
            for r0 in range(0, cap, unit):
                covering = [parts[t][r0 - starts[t]:r0 - starts[t] + unit]
                            for t in range(SLOT_TILES) if starts[t] <= r0 < starts[t] + window]
                o_ref[j, r0:r0 + unit, :] = functools.reduce(jnp.add, covering).astype(o_ref.dtype)

        @pl.when(jnp.logical_not(in_windows))
        def _(j=j, prow=prow):
            slot = lax.broadcasted_iota(jnp.int32, (cap, seq), 0)
            o_ref[j] = jnp.dot((slot == prow).astype(BF16), h_ref[...],
                               preferred_element_type=F32).astype(o_ref.dtype)


def gather_tokens(h, prow, *, batch, seq, n_experts, capacity):
    d = h.shape[1]
    e_pad = prow.shape[1]
    ng = _tile(n_experts, 4)
    window = capacity // 2
    ok = jnp.all(_slots_inside_windows(prow, n_experts=n_experts, capacity=capacity, window=window),
                 axis=-1).astype(jnp.int32)
    vmem = 2 * (seq * d * 2 + e_pad * seq * 4 + ng * capacity * d * 2) + 2 * capacity * seq * 8 \
        + 3 * capacity * d * 4
    return pl.pallas_call(
        functools.partial(_gather_kernel, experts_per_step=ng, window=window),
        grid_spec=pltpu.PrefetchScalarGridSpec(
            num_scalar_prefetch=1,
            grid=(batch, n_experts // ng),
            in_specs=[pl.BlockSpec((seq, d), lambda b, g, ok_ref: (b, 0)),
                      pl.BlockSpec((1, e_pad, seq), lambda b, g, ok_ref: (b, 0, 0))],
            out_specs=pl.BlockSpec((ng, None, capacity, d), lambda b, g, ok_ref: (g, b, 0, 0))),
        out_shape=jax.ShapeDtypeStruct((n_experts, batch, capacity, d), BF16),
        compiler_params=_params(("parallel", "arbitrary"), vmem),
        name="gather_tokens",
    )(ok, h, prow)


def _ffn_kernel(x_ref, wg_ref, wu_ref, wd_ref, prow_ref, arow_ref, o_ref, acc_ref, gate_ref,
                *, capacity, gate_steps):
    e = pl.program_id(0)
    f = pl.program_id(1)
    nf = pl.num_programs(1)
    rows = x_ref.shape[0]
    n_seq = rows // capacity

    def chunk_product():
        x = x_ref[...]
        a = jnp.dot(x, wg_ref[...].astype(BF16), preferred_element_type=F32)
        u = jnp.dot(x, wu_ref[...].astype(BF16), preferred_element_type=F32)
        hidden = (a / (1.0 + jnp.exp(-a)) * u).astype(BF16)
        return jnp.dot(hidden, wd_ref[...].astype(BF16), preferred_element_type=F32)

    def gate_share():
        seq = prow_ref.shape[2]
        slot = lax.broadcasted_iota(jnp.int32, (capacity, seq), 0)
        per_step = -(-n_seq // gate_steps)
        for j in range(per_step):
            b = jnp.minimum(f * per_step + j, n_seq - 1)
            mine = slot == prow_ref[b, pl.ds(e, 1), :]
            gate_ref[pl.ds(pl.multiple_of(b * capacity, capacity), capacity), :] = jnp.sum(
                jnp.where(mine, arow_ref[b, pl.ds(e, 1), :], 0.0), axis=1, keepdims=True)

    @pl.when(f == 0)
    def _():
        gate_share()
        acc_ref[...] = chunk_product()

    @pl.when((f > 0) & (f < nf - 1))
    def _():
        gate_share()
        acc_ref[...] += chunk_product()

    @pl.when(f == nf - 1)
    def _():
        gate_share()
        o_ref[...] = ((acc_ref[...] + chunk_product()) * gate_ref[...]).astype(o_ref.dtype)


def expert_ffn(xs, w_gate, w_up, w_down, prow, arow, layer, *, rows_per_expert, capacity):
    _, n_experts, d, ff = w_gate.shape
    tf = _tile(ff // 2, 256)
    r = rows_per_expert
    batch, e_pad, seq = prow.shape
    vmem = 2 * (r * d * 2 + 3 * d * tf * 4 + r * d * 2 + 2 * batch * e_pad * seq * 4) + r * d * 4 \
        + r * LANES * 4 + 3 * d * tf * 2 + 4 * r * tf * 4 + r * d * 4 + 2 * capacity * seq * 4
    return pl.pallas_call(
        functools.partial(_ffn_kernel, capacity=capacity, gate_steps=ff // tf),
        grid=(n_experts, ff // tf),
        in_specs=[pl.BlockSpec((r, d), lambda e, f: (e, 0)),
                  pl.BlockSpec((None, None, d, tf), lambda e, f: (layer, e, 0, f)),
                  pl.BlockSpec((None, None, d, tf), lambda e, f: (layer, e, 0, f)),
                  pl.BlockSpec((None, None, tf, d), lambda e, f: (layer, e, f, 0)),
                  pl.BlockSpec((batch, e_pad, seq), lambda e, f: (0, 0, 0)),
                  pl.BlockSpec((batch, e_pad, seq), lambda e, f: (0, 0, 0))],
        out_specs=pl.BlockSpec((r, d), lambda e, f: (e, 0)),
        out_shape=jax.ShapeDtypeStruct((n_experts * r, d), BF16),
        scratch_shapes=[pltpu.VMEM((r, d), F32), pltpu.VMEM((r, 1), F32)],
        compiler_params=_params(("parallel", "arbitrary"), vmem),
        name="expert_ffn",
    )(xs, w_gate, w_up, w_down, prow, arow)


def _combine_kernel(ok_ref, x_ref, ywin_ref, yall_ref, pcol_ref, g_ref, out_ref, *, final_norm):
    b = pl.program_id(0)
    t = pl.program_id(1)
    g = pl.program_id(2)
    tt, d = x_ref.shape
    n_group, _, window, _ = ywin_ref.shape
    cap = yall_ref.shape[1]

    @pl.when(g == 0)
    def _():
        out_ref[...] = x_ref[...]

    lane = lax.broadcasted_iota(jnp.int32, (tt, LANES), 1)
    pcol = pcol_ref[...]

    def scatter(y_rows, n_slots, first_slot):
        slot = (lax.broadcasted_iota(jnp.int32, (tt, n_slots), 1) + first_slot).astype(F32)
        hots = []
        for j in range(n_group):
            slot_of_token = jnp.sum(jnp.where(lane == g * n_group + j, pcol, 0.0),
                                    axis=1, keepdims=True)
            hots.append((slot == slot_of_token).astype(BF16))
        out_ref[...] += jnp.dot(jnp.concatenate(hots, axis=1), y_rows, preferred_element_type=F32)

    in_window = ok_ref[b, t, g] != 0

    @pl.when(in_window)
    def _():
        scatter(ywin_ref[...].reshape(n_group * window, d), window, _window_start(t, cap, window))

    @pl.when(jnp.logical_not(in_window))
    def _():
        scatter(yall_ref[...].reshape(n_group * cap, d), cap, 0)

    if final_norm:
        @pl.when(g == pl.num_programs(2) - 1)
        def _():
            out_ref[...] = _rms(out_ref[...], g_ref[...])


def combine(x_res, y, prow, pcol, gain, *, batch, seq, n_experts, capacity, final_norm):
    d = x_res.shape[1]
    nt = SLOT_TILES
    tt = seq // nt
    window = capacity // 2
    ng = _tile(n_experts, 8)
    n_groups = n_experts // ng
    y4 = y.reshape(n_experts, batch, capacity, d)
    inside = _slots_inside_windows(prow, n_experts=n_experts, capacity=capacity, window=window)
    ok = jnp.all(inside.reshape(batch, n_groups, ng, nt), axis=2).transpose(0, 2, 1).astype(jnp.int32)
    vmem = 2 * (tt * d * 4 + ng * window * d * 2 + ng * capacity * d * 2 + tt * LANES * 4 + tt * d * 4) \
        + 3 * tt * d * 4 + 2 * tt * capacity * 4

    def full_block(b, t, g, ok_ref):
        need = ok_ref[b, t, g] == 0
        return (jnp.where(need, g, 0), jnp.where(need, b, 0), 0, 0)

    return pl.pallas_call(
        functools.partial(_combine_kernel, final_norm=final_norm),
        grid_spec=pltpu.PrefetchScalarGridSpec(
            num_scalar_prefetch=1,
            grid=(batch, nt, n_groups),
            in_specs=[pl.BlockSpec((tt, d), lambda b, t, g, ok_ref: (b * nt + t, 0)),
                      pl.BlockSpec((pl.Element(ng), pl.Element(1), pl.Element(window), pl.Element(d)),
                                   lambda b, t, g, ok_ref: (g * ng, b, _window_start(t, capacity, window), 0)),
                      pl.BlockSpec((ng, None, capacity, d), full_block),
                      pl.BlockSpec((tt, LANES), lambda b, t, g, ok_ref: (b * nt + t, 0)),
                      pl.BlockSpec((1, d), lambda b, t, g, ok_ref: (0, 0))],
            out_specs=pl.BlockSpec((tt, d), lambda b, t, g, ok_ref: (b * nt + t, 0))),
        out_shape=jax.ShapeDtypeStruct((batch * seq, d), F32),
        compiler_params=_params(("parallel", "parallel", "arbitrary"), vmem),
        name="combine",
    )(ok, x_res, y4, y4, pcol, gain.reshape(1, d))


def _pool_kernel(x_ref, prev_ref, next_ref, ng_ref, w_ref, s_ref, g_ref, xo_ref, ho_ref, pad_ref,
                 *, seq, windows):
    i = pl.program_id(0)
    tm, d = x_ref.shape
    halo = prev_ref.shape[0]
    pg = w_ref.shape[1]
    tiles_per_seq = seq // tm
    tile_in_seq = i % tiles_per_seq
    norm_gain = ng_ref[...]
    h = _rms(x_ref[...], norm_gain)
    pad_ref[0:halo, :] = jnp.where(tile_in_seq == 0, 0.0, _rms(prev_ref[...], norm_gain))
    pad_ref[halo:halo + tm, :] = h
    pad_ref[halo + tm:2 * halo + tm, :] = jnp.where(tile_in_seq == tiles_per_seq - 1, 0.0,
                                                    _rms(next_ref[...], norm_gain))
    t = tile_in_seq * tm + lax.broadcasted_iota(jnp.int32, (tm, 1), 0)
    for gi, w in enumerate(windows):
        cols = slice(gi * pg, (gi + 1) * pg)
        acc = pad_ref[:, cols]
        span = 1
        while span < w:
            acc = acc + pltpu.roll(acc, acc.shape[0] - span, axis=0)
            span *= 2
        total = acc[halo - w // 2:halo - w // 2 + tm, :]
        count = (jnp.minimum(t + (w - w // 2), seq) - jnp.maximum(t - w // 2, 0)).astype(F32)
        pooled = (total / count - h[:, cols]).astype(BF16)
        mixed = jnp.dot(pooled, w_ref[gi], preferred_element_type=F32)
        xo_ref[:, cols] = mixed * s_ref[:, cols] + x_ref[:, cols]
    ho_ref[...] = _rms(xo_ref[...], g_ref[...]).astype(ho_ref.dtype)


def pool_mixer(x, norm_gain, w_pool, pool_scale, next_gain, *, seq):
    m, d = x.shape
    ng, pg, _ = w_pool.shape
    halo = 8
    assert all(w & (w - 1) == 0 and w // 2 <= halo for w in POOL_WINDOWS) and ng == len(POOL_WINDOWS)
    tm = _tile(seq, 512)
    per = tm // halo
    n_halo_blocks = m // halo
    vmem = 2 * (2 * tm * d * 4 + 2 * halo * d * 4 + ng * pg * pg * 2 + tm * d * 2) \
        + (tm + 2 * halo) * d * 4 + 6 * tm * d * 4
    return pl.pallas_call(
        functools.partial(_pool_kernel, seq=seq, windows=POOL_WINDOWS),
        grid=(m // tm,),
        in_specs=[pl.BlockSpec((tm, d), lambda i: (i, 0)),
                  pl.BlockSpec((halo, d), lambda i: (jnp.maximum(i * per - 1, 0), 0)),
                  pl.BlockSpec((halo, d), lambda i: (jnp.minimum((i + 1) * per, n_halo_blocks - 1), 0)),
                  pl.BlockSpec((1, d), lambda i: (0, 0)),
                  pl.BlockSpec((ng, pg, pg), lambda i: (0, 0, 0)),
                  pl.BlockSpec((1, d), lambda i: (0, 0)),
                  pl.BlockSpec((1, d), lambda i: (0, 0))],
        out_specs=[pl.BlockSpec((tm, d), lambda i: (i, 0)),
                   pl.BlockSpec((tm, d), lambda i: (i, 0))],
        out_shape=[jax.ShapeDtypeStruct((m, d), F32), jax.ShapeDtypeStruct((m, d), BF16)],
        scratch_shapes=[pltpu.VMEM((tm + 2 * halo, d), F32)],
        compiler_params=_params(("parallel",), vmem),
        name="pool_mixer",
    )(x, x, x, norm_gain.reshape(1, d), w_pool, pool_scale.reshape(1, d), next_gain.reshape(1, d))


def _rope_tables(seq):
    half = HEAD_DIM // 2
    rows = seq // GRID_W
    row_idx = jnp.repeat(jnp.arange(rows), GRID_W).astype(F32)
    col_idx = jnp.tile(jnp.arange(GRID_W), rows).astype(F32)
    inv_freq = 1.0 / (ROPE_THETA ** (jnp.arange(0, half, 2, dtype=F32) / half))
    ang = jnp.concatenate([row_idx[:, None] * inv_freq[None, :],
                           col_idx[:, None] * inv_freq[None, :]], axis=-1)
    cos, sin = jnp.cos(ang), jnp.sin(ang)
    zero = jnp.zeros_like(sin)
    cos_full = jnp.stack([cos, cos], axis=-1).reshape(seq, HEAD_DIM)
    sin_even = jnp.stack([-sin, zero], axis=-1).reshape(seq, HEAD_DIM)
    sin_odd = jnp.stack([zero, sin], axis=-1).reshape(seq, HEAD_DIM)
    return cos_full, sin_even, sin_odd


def _split_router(w_router):
    d, n_experts = w_router.shape
    w = jnp.pad(w_router, ((0, 0), (0, LANES - n_experts)))
    hi = w.astype(BF16)
    lo = (w - hi.astype(F32)).astype(BF16)
    return jnp.concatenate([hi, lo], axis=1), n_experts


def _moe(x_res, h, aff, w_gate, w_up, w_down, layer, gain, *, batch, seq, final_norm):
    n_experts = w_gate.shape[1]
    capacity = EC_CAPACITY_FACTOR * seq // n_experts
    d = h.shape[1]
    prow, arow, pcol = route(aff, batch=batch, seq=seq, n_experts=n_experts, capacity=capacity)
    xs = gather_tokens(h, prow, batch=batch, seq=seq, n_experts=n_experts, capacity=capacity)
    y = expert_ffn(xs.reshape(n_experts * batch * capacity, d), w_gate, w_up, w_down, prow, arow,
                   layer, rows_per_expert=batch * capacity, capacity=capacity)
    return combine(x_res, y, prow, pcol, gain, batch=batch, seq=seq,
                   n_experts=n_experts, capacity=capacity, final_norm=final_norm)


def _cross_block(x, h, mem2d, layer, cross_w_q, cross_w_k, cross_w_v, cross_w_o, mem_norm,
                 ffn_norm, router_w, *, batch, seq, mem_len):
    mem_n = rms_norm_rows(mem2d, mem_norm[layer], BF16)
    k = matmul_streamed(mem_n, cross_w_k, (layer,))
    v = matmul_streamed(mem_n, cross_w_v, (layer,))
    o = cross_attention(h, cast_weight(cross_w_q, (layer,)), k, v,
                        batch=batch, seq=seq, mem_len=mem_len)
    return matmul_residual_norm([o], cast_weight(cross_w_o, (layer,)), x, ffn_norm[layer],
                                router=_split_router(router_w[layer]))


def kernel(x, mem, mix_norm, attn_w_in, q_gain, k_gain, fourier_w, attn_w_out, pool_w, pool_scale,
           cross_norm, mem_norm, cross_w_q, cross_w_k, cross_w_v, cross_w_o, ffn_norm, router_w,
           expert_w_gate, expert_w_up, expert_w_down, final_norm):
    batch, seq, d = x.shape
    mem_len = mem.shape[1]
    depth = mix_norm.shape[0]
    attn_width = N_Q_HEADS * HEAD_DIM
    kv_width = N_KV_HEADS * HEAD_DIM
    fourier_width = N_FOURIER_GROUPS * FOURIER_GROUP
    rope_width = attn_width + kv_width
    cos_t, sin_a, sin_b = _rope_tables(seq)

    xf = x.reshape(batch * seq, d)
    mem2d = mem.reshape(batch * mem_len, d)
    cross = functools.partial(_cross_block, mem2d=mem2d, cross_w_q=cross_w_q, cross_w_k=cross_w_k,
                              cross_w_v=cross_w_v, cross_w_o=cross_w_o, mem_norm=mem_norm,
                              ffn_norm=ffn_norm, router_w=router_w,
                              batch=batch, seq=seq, mem_len=mem_len)
    for layer in range(depth):
        i = layer // 2
        if layer % 2 == 0:
            gain_cols = jnp.concatenate([jnp.tile(q_gain[i], N_Q_HEADS), jnp.tile(k_gain[i], N_KV_HEADS),
                                         jnp.ones((kv_width + fourier_width,), F32)]).reshape(1, -1)
            proj = proj_in(xf, mix_norm[layer], cast_weight(attn_w_in, (i,)), gain_cols,
                           cos_t, sin_a, sin_b, seq=seq, rope_width=rope_width)
            o_attn = gqa_attention(proj, batch=batch, seq=seq, attn_width=attn_width, kv_width=kv_width)
            o_four = fourier_mixer(proj, fourier_w[i], batch=batch, seq=seq,
                                   col0=attn_width + 2 * kv_width)
            xf, h = matmul_residual_norm([o_attn, o_four], cast_weight(attn_w_out, (i,)), xf,
                                         cross_norm[layer])
        else:
            xf, h = pool_mixer(xf, mix_norm[layer], pool_w[i].astype(BF16), pool_scale[i],
                               cross_norm[layer], seq=seq)
        xf, h, aff = cross(xf, h, layer=layer)
        xf = _moe(xf, h, aff, expert_w_gate, expert_w_up, expert_w_down, layer, final_norm,
                  batch=batch, seq=seq, final_norm=layer == depth - 1)
    return xf.reshape(batch, seq, d)
```

```python
import functools
import math

import jax
import jax.numpy as jnp
from jax import lax
from jax.experimental import pallas as pl
from jax.experimental.pallas import tpu as pltpu

F32 = jnp.float32
BF16 = jnp.bfloat16

GRID_W = 64
HEAD_DIM = 128
N_Q_HEADS = 12
N_KV_HEADS = 4
N_FOURIER_GROUPS = 4
FOURIER_GROUP = 128
ROPE_THETA = 10000.0
POOL_WINDOWS = (2, 4, 8, 16)
N_MEM_HEADS = 4
EC_CAPACITY_FACTOR = 2
NORM_EPS = 1e-6

LANES = 128
V7X_VMEM_BYTES = 64 * 1024 * 1024
VMEM_HEADROOM_BYTES = 6 * 1024 * 1024
COL_CHUNK = 512
LOG2_E = math.log2(math.e)


def _params(semantics, vmem_bytes):
    limit = min(int(vmem_bytes) + VMEM_HEADROOM_BYTES, V7X_VMEM_BYTES - VMEM_HEADROOM_BYTES)
    return pltpu.CompilerParams(dimension_semantics=semantics, vmem_limit_bytes=limit)


def _tile(n, pref):
    t = min(n, pref)
    while n % t:
        t //= 2
    return t


def _rms(x, gain):
    ms = jnp.mean(x * x, axis=-1, keepdims=True)
    return x * lax.rsqrt(ms + NORM_EPS) * gain


def _softmax_numerators(scores, scale):
    c = scale * LOG2_E
    return [jnp.exp2((s - jnp.max(s, axis=-1, keepdims=True)) * c) for s in scores]


def _norm_kernel(x_ref, g_ref, o_ref):
    o_ref[...] = _rms(x_ref[...], g_ref[...]).astype(o_ref.dtype)


def rms_norm_rows(x, gain, out_dtype):
    m, d = x.shape
    tm = _tile(m, 512)
    return pl.pallas_call(
        _norm_kernel,
        grid=(m // tm,),
        in_specs=[pl.BlockSpec((tm, d), lambda i: (i, 0)),
                  pl.BlockSpec((1, d), lambda i: (0, 0))],
        out_specs=pl.BlockSpec((tm, d), lambda i: (i, 0)),
        out_shape=jax.ShapeDtypeStruct((m, d), out_dtype),
        compiler_params=_params(("parallel",), 2 * tm * d * 8),
        name="rms_norm_rows",
    )(x, gain.reshape(1, d))


def _cast_kernel(w_ref, o_ref):
    o_ref[...] = w_ref[...].astype(o_ref.dtype)


def cast_weight(w, lead):
    k, n = w.shape[-2:]
    tk = _tile(k, 1024)
    squeezed = (None,) * len(lead)
    return pl.pallas_call(
        _cast_kernel,
        grid=(k // tk,),
        in_specs=[pl.BlockSpec(squeezed + (tk, n), lambda i: tuple(lead) + (i, 0))],
        out_specs=pl.BlockSpec((tk, n), lambda i: (i, 0)),
        out_shape=jax.ShapeDtypeStruct((k, n), BF16),
        compiler_params=_params(("parallel",), 2 * tk * n * 6),
        name="cast_weight",
    )(w)


def _mm_stream_kernel(a_ref, w_ref, o_ref):
    o_ref[...] = jnp.dot(a_ref[...], w_ref[...].astype(BF16),
                         preferred_element_type=F32).astype(o_ref.dtype)


def matmul_streamed(a, w, lead, out_dtype=BF16):
    m, k = a.shape
    n = w.shape[-1]
    tn = _tile(n, COL_CHUNK)
    squeezed = (None,) * len(lead)
    vmem = 2 * (m * k * 2 + k * tn * 4 + m * tn * 2) + k * tn * 2 + m * tn * 4
    return pl.pallas_call(
        _mm_stream_kernel,
        grid=(n // tn,),
        in_specs=[pl.BlockSpec((m, k), lambda j: (0, 0)),
                  pl.BlockSpec(squeezed + (k, tn), lambda j: tuple(lead) + (0, j))],
        out_specs=pl.BlockSpec((m, tn), lambda j: (0, j)),
        out_shape=jax.ShapeDtypeStruct((m, n), out_dtype),
        compiler_params=_params(("parallel",), vmem),
        name="matmul_streamed",
    )(a, w)


def _proj_in_kernel(a_ref, ng_ref, w_ref, g_ref, cos_ref, sa_ref, sb_ref, o_ref, *, n_rope_heads):
    n = o_ref.shape[1]
    tn = _tile(n, COL_CHUNK)
    a = _rms(a_ref[...], ng_ref[...]).astype(BF16)
    cos, sin_a, sin_b = cos_ref[...], sa_ref[...], sb_ref[...]
    chunks = [jnp.dot(a, w_ref[:, c * tn:(c + 1) * tn], preferred_element_type=F32)
              for c in range(n // tn)]
    for c, acc in enumerate(chunks):
        for h in range(tn // HEAD_DIM):
            head = c * (tn // HEAD_DIM) + h
            col = slice(head * HEAD_DIM, (head + 1) * HEAD_DIM)
            xh = acc[:, h * HEAD_DIM:(h + 1) * HEAD_DIM]
            if head < n_rope_heads:
                ms = jnp.mean(xh * xh, axis=-1, keepdims=True)
                y = xh * lax.rsqrt(ms + NORM_EPS) * g_ref[:, col]
                xh = (y * cos + pltpu.roll(y, HEAD_DIM - 1, axis=1) * sin_a
                      + pltpu.roll(y, 1, axis=1) * sin_b)
            o_ref[:, col] = xh.astype(o_ref.dtype)


def proj_in(x, norm_gain, w_in, gain_cols, cos_t, sin_a, sin_b, *, seq, rope_width):
    m, k = x.shape
    n = w_in.shape[1]
    tm = _tile(seq, 512)
    n_pos_blocks = seq // tm
    vmem = 2 * (tm * k * 4 + k * n * 2 + tm * n * 2 + 3 * tm * HEAD_DIM * 4) + 6 * tm * COL_CHUNK * 4 \
        + 2 * tm * k * 4
    table = pl.BlockSpec((tm, HEAD_DIM), lambda i: (i % n_pos_blocks, 0))
    return pl.pallas_call(
        functools.partial(_proj_in_kernel, n_rope_heads=rope_width // HEAD_DIM),
        grid=(m // tm,),
        in_specs=[pl.BlockSpec((tm, k), lambda i: (i, 0)),
                  pl.BlockSpec((1, k), lambda i: (0, 0)),
                  pl.BlockSpec((k, n), lambda i: (0, 0)),
                  pl.BlockSpec((1, n), lambda i: (0, 0)),
                  table, table, table],
        out_specs=pl.BlockSpec((tm, n), lambda i: (i, 0)),
        out_shape=jax.ShapeDtypeStruct((m, n), BF16),
        compiler_params=_params(("parallel",), vmem),
        name="proj_in",
    )(x, norm_gain.reshape(1, k), w_in, gain_cols, cos_t, sin_a, sin_b)


def _attn_kernel(q_ref, k_ref, v_ref, o_ref, *, group, scale):
    k = k_ref[...]
    v = v_ref[...]
    v_ones = jnp.concatenate([v, jnp.ones_like(v)], axis=1)
    heads = [slice(g * HEAD_DIM, (g + 1) * HEAD_DIM) for g in range(group)]
    scores = [lax.dot_general(q_ref[:, sl], k, (((1,), (1,)), ((), ())),
                              preferred_element_type=F32) for sl in heads]
    for sl, e in zip(heads, _softmax_numerators(scores, scale)):
        ol = jnp.dot(e.astype(BF16), v_ones, preferred_element_type=F32)
        o_ref[:, sl] = (ol[:, :HEAD_DIM] / ol[:, HEAD_DIM:HEAD_DIM + 1]).astype(o_ref.dtype)


def gqa_attention(proj, *, batch, seq, attn_width, kv_width):
    group = N_Q_HEADS // N_KV_HEADS
    gw = group * HEAD_DIM
    tq = _tile(seq, 1024)
    nq = seq // tq
    k_col0 = attn_width // HEAD_DIM
    v_col0 = (attn_width + kv_width) // HEAD_DIM
    vmem = 2 * (tq * gw * 2 * 2 + 2 * seq * HEAD_DIM * 2) + group * tq * seq * 6
    return pl.pallas_call(
        functools.partial(_attn_kernel, group=group, scale=HEAD_DIM ** -0.5),
        grid=(batch, N_KV_HEADS, nq),
        in_specs=[pl.BlockSpec((tq, gw), lambda b, h, i: (b * nq + i, h)),
                  pl.BlockSpec((seq, HEAD_DIM), lambda b, h, i: (b, k_col0 + h)),
                  pl.BlockSpec((seq, HEAD_DIM), lambda b, h, i: (b, v_col0 + h))],
        out_specs=pl.BlockSpec((tq, gw), lambda b, h, i: (b * nq + i, h)),
        out_shape=jax.ShapeDtypeStruct((batch * seq, attn_width), BF16),
        compiler_params=_params(("parallel", "parallel", "parallel"), vmem),
        name="gqa_attention",
    )(proj, proj, proj)


def _fourier_kernel(f_ref, cc_ref, sc_ref, cs_ref, ss_ref, wf_ref, o_ref, ab_ref, *, seq, norm):
    r = pl.program_id(1)
    ng = wf_ref.shape[0]
    cg = wf_ref.shape[1]

    @pl.when(r == 0)
    def _():
        for g in range(ng):
            fg = f_ref[:, g * cg:(g + 1) * cg]
            ab_ref[0:seq, g * cg:(g + 1) * cg] = jnp.dot(
                fg, cc_ref[...], preferred_element_type=F32).astype(BF16)
            ab_ref[seq:2 * seq, g * cg:(g + 1) * cg] = jnp.dot(
                fg, sc_ref[...], preferred_element_type=F32).astype(BF16)

    z = (jnp.dot(cs_ref[...], ab_ref[0:seq, :], preferred_element_type=F32)
         - jnp.dot(ss_ref[...], ab_ref[seq:2 * seq, :], preferred_element_type=F32)) * norm
    for g in range(ng):
        zg = z[:, g * cg:(g + 1) * cg].astype(BF16)
        o_ref[:, g * cg:(g + 1) * cg] = jnp.dot(
            zg, wf_ref[g].astype(BF16), preferred_element_type=F32).astype(o_ref.dtype)


def _dft_cos_sin(n):
    lo = _tile(n, 64)
    hi = n // lo
    j = jnp.arange(n, dtype=jnp.int32)

    def table(k):
        ang = ((k[:, None] * j[None, :]) % n).astype(F32) * (2.0 * math.pi / n)
        return jnp.cos(ang), jnp.sin(ang)

    ca, sa = table(jnp.arange(hi, dtype=jnp.int32) * lo)
    cb, sb = table(jnp.arange(lo, dtype=jnp.int32))
    cos = ca[:, None, :] * cb[None, :, :] - sa[:, None, :] * sb[None, :, :]
    sin = sa[:, None, :] * cb[None, :, :] + ca[:, None, :] * sb[None, :, :]
    return cos.reshape(n, n).astype(BF16), sin.reshape(n, n).astype(BF16)


def fourier_mixer(proj, w_fourier, *, batch, seq, col0):
    ng, cg, _ = w_fourier.shape
    fw = ng * cg
    tr = _tile(seq, 512)
    nr = seq // tr
    cc, sc = _dft_cos_sin(cg)
    cs, ss = _dft_cos_sin(seq)
    norm = 1.0 / math.sqrt(seq * cg)
    vmem = 2 * (seq * fw * 2 + 2 * cg * cg * 2 + 2 * tr * seq * 2 + ng * cg * cg * 4 + tr * fw * 2) \
        + 2 * seq * fw * 2 + 4 * tr * fw * 4
    return pl.pallas_call(
        functools.partial(_fourier_kernel, seq=seq, norm=norm),
        grid=(batch, nr),
        in_specs=[pl.BlockSpec((seq, fw), lambda b, r: (b, col0 // fw)),
                  pl.BlockSpec((cg, cg), lambda b, r: (0, 0)),
                  pl.BlockSpec((cg, cg), lambda b, r: (0, 0)),
                  pl.BlockSpec((tr, seq), lambda b, r: (r, 0)),
                  pl.BlockSpec((tr, seq), lambda b, r: (r, 0)),
                  pl.BlockSpec((ng, cg, cg), lambda b, r: (0, 0, 0))],
        out_specs=pl.BlockSpec((tr, fw), lambda b, r: (b * nr + r, 0)),
        out_shape=jax.ShapeDtypeStruct((batch * seq, fw), BF16),
        scratch_shapes=[pltpu.VMEM((2 * seq, fw), BF16)],
        compiler_params=_params(("parallel", "arbitrary"), vmem),
        name="fourier_mixer",
    )(proj, cc, sc, cs, ss, w_fourier)


def _router_affinity(hn, wr_ref, n_experts):
    hi = hn.astype(BF16)
    lo = (hn - hi.astype(F32)).astype(BF16)
    r_hi = jnp.dot(hi, wr_ref[...], preferred_element_type=F32)
    r_lo = jnp.dot(lo, wr_ref[...], preferred_element_type=F32)
    logits = r_hi[:, :LANES] + r_hi[:, LANES:] + r_lo[:, :LANES]
    lane = lax.broadcasted_iota(jnp.int32, logits.shape, 1)
    logits = jnp.where(lane < n_experts, logits, -jnp.inf)
    e = jnp.exp(logits - jnp.max(logits, axis=-1, keepdims=True))
    return e / jnp.sum(e, axis=-1, keepdims=True)


def _mm_res_norm_kernel(*refs, n_a, with_router, n_experts):
    a_refs = refs[:n_a]
    w_ref, x_ref, g_ref = refs[n_a:n_a + 3]
    pos = n_a + 3
    if with_router:
        wr_ref = refs[pos]
        pos += 1
    xo_ref, ho_ref = refs[pos:pos + 2]
    if with_router:
        aff_ref = refs[pos + 2]
    d = xo_ref.shape[1]
    tn = _tile(d, COL_CHUNK)
    for c in range(d // tn):
        sl = slice(c * tn, (c + 1) * tn)
        acc = x_ref[:, sl]
        row0 = 0
        for a_ref in a_refs:
            ka = a_ref.shape[1]
            acc = acc + jnp.dot(a_ref[...], w_ref[row0:row0 + ka, sl], preferred_element_type=F32)
            row0 += ka
        xo_ref[:, sl] = acc
    hn = _rms(xo_ref[...], g_ref[...])
    ho_ref[...] = hn.astype(ho_ref.dtype)
    if with_router:
        aff_ref[...] = _router_affinity(hn, wr_ref, n_experts)


def matmul_residual_norm(a_list, w, x_res, gain, *, router=None, tm_pref=512):
    m, d = x_res.shape
    k = w.shape[0]
    tm = _tile(m, tm_pref)
    in_specs = [pl.BlockSpec((tm, a.shape[1]), lambda i: (i, 0)) for a in a_list]
    in_specs += [pl.BlockSpec((k, d), lambda i: (0, 0)),
                 pl.BlockSpec((tm, d), lambda i: (i, 0)),
                 pl.BlockSpec((1, d), lambda i: (0, 0))]
    args = list(a_list) + [w, x_res, gain.reshape(1, d)]
    out_specs = [pl.BlockSpec((tm, d), lambda i: (i, 0)),
                 pl.BlockSpec((tm, d), lambda i: (i, 0))]
    out_shape = [jax.ShapeDtypeStruct((m, d), F32), jax.ShapeDtypeStruct((m, d), BF16)]
    n_experts = 0
    if router is not None:
        wr, n_experts = router
        in_specs.append(pl.BlockSpec((d, 2 * LANES), lambda i: (0, 0)))
        args.append(wr)
        out_specs.append(pl.BlockSpec((tm, LANES), lambda i: (i, 0)))
        out_shape.append(jax.ShapeDtypeStruct((m, LANES), F32))
    vmem = 2 * (tm * k * 2 + k * d * 2 + tm * d * 4 + tm * d * 4 + tm * d * 2) \
        + 3 * tm * d * 4 + 4 * d * LANES * 2
    return pl.pallas_call(
        functools.partial(_mm_res_norm_kernel, n_a=len(a_list),
                          with_router=router is not None, n_experts=n_experts),
        grid=(m // tm,),
        in_specs=in_specs,
        out_specs=out_specs,
        out_shape=out_shape,
        compiler_params=_params(("parallel",), vmem),
        name="matmul_residual_norm",
    )(*args)


def _cross_attn_kernel(h_ref, wq_ref, k_ref, v_ref, o_ref, *, n_heads, scale):
    dh = h_ref.shape[1] // n_heads
    h = h_ref[...]
    heads = [slice(i * dh, (i + 1) * dh) for i in range(n_heads)]
    qs = [jnp.dot(h, wq_ref[:, sl], preferred_element_type=F32).astype(BF16) for sl in heads]
    scores = [lax.dot_general(q, k_ref[:, sl], (((1,), (1,)), ((), ())),
                              preferred_element_type=F32) for q, sl in zip(qs, heads)]
    for sl, e in zip(heads, _softmax_numerators(scores, scale)):
        l = jnp.sum(e, axis=-1, keepdims=True)
        o = jnp.dot(e.astype(BF16), v_ref[:, sl], preferred_element_type=F32) / l
        o_ref[:, sl] = o.astype(o_ref.dtype)


def cross_attention(h, w_q, k, v, *, batch, seq, mem_len):
    d = h.shape[1]
    tq = _tile(seq, 1024)
    nq = seq // tq
    vmem = 2 * (2 * tq * d * 2 + d * d * 2 + 2 * mem_len * d * 2) + 6 * tq * mem_len * 4 + 3 * tq * d * 4
    return pl.pallas_call(
        functools.partial(_cross_attn_kernel, n_heads=N_MEM_HEADS,
                          scale=(d // N_MEM_HEADS) ** -0.5),
        grid=(batch, nq),
        in_specs=[pl.BlockSpec((tq, d), lambda b, i: (b * nq + i, 0)),
                  pl.BlockSpec((d, d), lambda b, i: (0, 0)),
                  pl.BlockSpec((mem_len, d), lambda b, i: (b, 0)),
                  pl.BlockSpec((mem_len, d), lambda b, i: (b, 0))],
        out_specs=pl.BlockSpec((tq, d), lambda b, i: (b * nq + i, 0)),
        out_shape=jax.ShapeDtypeStruct((batch * seq, d), BF16),
        compiler_params=_params(("parallel", "parallel"), vmem),
        name="cross_attention",
    )(h, w_q, k, v)


def _prefix_count(x, lane):
    n = x.shape[1]
    shift = 1
    while shift < n:
        x = x + jnp.where(lane >= shift, pltpu.roll(x, shift, axis=1), 0)
        shift *= 2
    return x


def _route_kernel(aff_ref, prow_ref, arow_ref, pcol_ref, *, n_experts, capacity):
    batch, e_pad, seq = prow_ref.shape
    a_t = jnp.concatenate([aff_ref[b * seq:(b + 1) * seq, :].T[:e_pad] for b in range(batch)],
                          axis=0)
    for b in range(batch):
        arow_ref[b] = a_t[b * e_pad:(b + 1) * e_pad]

    def body(i, thr_bits):
        cand = thr_bits | jnp.left_shift(jnp.int32(1), 30 - i)
        cnt = jnp.sum((a_t >= pltpu.bitcast(cand, F32)).astype(F32), axis=1, keepdims=True)
        return jnp.where(cnt >= capacity, cand, thr_bits)

    thr = pltpu.bitcast(lax.fori_loop(0, 31, body, jnp.zeros((batch * e_pad, 1), jnp.int32)), F32)
    lane = lax.broadcasted_iota(jnp.int32, a_t.shape, 1)
    gt = a_t > thr
    eq = a_t == thr
    need = capacity - jnp.sum(gt.astype(F32), axis=1, keepdims=True).astype(jnp.int32)
    sel = gt | (eq & (_prefix_count(eq.astype(jnp.int32), lane) <= need))
    pos = jnp.where(sel, _prefix_count(sel.astype(jnp.int32), lane) - 1, -1)
    row = lax.broadcasted_iota(jnp.int32, a_t.shape, 0)
    pos = jnp.where(row % e_pad < n_experts, pos, -1)
    unused = jnp.full((LANES - e_pad, seq), -1.0, F32)
    for b in range(batch):
        pos_b = pos[b * e_pad:(b + 1) * e_pad]
        prow_ref[b] = pos_b
        pcol_ref[b * seq:(b + 1) * seq, :] = jnp.concatenate([pos_b.astype(F32), unused], axis=0).T


def route(aff, *, batch, seq, n_experts, capacity):
    e_pad = max(8, n_experts)
    return pl.pallas_call(
        functools.partial(_route_kernel, n_experts=n_experts, capacity=capacity),
        grid=(1,),
        in_specs=[pl.BlockSpec((batch * seq, LANES), lambda i: (0, 0))],
        out_specs=[pl.BlockSpec((batch, e_pad, seq), lambda i: (0, 0, 0)),
                   pl.BlockSpec((batch, e_pad, seq), lambda i: (0, 0, 0)),
                   pl.BlockSpec((batch * seq, LANES), lambda i: (0, 0))],
        out_shape=[jax.ShapeDtypeStruct((batch, e_pad, seq), jnp.int32),
                   jax.ShapeDtypeStruct((batch, e_pad, seq), F32),
                   jax.ShapeDtypeStruct((batch * seq, LANES), F32)],
        compiler_params=_params(("arbitrary",), 8 * batch * seq * LANES * 4),
        name="route",
    )(aff)


GATHER_TILES = 4
COMBINE_TILES = 4


def _window_size(capacity, n_tiles):
    return capacity // n_tiles + 2 * _window_margin(capacity)


def _window_margin(capacity):
    return capacity // 8


def _window_start(tile, capacity, n_tiles, clip=jnp.clip):
    share = capacity // n_tiles
    unit = _window_margin(capacity)
    window = _window_size(capacity, n_tiles)
    assert share % unit == 0 and (capacity - window) % unit == 0
    return clip(tile * (share // unit) - 1, 0, (capacity - window) // unit) * unit


def _static_clip(v, lo, hi):
    return max(lo, min(v, hi))


def _slots_inside_windows(prow, *, n_experts, capacity, n_tiles):
    batch, _, seq = prow.shape
    pos = prow[:, :n_experts, :].reshape(batch, n_experts, n_tiles, seq // n_tiles)
    first = _window_start(jnp.arange(n_tiles, dtype=jnp.int32), capacity, n_tiles)[None, None, :]
    lowest = jnp.min(jnp.where(pos >= 0, pos, capacity), axis=-1)
    highest = jnp.max(pos, axis=-1)
    return (lowest >= first) & (highest < first + _window_size(capacity, n_tiles))


def _gather_kernel(ok_ref, h_ref, prow_ref, o_ref, *, experts_per_step, n_tiles):
    b = pl.program_id(0)
    g = pl.program_id(1)
    cap = o_ref.shape[1]
    seq = h_ref.shape[0]
    tt = seq // n_tiles
    window = _window_size(cap, n_tiles)
    starts = [_window_start(t, cap, n_tiles, _static_clip) for t in range(n_tiles)]
    unit = _window_margin(cap)
    for j in range(experts_per_step):
        expert = g * experts_per_step + j
        prow = prow_ref[0, pl.ds(expert, 1), :]
        in_windows = ok_ref[b, expert] != 0

        @pl.when(in_windows)
        def _(j=j, prow=prow):
            slot = lax.broadcasted_iota(jnp.int32, (window, tt), 0)
            parts = [jnp.dot((slot + starts[t] == prow[:, t * tt:(t + 1) * tt]).astype(BF16),
                             h_ref[t * tt:(t + 1) * tt, :], preferred_element_type=F32)
                     for t in range(n_tiles)]
            for r0 in range(0, cap, unit):
                covering = [parts[t][r0 - starts[t]:r0 - starts[t] + unit]
                            for t in range(n_tiles) if starts[t] <= r0 < starts[t] + window]
                o_ref[j, r0:r0 + unit, :] = functools.reduce(jnp.add, covering).astype(o_ref.dtype)

        @pl.when(jnp.logical_not(in_windows))
        def _(j=j, prow=prow):
            slot = lax.broadcasted_iota(jnp.int32, (cap, seq), 0)
            o_ref[j] = jnp.dot((slot == prow).astype(BF16), h_ref[...],
                               preferred_element_type=F32).astype(o_ref.dtype)


def gather_tokens(h, prow, *, batch, seq, n_experts, capacity):
    d = h.shape[1]
    e_pad = prow.shape[1]
    ng = _tile(n_experts, 4)
    ok = jnp.all(_slots_inside_windows(prow, n_experts=n_experts, capacity=capacity,
                                       n_tiles=GATHER_TILES), axis=-1).astype(jnp.int32)
    vmem = 2 * (seq * d * 2 + e_pad * seq * 4 + ng * capacity * d * 2) + 2 * capacity * seq * 8 \
        + 3 * capacity * d * 4
    return pl.pallas_call(
        functools.partial(_gather_kernel, experts_per_step=ng, n_tiles=GATHER_TILES),
        grid_spec=pltpu.PrefetchScalarGridSpec(
            num_scalar_prefetch=1,
            grid=(batch, n_experts // ng),
            in_specs=[pl.BlockSpec((seq, d), lambda b, g, ok_ref: (b, 0)),
                      pl.BlockSpec((1, e_pad, seq), lambda b, g, ok_ref: (b, 0, 0))],
            out_specs=pl.BlockSpec((ng, None, capacity, d), lambda b, g, ok_ref: (g, b, 0, 0))),
        out_shape=jax.ShapeDtypeStruct((n_experts, batch, capacity, d), BF16),
        compiler_params=_params(("parallel", "arbitrary"), vmem),
        name="gather_tokens",
    )(ok, h, prow)


def _ffn_kernel(x_ref, wg_ref, wu_ref, wd_ref, prow_ref, arow_ref, o_ref, acc_ref, gate_ref,
                *, capacity, gate_steps):
    e = pl.program_id(0)
    f = pl.program_id(1)
    nf = pl.num_programs(1)
    rows = x_ref.shape[0]
    n_seq = rows // capacity

    def chunk_product():
        x = x_ref[...]
        a = jnp.dot(x, wg_ref[...].astype(BF16), preferred_element_type=F32)
        u = jnp.dot(x, wu_ref[...].astype(BF16), preferred_element_type=F32)
        hidden = (a / (1.0 + jnp.exp(-a)) * u).astype(BF16)
        return jnp.dot(hidden, wd_ref[...].astype(BF16), preferred_element_type=F32)

    def gate_share():
        seq = prow_ref.shape[2]
        slot = lax.broadcasted_iota(jnp.int32, (capacity, seq), 0)
        per_step = -(-n_seq // gate_steps)
        for j in range(per_step):
            b = jnp.minimum(f * per_step + j, n_seq - 1)
            mine = slot == prow_ref[b, pl.ds(e, 1), :]
            gate_ref[pl.ds(pl.multiple_of(b * capacity, capacity), capacity), :] = jnp.sum(
                jnp.where(mine, arow_ref[b, pl.ds(e, 1), :], 0.0), axis=1, keepdims=True)

    @pl.when(f == 0)
    def _():
        gate_share()
        acc_ref[...] = chunk_product()

    @pl.when((f > 0) & (f < nf - 1))
    def _():
        gate_share()
        acc_ref[...] += chunk_product()

    @pl.when(f == nf - 1)
    def _():
        gate_share()
        o_ref[...] = ((acc_ref[...] + chunk_product()) * gate_ref[...]).astype(o_ref.dtype)


def expert_ffn(xs, w_gate, w_up, w_down, prow, arow, layer, *, rows_per_expert, capacity):
    _, n_experts, d, ff = w_gate.shape
    tf = _tile(ff // 2, 256)
    r = rows_per_expert
    batch, e_pad, seq = prow.shape
    vmem = 2 * (r * d * 2 + 3 * d * tf * 4 + r * d * 2 + 2 * batch * e_pad * seq * 4) + r * d * 4 \
        + r * LANES * 4 + 3 * d * tf * 2 + 4 * r * tf * 4 + r * d * 4 + 2 * capacity * seq * 4
    return pl.pallas_call(
        functools.partial(_ffn_kernel, capacity=capacity, gate_steps=ff // tf),
        grid=(n_experts, ff // tf),
        in_specs=[pl.BlockSpec((r, d), lambda e, f: (e, 0)),
                  pl.BlockSpec((None, None, d, tf), lambda e, f: (layer, e, 0, f)),
                  pl.BlockSpec((None, None, d, tf), lambda e, f: (layer, e, 0, f)),
                  pl.BlockSpec((None, None, tf, d), lambda e, f: (layer, e, f, 0)),
                  pl.BlockSpec((batch, e_pad, seq), lambda e, f: (0, 0, 0)),
                  pl.BlockSpec((batch, e_pad, seq), lambda e, f: (0, 0, 0))],
        out_specs=pl.BlockSpec((r, d), lambda e, f: (e, 0)),
        out_shape=jax.ShapeDtypeStruct((n_experts * r, d), BF16),
        scratch_shapes=[pltpu.VMEM((r, d), F32), pltpu.VMEM((r, 1), F32)],
        compiler_params=_params(("parallel", "arbitrary"), vmem),
        name="expert_ffn",
    )(xs, w_gate, w_up, w_down, prow, arow)


def _combine_kernel(ok_ref, x_ref, ywin_ref, yall_ref, pcol_ref, g_ref, out_ref, *, final_norm):
    b = pl.program_id(0)
    t = pl.program_id(1)
    g = pl.program_id(2)
    tt, d = x_ref.shape
    n_group, _, window, _ = ywin_ref.shape
    cap = yall_ref.shape[1]

    @pl.when(g == 0)
    def _():
        out_ref[...] = x_ref[...]

    lane = lax.broadcasted_iota(jnp.int32, (tt, LANES), 1)
    pcol = pcol_ref[...]

    def scatter(y_rows, n_slots, first_slot):
        slot = (lax.broadcasted_iota(jnp.int32, (tt, n_slots), 1) + first_slot).astype(F32)
        hots = []
        for j in range(n_group):
            slot_of_token = jnp.sum(jnp.where(lane == g * n_group + j, pcol, 0.0),
                                    axis=1, keepdims=True)
            hots.append((slot == slot_of_token).astype(BF16))
        out_ref[...] += jnp.dot(jnp.concatenate(hots, axis=1), y_rows, preferred_element_type=F32)

    in_window = ok_ref[b, t, g] != 0

    @pl.when(in_window)
    def _():
        scatter(ywin_ref[...].reshape(n_group * window, d), window,
                _window_start(t, cap, COMBINE_TILES))

    @pl.when(jnp.logical_not(in_window))
    def _():
        scatter(yall_ref[...].reshape(n_group * cap, d), cap, 0)

    if final_norm:
        @pl.when(g == pl.num_programs(2) - 1)
        def _():
            out_ref[...] = _rms(out_ref[...], g_ref[...])


def combine(x_res, y, prow, pcol, gain, *, batch, seq, n_experts, capacity, final_norm):
    d = x_res.shape[1]
    nt = COMBINE_TILES
    tt = seq // nt
    window = _window_size(capacity, nt)
    ng = _tile(n_experts, 8)
    n_groups = n_experts // ng
    y4 = y.reshape(n_experts, batch, capacity, d)
    inside = _slots_inside_windows(prow, n_experts=n_experts, capacity=capacity, n_tiles=nt)
    ok = jnp.all(inside.reshape(batch, n_groups, ng, nt), axis=2).transpose(0, 2, 1).astype(jnp.int32)
    vmem = 2 * (tt * d * 4 + ng * window * d * 2 + ng * capacity * d * 2 + tt * LANES * 4 + tt * d * 4) \
        + 3 * tt * d * 4 + 2 * tt * capacity * 4

    def full_block(b, t, g, ok_ref):
        need = ok_ref[b, t, g] == 0
        return (jnp.where(need, g, 0), jnp.where(need, b, 0), 0, 0)

    return pl.pallas_call(
        functools.partial(_combine_kernel, final_norm=final_norm),
        grid_spec=pltpu.PrefetchScalarGridSpec(
            num_scalar_prefetch=1,
            grid=(batch, nt, n_groups),
            in_specs=[pl.BlockSpec((tt, d), lambda b, t, g, ok_ref: (b * nt + t, 0)),
                      pl.BlockSpec((pl.Element(ng), pl.Element(1), pl.Element(window), pl.Element(d)),
                                   lambda b, t, g, ok_ref: (g * ng, b, _window_start(t, capacity, nt), 0)),
                      pl.BlockSpec((ng, None, capacity, d), full_block),
                      pl.BlockSpec((tt, LANES), lambda b, t, g, ok_ref: (b * nt + t, 0)),
                      pl.BlockSpec((1, d), lambda b, t, g, ok_ref: (0, 0))],
            out_specs=pl.BlockSpec((tt, d), lambda b, t, g, ok_ref: (b * nt + t, 0))),
        out_shape=jax.ShapeDtypeStruct((batch * seq, d), F32),
        compiler_params=_params(("parallel", "parallel", "arbitrary"), vmem),
        name="combine",
    )(ok, x_res, y4, y4, pcol, gain.reshape(1, d))


def _pool_kernel(x_ref, prev_ref, next_ref, ng_ref, w_ref, s_ref, g_ref, xo_ref, ho_ref, pad_ref,
                 *, seq, windows):
    i = pl.program_id(0)
    tm, d = x_ref.shape
    halo = prev_ref.shape[0]
    pg = w_ref.shape[1]
    tiles_per_seq = seq // tm
    tile_in_seq = i % tiles_per_seq
    norm_gain = ng_ref[...]
    h = _rms(x_ref[...], norm_gain)
    pad_ref[0:halo, :] = jnp.where(tile_in_seq == 0, 0.0, _rms(prev_ref[...], norm_gain))
    pad_ref[halo:halo + tm, :] = h
    pad_ref[halo + tm:2 * halo + tm, :] = jnp.where(tile_in_seq == tiles_per_seq - 1, 0.0,
                                                    _rms(next_ref[...], norm_gain))
    t = tile_in_seq * tm + lax.broadcasted_iota(jnp.int32, (tm, 1), 0)
    for gi, w in enumerate(windows):
        cols = slice(gi * pg, (gi + 1) * pg)
        acc = pad_ref[:, cols]
        span = 1
        while span < w:
            acc = acc + pltpu.roll(acc, acc.shape[0] - span, axis=0)
            span *= 2
        total = acc[halo - w // 2:halo - w // 2 + tm, :]
        count = (jnp.minimum(t + (w - w // 2), seq) - jnp.maximum(t - w // 2, 0)).astype(F32)
        pooled = (total / count - h[:, cols]).astype(BF16)
        mixed = jnp.dot(pooled, w_ref[gi], preferred_element_type=F32)
        xo_ref[:, cols] = mixed * s_ref[:, cols] + x_ref[:, cols]
    ho_ref[...] = _rms(xo_ref[...], g_ref[...]).astype(ho_ref.dtype)


def pool_mixer(x, norm_gain, w_pool, pool_scale, next_gain, *, seq):
    m, d = x.shape
    ng, pg, _ = w_pool.shape
    halo = 8
    assert all(w & (w - 1) == 0 and w // 2 <= halo for w in POOL_WINDOWS) and ng == len(POOL_WINDOWS)
    tm = _tile(seq, 512)
    per = tm // halo
    n_halo_blocks = m // halo
    vmem = 2 * (2 * tm * d * 4 + 2 * halo * d * 4 + ng * pg * pg * 2 + tm * d * 2) \
        + (tm + 2 * halo) * d * 4 + 6 * tm * d * 4
    return pl.pallas_call(
        functools.partial(_pool_kernel, seq=seq, windows=POOL_WINDOWS),
        grid=(m // tm,),
        in_specs=[pl.BlockSpec((tm, d), lambda i: (i, 0)),
                  pl.BlockSpec((halo, d), lambda i: (jnp.maximum(i * per - 1, 0), 0)),
                  pl.BlockSpec((halo, d), lambda i: (jnp.minimum((i + 1) * per, n_halo_blocks - 1), 0)),
                  pl.BlockSpec((1, d), lambda i: (0, 0)),
                  pl.BlockSpec((ng, pg, pg), lambda i: (0, 0, 0)),
                  pl.BlockSpec((1, d), lambda i: (0, 0)),
                  pl.BlockSpec((1, d), lambda i: (0, 0))],
        out_specs=[pl.BlockSpec((tm, d), lambda i: (i, 0)),
                   pl.BlockSpec((tm, d), lambda i: (i, 0))],
        out_shape=[jax.ShapeDtypeStruct((m, d), F32), jax.ShapeDtypeStruct((m, d), BF16)],
        scratch_shapes=[pltpu.VMEM((tm + 2 * halo, d), F32)],
        compiler_params=_params(("parallel",), vmem),
        name="pool_mixer",
    )(x, x, x, norm_gain.reshape(1, d), w_pool, pool_scale.reshape(1, d), next_gain.reshape(1, d))


def _rope_tables(seq):
    half = HEAD_DIM // 2
    rows = seq // GRID_W
    row_idx = jnp.repeat(jnp.arange(rows), GRID_W).astype(F32)
    col_idx = jnp.tile(jnp.arange(GRID_W), rows).astype(F32)
    inv_freq = 1.0 / (ROPE_THETA ** (jnp.arange(0, half, 2, dtype=F32) / half))
    ang = jnp.concatenate([row_idx[:, None] * inv_freq[None, :],
                           col_idx[:, None] * inv_freq[None, :]], axis=-1)
    cos, sin = jnp.cos(ang), jnp.sin(ang)
    zero = jnp.zeros_like(sin)
    cos_full = jnp.stack([cos, cos], axis=-1).reshape(seq, HEAD_DIM)
    sin_even = jnp.stack([-sin, zero], axis=-1).reshape(seq, HEAD_DIM)
    sin_odd = jnp.stack([zero, sin], axis=-1).reshape(seq, HEAD_DIM)
    return cos_full, sin_even, sin_odd


def _split_router(w_router):
    d, n_experts = w_router.shape
    w = jnp.pad(w_router, ((0, 0), (0, LANES - n_experts)))
    hi = w.astype(BF16)
    lo = (w - hi.astype(F32)).astype(BF16)
    return jnp.concatenate([hi, lo], axis=1), n_experts


def _moe(x_res, h, aff, w_gate, w_up, w_down, layer, gain, *, batch, seq, final_norm):
    n_experts = w_gate.shape[1]
    capacity = EC_CAPACITY_FACTOR * seq // n_experts
    d = h.shape[1]
    prow, arow, pcol = route(aff, batch=batch, seq=seq, n_experts=n_experts, capacity=capacity)
    xs = gather_tokens(h, prow, batch=batch, seq=seq, n_experts=n_experts, capacity=capacity)
    y = expert_ffn(xs.reshape(n_experts * batch * capacity, d), w_gate, w_up, w_down, prow, arow,
                   layer, rows_per_expert=batch * capacity, capacity=capacity)
    return combine(x_res, y, prow, pcol, gain, batch=batch, seq=seq,
                   n_experts=n_experts, capacity=capacity, final_norm=final_norm)


def _cross_block(x, h, mem2d, layer, cross_w_q, cross_w_k, cross_w_v, cross_w_o, mem_norm,
                 ffn_norm, router_w, *, batch, seq, mem_len):
    mem_n = rms_norm_rows(mem2d, mem_norm[layer], BF16)
    k = matmul_streamed(mem_n, cross_w_k, (layer,))
    v = matmul_streamed(mem_n, cross_w_v, (layer,))
    o = cross_attention(h, cast_weight(cross_w_q, (layer,)), k, v,
                        batch=batch, seq=seq, mem_len=mem_len)
    return matmul_residual_norm([o], cast_weight(cross_w_o, (layer,)), x, ffn_norm[layer],
                                router=_split_router(router_w[layer]))


def kernel(x, mem, mix_norm, attn_w_in, q_gain, k_gain, fourier_w, attn_w_out, pool_w, pool_scale,
           cross_norm, mem_norm, cross_w_q, cross_w_k, cross_w_v, cross_w_o, ffn_norm, router_w,
           expert_w_gate, expert_w_up, expert_w_down, final_norm):
    batch, seq, d = x.shape
    mem_len = mem.shape[1]
    depth = mix_norm.shape[0]
    attn_width = N_Q_HEADS * HEAD_DIM
    kv_width = N_KV_HEADS * HEAD_DIM
    fourier_width = N_FOURIER_GROUPS * FOURIER_GROUP
    rope_width = attn_width + kv_width
    cos_t, sin_a, sin_b = _rope_tables(seq)

    xf = x.reshape(batch * seq, d)
    mem2d = mem.reshape(batch * mem_len, d)
    cross = functools.partial(_cross_block, mem2d=mem2d, cross_w_q=cross_w_q, cross_w_k=cross_w_k,
                              cross_w_v=cross_w_v, cross_w_o=cross_w_o, mem_norm=mem_norm,
                              ffn_norm=ffn_norm, router_w=router_w,
                              batch=batch, seq=seq, mem_len=mem_len)
    for layer in range(depth):
        i = layer // 2
        if layer % 2 == 0:
            gain_cols = jnp.concatenate([jnp.tile(q_gain[i], N_Q_HEADS), jnp.tile(k_gain[i], N_KV_HEADS),
                                         jnp.ones((kv_width + fourier_width,), F32)]).reshape(1, -1)
            proj = proj_in(xf, mix_norm[layer], cast_weight(attn_w_in, (i,)), gain_cols,
                           cos_t, sin_a, sin_b, seq=seq, rope_width=rope_width)
            o_attn = gqa_attention(proj, batch=batch, seq=seq, attn_width=attn_width, kv_width=kv_width)
            o_four = fourier_mixer(proj, fourier_w[i], batch=batch, seq=seq,
                                   col0=attn_width + 2 * kv_width)
            xf, h = matmul_residual_norm([o_attn, o_four], cast_weight(attn_w_out, (i,)), xf,
                                         cross_norm[layer])
        else:
            xf, h = pool_mixer(xf, mix_norm[layer], pool_w[i].astype(BF16), pool_scale[i],
                               cross_norm[layer], seq=seq)
        xf, h, aff = cross(xf, h, layer=layer)
        xf = _moe(xf, h, aff, expert_w_gate, expert_w_up, expert_w_down, layer, final_norm,
                  batch=batch, seq=seq, final_norm=layer == depth - 1)
    return xf.reshape(batch, seq, d)
```

```python
import functools
import math

import jax
import jax.numpy as jnp
from jax import lax
from jax.experimental import pallas as pl
from jax.experimental.pallas import tpu as pltpu

F32 = jnp.float32
BF16 = jnp.bfloat16

GRID_W = 64
HEAD_DIM = 128
N_Q_HEADS = 12
N_KV_HEADS = 4
N_FOURIER_GROUPS = 4
FOURIER_GROUP = 128
ROPE_THETA = 10000.0
POOL_WINDOWS = (2, 4, 8, 16)
N_MEM_HEADS = 4
EC_CAPACITY_FACTOR = 2
NORM_EPS = 1e-6

LANES = 128
V7X_VMEM_BYTES = 64 * 1024 * 1024
VMEM_HEADROOM_BYTES = 6 * 1024 * 1024
COL_CHUNK = 512
LOG2_E = math.log2(math.e)


def _params(semantics, vmem_bytes):
    limit = min(int(vmem_bytes) + VMEM_HEADROOM_BYTES, V7X_VMEM_BYTES - VMEM_HEADROOM_BYTES)
    return pltpu.CompilerParams(dimension_semantics=semantics, vmem_limit_bytes=limit)


def _tile(n, pref):
    t = min(n, pref)
    while n % t:
        t //= 2
    return t


def _rms(x, gain):
    ms = jnp.mean(x * x, axis=-1, keepdims=True)
    return x * lax.rsqrt(ms + NORM_EPS) * gain


def _softmax_numerators(scores, scale):
    c = scale * LOG2_E
    return [jnp.exp2((s - jnp.max(s, axis=-1, keepdims=True)) * c) for s in scores]


def _norm_kernel(x_ref, g_ref, o_ref):
    o_ref[...] = _rms(x_ref[...], g_ref[...]).astype(o_ref.dtype)


def rms_norm_rows(x, gain, out_dtype):
    m, d = x.shape
    tm = _tile(m, 512)
    return pl.pallas_call(
        _norm_kernel,
        grid=(m // tm,),
        in_specs=[pl.BlockSpec((tm, d), lambda i: (i, 0)),
                  pl.BlockSpec((1, d), lambda i: (0, 0))],
        out_specs=pl.BlockSpec((tm, d), lambda i: (i, 0)),
        out_shape=jax.ShapeDtypeStruct((m, d), out_dtype),
        compiler_params=_params(("parallel",), 2 * tm * d * 8),
        name="rms_norm_rows",
    )(x, gain.reshape(1, d))


def _cast_kernel(w_ref, o_ref):
    o_ref[...] = w_ref[...].astype(o_ref.dtype)


def cast_weight(w, lead):
    k, n = w.shape[-2:]
    tk = _tile(k, 1024)
    squeezed = (None,) * len(lead)
    return pl.pallas_call(
        _cast_kernel,
        grid=(k // tk,),
        in_specs=[pl.BlockSpec(squeezed + (tk, n), lambda i: tuple(lead) + (i, 0))],
        out_specs=pl.BlockSpec((tk, n), lambda i: (i, 0)),
        out_shape=jax.ShapeDtypeStruct((k, n), BF16),
        compiler_params=_params(("parallel",), 2 * tk * n * 6),
        name="cast_weight",
    )(w)


def _mm_stream_kernel(a_ref, w_ref, o_ref):
    o_ref[...] = jnp.dot(a_ref[...], w_ref[...].astype(BF16),
                         preferred_element_type=F32).astype(o_ref.dtype)


def matmul_streamed(a, w, lead, out_dtype=BF16):
    m, k = a.shape
    n = w.shape[-1]
    tn = _tile(n, COL_CHUNK)
    squeezed = (None,) * len(lead)
    vmem = 2 * (m * k * 2 + k * tn * 4 + m * tn * 2) + k * tn * 2 + m * tn * 4
    return pl.pallas_call(
        _mm_stream_kernel,
        grid=(n // tn,),
        in_specs=[pl.BlockSpec((m, k), lambda j: (0, 0)),
                  pl.BlockSpec(squeezed + (k, tn), lambda j: tuple(lead) + (0, j))],
        out_specs=pl.BlockSpec((m, tn), lambda j: (0, j)),
        out_shape=jax.ShapeDtypeStruct((m, n), out_dtype),
        compiler_params=_params(("parallel",), vmem),
        name="matmul_streamed",
    )(a, w)


def _proj_in_kernel(a_ref, ng_ref, w_ref, g_ref, cos_ref, sa_ref, sb_ref, o_ref, *, n_rope_heads):
    n = o_ref.shape[1]
    tn = _tile(n, COL_CHUNK)
    a = _rms(a_ref[...], ng_ref[...]).astype(BF16)
    cos, sin_a, sin_b = cos_ref[...], sa_ref[...], sb_ref[...]
    chunks = [jnp.dot(a, w_ref[:, c * tn:(c + 1) * tn], preferred_element_type=F32)
              for c in range(n // tn)]
    for c, acc in enumerate(chunks):
        for h in range(tn // HEAD_DIM):
            head = c * (tn // HEAD_DIM) + h
            col = slice(head * HEAD_DIM, (head + 1) * HEAD_DIM)
            xh = acc[:, h * HEAD_DIM:(h + 1) * HEAD_DIM]
            if head < n_rope_heads:
                ms = jnp.mean(xh * xh, axis=-1, keepdims=True)
                y = xh * lax.rsqrt(ms + NORM_EPS) * g_ref[:, col]
                xh = (y * cos + pltpu.roll(y, HEAD_DIM - 1, axis=1) * sin_a
                      + pltpu.roll(y, 1, axis=1) * sin_b)
            o_ref[:, col] = xh.astype(o_ref.dtype)


def proj_in(x, norm_gain, w_in, gain_cols, cos_t, sin_a, sin_b, *, seq, rope_width):
    m, k = x.shape
    n = w_in.shape[1]
    tm = _tile(seq, 512)
    n_pos_blocks = seq // tm
    vmem = 2 * (tm * k * 4 + k * n * 2 + tm * n * 2 + 3 * tm * HEAD_DIM * 4) + 6 * tm * COL_CHUNK * 4 \
        + 2 * tm * k * 4
    table = pl.BlockSpec((tm, HEAD_DIM), lambda i: (i % n_pos_blocks, 0))
    return pl.pallas_call(
        functools.partial(_proj_in_kernel, n_rope_heads=rope_width // HEAD_DIM),
        grid=(m // tm,),
        in_specs=[pl.BlockSpec((tm, k), lambda i: (i, 0)),
                  pl.BlockSpec((1, k), lambda i: (0, 0)),
                  pl.BlockSpec((k, n), lambda i: (0, 0)),
                  pl.BlockSpec((1, n), lambda i: (0, 0)),
                  table, table, table],
        out_specs=pl.BlockSpec((tm, n), lambda i: (i, 0)),
        out_shape=jax.ShapeDtypeStruct((m, n), BF16),
        compiler_params=_params(("parallel",), vmem),
        name="proj_in",
    )(x, norm_gain.reshape(1, k), w_in, gain_cols, cos_t, sin_a, sin_b)


def _attn_kernel(q_ref, k_ref, v_ref, o_ref, *, group, scale):
    k = k_ref[...]
    v = v_ref[...]
    v_ones = jnp.concatenate([v, jnp.ones_like(v)], axis=1)
    heads = [slice(g * HEAD_DIM, (g + 1) * HEAD_DIM) for g in range(group)]
    scores = [lax.dot_general(q_ref[:, sl], k, (((1,), (1,)), ((), ())),
                              preferred_element_type=F32) for sl in heads]
    for sl, e in zip(heads, _softmax_numerators(scores, scale)):
        ol = jnp.dot(e.astype(BF16), v_ones, preferred_element_type=F32)
        o_ref[:, sl] = (ol[:, :HEAD_DIM] / ol[:, HEAD_DIM:HEAD_DIM + 1]).astype(o_ref.dtype)


def gqa_attention(proj, *, batch, seq, attn_width, kv_width):
    group = N_Q_HEADS // N_KV_HEADS
    gw = group * HEAD_DIM
    tq = _tile(seq, 1024)
    nq = seq // tq
    k_col0 = attn_width // HEAD_DIM
    v_col0 = (attn_width + kv_width) // HEAD_DIM
    vmem = 2 * (tq * gw * 2 * 2 + 2 * seq * HEAD_DIM * 2) + group * tq * seq * 6
    return pl.pallas_call(
        functools.partial(_attn_kernel, group=group, scale=HEAD_DIM ** -0.5),
        grid=(batch, N_KV_HEADS, nq),
        in_specs=[pl.BlockSpec((tq, gw), lambda b, h, i: (b * nq + i, h)),
                  pl.BlockSpec((seq, HEAD_DIM), lambda b, h, i: (b, k_col0 + h)),
                  pl.BlockSpec((seq, HEAD_DIM), lambda b, h, i: (b, v_col0 + h))],
        out_specs=pl.BlockSpec((tq, gw), lambda b, h, i: (b * nq + i, h)),
        out_shape=jax.ShapeDtypeStruct((batch * seq, attn_width), BF16),
        compiler_params=_params(("parallel", "parallel", "parallel"), vmem),
        name="gqa_attention",
    )(proj, proj, proj)


def _fourier_kernel(f_ref, cc_ref, sc_ref, cs_ref, ss_ref, wf_ref, o_ref, ab_ref, *, seq, norm):
    r = pl.program_id(1)
    ng = wf_ref.shape[0]
    cg = wf_ref.shape[1]

    @pl.when(r == 0)
    def _():
        for g in range(ng):
            fg = f_ref[:, g * cg:(g + 1) * cg]
            ab_ref[0:seq, g * cg:(g + 1) * cg] = jnp.dot(
                fg, cc_ref[...], preferred_element_type=F32).astype(BF16)
            ab_ref[seq:2 * seq, g * cg:(g + 1) * cg] = jnp.dot(
                fg, sc_ref[...], preferred_element_type=F32).astype(BF16)

    z = (jnp.dot(cs_ref[...], ab_ref[0:seq, :], preferred_element_type=F32)
         - jnp.dot(ss_ref[...], ab_ref[seq:2 * seq, :], preferred_element_type=F32)) * norm
    for g in range(ng):
        zg = z[:, g * cg:(g + 1) * cg].astype(BF16)
        o_ref[:, g * cg:(g + 1) * cg] = jnp.dot(
            zg, wf_ref[g].astype(BF16), preferred_element_type=F32).astype(o_ref.dtype)


def _dft_cos_sin(n):
    lo = _tile(n, 64)
    hi = n // lo
    j = jnp.arange(n, dtype=jnp.int32)

    def table(k):
        ang = ((k[:, None] * j[None, :]) % n).astype(F32) * (2.0 * math.pi / n)
        return jnp.cos(ang), jnp.sin(ang)

    ca, sa = table(jnp.arange(hi, dtype=jnp.int32) * lo)
    cb, sb = table(jnp.arange(lo, dtype=jnp.int32))
    cos = ca[:, None, :] * cb[None, :, :] - sa[:, None, :] * sb[None, :, :]
    sin = sa[:, None, :] * cb[None, :, :] + ca[:, None, :] * sb[None, :, :]
    return cos.reshape(n, n).astype(BF16), sin.reshape(n, n).astype(BF16)


def fourier_mixer(proj, w_fourier, *, batch, seq, col0):
    ng, cg, _ = w_fourier.shape
    fw = ng * cg
    tr = _tile(seq, 512)
    nr = seq // tr
    cc, sc = _dft_cos_sin(cg)
    cs, ss = _dft_cos_sin(seq)
    norm = 1.0 / math.sqrt(seq * cg)
    vmem = 2 * (seq * fw * 2 + 2 * cg * cg * 2 + 2 * tr * seq * 2 + ng * cg * cg * 4 + tr * fw * 2) \
        + 2 * seq * fw * 2 + 4 * tr * fw * 4
    return pl.pallas_call(
        functools.partial(_fourier_kernel, seq=seq, norm=norm),
        grid=(batch, nr),
        in_specs=[pl.BlockSpec((seq, fw), lambda b, r: (b, col0 // fw)),
                  pl.BlockSpec((cg, cg), lambda b, r: (0, 0)),
                  pl.BlockSpec((cg, cg), lambda b, r: (0, 0)),
                  pl.BlockSpec((tr, seq), lambda b, r: (r, 0)),
                  pl.BlockSpec((tr, seq), lambda b, r: (r, 0)),
                  pl.BlockSpec((ng, cg, cg), lambda b, r: (0, 0, 0))],
        out_specs=pl.BlockSpec((tr, fw), lambda b, r: (b * nr + r, 0)),
        out_shape=jax.ShapeDtypeStruct((batch * seq, fw), BF16),
        scratch_shapes=[pltpu.VMEM((2 * seq, fw), BF16)],
        compiler_params=_params(("parallel", "arbitrary"), vmem),
        name="fourier_mixer",
    )(proj, cc, sc, cs, ss, w_fourier)


def _router_affinity(hn, wr_ref, n_experts):
    hi = hn.astype(BF16)
    lo = (hn - hi.astype(F32)).astype(BF16)
    r_hi = jnp.dot(hi, wr_ref[...], preferred_element_type=F32)
    r_lo = jnp.dot(lo, wr_ref[...], preferred_element_type=F32)
    logits = r_hi[:, :LANES] + r_hi[:, LANES:] + r_lo[:, :LANES]
    lane = lax.broadcasted_iota(jnp.int32, logits.shape, 1)
    logits = jnp.where(lane < n_experts, logits, -jnp.inf)
    e = jnp.exp(logits - jnp.max(logits, axis=-1, keepdims=True))
    return e / jnp.sum(e, axis=-1, keepdims=True)


def _mm_res_norm_kernel(*refs, n_a, with_router, n_experts):
    a_refs = refs[:n_a]
    w_ref, x_ref, g_ref = refs[n_a:n_a + 3]
    pos = n_a + 3
    if with_router:
        wr_ref = refs[pos]
        pos += 1
    xo_ref, ho_ref = refs[pos:pos + 2]
    if with_router:
        aff_ref = refs[pos + 2]
    d = xo_ref.shape[1]
    tn = _tile(d, COL_CHUNK)
    for c in range(d // tn):
        sl = slice(c * tn, (c + 1) * tn)
        acc = x_ref[:, sl]
        row0 = 0
        for a_ref in a_refs:
            ka = a_ref.shape[1]
            acc = acc + jnp.dot(a_ref[...], w_ref[row0:row0 + ka, sl], preferred_element_type=F32)
            row0 += ka
        xo_ref[:, sl] = acc
    hn = _rms(xo_ref[...], g_ref[...])
    ho_ref[...] = hn.astype(ho_ref.dtype)
    if with_router:
        aff_ref[...] = _router_affinity(hn, wr_ref, n_experts)


def matmul_residual_norm(a_list, w, x_res, gain, *, router=None, tm_pref=512):
    m, d = x_res.shape
    k = w.shape[0]
    tm = _tile(m, tm_pref)
    in_specs = [pl.BlockSpec((tm, a.shape[1]), lambda i: (i, 0)) for a in a_list]
    in_specs += [pl.BlockSpec((k, d), lambda i: (0, 0)),
                 pl.BlockSpec((tm, d), lambda i: (i, 0)),
                 pl.BlockSpec((1, d), lambda i: (0, 0))]
    args = list(a_list) + [w, x_res, gain.reshape(1, d)]
    out_specs = [pl.BlockSpec((tm, d), lambda i: (i, 0)),
                 pl.BlockSpec((tm, d), lambda i: (i, 0))]
    out_shape = [jax.ShapeDtypeStruct((m, d), F32), jax.ShapeDtypeStruct((m, d), BF16)]
    n_experts = 0
    if router is not None:
        wr, n_experts = router
        in_specs.append(pl.BlockSpec((d, 2 * LANES), lambda i: (0, 0)))
        args.append(wr)
        out_specs.append(pl.BlockSpec((tm, LANES), lambda i: (i, 0)))
        out_shape.append(jax.ShapeDtypeStruct((m, LANES), F32))
    vmem = 2 * (tm * k * 2 + k * d * 2 + tm * d * 4 + tm * d * 4 + tm * d * 2) \
        + 3 * tm * d * 4 + 4 * d * LANES * 2
    return pl.pallas_call(
        functools.partial(_mm_res_norm_kernel, n_a=len(a_list),
                          with_router=router is not None, n_experts=n_experts),
        grid=(m // tm,),
        in_specs=in_specs,
        out_specs=out_specs,
        out_shape=out_shape,
        compiler_params=_params(("parallel",), vmem),
        name="matmul_residual_norm",
    )(*args)


def _cross_attn_kernel(h_ref, wq_ref, k_ref, v_ref, o_ref, *, n_heads, scale):
    dh = h_ref.shape[1] // n_heads
    h = h_ref[...]
    heads = [slice(i * dh, (i + 1) * dh) for i in range(n_heads)]
    qs = [jnp.dot(h, wq_ref[:, sl], preferred_element_type=F32).astype(BF16) for sl in heads]
    scores = [lax.dot_general(q, k_ref[:, sl], (((1,), (1,)), ((), ())),
                              preferred_element_type=F32) for q, sl in zip(qs, heads)]
    for sl, e in zip(heads, _softmax_numerators(scores, scale)):
        l = jnp.sum(e, axis=-1, keepdims=True)
        o = jnp.dot(e.astype(BF16), v_ref[:, sl], preferred_element_type=F32) / l
        o_ref[:, sl] = o.astype(o_ref.dtype)


def cross_attention(h, w_q, k, v, *, batch, seq, mem_len):
    d = h.shape[1]
    tq = _tile(seq, 1024)
    nq = seq // tq
    vmem = 2 * (2 * tq * d * 2 + d * d * 2 + 2 * mem_len * d * 2) + 6 * tq * mem_len * 4 + 3 * tq * d * 4
    return pl.pallas_call(
        functools.partial(_cross_attn_kernel, n_heads=N_MEM_HEADS,
                          scale=(d // N_MEM_HEADS) ** -0.5),
        grid=(batch, nq),
        in_specs=[pl.BlockSpec((tq, d), lambda b, i: (b * nq + i, 0)),
                  pl.BlockSpec((d, d), lambda b, i: (0, 0)),
                  pl.BlockSpec((mem_len, d), lambda b, i: (b, 0)),
                  pl.BlockSpec((mem_len, d), lambda b, i: (b, 0))],
        out_specs=pl.BlockSpec((tq, d), lambda b, i: (b * nq + i, 0)),
        out_shape=jax.ShapeDtypeStruct((batch * seq, d), BF16),
        compiler_params=_params(("parallel", "parallel"), vmem),
        name="cross_attention",
    )(h, w_q, k, v)


def _prefix_count(x, lane):
    n = x.shape[1]
    shift = 1
    while shift < n:
        x = x + jnp.where(lane >= shift, pltpu.roll(x, shift, axis=1), 0)
        shift *= 2
    return x


def _route_kernel(aff_ref, prow_ref, arow_ref, pcol_ref, *, n_experts, capacity):
    batch, e_pad, seq = prow_ref.shape
    a_t = jnp.concatenate([aff_ref[b * seq:(b + 1) * seq, :].T[:e_pad] for b in range(batch)],
                          axis=0)
    for b in range(batch):
        arow_ref[b] = a_t[b * e_pad:(b + 1) * e_pad]

    def body(i, thr_bits):
        cand = thr_bits | jnp.left_shift(jnp.int32(1), 30 - i)
        cnt = jnp.sum((a_t >= pltpu.bitcast(cand, F32)).astype(F32), axis=1, keepdims=True)
        return jnp.where(cnt >= capacity, cand, thr_bits)

    thr = pltpu.bitcast(lax.fori_loop(0, 31, body, jnp.zeros((batch * e_pad, 1), jnp.int32)), F32)
    lane = lax.broadcasted_iota(jnp.int32, a_t.shape, 1)
    gt = a_t > thr
    eq = a_t == thr
    need = capacity - jnp.sum(gt.astype(F32), axis=1, keepdims=True).astype(jnp.int32)
    sel = gt | (eq & (_prefix_count(eq.astype(jnp.int32), lane) <= need))
    pos = jnp.where(sel, _prefix_count(sel.astype(jnp.int32), lane) - 1, -1)
    row = lax.broadcasted_iota(jnp.int32, a_t.shape, 0)
    pos = jnp.where(row % e_pad < n_experts, pos, -1)
    unused = jnp.full((LANES - e_pad, seq), -1.0, F32)
    for b in range(batch):
        pos_b = pos[b * e_pad:(b + 1) * e_pad]
        prow_ref[b] = pos_b
        pcol_ref[b * seq:(b + 1) * seq, :] = jnp.concatenate([pos_b.astype(F32), unused], axis=0).T


def route(aff, *, batch, seq, n_experts, capacity):
    e_pad = max(8, n_experts)
    return pl.pallas_call(
        functools.partial(_route_kernel, n_experts=n_experts, capacity=capacity),
        grid=(1,),
        in_specs=[pl.BlockSpec((batch * seq, LANES), lambda i: (0, 0))],
        out_specs=[pl.BlockSpec((batch, e_pad, seq), lambda i: (0, 0, 0)),
                   pl.BlockSpec((batch, e_pad, seq), lambda i: (0, 0, 0)),
                   pl.BlockSpec((batch * seq, LANES), lambda i: (0, 0))],
        out_shape=[jax.ShapeDtypeStruct((batch, e_pad, seq), jnp.int32),
                   jax.ShapeDtypeStruct((batch, e_pad, seq), F32),
                   jax.ShapeDtypeStruct((batch * seq, LANES), F32)],
        compiler_params=_params(("arbitrary",), 8 * batch * seq * LANES * 4),
        name="route",
    )(aff)


GATHER_TILES = 4
COMBINE_TILES = 4


def _window_size(capacity, n_tiles):
    return capacity // n_tiles + 2 * _window_margin(capacity)


def _window_margin(capacity):
    return capacity // 8


def _window_start(tile, capacity, n_tiles, clip=jnp.clip):
    share = capacity // n_tiles
    unit = _window_margin(capacity)
    window = _window_size(capacity, n_tiles)
    assert share % unit == 0 and (capacity - window) % unit == 0
    return clip(tile * (share // unit) - 1, 0, (capacity - window) // unit) * unit


def _static_clip(v, lo, hi):
    return max(lo, min(v, hi))


def _slots_inside_windows(prow, *, n_experts, capacity, n_tiles):
    batch, _, seq = prow.shape
    pos = prow[:, :n_experts, :].reshape(batch, n_experts, n_tiles, seq // n_tiles)
    first = _window_start(jnp.arange(n_tiles, dtype=jnp.int32), capacity, n_tiles)[None, None, :]
    lowest = jnp.min(jnp.where(pos >= 0, pos, capacity), axis=-1)
    highest = jnp.max(pos, axis=-1)
    return (lowest >= first) & (highest < first + _window_size(capacity, n_tiles))


def _gather_kernel(ok_ref, h_ref, prow_ref, o_ref, *, experts_per_step, n_tiles):
    b = pl.program_id(0)
    g = pl.program_id(1)
    cap = o_ref.shape[1]
    seq = h_ref.shape[0]
    tt = seq // n_tiles
    window = _window_size(cap, n_tiles)
    starts = [_window_start(t, cap, n_tiles, _static_clip) for t in range(n_tiles)]
    unit = _window_margin(cap)
    for j in range(experts_per_step):
        expert = g * experts_per_step + j
        prow = prow_ref[0, pl.ds(expert, 1), :]
        in_windows = ok_ref[b, expert] != 0

        @pl.when(in_windows)
        def _(j=j, prow=prow):
            slot = lax.broadcasted_iota(jnp.int32, (window, tt), 0)
            parts = [jnp.dot((slot + starts[t] == prow[:, t * tt:(t + 1) * tt]).astype(BF16),
                             h_ref[t * tt:(t + 1) * tt, :], preferred_element_type=F32)
                     for t in range(n_tiles)]
            for r0 in range(0, cap, unit):
                covering = [parts[t][r0 - starts[t]:r0 - starts[t] + unit]
                            for t in range(n_tiles) if starts[t] <= r0 < starts[t] + window]
                o_ref[j, r0:r0 + unit, :] = functools.reduce(jnp.add, covering).astype(o_ref.dtype)

        @pl.when(jnp.logical_not(in_windows))
        def _(j=j, prow=prow):
            slot = lax.broadcasted_iota(jnp.int32, (cap, seq), 0)
            o_ref[j] = jnp.dot((slot == prow).astype(BF16), h_ref[...],
                               preferred_element_type=F32).astype(o_ref.dtype)


def gather_tokens(h, prow, *, batch, seq, n_experts, capacity):
    d = h.shape[1]
    e_pad = prow.shape[1]
    ng = _tile(n_experts, 4)
    ok = jnp.all(_slots_inside_windows(prow, n_experts=n_experts, capacity=capacity,
                                       n_tiles=GATHER_TILES), axis=-1).astype(jnp.int32)
    vmem = 2 * (seq * d * 2 + e_pad * seq * 4 + ng * capacity * d * 2) + 2 * capacity * seq * 8 \
        + 3 * capacity * d * 4
    return pl.pallas_call(
        functools.partial(_gather_kernel, experts_per_step=ng, n_tiles=GATHER_TILES),
        grid_spec=pltpu.PrefetchScalarGridSpec(
            num_scalar_prefetch=1,
            grid=(batch, n_experts // ng),
            in_specs=[pl.BlockSpec((seq, d), lambda b, g, ok_ref: (b, 0)),
                      pl.BlockSpec((1, e_pad, seq), lambda b, g, ok_ref: (b, 0, 0))],
            out_specs=pl.BlockSpec((ng, None, capacity, d), lambda b, g, ok_ref: (g, b, 0, 0))),
        out_shape=jax.ShapeDtypeStruct((n_experts, batch, capacity, d), BF16),
        compiler_params=_params(("parallel", "arbitrary"), vmem),
        name="gather_tokens",
    )(ok, h, prow)


def _ffn_kernel(x_ref, wg_ref, wu_ref, wd_ref, prow_ref, arow_ref, o_ref, acc_ref, gate_ref,
                *, capacity, gate_steps):
    e = pl.program_id(0)
    f = pl.program_id(1)
    nf = pl.num_programs(1)
    rows = x_ref.shape[0]
    n_seq = rows // capacity

    def chunk_product():
        x = x_ref[...]
        a = jnp.dot(x, wg_ref[...].astype(BF16), preferred_element_type=F32)
        u = jnp.dot(x, wu_ref[...].astype(BF16), preferred_element_type=F32)
        hidden = (a / (1.0 + jnp.exp(-a)) * u).astype(BF16)
        return jnp.dot(hidden, wd_ref[...].astype(BF16), preferred_element_type=F32)

    def gate_share():
        seq = prow_ref.shape[2]
        slot = lax.broadcasted_iota(jnp.int32, (capacity, seq), 0)
        per_step = -(-n_seq // gate_steps)
        for j in range(per_step):
            b = jnp.minimum(f * per_step + j, n_seq - 1)
            mine = slot == prow_ref[b, pl.ds(e, 1), :]
            gate_ref[pl.ds(pl.multiple_of(b * capacity, capacity), capacity), :] = jnp.sum(
                jnp.where(mine, arow_ref[b, pl.ds(e, 1), :], 0.0), axis=1, keepdims=True)

    @pl.when(f == 0)
    def _():
        gate_share()
        acc_ref[...] = chunk_product()

    @pl.when((f > 0) & (f < nf - 1))
    def _():
        gate_share()
        acc_ref[...] += chunk_product()

    @pl.when(f == nf - 1)
    def _():
        gate_share()
        o_ref[...] = ((acc_ref[...] + chunk_product()) * gate_ref[...]).astype(o_ref.dtype)


def expert_ffn(xs, w_gate, w_up, w_down, prow, arow, layer, *, rows_per_expert, capacity):
    _, n_experts, d, ff = w_gate.shape
    tf = _tile(ff // 2, 512)
    r = rows_per_expert
    batch, e_pad, seq = prow.shape
    vmem = 2 * (r * d * 2 + 3 * d * tf * 4 + r * d * 2 + 2 * batch * e_pad * seq * 4) + r * d * 4 \
        + r * LANES * 4 + 3 * d * tf * 2 + 4 * r * tf * 4 + r * d * 4 + 2 * capacity * seq * 4
    return pl.pallas_call(
        functools.partial(_ffn_kernel, capacity=capacity, gate_steps=ff // tf),
        grid=(n_experts, ff // tf),
        in_specs=[pl.BlockSpec((r, d), lambda e, f: (e, 0)),
                  pl.BlockSpec((None, None, d, tf), lambda e, f: (layer, e, 0, f)),
                  pl.BlockSpec((None, None, d, tf), lambda e, f: (layer, e, 0, f)),
                  pl.BlockSpec((None, None, tf, d), lambda e, f: (layer, e, f, 0)),
                  pl.BlockSpec((batch, e_pad, seq), lambda e, f: (0, 0, 0)),
                  pl.BlockSpec((batch, e_pad, seq), lambda e, f: (0, 0, 0))],
        out_specs=pl.BlockSpec((r, d), lambda e, f: (e, 0)),
        out_shape=jax.ShapeDtypeStruct((n_experts * r, d), BF16),
        scratch_shapes=[pltpu.VMEM((r, d), F32), pltpu.VMEM((r, 1), F32)],
        compiler_params=_params(("parallel", "arbitrary"), vmem),
        name="expert_ffn",
    )(xs, w_gate, w_up, w_down, prow, arow)


def _combine_kernel(ok_ref, x_ref, ywin_ref, yall_ref, pcol_ref, g_ref, out_ref, *, final_norm):
    b = pl.program_id(0)
    t = pl.program_id(1)
    g = pl.program_id(2)
    tt, d = x_ref.shape
    n_group, _, window, _ = ywin_ref.shape
    cap = yall_ref.shape[1]

    @pl.when(g == 0)
    def _():
        out_ref[...] = x_ref[...]

    lane = lax.broadcasted_iota(jnp.int32, (tt, LANES), 1)
    pcol = pcol_ref[...]

    def scatter(y_rows, n_slots, first_slot):
        slot = (lax.broadcasted_iota(jnp.int32, (tt, n_slots), 1) + first_slot).astype(F32)
        hots = []
        for j in range(n_group):
            slot_of_token = jnp.sum(jnp.where(lane == g * n_group + j, pcol, 0.0),
                                    axis=1, keepdims=True)
            hots.append((slot == slot_of_token).astype(BF16))
        out_ref[...] += jnp.dot(jnp.concatenate(hots, axis=1), y_rows, preferred_element_type=F32)

    in_window = ok_ref[b, t, g] != 0

    @pl.when(in_window)
    def _():
        scatter(ywin_ref[...].reshape(n_group * window, d), window,
                _window_start(t, cap, COMBINE_TILES))

    @pl.when(jnp.logical_not(in_window))
    def _():
        scatter(yall_ref[...].reshape(n_group * cap, d), cap, 0)

    if final_norm:
        @pl.when(g == pl.num_programs(2) - 1)
        def _():
            out_ref[...] = _rms(out_ref[...], g_ref[...])


def combine(x_res, y, prow, pcol, gain, *, batch, seq, n_experts, capacity, final_norm):
    d = x_res.shape[1]
    nt = COMBINE_TILES
    tt = seq // nt
    window = _window_size(capacity, nt)
    ng = _tile(n_experts, 8)
    n_groups = n_experts // ng
    y4 = y.reshape(n_experts, batch, capacity, d)
    inside = _slots_inside_windows(prow, n_experts=n_experts, capacity=capacity, n_tiles=nt)
    ok = jnp.all(inside.reshape(batch, n_groups, ng, nt), axis=2).transpose(0, 2, 1).astype(jnp.int32)
    vmem = 2 * (tt * d * 4 + ng * window * d * 2 + ng * capacity * d * 2 + tt * LANES * 4 + tt * d * 4) \
        + 3 * tt * d * 4 + 2 * tt * capacity * 4

    def full_block(b, t, g, ok_ref):
        need = ok_ref[b, t, g] == 0
        return (jnp.where(need, g, 0), jnp.where(need, b, 0), 0, 0)

    return pl.pallas_call(
        functools.partial(_combine_kernel, final_norm=final_norm),
        grid_spec=pltpu.PrefetchScalarGridSpec(
            num_scalar_prefetch=1,
            grid=(batch, nt, n_groups),
            in_specs=[pl.BlockSpec((tt, d), lambda b, t, g, ok_ref: (b * nt + t, 0)),
                      pl.BlockSpec((pl.Element(ng), pl.Element(1), pl.Element(window), pl.Element(d)),
                                   lambda b, t, g, ok_ref: (g * ng, b, _window_start(t, capacity, nt), 0)),
                      pl.BlockSpec((ng, None, capacity, d), full_block),
                      pl.BlockSpec((tt, LANES), lambda b, t, g, ok_ref: (b * nt + t, 0)),
                      pl.BlockSpec((1, d), lambda b, t, g, ok_ref: (0, 0))],
            out_specs=pl.BlockSpec((tt, d), lambda b, t, g, ok_ref: (b * nt + t, 0))),
        out_shape=jax.ShapeDtypeStruct((batch * seq, d), F32),
        compiler_params=_params(("parallel", "parallel", "arbitrary"), vmem),
        name="combine",
    )(ok, x_res, y4, y4, pcol, gain.reshape(1, d))


def _pool_kernel(x_ref, prev_ref, next_ref, ng_ref, w_ref, s_ref, g_ref, xo_ref, ho_ref, pad_ref,
                 *, seq, windows):
    i = pl.program_id(0)
    tm, d = x_ref.shape
    halo = prev_ref.shape[0]
    pg = w_ref.shape[1]
    tiles_per_seq = seq // tm
    tile_in_seq = i % tiles_per_seq
    norm_gain = ng_ref[...]
    h = _rms(x_ref[...], norm_gain)
    pad_ref[0:halo, :] = jnp.where(tile_in_seq == 0, 0.0, _rms(prev_ref[...], norm_gain))
    pad_ref[halo:halo + tm, :] = h
    pad_ref[halo + tm:2 * halo + tm, :] = jnp.where(tile_in_seq == tiles_per_seq - 1, 0.0,
                                                    _rms(next_ref[...], norm_gain))
    t = tile_in_seq * tm + lax.broadcasted_iota(jnp.int32, (tm, 1), 0)
    for gi, w in enumerate(windows):
        cols = slice(gi * pg, (gi + 1) * pg)
        acc = pad_ref[:, cols]
        span = 1
        while span < w:
            acc = acc + pltpu.roll(acc, acc.shape[0] - span, axis=0)
            span *= 2
        total = acc[halo - w // 2:halo - w // 2 + tm, :]
        count = (jnp.minimum(t + (w - w // 2), seq) - jnp.maximum(t - w // 2, 0)).astype(F32)
        pooled = (total / count - h[:, cols]).astype(BF16)
        mixed = jnp.dot(pooled, w_ref[gi], preferred_element_type=F32)
        xo_ref[:, cols] = mixed * s_ref[:, cols] + x_ref[:, cols]
    ho_ref[...] = _rms(xo_ref[...], g_ref[...]).astype(ho_ref.dtype)


def pool_mixer(x, norm_gain, w_pool, pool_scale, next_gain, *, seq):
    m, d = x.shape
    ng, pg, _ = w_pool.shape
    halo = 8
    assert all(w & (w - 1) == 0 and w // 2 <= halo for w in POOL_WINDOWS) and ng == len(POOL_WINDOWS)
    tm = _tile(seq, 512)
    per = tm // halo
    n_halo_blocks = m // halo
    vmem = 2 * (2 * tm * d * 4 + 2 * halo * d * 4 + ng * pg * pg * 2 + tm * d * 2) \
        + (tm + 2 * halo) * d * 4 + 6 * tm * d * 4
    return pl.pallas_call(
        functools.partial(_pool_kernel, seq=seq, windows=POOL_WINDOWS),
        grid=(m // tm,),
        in_specs=[pl.BlockSpec((tm, d), lambda i: (i, 0)),
                  pl.BlockSpec((halo, d), lambda i: (jnp.maximum(i * per - 1, 0), 0)),
                  pl.BlockSpec((halo, d), lambda i: (jnp.minimum((i + 1) * per, n_halo_blocks - 1), 0)),
                  pl.BlockSpec((1, d), lambda i: (0, 0)),
                  pl.BlockSpec((ng, pg, pg), lambda i: (0, 0, 0)),
                  pl.BlockSpec((1, d), lambda i: (0, 0)),
                  pl.BlockSpec((1, d), lambda i: (0, 0))],
        out_specs=[pl.BlockSpec((tm, d), lambda i: (i, 0)),
                   pl.BlockSpec((tm, d), lambda i: (i, 0))],
        out_shape=[jax.ShapeDtypeStruct((m, d), F32), jax.ShapeDtypeStruct((m, d), BF16)],
        scratch_shapes=[pltpu.VMEM((tm + 2 * halo, d), F32)],
        compiler_params=_params(("parallel",), vmem),
        name="pool_mixer",
    )(x, x, x, norm_gain.reshape(1, d), w_pool, pool_scale.reshape(1, d), next_gain.reshape(1, d))


def _rope_tables(seq):
    half = HEAD_DIM // 2
    rows = seq // GRID_W
    row_idx = jnp.repeat(jnp.arange(rows), GRID_W).astype(F32)
    col_idx = jnp.tile(jnp.arange(GRID_W), rows).astype(F32)
    inv_freq = 1.0 / (ROPE_THETA ** (jnp.arange(0, half, 2, dtype=F32) / half))
    ang = jnp.concatenate([row_idx[:, None] * inv_freq[None, :],
                           col_idx[:, None] * inv_freq[None, :]], axis=-1)
    cos, sin = jnp.cos(ang), jnp.sin(ang)
    zero = jnp.zeros_like(sin)
    cos_full = jnp.stack([cos, cos], axis=-1).reshape(seq, HEAD_DIM)
    sin_even = jnp.stack([-sin, zero], axis=-1).reshape(seq, HEAD_DIM)
    sin_odd = jnp.stack([zero, sin], axis=-1).reshape(seq, HEAD_DIM)
    return cos_full, sin_even, sin_odd


def _split_router(w_router):
    d, n_experts = w_router.shape
    w = jnp.pad(w_router, ((0, 0), (0, LANES - n_experts)))
    hi = w.astype(BF16)
    lo = (w - hi.astype(F32)).astype(BF16)
    return jnp.concatenate([hi, lo], axis=1), n_experts


def _moe(x_res, h, aff, w_gate, w_up, w_down, layer, gain, *, batch, seq, final_norm):
    n_experts = w_gate.shape[1]
    capacity = EC_CAPACITY_FACTOR * seq // n_experts
    d = h.shape[1]
    prow, arow, pcol = route(aff, batch=batch, seq=seq, n_experts=n_experts, capacity=capacity)
    xs = gather_tokens(h, prow, batch=batch, seq=seq, n_experts=n_experts, capacity=capacity)
    y = expert_ffn(xs.reshape(n_experts * batch * capacity, d), w_gate, w_up, w_down, prow, arow,
                   layer, rows_per_expert=batch * capacity, capacity=capacity)
    return combine(x_res, y, prow, pcol, gain, batch=batch, seq=seq,
                   n_experts=n_experts, capacity=capacity, final_norm=final_norm)


def _cross_block(x, h, mem2d, layer, cross_w_q, cross_w_k, cross_w_v, cross_w_o, mem_norm,
                 ffn_norm, router_w, *, batch, seq, mem_len):
    mem_n = rms_norm_rows(mem2d, mem_norm[layer], BF16)
    k = matmul_streamed(mem_n, cross_w_k, (layer,))
    v = matmul_streamed(mem_n, cross_w_v, (layer,))
    o = cross_attention(h, cast_weight(cross_w_q, (layer,)), k, v,
                        batch=batch, seq=seq, mem_len=mem_len)
    return matmul_residual_norm([o], cast_weight(cross_w_o, (layer,)), x, ffn_norm[layer],
                                router=_split_router(router_w[layer]))


def kernel(x, mem, mix_norm, attn_w_in, q_gain, k_gain, fourier_w, attn_w_out, pool_w, pool_scale,
           cross_norm, mem_norm, cross_w_q, cross_w_k, cross_w_v, cross_w_o, ffn_norm, router_w,
           expert_w_gate, expert_w_up, expert_w_down, final_norm):
    batch, seq, d = x.shape
    mem_len = mem.shape[1]
    depth = mix_norm.shape[0]
    attn_width = N_Q_HEADS * HEAD_DIM
    kv_width = N_KV_HEADS * HEAD_DIM
    fourier_width = N_FOURIER_GROUPS * FOURIER_GROUP
    rope_width = attn_width + kv_width
    cos_t, sin_a, sin_b = _rope_tables(seq)

    xf = x.reshape(batch * seq, d)
    mem2d = mem.reshape(batch * mem_len, d)
    cross = functools.partial(_cross_block, mem2d=mem2d, cross_w_q=cross_w_q, cross_w_k=cross_w_k,
                              cross_w_v=cross_w_v, cross_w_o=cross_w_o, mem_norm=mem_norm,
                              ffn_norm=ffn_norm, router_w=router_w,
                              batch=batch, seq=seq, mem_len=mem_len)
    for layer in range(depth):
        i = layer // 2
        if layer % 2 == 0:
            gain_cols = jnp.concatenate([jnp.tile(q_gain[i], N_Q_HEADS), jnp.tile(k_gain[i], N_KV_HEADS),
                                         jnp.ones((kv_width + fourier_width,), F32)]).reshape(1, -1)
            proj = proj_in(xf, mix_norm[layer], cast_weight(attn_w_in, (i,)), gain_cols,
                           cos_t, sin_a, sin_b, seq=seq, rope_width=rope_width)
            o_attn = gqa_attention(proj, batch=batch, seq=seq, attn_width=attn_width, kv_width=kv_width)
            o_four = fourier_mixer(proj, fourier_w[i], batch=batch, seq=seq,
                                   col0=attn_width + 2 * kv_width)
            xf, h = matmul_residual_norm([o_attn, o_four], cast_weight(attn_w_out, (i,)), xf,
                                         cross_norm[layer])
        else:
            xf, h = pool_mixer(xf, mix_norm[layer], pool_w[i].astype(BF16), pool_scale[i],
                               cross_norm[layer], seq=seq)
        xf, h, aff = cross(xf, h, layer=layer)
        xf = _moe(xf, h, aff, expert_w_gate, expert_w_up, expert_w_down, layer, final_norm,
                  batch=batch, seq=seq, final_norm=layer == depth - 1)
    return xf.reshape(batch, seq, d)
```

```python
import functools
import math

import jax
import jax.numpy as jnp
from jax import lax
from jax.experimental import pallas as pl
from jax.experimental.pallas import tpu as pltpu

F32 = jnp.float32
BF16 = jnp.bfloat16

GRID_W = 64
HEAD_DIM = 128
N_Q_HEADS = 12
N_KV_HEADS = 4
N_FOURIER_GROUPS = 4
FOURIER_GROUP = 128
ROPE_THETA = 10000.0
POOL_WINDOWS = (2, 4, 8, 16)
N_MEM_HEADS = 4
EC_CAPACITY_FACTOR = 2
NORM_EPS = 1e-6

LANES = 128
SUBLANES = 8
V7X_VMEM_BYTES = 64 * 1024 * 1024
VMEM_HEADROOM_BYTES = 6 * 1024 * 1024
COL_CHUNK = 512
LOG2_E = math.log2(math.e)


def _params(semantics, vmem_bytes):
    limit = min(int(vmem_bytes) + VMEM_HEADROOM_BYTES, V7X_VMEM_BYTES - VMEM_HEADROOM_BYTES)
    return pltpu.CompilerParams(dimension_semantics=semantics, vmem_limit_bytes=limit)


def _tile(n, pref):
    t = min(n, pref)
    while n % t:
        t //= 2
    return t


def _rms(x, gain):
    ms = jnp.mean(x * x, axis=-1, keepdims=True)
    return x * lax.rsqrt(ms + NORM_EPS) * gain


def _softmax_numerators(scores, scale):
    c = scale * LOG2_E
    return [jnp.exp2((s - jnp.max(s, axis=-1, keepdims=True)) * c) for s in scores]


def _norm_kernel(x_ref, g_ref, o_ref):
    o_ref[...] = _rms(x_ref[...], g_ref[...]).astype(o_ref.dtype)


def rms_norm_rows(x, gain, out_dtype):
    m, d = x.shape
    tm = _tile(m, 512)
    return pl.pallas_call(
        _norm_kernel,
        grid=(m // tm,),
        in_specs=[pl.BlockSpec((tm, d), lambda i: (i, 0)),
                  pl.BlockSpec((1, d), lambda i: (0, 0))],
        out_specs=pl.BlockSpec((tm, d), lambda i: (i, 0)),
        out_shape=jax.ShapeDtypeStruct((m, d), out_dtype),
        compiler_params=_params(("parallel",), 2 * tm * d * 8),
        name="rms_norm_rows",
    )(x, gain.reshape(1, d))


def _cast_kernel(w_ref, o_ref):
    o_ref[...] = w_ref[...].astype(o_ref.dtype)


def cast_weight(w, lead):
    k, n = w.shape[-2:]
    tk = _tile(k, 1024)
    squeezed = (None,) * len(lead)
    return pl.pallas_call(
        _cast_kernel,
        grid=(k // tk,),
        in_specs=[pl.BlockSpec(squeezed + (tk, n), lambda i: tuple(lead) + (i, 0))],
        out_specs=pl.BlockSpec((tk, n), lambda i: (i, 0)),
        out_shape=jax.ShapeDtypeStruct((k, n), BF16),
        compiler_params=_params(("parallel",), 2 * tk * n * 6),
        name="cast_weight",
    )(w)


def _mm_stream_kernel(a_ref, w_ref, o_ref):
    o_ref[...] = jnp.dot(a_ref[...], w_ref[...].astype(BF16),
                         preferred_element_type=F32).astype(o_ref.dtype)


def matmul_streamed(a, w, lead, out_dtype=BF16):
    m, k = a.shape
    n = w.shape[-1]
    tn = _tile(n, COL_CHUNK)
    squeezed = (None,) * len(lead)
    vmem = 2 * (m * k * 2 + k * tn * 4 + m * tn * 2) + k * tn * 2 + m * tn * 4
    return pl.pallas_call(
        _mm_stream_kernel,
        grid=(n // tn,),
        in_specs=[pl.BlockSpec((m, k), lambda j: (0, 0)),
                  pl.BlockSpec(squeezed + (k, tn), lambda j: tuple(lead) + (0, j))],
        out_specs=pl.BlockSpec((m, tn), lambda j: (0, j)),
        out_shape=jax.ShapeDtypeStruct((m, n), out_dtype),
        compiler_params=_params(("parallel",), vmem),
        name="matmul_streamed",
    )(a, w)


def _proj_in_kernel(a_ref, ng_ref, w_ref, g_ref, cos_ref, sa_ref, sb_ref, o_ref, *, n_rope_heads):
    n = o_ref.shape[1]
    tn = _tile(n, COL_CHUNK)
    a = _rms(a_ref[...], ng_ref[...]).astype(BF16)
    cos, sin_a, sin_b = cos_ref[...], sa_ref[...], sb_ref[...]
    chunks = [jnp.dot(a, w_ref[:, c * tn:(c + 1) * tn], preferred_element_type=F32)
              for c in range(n // tn)]
    for c, acc in enumerate(chunks):
        for h in range(tn // HEAD_DIM):
            head = c * (tn // HEAD_DIM) + h
            col = slice(head * HEAD_DIM, (head + 1) * HEAD_DIM)
            xh = acc[:, h * HEAD_DIM:(h + 1) * HEAD_DIM]
            if head < n_rope_heads:
                ms = jnp.mean(xh * xh, axis=-1, keepdims=True)
                y = xh * lax.rsqrt(ms + NORM_EPS) * g_ref[:, col]
                xh = (y * cos + pltpu.roll(y, HEAD_DIM - 1, axis=1) * sin_a
                      + pltpu.roll(y, 1, axis=1) * sin_b)
            o_ref[:, col] = xh.astype(o_ref.dtype)


def proj_in(x, norm_gain, w_in, gain_cols, cos_t, sin_a, sin_b, *, seq, rope_width):
    m, k = x.shape
    n = w_in.shape[1]
    tm = _tile(seq, 512)
    n_pos_blocks = seq // tm
    vmem = 2 * (tm * k * 4 + k * n * 2 + tm * n * 2 + 3 * tm * HEAD_DIM * 4) + 6 * tm * COL_CHUNK * 4 \
        + 2 * tm * k * 4
    table = pl.BlockSpec((tm, HEAD_DIM), lambda i: (i % n_pos_blocks, 0))
    return pl.pallas_call(
        functools.partial(_proj_in_kernel, n_rope_heads=rope_width // HEAD_DIM),
        grid=(m // tm,),
        in_specs=[pl.BlockSpec((tm, k), lambda i: (i, 0)),
                  pl.BlockSpec((1, k), lambda i: (0, 0)),
                  pl.BlockSpec((k, n), lambda i: (0, 0)),
                  pl.BlockSpec((1, n), lambda i: (0, 0)),
                  table, table, table],
        out_specs=pl.BlockSpec((tm, n), lambda i: (i, 0)),
        out_shape=jax.ShapeDtypeStruct((m, n), BF16),
        compiler_params=_params(("parallel",), vmem),
        name="proj_in",
    )(x, norm_gain.reshape(1, k), w_in, gain_cols, cos_t, sin_a, sin_b)


def _attn_kernel(q_ref, k_ref, v_ref, o_ref, *, group, scale):
    k = k_ref[...]
    v = v_ref[...]
    v_ones = jnp.concatenate([v, jnp.ones_like(v)], axis=1)
    heads = [slice(g * HEAD_DIM, (g + 1) * HEAD_DIM) for g in range(group)]
    scores = [lax.dot_general(q_ref[:, sl], k, (((1,), (1,)), ((), ())),
                              preferred_element_type=F32) for sl in heads]
    for sl, e in zip(heads, _softmax_numerators(scores, scale)):
        ol = jnp.dot(e.astype(BF16), v_ones, preferred_element_type=F32)
        o_ref[:, sl] = (ol[:, :HEAD_DIM] / ol[:, HEAD_DIM:HEAD_DIM + 1]).astype(o_ref.dtype)


def gqa_attention(proj, *, batch, seq, attn_width, kv_width):
    group = N_Q_HEADS // N_KV_HEADS
    gw = group * HEAD_DIM
    tq = _tile(seq, 1024)
    nq = seq // tq
    k_col0 = attn_width // HEAD_DIM
    v_col0 = (attn_width + kv_width) // HEAD_DIM
    vmem = 2 * (tq * gw * 2 * 2 + 2 * seq * HEAD_DIM * 2) + group * tq * seq * 6
    return pl.pallas_call(
        functools.partial(_attn_kernel, group=group, scale=HEAD_DIM ** -0.5),
        grid=(batch, N_KV_HEADS, nq),
        in_specs=[pl.BlockSpec((tq, gw), lambda b, h, i: (b * nq + i, h)),
                  pl.BlockSpec((seq, HEAD_DIM), lambda b, h, i: (b, k_col0 + h)),
                  pl.BlockSpec((seq, HEAD_DIM), lambda b, h, i: (b, v_col0 + h))],
        out_specs=pl.BlockSpec((tq, gw), lambda b, h, i: (b * nq + i, h)),
        out_shape=jax.ShapeDtypeStruct((batch * seq, attn_width), BF16),
        compiler_params=_params(("parallel", "parallel", "parallel"), vmem),
        name="gqa_attention",
    )(proj, proj, proj)


def _fourier_kernel(f_ref, cc_ref, sc_ref, cs_ref, ss_ref, wf_ref, o_ref, ab_ref, *, seq, norm):
    r = pl.program_id(1)
    ng = wf_ref.shape[0]
    cg = wf_ref.shape[1]

    @pl.when(r == 0)
    def _():
        for g in range(ng):
            fg = f_ref[:, g * cg:(g + 1) * cg]
            ab_ref[0:seq, g * cg:(g + 1) * cg] = jnp.dot(
                fg, cc_ref[...], preferred_element_type=F32).astype(BF16)
            ab_ref[seq:2 * seq, g * cg:(g + 1) * cg] = jnp.dot(
                fg, sc_ref[...], preferred_element_type=F32).astype(BF16)

    z = (jnp.dot(cs_ref[...], ab_ref[0:seq, :], preferred_element_type=F32)
         - jnp.dot(ss_ref[...], ab_ref[seq:2 * seq, :], preferred_element_type=F32)) * norm
    for g in range(ng):
        zg = z[:, g * cg:(g + 1) * cg].astype(BF16)
        o_ref[:, g * cg:(g + 1) * cg] = jnp.dot(
            zg, wf_ref[g].astype(BF16), preferred_element_type=F32).astype(o_ref.dtype)


def _dft_cos_sin(n):
    lo = _tile(n, 64)
    hi = n // lo
    j = jnp.arange(n, dtype=jnp.int32)

    def table(k):
        ang = ((k[:, None] * j[None, :]) % n).astype(F32) * (2.0 * math.pi / n)
        return jnp.cos(ang), jnp.sin(ang)

    ca, sa = table(jnp.arange(hi, dtype=jnp.int32) * lo)
    cb, sb = table(jnp.arange(lo, dtype=jnp.int32))
    cos = ca[:, None, :] * cb[None, :, :] - sa[:, None, :] * sb[None, :, :]
    sin = sa[:, None, :] * cb[None, :, :] + ca[:, None, :] * sb[None, :, :]
    return cos.reshape(n, n).astype(BF16), sin.reshape(n, n).astype(BF16)


def fourier_mixer(proj, w_fourier, *, batch, seq, col0):
    ng, cg, _ = w_fourier.shape
    fw = ng * cg
    tr = _tile(seq, 512)
    nr = seq // tr
    cc, sc = _dft_cos_sin(cg)
    cs, ss = _dft_cos_sin(seq)
    norm = 1.0 / math.sqrt(seq * cg)
    vmem = 2 * (seq * fw * 2 + 2 * cg * cg * 2 + 2 * tr * seq * 2 + ng * cg * cg * 4 + tr * fw * 2) \
        + 2 * seq * fw * 2 + 4 * tr * fw * 4
    return pl.pallas_call(
        functools.partial(_fourier_kernel, seq=seq, norm=norm),
        grid=(batch, nr),
        in_specs=[pl.BlockSpec((seq, fw), lambda b, r: (b, col0 // fw)),
                  pl.BlockSpec((cg, cg), lambda b, r: (0, 0)),
                  pl.BlockSpec((cg, cg), lambda b, r: (0, 0)),
                  pl.BlockSpec((tr, seq), lambda b, r: (r, 0)),
                  pl.BlockSpec((tr, seq), lambda b, r: (r, 0)),
                  pl.BlockSpec((ng, cg, cg), lambda b, r: (0, 0, 0))],
        out_specs=pl.BlockSpec((tr, fw), lambda b, r: (b * nr + r, 0)),
        out_shape=jax.ShapeDtypeStruct((batch * seq, fw), BF16),
        scratch_shapes=[pltpu.VMEM((2 * seq, fw), BF16)],
        compiler_params=_params(("parallel", "arbitrary"), vmem),
        name="fourier_mixer",
    )(proj, cc, sc, cs, ss, w_fourier)


def _router_affinity(hn, wr_ref, n_experts):
    hi = hn.astype(BF16)
    lo = (hn - hi.astype(F32)).astype(BF16)
    r_hi = jnp.dot(hi, wr_ref[...], preferred_element_type=F32)
    r_lo = jnp.dot(lo, wr_ref[...], preferred_element_type=F32)
    logits = r_hi[:, :LANES] + r_hi[:, LANES:] + r_lo[:, :LANES]
    lane = lax.broadcasted_iota(jnp.int32, logits.shape, 1)
    logits = jnp.where(lane < n_experts, logits, -jnp.inf)
    e = jnp.exp(logits - jnp.max(logits, axis=-1, keepdims=True))
    return e / jnp.sum(e, axis=-1, keepdims=True)


def _mm_res_norm_kernel(*refs, n_a, with_router, n_experts):
    a_refs = refs[:n_a]
    w_ref, x_ref, g_ref = refs[n_a:n_a + 3]
    pos = n_a + 3
    if with_router:
        wr_ref = refs[pos]
        pos += 1
    xo_ref, ho_ref = refs[pos:pos + 2]
    if with_router:
        aff_ref = refs[pos + 2]
    d = xo_ref.shape[1]
    tn = _tile(d, COL_CHUNK)
    for c in range(d // tn):
        sl = slice(c * tn, (c + 1) * tn)
        acc = x_ref[:, sl]
        row0 = 0
        for a_ref in a_refs:
            ka = a_ref.shape[1]
            acc = acc + jnp.dot(a_ref[...], w_ref[row0:row0 + ka, sl], preferred_element_type=F32)
            row0 += ka
        xo_ref[:, sl] = acc
    hn = _rms(xo_ref[...], g_ref[...])
    ho_ref[...] = hn.astype(ho_ref.dtype)
    if with_router:
        aff_ref[...] = _router_affinity(hn, wr_ref, n_experts)


def matmul_residual_norm(a_list, w, x_res, gain, *, router=None, tm_pref=512):
    m, d = x_res.shape
    k = w.shape[0]
    tm = _tile(m, tm_pref)
    in_specs = [pl.BlockSpec((tm, a.shape[1]), lambda i: (i, 0)) for a in a_list]
    in_specs += [pl.BlockSpec((k, d), lambda i: (0, 0)),
                 pl.BlockSpec((tm, d), lambda i: (i, 0)),
                 pl.BlockSpec((1, d), lambda i: (0, 0))]
    args = list(a_list) + [w, x_res, gain.reshape(1, d)]
    out_specs = [pl.BlockSpec((tm, d), lambda i: (i, 0)),
                 pl.BlockSpec((tm, d), lambda i: (i, 0))]
    out_shape = [jax.ShapeDtypeStruct((m, d), F32), jax.ShapeDtypeStruct((m, d), BF16)]
    n_experts = 0
    if router is not None:
        wr, n_experts = router
        in_specs.append(pl.BlockSpec((d, 2 * LANES), lambda i: (0, 0)))
        args.append(wr)
        out_specs.append(pl.BlockSpec((tm, LANES), lambda i: (i, 0)))
        out_shape.append(jax.ShapeDtypeStruct((m, LANES), F32))
    vmem = 2 * (tm * k * 2 + k * d * 2 + tm * d * 4 + tm * d * 4 + tm * d * 2) \
        + 3 * tm * d * 4 + 4 * d * LANES * 2
    return pl.pallas_call(
        functools.partial(_mm_res_norm_kernel, n_a=len(a_list),
                          with_router=router is not None, n_experts=n_experts),
        grid=(m // tm,),
        in_specs=in_specs,
        out_specs=out_specs,
        out_shape=out_shape,
        compiler_params=_params(("parallel",), vmem),
        name="matmul_residual_norm",
    )(*args)


def _cross_attn_kernel(h_ref, wq_ref, k_ref, v_ref, o_ref, *, n_heads, scale):
    dh = h_ref.shape[1] // n_heads
    h = h_ref[...]
    heads = [slice(i * dh, (i + 1) * dh) for i in range(n_heads)]
    qs = [jnp.dot(h, wq_ref[:, sl], preferred_element_type=F32).astype(BF16) for sl in heads]
    scores = [lax.dot_general(q, k_ref[:, sl], (((1,), (1,)), ((), ())),
                              preferred_element_type=F32) for q, sl in zip(qs, heads)]
    for sl, e in zip(heads, _softmax_numerators(scores, scale)):
        l = jnp.sum(e, axis=-1, keepdims=True)
        o = jnp.dot(e.astype(BF16), v_ref[:, sl], preferred_element_type=F32) / l
        o_ref[:, sl] = o.astype(o_ref.dtype)


def cross_attention(h, w_q, k, v, *, batch, seq, mem_len):
    d = h.shape[1]
    tq = _tile(seq, 1024)
    nq = seq // tq
    vmem = 2 * (2 * tq * d * 2 + d * d * 2 + 2 * mem_len * d * 2) + 6 * tq * mem_len * 4 + 3 * tq * d * 4
    return pl.pallas_call(
        functools.partial(_cross_attn_kernel, n_heads=N_MEM_HEADS,
                          scale=(d // N_MEM_HEADS) ** -0.5),
        grid=(batch, nq),
        in_specs=[pl.BlockSpec((tq, d), lambda b, i: (b * nq + i, 0)),
                  pl.BlockSpec((d, d), lambda b, i: (0, 0)),
                  pl.BlockSpec((mem_len, d), lambda b, i: (b, 0)),
                  pl.BlockSpec((mem_len, d), lambda b, i: (b, 0))],
        out_specs=pl.BlockSpec((tq, d), lambda b, i: (b * nq + i, 0)),
        out_shape=jax.ShapeDtypeStruct((batch * seq, d), BF16),
        compiler_params=_params(("parallel", "parallel"), vmem),
        name="cross_attention",
    )(h, w_q, k, v)


def _prefix_count(x, lane):
    n = x.shape[1]
    shift = 1
    while shift < n:
        x = x + jnp.where(lane >= shift, pltpu.roll(x, shift, axis=1), 0)
        shift *= 2
    return x


def _route_kernel(aff_ref, prow_ref, arow_ref, pcol_ref, *, n_experts, capacity):
    batch, e_pad, seq = prow_ref.shape
    a_t = jnp.concatenate([aff_ref[b * seq:(b + 1) * seq, :].T[:e_pad] for b in range(batch)],
                          axis=0)
    for b in range(batch):
        arow_ref[b] = a_t[b * e_pad:(b + 1) * e_pad]

    def body(i, thr_bits):
        cand = thr_bits | jnp.left_shift(jnp.int32(1), 30 - i)
        cnt = jnp.sum((a_t >= pltpu.bitcast(cand, F32)).astype(F32), axis=1, keepdims=True)
        return jnp.where(cnt >= capacity, cand, thr_bits)

    thr = pltpu.bitcast(lax.fori_loop(0, 31, body, jnp.zeros((batch * e_pad, 1), jnp.int32)), F32)
    lane = lax.broadcasted_iota(jnp.int32, a_t.shape, 1)
    gt = a_t > thr
    eq = a_t == thr
    need = capacity - jnp.sum(gt.astype(F32), axis=1, keepdims=True).astype(jnp.int32)
    sel = gt | (eq & (_prefix_count(eq.astype(jnp.int32), lane) <= need))
    pos = jnp.where(sel, _prefix_count(sel.astype(jnp.int32), lane) - 1, -1)
    row = lax.broadcasted_iota(jnp.int32, a_t.shape, 0)
    pos = jnp.where(row % e_pad < n_experts, pos, -1)
    unused = jnp.full((LANES - e_pad, seq), -1.0, F32)
    for b in range(batch):
        pos_b = pos[b * e_pad:(b + 1) * e_pad]
        prow_ref[b] = pos_b
        pcol_ref[b * seq:(b + 1) * seq, :] = jnp.concatenate([pos_b.astype(F32), unused], axis=0).T


def route(aff, *, batch, seq, n_experts, capacity):
    e_pad = max(SUBLANES, n_experts)
    return pl.pallas_call(
        functools.partial(_route_kernel, n_experts=n_experts, capacity=capacity),
        grid=(1,),
        in_specs=[pl.BlockSpec((batch * seq, LANES), lambda i: (0, 0))],
        out_specs=[pl.BlockSpec((batch, e_pad, seq), lambda i: (0, 0, 0)),
                   pl.BlockSpec((batch, e_pad, seq), lambda i: (0, 0, 0)),
                   pl.BlockSpec((batch * seq, LANES), lambda i: (0, 0))],
        out_shape=[jax.ShapeDtypeStruct((batch, e_pad, seq), jnp.int32),
                   jax.ShapeDtypeStruct((batch, e_pad, seq), F32),
                   jax.ShapeDtypeStruct((batch * seq, LANES), F32)],
        compiler_params=_params(("arbitrary",), 8 * batch * seq * LANES * 4),
        name="route",
    )(aff)


GATHER_TILES = 4
COMBINE_TILES = 4


def _window_size(capacity, n_tiles):
    return capacity // n_tiles + 2 * _window_margin(capacity)


def _window_margin(capacity):
    return capacity // 8


def _window_start(tile, capacity, n_tiles, clip=jnp.clip):
    share = capacity // n_tiles
    unit = _window_margin(capacity)
    window = _window_size(capacity, n_tiles)
    assert share % unit == 0 and (capacity - window) % unit == 0
    return clip(tile * (share // unit) - 1, 0, (capacity - window) // unit) * unit


def _static_clip(v, lo, hi):
    return max(lo, min(v, hi))


def _slots_inside_windows(prow, *, n_experts, capacity, n_tiles):
    batch, _, seq = prow.shape
    pos = prow[:, :n_experts, :].reshape(batch, n_experts, n_tiles, seq // n_tiles)
    first = _window_start(jnp.arange(n_tiles, dtype=jnp.int32), capacity, n_tiles)[None, None, :]
    lowest = jnp.min(jnp.where(pos >= 0, pos, capacity), axis=-1)
    highest = jnp.max(pos, axis=-1)
    return (lowest >= first) & (highest < first + _window_size(capacity, n_tiles))


def _gather_kernel(ok_ref, h_ref, prow_ref, o_ref, *, experts_per_step, n_tiles):
    b = pl.program_id(0)
    g = pl.program_id(1)
    cap = o_ref.shape[1]
    seq = h_ref.shape[0]
    tt = seq // n_tiles
    window = _window_size(cap, n_tiles)
    starts = [_window_start(t, cap, n_tiles, _static_clip) for t in range(n_tiles)]
    unit = _window_margin(cap)
    for j in range(experts_per_step):
        expert = g * experts_per_step + j
        prow = prow_ref[0, pl.ds(expert, 1), :]
        in_windows = ok_ref[b, expert] != 0

        @pl.when(in_windows)
        def _(j=j, prow=prow):
            slot = lax.broadcasted_iota(jnp.int32, (window, tt), 0)
            parts = [jnp.dot((slot + starts[t] == prow[:, t * tt:(t + 1) * tt]).astype(BF16),
                             h_ref[t * tt:(t + 1) * tt, :], preferred_element_type=F32)
                     for t in range(n_tiles)]
            for r0 in range(0, cap, unit):
                covering = [parts[t][r0 - starts[t]:r0 - starts[t] + unit]
                            for t in range(n_tiles) if starts[t] <= r0 < starts[t] + window]
                o_ref[j, r0:r0 + unit, :] = functools.reduce(jnp.add, covering).astype(o_ref.dtype)

        @pl.when(jnp.logical_not(in_windows))
        def _(j=j, prow=prow):
            slot = lax.broadcasted_iota(jnp.int32, (cap, seq), 0)
            o_ref[j] = jnp.dot((slot == prow).astype(BF16), h_ref[...],
                               preferred_element_type=F32).astype(o_ref.dtype)


def gather_tokens(h, prow, *, batch, seq, n_experts, capacity):
    d = h.shape[1]
    e_pad = prow.shape[1]
    ng = _tile(n_experts, 8)
    ok = jnp.all(_slots_inside_windows(prow, n_experts=n_experts, capacity=capacity,
                                       n_tiles=GATHER_TILES), axis=-1).astype(jnp.int32)
    vmem = 2 * (seq * d * 2 + e_pad * seq * 4 + ng * capacity * d * 2) + 2 * capacity * seq * 8 \
        + 3 * capacity * d * 4
    return pl.pallas_call(
        functools.partial(_gather_kernel, experts_per_step=ng, n_tiles=GATHER_TILES),
        grid_spec=pltpu.PrefetchScalarGridSpec(
            num_scalar_prefetch=1,
            grid=(batch, n_experts // ng),
            in_specs=[pl.BlockSpec((seq, d), lambda b, g, ok_ref: (b, 0)),
                      pl.BlockSpec((1, e_pad, seq), lambda b, g, ok_ref: (b, 0, 0))],
            out_specs=pl.BlockSpec((ng, None, capacity, d), lambda b, g, ok_ref: (g, b, 0, 0))),
        out_shape=jax.ShapeDtypeStruct((n_experts, batch, capacity, d), BF16),
        compiler_params=_params(("parallel", "arbitrary"), vmem),
        name="gather_tokens",
    )(ok, h, prow)


def _ffn_kernel(x_ref, wg_ref, wu_ref, wd_ref, prow_ref, arow_ref, o_ref, acc_ref, gate_ref,
                *, capacity, gate_steps):
    e = pl.program_id(0)
    f = pl.program_id(1)
    nf = pl.num_programs(1)
    rows = x_ref.shape[0]
    n_seq = rows // capacity

    def chunk_product():
        x = x_ref[...]
        a = jnp.dot(x, wg_ref[...].astype(BF16), preferred_element_type=F32)
        u = jnp.dot(x, wu_ref[...].astype(BF16), preferred_element_type=F32)
        hidden = (a / (1.0 + jnp.exp(-a)) * u).astype(BF16)
        return jnp.dot(hidden, wd_ref[...].astype(BF16), preferred_element_type=F32)

    def gate_share():
        seq = prow_ref.shape[2]
        slot = lax.broadcasted_iota(jnp.int32, (capacity, seq), 0)
        per_step = -(-n_seq // gate_steps)
        for j in range(per_step):
            b = jnp.minimum(f * per_step + j, n_seq - 1)
            mine = slot == prow_ref[b, pl.ds(e, 1), :]
            gate_ref[pl.ds(pl.multiple_of(b * capacity, capacity), capacity), :] = jnp.sum(
                jnp.where(mine, arow_ref[b, pl.ds(e, 1), :], 0.0), axis=1, keepdims=True)

    @pl.when(f == 0)
    def _():
        gate_share()
        acc_ref[...] = chunk_product()

    @pl.when((f > 0) & (f < nf - 1))
    def _():
        gate_share()
        acc_ref[...] += chunk_product()

    @pl.when(f == nf - 1)
    def _():
        gate_share()
        o_ref[...] = ((acc_ref[...] + chunk_product()) * gate_ref[...]).astype(o_ref.dtype)


def expert_ffn(xs, w_gate, w_up, w_down, prow, arow, layer, *, rows_per_expert, capacity):
    _, n_experts, d, ff = w_gate.shape
    tf = _tile(ff // 2, 512)
    r = rows_per_expert
    batch, e_pad, seq = prow.shape
    vmem = 2 * (r * d * 2 + 3 * d * tf * 4 + r * d * 2 + 2 * batch * e_pad * seq * 4) + r * d * 4 \
        + r * LANES * 4 + 3 * d * tf * 2 + 4 * r * tf * 4 + r * d * 4 + 2 * capacity * seq * 4
    return pl.pallas_call(
        functools.partial(_ffn_kernel, capacity=capacity, gate_steps=ff // tf),
        grid=(n_experts, ff // tf),
        in_specs=[pl.BlockSpec((r, d), lambda e, f: (e, 0)),
                  pl.BlockSpec((None, None, d, tf), lambda e, f: (layer, e, 0, f)),
                  pl.BlockSpec((None, None, d, tf), lambda e, f: (layer, e, 0, f)),
                  pl.BlockSpec((None, None, tf, d), lambda e, f: (layer, e, f, 0)),
                  pl.BlockSpec((batch, e_pad, seq), lambda e, f: (0, 0, 0)),
                  pl.BlockSpec((batch, e_pad, seq), lambda e, f: (0, 0, 0))],
        out_specs=pl.BlockSpec((r, d), lambda e, f: (e, 0)),
        out_shape=jax.ShapeDtypeStruct((n_experts * r, d), BF16),
        scratch_shapes=[pltpu.VMEM((r, d), F32), pltpu.VMEM((r, 1), F32)],
        compiler_params=_params(("parallel", "arbitrary"), vmem),
        name="expert_ffn",
    )(xs, w_gate, w_up, w_down, prow, arow)


def _combine_kernel(ok_ref, x_ref, ywin_ref, yall_ref, pcol_ref, g_ref, out_ref, *, final_norm):
    b = pl.program_id(0)
    t = pl.program_id(1)
    g = pl.program_id(2)
    tt, d = x_ref.shape
    n_group, _, window, _ = ywin_ref.shape
    cap = yall_ref.shape[1]

    @pl.when(g == 0)
    def _():
        out_ref[...] = x_ref[...]

    lane = lax.broadcasted_iota(jnp.int32, (tt, LANES), 1)
    pcol = pcol_ref[...]

    def scatter(y_rows, n_slots, first_slot):
        slot = (lax.broadcasted_iota(jnp.int32, (tt, n_slots), 1) + first_slot).astype(F32)
        hots = []
        for j in range(n_group):
            slot_of_token = jnp.sum(jnp.where(lane == g * n_group + j, pcol, 0.0),
                                    axis=1, keepdims=True)
            hots.append((slot == slot_of_token).astype(BF16))
        out_ref[...] += jnp.dot(jnp.concatenate(hots, axis=1), y_rows, preferred_element_type=F32)

    in_window = ok_ref[b, t, g] != 0

    @pl.when(in_window)
    def _():
        scatter(ywin_ref[...].reshape(n_group * window, d), window,
                _window_start(t, cap, COMBINE_TILES))

    @pl.when(jnp.logical_not(in_window))
    def _():
        scatter(yall_ref[...].reshape(n_group * cap, d), cap, 0)

    if final_norm:
        @pl.when(g == pl.num_programs(2) - 1)
        def _():
            out_ref[...] = _rms(out_ref[...], g_ref[...])


def combine(x_res, y, prow, pcol, gain, *, batch, seq, n_experts, capacity, final_norm):
    d = x_res.shape[1]
    nt = COMBINE_TILES
    tt = seq // nt
    window = _window_size(capacity, nt)
    ng = _tile(n_experts, 8)
    n_groups = n_experts // ng
    y4 = y.reshape(n_experts, batch, capacity, d)
    inside = _slots_inside_windows(prow, n_experts=n_experts, capacity=capacity, n_tiles=nt)
    ok = jnp.all(inside.reshape(batch, n_groups, ng, nt), axis=2).transpose(0, 2, 1).astype(jnp.int32)
    vmem = 2 * (tt * d * 4 + ng * window * d * 2 + ng * capacity * d * 2 + tt * LANES * 4 + tt * d * 4) \
        + 3 * tt * d * 4 + 2 * tt * capacity * 4

    def full_block(b, t, g, ok_ref):
        need = ok_ref[b, t, g] == 0
        return (jnp.where(need, g, 0), jnp.where(need, b, 0), 0, 0)

    return pl.pallas_call(
        functools.partial(_combine_kernel, final_norm=final_norm),
        grid_spec=pltpu.PrefetchScalarGridSpec(
            num_scalar_prefetch=1,
            grid=(batch, nt, n_groups),
            in_specs=[pl.BlockSpec((tt, d), lambda b, t, g, ok_ref: (b * nt + t, 0)),
                      pl.BlockSpec((pl.Element(ng), pl.Element(1), pl.Element(window), pl.Element(d)),
                                   lambda b, t, g, ok_ref: (g * ng, b, _window_start(t, capacity, nt), 0)),
                      pl.BlockSpec((ng, None, capacity, d), full_block),
                      pl.BlockSpec((tt, LANES), lambda b, t, g, ok_ref: (b * nt + t, 0)),
                      pl.BlockSpec((1, d), lambda b, t, g, ok_ref: (0, 0))],
            out_specs=pl.BlockSpec((tt, d), lambda b, t, g, ok_ref: (b * nt + t, 0))),
        out_shape=jax.ShapeDtypeStruct((batch * seq, d), F32),
        compiler_params=_params(("parallel", "parallel", "arbitrary"), vmem),
        name="combine",
    )(ok, x_res, y4, y4, pcol, gain.reshape(1, d))


def _pool_kernel(x_ref, prev_ref, next_ref, ng_ref, w_ref, s_ref, g_ref, xo_ref, ho_ref, pad_ref,
                 *, seq, windows):
    i = pl.program_id(0)
    tm, d = x_ref.shape
    halo = prev_ref.shape[0]
    pg = w_ref.shape[1]
    tiles_per_seq = seq // tm
    tile_in_seq = i % tiles_per_seq
    norm_gain = ng_ref[...]
    h = _rms(x_ref[...], norm_gain)
    pad_ref[0:halo, :] = jnp.where(tile_in_seq == 0, 0.0, _rms(prev_ref[...], norm_gain))
    pad_ref[halo:halo + tm, :] = h
    pad_ref[halo + tm:2 * halo + tm, :] = jnp.where(tile_in_seq == tiles_per_seq - 1, 0.0,
                                                    _rms(next_ref[...], norm_gain))
    t = tile_in_seq * tm + lax.broadcasted_iota(jnp.int32, (tm, 1), 0)
    for gi, w in enumerate(windows):
        cols = slice(gi * pg, (gi + 1) * pg)
        acc = pad_ref[:, cols]
        span = 1
        while span < w:
            acc = acc + pltpu.roll(acc, acc.shape[0] - span, axis=0)
            span *= 2
        total = acc[halo - w // 2:halo - w // 2 + tm, :]
        count = (jnp.minimum(t + (w - w // 2), seq) - jnp.maximum(t - w // 2, 0)).astype(F32)
        pooled = (total / count - h[:, cols]).astype(BF16)
        mixed = jnp.dot(pooled, w_ref[gi], preferred_element_type=F32)
        xo_ref[:, cols] = mixed * s_ref[:, cols] + x_ref[:, cols]
    ho_ref[...] = _rms(xo_ref[...], g_ref[...]).astype(ho_ref.dtype)


def pool_mixer(x, norm_gain, w_pool, pool_scale, next_gain, *, seq):
    m, d = x.shape
    ng, pg, _ = w_pool.shape
    halo = SUBLANES
    assert all(w & (w - 1) == 0 and w // 2 <= halo for w in POOL_WINDOWS) and ng == len(POOL_WINDOWS)
    tm = _tile(seq, 512)
    per = tm // halo
    n_halo_blocks = m // halo
    vmem = 2 * (2 * tm * d * 4 + 2 * halo * d * 4 + ng * pg * pg * 2 + tm * d * 2) \
        + (tm + 2 * halo) * d * 4 + 6 * tm * d * 4
    return pl.pallas_call(
        functools.partial(_pool_kernel, seq=seq, windows=POOL_WINDOWS),
        grid=(m // tm,),
        in_specs=[pl.BlockSpec((tm, d), lambda i: (i, 0)),
                  pl.BlockSpec((halo, d), lambda i: (jnp.maximum(i * per - 1, 0), 0)),
                  pl.BlockSpec((halo, d), lambda i: (jnp.minimum((i + 1) * per, n_halo_blocks - 1), 0)),
                  pl.BlockSpec((1, d), lambda i: (0, 0)),
                  pl.BlockSpec((ng, pg, pg), lambda i: (0, 0, 0)),
                  pl.BlockSpec((1, d), lambda i: (0, 0)),
                  pl.BlockSpec((1, d), lambda i: (0, 0))],
        out_specs=[pl.BlockSpec((tm, d), lambda i: (i, 0)),
                   pl.BlockSpec((tm, d), lambda i: (i, 0))],
        out_shape=[jax.ShapeDtypeStruct((m, d), F32), jax.ShapeDtypeStruct((m, d), BF16)],
        scratch_shapes=[pltpu.VMEM((tm + 2 * halo, d), F32)],
        compiler_params=_params(("parallel",), vmem),
        name="pool_mixer",
    )(x, x, x, norm_gain.reshape(1, d), w_pool, pool_scale.reshape(1, d), next_gain.reshape(1, d))


def _rope_tables(seq):
    half = HEAD_DIM // 2
    rows = seq // GRID_W
    row_idx = jnp.repeat(jnp.arange(rows), GRID_W).astype(F32)
    col_idx = jnp.tile(jnp.arange(GRID_W), rows).astype(F32)
    inv_freq = 1.0 / (ROPE_THETA ** (jnp.arange(0, half, 2, dtype=F32) / half))
    ang = jnp.concatenate([row_idx[:, None] * inv_freq[None, :],
                           col_idx[:, None] * inv_freq[None, :]], axis=-1)
    cos, sin = jnp.cos(ang), jnp.sin(ang)
    zero = jnp.zeros_like(sin)
    cos_full = jnp.stack([cos, cos], axis=-1).reshape(seq, HEAD_DIM)
    sin_even = jnp.stack([-sin, zero], axis=-1).reshape(seq, HEAD_DIM)
    sin_odd = jnp.stack([zero, sin], axis=-1).reshape(seq, HEAD_DIM)
    return cos_full, sin_even, sin_odd


def _split_router(w_router):
    d, n_experts = w_router.shape
    w = jnp.pad(w_router, ((0, 0), (0, LANES - n_experts)))
    hi = w.astype(BF16)
    lo = (w - hi.astype(F32)).astype(BF16)
    return jnp.concatenate([hi, lo], axis=1), n_experts


def _moe(x_res, h, aff, w_gate, w_up, w_down, layer, gain, *, batch, seq, final_norm):
    n_experts = w_gate.shape[1]
    capacity = EC_CAPACITY_FACTOR * seq // n_experts
    d = h.shape[1]
    prow, arow, pcol = route(aff, batch=batch, seq=seq, n_experts=n_experts, capacity=capacity)
    xs = gather_tokens(h, prow, batch=batch, seq=seq, n_experts=n_experts, capacity=capacity)
    y = expert_ffn(xs.reshape(n_experts * batch * capacity, d), w_gate, w_up, w_down, prow, arow,
                   layer, rows_per_expert=batch * capacity, capacity=capacity)
    return combine(x_res, y, prow, pcol, gain, batch=batch, seq=seq,
                   n_experts=n_experts, capacity=capacity, final_norm=final_norm)


def _cross_block(x, h, mem2d, layer, cross_w_q, cross_w_k, cross_w_v, cross_w_o, mem_norm,
                 ffn_norm, router_w, *, batch, seq, mem_len):
    mem_n = rms_norm_rows(mem2d, mem_norm[layer], BF16)
    k = matmul_streamed(mem_n, cross_w_k, (layer,))
    v = matmul_streamed(mem_n, cross_w_v, (layer,))
    o = cross_attention(h, cast_weight(cross_w_q, (layer,)), k, v,
                        batch=batch, seq=seq, mem_len=mem_len)
    return matmul_residual_norm([o], cast_weight(cross_w_o, (layer,)), x, ffn_norm[layer],
                                router=_split_router(router_w[layer]))


def kernel(x, mem, mix_norm, attn_w_in, q_gain, k_gain, fourier_w, attn_w_out, pool_w, pool_scale,
           cross_norm, mem_norm, cross_w_q, cross_w_k, cross_w_v, cross_w_o, ffn_norm, router_w,
           expert_w_gate, expert_w_up, expert_w_down, final_norm):
    batch, seq, d = x.shape
    mem_len = mem.shape[1]
    depth = mix_norm.shape[0]
    attn_width = N_Q_HEADS * HEAD_DIM
    kv_width = N_KV_HEADS * HEAD_DIM
    fourier_width = N_FOURIER_GROUPS * FOURIER_GROUP
    rope_width = attn_width + kv_width
    cos_t, sin_a, sin_b = _rope_tables(seq)

    xf = x.reshape(batch * seq, d)
    mem2d = mem.reshape(batch * mem_len, d)
    cross = functools.partial(_cross_block, mem2d=mem2d, cross_w_q=cross_w_q, cross_w_k=cross_w_k,
                              cross_w_v=cross_w_v, cross_w_o=cross_w_o, mem_norm=mem_norm,
                              ffn_norm=ffn_norm, router_w=router_w,
                              batch=batch, seq=seq, mem_len=mem_len)
    for layer in range(depth):
        i = layer // 2
        if layer % 2 == 0:
            gain_cols = jnp.concatenate([jnp.tile(q_gain[i], N_Q_HEADS), jnp.tile(k_gain[i], N_KV_HEADS),
                                         jnp.ones((kv_width + fourier_width,), F32)]).reshape(1, -1)
            proj = proj_in(xf, mix_norm[layer], cast_weight(attn_w_in, (i,)), gain_cols,
                           cos_t, sin_a, sin_b, seq=seq, rope_width=rope_width)
            o_attn = gqa_attention(proj, batch=batch, seq=seq, attn_width=attn_width, kv_width=kv_width)
            o_four = fourier_mixer(proj, fourier_w[i], batch=batch, seq=seq,
                                   col0=attn_width + 2 * kv_width)
            xf, h = matmul_residual_norm([o_attn, o_four], cast_weight(attn_w_out, (i,)), xf,
                                         cross_norm[layer], tm_pref=1024)
        else:
            xf, h = pool_mixer(xf, mix_norm[layer], pool_w[i].astype(BF16), pool_scale[i],
                               cross_norm[layer], seq=seq)
        xf, h, aff = cross(xf, h, layer=layer)
        xf = _moe(xf, h, aff, expert_w_gate, expert_w_up, expert_w_down, layer, final_norm,
                  batch=batch, seq=seq, final_norm=layer == depth - 1)
    return xf.reshape(batch, seq, d)
```

```python
import functools
import math

import jax
import jax.numpy as jnp
from jax import lax
from jax.experimental import pallas as pl
from jax.experimental.pallas import tpu as pltpu

F32 = jnp.float32
BF16 = jnp.bfloat16

GRID_W = 64
HEAD_DIM = 128
N_Q_HEADS = 12
N_KV_HEADS = 4
N_FOURIER_GROUPS = 4
FOURIER_GROUP = 128
ROPE_THETA = 10000.0
POOL_WINDOWS = (2, 4, 8, 16)
N_MEM_HEADS = 4
EC_CAPACITY_FACTOR = 2
NORM_EPS = 1e-6

LANES = 128
SUBLANES = 8
V7X_VMEM_BYTES = 64 * 1024 * 1024
VMEM_HEADROOM_BYTES = 6 * 1024 * 1024
COL_CHUNK = 512
LOG2_E = math.log2(math.e)


def _params(semantics, vmem_bytes):
    limit = min(int(vmem_bytes) + VMEM_HEADROOM_BYTES, V7X_VMEM_BYTES - VMEM_HEADROOM_BYTES)
    return pltpu.CompilerParams(dimension_semantics=semantics, vmem_limit_bytes=limit)


def _tile(n, pref):
    t = min(n, pref)
    while n % t:
        t //= 2
    return t


def _rms(x, gain):
    ms = jnp.mean(x * x, axis=-1, keepdims=True)
    return x * lax.rsqrt(ms + NORM_EPS) * gain


def _softmax_numerators(scores, scale):
    c = scale * LOG2_E
    return [jnp.exp2((s - jnp.max(s, axis=-1, keepdims=True)) * c) for s in scores]


def _norm_kernel(x_ref, g_ref, o_ref):
    o_ref[...] = _rms(x_ref[...], g_ref[...]).astype(o_ref.dtype)


def rms_norm_rows(x, gain, out_dtype):
    m, d = x.shape
    tm = _tile(m, 512)
    return pl.pallas_call(
        _norm_kernel,
        grid=(m // tm,),
        in_specs=[pl.BlockSpec((tm, d), lambda i: (i, 0)),
                  pl.BlockSpec((1, d), lambda i: (0, 0))],
        out_specs=pl.BlockSpec((tm, d), lambda i: (i, 0)),
        out_shape=jax.ShapeDtypeStruct((m, d), out_dtype),
        compiler_params=_params(("parallel",), 2 * tm * d * 8),
        name="rms_norm_rows",
    )(x, gain.reshape(1, d))


def _cast_kernel(w_ref, o_ref):
    o_ref[...] = w_ref[...].astype(o_ref.dtype)


def cast_weight(w, lead):
    k, n = w.shape[-2:]
    tk = _tile(k, 1024)
    squeezed = (None,) * len(lead)
    return pl.pallas_call(
        _cast_kernel,
        grid=(k // tk,),
        in_specs=[pl.BlockSpec(squeezed + (tk, n), lambda i: tuple(lead) + (i, 0))],
        out_specs=pl.BlockSpec((tk, n), lambda i: (i, 0)),
        out_shape=jax.ShapeDtypeStruct((k, n), BF16),
        compiler_params=_params(("parallel",), 2 * tk * n * 6),
        name="cast_weight",
    )(w)


def _mm_stream_kernel(a_ref, w_ref, o_ref):
    o_ref[...] = jnp.dot(a_ref[...], w_ref[...].astype(BF16),
                         preferred_element_type=F32).astype(o_ref.dtype)


def matmul_streamed(a, w, lead, out_dtype=BF16):
    m, k = a.shape
    n = w.shape[-1]
    tn = _tile(n, 2 * COL_CHUNK)
    squeezed = (None,) * len(lead)
    vmem = 2 * (m * k * 2 + k * tn * 4 + m * tn * 2) + k * tn * 2 + m * tn * 4
    return pl.pallas_call(
        _mm_stream_kernel,
        grid=(n // tn,),
        in_specs=[pl.BlockSpec((m, k), lambda j: (0, 0)),
                  pl.BlockSpec(squeezed + (k, tn), lambda j: tuple(lead) + (0, j))],
        out_specs=pl.BlockSpec((m, tn), lambda j: (0, j)),
        out_shape=jax.ShapeDtypeStruct((m, n), out_dtype),
        compiler_params=_params(("parallel",), vmem),
        name="matmul_streamed",
    )(a, w)


def _proj_in_kernel(a_ref, ng_ref, w_ref, g_ref, cos_ref, sa_ref, sb_ref, o_ref, *, n_rope_heads):
    n = o_ref.shape[1]
    tn = _tile(n, COL_CHUNK)
    a = _rms(a_ref[...], ng_ref[...]).astype(BF16)
    cos, sin_a, sin_b = cos_ref[...], sa_ref[...], sb_ref[...]
    chunks = [jnp.dot(a, w_ref[:, c * tn:(c + 1) * tn], preferred_element_type=F32)
              for c in range(n // tn)]
    for c, acc in enumerate(chunks):
        for h in range(tn // HEAD_DIM):
            head = c * (tn // HEAD_DIM) + h
            col = slice(head * HEAD_DIM, (head + 1) * HEAD_DIM)
            xh = acc[:, h * HEAD_DIM:(h + 1) * HEAD_DIM]
            if head < n_rope_heads:
                ms = jnp.mean(xh * xh, axis=-1, keepdims=True)
                y = xh * lax.rsqrt(ms + NORM_EPS) * g_ref[:, col]
                xh = (y * cos + pltpu.roll(y, HEAD_DIM - 1, axis=1) * sin_a
                      + pltpu.roll(y, 1, axis=1) * sin_b)
            o_ref[:, col] = xh.astype(o_ref.dtype)


def proj_in(x, norm_gain, w_in, gain_cols, cos_t, sin_a, sin_b, *, seq, rope_width):
    m, k = x.shape
    n = w_in.shape[1]
    tm = _tile(seq, 512)
    n_pos_blocks = seq // tm
    vmem = 2 * (tm * k * 4 + k * n * 2 + tm * n * 2 + 3 * tm * HEAD_DIM * 4) + 6 * tm * COL_CHUNK * 4 \
        + 2 * tm * k * 4
    table = pl.BlockSpec((tm, HEAD_DIM), lambda i: (i % n_pos_blocks, 0))
    return pl.pallas_call(
        functools.partial(_proj_in_kernel, n_rope_heads=rope_width // HEAD_DIM),
        grid=(m // tm,),
        in_specs=[pl.BlockSpec((tm, k), lambda i: (i, 0)),
                  pl.BlockSpec((1, k), lambda i: (0, 0)),
                  pl.BlockSpec((k, n), lambda i: (0, 0)),
                  pl.BlockSpec((1, n), lambda i: (0, 0)),
                  table, table, table],
        out_specs=pl.BlockSpec((tm, n), lambda i: (i, 0)),
        out_shape=jax.ShapeDtypeStruct((m, n), BF16),
        compiler_params=_params(("parallel",), vmem),
        name="proj_in",
    )(x, norm_gain.reshape(1, k), w_in, gain_cols, cos_t, sin_a, sin_b)


def _attn_kernel(q_ref, k_ref, v_ref, o_ref, *, group, scale):
    k = k_ref[...]
    v = v_ref[...]
    v_ones = jnp.concatenate([v, jnp.ones_like(v)], axis=1)
    heads = [slice(g * HEAD_DIM, (g + 1) * HEAD_DIM) for g in range(group)]
    scores = [lax.dot_general(q_ref[:, sl], k, (((1,), (1,)), ((), ())),
                              preferred_element_type=F32) for sl in heads]
    for sl, e in zip(heads, _softmax_numerators(scores, scale)):
        ol = jnp.dot(e.astype(BF16), v_ones, preferred_element_type=F32)
        o_ref[:, sl] = (ol[:, :HEAD_DIM] / ol[:, HEAD_DIM:HEAD_DIM + 1]).astype(o_ref.dtype)


def gqa_attention(proj, *, batch, seq, attn_width, kv_width):
    group = N_Q_HEADS // N_KV_HEADS
    gw = group * HEAD_DIM
    tq = _tile(seq, 1024)
    nq = seq // tq
    k_col0 = attn_width // HEAD_DIM
    v_col0 = (attn_width + kv_width) // HEAD_DIM
    vmem = 2 * (tq * gw * 2 * 2 + 2 * seq * HEAD_DIM * 2) + group * tq * seq * 6
    return pl.pallas_call(
        functools.partial(_attn_kernel, group=group, scale=HEAD_DIM ** -0.5),
        grid=(batch, N_KV_HEADS, nq),
        in_specs=[pl.BlockSpec((tq, gw), lambda b, h, i: (b * nq + i, h)),
                  pl.BlockSpec((seq, HEAD_DIM), lambda b, h, i: (b, k_col0 + h)),
                  pl.BlockSpec((seq, HEAD_DIM), lambda b, h, i: (b, v_col0 + h))],
        out_specs=pl.BlockSpec((tq, gw), lambda b, h, i: (b * nq + i, h)),
        out_shape=jax.ShapeDtypeStruct((batch * seq, attn_width), BF16),
        compiler_params=_params(("parallel", "parallel", "parallel"), vmem),
        name="gqa_attention",
    )(proj, proj, proj)


def _fourier_kernel(f_ref, cc_ref, sc_ref, cs_ref, ss_ref, wf_ref, o_ref, ab_ref, *, seq, norm):
    r = pl.program_id(1)
    ng = wf_ref.shape[0]
    cg = wf_ref.shape[1]

    @pl.when(r == 0)
    def _():
        for g in range(ng):
            fg = f_ref[:, g * cg:(g + 1) * cg]
            ab_ref[0:seq, g * cg:(g + 1) * cg] = jnp.dot(
                fg, cc_ref[...], preferred_element_type=F32).astype(BF16)
            ab_ref[seq:2 * seq, g * cg:(g + 1) * cg] = jnp.dot(
                fg, sc_ref[...], preferred_element_type=F32).astype(BF16)

    z = (jnp.dot(cs_ref[...], ab_ref[0:seq, :], preferred_element_type=F32)
         - jnp.dot(ss_ref[...], ab_ref[seq:2 * seq, :], preferred_element_type=F32)) * norm
    for g in range(ng):
        zg = z[:, g * cg:(g + 1) * cg].astype(BF16)
        o_ref[:, g * cg:(g + 1) * cg] = jnp.dot(
            zg, wf_ref[g].astype(BF16), preferred_element_type=F32).astype(o_ref.dtype)


def _dft_cos_sin(n):
    lo = _tile(n, 64)
    hi = n // lo
    j = jnp.arange(n, dtype=jnp.int32)

    def table(k):
        ang = ((k[:, None] * j[None, :]) % n).astype(F32) * (2.0 * math.pi / n)
        return jnp.cos(ang), jnp.sin(ang)

    ca, sa = table(jnp.arange(hi, dtype=jnp.int32) * lo)
    cb, sb = table(jnp.arange(lo, dtype=jnp.int32))
    cos = ca[:, None, :] * cb[None, :, :] - sa[:, None, :] * sb[None, :, :]
    sin = sa[:, None, :] * cb[None, :, :] + ca[:, None, :] * sb[None, :, :]
    return cos.reshape(n, n).astype(BF16), sin.reshape(n, n).astype(BF16)


def fourier_mixer(proj, w_fourier, *, batch, seq, col0):
    ng, cg, _ = w_fourier.shape
    fw = ng * cg
    tr = _tile(seq, 1024)
    nr = seq // tr
    cc, sc = _dft_cos_sin(cg)
    cs, ss = _dft_cos_sin(seq)
    norm = 1.0 / math.sqrt(seq * cg)
    vmem = 2 * (seq * fw * 2 + 2 * cg * cg * 2 + 2 * tr * seq * 2 + ng * cg * cg * 4 + tr * fw * 2) \
        + 2 * seq * fw * 2 + 4 * tr * fw * 4
    return pl.pallas_call(
        functools.partial(_fourier_kernel, seq=seq, norm=norm),
        grid=(batch, nr),
        in_specs=[pl.BlockSpec((seq, fw), lambda b, r: (b, col0 // fw)),
                  pl.BlockSpec((cg, cg), lambda b, r: (0, 0)),
                  pl.BlockSpec((cg, cg), lambda b, r: (0, 0)),
                  pl.BlockSpec((tr, seq), lambda b, r: (r, 0)),
                  pl.BlockSpec((tr, seq), lambda b, r: (r, 0)),
                  pl.BlockSpec((ng, cg, cg), lambda b, r: (0, 0, 0))],
        out_specs=pl.BlockSpec((tr, fw), lambda b, r: (b * nr + r, 0)),
        out_shape=jax.ShapeDtypeStruct((batch * seq, fw), BF16),
        scratch_shapes=[pltpu.VMEM((2 * seq, fw), BF16)],
        compiler_params=_params(("parallel", "arbitrary"), vmem),
        name="fourier_mixer",
    )(proj, cc, sc, cs, ss, w_fourier)


def _router_affinity(hn, wr_ref, n_experts):
    hi = hn.astype(BF16)
    lo = (hn - hi.astype(F32)).astype(BF16)
    r_hi = jnp.dot(hi, wr_ref[...], preferred_element_type=F32)
    r_lo = jnp.dot(lo, wr_ref[...], preferred_element_type=F32)
    logits = r_hi[:, :LANES] + r_hi[:, LANES:] + r_lo[:, :LANES]
    lane = lax.broadcasted_iota(jnp.int32, logits.shape, 1)
    logits = jnp.where(lane < n_experts, logits, -jnp.inf)
    e = jnp.exp(logits - jnp.max(logits, axis=-1, keepdims=True))
    return e / jnp.sum(e, axis=-1, keepdims=True)


def _mm_res_norm_kernel(*refs, n_a, with_router, n_experts):
    a_refs = refs[:n_a]
    w_ref, x_ref, g_ref = refs[n_a:n_a + 3]
    pos = n_a + 3
    if with_router:
        wr_ref = refs[pos]
        pos += 1
    xo_ref, ho_ref = refs[pos:pos + 2]
    if with_router:
        aff_ref = refs[pos + 2]
    d = xo_ref.shape[1]
    tn = _tile(d, COL_CHUNK)
    for c in range(d // tn):
        sl = slice(c * tn, (c + 1) * tn)
        acc = x_ref[:, sl]
        row0 = 0
        for a_ref in a_refs:
            ka = a_ref.shape[1]
            acc = acc + jnp.dot(a_ref[...], w_ref[row0:row0 + ka, sl], preferred_element_type=F32)
            row0 += ka
        xo_ref[:, sl] = acc
    hn = _rms(xo_ref[...], g_ref[...])
    ho_ref[...] = hn.astype(ho_ref.dtype)
    if with_router:
        aff_ref[...] = _router_affinity(hn, wr_ref, n_experts)


def matmul_residual_norm(a_list, w, x_res, gain, *, router=None):
    m, d = x_res.shape
    k = w.shape[0]
    tm = _tile(m, 512)
    in_specs = [pl.BlockSpec((tm, a.shape[1]), lambda i: (i, 0)) for a in a_list]
    in_specs += [pl.BlockSpec((k, d), lambda i: (0, 0)),
                 pl.BlockSpec((tm, d), lambda i: (i, 0)),
                 pl.BlockSpec((1, d), lambda i: (0, 0))]
    args = list(a_list) + [w, x_res, gain.reshape(1, d)]
    out_specs = [pl.BlockSpec((tm, d), lambda i: (i, 0)),
                 pl.BlockSpec((tm, d), lambda i: (i, 0))]
    out_shape = [jax.ShapeDtypeStruct((m, d), F32), jax.ShapeDtypeStruct((m, d), BF16)]
    n_experts = 0
    if router is not None:
        wr, n_experts = router
        in_specs.append(pl.BlockSpec((d, 2 * LANES), lambda i: (0, 0)))
        args.append(wr)
        out_specs.append(pl.BlockSpec((tm, LANES), lambda i: (i, 0)))
        out_shape.append(jax.ShapeDtypeStruct((m, LANES), F32))
    vmem = 2 * (tm * k * 2 + k * d * 2 + tm * d * 4 + tm * d * 4 + tm * d * 2) \
        + 3 * tm * d * 4 + 4 * d * LANES * 2
    return pl.pallas_call(
        functools.partial(_mm_res_norm_kernel, n_a=len(a_list),
                          with_router=router is not None, n_experts=n_experts),
        grid=(m // tm,),
        in_specs=in_specs,
        out_specs=out_specs,
        out_shape=out_shape,
        compiler_params=_params(("parallel",), vmem),
        name="matmul_residual_norm",
    )(*args)


def _cross_block_kernel(h_ref, wq_ref, k_ref, v_ref, wo_ref, x_ref, g_ref, wr_ref,
                        xo_ref, ho_ref, aff_ref, o_ref, *, n_heads, scale, n_experts):
    dh = h_ref.shape[1] // n_heads
    h = h_ref[...]
    heads = [slice(i * dh, (i + 1) * dh) for i in range(n_heads)]
    qs = [jnp.dot(h, wq_ref[:, sl], preferred_element_type=F32).astype(BF16) for sl in heads]
    scores = [lax.dot_general(q, k_ref[:, sl], (((1,), (1,)), ((), ())),
                              preferred_element_type=F32) for q, sl in zip(qs, heads)]
    for sl, e in zip(heads, _softmax_numerators(scores, scale)):
        l = jnp.sum(e, axis=-1, keepdims=True)
        o = jnp.dot(e.astype(BF16), v_ref[:, sl], preferred_element_type=F32) / l
        o_ref[:, sl] = o.astype(o_ref.dtype)
    d = xo_ref.shape[1]
    tn = _tile(d, COL_CHUNK)
    o = o_ref[...]
    for c in range(d // tn):
        sl = slice(c * tn, (c + 1) * tn)
        xo_ref[:, sl] = x_ref[:, sl] + jnp.dot(o, wo_ref[:, sl], preferred_element_type=F32)
    hn = _rms(xo_ref[...], g_ref[...])
    ho_ref[...] = hn.astype(ho_ref.dtype)
    aff_ref[...] = _router_affinity(hn, wr_ref, n_experts)


def cross_block(h, w_q, k, v, w_o, x_res, gain, router, *, batch, seq, mem_len):
    m, d = x_res.shape
    wr, n_experts = router
    tq = _tile(seq, 512)
    nq = seq // tq
    once = pl.Buffered(1)
    vmem = 2 * (tq * d * 2 + 2 * mem_len * d * 2 + 2 * tq * d * 4 + tq * d * 2 + tq * LANES * 4) \
        + 2 * d * d * 2 + 2 * d * LANES * 2 + tq * d * 2 + 5 * tq * d * 4
    row = lambda b, i: (b * nq + i, 0)
    fixed = lambda b, i: (0, 0)
    return pl.pallas_call(
        functools.partial(_cross_block_kernel, n_heads=N_MEM_HEADS,
                          scale=(d // N_MEM_HEADS) ** -0.5, n_experts=n_experts),
        grid=(batch, nq),
        in_specs=[pl.BlockSpec((tq, d), row),
                  pl.BlockSpec((d, d), fixed, pipeline_mode=once),
                  pl.BlockSpec((mem_len, d), lambda b, i: (b, 0)),
                  pl.BlockSpec((mem_len, d), lambda b, i: (b, 0)),
                  pl.BlockSpec((d, d), fixed, pipeline_mode=once),
                  pl.BlockSpec((tq, d), row),
                  pl.BlockSpec((1, d), fixed),
                  pl.BlockSpec((d, 2 * LANES), fixed, pipeline_mode=once)],
        out_specs=[pl.BlockSpec((tq, d), row),
                   pl.BlockSpec((tq, d), row),
                   pl.BlockSpec((tq, LANES), row)],
        out_shape=[jax.ShapeDtypeStruct((m, d), F32), jax.ShapeDtypeStruct((m, d), BF16),
                   jax.ShapeDtypeStruct((m, LANES), F32)],
        scratch_shapes=[pltpu.VMEM((tq, d), BF16)],
        compiler_params=_params(("parallel", "parallel"), vmem),
        name="cross_block",
    )(h, w_q, k, v, w_o, x_res, gain.reshape(1, d), wr)


def _prefix_count(x, lane):
    n = x.shape[1]
    shift = 1
    while shift < n:
        x = x + jnp.where(lane >= shift, pltpu.roll(x, shift, axis=1), 0)
        shift *= 2
    return x


def _route_kernel(aff_ref, prow_ref, arow_ref, pcol_ref, *, n_experts, capacity):
    batch, e_pad, seq = prow_ref.shape
    a_t = jnp.concatenate([aff_ref[b * seq:(b + 1) * seq, :].T[:e_pad] for b in range(batch)],
                          axis=0)
    for b in range(batch):
        arow_ref[b] = a_t[b * e_pad:(b + 1) * e_pad]

    def body(i, thr_bits):
        cand = thr_bits | jnp.left_shift(jnp.int32(1), 30 - i)
        cnt = jnp.sum((a_t >= pltpu.bitcast(cand, F32)).astype(F32), axis=1, keepdims=True)
        return jnp.where(cnt >= capacity, cand, thr_bits)

    thr = pltpu.bitcast(lax.fori_loop(0, 31, body, jnp.zeros((batch * e_pad, 1), jnp.int32)), F32)
    lane = lax.broadcasted_iota(jnp.int32, a_t.shape, 1)
    gt = a_t > thr
    eq = a_t == thr
    need = capacity - jnp.sum(gt.astype(F32), axis=1, keepdims=True).astype(jnp.int32)
    sel = gt | (eq & (_prefix_count(eq.astype(jnp.int32), lane) <= need))
    pos = jnp.where(sel, _prefix_count(sel.astype(jnp.int32), lane) - 1, -1)
    row = lax.broadcasted_iota(jnp.int32, a_t.shape, 0)
    pos = jnp.where(row % e_pad < n_experts, pos, -1)
    unused = jnp.full((LANES - e_pad, seq), -1.0, F32)
    for b in range(batch):
        pos_b = pos[b * e_pad:(b + 1) * e_pad]
        prow_ref[b] = pos_b
        pcol_ref[b * seq:(b + 1) * seq, :] = jnp.concatenate([pos_b.astype(F32), unused], axis=0).T


def route(aff, *, batch, seq, n_experts, capacity):
    e_pad = max(SUBLANES, n_experts)
    return pl.pallas_call(
        functools.partial(_route_kernel, n_experts=n_experts, capacity=capacity),
        grid=(1,),
        in_specs=[pl.BlockSpec((batch * seq, LANES), lambda i: (0, 0))],
        out_specs=[pl.BlockSpec((batch, e_pad, seq), lambda i: (0, 0, 0)),
                   pl.BlockSpec((batch, e_pad, seq), lambda i: (0, 0, 0)),
                   pl.BlockSpec((batch * seq, LANES), lambda i: (0, 0))],
        out_shape=[jax.ShapeDtypeStruct((batch, e_pad, seq), jnp.int32),
                   jax.ShapeDtypeStruct((batch, e_pad, seq), F32),
                   jax.ShapeDtypeStruct((batch * seq, LANES), F32)],
        compiler_params=_params(("arbitrary",), 8 * batch * seq * LANES * 4),
        name="route",
    )(aff)


GATHER_TILES = 4
COMBINE_TILES = 4


def _window_size(capacity, n_tiles):
    return capacity // n_tiles + 2 * _window_margin(capacity)


def _window_margin(capacity):
    return capacity // 8


def _window_start(tile, capacity, n_tiles, clip=jnp.clip):
    share = capacity // n_tiles
    unit = _window_margin(capacity)
    window = _window_size(capacity, n_tiles)
    assert share % unit == 0 and (capacity - window) % unit == 0
    return clip(tile * (share // unit) - 1, 0, (capacity - window) // unit) * unit


def _static_clip(v, lo, hi):
    return max(lo, min(v, hi))


def _slots_inside_windows(prow, *, n_experts, capacity, n_tiles):
    batch, _, seq = prow.shape
    pos = prow[:, :n_experts, :].reshape(batch, n_experts, n_tiles, seq // n_tiles)
    first = _window_start(jnp.arange(n_tiles, dtype=jnp.int32), capacity, n_tiles)[None, None, :]
    lowest = jnp.min(jnp.where(pos >= 0, pos, capacity), axis=-1)
    highest = jnp.max(pos, axis=-1)
    return (lowest >= first) & (highest < first + _window_size(capacity, n_tiles))


def _gather_kernel(ok_ref, h_ref, prow_ref, o_ref, *, experts_per_step, n_tiles):
    b = pl.program_id(0)
    g = pl.program_id(1)
    cap = o_ref.shape[1]
    seq = h_ref.shape[0]
    tt = seq // n_tiles
    window = _window_size(cap, n_tiles)
    starts = [_window_start(t, cap, n_tiles, _static_clip) for t in range(n_tiles)]
    unit = _window_margin(cap)
    for j in range(experts_per_step):
        expert = g * experts_per_step + j
        prow = prow_ref[0, pl.ds(expert, 1), :]
        in_windows = ok_ref[b, expert] != 0

        @pl.when(in_windows)
        def _(j=j, prow=prow):
            slot = lax.broadcasted_iota(jnp.int32, (window, tt), 0)
            parts = [jnp.dot((slot + starts[t] == prow[:, t * tt:(t + 1) * tt]).astype(BF16),
                             h_ref[t * tt:(t + 1) * tt, :], preferred_element_type=F32)
                     for t in range(n_tiles)]
            for r0 in range(0, cap, unit):
                covering = [parts[t][r0 - starts[t]:r0 - starts[t] + unit]
                            for t in range(n_tiles) if starts[t] <= r0 < starts[t] + window]
                o_ref[j, r0:r0 + unit, :] = functools.reduce(jnp.add, covering).astype(o_ref.dtype)

        @pl.when(jnp.logical_not(in_windows))
        def _(j=j, prow=prow):
            slot = lax.broadcasted_iota(jnp.int32, (cap, seq), 0)
            o_ref[j] = jnp.dot((slot == prow).astype(BF16), h_ref[...],
                               preferred_element_type=F32).astype(o_ref.dtype)


def gather_tokens(h, prow, *, batch, seq, n_experts, capacity):
    d = h.shape[1]
    e_pad = prow.shape[1]
    ng = _tile(n_experts, 4)
    ok = jnp.all(_slots_inside_windows(prow, n_experts=n_experts, capacity=capacity,
                                       n_tiles=GATHER_TILES), axis=-1).astype(jnp.int32)
    vmem = 2 * (seq * d * 2 + e_pad * seq * 4 + ng * capacity * d * 2) + 2 * capacity * seq * 8 \
        + 3 * capacity * d * 4
    return pl.pallas_call(
        functools.partial(_gather_kernel, experts_per_step=ng, n_tiles=GATHER_TILES),
        grid_spec=pltpu.PrefetchScalarGridSpec(
            num_scalar_prefetch=1,
            grid=(batch, n_experts // ng),
            in_specs=[pl.BlockSpec((seq, d), lambda b, g, ok_ref: (b, 0)),
                      pl.BlockSpec((1, e_pad, seq), lambda b, g, ok_ref: (b, 0, 0))],
            out_specs=pl.BlockSpec((ng, None, capacity, d), lambda b, g, ok_ref: (g, b, 0, 0))),
        out_shape=jax.ShapeDtypeStruct((n_experts, batch, capacity, d), BF16),
        compiler_params=_params(("parallel", "arbitrary"), vmem),
        name="gather_tokens",
    )(ok, h, prow)


def _ffn_kernel(x_ref, wg_ref, wu_ref, wd_ref, prow_ref, arow_ref, o_ref, acc_ref, gate_ref,
                *, capacity, gate_steps):
    e = pl.program_id(0)
    f = pl.program_id(1)
    nf = pl.num_programs(1)
    rows = x_ref.shape[0]
    n_seq = rows // capacity

    def chunk_product():
        x = x_ref[...]
        a = jnp.dot(x, wg_ref[...].astype(BF16), preferred_element_type=F32)
        u = jnp.dot(x, wu_ref[...].astype(BF16), preferred_element_type=F32)
        hidden = (a / (1.0 + jnp.exp(-a)) * u).astype(BF16)
        return jnp.dot(hidden, wd_ref[...].astype(BF16), preferred_element_type=F32)

    def gate_share():
        seq = prow_ref.shape[2]
        slot = lax.broadcasted_iota(jnp.int32, (capacity, seq), 0)
        per_step = -(-n_seq // gate_steps)
        for j in range(per_step):
            b = jnp.minimum(f * per_step + j, n_seq - 1)
            mine = slot == prow_ref[b, pl.ds(e, 1), :]
            gate_ref[pl.ds(pl.multiple_of(b * capacity, capacity), capacity), :] = jnp.sum(
                jnp.where(mine, arow_ref[b, pl.ds(e, 1), :], 0.0), axis=1, keepdims=True)

    @pl.when(f == 0)
    def _():
        gate_share()
        acc_ref[...] = chunk_product()

    @pl.when((f > 0) & (f < nf - 1))
    def _():
        gate_share()
        acc_ref[...] += chunk_product()

    @pl.when(f == nf - 1)
    def _():
        gate_share()
        o_ref[...] = ((acc_ref[...] + chunk_product()) * gate_ref[...]).astype(o_ref.dtype)


def expert_ffn(xs, w_gate, w_up, w_down, prow, arow, layer, *, rows_per_expert, capacity):
    _, n_experts, d, ff = w_gate.shape
    tf = _tile(ff // 2, 512)
    r = rows_per_expert
    batch, e_pad, seq = prow.shape
    vmem = 2 * (r * d * 2 + 3 * d * tf * 4 + r * d * 2 + 2 * batch * e_pad * seq * 4) + r * d * 4 \
        + r * LANES * 4 + 3 * d * tf * 2 + 4 * r * tf * 4 + r * d * 4 + 2 * capacity * seq * 4
    return pl.pallas_call(
        functools.partial(_ffn_kernel, capacity=capacity, gate_steps=ff // tf),
        grid=(n_experts, ff // tf),
        in_specs=[pl.BlockSpec((r, d), lambda e, f: (e, 0)),
                  pl.BlockSpec((None, None, d, tf), lambda e, f: (layer, e, 0, f)),
                  pl.BlockSpec((None, None, d, tf), lambda e, f: (layer, e, 0, f)),
                  pl.BlockSpec((None, None, tf, d), lambda e, f: (layer, e, f, 0)),
                  pl.BlockSpec((batch, e_pad, seq), lambda e, f: (0, 0, 0)),
                  pl.BlockSpec((batch, e_pad, seq), lambda e, f: (0, 0, 0))],
        out_specs=pl.BlockSpec((r, d), lambda e, f: (e, 0)),
        out_shape=jax.ShapeDtypeStruct((n_experts * r, d), BF16),
        scratch_shapes=[pltpu.VMEM((r, d), F32), pltpu.VMEM((r, 1), F32)],
        compiler_params=_params(("parallel", "arbitrary"), vmem),
        name="expert_ffn",
    )(xs, w_gate, w_up, w_down, prow, arow)


def _combine_kernel(ok_ref, x_ref, ywin_ref, yall_ref, pcol_ref, g_ref, out_ref, *, final_norm):
    b = pl.program_id(0)
    t = pl.program_id(1)
    g = pl.program_id(2)
    tt, d = x_ref.shape
    n_group, _, window, _ = ywin_ref.shape
    cap = yall_ref.shape[1]

    @pl.when(g == 0)
    def _():
        out_ref[...] = x_ref[...]

    lane = lax.broadcasted_iota(jnp.int32, (tt, LANES), 1)
    pcol = pcol_ref[...]

    def scatter(y_rows, n_slots, first_slot):
        slot = (lax.broadcasted_iota(jnp.int32, (tt, n_slots), 1) + first_slot).astype(F32)
        hots = []
        for j in range(n_group):
            slot_of_token = jnp.sum(jnp.where(lane == g * n_group + j, pcol, 0.0),
                                    axis=1, keepdims=True)
            hots.append((slot == slot_of_token).astype(BF16))
        out_ref[...] += jnp.dot(jnp.concatenate(hots, axis=1), y_rows, preferred_element_type=F32)

    in_window = ok_ref[b, t, g] != 0

    @pl.when(in_window)
    def _():
        scatter(ywin_ref[...].reshape(n_group * window, d), window,
                _window_start(t, cap, COMBINE_TILES))

    @pl.when(jnp.logical_not(in_window))
    def _():
        scatter(yall_ref[...].reshape(n_group * cap, d), cap, 0)

    if final_norm:
        @pl.when(g == pl.num_programs(2) - 1)
        def _():
            out_ref[...] = _rms(out_ref[...], g_ref[...])


def combine(x_res, y, prow, pcol, gain, *, batch, seq, n_experts, capacity, final_norm):
    d = x_res.shape[1]
    nt = COMBINE_TILES
    tt = seq // nt
    window = _window_size(capacity, nt)
    ng = _tile(n_experts, 8)
    n_groups = n_experts // ng
    y4 = y.reshape(n_experts, batch, capacity, d)
    inside = _slots_inside_windows(prow, n_experts=n_experts, capacity=capacity, n_tiles=nt)
    ok = jnp.all(inside.reshape(batch, n_groups, ng, nt), axis=2).transpose(0, 2, 1).astype(jnp.int32)
    vmem = 2 * (tt * d * 4 + ng * window * d * 2 + ng * capacity * d * 2 + tt * LANES * 4 + tt * d * 4) \
        + 3 * tt * d * 4 + 2 * tt * capacity * 4

    def full_block(b, t, g, ok_ref):
        need = ok_ref[b, t, g] == 0
        return (jnp.where(need, g, 0), jnp.where(need, b, 0), 0, 0)

    return pl.pallas_call(
        functools.partial(_combine_kernel, final_norm=final_norm),
        grid_spec=pltpu.PrefetchScalarGridSpec(
            num_scalar_prefetch=1,
            grid=(batch, nt, n_groups),
            in_specs=[pl.BlockSpec((tt, d), lambda b, t, g, ok_ref: (b * nt + t, 0)),
                      pl.BlockSpec((pl.Element(ng), pl.Element(1), pl.Element(window), pl.Element(d)),
                                   lambda b, t, g, ok_ref: (g * ng, b, _window_start(t, capacity, nt), 0)),
                      pl.BlockSpec((ng, None, capacity, d), full_block),
                      pl.BlockSpec((tt, LANES), lambda b, t, g, ok_ref: (b * nt + t, 0)),
                      pl.BlockSpec((1, d), lambda b, t, g, ok_ref: (0, 0))],
            out_specs=pl.BlockSpec((tt, d), lambda b, t, g, ok_ref: (b * nt + t, 0))),
        out_shape=jax.ShapeDtypeStruct((batch * seq, d), F32),
        compiler_params=_params(("parallel", "parallel", "arbitrary"), vmem),
        name="combine",
    )(ok, x_res, y4, y4, pcol, gain.reshape(1, d))


def _pool_kernel(x_ref, prev_ref, next_ref, ng_ref, w_ref, s_ref, g_ref, xo_ref, ho_ref, pad_ref,
                 *, seq, windows):
    i = pl.program_id(0)
    tm, d = x_ref.shape
    halo = prev_ref.shape[0]
    pg = w_ref.shape[1]
    tiles_per_seq = seq // tm
    tile_in_seq = i % tiles_per_seq
    norm_gain = ng_ref[...]
    h = _rms(x_ref[...], norm_gain)
    pad_ref[0:halo, :] = jnp.where(tile_in_seq == 0, 0.0, _rms(prev_ref[...], norm_gain))
    pad_ref[halo:halo + tm, :] = h
    pad_ref[halo + tm:2 * halo + tm, :] = jnp.where(tile_in_seq == tiles_per_seq - 1, 0.0,
                                                    _rms(next_ref[...], norm_gain))
    t = tile_in_seq * tm + lax.broadcasted_iota(jnp.int32, (tm, 1), 0)
    for gi, w in enumerate(windows):
        cols = slice(gi * pg, (gi + 1) * pg)
        acc = pad_ref[:, cols]
        span = 1
        while span < w:
            acc = acc + pltpu.roll(acc, acc.shape[0] - span, axis=0)
            span *= 2
        total = acc[halo - w // 2:halo - w // 2 + tm, :]
        count = (jnp.minimum(t + (w - w // 2), seq) - jnp.maximum(t - w // 2, 0)).astype(F32)
        pooled = (total / count - h[:, cols]).astype(BF16)
        mixed = jnp.dot(pooled, w_ref[gi], preferred_element_type=F32)
        xo_ref[:, cols] = mixed * s_ref[:, cols] + x_ref[:, cols]
    ho_ref[...] = _rms(xo_ref[...], g_ref[...]).astype(ho_ref.dtype)


def pool_mixer(x, norm_gain, w_pool, pool_scale, next_gain, *, seq):
    m, d = x.shape
    ng, pg, _ = w_pool.shape
    halo = SUBLANES
    assert all(w & (w - 1) == 0 and w // 2 <= halo for w in POOL_WINDOWS) and ng == len(POOL_WINDOWS)
    tm = _tile(seq, 1024)
    per = tm // halo
    n_halo_blocks = m // halo
    vmem = 2 * (2 * tm * d * 4 + 2 * halo * d * 4 + ng * pg * pg * 2 + tm * d * 2) \
        + (tm + 2 * halo) * d * 4 + 6 * tm * d * 4
    return pl.pallas_call(
        functools.partial(_pool_kernel, seq=seq, windows=POOL_WINDOWS),
        grid=(m // tm,),
        in_specs=[pl.BlockSpec((tm, d), lambda i: (i, 0)),
                  pl.BlockSpec((halo, d), lambda i: (jnp.maximum(i * per - 1, 0), 0)),
                  pl.BlockSpec((halo, d), lambda i: (jnp.minimum((i + 1) * per, n_halo_blocks - 1), 0)),
                  pl.BlockSpec((1, d), lambda i: (0, 0)),
                  pl.BlockSpec((ng, pg, pg), lambda i: (0, 0, 0)),
                  pl.BlockSpec((1, d), lambda i: (0, 0)),
                  pl.BlockSpec((1, d), lambda i: (0, 0))],
        out_specs=[pl.BlockSpec((tm, d), lambda i: (i, 0)),
                   pl.BlockSpec((tm, d), lambda i: (i, 0))],
        out_shape=[jax.ShapeDtypeStruct((m, d), F32), jax.ShapeDtypeStruct((m, d), BF16)],
        scratch_shapes=[pltpu.VMEM((tm + 2 * halo, d), F32)],
        compiler_params=_params(("parallel",), vmem),
        name="pool_mixer",
    )(x, x, x, norm_gain.reshape(1, d), w_pool, pool_scale.reshape(1, d), next_gain.reshape(1, d))


def _rope_tables(seq):
    half = HEAD_DIM // 2
    rows = seq // GRID_W
    row_idx = jnp.repeat(jnp.arange(rows), GRID_W).astype(F32)
    col_idx = jnp.tile(jnp.arange(GRID_W), rows).astype(F32)
    inv_freq = 1.0 / (ROPE_THETA ** (jnp.arange(0, half, 2, dtype=F32) / half))
    ang = jnp.concatenate([row_idx[:, None] * inv_freq[None, :],
                           col_idx[:, None] * inv_freq[None, :]], axis=-1)
    cos, sin = jnp.cos(ang), jnp.sin(ang)
    zero = jnp.zeros_like(sin)
    cos_full = jnp.stack([cos, cos], axis=-1).reshape(seq, HEAD_DIM)
    sin_even = jnp.stack([-sin, zero], axis=-1).reshape(seq, HEAD_DIM)
    sin_odd = jnp.stack([zero, sin], axis=-1).reshape(seq, HEAD_DIM)
    return cos_full, sin_even, sin_odd


def _split_router(w_router):
    d, n_experts = w_router.shape
    w = jnp.pad(w_router, ((0, 0), (0, LANES - n_experts)))
    hi = w.astype(BF16)
    lo = (w - hi.astype(F32)).astype(BF16)
    return jnp.concatenate([hi, lo], axis=1), n_experts


def _moe(x_res, h, aff, w_gate, w_up, w_down, layer, gain, *, batch, seq, final_norm):
    n_experts = w_gate.shape[1]
    capacity = EC_CAPACITY_FACTOR * seq // n_experts
    d = h.shape[1]
    prow, arow, pcol = route(aff, batch=batch, seq=seq, n_experts=n_experts, capacity=capacity)
    xs = gather_tokens(h, prow, batch=batch, seq=seq, n_experts=n_experts, capacity=capacity)
    y = expert_ffn(xs.reshape(n_experts * batch * capacity, d), w_gate, w_up, w_down, prow, arow,
                   layer, rows_per_expert=batch * capacity, capacity=capacity)
    return combine(x_res, y, prow, pcol, gain, batch=batch, seq=seq,
                   n_experts=n_experts, capacity=capacity, final_norm=final_norm)


def _cross_block(x, h, mem2d, layer, cross_w_q, cross_w_k, cross_w_v, cross_w_o, mem_norm,
                 ffn_norm, router_w, *, batch, seq, mem_len):
    mem_n = rms_norm_rows(mem2d, mem_norm[layer], BF16)
    k = matmul_streamed(mem_n, cross_w_k, (layer,))
    v = matmul_streamed(mem_n, cross_w_v, (layer,))
    return cross_block(h, cast_weight(cross_w_q, (layer,)), k, v, cast_weight(cross_w_o, (layer,)),
                       x, ffn_norm[layer], _split_router(router_w[layer]),
                       batch=batch, seq=seq, mem_len=mem_len)


def kernel(x, mem, mix_norm, attn_w_in, q_gain, k_gain, fourier_w, attn_w_out, pool_w, pool_scale,
           cross_norm, mem_norm, cross_w_q, cross_w_k, cross_w_v, cross_w_o, ffn_norm, router_w,
           expert_w_gate, expert_w_up, expert_w_down, final_norm):
    batch, seq, d = x.shape
    mem_len = mem.shape[1]
    depth = mix_norm.shape[0]
    attn_width = N_Q_HEADS * HEAD_DIM
    kv_width = N_KV_HEADS * HEAD_DIM
    fourier_width = N_FOURIER_GROUPS * FOURIER_GROUP
    rope_width = attn_width + kv_width
    cos_t, sin_a, sin_b = _rope_tables(seq)

    xf = x.reshape(batch * seq, d)
    mem2d = mem.reshape(batch * mem_len, d)
    cross = functools.partial(_cross_block, mem2d=mem2d, cross_w_q=cross_w_q, cross_w_k=cross_w_k,
                              cross_w_v=cross_w_v, cross_w_o=cross_w_o, mem_norm=mem_norm,
                              ffn_norm=ffn_norm, router_w=router_w,
                              batch=batch, seq=seq, mem_len=mem_len)
    for layer in range(depth):
        i = layer // 2
        if layer % 2 == 0:
            gain_cols = jnp.concatenate([jnp.tile(q_gain[i], N_Q_HEADS), jnp.tile(k_gain[i], N_KV_HEADS),
                                         jnp.ones((kv_width + fourier_width,), F32)]).reshape(1, -1)
            proj = proj_in(xf, mix_norm[layer], cast_weight(attn_w_in, (i,)), gain_cols,
                           cos_t, sin_a, sin_b, seq=seq, rope_width=rope_width)
            o_attn = gqa_attention(proj, batch=batch, seq=seq, attn_width=attn_width, kv_width=kv_width)
            o_four = fourier_mixer(proj, fourier_w[i], batch=batch, seq=seq,
                                   col0=attn_width + 2 * kv_width)
            xf, h = matmul_residual_norm([o_attn, o_four], cast_weight(attn_w_out, (i,)), xf,
                                         cross_norm[layer])
        else:
            xf, h = pool_mixer(xf, mix_norm[layer], pool_w[i].astype(BF16), pool_scale[i],
                               cross_norm[layer], seq=seq)
        xf, h, aff = cross(xf, h, layer=layer)
        xf = _moe(xf, h, aff, expert_w_gate, expert_w_up, expert_w_down, layer, final_norm,
                  batch=batch, seq=seq, final_norm=layer == depth - 1)
    return xf.reshape(batch, seq, d)
```

```python
import functools
import math

import jax
import jax.numpy as jnp
from jax import lax
from jax.experimental import pallas as pl
from jax.experimental.pallas import tpu as pltpu

F32 = jnp.float32
BF16 = jnp.bfloat16

GRID_W = 64
HEAD_DIM = 128
N_Q_HEADS = 12
N_KV_HEADS = 4
N_FOURIER_GROUPS = 4
FOURIER_GROUP = 128
ROPE_THETA = 10000.0
POOL_WINDOWS = (2, 4, 8, 16)
N_MEM_HEADS = 4
EC_CAPACITY_FACTOR = 2
NORM_EPS = 1e-6

LANES = 128
SUBLANES = 8
V7X_VMEM_BYTES = 64 * 1024 * 1024
VMEM_HEADROOM_BYTES = 6 * 1024 * 1024
COL_CHUNK = 512
LOG2_E = math.log2(math.e)


def _params(semantics, vmem_bytes):
    del vmem_bytes
    return pltpu.CompilerParams(dimension_semantics=semantics,
                                vmem_limit_bytes=V7X_VMEM_BYTES - VMEM_HEADROOM_BYTES)


def _tile(n, pref):
    t = min(n, pref)
    while n % t:
        t //= 2
    return t


def _rms(x, gain):
    ms = jnp.mean(x * x, axis=-1, keepdims=True)
    return x * lax.rsqrt(ms + NORM_EPS) * gain


def _softmax_numerators(scores, scale):
    c = scale * LOG2_E
    return [jnp.exp2((s - jnp.max(s, axis=-1, keepdims=True)) * c) for s in scores]


def _norm_kernel(x_ref, g_ref, o_ref):
    o_ref[...] = _rms(x_ref[...], g_ref[...]).astype(o_ref.dtype)


def rms_norm_rows(x, gain, out_dtype):
    m, d = x.shape
    tm = _tile(m, 512)
    return pl.pallas_call(
        _norm_kernel,
        grid=(m // tm,),
        in_specs=[pl.BlockSpec((tm, d), lambda i: (i, 0)),
                  pl.BlockSpec((1, d), lambda i: (0, 0))],
        out_specs=pl.BlockSpec((tm, d), lambda i: (i, 0)),
        out_shape=jax.ShapeDtypeStruct((m, d), out_dtype),
        compiler_params=_params(("parallel",), 2 * tm * d * 8),
        name="rms_norm_rows",
    )(x, gain.reshape(1, d))


def _cast_kernel(w_ref, o_ref):
    o_ref[...] = w_ref[...].astype(o_ref.dtype)


def cast_weight(w, lead):
    k, n = w.shape[-2:]
    tk = _tile(k, 1024)
    squeezed = (None,) * len(lead)
    return pl.pallas_call(
        _cast_kernel,
        grid=(k // tk,),
        in_specs=[pl.BlockSpec(squeezed + (tk, n), lambda i: tuple(lead) + (i, 0))],
        out_specs=pl.BlockSpec((tk, n), lambda i: (i, 0)),
        out_shape=jax.ShapeDtypeStruct((k, n), BF16),
        compiler_params=_params(("parallel",), 2 * tk * n * 6),
        name="cast_weight",
    )(w)


def _mm_stream_kernel(a_ref, w_ref, o_ref):
    o_ref[...] = jnp.dot(a_ref[...], w_ref[...].astype(BF16),
                         preferred_element_type=F32).astype(o_ref.dtype)


def matmul_streamed(a, w, lead, out_dtype=BF16):
    m, k = a.shape
    n = w.shape[-1]
    tn = _tile(n, 2 * COL_CHUNK)
    squeezed = (None,) * len(lead)
    vmem = 2 * (m * k * 2 + k * tn * 4 + m * tn * 2) + k * tn * 2 + m * tn * 4
    return pl.pallas_call(
        _mm_stream_kernel,
        grid=(n // tn,),
        in_specs=[pl.BlockSpec((m, k), lambda j: (0, 0)),
                  pl.BlockSpec(squeezed + (k, tn), lambda j: tuple(lead) + (0, j))],
        out_specs=pl.BlockSpec((m, tn), lambda j: (0, j)),
        out_shape=jax.ShapeDtypeStruct((m, n), out_dtype),
        compiler_params=_params(("parallel",), vmem),
        name="matmul_streamed",
    )(a, w)


def _proj_in_kernel(a_ref, ng_ref, w_ref, g_ref, cos_ref, sa_ref, sb_ref, o_ref, *, n_rope_heads):
    n = o_ref.shape[1]
    tn = _tile(n, COL_CHUNK)
    a = _rms(a_ref[...], ng_ref[...]).astype(BF16)
    cos, sin_a, sin_b = cos_ref[...], sa_ref[...], sb_ref[...]
    chunks = [jnp.dot(a, w_ref[:, c * tn:(c + 1) * tn], preferred_element_type=F32)
              for c in range(n // tn)]
    for c, acc in enumerate(chunks):
        for h in range(tn // HEAD_DIM):
            head = c * (tn // HEAD_DIM) + h
            col = slice(head * HEAD_DIM, (head + 1) * HEAD_DIM)
            xh = acc[:, h * HEAD_DIM:(h + 1) * HEAD_DIM]
            if head < n_rope_heads:
                ms = jnp.mean(xh * xh, axis=-1, keepdims=True)
                y = xh * lax.rsqrt(ms + NORM_EPS) * g_ref[:, col]
                xh = (y * cos + pltpu.roll(y, HEAD_DIM - 1, axis=1) * sin_a
                      + pltpu.roll(y, 1, axis=1) * sin_b)
            o_ref[:, col] = xh.astype(o_ref.dtype)


def proj_in(x, norm_gain, w_in, gain_cols, cos_t, sin_a, sin_b, *, seq, rope_width):
    m, k = x.shape
    n = w_in.shape[1]
    tm = _tile(seq, 512)
    n_pos_blocks = seq // tm
    vmem = 2 * (tm * k * 4 + k * n * 2 + tm * n * 2 + 3 * tm * HEAD_DIM * 4) + 6 * tm * COL_CHUNK * 4 \
        + 2 * tm * k * 4
    table = pl.BlockSpec((tm, HEAD_DIM), lambda i: (i % n_pos_blocks, 0))
    return pl.pallas_call(
        functools.partial(_proj_in_kernel, n_rope_heads=rope_width // HEAD_DIM),
        grid=(m // tm,),
        in_specs=[pl.BlockSpec((tm, k), lambda i: (i, 0)),
                  pl.BlockSpec((1, k), lambda i: (0, 0)),
                  pl.BlockSpec((k, n), lambda i: (0, 0)),
                  pl.BlockSpec((1, n), lambda i: (0, 0)),
                  table, table, table],
        out_specs=pl.BlockSpec((tm, n), lambda i: (i, 0)),
        out_shape=jax.ShapeDtypeStruct((m, n), BF16),
        compiler_params=_params(("parallel",), vmem),
        name="proj_in",
    )(x, norm_gain.reshape(1, k), w_in, gain_cols, cos_t, sin_a, sin_b)


def _attn_kernel(q_ref, k_ref, v_ref, o_ref, *, group, scale):
    k = k_ref[...]
    v = v_ref[...]
    v_ones = jnp.concatenate([v, jnp.ones_like(v)], axis=1)
    heads = [slice(g * HEAD_DIM, (g + 1) * HEAD_DIM) for g in range(group)]
    scores = [lax.dot_general(q_ref[:, sl], k, (((1,), (1,)), ((), ())),
                              preferred_element_type=F32) for sl in heads]
    for sl, e in zip(heads, _softmax_numerators(scores, scale)):
        ol = jnp.dot(e.astype(BF16), v_ones, preferred_element_type=F32)
        o_ref[:, sl] = (ol[:, :HEAD_DIM] / ol[:, HEAD_DIM:HEAD_DIM + 1]).astype(o_ref.dtype)


def gqa_attention(proj, *, batch, seq, attn_width, kv_width):
    group = N_Q_HEADS // N_KV_HEADS
    gw = group * HEAD_DIM
    tq = _tile(seq, 1024)
    nq = seq // tq
    k_col0 = attn_width // HEAD_DIM
    v_col0 = (attn_width + kv_width) // HEAD_DIM
    vmem = 2 * (tq * gw * 2 * 2 + 2 * seq * HEAD_DIM * 2) + group * tq * seq * 6
    return pl.pallas_call(
        functools.partial(_attn_kernel, group=group, scale=HEAD_DIM ** -0.5),
        grid=(batch, N_KV_HEADS, nq),
        in_specs=[pl.BlockSpec((tq, gw), lambda b, h, i: (b * nq + i, h)),
                  pl.BlockSpec((seq, HEAD_DIM), lambda b, h, i: (b, k_col0 + h)),
                  pl.BlockSpec((seq, HEAD_DIM), lambda b, h, i: (b, v_col0 + h))],
        out_specs=pl.BlockSpec((tq, gw), lambda b, h, i: (b * nq + i, h)),
        out_shape=jax.ShapeDtypeStruct((batch * seq, attn_width), BF16),
        compiler_params=_params(("parallel", "parallel", "parallel"), vmem),
        name="gqa_attention",
    )(proj, proj, proj)


def _fourier_kernel(f_ref, cc_ref, sc_ref, cs_ref, ss_ref, wf_ref, o_ref, ab_ref, *, seq, norm):
    r = pl.program_id(1)
    ng = wf_ref.shape[0]
    cg = wf_ref.shape[1]

    @pl.when(r == 0)
    def _():
        for g in range(ng):
            fg = f_ref[:, g * cg:(g + 1) * cg]
            ab_ref[0:seq, g * cg:(g + 1) * cg] = jnp.dot(
                fg, cc_ref[...], preferred_element_type=F32).astype(BF16)
            ab_ref[seq:2 * seq, g * cg:(g + 1) * cg] = jnp.dot(
                fg, sc_ref[...], preferred_element_type=F32).astype(BF16)

    z = (jnp.dot(cs_ref[...], ab_ref[0:seq, :], preferred_element_type=F32)
         - jnp.dot(ss_ref[...], ab_ref[seq:2 * seq, :], preferred_element_type=F32)) * norm
    for g in range(ng):
        zg = z[:, g * cg:(g + 1) * cg].astype(BF16)
        o_ref[:, g * cg:(g + 1) * cg] = jnp.dot(
            zg, wf_ref[g].astype(BF16), preferred_element_type=F32).astype(o_ref.dtype)


def _dft_cos_sin(n):
    lo = _tile(n, 64)
    hi = n // lo
    j = jnp.arange(n, dtype=jnp.int32)

    def table(k):
        ang = ((k[:, None] * j[None, :]) % n).astype(F32) * (2.0 * math.pi / n)
        return jnp.cos(ang), jnp.sin(ang)

    ca, sa = table(jnp.arange(hi, dtype=jnp.int32) * lo)
    cb, sb = table(jnp.arange(lo, dtype=jnp.int32))
    cos = ca[:, None, :] * cb[None, :, :] - sa[:, None, :] * sb[None, :, :]
    sin = sa[:, None, :] * cb[None, :, :] + ca[:, None, :] * sb[None, :, :]
    return cos.reshape(n, n).astype(BF16), sin.reshape(n, n).astype(BF16)


def fourier_mixer(proj, w_fourier, *, batch, seq, col0):
    ng, cg, _ = w_fourier.shape
    fw = ng * cg
    tr = _tile(seq, 1024)
    nr = seq // tr
    cc, sc = _dft_cos_sin(cg)
    cs, ss = _dft_cos_sin(seq)
    norm = 1.0 / math.sqrt(seq * cg)
    vmem = 2 * (seq * fw * 2 + 2 * cg * cg * 2 + 2 * tr * seq * 2 + ng * cg * cg * 4 + tr * fw * 2) \
        + 2 * seq * fw * 2 + 4 * tr * fw * 4
    return pl.pallas_call(
        functools.partial(_fourier_kernel, seq=seq, norm=norm),
        grid=(batch, nr),
        in_specs=[pl.BlockSpec((seq, fw), lambda b, r: (b, col0 // fw)),
                  pl.BlockSpec((cg, cg), lambda b, r: (0, 0)),
                  pl.BlockSpec((cg, cg), lambda b, r: (0, 0)),
                  pl.BlockSpec((tr, seq), lambda b, r: (r, 0)),
                  pl.BlockSpec((tr, seq), lambda b, r: (r, 0)),
                  pl.BlockSpec((ng, cg, cg), lambda b, r: (0, 0, 0))],
        out_specs=pl.BlockSpec((tr, fw), lambda b, r: (b * nr + r, 0)),
        out_shape=jax.ShapeDtypeStruct((batch * seq, fw), BF16),
        scratch_shapes=[pltpu.VMEM((2 * seq, fw), BF16)],
        compiler_params=_params(("parallel", "arbitrary"), vmem),
        name="fourier_mixer",
    )(proj, cc, sc, cs, ss, w_fourier)


def _router_affinity(hn, wr_ref, n_experts):
    hi = hn.astype(BF16)
    lo = (hn - hi.astype(F32)).astype(BF16)
    r_hi = jnp.dot(hi, wr_ref[...], preferred_element_type=F32)
    r_lo = jnp.dot(lo, wr_ref[...], preferred_element_type=F32)
    logits = r_hi[:, :LANES] + r_hi[:, LANES:] + r_lo[:, :LANES]
    lane = lax.broadcasted_iota(jnp.int32, logits.shape, 1)
    logits = jnp.where(lane < n_experts, logits, -jnp.inf)
    e = jnp.exp(logits - jnp.max(logits, axis=-1, keepdims=True))
    return e / jnp.sum(e, axis=-1, keepdims=True)


def _mm_res_norm_kernel(*refs, n_a, with_router, n_experts):
    a_refs = refs[:n_a]
    w_ref, x_ref, g_ref = refs[n_a:n_a + 3]
    pos = n_a + 3
    if with_router:
        wr_ref = refs[pos]
        pos += 1
    xo_ref, ho_ref = refs[pos:pos + 2]
    if with_router:
        aff_ref = refs[pos + 2]
    d = xo_ref.shape[1]
    tn = _tile(d, COL_CHUNK)
    for c in range(d // tn):
        sl = slice(c * tn, (c + 1) * tn)
        acc = x_ref[:, sl]
        row0 = 0
        for a_ref in a_refs:
            ka = a_ref.shape[1]
            acc = acc + jnp.dot(a_ref[...], w_ref[row0:row0 + ka, sl], preferred_element_type=F32)
            row0 += ka
        xo_ref[:, sl] = acc
    hn = _rms(xo_ref[...], g_ref[...])
    ho_ref[...] = hn.astype(ho_ref.dtype)
    if with_router:
        aff_ref[...] = _router_affinity(hn, wr_ref, n_experts)


def matmul_residual_norm(a_list, w, x_res, gain, *, router=None):
    m, d = x_res.shape
    k = w.shape[0]
    tm = _tile(m, 512)
    in_specs = [pl.BlockSpec((tm, a.shape[1]), lambda i: (i, 0)) for a in a_list]
    in_specs += [pl.BlockSpec((k, d), lambda i: (0, 0)),
                 pl.BlockSpec((tm, d), lambda i: (i, 0)),
                 pl.BlockSpec((1, d), lambda i: (0, 0))]
    args = list(a_list) + [w, x_res, gain.reshape(1, d)]
    out_specs = [pl.BlockSpec((tm, d), lambda i: (i, 0)),
                 pl.BlockSpec((tm, d), lambda i: (i, 0))]
    out_shape = [jax.ShapeDtypeStruct((m, d), F32), jax.ShapeDtypeStruct((m, d), BF16)]
    n_experts = 0
    if router is not None:
        wr, n_experts = router
        in_specs.append(pl.BlockSpec((d, 2 * LANES), lambda i: (0, 0)))
        args.append(wr)
        out_specs.append(pl.BlockSpec((tm, LANES), lambda i: (i, 0)))
        out_shape.append(jax.ShapeDtypeStruct((m, LANES), F32))
    vmem = 2 * (tm * k * 2 + k * d * 2 + tm * d * 4 + tm * d * 4 + tm * d * 2) \
        + 3 * tm * d * 4 + 4 * d * LANES * 2
    return pl.pallas_call(
        functools.partial(_mm_res_norm_kernel, n_a=len(a_list),
                          with_router=router is not None, n_experts=n_experts),
        grid=(m // tm,),
        in_specs=in_specs,
        out_specs=out_specs,
        out_shape=out_shape,
        compiler_params=_params(("parallel",), vmem),
        name="matmul_residual_norm",
    )(*args)


def _cross_block_kernel(h_ref, wq_ref, k_ref, v_ref, wo_ref, x_ref, g_ref, wr_ref,
                        xo_ref, ho_ref, aff_ref, o_ref, *, n_heads, scale, n_experts):
    dh = h_ref.shape[1] // n_heads
    h = h_ref[...]
    heads = [slice(i * dh, (i + 1) * dh) for i in range(n_heads)]
    qs = [jnp.dot(h, wq_ref[:, sl], preferred_element_type=F32).astype(BF16) for sl in heads]
    scores = [lax.dot_general(q, k_ref[:, sl], (((1,), (1,)), ((), ())),
                              preferred_element_type=F32) for q, sl in zip(qs, heads)]
    for sl, e in zip(heads, _softmax_numerators(scores, scale)):
        l = jnp.sum(e, axis=-1, keepdims=True)
        o = jnp.dot(e.astype(BF16), v_ref[:, sl], preferred_element_type=F32) / l
        o_ref[:, sl] = o.astype(o_ref.dtype)
    d = xo_ref.shape[1]
    tn = _tile(d, COL_CHUNK)
    o = o_ref[...]
    for c in range(d // tn):
        sl = slice(c * tn, (c + 1) * tn)
        xo_ref[:, sl] = x_ref[:, sl] + jnp.dot(o, wo_ref[:, sl], preferred_element_type=F32)
    hn = _rms(xo_ref[...], g_ref[...])
    ho_ref[...] = hn.astype(ho_ref.dtype)
    aff_ref[...] = _router_affinity(hn, wr_ref, n_experts)


def cross_block(h, w_q, k, v, w_o, x_res, gain, router, *, batch, seq, mem_len):
    m, d = x_res.shape
    wr, n_experts = router
    tq = _tile(seq, 512)
    nq = seq // tq
    once = pl.Buffered(1)
    vmem = 2 * (tq * d * 2 + 2 * mem_len * d * 2 + 2 * tq * d * 4 + tq * d * 2 + tq * LANES * 4) \
        + 2 * d * d * 2 + 2 * d * LANES * 2 + tq * d * 2 + 5 * tq * d * 4
    row = lambda b, i: (b * nq + i, 0)
    fixed = lambda b, i: (0, 0)
    return pl.pallas_call(
        functools.partial(_cross_block_kernel, n_heads=N_MEM_HEADS,
                          scale=(d // N_MEM_HEADS) ** -0.5, n_experts=n_experts),
        grid=(batch, nq),
        in_specs=[pl.BlockSpec((tq, d), row),
                  pl.BlockSpec((d, d), fixed, pipeline_mode=once),
                  pl.BlockSpec((mem_len, d), lambda b, i: (b, 0)),
                  pl.BlockSpec((mem_len, d), lambda b, i: (b, 0)),
                  pl.BlockSpec((d, d), fixed, pipeline_mode=once),
                  pl.BlockSpec((tq, d), row),
                  pl.BlockSpec((1, d), fixed),
                  pl.BlockSpec((d, 2 * LANES), fixed, pipeline_mode=once)],
        out_specs=[pl.BlockSpec((tq, d), row),
                   pl.BlockSpec((tq, d), row),
                   pl.BlockSpec((tq, LANES), row)],
        out_shape=[jax.ShapeDtypeStruct((m, d), F32), jax.ShapeDtypeStruct((m, d), BF16),
                   jax.ShapeDtypeStruct((m, LANES), F32)],
        scratch_shapes=[pltpu.VMEM((tq, d), BF16)],
        compiler_params=_params(("parallel", "parallel"), vmem),
        name="cross_block",
    )(h, w_q, k, v, w_o, x_res, gain.reshape(1, d), wr)


def _prefix_count(x, lane):
    n = x.shape[1]
    shift = 1
    while shift < n:
        x = x + jnp.where(lane >= shift, pltpu.roll(x, shift, axis=1), 0)
        shift *= 2
    return x


def _route_kernel(aff_ref, prow_ref, arow_ref, pcol_ref, *, n_experts, capacity):
    batch, e_pad, seq = prow_ref.shape
    a_t = jnp.concatenate([aff_ref[b * seq:(b + 1) * seq, :].T[:e_pad] for b in range(batch)],
                          axis=0)
    for b in range(batch):
        arow_ref[b] = a_t[b * e_pad:(b + 1) * e_pad]

    def body(i, thr_bits):
        cand = thr_bits | jnp.left_shift(jnp.int32(1), 30 - i)
        cnt = jnp.sum((a_t >= pltpu.bitcast(cand, F32)).astype(F32), axis=1, keepdims=True)
        return jnp.where(cnt >= capacity, cand, thr_bits)

    thr = pltpu.bitcast(lax.fori_loop(0, 31, body, jnp.zeros((batch * e_pad, 1), jnp.int32)), F32)
    lane = lax.broadcasted_iota(jnp.int32, a_t.shape, 1)
    gt = a_t > thr
    eq = a_t == thr
    need = capacity - jnp.sum(gt.astype(F32), axis=1, keepdims=True).astype(jnp.int32)
    sel = gt | (eq & (_prefix_count(eq.astype(jnp.int32), lane) <= need))
    pos = jnp.where(sel, _prefix_count(sel.astype(jnp.int32), lane) - 1, -1)
    row = lax.broadcasted_iota(jnp.int32, a_t.shape, 0)
    pos = jnp.where(row % e_pad < n_experts, pos, -1)
    unused = jnp.full((LANES - e_pad, seq), -1.0, F32)
    for b in range(batch):
        pos_b = pos[b * e_pad:(b + 1) * e_pad]
        prow_ref[b] = pos_b
        pcol_ref[b * seq:(b + 1) * seq, :] = jnp.concatenate([pos_b.astype(F32), unused], axis=0).T


def route(aff, *, batch, seq, n_experts, capacity):
    e_pad = max(SUBLANES, n_experts)
    return pl.pallas_call(
        functools.partial(_route_kernel, n_experts=n_experts, capacity=capacity),
        grid=(1,),
        in_specs=[pl.BlockSpec((batch * seq, LANES), lambda i: (0, 0))],
        out_specs=[pl.BlockSpec((batch, e_pad, seq), lambda i: (0, 0, 0)),
                   pl.BlockSpec((batch, e_pad, seq), lambda i: (0, 0, 0)),
                   pl.BlockSpec((batch * seq, LANES), lambda i: (0, 0))],
        out_shape=[jax.ShapeDtypeStruct((batch, e_pad, seq), jnp.int32),
                   jax.ShapeDtypeStruct((batch, e_pad, seq), F32),
                   jax.ShapeDtypeStruct((batch * seq, LANES), F32)],
        compiler_params=_params(("arbitrary",), 8 * batch * seq * LANES * 4),
        name="route",
    )(aff)


GATHER_TILES = 4
COMBINE_TILES = 4


def _window_size(capacity, n_tiles):
    return capacity // n_tiles + 2 * _window_margin(capacity)


def _window_margin(capacity):
    return capacity // 8


def _window_start(tile, capacity, n_tiles, clip=jnp.clip):
    share = capacity // n_tiles
    unit = _window_margin(capacity)
    window = _window_size(capacity, n_tiles)
    assert share % unit == 0 and (capacity - window) % unit == 0
    return clip(tile * (share // unit) - 1, 0, (capacity - window) // unit) * unit


def _static_clip(v, lo, hi):
    return max(lo, min(v, hi))


def _slots_inside_windows(prow, *, n_experts, capacity, n_tiles):
    batch, _, seq = prow.shape
    pos = prow[:, :n_experts, :].reshape(batch, n_experts, n_tiles, seq // n_tiles)
    first = _window_start(jnp.arange(n_tiles, dtype=jnp.int32), capacity, n_tiles)[None, None, :]
    lowest = jnp.min(jnp.where(pos >= 0, pos, capacity), axis=-1)
    highest = jnp.max(pos, axis=-1)
    return (lowest >= first) & (highest < first + _window_size(capacity, n_tiles))


def _gather_kernel(ok_ref, h_ref, prow_ref, o_ref, *, experts_per_step, n_tiles):
    b = pl.program_id(0)
    g = pl.program_id(1)
    cap = o_ref.shape[1]
    seq = h_ref.shape[0]
    tt = seq // n_tiles
    window = _window_size(cap, n_tiles)
    starts = [_window_start(t, cap, n_tiles, _static_clip) for t in range(n_tiles)]
    unit = _window_margin(cap)
    for j in range(experts_per_step):
        expert = g * experts_per_step + j
        prow = prow_ref[0, pl.ds(expert, 1), :]
        in_windows = ok_ref[b, expert] != 0

        @pl.when(in_windows)
        def _(j=j, prow=prow):
            slot = lax.broadcasted_iota(jnp.int32, (window, tt), 0)
            parts = [jnp.dot((slot + starts[t] == prow[:, t * tt:(t + 1) * tt]).astype(BF16),
                             h_ref[t * tt:(t + 1) * tt, :], preferred_element_type=F32)
                     for t in range(n_tiles)]
            for r0 in range(0, cap, unit):
                covering = [parts[t][r0 - starts[t]:r0 - starts[t] + unit]
                            for t in range(n_tiles) if starts[t] <= r0 < starts[t] + window]
                o_ref[j, r0:r0 + unit, :] = functools.reduce(jnp.add, covering).astype(o_ref.dtype)

        @pl.when(jnp.logical_not(in_windows))
        def _(j=j, prow=prow):
            slot = lax.broadcasted_iota(jnp.int32, (cap, seq), 0)
            o_ref[j] = jnp.dot((slot == prow).astype(BF16), h_ref[...],
                               preferred_element_type=F32).astype(o_ref.dtype)


def gather_tokens(h, prow, *, batch, seq, n_experts, capacity):
    d = h.shape[1]
    e_pad = prow.shape[1]
    ng = _tile(n_experts, 4)
    ok = jnp.all(_slots_inside_windows(prow, n_experts=n_experts, capacity=capacity,
                                       n_tiles=GATHER_TILES), axis=-1).astype(jnp.int32)
    vmem = 2 * (seq * d * 2 + e_pad * seq * 4 + ng * capacity * d * 2) + 2 * capacity * seq * 8 \
        + 3 * capacity * d * 4
    return pl.pallas_call(
        functools.partial(_gather_kernel, experts_per_step=ng, n_tiles=GATHER_TILES),
        grid_spec=pltpu.PrefetchScalarGridSpec(
            num_scalar_prefetch=1,
            grid=(batch, n_experts // ng),
            in_specs=[pl.BlockSpec((seq, d), lambda b, g, ok_ref: (b, 0)),
                      pl.BlockSpec((1, e_pad, seq), lambda b, g, ok_ref: (b, 0, 0))],
            out_specs=pl.BlockSpec((ng, None, capacity, d), lambda b, g, ok_ref: (g, b, 0, 0))),
        out_shape=jax.ShapeDtypeStruct((n_experts, batch, capacity, d), BF16),
        compiler_params=_params(("parallel", "arbitrary"), vmem),
        name="gather_tokens",
    )(ok, h, prow)


def _ffn_kernel(x_ref, wg_ref, wu_ref, wd_ref, prow_ref, arow_ref, o_ref, acc_ref, gate_ref,
                *, capacity, gate_steps):
    e = pl.program_id(0)
    f = pl.program_id(1)
    nf = pl.num_programs(1)
    rows = x_ref.shape[0]
    n_seq = rows // capacity

    def chunk_product():
        x = x_ref[...]
        a = jnp.dot(x, wg_ref[...].astype(BF16), preferred_element_type=F32)
        u = jnp.dot(x, wu_ref[...].astype(BF16), preferred_element_type=F32)
        hidden = (a / (1.0 + jnp.exp(-a)) * u).astype(BF16)
        return jnp.dot(hidden, wd_ref[...].astype(BF16), preferred_element_type=F32)

    def gate_share():
        seq = prow_ref.shape[2]
        slot = lax.broadcasted_iota(jnp.int32, (capacity, seq), 0)
        per_step = -(-n_seq // gate_steps)
        for j in range(per_step):
            b = jnp.minimum(f * per_step + j, n_seq - 1)
            mine = slot == prow_ref[b, pl.ds(e, 1), :]
            gate_ref[pl.ds(pl.multiple_of(b * capacity, capacity), capacity), :] = jnp.sum(
                jnp.where(mine, arow_ref[b, pl.ds(e, 1), :], 0.0), axis=1, keepdims=True)

    @pl.when(f == 0)
    def _():
        gate_share()
        acc_ref[...] = chunk_product()

    @pl.when((f > 0) & (f < nf - 1))
    def _():
        gate_share()
        acc_ref[...] += chunk_product()

    @pl.when(f == nf - 1)
    def _():
        gate_share()
        o_ref[...] = ((acc_ref[...] + chunk_product()) * gate_ref[...]).astype(o_ref.dtype)


def expert_ffn(xs, w_gate, w_up, w_down, prow, arow, layer, *, rows_per_expert, capacity):
    _, n_experts, d, ff = w_gate.shape
    tf = _tile(ff // 2, 512)
    r = rows_per_expert
    batch, e_pad, seq = prow.shape
    vmem = 2 * (r * d * 2 + 3 * d * tf * 4 + r * d * 2 + 2 * batch * e_pad * seq * 4) + r * d * 4 \
        + r * LANES * 4 + 3 * d * tf * 2 + 4 * r * tf * 4 + r * d * 4 + 2 * capacity * seq * 4
    return pl.pallas_call(
        functools.partial(_ffn_kernel, capacity=capacity, gate_steps=ff // tf),
        grid=(n_experts, ff // tf),
        in_specs=[pl.BlockSpec((r, d), lambda e, f: (e, 0)),
                  pl.BlockSpec((None, None, d, tf), lambda e, f: (layer, e, 0, f)),
                  pl.BlockSpec((None, None, d, tf), lambda e, f: (layer, e, 0, f)),
                  pl.BlockSpec((None, None, tf, d), lambda e, f: (layer, e, f, 0)),
                  pl.BlockSpec((batch, e_pad, seq), lambda e, f: (0, 0, 0)),
                  pl.BlockSpec((batch, e_pad, seq), lambda e, f: (0, 0, 0))],
        out_specs=pl.BlockSpec((r, d), lambda e, f: (e, 0)),
        out_shape=jax.ShapeDtypeStruct((n_experts * r, d), BF16),
        scratch_shapes=[pltpu.VMEM((r, d), F32), pltpu.VMEM((r, 1), F32)],
        compiler_params=_params(("parallel", "arbitrary"), vmem),
        name="expert_ffn",
    )(xs, w_gate, w_up, w_down, prow, arow)


def _combine_kernel(ok_ref, x_ref, ywin_ref, yall_ref, pcol_ref, g_ref, out_ref, *, final_norm):
    b = pl.program_id(0)
    t = pl.program_id(1)
    g = pl.program_id(2)
    tt, d = x_ref.shape
    n_group, _, window, _ = ywin_ref.shape
    cap = yall_ref.shape[1]

    @pl.when(g == 0)
    def _():
        out_ref[...] = x_ref[...]

    lane = lax.broadcasted_iota(jnp.int32, (tt, LANES), 1)
    pcol = pcol_ref[...]

    def scatter(y_rows, n_slots, first_slot):
        slot = (lax.broadcasted_iota(jnp.int32, (tt, n_slots), 1) + first_slot).astype(F32)
        hots = []
        for j in range(n_group):
            slot_of_token = jnp.sum(jnp.where(lane == g * n_group + j, pcol, 0.0),
                                    axis=1, keepdims=True)
            hots.append((slot == slot_of_token).astype(BF16))
        out_ref[...] += jnp.dot(jnp.concatenate(hots, axis=1), y_rows, preferred_element_type=F32)

    in_window = ok_ref[b, t, g] != 0

    @pl.when(in_window)
    def _():
        scatter(ywin_ref[...].reshape(n_group * window, d), window,
                _window_start(t, cap, COMBINE_TILES))

    @pl.when(jnp.logical_not(in_window))
    def _():
        scatter(yall_ref[...].reshape(n_group * cap, d), cap, 0)

    if final_norm:
        @pl.when(g == pl.num_programs(2) - 1)
        def _():
            out_ref[...] = _rms(out_ref[...], g_ref[...])


def combine(x_res, y, prow, pcol, gain, *, batch, seq, n_experts, capacity, final_norm):
    d = x_res.shape[1]
    nt = COMBINE_TILES
    tt = seq // nt
    window = _window_size(capacity, nt)
    ng = _tile(n_experts, 8)
    n_groups = n_experts // ng
    y4 = y.reshape(n_experts, batch, capacity, d)
    inside = _slots_inside_windows(prow, n_experts=n_experts, capacity=capacity, n_tiles=nt)
    ok = jnp.all(inside.reshape(batch, n_groups, ng, nt), axis=2).transpose(0, 2, 1).astype(jnp.int32)
    vmem = 2 * (tt * d * 4 + ng * window * d * 2 + ng * capacity * d * 2 + tt * LANES * 4 + tt * d * 4) \
        + 3 * tt * d * 4 + 2 * tt * capacity * 4

    def full_block(b, t, g, ok_ref):
        need = ok_ref[b, t, g] == 0
        return (jnp.where(need, g, 0), jnp.where(need, b, 0), 0, 0)

    return pl.pallas_call(
        functools.partial(_combine_kernel, final_norm=final_norm),
        grid_spec=pltpu.PrefetchScalarGridSpec(
            num_scalar_prefetch=1,
            grid=(batch, nt, n_groups),
            in_specs=[pl.BlockSpec((tt, d), lambda b, t, g, ok_ref: (b * nt + t, 0)),
                      pl.BlockSpec((pl.Element(ng), pl.Element(1), pl.Element(window), pl.Element(d)),
                                   lambda b, t, g, ok_ref: (g * ng, b, _window_start(t, capacity, nt), 0)),
                      pl.BlockSpec((ng, None, capacity, d), full_block),
                      pl.BlockSpec((tt, LANES), lambda b, t, g, ok_ref: (b * nt + t, 0)),
                      pl.BlockSpec((1, d), lambda b, t, g, ok_ref: (0, 0))],
            out_specs=pl.BlockSpec((tt, d), lambda b, t, g, ok_ref: (b * nt + t, 0))),
        out_shape=jax.ShapeDtypeStruct((batch * seq, d), F32),
        compiler_params=_params(("parallel", "parallel", "arbitrary"), vmem),
        name="combine",
    )(ok, x_res, y4, y4, pcol, gain.reshape(1, d))


def _pool_kernel(x_ref, prev_ref, next_ref, ng_ref, w_ref, s_ref, g_ref, xo_ref, ho_ref, pad_ref,
                 *, seq, windows):
    i = pl.program_id(0)
    tm, d = x_ref.shape
    halo = prev_ref.shape[0]
    pg = w_ref.shape[1]
    tiles_per_seq = seq // tm
    tile_in_seq = i % tiles_per_seq
    norm_gain = ng_ref[...]
    h = _rms(x_ref[...], norm_gain)
    pad_ref[0:halo, :] = jnp.where(tile_in_seq == 0, 0.0, _rms(prev_ref[...], norm_gain))
    pad_ref[halo:halo + tm, :] = h
    pad_ref[halo + tm:2 * halo + tm, :] = jnp.where(tile_in_seq == tiles_per_seq - 1, 0.0,
                                                    _rms(next_ref[...], norm_gain))
    t = tile_in_seq * tm + lax.broadcasted_iota(jnp.int32, (tm, 1), 0)
    for gi, w in enumerate(windows):
        cols = slice(gi * pg, (gi + 1) * pg)
        acc = pad_ref[:, cols]
        span = 1
        while span < w:
            acc = acc + pltpu.roll(acc, acc.shape[0] - span, axis=0)
            span *= 2
        total = acc[halo - w // 2:halo - w // 2 + tm, :]
        count = (jnp.minimum(t + (w - w // 2), seq) - jnp.maximum(t - w // 2, 0)).astype(F32)
        pooled = (total / count - h[:, cols]).astype(BF16)
        mixed = jnp.dot(pooled, w_ref[gi], preferred_element_type=F32)
        xo_ref[:, cols] = mixed * s_ref[:, cols] + x_ref[:, cols]
    ho_ref[...] = _rms(xo_ref[...], g_ref[...]).astype(ho_ref.dtype)


def pool_mixer(x, norm_gain, w_pool, pool_scale, next_gain, *, seq):
    m, d = x.shape
    ng, pg, _ = w_pool.shape
    halo = SUBLANES
    assert all(w & (w - 1) == 0 and w // 2 <= halo for w in POOL_WINDOWS) and ng == len(POOL_WINDOWS)
    tm = _tile(seq, 1024)
    per = tm // halo
    n_halo_blocks = m // halo
    vmem = 2 * (2 * tm * d * 4 + 2 * halo * d * 4 + ng * pg * pg * 2 + tm * d * 2) \
        + (tm + 2 * halo) * d * 4 + 6 * tm * d * 4
    return pl.pallas_call(
        functools.partial(_pool_kernel, seq=seq, windows=POOL_WINDOWS),
        grid=(m // tm,),
        in_specs=[pl.BlockSpec((tm, d), lambda i: (i, 0)),
                  pl.BlockSpec((halo, d), lambda i: (jnp.maximum(i * per - 1, 0), 0)),
                  pl.BlockSpec((halo, d), lambda i: (jnp.minimum((i + 1) * per, n_halo_blocks - 1), 0)),
                  pl.BlockSpec((1, d), lambda i: (0, 0)),
                  pl.BlockSpec((ng, pg, pg), lambda i: (0, 0, 0)),
                  pl.BlockSpec((1, d), lambda i: (0, 0)),
                  pl.BlockSpec((1, d), lambda i: (0, 0))],
        out_specs=[pl.BlockSpec((tm, d), lambda i: (i, 0)),
                   pl.BlockSpec((tm, d), lambda i: (i, 0))],
        out_shape=[jax.ShapeDtypeStruct((m, d), F32), jax.ShapeDtypeStruct((m, d), BF16)],
        scratch_shapes=[pltpu.VMEM((tm + 2 * halo, d), F32)],
        compiler_params=_params(("parallel",), vmem),
        name="pool_mixer",
    )(x, x, x, norm_gain.reshape(1, d), w_pool, pool_scale.reshape(1, d), next_gain.reshape(1, d))


def _rope_tables(seq):
    half = HEAD_DIM // 2
    rows = seq // GRID_W
    row_idx = jnp.repeat(jnp.arange(rows), GRID_W).astype(F32)
    col_idx = jnp.tile(jnp.arange(GRID_W), rows).astype(F32)
    inv_freq = 1.0 / (ROPE_THETA ** (jnp.arange(0, half, 2, dtype=F32) / half))
    ang = jnp.concatenate([row_idx[:, None] * inv_freq[None, :],
                           col_idx[:, None] * inv_freq[None, :]], axis=-1)
    cos, sin = jnp.cos(ang), jnp.sin(ang)
    zero = jnp.zeros_like(sin)
    cos_full = jnp.stack([cos, cos], axis=-1).reshape(seq, HEAD_DIM)
    sin_even = jnp.stack([-sin, zero], axis=-1).reshape(seq, HEAD_DIM)
    sin_odd = jnp.stack([zero, sin], axis=-1).reshape(seq, HEAD_DIM)
    return cos_full, sin_even, sin_odd


def _split_router(w_router):
    d, n_experts = w_router.shape
    w = jnp.pad(w_router, ((0, 0), (0, LANES - n_experts)))
    hi = w.astype(BF16)
    lo = (w - hi.astype(F32)).astype(BF16)
    return jnp.concatenate([hi, lo], axis=1), n_experts


def _moe(x_res, h, aff, w_gate, w_up, w_down, layer, gain, *, batch, seq, final_norm):
    n_experts = w_gate.shape[1]
    capacity = EC_CAPACITY_FACTOR * seq // n_experts
    d = h.shape[1]
    prow, arow, pcol = route(aff, batch=batch, seq=seq, n_experts=n_experts, capacity=capacity)
    xs = gather_tokens(h, prow, batch=batch, seq=seq, n_experts=n_experts, capacity=capacity)
    y = expert_ffn(xs.reshape(n_experts * batch * capacity, d), w_gate, w_up, w_down, prow, arow,
                   layer, rows_per_expert=batch * capacity, capacity=capacity)
    return combine(x_res, y, prow, pcol, gain, batch=batch, seq=seq,
                   n_experts=n_experts, capacity=capacity, final_norm=final_norm)


def _cross_block(x, h, mem2d, layer, cross_w_q, cross_w_k, cross_w_v, cross_w_o, mem_norm,
                 ffn_norm, router_w, *, batch, seq, mem_len):
    mem_n = rms_norm_rows(mem2d, mem_norm[layer], BF16)
    k = matmul_streamed(mem_n, cross_w_k, (layer,))
    v = matmul_streamed(mem_n, cross_w_v, (layer,))
    return cross_block(h, cast_weight(cross_w_q, (layer,)), k, v, cast_weight(cross_w_o, (layer,)),
                       x, ffn_norm[layer], _split_router(router_w[layer]),
                       batch=batch, seq=seq, mem_len=mem_len)


def kernel(x, mem, mix_norm, attn_w_in, q_gain, k_gain, fourier_w, attn_w_out, pool_w, pool_scale,
           cross_norm, mem_norm, cross_w_q, cross_w_k, cross_w_v, cross_w_o, ffn_norm, router_w,
           expert_w_gate, expert_w_up, expert_w_down, final_norm):
    batch, seq, d = x.shape
    mem_len = mem.shape[1]
    depth = mix_norm.shape[0]
    attn_width = N_Q_HEADS * HEAD_DIM
    kv_width = N_KV_HEADS * HEAD_DIM
    fourier_width = N_FOURIER_GROUPS * FOURIER_GROUP
    rope_width = attn_width + kv_width
    cos_t, sin_a, sin_b = _rope_tables(seq)

    xf = x.reshape(batch * seq, d)
    mem2d = mem.reshape(batch * mem_len, d)
    cross = functools.partial(_cross_block, mem2d=mem2d, cross_w_q=cross_w_q, cross_w_k=cross_w_k,
                              cross_w_v=cross_w_v, cross_w_o=cross_w_o, mem_norm=mem_norm,
                              ffn_norm=ffn_norm, router_w=router_w,
                              batch=batch, seq=seq, mem_len=mem_len)
    for layer in range(depth):
        i = layer // 2
        if layer % 2 == 0:
            gain_cols = jnp.concatenate([jnp.tile(q_gain[i], N_Q_HEADS), jnp.tile(k_gain[i], N_KV_HEADS),
                                         jnp.ones((kv_width + fourier_width,), F32)]).reshape(1, -1)
            proj = proj_in(xf, mix_norm[layer], cast_weight(attn_w_in, (i,)), gain_cols,
                           cos_t, sin_a, sin_b, seq=seq, rope_width=rope_width)
            o_attn = gqa_attention(proj, batch=batch, seq=seq, attn_width=attn_width, kv_width=kv_width)
            o_four = fourier_mixer(proj, fourier_w[i], batch=batch, seq=seq,
                                   col0=attn_width + 2 * kv_width)
            xf, h = matmul_residual_norm([o_attn, o_four], cast_weight(attn_w_out, (i,)), xf,
                                         cross_norm[layer])
        else:
            xf, h = pool_mixer(xf, mix_norm[layer], pool_w[i].astype(BF16), pool_scale[i],
                               cross_norm[layer], seq=seq)
        xf, h, aff = cross(xf, h, layer=layer)
        xf = _moe(xf, h, aff, expert_w_gate, expert_w_up, expert_w_down, layer, final_norm,
                  batch=batch, seq=seq, final_norm=layer == depth - 1)
    return xf.reshape(batch, seq, d)
```

```python
import functools
import math

import jax
import jax.numpy as jnp
from jax import lax
from jax.experimental import pallas as pl
from jax.experimental.pallas import tpu as pltpu

F32 = jnp.float32
BF16 = jnp.bfloat16

GRID_W = 64
HEAD_DIM = 128
N_Q_HEADS = 12
N_KV_HEADS = 4
N_FOURIER_GROUPS = 4
FOURIER_GROUP = 128
ROPE_THETA = 10000.0
POOL_WINDOWS = (2, 4, 8, 16)
N_MEM_HEADS = 4
EC_CAPACITY_FACTOR = 2
NORM_EPS = 1e-6

LANES = 128
SUBLANES = 8
V7X_VMEM_BYTES = 64 * 1024 * 1024
VMEM_HEADROOM_BYTES = 6 * 1024 * 1024
COL_CHUNK = 512
LOG2_E = math.log2(math.e)


def _params(semantics, vmem_bytes):
    limit = min(int(vmem_bytes) + VMEM_HEADROOM_BYTES, V7X_VMEM_BYTES - VMEM_HEADROOM_BYTES)
    return pltpu.CompilerParams(dimension_semantics=semantics, vmem_limit_bytes=limit)


def _tile(n, pref):
    t = min(n, pref)
    while n % t:
        t //= 2
    return t


def _rms(x, gain):
    ms = jnp.mean(x * x, axis=-1, keepdims=True)
    return x * lax.rsqrt(ms + NORM_EPS) * gain


def _softmax_numerators(scores, scale):
    c = scale * LOG2_E
    return [jnp.exp2((s - jnp.max(s, axis=-1, keepdims=True)) * c) for s in scores]


def _norm_kernel(x_ref, g_ref, o_ref):
    o_ref[...] = _rms(x_ref[...], g_ref[...]).astype(o_ref.dtype)


def rms_norm_rows(x, gain, out_dtype):
    m, d = x.shape
    tm = _tile(m, 512)
    return pl.pallas_call(
        _norm_kernel,
        grid=(m // tm,),
        in_specs=[pl.BlockSpec((tm, d), lambda i: (i, 0)),
                  pl.BlockSpec((1, d), lambda i: (0, 0))],
        out_specs=pl.BlockSpec((tm, d), lambda i: (i, 0)),
        out_shape=jax.ShapeDtypeStruct((m, d), out_dtype),
        compiler_params=_params(("parallel",), 2 * tm * d * 8),
        name="rms_norm_rows",
    )(x, gain.reshape(1, d))


def _cast_kernel(w_ref, o_ref):
    o_ref[...] = w_ref[...].astype(o_ref.dtype)


def cast_weight(w, lead):
    k, n = w.shape[-2:]
    tk = _tile(k, 1024)
    squeezed = (None,) * len(lead)
    return pl.pallas_call(
        _cast_kernel,
        grid=(k // tk,),
        in_specs=[pl.BlockSpec(squeezed + (tk, n), lambda i: tuple(lead) + (i, 0))],
        out_specs=pl.BlockSpec((tk, n), lambda i: (i, 0)),
        out_shape=jax.ShapeDtypeStruct((k, n), BF16),
        compiler_params=_params(("parallel",), 2 * tk * n * 6),
        name="cast_weight",
    )(w)


def _mm_stream_kernel(a_ref, w_ref, o_ref):
    o_ref[...] = jnp.dot(a_ref[...], w_ref[...].astype(BF16),
                         preferred_element_type=F32).astype(o_ref.dtype)


def matmul_streamed(a, w, lead, out_dtype=BF16):
    m, k = a.shape
    n = w.shape[-1]
    tn = _tile(n, 2 * COL_CHUNK)
    squeezed = (None,) * len(lead)
    vmem = 2 * (m * k * 2 + k * tn * 4 + m * tn * 2) + k * tn * 2 + m * tn * 4
    return pl.pallas_call(
        _mm_stream_kernel,
        grid=(n // tn,),
        in_specs=[pl.BlockSpec((m, k), lambda j: (0, 0)),
                  pl.BlockSpec(squeezed + (k, tn), lambda j: tuple(lead) + (0, j))],
        out_specs=pl.BlockSpec((m, tn), lambda j: (0, j)),
        out_shape=jax.ShapeDtypeStruct((m, n), out_dtype),
        compiler_params=_params(("parallel",), vmem),
        name="matmul_streamed",
    )(a, w)


def _proj_in_kernel(a_ref, ng_ref, w_ref, g_ref, cos_ref, sa_ref, sb_ref, o_ref, *, n_rope_heads):
    n = o_ref.shape[1]
    tn = _tile(n, COL_CHUNK)
    a = _rms(a_ref[...], ng_ref[...]).astype(BF16)
    cos, sin_a, sin_b = cos_ref[...], sa_ref[...], sb_ref[...]
    chunks = [jnp.dot(a, w_ref[:, c * tn:(c + 1) * tn], preferred_element_type=F32)
              for c in range(n // tn)]
    for c, acc in enumerate(chunks):
        for h in range(tn // HEAD_DIM):
            head = c * (tn // HEAD_DIM) + h
            col = slice(head * HEAD_DIM, (head + 1) * HEAD_DIM)
            xh = acc[:, h * HEAD_DIM:(h + 1) * HEAD_DIM]
            if head < n_rope_heads:
                ms = jnp.mean(xh * xh, axis=-1, keepdims=True)
                y = xh * lax.rsqrt(ms + NORM_EPS) * g_ref[:, col]
                xh = (y * cos + pltpu.roll(y, HEAD_DIM - 1, axis=1) * sin_a
                      + pltpu.roll(y, 1, axis=1) * sin_b)
            o_ref[:, col] = xh.astype(o_ref.dtype)


def proj_in(x, norm_gain, w_in, gain_cols, cos_t, sin_a, sin_b, *, seq, rope_width):
    m, k = x.shape
    n = w_in.shape[1]
    tm = _tile(seq, 512)
    n_pos_blocks = seq // tm
    vmem = 2 * (tm * k * 4 + k * n * 2 + tm * n * 2 + 3 * tm * HEAD_DIM * 4) + 6 * tm * COL_CHUNK * 4 \
        + 2 * tm * k * 4
    table = pl.BlockSpec((tm, HEAD_DIM), lambda i: (i % n_pos_blocks, 0))
    return pl.pallas_call(
        functools.partial(_proj_in_kernel, n_rope_heads=rope_width // HEAD_DIM),
        grid=(m // tm,),
        in_specs=[pl.BlockSpec((tm, k), lambda i: (i, 0)),
                  pl.BlockSpec((1, k), lambda i: (0, 0)),
                  pl.BlockSpec((k, n), lambda i: (0, 0)),
                  pl.BlockSpec((1, n), lambda i: (0, 0)),
                  table, table, table],
        out_specs=pl.BlockSpec((tm, n), lambda i: (i, 0)),
        out_shape=jax.ShapeDtypeStruct((m, n), BF16),
        compiler_params=_params(("parallel",), vmem),
        name="proj_in",
    )(x, norm_gain.reshape(1, k), w_in, gain_cols, cos_t, sin_a, sin_b)


def _attn_kernel(q_ref, k_ref, v_ref, o_ref, *, group, scale):
    k = k_ref[...]
    v = v_ref[...]
    v_ones = jnp.concatenate([v, jnp.ones_like(v)], axis=1)
    heads = [slice(g * HEAD_DIM, (g + 1) * HEAD_DIM) for g in range(group)]
    scores = [lax.dot_general(q_ref[:, sl], k, (((1,), (1,)), ((), ())),
                              preferred_element_type=F32) for sl in heads]
    for sl, e in zip(heads, _softmax_numerators(scores, scale)):
        ol = jnp.dot(e.astype(BF16), v_ones, preferred_element_type=F32)
        o_ref[:, sl] = (ol[:, :HEAD_DIM] / ol[:, HEAD_DIM:HEAD_DIM + 1]).astype(o_ref.dtype)


def gqa_attention(proj, *, batch, seq, attn_width, kv_width):
    group = N_Q_HEADS // N_KV_HEADS
    gw = group * HEAD_DIM
    tq = _tile(seq, 1024)
    nq = seq // tq
    k_col0 = attn_width // HEAD_DIM
    v_col0 = (attn_width + kv_width) // HEAD_DIM
    vmem = 2 * (tq * gw * 2 * 2 + 2 * seq * HEAD_DIM * 2) + group * tq * seq * 6
    return pl.pallas_call(
        functools.partial(_attn_kernel, group=group, scale=HEAD_DIM ** -0.5),
        grid=(batch, N_KV_HEADS, nq),
        in_specs=[pl.BlockSpec((tq, gw), lambda b, h, i: (b * nq + i, h)),
                  pl.BlockSpec((seq, HEAD_DIM), lambda b, h, i: (b, k_col0 + h)),
                  pl.BlockSpec((seq, HEAD_DIM), lambda b, h, i: (b, v_col0 + h))],
        out_specs=pl.BlockSpec((tq, gw), lambda b, h, i: (b * nq + i, h)),
        out_shape=jax.ShapeDtypeStruct((batch * seq, attn_width), BF16),
        compiler_params=_params(("parallel", "parallel", "parallel"), vmem),
        name="gqa_attention",
    )(proj, proj, proj)


def _fourier_kernel(f_ref, cc_ref, sc_ref, cs_ref, ss_ref, wf_ref, o_ref, ab_ref, *, seq, norm):
    r = pl.program_id(1)
    ng = wf_ref.shape[0]
    cg = wf_ref.shape[1]

    @pl.when(r == 0)
    def _():
        for g in range(ng):
            fg = f_ref[:, g * cg:(g + 1) * cg]
            ab_ref[0:seq, g * cg:(g + 1) * cg] = jnp.dot(
                fg, cc_ref[...], preferred_element_type=F32).astype(BF16)
            ab_ref[seq:2 * seq, g * cg:(g + 1) * cg] = jnp.dot(
                fg, sc_ref[...], preferred_element_type=F32).astype(BF16)

    z = (jnp.dot(cs_ref[...], ab_ref[0:seq, :], preferred_element_type=F32)
         - jnp.dot(ss_ref[...], ab_ref[seq:2 * seq, :], preferred_element_type=F32)) * norm
    for g in range(ng):
        zg = z[:, g * cg:(g + 1) * cg].astype(BF16)
        o_ref[:, g * cg:(g + 1) * cg] = jnp.dot(
            zg, wf_ref[g].astype(BF16), preferred_element_type=F32).astype(o_ref.dtype)


def _dft_cos_sin(n):
    lo = _tile(n, 64)
    hi = n // lo
    j = jnp.arange(n, dtype=jnp.int32)

    def table(k):
        ang = ((k[:, None] * j[None, :]) % n).astype(F32) * (2.0 * math.pi / n)
        return jnp.cos(ang), jnp.sin(ang)

    ca, sa = table(jnp.arange(hi, dtype=jnp.int32) * lo)
    cb, sb = table(jnp.arange(lo, dtype=jnp.int32))
    cos = ca[:, None, :] * cb[None, :, :] - sa[:, None, :] * sb[None, :, :]
    sin = sa[:, None, :] * cb[None, :, :] + ca[:, None, :] * sb[None, :, :]
    return cos.reshape(n, n).astype(BF16), sin.reshape(n, n).astype(BF16)


def fourier_mixer(proj, w_fourier, *, batch, seq, col0):
    ng, cg, _ = w_fourier.shape
    fw = ng * cg
    tr = _tile(seq, 1024)
    nr = seq // tr
    cc, sc = _dft_cos_sin(cg)
    cs, ss = _dft_cos_sin(seq)
    norm = 1.0 / math.sqrt(seq * cg)
    vmem = 2 * (seq * fw * 2 + 2 * cg * cg * 2 + 2 * tr * seq * 2 + ng * cg * cg * 4 + tr * fw * 2) \
        + 2 * seq * fw * 2 + 4 * tr * fw * 4
    return pl.pallas_call(
        functools.partial(_fourier_kernel, seq=seq, norm=norm),
        grid=(batch, nr),
        in_specs=[pl.BlockSpec((seq, fw), lambda b, r: (b, col0 // fw)),
                  pl.BlockSpec((cg, cg), lambda b, r: (0, 0)),
                  pl.BlockSpec((cg, cg), lambda b, r: (0, 0)),
                  pl.BlockSpec((tr, seq), lambda b, r: (r, 0)),
                  pl.BlockSpec((tr, seq), lambda b, r: (r, 0)),
                  pl.BlockSpec((ng, cg, cg), lambda b, r: (0, 0, 0))],
        out_specs=pl.BlockSpec((tr, fw), lambda b, r: (b * nr + r, 0)),
        out_shape=jax.ShapeDtypeStruct((batch * seq, fw), BF16),
        scratch_shapes=[pltpu.VMEM((2 * seq, fw), BF16)],
        compiler_params=_params(("parallel", "arbitrary"), vmem),
        name="fourier_mixer",
    )(proj, cc, sc, cs, ss, w_fourier)


def _router_affinity(hn, wr_ref, n_experts):
    hi = hn.astype(BF16)
    lo = (hn - hi.astype(F32)).astype(BF16)
    r_hi = jnp.dot(hi, wr_ref[...], preferred_element_type=F32)
    r_lo = jnp.dot(lo, wr_ref[...], preferred_element_type=F32)
    logits = r_hi[:, :LANES] + r_hi[:, LANES:] + r_lo[:, :LANES]
    lane = lax.broadcasted_iota(jnp.int32, logits.shape, 1)
    logits = jnp.where(lane < n_experts, logits, -jnp.inf)
    e = jnp.exp(logits - jnp.max(logits, axis=-1, keepdims=True))
    return e / jnp.sum(e, axis=-1, keepdims=True)


def _mm_res_norm_kernel(*refs, n_a, with_router, n_experts):
    a_refs = refs[:n_a]
    w_ref, x_ref, g_ref = refs[n_a:n_a + 3]
    pos = n_a + 3
    if with_router:
        wr_ref = refs[pos]
        pos += 1
    xo_ref, ho_ref = refs[pos:pos + 2]
    if with_router:
        aff_ref = refs[pos + 2]
    d = xo_ref.shape[1]
    tn = _tile(d, COL_CHUNK)
    for c in range(d // tn):
        sl = slice(c * tn, (c + 1) * tn)
        acc = x_ref[:, sl]
        row0 = 0
        for a_ref in a_refs:
            ka = a_ref.shape[1]
            acc = acc + jnp.dot(a_ref[...], w_ref[row0:row0 + ka, sl], preferred_element_type=F32)
            row0 += ka
        xo_ref[:, sl] = acc
    hn = _rms(xo_ref[...], g_ref[...])
    ho_ref[...] = hn.astype(ho_ref.dtype)
    if with_router:
        aff_ref[...] = _router_affinity(hn, wr_ref, n_experts)


def matmul_residual_norm(a_list, w, x_res, gain, *, router=None):
    m, d = x_res.shape
    k = w.shape[0]
    tm = _tile(m, 512)
    in_specs = [pl.BlockSpec((tm, a.shape[1]), lambda i: (i, 0)) for a in a_list]
    in_specs += [pl.BlockSpec((k, d), lambda i: (0, 0)),
                 pl.BlockSpec((tm, d), lambda i: (i, 0)),
                 pl.BlockSpec((1, d), lambda i: (0, 0))]
    args = list(a_list) + [w, x_res, gain.reshape(1, d)]
    out_specs = [pl.BlockSpec((tm, d), lambda i: (i, 0)),
                 pl.BlockSpec((tm, d), lambda i: (i, 0))]
    out_shape = [jax.ShapeDtypeStruct((m, d), F32), jax.ShapeDtypeStruct((m, d), BF16)]
    n_experts = 0
    if router is not None:
        wr, n_experts = router
        in_specs.append(pl.BlockSpec((d, 2 * LANES), lambda i: (0, 0)))
        args.append(wr)
        out_specs.append(pl.BlockSpec((tm, LANES), lambda i: (i, 0)))
        out_shape.append(jax.ShapeDtypeStruct((m, LANES), F32))
    vmem = 2 * (tm * k * 2 + k * d * 2 + tm * d * 4 + tm * d * 4 + tm * d * 2) \
        + 3 * tm * d * 4 + 4 * d * LANES * 2
    return pl.pallas_call(
        functools.partial(_mm_res_norm_kernel, n_a=len(a_list),
                          with_router=router is not None, n_experts=n_experts),
        grid=(m // tm,),
        in_specs=in_specs,
        out_specs=out_specs,
        out_shape=out_shape,
        compiler_params=_params(("parallel",), vmem),
        name="matmul_residual_norm",
    )(*args)


def _cross_block_kernel(h_ref, wq_ref, k_ref, v_ref, wo_ref, x_ref, g_ref, wr_ref,
                        xo_ref, ho_ref, aff_ref, o_ref, *, n_heads, scale, n_experts):
    dh = h_ref.shape[1] // n_heads
    h = h_ref[...]
    heads = [slice(i * dh, (i + 1) * dh) for i in range(n_heads)]
    qs = [jnp.dot(h, wq_ref[:, sl], preferred_element_type=F32).astype(BF16) for sl in heads]
    scores = [lax.dot_general(q, k_ref[:, sl], (((1,), (1,)), ((), ())),
                              preferred_element_type=F32) for q, sl in zip(qs, heads)]
    for sl, e in zip(heads, _softmax_numerators(scores, scale)):
        l = jnp.sum(e, axis=-1, keepdims=True)
        o = jnp.dot(e.astype(BF16), v_ref[:, sl], preferred_element_type=F32) / l
        o_ref[:, sl] = o.astype(o_ref.dtype)
    d = xo_ref.shape[1]
    tn = _tile(d, COL_CHUNK)
    o = o_ref[...]
    for c in range(d // tn):
        sl = slice(c * tn, (c + 1) * tn)
        xo_ref[:, sl] = x_ref[:, sl] + jnp.dot(o, wo_ref[:, sl], preferred_element_type=F32)
    hn = _rms(xo_ref[...], g_ref[...])
    ho_ref[...] = hn.astype(ho_ref.dtype)
    aff_ref[...] = _router_affinity(hn, wr_ref, n_experts)


def cross_block(h, w_q, k, v, w_o, x_res, gain, router, *, batch, seq, mem_len):
    m, d = x_res.shape
    wr, n_experts = router
    tq = _tile(seq, 512)
    nq = seq // tq
    once = pl.Buffered(1)
    vmem = 2 * (tq * d * 2 + 2 * mem_len * d * 2 + 2 * tq * d * 4 + tq * d * 2 + tq * LANES * 4) \
        + 2 * d * d * 2 + 2 * d * LANES * 2 + tq * d * 2 + 5 * tq * d * 4
    row = lambda b, i: (b * nq + i, 0)
    fixed = lambda b, i: (0, 0)
    return pl.pallas_call(
        functools.partial(_cross_block_kernel, n_heads=N_MEM_HEADS,
                          scale=(d // N_MEM_HEADS) ** -0.5, n_experts=n_experts),
        grid=(batch, nq),
        in_specs=[pl.BlockSpec((tq, d), row),
                  pl.BlockSpec((d, d), fixed, pipeline_mode=once),
                  pl.BlockSpec((mem_len, d), lambda b, i: (b, 0)),
                  pl.BlockSpec((mem_len, d), lambda b, i: (b, 0)),
                  pl.BlockSpec((d, d), fixed, pipeline_mode=once),
                  pl.BlockSpec((tq, d), row),
                  pl.BlockSpec((1, d), fixed),
                  pl.BlockSpec((d, 2 * LANES), fixed, pipeline_mode=once)],
        out_specs=[pl.BlockSpec((tq, d), row),
                   pl.BlockSpec((tq, d), row),
                   pl.BlockSpec((tq, LANES), row)],
        out_shape=[jax.ShapeDtypeStruct((m, d), F32), jax.ShapeDtypeStruct((m, d), BF16),
                   jax.ShapeDtypeStruct((m, LANES), F32)],
        scratch_shapes=[pltpu.VMEM((tq, d), BF16)],
        compiler_params=_params(("parallel", "parallel"), vmem),
        name="cross_block",
    )(h, w_q, k, v, w_o, x_res, gain.reshape(1, d), wr)


def _prefix_count(x, lane):
    n = x.shape[1]
    shift = 1
    while shift < n:
        x = x + jnp.where(lane >= shift, pltpu.roll(x, shift, axis=1), 0)
        shift *= 2
    return x


def _route_kernel(aff_ref, prow_ref, arow_ref, pcol_ref, *, n_experts, capacity):
    batch, e_pad, seq = prow_ref.shape
    a_t = jnp.concatenate([aff_ref[b * seq:(b + 1) * seq, :].T[:e_pad] for b in range(batch)],
                          axis=0)
    for b in range(batch):
        arow_ref[b] = a_t[b * e_pad:(b + 1) * e_pad]

    def body(i, thr_bits):
        cand = thr_bits | jnp.left_shift(jnp.int32(1), 30 - i)
        cnt = jnp.sum((a_t >= pltpu.bitcast(cand, F32)).astype(F32), axis=1, keepdims=True)
        return jnp.where(cnt >= capacity, cand, thr_bits)

    thr = pltpu.bitcast(lax.fori_loop(0, 31, body, jnp.zeros((batch * e_pad, 1), jnp.int32)), F32)
    lane = lax.broadcasted_iota(jnp.int32, a_t.shape, 1)
    gt = a_t > thr
    eq = a_t == thr
    need = capacity - jnp.sum(gt.astype(F32), axis=1, keepdims=True).astype(jnp.int32)
    sel = gt | (eq & (_prefix_count(eq.astype(jnp.int32), lane) <= need))
    pos = jnp.where(sel, _prefix_count(sel.astype(jnp.int32), lane) - 1, -1)
    row = lax.broadcasted_iota(jnp.int32, a_t.shape, 0)
    pos = jnp.where(row % e_pad < n_experts, pos, -1)
    unused = jnp.full((LANES - e_pad, seq), -1.0, F32)
    for b in range(batch):
        pos_b = pos[b * e_pad:(b + 1) * e_pad]
        prow_ref[b] = pos_b
        pcol_ref[b * seq:(b + 1) * seq, :] = jnp.concatenate([pos_b.astype(F32), unused], axis=0).T


def route(aff, *, batch, seq, n_experts, capacity):
    e_pad = max(SUBLANES, n_experts)
    return pl.pallas_call(
        functools.partial(_route_kernel, n_experts=n_experts, capacity=capacity),
        grid=(1,),
        in_specs=[pl.BlockSpec((batch * seq, LANES), lambda i: (0, 0))],
        out_specs=[pl.BlockSpec((batch, e_pad, seq), lambda i: (0, 0, 0)),
                   pl.BlockSpec((batch, e_pad, seq), lambda i: (0, 0, 0)),
                   pl.BlockSpec((batch * seq, LANES), lambda i: (0, 0))],
        out_shape=[jax.ShapeDtypeStruct((batch, e_pad, seq), jnp.int32),
                   jax.ShapeDtypeStruct((batch, e_pad, seq), F32),
                   jax.ShapeDtypeStruct((batch * seq, LANES), F32)],
        compiler_params=_params(("arbitrary",), 8 * batch * seq * LANES * 4),
        name="route",
    )(aff)


GATHER_TILES = 8
COMBINE_TILES = 4


def _window_size(capacity, n_tiles):
    return capacity // n_tiles + 2 * _window_margin(capacity)


def _window_margin(capacity):
    return capacity // 8


def _window_start(tile, capacity, n_tiles, clip=jnp.clip):
    share = capacity // n_tiles
    unit = _window_margin(capacity)
    window = _window_size(capacity, n_tiles)
    assert share % unit == 0 and (capacity - window) % unit == 0
    return clip(tile * (share // unit) - 1, 0, (capacity - window) // unit) * unit


def _static_clip(v, lo, hi):
    return max(lo, min(v, hi))


def _slots_inside_windows(prow, *, n_experts, capacity, n_tiles):
    batch, _, seq = prow.shape
    pos = prow[:, :n_experts, :].reshape(batch, n_experts, n_tiles, seq // n_tiles)
    first = _window_start(jnp.arange(n_tiles, dtype=jnp.int32), capacity, n_tiles)[None, None, :]
    lowest = jnp.min(jnp.where(pos >= 0, pos, capacity), axis=-1)
    highest = jnp.max(pos, axis=-1)
    return (lowest >= first) & (highest < first + _window_size(capacity, n_tiles))


def _gather_kernel(ok_ref, h_ref, prow_ref, o_ref, *, experts_per_step, n_tiles):
    b = pl.program_id(0)
    g = pl.program_id(1)
    cap = o_ref.shape[1]
    seq = h_ref.shape[0]
    tt = seq // n_tiles
    window = _window_size(cap, n_tiles)
    starts = [_window_start(t, cap, n_tiles, _static_clip) for t in range(n_tiles)]
    unit = _window_margin(cap)
    first = g * experts_per_step
    prows = [prow_ref[0, pl.ds(first + j, 1), :] for j in range(experts_per_step)]
    in_windows = ok_ref[b, g] != 0

    @pl.when(in_windows)
    def _():
        slot = lax.broadcasted_iota(jnp.int32, (unit, 1), 0)
        for r0 in range(0, cap, unit):
            covering = [t for t in range(n_tiles) if starts[t] <= r0 < starts[t] + window]
            assert covering == list(range(covering[0], covering[-1] + 1))
            tok = slice(covering[0] * tt, (covering[-1] + 1) * tt)
            onehot = jnp.concatenate([(slot == prow[:, tok] - r0).astype(BF16) for prow in prows],
                                     axis=0)
            rows = jnp.dot(onehot, h_ref[tok, :], preferred_element_type=F32).astype(o_ref.dtype)
            for j in range(experts_per_step):
                o_ref[j, r0:r0 + unit, :] = rows[j * unit:(j + 1) * unit]

    @pl.when(jnp.logical_not(in_windows))
    def _():
        slot = lax.broadcasted_iota(jnp.int32, (cap, seq), 0)
        for j, prow in enumerate(prows):
            o_ref[j] = jnp.dot((slot == prow).astype(BF16), h_ref[...],
                               preferred_element_type=F32).astype(o_ref.dtype)


def gather_tokens(h, prow, *, batch, seq, n_experts, capacity):
    d = h.shape[1]
    e_pad = prow.shape[1]
    ng = _tile(n_experts, 8)
    ok = jnp.all(_slots_inside_windows(prow, n_experts=n_experts, capacity=capacity,
                                       n_tiles=GATHER_TILES), axis=-1)
    ok = jnp.all(ok.reshape(batch, n_experts // ng, ng), axis=-1).astype(jnp.int32)
    vmem = 2 * (seq * d * 2 + e_pad * seq * 4 + ng * capacity * d * 2) + 2 * capacity * seq * 8 \
        + 3 * capacity * d * 4
    return pl.pallas_call(
        functools.partial(_gather_kernel, experts_per_step=ng, n_tiles=GATHER_TILES),
        grid_spec=pltpu.PrefetchScalarGridSpec(
            num_scalar_prefetch=1,
            grid=(batch, n_experts // ng),
            in_specs=[pl.BlockSpec((seq, d), lambda b, g, ok_ref: (b, 0)),
                      pl.BlockSpec((1, e_pad, seq), lambda b, g, ok_ref: (b, 0, 0))],
            out_specs=pl.BlockSpec((ng, None, capacity, d), lambda b, g, ok_ref: (g, b, 0, 0))),
        out_shape=jax.ShapeDtypeStruct((n_experts, batch, capacity, d), BF16),
        compiler_params=_params(("parallel", "arbitrary"), vmem),
        name="gather_tokens",
    )(ok, h, prow)


def _ffn_kernel(x_ref, wg_ref, wu_ref, wd_ref, prow_ref, arow_ref, o_ref, acc_ref, gate_ref,
                *, capacity, gate_steps):
    e = pl.program_id(0)
    f = pl.program_id(1)
    nf = pl.num_programs(1)
    rows = x_ref.shape[0]
    n_seq = rows // capacity

    def chunk_product():
        x = x_ref[...]
        a = jnp.dot(x, wg_ref[...].astype(BF16), preferred_element_type=F32)
        u = jnp.dot(x, wu_ref[...].astype(BF16), preferred_element_type=F32)
        hidden = (a / (1.0 + jnp.exp(-a)) * u).astype(BF16)
        return jnp.dot(hidden, wd_ref[...].astype(BF16), preferred_element_type=F32)

    def gate_share():
        seq = prow_ref.shape[2]
        slot = lax.broadcasted_iota(jnp.int32, (capacity, seq), 0)
        per_step = -(-n_seq // gate_steps)
        for j in range(per_step):
            b = jnp.minimum(f * per_step + j, n_seq - 1)
            mine = slot == prow_ref[b, pl.ds(e, 1), :]
            gate_ref[pl.ds(pl.multiple_of(b * capacity, capacity), capacity), :] = jnp.sum(
                jnp.where(mine, arow_ref[b, pl.ds(e, 1), :], 0.0), axis=1, keepdims=True)

    @pl.when(f == 0)
    def _():
        gate_share()
        acc_ref[...] = chunk_product()

    @pl.when((f > 0) & (f < nf - 1))
    def _():
        gate_share()
        acc_ref[...] += chunk_product()

    @pl.when(f == nf - 1)
    def _():
        gate_share()
        o_ref[...] = ((acc_ref[...] + chunk_product()) * gate_ref[...]).astype(o_ref.dtype)


def expert_ffn(xs, w_gate, w_up, w_down, prow, arow, layer, *, rows_per_expert, capacity):
    _, n_experts, d, ff = w_gate.shape
    tf = _tile(ff // 2, 512)
    r = rows_per_expert
    batch, e_pad, seq = prow.shape
    vmem = 2 * (r * d * 2 + 3 * d * tf * 4 + r * d * 2 + 2 * batch * e_pad * seq * 4) + r * d * 4 \
        + r * LANES * 4 + 3 * d * tf * 2 + 4 * r * tf * 4 + r * d * 4 + 2 * capacity * seq * 4
    return pl.pallas_call(
        functools.partial(_ffn_kernel, capacity=capacity, gate_steps=ff // tf),
        grid=(n_experts, ff // tf),
        in_specs=[pl.BlockSpec((r, d), lambda e, f: (e, 0)),
                  pl.BlockSpec((None, None, d, tf), lambda e, f: (layer, e, 0, f)),
                  pl.BlockSpec((None, None, d, tf), lambda e, f: (layer, e, 0, f)),
                  pl.BlockSpec((None, None, tf, d), lambda e, f: (layer, e, f, 0)),
                  pl.BlockSpec((batch, e_pad, seq), lambda e, f: (0, 0, 0)),
                  pl.BlockSpec((batch, e_pad, seq), lambda e, f: (0, 0, 0))],
        out_specs=pl.BlockSpec((r, d), lambda e, f: (e, 0)),
        out_shape=jax.ShapeDtypeStruct((n_experts * r, d), BF16),
        scratch_shapes=[pltpu.VMEM((r, d), F32), pltpu.VMEM((r, 1), F32)],
        compiler_params=_params(("parallel", "arbitrary"), vmem),
        name="expert_ffn",
    )(xs, w_gate, w_up, w_down, prow, arow)


def _combine_kernel(ok_ref, x_ref, ywin_ref, yall_ref, pcol_ref, g_ref, out_ref, *, final_norm):
    b = pl.program_id(0)
    t = pl.program_id(1)
    g = pl.program_id(2)
    tt, d = x_ref.shape
    n_group, _, window, _ = ywin_ref.shape
    cap = yall_ref.shape[1]

    @pl.when(g == 0)
    def _():
        out_ref[...] = x_ref[...]

    lane = lax.broadcasted_iota(jnp.int32, (tt, LANES), 1)
    pcol = pcol_ref[...]

    def scatter(y_rows, n_slots, first_slot):
        slot = (lax.broadcasted_iota(jnp.int32, (tt, n_slots), 1) + first_slot).astype(F32)
        hots = []
        for j in range(n_group):
            slot_of_token = jnp.sum(jnp.where(lane == g * n_group + j, pcol, 0.0),
                                    axis=1, keepdims=True)
            hots.append((slot == slot_of_token).astype(BF16))
        out_ref[...] += jnp.dot(jnp.concatenate(hots, axis=1), y_rows, preferred_element_type=F32)

    in_window = ok_ref[b, t, g] != 0

    @pl.when(in_window)
    def _():
        scatter(ywin_ref[...].reshape(n_group * window, d), window,
                _window_start(t, cap, COMBINE_TILES))

    @pl.when(jnp.logical_not(in_window))
    def _():
        scatter(yall_ref[...].reshape(n_group * cap, d), cap, 0)

    if final_norm:
        @pl.when(g == pl.num_programs(2) - 1)
        def _():
            out_ref[...] = _rms(out_ref[...], g_ref[...])


def combine(x_res, y, prow, pcol, gain, *, batch, seq, n_experts, capacity, final_norm):
    d = x_res.shape[1]
    nt = COMBINE_TILES
    tt = seq // nt
    window = _window_size(capacity, nt)
    ng = _tile(n_experts, 8)
    n_groups = n_experts // ng
    y4 = y.reshape(n_experts, batch, capacity, d)
    inside = _slots_inside_windows(prow, n_experts=n_experts, capacity=capacity, n_tiles=nt)
    ok = jnp.all(inside.reshape(batch, n_groups, ng, nt), axis=2).transpose(0, 2, 1).astype(jnp.int32)
    vmem = 2 * (tt * d * 4 + ng * window * d * 2 + ng * capacity * d * 2 + tt * LANES * 4 + tt * d * 4) \
        + 3 * tt * d * 4 + 2 * tt * capacity * 4

    def full_block(b, t, g, ok_ref):
        need = ok_ref[b, t, g] == 0
        return (jnp.where(need, g, 0), jnp.where(need, b, 0), 0, 0)

    return pl.pallas_call(
        functools.partial(_combine_kernel, final_norm=final_norm),
        grid_spec=pltpu.PrefetchScalarGridSpec(
            num_scalar_prefetch=1,
            grid=(batch, nt, n_groups),
            in_specs=[pl.BlockSpec((tt, d), lambda b, t, g, ok_ref: (b * nt + t, 0)),
                      pl.BlockSpec((pl.Element(ng), pl.Element(1), pl.Element(window), pl.Element(d)),
                                   lambda b, t, g, ok_ref: (g * ng, b, _window_start(t, capacity, nt), 0)),
                      pl.BlockSpec((ng, None, capacity, d), full_block),
                      pl.BlockSpec((tt, LANES), lambda b, t, g, ok_ref: (b * nt + t, 0)),
                      pl.BlockSpec((1, d), lambda b, t, g, ok_ref: (0, 0))],
            out_specs=pl.BlockSpec((tt, d), lambda b, t, g, ok_ref: (b * nt + t, 0))),
        out_shape=jax.ShapeDtypeStruct((batch * seq, d), F32),
        compiler_params=_params(("parallel", "parallel", "arbitrary"), vmem),
        name="combine",
    )(ok, x_res, y4, y4, pcol, gain.reshape(1, d))


def _pool_kernel(x_ref, prev_ref, next_ref, ng_ref, w_ref, s_ref, g_ref, xo_ref, ho_ref, pad_ref,
                 *, seq, windows):
    i = pl.program_id(0)
    tm, d = x_ref.shape
    halo = prev_ref.shape[0]
    pg = w_ref.shape[1]
    tiles_per_seq = seq // tm
    tile_in_seq = i % tiles_per_seq
    norm_gain = ng_ref[...]
    h = _rms(x_ref[...], norm_gain)
    pad_ref[0:halo, :] = jnp.where(tile_in_seq == 0, 0.0, _rms(prev_ref[...], norm_gain))
    pad_ref[halo:halo + tm, :] = h
    pad_ref[halo + tm:2 * halo + tm, :] = jnp.where(tile_in_seq == tiles_per_seq - 1, 0.0,
                                                    _rms(next_ref[...], norm_gain))
    t = tile_in_seq * tm + lax.broadcasted_iota(jnp.int32, (tm, 1), 0)
    for gi, w in enumerate(windows):
        cols = slice(gi * pg, (gi + 1) * pg)
        acc = pad_ref[:, cols]
        span = 1
        while span < w:
            acc = acc + pltpu.roll(acc, acc.shape[0] - span, axis=0)
            span *= 2
        total = acc[halo - w // 2:halo - w // 2 + tm, :]
        count = (jnp.minimum(t + (w - w // 2), seq) - jnp.maximum(t - w // 2, 0)).astype(F32)
        pooled = (total / count - h[:, cols]).astype(BF16)
        mixed = jnp.dot(pooled, w_ref[gi], preferred_element_type=F32)
        xo_ref[:, cols] = mixed * s_ref[:, cols] + x_ref[:, cols]
    ho_ref[...] = _rms(xo_ref[...], g_ref[...]).astype(ho_ref.dtype)


def pool_mixer(x, norm_gain, w_pool, pool_scale, next_gain, *, seq):
    m, d = x.shape
    ng, pg, _ = w_pool.shape
    halo = SUBLANES
    assert all(w & (w - 1) == 0 and w // 2 <= halo for w in POOL_WINDOWS) and ng == len(POOL_WINDOWS)
    tm = _tile(seq, 1024)
    per = tm // halo
    n_halo_blocks = m // halo
    vmem = 2 * (2 * tm * d * 4 + 2 * halo * d * 4 + ng * pg * pg * 2 + tm * d * 2) \
        + (tm + 2 * halo) * d * 4 + 6 * tm * d * 4
    return pl.pallas_call(
        functools.partial(_pool_kernel, seq=seq, windows=POOL_WINDOWS),
        grid=(m // tm,),
        in_specs=[pl.BlockSpec((tm, d), lambda i: (i, 0)),
                  pl.BlockSpec((halo, d), lambda i: (jnp.maximum(i * per - 1, 0), 0)),
                  pl.BlockSpec((halo, d), lambda i: (jnp.minimum((i + 1) * per, n_halo_blocks - 1), 0)),
                  pl.BlockSpec((1, d), lambda i: (0, 0)),
                  pl.BlockSpec((ng, pg, pg), lambda i: (0, 0, 0)),
                  pl.BlockSpec((1, d), lambda i: (0, 0)),
                  pl.BlockSpec((1, d), lambda i: (0, 0))],
        out_specs=[pl.BlockSpec((tm, d), lambda i: (i, 0)),
                   pl.BlockSpec((tm, d), lambda i: (i, 0))],
        out_shape=[jax.ShapeDtypeStruct((m, d), F32), jax.ShapeDtypeStruct((m, d), BF16)],
        scratch_shapes=[pltpu.VMEM((tm + 2 * halo, d), F32)],
        compiler_params=_params(("parallel",), vmem),
        name="pool_mixer",
    )(x, x, x, norm_gain.reshape(1, d), w_pool, pool_scale.reshape(1, d), next_gain.reshape(1, d))


def _rope_tables(seq):
    half = HEAD_DIM // 2
    rows = seq // GRID_W
    row_idx = jnp.repeat(jnp.arange(rows), GRID_W).astype(F32)
    col_idx = jnp.tile(jnp.arange(GRID_W), rows).astype(F32)
    inv_freq = 1.0 / (ROPE_THETA ** (jnp.arange(0, half, 2, dtype=F32) / half))
    ang = jnp.concatenate([row_idx[:, None] * inv_freq[None, :],
                           col_idx[:, None] * inv_freq[None, :]], axis=-1)
    cos, sin = jnp.cos(ang), jnp.sin(ang)
    zero = jnp.zeros_like(sin)
    cos_full = jnp.stack([cos, cos], axis=-1).reshape(seq, HEAD_DIM)
    sin_even = jnp.stack([-sin, zero], axis=-1).reshape(seq, HEAD_DIM)
    sin_odd = jnp.stack([zero, sin], axis=-1).reshape(seq, HEAD_DIM)
    return cos_full, sin_even, sin_odd


def _split_router(w_router):
    d, n_experts = w_router.shape
    w = jnp.pad(w_router, ((0, 0), (0, LANES - n_experts)))
    hi = w.astype(BF16)
    lo = (w - hi.astype(F32)).astype(BF16)
    return jnp.concatenate([hi, lo], axis=1), n_experts


def _moe(x_res, h, aff, w_gate, w_up, w_down, layer, gain, *, batch, seq, final_norm):
    n_experts = w_gate.shape[1]
    capacity = EC_CAPACITY_FACTOR * seq // n_experts
    d = h.shape[1]
    prow, arow, pcol = route(aff, batch=batch, seq=seq, n_experts=n_experts, capacity=capacity)
    xs = gather_tokens(h, prow, batch=batch, seq=seq, n_experts=n_experts, capacity=capacity)
    y = expert_ffn(xs.reshape(n_experts * batch * capacity, d), w_gate, w_up, w_down, prow, arow,
                   layer, rows_per_expert=batch * capacity, capacity=capacity)
    return combine(x_res, y, prow, pcol, gain, batch=batch, seq=seq,
                   n_experts=n_experts, capacity=capacity, final_norm=final_norm)


def _cross_block(x, h, mem2d, layer, cross_w_q, cross_w_k, cross_w_v, cross_w_o, mem_norm,
                 ffn_norm, router_w, *, batch, seq, mem_len):
    mem_n = rms_norm_rows(mem2d, mem_norm[layer], BF16)
    k = matmul_streamed(mem_n, cross_w_k, (layer,))
    v = matmul_streamed(mem_n, cross_w_v, (layer,))
    return cross_block(h, cast_weight(cross_w_q, (layer,)), k, v, cast_weight(cross_w_o, (layer,)),
                       x, ffn_norm[layer], _split_router(router_w[layer]),
                       batch=batch, seq=seq, mem_len=mem_len)


def kernel(x, mem, mix_norm, attn_w_in, q_gain, k_gain, fourier_w, attn_w_out, pool_w, pool_scale,
           cross_norm, mem_norm, cross_w_q, cross_w_k, cross_w_v, cross_w_o, ffn_norm, router_w,
           expert_w_gate, expert_w_up, expert_w_down, final_norm):
    batch, seq, d = x.shape
    mem_len = mem.shape[1]
    depth = mix_norm.shape[0]
    attn_width = N_Q_HEADS * HEAD_DIM
    kv_width = N_KV_HEADS * HEAD_DIM
    fourier_width = N_FOURIER_GROUPS * FOURIER_GROUP
    rope_width = attn_width + kv_width
    cos_t, sin_a, sin_b = _rope_tables(seq)

    xf = x.reshape(batch * seq, d)
    mem2d = mem.reshape(batch * mem_len, d)
    cross = functools.partial(_cross_block, mem2d=mem2d, cross_w_q=cross_w_q, cross_w_k=cross_w_k,
                              cross_w_v=cross_w_v, cross_w_o=cross_w_o, mem_norm=mem_norm,
                              ffn_norm=ffn_norm, router_w=router_w,
                              batch=batch, seq=seq, mem_len=mem_len)
    for layer in range(depth):
        i = layer // 2
        if layer % 2 == 0:
            gain_cols = jnp.concatenate([jnp.tile(q_gain[i], N_Q_HEADS), jnp.tile(k_gain[i], N_KV_HEADS),
                                         jnp.ones((kv_width + fourier_width,), F32)]).reshape(1, -1)
            proj = proj_in(xf, mix_norm[layer], cast_weight(attn_w_in, (i,)), gain_cols,
                           cos_t, sin_a, sin_b, seq=seq, rope_width=rope_width)
            o_attn = gqa_attention(proj, batch=batch, seq=seq, attn_width=attn_width, kv_width=kv_width)
            o_four = fourier_mixer(proj, fourier_w[i], batch=batch, seq=seq,
                                   col0=attn_width + 2 * kv_width)
            xf, h = matmul_residual_norm([o_attn, o_four], cast_weight(attn_w_out, (i,)), xf,
                                         cross_norm[layer])
        else:
            xf, h = pool_mixer(xf, mix_norm[layer], pool_w[i].astype(BF16), pool_scale[i],
                               cross_norm[layer], seq=seq)
        xf, h, aff = cross(xf, h, layer=layer)
        xf = _moe(xf, h, aff, expert_w_gate, expert_w_up, expert_w_down, layer, final_norm,
                  batch=batch, seq=seq, final_norm=layer == depth - 1)
    return xf.reshape(batch, seq, d)
```

```python
import functools
import math

import jax
import jax.numpy as jnp
from jax import lax
from jax.experimental import pallas as pl
from jax.experimental.pallas import tpu as pltpu

F32 = jnp.float32
BF16 = jnp.bfloat16

GRID_W = 64
HEAD_DIM = 128
N_Q_HEADS = 12
N_KV_HEADS = 4
N_FOURIER_GROUPS = 4
FOURIER_GROUP = 128
ROPE_THETA = 10000.0
POOL_WINDOWS = (2, 4, 8, 16)
N_MEM_HEADS = 4
EC_CAPACITY_FACTOR = 2
NORM_EPS = 1e-6

LANES = 128
SUBLANES = 8
V7X_VMEM_BYTES = 64 * 1024 * 1024
VMEM_HEADROOM_BYTES = 6 * 1024 * 1024
COL_CHUNK = 512
LOG2_E = math.log2(math.e)


def _params(semantics, vmem_bytes):
    limit = min(int(vmem_bytes) + VMEM_HEADROOM_BYTES, V7X_VMEM_BYTES - VMEM_HEADROOM_BYTES)
    return pltpu.CompilerParams(dimension_semantics=semantics, vmem_limit_bytes=limit)


def _tile(n, pref):
    t = min(n, pref)
    while n % t:
        t //= 2
    return t


def _rms(x, gain):
    ms = jnp.mean(x * x, axis=-1, keepdims=True)
    return x * lax.rsqrt(ms + NORM_EPS) * gain


def _softmax_numerators(scores, scale):
    c = scale * LOG2_E
    return [jnp.exp2((s - jnp.max(s, axis=-1, keepdims=True)) * c) for s in scores]


def _norm_kernel(x_ref, g_ref, o_ref):
    o_ref[...] = _rms(x_ref[...], g_ref[...]).astype(o_ref.dtype)


def rms_norm_rows(x, gain, out_dtype):
    m, d = x.shape
    tm = _tile(m, 512)
    return pl.pallas_call(
        _norm_kernel,
        grid=(m // tm,),
        in_specs=[pl.BlockSpec((tm, d), lambda i: (i, 0)),
                  pl.BlockSpec((1, d), lambda i: (0, 0))],
        out_specs=pl.BlockSpec((tm, d), lambda i: (i, 0)),
        out_shape=jax.ShapeDtypeStruct((m, d), out_dtype),
        compiler_params=_params(("parallel",), 2 * tm * d * 8),
        name="rms_norm_rows",
    )(x, gain.reshape(1, d))


def _cast_kernel(w_ref, o_ref):
    o_ref[...] = w_ref[...].astype(o_ref.dtype)


def cast_weight(w, lead):
    k, n = w.shape[-2:]
    tk = _tile(k, 1024)
    squeezed = (None,) * len(lead)
    return pl.pallas_call(
        _cast_kernel,
        grid=(k // tk,),
        in_specs=[pl.BlockSpec(squeezed + (tk, n), lambda i: tuple(lead) + (i, 0))],
        out_specs=pl.BlockSpec((tk, n), lambda i: (i, 0)),
        out_shape=jax.ShapeDtypeStruct((k, n), BF16),
        compiler_params=_params(("parallel",), 2 * tk * n * 6),
        name="cast_weight",
    )(w)


def _mm_stream_kernel(a_ref, w_ref, o_ref):
    o_ref[...] = jnp.dot(a_ref[...], w_ref[...].astype(BF16),
                         preferred_element_type=F32).astype(o_ref.dtype)


def matmul_streamed(a, w, lead, out_dtype=BF16):
    m, k = a.shape
    n = w.shape[-1]
    tn = _tile(n, 2 * COL_CHUNK)
    squeezed = (None,) * len(lead)
    vmem = 2 * (m * k * 2 + k * tn * 4 + m * tn * 2) + k * tn * 2 + m * tn * 4
    return pl.pallas_call(
        _mm_stream_kernel,
        grid=(n // tn,),
        in_specs=[pl.BlockSpec((m, k), lambda j: (0, 0)),
                  pl.BlockSpec(squeezed + (k, tn), lambda j: tuple(lead) + (0, j))],
        out_specs=pl.BlockSpec((m, tn), lambda j: (0, j)),
        out_shape=jax.ShapeDtypeStruct((m, n), out_dtype),
        compiler_params=_params(("parallel",), vmem),
        name="matmul_streamed",
    )(a, w)


def _proj_in_kernel(a_ref, ng_ref, w_ref, g_ref, cos_ref, sa_ref, sb_ref, o_ref, *, n_rope_heads):
    n = o_ref.shape[1]
    tn = _tile(n, COL_CHUNK)
    a = _rms(a_ref[...], ng_ref[...]).astype(BF16)
    cos, sin_a, sin_b = cos_ref[...], sa_ref[...], sb_ref[...]
    chunks = [jnp.dot(a, w_ref[:, c * tn:(c + 1) * tn], preferred_element_type=F32)
              for c in range(n // tn)]
    for c, acc in enumerate(chunks):
        for h in range(tn // HEAD_DIM):
            head = c * (tn // HEAD_DIM) + h
            col = slice(head * HEAD_DIM, (head + 1) * HEAD_DIM)
            xh = acc[:, h * HEAD_DIM:(h + 1) * HEAD_DIM]
            if head < n_rope_heads:
                ms = jnp.mean(xh * xh, axis=-1, keepdims=True)
                y = xh * lax.rsqrt(ms + NORM_EPS) * g_ref[:, col]
                xh = (y * cos + pltpu.roll(y, HEAD_DIM - 1, axis=1) * sin_a
                      + pltpu.roll(y, 1, axis=1) * sin_b)
            o_ref[:, col] = xh.astype(o_ref.dtype)


def proj_in(x, norm_gain, w_in, gain_cols, cos_t, sin_a, sin_b, *, seq, rope_width):
    m, k = x.shape
    n = w_in.shape[1]
    tm = _tile(seq, 512)
    n_pos_blocks = seq // tm
    vmem = 2 * (tm * k * 4 + k * n * 2 + tm * n * 2 + 3 * tm * HEAD_DIM * 4) + 6 * tm * COL_CHUNK * 4 \
        + 2 * tm * k * 4
    table = pl.BlockSpec((tm, HEAD_DIM), lambda i: (i % n_pos_blocks, 0))
    return pl.pallas_call(
        functools.partial(_proj_in_kernel, n_rope_heads=rope_width // HEAD_DIM),
        grid=(m // tm,),
        in_specs=[pl.BlockSpec((tm, k), lambda i: (i, 0)),
                  pl.BlockSpec((1, k), lambda i: (0, 0)),
                  pl.BlockSpec((k, n), lambda i: (0, 0)),
                  pl.BlockSpec((1, n), lambda i: (0, 0)),
                  table, table, table],
        out_specs=pl.BlockSpec((tm, n), lambda i: (i, 0)),
        out_shape=jax.ShapeDtypeStruct((m, n), BF16),
        compiler_params=_params(("parallel",), vmem),
        name="proj_in",
    )(x, norm_gain.reshape(1, k), w_in, gain_cols, cos_t, sin_a, sin_b)


def _attn_kernel(q_ref, k_ref, v_ref, o_ref, *, group, scale):
    k = k_ref[...]
    v = v_ref[...]
    v_ones = jnp.concatenate([v, jnp.ones_like(v)], axis=1)
    heads = [slice(g * HEAD_DIM, (g + 1) * HEAD_DIM) for g in range(group)]
    scores = [lax.dot_general(q_ref[:, sl], k, (((1,), (1,)), ((), ())),
                              preferred_element_type=F32) for sl in heads]
    for sl, e in zip(heads, _softmax_numerators(scores, scale)):
        ol = jnp.dot(e.astype(BF16), v_ones, preferred_element_type=F32)
        o_ref[:, sl] = (ol[:, :HEAD_DIM] / ol[:, HEAD_DIM:HEAD_DIM + 1]).astype(o_ref.dtype)


def gqa_attention(proj, *, batch, seq, attn_width, kv_width):
    group = N_Q_HEADS // N_KV_HEADS
    gw = group * HEAD_DIM
    tq = _tile(seq, 1024)
    nq = seq // tq
    k_col0 = attn_width // HEAD_DIM
    v_col0 = (attn_width + kv_width) // HEAD_DIM
    vmem = 2 * (tq * gw * 2 * 2 + 2 * seq * HEAD_DIM * 2) + group * tq * seq * 6
    return pl.pallas_call(
        functools.partial(_attn_kernel, group=group, scale=HEAD_DIM ** -0.5),
        grid=(batch, N_KV_HEADS, nq),
        in_specs=[pl.BlockSpec((tq, gw), lambda b, h, i: (b * nq + i, h)),
                  pl.BlockSpec((seq, HEAD_DIM), lambda b, h, i: (b, k_col0 + h)),
                  pl.BlockSpec((seq, HEAD_DIM), lambda b, h, i: (b, v_col0 + h))],
        out_specs=pl.BlockSpec((tq, gw), lambda b, h, i: (b * nq + i, h)),
        out_shape=jax.ShapeDtypeStruct((batch * seq, attn_width), BF16),
        compiler_params=_params(("parallel", "parallel", "parallel"), vmem),
        name="gqa_attention",
    )(proj, proj, proj)


def _fourier_kernel(f_ref, cc_ref, sc_ref, cs_ref, ss_ref, wf_ref, o_ref, ab_ref, *, seq, norm):
    r = pl.program_id(1)
    ng = wf_ref.shape[0]
    cg = wf_ref.shape[1]

    @pl.when(r == 0)
    def _():
        for g in range(ng):
            fg = f_ref[:, g * cg:(g + 1) * cg]
            ab_ref[0:seq, g * cg:(g + 1) * cg] = jnp.dot(
                fg, cc_ref[...], preferred_element_type=F32).astype(BF16)
            ab_ref[seq:2 * seq, g * cg:(g + 1) * cg] = jnp.dot(
                fg, sc_ref[...], preferred_element_type=F32).astype(BF16)

    z = (jnp.dot(cs_ref[...], ab_ref[0:seq, :], preferred_element_type=F32)
         - jnp.dot(ss_ref[...], ab_ref[seq:2 * seq, :], preferred_element_type=F32)) * norm
    for g in range(ng):
        zg = z[:, g * cg:(g + 1) * cg].astype(BF16)
        o_ref[:, g * cg:(g + 1) * cg] = jnp.dot(
            zg, wf_ref[g].astype(BF16), preferred_element_type=F32).astype(o_ref.dtype)


def _dft_cos_sin(n):
    lo = _tile(n, 64)
    hi = n // lo
    j = jnp.arange(n, dtype=jnp.int32)

    def table(k):
        ang = ((k[:, None] * j[None, :]) % n).astype(F32) * (2.0 * math.pi / n)
        return jnp.cos(ang), jnp.sin(ang)

    ca, sa = table(jnp.arange(hi, dtype=jnp.int32) * lo)
    cb, sb = table(jnp.arange(lo, dtype=jnp.int32))
    cos = ca[:, None, :] * cb[None, :, :] - sa[:, None, :] * sb[None, :, :]
    sin = sa[:, None, :] * cb[None, :, :] + ca[:, None, :] * sb[None, :, :]
    return cos.reshape(n, n).astype(BF16), sin.reshape(n, n).astype(BF16)


def fourier_mixer(proj, w_fourier, *, batch, seq, col0):
    ng, cg, _ = w_fourier.shape
    fw = ng * cg
    tr = _tile(seq, 1024)
    nr = seq // tr
    cc, sc = _dft_cos_sin(cg)
    cs, ss = _dft_cos_sin(seq)
    norm = 1.0 / math.sqrt(seq * cg)
    vmem = 2 * (seq * fw * 2 + 2 * cg * cg * 2 + 2 * tr * seq * 2 + ng * cg * cg * 4 + tr * fw * 2) \
        + 2 * seq * fw * 2 + 4 * tr * fw * 4
    return pl.pallas_call(
        functools.partial(_fourier_kernel, seq=seq, norm=norm),
        grid=(batch, nr),
        in_specs=[pl.BlockSpec((seq, fw), lambda b, r: (b, col0 // fw)),
                  pl.BlockSpec((cg, cg), lambda b, r: (0, 0)),
                  pl.BlockSpec((cg, cg), lambda b, r: (0, 0)),
                  pl.BlockSpec((tr, seq), lambda b, r: (r, 0)),
                  pl.BlockSpec((tr, seq), lambda b, r: (r, 0)),
                  pl.BlockSpec((ng, cg, cg), lambda b, r: (0, 0, 0))],
        out_specs=pl.BlockSpec((tr, fw), lambda b, r: (b * nr + r, 0)),
        out_shape=jax.ShapeDtypeStruct((batch * seq, fw), BF16),
        scratch_shapes=[pltpu.VMEM((2 * seq, fw), BF16)],
        compiler_params=_params(("parallel", "arbitrary"), vmem),
        name="fourier_mixer",
    )(proj, cc, sc, cs, ss, w_fourier)


def _router_affinity(hn, wr_ref, n_experts):
    hi = hn.astype(BF16)
    lo = (hn - hi.astype(F32)).astype(BF16)
    r_hi = jnp.dot(hi, wr_ref[...], preferred_element_type=F32)
    r_lo = jnp.dot(lo, wr_ref[...], preferred_element_type=F32)
    logits = r_hi[:, :LANES] + r_hi[:, LANES:] + r_lo[:, :LANES]
    lane = lax.broadcasted_iota(jnp.int32, logits.shape, 1)
    logits = jnp.where(lane < n_experts, logits, -jnp.inf)
    e = jnp.exp(logits - jnp.max(logits, axis=-1, keepdims=True))
    return e / jnp.sum(e, axis=-1, keepdims=True)


def _mm_res_norm_kernel(*refs, n_a, with_router, n_experts):
    a_refs = refs[:n_a]
    w_ref, x_ref, g_ref = refs[n_a:n_a + 3]
    pos = n_a + 3
    if with_router:
        wr_ref = refs[pos]
        pos += 1
    xo_ref, ho_ref = refs[pos:pos + 2]
    if with_router:
        aff_ref = refs[pos + 2]
    d = xo_ref.shape[1]
    tn = _tile(d, COL_CHUNK)
    for c in range(d // tn):
        sl = slice(c * tn, (c + 1) * tn)
        acc = x_ref[:, sl]
        row0 = 0
        for a_ref in a_refs:
            ka = a_ref.shape[1]
            acc = acc + jnp.dot(a_ref[...], w_ref[row0:row0 + ka, sl], preferred_element_type=F32)
            row0 += ka
        xo_ref[:, sl] = acc
    hn = _rms(xo_ref[...], g_ref[...])
    ho_ref[...] = hn.astype(ho_ref.dtype)
    if with_router:
        aff_ref[...] = _router_affinity(hn, wr_ref, n_experts)


def matmul_residual_norm(a_list, w, x_res, gain, *, router=None):
    m, d = x_res.shape
    k = w.shape[0]
    tm = _tile(m, 512)
    in_specs = [pl.BlockSpec((tm, a.shape[1]), lambda i: (i, 0)) for a in a_list]
    in_specs += [pl.BlockSpec((k, d), lambda i: (0, 0)),
                 pl.BlockSpec((tm, d), lambda i: (i, 0)),
                 pl.BlockSpec((1, d), lambda i: (0, 0))]
    args = list(a_list) + [w, x_res, gain.reshape(1, d)]
    out_specs = [pl.BlockSpec((tm, d), lambda i: (i, 0)),
                 pl.BlockSpec((tm, d), lambda i: (i, 0))]
    out_shape = [jax.ShapeDtypeStruct((m, d), F32), jax.ShapeDtypeStruct((m, d), BF16)]
    n_experts = 0
    if router is not None:
        wr, n_experts = router
        in_specs.append(pl.BlockSpec((d, 2 * LANES), lambda i: (0, 0)))
        args.append(wr)
        out_specs.append(pl.BlockSpec((tm, LANES), lambda i: (i, 0)))
        out_shape.append(jax.ShapeDtypeStruct((m, LANES), F32))
    vmem = 2 * (tm * k * 2 + k * d * 2 + tm * d * 4 + tm * d * 4 + tm * d * 2) \
        + 3 * tm * d * 4 + 4 * d * LANES * 2
    return pl.pallas_call(
        functools.partial(_mm_res_norm_kernel, n_a=len(a_list),
                          with_router=router is not None, n_experts=n_experts),
        grid=(m // tm,),
        in_specs=in_specs,
        out_specs=out_specs,
        out_shape=out_shape,
        compiler_params=_params(("parallel",), vmem),
        name="matmul_residual_norm",
    )(*args)


def _cross_block_kernel(h_ref, wq_ref, k_ref, v_ref, wo_ref, x_ref, g_ref, wr_ref,
                        xo_ref, ho_ref, aff_ref, o_ref, *, n_heads, scale, n_experts):
    dh = h_ref.shape[1] // n_heads
    h = h_ref[...]
    heads = [slice(i * dh, (i + 1) * dh) for i in range(n_heads)]
    qs = [jnp.dot(h, wq_ref[:, sl], preferred_element_type=F32).astype(BF16) for sl in heads]
    scores = [lax.dot_general(q, k_ref[:, sl], (((1,), (1,)), ((), ())),
                              preferred_element_type=F32) for q, sl in zip(qs, heads)]
    for sl, e in zip(heads, _softmax_numerators(scores, scale)):
        l = jnp.sum(e, axis=-1, keepdims=True)
        o = jnp.dot(e.astype(BF16), v_ref[:, sl], preferred_element_type=F32) / l
        o_ref[:, sl] = o.astype(o_ref.dtype)
    d = xo_ref.shape[1]
    tn = _tile(d, COL_CHUNK)
    o = o_ref[...]
    for c in range(d // tn):
        sl = slice(c * tn, (c + 1) * tn)
        xo_ref[:, sl] = x_ref[:, sl] + jnp.dot(o, wo_ref[:, sl], preferred_element_type=F32)
    hn = _rms(xo_ref[...], g_ref[...])
    ho_ref[...] = hn.astype(ho_ref.dtype)
    aff_ref[...] = _router_affinity(hn, wr_ref, n_experts)


def cross_block(h, w_q, k, v, w_o, x_res, gain, router, *, batch, seq, mem_len):
    m, d = x_res.shape
    wr, n_experts = router
    tq = _tile(seq, 512)
    nq = seq // tq
    once = pl.Buffered(1)
    vmem = 2 * (tq * d * 2 + 2 * mem_len * d * 2 + 2 * tq * d * 4 + tq * d * 2 + tq * LANES * 4) \
        + 2 * d * d * 2 + 2 * d * LANES * 2 + tq * d * 2 + 5 * tq * d * 4
    row = lambda b, i: (b * nq + i, 0)
    fixed = lambda b, i: (0, 0)
    return pl.pallas_call(
        functools.partial(_cross_block_kernel, n_heads=N_MEM_HEADS,
                          scale=(d // N_MEM_HEADS) ** -0.5, n_experts=n_experts),
        grid=(batch, nq),
        in_specs=[pl.BlockSpec((tq, d), row),
                  pl.BlockSpec((d, d), fixed, pipeline_mode=once),
                  pl.BlockSpec((mem_len, d), lambda b, i: (b, 0)),
                  pl.BlockSpec((mem_len, d), lambda b, i: (b, 0)),
                  pl.BlockSpec((d, d), fixed, pipeline_mode=once),
                  pl.BlockSpec((tq, d), row),
                  pl.BlockSpec((1, d), fixed),
                  pl.BlockSpec((d, 2 * LANES), fixed, pipeline_mode=once)],
        out_specs=[pl.BlockSpec((tq, d), row),
                   pl.BlockSpec((tq, d), row),
                   pl.BlockSpec((tq, LANES), row)],
        out_shape=[jax.ShapeDtypeStruct((m, d), F32), jax.ShapeDtypeStruct((m, d), BF16),
                   jax.ShapeDtypeStruct((m, LANES), F32)],
        scratch_shapes=[pltpu.VMEM((tq, d), BF16)],
        compiler_params=_params(("parallel", "parallel"), vmem),
        name="cross_block",
    )(h, w_q, k, v, w_o, x_res, gain.reshape(1, d), wr)


def _prefix_count(x, lane):
    n = x.shape[1]
    shift = 1
    while shift < n:
        x = x + jnp.where(lane >= shift, pltpu.roll(x, shift, axis=1), 0)
        shift *= 2
    return x


def _route_kernel(aff_ref, prow_ref, arow_ref, pcol_ref, *, n_experts, capacity):
    batch, e_pad, seq = prow_ref.shape
    a_t = jnp.concatenate([aff_ref[b * seq:(b + 1) * seq, :].T[:e_pad] for b in range(batch)],
                          axis=0)
    for b in range(batch):
        arow_ref[b] = a_t[b * e_pad:(b + 1) * e_pad]

    def body(i, thr_bits):
        cand = thr_bits | jnp.left_shift(jnp.int32(1), 30 - i)
        cnt = jnp.sum((a_t >= pltpu.bitcast(cand, F32)).astype(F32), axis=1, keepdims=True)
        return jnp.where(cnt >= capacity, cand, thr_bits)

    thr = pltpu.bitcast(lax.fori_loop(0, 31, body, jnp.zeros((batch * e_pad, 1), jnp.int32)), F32)
    lane = lax.broadcasted_iota(jnp.int32, a_t.shape, 1)
    gt = a_t > thr
    eq = a_t == thr
    need = capacity - jnp.sum(gt.astype(F32), axis=1, keepdims=True).astype(jnp.int32)
    sel = gt | (eq & (_prefix_count(eq.astype(jnp.int32), lane) <= need))
    pos = jnp.where(sel, _prefix_count(sel.astype(jnp.int32), lane) - 1, -1)
    row = lax.broadcasted_iota(jnp.int32, a_t.shape, 0)
    pos = jnp.where(row % e_pad < n_experts, pos, -1)
    unused = jnp.full((LANES - e_pad, seq), -1.0, F32)
    for b in range(batch):
        pos_b = pos[b * e_pad:(b + 1) * e_pad]
        prow_ref[b] = pos_b
        pcol_ref[b * seq:(b + 1) * seq, :] = jnp.concatenate([pos_b.astype(F32), unused], axis=0).T


def route(aff, *, batch, seq, n_experts, capacity):
    e_pad = max(SUBLANES, n_experts)
    return pl.pallas_call(
        functools.partial(_route_kernel, n_experts=n_experts, capacity=capacity),
        grid=(1,),
        in_specs=[pl.BlockSpec((batch * seq, LANES), lambda i: (0, 0))],
        out_specs=[pl.BlockSpec((batch, e_pad, seq), lambda i: (0, 0, 0)),
                   pl.BlockSpec((batch, e_pad, seq), lambda i: (0, 0, 0)),
                   pl.BlockSpec((batch * seq, LANES), lambda i: (0, 0))],
        out_shape=[jax.ShapeDtypeStruct((batch, e_pad, seq), jnp.int32),
                   jax.ShapeDtypeStruct((batch, e_pad, seq), F32),
                   jax.ShapeDtypeStruct((batch * seq, LANES), F32)],
        compiler_params=_params(("arbitrary",), 8 * batch * seq * LANES * 4),
        name="route",
    )(aff)


GATHER_TILES = 8
COMBINE_TILES = 4


def _window_size(capacity, n_tiles):
    return capacity // n_tiles + 2 * _window_margin(capacity)


def _window_margin(capacity):
    return capacity // 8


def _window_start(tile, capacity, n_tiles, clip=jnp.clip):
    share = capacity // n_tiles
    unit = _window_margin(capacity)
    window = _window_size(capacity, n_tiles)
    assert share % unit == 0 and (capacity - window) % unit == 0
    return clip(tile * (share // unit) - 1, 0, (capacity - window) // unit) * unit


def _static_clip(v, lo, hi):
    return max(lo, min(v, hi))


def _slots_inside_windows(prow, *, n_experts, capacity, n_tiles):
    batch, _, seq = prow.shape
    pos = prow[:, :n_experts, :].reshape(batch, n_experts, n_tiles, seq // n_tiles)
    first = _window_start(jnp.arange(n_tiles, dtype=jnp.int32), capacity, n_tiles)[None, None, :]
    lowest = jnp.min(jnp.where(pos >= 0, pos, capacity), axis=-1)
    highest = jnp.max(pos, axis=-1)
    return (lowest >= first) & (highest < first + _window_size(capacity, n_tiles))


def _gather_kernel(ok_ref, h_ref, prow_ref, o_ref, *, experts_per_step, n_tiles):
    b = pl.program_id(0)
    g = pl.program_id(1)
    cap = o_ref.shape[1]
    seq = h_ref.shape[0]
    tt = seq // n_tiles
    window = _window_size(cap, n_tiles)
    starts = [_window_start(t, cap, n_tiles, _static_clip) for t in range(n_tiles)]
    unit = _window_margin(cap)
    first = g * experts_per_step
    prows = [prow_ref[0, pl.ds(first + j, 1), :] for j in range(experts_per_step)]
    in_windows = ok_ref[b, g] != 0

    @pl.when(in_windows)
    def _():
        slot = lax.broadcasted_iota(jnp.int32, (unit, 1), 0)
        for r0 in range(0, cap, unit):
            covering = [t for t in range(n_tiles) if starts[t] <= r0 < starts[t] + window]
            assert covering == list(range(covering[0], covering[-1] + 1))
            tok = slice(covering[0] * tt, (covering[-1] + 1) * tt)
            onehot = jnp.concatenate([(slot == prow[:, tok] - r0).astype(BF16) for prow in prows],
                                     axis=0)
            rows = jnp.dot(onehot, h_ref[tok, :], preferred_element_type=F32).astype(o_ref.dtype)
            for j in range(experts_per_step):
                o_ref[j, r0:r0 + unit, :] = rows[j * unit:(j + 1) * unit]

    @pl.when(jnp.logical_not(in_windows))
    def _():
        slot = lax.broadcasted_iota(jnp.int32, (cap, seq), 0)
        for j, prow in enumerate(prows):
            o_ref[j] = jnp.dot((slot == prow).astype(BF16), h_ref[...],
                               preferred_element_type=F32).astype(o_ref.dtype)


def gather_tokens(h, prow, *, batch, seq, n_experts, capacity):
    d = h.shape[1]
    e_pad = prow.shape[1]
    ng = _tile(n_experts, 8)
    ok = jnp.all(_slots_inside_windows(prow, n_experts=n_experts, capacity=capacity,
                                       n_tiles=GATHER_TILES), axis=-1)
    ok = jnp.all(ok.reshape(batch, n_experts // ng, ng), axis=-1).astype(jnp.int32)
    vmem = 2 * (seq * d * 2 + e_pad * seq * 4 + ng * capacity * d * 2) + 2 * capacity * seq * 8 \
        + 3 * capacity * d * 4
    return pl.pallas_call(
        functools.partial(_gather_kernel, experts_per_step=ng, n_tiles=GATHER_TILES),
        grid_spec=pltpu.PrefetchScalarGridSpec(
            num_scalar_prefetch=1,
            grid=(batch, n_experts // ng),
            in_specs=[pl.BlockSpec((seq, d), lambda b, g, ok_ref: (b, 0)),
                      pl.BlockSpec((1, e_pad, seq), lambda b, g, ok_ref: (b, 0, 0))],
            out_specs=pl.BlockSpec((ng, None, capacity, d), lambda b, g, ok_ref: (g, b, 0, 0))),
        out_shape=jax.ShapeDtypeStruct((n_experts, batch, capacity, d), BF16),
        compiler_params=_params(("parallel", "arbitrary"), vmem),
        name="gather_tokens",
    )(ok, h, prow)


def _ffn_kernel(x_ref, wg_ref, wu_ref, wd_ref, prow_ref, arow_ref, o_ref, acc_ref, gate_ref,
                *, capacity, gate_steps):
    e = pl.program_id(0)
    f = pl.program_id(1)
    nf = pl.num_programs(1)
    rows = x_ref.shape[0]
    n_seq = rows // capacity

    def chunk_product():
        x = x_ref[...]
        a = jnp.dot(x, wg_ref[...].astype(BF16), preferred_element_type=F32)
        u = jnp.dot(x, wu_ref[...].astype(BF16), preferred_element_type=F32)
        hidden = (a / (1.0 + jnp.exp(-a)) * u).astype(BF16)
        return jnp.dot(hidden, wd_ref[...].astype(BF16), preferred_element_type=F32)

    def gate_share():
        seq = prow_ref.shape[2]
        slot = lax.broadcasted_iota(jnp.int32, (capacity, seq), 0)
        per_step = -(-n_seq // gate_steps)
        for j in range(per_step):
            b = jnp.minimum(f * per_step + j, n_seq - 1)
            mine = slot == prow_ref[b, pl.ds(e, 1), :]
            gate_ref[pl.ds(pl.multiple_of(b * capacity, capacity), capacity), :] = jnp.sum(
                jnp.where(mine, arow_ref[b, pl.ds(e, 1), :], 0.0), axis=1, keepdims=True)

    @pl.when(f == 0)
    def _():
        gate_share()
        acc_ref[...] = chunk_product()

    @pl.when((f > 0) & (f < nf - 1))
    def _():
        gate_share()
        acc_ref[...] += chunk_product()

    @pl.when(f == nf - 1)
    def _():
        gate_share()
        o_ref[...] = ((acc_ref[...] + chunk_product()) * gate_ref[...]).astype(o_ref.dtype)


def expert_ffn(xs, w_gate, w_up, w_down, prow, arow, layer, *, rows_per_expert, capacity):
    _, n_experts, d, ff = w_gate.shape
    tf = _tile(ff // 2, 512)
    r = rows_per_expert
    batch, e_pad, seq = prow.shape
    vmem = 2 * (r * d * 2 + 3 * d * tf * 4 + r * d * 2 + 2 * batch * e_pad * seq * 4) + r * d * 4 \
        + r * LANES * 4 + 3 * d * tf * 2 + 4 * r * tf * 4 + r * d * 4 + 2 * capacity * seq * 4
    return pl.pallas_call(
        functools.partial(_ffn_kernel, capacity=capacity, gate_steps=ff // tf),
        grid=(n_experts, ff // tf),
        in_specs=[pl.BlockSpec((r, d), lambda e, f: (e, 0)),
                  pl.BlockSpec((None, None, d, tf), lambda e, f: (layer, e, 0, f)),
                  pl.BlockSpec((None, None, d, tf), lambda e, f: (layer, e, 0, f)),
                  pl.BlockSpec((None, None, tf, d), lambda e, f: (layer, e, f, 0)),
                  pl.BlockSpec((batch, e_pad, seq), lambda e, f: (0, 0, 0)),
                  pl.BlockSpec((batch, e_pad, seq), lambda e, f: (0, 0, 0))],
        out_specs=pl.BlockSpec((r, d), lambda e, f: (e, 0)),
        out_shape=jax.ShapeDtypeStruct((n_experts * r, d), BF16),
        scratch_shapes=[pltpu.VMEM((r, d), F32), pltpu.VMEM((r, 1), F32)],
        compiler_params=_params(("parallel", "arbitrary"), vmem),
        name="expert_ffn",
    )(xs, w_gate, w_up, w_down, prow, arow)


def _combine_kernel(ok_ref, x_ref, y_ref, pcol_ref, g_ref, out_ref, *, n_group, n_tiles, final_norm):
    b = pl.program_id(0)
    t = pl.program_id(1)
    tt, d = x_ref.shape
    n_experts, cap, _ = y_ref.shape
    window = _window_size(cap, n_tiles)
    lane = lax.broadcasted_iota(jnp.int32, (tt, LANES), 1)
    pcol = pcol_ref[...]

    def scatter(rows_of, n_slots, first_slot):
        slot = (lax.broadcasted_iota(jnp.int32, (tt, n_slots), 1) + first_slot).astype(F32)
        acc = x_ref[...]
        for e0 in range(0, n_experts, n_group):
            hots = []
            for e in range(e0, e0 + n_group):
                slot_of_token = jnp.sum(jnp.where(lane == e, pcol, 0.0),
                                        axis=1, keepdims=True)
                hots.append((slot == slot_of_token).astype(BF16))
            acc = acc + jnp.dot(jnp.concatenate(hots, axis=1), rows_of(e0),
                                preferred_element_type=F32)
        out_ref[...] = _rms(acc, g_ref[...]) if final_norm else acc

    in_window = ok_ref[b, t] != 0

    @pl.when(in_window)
    def _():
        first = pl.multiple_of(_window_start(t, cap, n_tiles), _window_margin(cap))
        scatter(lambda e0: jnp.concatenate([y_ref[e, pl.ds(first, window), :]
                                            for e in range(e0, e0 + n_group)], axis=0),
                window, first)

    @pl.when(jnp.logical_not(in_window))
    def _():
        scatter(lambda e0: y_ref[e0:e0 + n_group].reshape(n_group * cap, d), cap, 0)


def combine(x_res, y, prow, pcol, gain, *, batch, seq, n_experts, capacity, final_norm):
    d = x_res.shape[1]
    nt = COMBINE_TILES
    tt = seq // nt
    ng = _tile(n_experts, 8)
    y4 = y.reshape(n_experts, batch, capacity, d)
    inside = _slots_inside_windows(prow, n_experts=n_experts, capacity=capacity, n_tiles=nt)
    ok = jnp.all(inside, axis=1).astype(jnp.int32)
    vmem = 2 * (2 * tt * d * 4 + n_experts * capacity * d * 2 + tt * LANES * 4) \
        + 3 * tt * d * 4 + ng * capacity * d * 2 + 2 * tt * ng * capacity * 4
    return pl.pallas_call(
        functools.partial(_combine_kernel, n_group=ng, n_tiles=nt, final_norm=final_norm),
        grid_spec=pltpu.PrefetchScalarGridSpec(
            num_scalar_prefetch=1,
            grid=(batch, nt),
            in_specs=[pl.BlockSpec((tt, d), lambda b, t, ok_ref: (b * nt + t, 0)),
                      pl.BlockSpec((n_experts, None, capacity, d), lambda b, t, ok_ref: (0, b, 0, 0)),
                      pl.BlockSpec((tt, LANES), lambda b, t, ok_ref: (b * nt + t, 0)),
                      pl.BlockSpec((1, d), lambda b, t, ok_ref: (0, 0))],
            out_specs=pl.BlockSpec((tt, d), lambda b, t, ok_ref: (b * nt + t, 0))),
        out_shape=jax.ShapeDtypeStruct((batch * seq, d), F32),
        compiler_params=_params(("parallel", "arbitrary"), vmem),
        name="combine",
    )(ok, x_res, y4, pcol, gain.reshape(1, d))


def _pool_kernel(x_ref, prev_ref, next_ref, ng_ref, w_ref, s_ref, g_ref, xo_ref, ho_ref, pad_ref,
                 *, seq, windows):
    i = pl.program_id(0)
    tm, d = x_ref.shape
    halo = prev_ref.shape[0]
    pg = w_ref.shape[1]
    tiles_per_seq = seq // tm
    tile_in_seq = i % tiles_per_seq
    norm_gain = ng_ref[...]
    h = _rms(x_ref[...], norm_gain)
    pad_ref[0:halo, :] = jnp.where(tile_in_seq == 0, 0.0, _rms(prev_ref[...], norm_gain))
    pad_ref[halo:halo + tm, :] = h
    pad_ref[halo + tm:2 * halo + tm, :] = jnp.where(tile_in_seq == tiles_per_seq - 1, 0.0,
                                                    _rms(next_ref[...], norm_gain))
    t = tile_in_seq * tm + lax.broadcasted_iota(jnp.int32, (tm, 1), 0)
    for gi, w in enumerate(windows):
        cols = slice(gi * pg, (gi + 1) * pg)
        acc = pad_ref[:, cols]
        span = 1
        while span < w:
            acc = acc + pltpu.roll(acc, acc.shape[0] - span, axis=0)
            span *= 2
        total = acc[halo - w // 2:halo - w // 2 + tm, :]
        count = (jnp.minimum(t + (w - w // 2), seq) - jnp.maximum(t - w // 2, 0)).astype(F32)
        pooled = (total / count - h[:, cols]).astype(BF16)
        mixed = jnp.dot(pooled, w_ref[gi], preferred_element_type=F32)
        xo_ref[:, cols] = mixed * s_ref[:, cols] + x_ref[:, cols]
    ho_ref[...] = _rms(xo_ref[...], g_ref[...]).astype(ho_ref.dtype)


def pool_mixer(x, norm_gain, w_pool, pool_scale, next_gain, *, seq):
    m, d = x.shape
    ng, pg, _ = w_pool.shape
    halo = SUBLANES
    assert all(w & (w - 1) == 0 and w // 2 <= halo for w in POOL_WINDOWS) and ng == len(POOL_WINDOWS)
    tm = _tile(seq, 1024)
    per = tm // halo
    n_halo_blocks = m // halo
    vmem = 2 * (2 * tm * d * 4 + 2 * halo * d * 4 + ng * pg * pg * 2 + tm * d * 2) \
        + (tm + 2 * halo) * d * 4 + 6 * tm * d * 4
    return pl.pallas_call(
        functools.partial(_pool_kernel, seq=seq, windows=POOL_WINDOWS),
        grid=(m // tm,),
        in_specs=[pl.BlockSpec((tm, d), lambda i: (i, 0)),
                  pl.BlockSpec((halo, d), lambda i: (jnp.maximum(i * per - 1, 0), 0)),
                  pl.BlockSpec((halo, d), lambda i: (jnp.minimum((i + 1) * per, n_halo_blocks - 1), 0)),
                  pl.BlockSpec((1, d), lambda i: (0, 0)),
                  pl.BlockSpec((ng, pg, pg), lambda i: (0, 0, 0)),
                  pl.BlockSpec((1, d), lambda i: (0, 0)),
                  pl.BlockSpec((1, d), lambda i: (0, 0))],
        out_specs=[pl.BlockSpec((tm, d), lambda i: (i, 0)),
                   pl.BlockSpec((tm, d), lambda i: (i, 0))],
        out_shape=[jax.ShapeDtypeStruct((m, d), F32), jax.ShapeDtypeStruct((m, d), BF16)],
        scratch_shapes=[pltpu.VMEM((tm + 2 * halo, d), F32)],
        compiler_params=_params(("parallel",), vmem),
        name="pool_mixer",
    )(x, x, x, norm_gain.reshape(1, d), w_pool, pool_scale.reshape(1, d), next_gain.reshape(1, d))


def _rope_tables(seq):
    half = HEAD_DIM // 2
    rows = seq // GRID_W
    row_idx = jnp.repeat(jnp.arange(rows), GRID_W).astype(F32)
    col_idx = jnp.tile(jnp.arange(GRID_W), rows).astype(F32)
    inv_freq = 1.0 / (ROPE_THETA ** (jnp.arange(0, half, 2, dtype=F32) / half))
    ang = jnp.concatenate([row_idx[:, None] * inv_freq[None, :],
                           col_idx[:, None] * inv_freq[None, :]], axis=-1)
    cos, sin = jnp.cos(ang), jnp.sin(ang)
    zero = jnp.zeros_like(sin)
    cos_full = jnp.stack([cos, cos], axis=-1).reshape(seq, HEAD_DIM)
    sin_even = jnp.stack([-sin, zero], axis=-1).reshape(seq, HEAD_DIM)
    sin_odd = jnp.stack([zero, sin], axis=-1).reshape(seq, HEAD_DIM)
    return cos_full, sin_even, sin_odd


def _split_router(w_router):
    d, n_experts = w_router.shape
    w = jnp.pad(w_router, ((0, 0), (0, LANES - n_experts)))
    hi = w.astype(BF16)
    lo = (w - hi.astype(F32)).astype(BF16)
    return jnp.concatenate([hi, lo], axis=1), n_experts


def _moe(x_res, h, aff, w_gate, w_up, w_down, layer, gain, *, batch, seq, final_norm):
    n_experts = w_gate.shape[1]
    capacity = EC_CAPACITY_FACTOR * seq // n_experts
    d = h.shape[1]
    prow, arow, pcol = route(aff, batch=batch, seq=seq, n_experts=n_experts, capacity=capacity)
    xs = gather_tokens(h, prow, batch=batch, seq=seq, n_experts=n_experts, capacity=capacity)
    y = expert_ffn(xs.reshape(n_experts * batch * capacity, d), w_gate, w_up, w_down, prow, arow,
                   layer, rows_per_expert=batch * capacity, capacity=capacity)
    return combine(x_res, y, prow, pcol, gain, batch=batch, seq=seq,
                   n_experts=n_experts, capacity=capacity, final_norm=final_norm)


def _cross_block(x, h, mem2d, layer, cross_w_q, cross_w_k, cross_w_v, cross_w_o, mem_norm,
                 ffn_norm, router_w, *, batch, seq, mem_len):
    mem_n = rms_norm_rows(mem2d, mem_norm[layer], BF16)
    k = matmul_streamed(mem_n, cross_w_k, (layer,))
    v = matmul_streamed(mem_n, cross_w_v, (layer,))
    return cross_block(h, cast_weight(cross_w_q, (layer,)), k, v, cast_weight(cross_w_o, (layer,)),
                       x, ffn_norm[layer], _split_router(router_w[layer]),
                       batch=batch, seq=seq, mem_len=mem_len)


def kernel(x, mem, mix_norm, attn_w_in, q_gain, k_gain, fourier_w, attn_w_out, pool_w, pool_scale,
           cross_norm, mem_norm, cross_w_q, cross_w_k, cross_w_v, cross_w_o, ffn_norm, router_w,
           expert_w_gate, expert_w_up, expert_w_down, final_norm):
    batch, seq, d = x.shape
    mem_len = mem.shape[1]
    depth = mix_norm.shape[0]
    attn_width = N_Q_HEADS * HEAD_DIM
    kv_width = N_KV_HEADS * HEAD_DIM
    fourier_width = N_FOURIER_GROUPS * FOURIER_GROUP
    rope_width = attn_width + kv_width
    cos_t, sin_a, sin_b = _rope_tables(seq)

    xf = x.reshape(batch * seq, d)
    mem2d = mem.reshape(batch * mem_len, d)
    cross = functools.partial(_cross_block, mem2d=mem2d, cross_w_q=cross_w_q, cross_w_k=cross_w_k,
                              cross_w_v=cross_w_v, cross_w_o=cross_w_o, mem_norm=mem_norm,
                              ffn_norm=ffn_norm, router_w=router_w,
                              batch=batch, seq=seq, mem_len=mem_len)
    for layer in range(depth):
        i = layer // 2
        if layer % 2 == 0:
            gain_cols = jnp.concatenate([jnp.tile(q_gain[i], N_Q_HEADS), jnp.tile(k_gain[i], N_KV_HEADS),
                                         jnp.ones((kv_width + fourier_width,), F32)]).reshape(1, -1)
            proj = proj_in(xf, mix_norm[layer], cast_weight(attn_w_in, (i,)), gain_cols,
                           cos_t, sin_a, sin_b, seq=seq, rope_width=rope_width)
            o_attn = gqa_attention(proj, batch=batch, seq=seq, attn_width=attn_width, kv_width=kv_width)
            o_four = fourier_mixer(proj, fourier_w[i], batch=batch, seq=seq,
                                   col0=attn_width + 2 * kv_width)
            xf, h = matmul_residual_norm([o_attn, o_four], cast_weight(attn_w_out, (i,)), xf,
                                         cross_norm[layer])
        else:
            xf, h = pool_mixer(xf, mix_norm[layer], pool_w[i].astype(BF16), pool_scale[i],
                               cross_norm[layer], seq=seq)
        xf, h, aff = cross(xf, h, layer=layer)
        xf = _moe(xf, h, aff, expert_w_gate, expert_w_up, expert_w_down, layer, final_norm,
                  batch=batch, seq=seq, final_norm=layer == depth - 1)
    return xf.reshape(batch, seq, d)
```

```python
import functools
import math

import jax
import jax.numpy as jnp
from jax import lax
from jax.experimental import pallas as pl
from jax.experimental.pallas import tpu as pltpu

F32 = jnp.float32
BF16 = jnp.bfloat16

GRID_W = 64
HEAD_DIM = 128
N_Q_HEADS = 12
N_KV_HEADS = 4
N_FOURIER_GROUPS = 4
FOURIER_GROUP = 128
ROPE_THETA = 10000.0
POOL_WINDOWS = (2, 4, 8, 16)
N_MEM_HEADS = 4
EC_CAPACITY_FACTOR = 2
NORM_EPS = 1e-6

LANES = 128
SUBLANES = 8
V7X_VMEM_BYTES = 64 * 1024 * 1024
VMEM_HEADROOM_BYTES = 6 * 1024 * 1024
COL_CHUNK = 512
LOG2_E = math.log2(math.e)


def _params(semantics, vmem_bytes):
    limit = min(int(vmem_bytes) + VMEM_HEADROOM_BYTES, V7X_VMEM_BYTES - VMEM_HEADROOM_BYTES)
    return pltpu.CompilerParams(dimension_semantics=semantics, vmem_limit_bytes=limit)


def _tile(n, pref):
    t = min(n, pref)
    while n % t:
        t //= 2
    return t


def _rms(x, gain):
    ms = jnp.mean(x * x, axis=-1, keepdims=True)
    return x * lax.rsqrt(ms + NORM_EPS) * gain


def _softmax_numerators(scores, scale):
    c = scale * LOG2_E
    return [jnp.exp2((s - jnp.max(s, axis=-1, keepdims=True)) * c) for s in scores]


def _cast_kernel(w_ref, o_ref):
    o_ref[...] = w_ref[...].astype(o_ref.dtype)


def cast_weight(w, lead):
    k, n = w.shape[-2:]
    tk = _tile(k, 1024)
    squeezed = (None,) * len(lead)
    return pl.pallas_call(
        _cast_kernel,
        grid=(k // tk,),
        in_specs=[pl.BlockSpec(squeezed + (tk, n), lambda i: tuple(lead) + (i, 0))],
        out_specs=pl.BlockSpec((tk, n), lambda i: (i, 0)),
        out_shape=jax.ShapeDtypeStruct((k, n), BF16),
        compiler_params=_params(("parallel",), 2 * tk * n * 6),
        name="cast_weight",
    )(w)


def _memory_kv_kernel(mem_ref, g_ref, wk_ref, wv_ref, o_ref, a_ref):
    s = pl.program_id(1)
    j = pl.program_id(2)

    @pl.when((s == 0) & (j == 0))
    def _():
        a_ref[...] = _rms(mem_ref[...], g_ref[...]).astype(a_ref.dtype)

    @pl.when(s == 0)
    def _():
        o_ref[...] = jnp.dot(a_ref[...], wk_ref[...].astype(BF16),
                             preferred_element_type=F32).astype(o_ref.dtype)

    @pl.when(s != 0)
    def _():
        o_ref[...] = jnp.dot(a_ref[...], wv_ref[...].astype(BF16),
                             preferred_element_type=F32).astype(o_ref.dtype)


def memory_kv(mem2d, mem_norm, w_k, w_v):
    m, d = mem2d.shape
    depth, k, n = w_k.shape
    tn = _tile(n, COL_CHUNK)
    nj = n // tn
    vmem = m * d * 4 + 4 * k * tn * 4 + 2 * m * tn * 2 + m * d * 2 + k * tn * 2 + m * tn * 4 \
        + 2 * m * d * 4
    return pl.pallas_call(
        _memory_kv_kernel,
        grid=(depth, 2, nj),
        in_specs=[pl.BlockSpec((m, d), lambda l, s, j: (0, 0), pipeline_mode=pl.Buffered(1)),
                  pl.BlockSpec((None, 1, d), lambda l, s, j: (l, 0, 0)),
                  pl.BlockSpec((None, k, tn), lambda l, s, j: (l, 0, jnp.where(s == 0, j, nj - 1))),
                  pl.BlockSpec((None, k, tn), lambda l, s, j: (l, 0, jnp.where(s == 0, 0, j)))],
        out_specs=pl.BlockSpec((None, None, m, tn), lambda l, s, j: (l, s, 0, j)),
        out_shape=jax.ShapeDtypeStruct((depth, 2, m, n), BF16),
        scratch_shapes=[pltpu.VMEM((m, d), BF16)],
        compiler_params=_params(("arbitrary", "arbitrary", "arbitrary"), vmem),
        name="memory_kv",
    )(mem2d, mem_norm.reshape(depth, 1, d), w_k, w_v)


def _proj_in_kernel(a_ref, ng_ref, w_ref, g_ref, cos_ref, sa_ref, sb_ref, o_ref, *, n_rope_heads):
    n = o_ref.shape[1]
    tn = _tile(n, COL_CHUNK)
    a = _rms(a_ref[...], ng_ref[...]).astype(BF16)
    cos, sin_a, sin_b = cos_ref[...], sa_ref[...], sb_ref[...]
    chunks = [jnp.dot(a, w_ref[:, c * tn:(c + 1) * tn], preferred_element_type=F32)
              for c in range(n // tn)]
    for c, acc in enumerate(chunks):
        for h in range(tn // HEAD_DIM):
            head = c * (tn // HEAD_DIM) + h
            col = slice(head * HEAD_DIM, (head + 1) * HEAD_DIM)
            xh = acc[:, h * HEAD_DIM:(h + 1) * HEAD_DIM]
            if head < n_rope_heads:
                ms = jnp.mean(xh * xh, axis=-1, keepdims=True)
                y = xh * lax.rsqrt(ms + NORM_EPS) * g_ref[:, col]
                xh = (y * cos + pltpu.roll(y, HEAD_DIM - 1, axis=1) * sin_a
                      + pltpu.roll(y, 1, axis=1) * sin_b)
            o_ref[:, col] = xh.astype(o_ref.dtype)


def proj_in(x, norm_gain, w_in, gain_cols, cos_t, sin_a, sin_b, *, seq, rope_width):
    m, k = x.shape
    n = w_in.shape[1]
    tm = _tile(seq, 512)
    n_pos_blocks = seq // tm
    vmem = 2 * (tm * k * 4 + k * n * 2 + tm * n * 2 + 3 * tm * HEAD_DIM * 4) + 6 * tm * COL_CHUNK * 4 \
        + 2 * tm * k * 4
    table = pl.BlockSpec((tm, HEAD_DIM), lambda i: (i % n_pos_blocks, 0))
    return pl.pallas_call(
        functools.partial(_proj_in_kernel, n_rope_heads=rope_width // HEAD_DIM),
        grid=(m // tm,),
        in_specs=[pl.BlockSpec((tm, k), lambda i: (i, 0)),
                  pl.BlockSpec((1, k), lambda i: (0, 0)),
                  pl.BlockSpec((k, n), lambda i: (0, 0)),
                  pl.BlockSpec((1, n), lambda i: (0, 0)),
                  table, table, table],
        out_specs=pl.BlockSpec((tm, n), lambda i: (i, 0)),
        out_shape=jax.ShapeDtypeStruct((m, n), BF16),
        compiler_params=_params(("parallel",), vmem),
        name="proj_in",
    )(x, norm_gain.reshape(1, k), w_in, gain_cols, cos_t, sin_a, sin_b)


def _attn_kernel(q_ref, k_ref, v_ref, o_ref, *, group, scale):
    k = k_ref[...]
    v = v_ref[...]
    v_ones = jnp.concatenate([v, jnp.ones_like(v)], axis=1)
    heads = [slice(g * HEAD_DIM, (g + 1) * HEAD_DIM) for g in range(group)]
    scores = [lax.dot_general(q_ref[:, sl], k, (((1,), (1,)), ((), ())),
                              preferred_element_type=F32) for sl in heads]
    for sl, e in zip(heads, _softmax_numerators(scores, scale)):
        ol = jnp.dot(e.astype(BF16), v_ones, preferred_element_type=F32)
        o_ref[:, sl] = (ol[:, :HEAD_DIM] / ol[:, HEAD_DIM:HEAD_DIM + 1]).astype(o_ref.dtype)


def gqa_attention(proj, *, batch, seq, attn_width, kv_width):
    group = N_Q_HEADS // N_KV_HEADS
    gw = group * HEAD_DIM
    tq = _tile(seq, 1024)
    nq = seq // tq
    k_col0 = attn_width // HEAD_DIM
    v_col0 = (attn_width + kv_width) // HEAD_DIM
    vmem = 2 * (tq * gw * 2 * 2 + 2 * seq * HEAD_DIM * 2) + group * tq * seq * 6
    return pl.pallas_call(
        functools.partial(_attn_kernel, group=group, scale=HEAD_DIM ** -0.5),
        grid=(batch, N_KV_HEADS, nq),
        in_specs=[pl.BlockSpec((tq, gw), lambda b, h, i: (b * nq + i, h)),
                  pl.BlockSpec((seq, HEAD_DIM), lambda b, h, i: (b, k_col0 + h)),
                  pl.BlockSpec((seq, HEAD_DIM), lambda b, h, i: (b, v_col0 + h))],
        out_specs=pl.BlockSpec((tq, gw), lambda b, h, i: (b * nq + i, h)),
        out_shape=jax.ShapeDtypeStruct((batch * seq, attn_width), BF16),
        compiler_params=_params(("parallel", "parallel", "parallel"), vmem),
        name="gqa_attention",
    )(proj, proj, proj)


def _fourier_kernel(f_ref, cc_ref, sc_ref, cs_ref, ss_ref, wf_ref, o_ref, ab_ref, *, seq, norm):
    r = pl.program_id(1)
    ng = wf_ref.shape[0]
    cg = wf_ref.shape[1]

    @pl.when(r == 0)
    def _():
        for g in range(ng):
            fg = f_ref[:, g * cg:(g + 1) * cg]
            ab_ref[0:seq, g * cg:(g + 1) * cg] = jnp.dot(
                fg, cc_ref[...], preferred_element_type=F32).astype(BF16)
            ab_ref[seq:2 * seq, g * cg:(g + 1) * cg] = jnp.dot(
                fg, sc_ref[...], preferred_element_type=F32).astype(BF16)

    z = (jnp.dot(cs_ref[...], ab_ref[0:seq, :], preferred_element_type=F32)
         - jnp.dot(ss_ref[...], ab_ref[seq:2 * seq, :], preferred_element_type=F32)) * norm
    for g in range(ng):
        zg = z[:, g * cg:(g + 1) * cg].astype(BF16)
        o_ref[:, g * cg:(g + 1) * cg] = jnp.dot(
            zg, wf_ref[g].astype(BF16), preferred_element_type=F32).astype(o_ref.dtype)


def _dft_cos_sin(n):
    lo = _tile(n, 64)
    hi = n // lo
    j = jnp.arange(n, dtype=jnp.int32)

    def table(k):
        ang = ((k[:, None] * j[None, :]) % n).astype(F32) * (2.0 * math.pi / n)
        return jnp.cos(ang), jnp.sin(ang)

    ca, sa = table(jnp.arange(hi, dtype=jnp.int32) * lo)
    cb, sb = table(jnp.arange(lo, dtype=jnp.int32))
    cos = ca[:, None, :] * cb[None, :, :] - sa[:, None, :] * sb[None, :, :]
    sin = sa[:, None, :] * cb[None, :, :] + ca[:, None, :] * sb[None, :, :]
    return cos.reshape(n, n).astype(BF16), sin.reshape(n, n).astype(BF16)


def fourier_mixer(proj, w_fourier, *, batch, seq, col0):
    ng, cg, _ = w_fourier.shape
    fw = ng * cg
    tr = _tile(seq, 1024)
    nr = seq // tr
    cc, sc = _dft_cos_sin(cg)
    cs, ss = _dft_cos_sin(seq)
    norm = 1.0 / math.sqrt(seq * cg)
    vmem = 2 * (seq * fw * 2 + 2 * cg * cg * 2 + 2 * tr * seq * 2 + ng * cg * cg * 4 + tr * fw * 2) \
        + 2 * seq * fw * 2 + 4 * tr * fw * 4
    return pl.pallas_call(
        functools.partial(_fourier_kernel, seq=seq, norm=norm),
        grid=(batch, nr),
        in_specs=[pl.BlockSpec((seq, fw), lambda b, r: (b, col0 // fw)),
                  pl.BlockSpec((cg, cg), lambda b, r: (0, 0)),
                  pl.BlockSpec((cg, cg), lambda b, r: (0, 0)),
                  pl.BlockSpec((tr, seq), lambda b, r: (r, 0)),
                  pl.BlockSpec((tr, seq), lambda b, r: (r, 0)),
                  pl.BlockSpec((ng, cg, cg), lambda b, r: (0, 0, 0))],
        out_specs=pl.BlockSpec((tr, fw), lambda b, r: (b * nr + r, 0)),
        out_shape=jax.ShapeDtypeStruct((batch * seq, fw), BF16),
        scratch_shapes=[pltpu.VMEM((2 * seq, fw), BF16)],
        compiler_params=_params(("parallel", "arbitrary"), vmem),
        name="fourier_mixer",
    )(proj, cc, sc, cs, ss, w_fourier)


def _router_affinity(hn, wr_ref, n_experts):
    hi = hn.astype(BF16)
    lo = (hn - hi.astype(F32)).astype(BF16)
    r_hi = jnp.dot(hi, wr_ref[...], preferred_element_type=F32)
    r_lo = jnp.dot(lo, wr_ref[...], preferred_element_type=F32)
    logits = r_hi[:, :LANES] + r_hi[:, LANES:] + r_lo[:, :LANES]
    lane = lax.broadcasted_iota(jnp.int32, logits.shape, 1)
    logits = jnp.where(lane < n_experts, logits, -jnp.inf)
    e = jnp.exp(logits - jnp.max(logits, axis=-1, keepdims=True))
    return e / jnp.sum(e, axis=-1, keepdims=True)


def _mm_res_norm_kernel(*refs, n_a, with_router, n_experts):
    a_refs = refs[:n_a]
    w_ref, x_ref, g_ref = refs[n_a:n_a + 3]
    pos = n_a + 3
    if with_router:
        wr_ref = refs[pos]
        pos += 1
    xo_ref, ho_ref = refs[pos:pos + 2]
    if with_router:
        aff_ref = refs[pos + 2]
    d = xo_ref.shape[1]
    tn = _tile(d, COL_CHUNK)
    for c in range(d // tn):
        sl = slice(c * tn, (c + 1) * tn)
        acc = x_ref[:, sl]
        row0 = 0
        for a_ref in a_refs:
            ka = a_ref.shape[1]
            acc = acc + jnp.dot(a_ref[...], w_ref[row0:row0 + ka, sl], preferred_element_type=F32)
            row0 += ka
        xo_ref[:, sl] = acc
    hn = _rms(xo_ref[...], g_ref[...])
    ho_ref[...] = hn.astype(ho_ref.dtype)
    if with_router:
        aff_ref[...] = _router_affinity(hn, wr_ref, n_experts)


def matmul_residual_norm(a_list, w, x_res, gain, *, router=None):
    m, d = x_res.shape
    k = w.shape[0]
    tm = _tile(m, 512)
    in_specs = [pl.BlockSpec((tm, a.shape[1]), lambda i: (i, 0)) for a in a_list]
    in_specs += [pl.BlockSpec((k, d), lambda i: (0, 0)),
                 pl.BlockSpec((tm, d), lambda i: (i, 0)),
                 pl.BlockSpec((1, d), lambda i: (0, 0))]
    args = list(a_list) + [w, x_res, gain.reshape(1, d)]
    out_specs = [pl.BlockSpec((tm, d), lambda i: (i, 0)),
                 pl.BlockSpec((tm, d), lambda i: (i, 0))]
    out_shape = [jax.ShapeDtypeStruct((m, d), F32), jax.ShapeDtypeStruct((m, d), BF16)]
    n_experts = 0
    if router is not None:
        wr, n_experts = router
        in_specs.append(pl.BlockSpec((d, 2 * LANES), lambda i: (0, 0)))
        args.append(wr)
        out_specs.append(pl.BlockSpec((tm, LANES), lambda i: (i, 0)))
        out_shape.append(jax.ShapeDtypeStruct((m, LANES), F32))
    vmem = 2 * (tm * k * 2 + k * d * 2 + tm * d * 4 + tm * d * 4 + tm * d * 2) \
        + 3 * tm * d * 4 + 4 * d * LANES * 2
    return pl.pallas_call(
        functools.partial(_mm_res_norm_kernel, n_a=len(a_list),
                          with_router=router is not None, n_experts=n_experts),
        grid=(m // tm,),
        in_specs=in_specs,
        out_specs=out_specs,
        out_shape=out_shape,
        compiler_params=_params(("parallel",), vmem),
        name="matmul_residual_norm",
    )(*args)


def _cross_block_kernel(h_ref, wq_ref, k_ref, v_ref, wo_ref, x_ref, g_ref, wr_ref,
                        xo_ref, ho_ref, aff_ref, o_ref, *, n_heads, scale, n_experts):
    dh = h_ref.shape[1] // n_heads
    h = h_ref[...]
    heads = [slice(i * dh, (i + 1) * dh) for i in range(n_heads)]
    qs = [jnp.dot(h, wq_ref[:, sl], preferred_element_type=F32).astype(BF16) for sl in heads]
    scores = [lax.dot_general(q, k_ref[:, sl], (((1,), (1,)), ((), ())),
                              preferred_element_type=F32) for q, sl in zip(qs, heads)]
    for sl, e in zip(heads, _softmax_numerators(scores, scale)):
        l = jnp.sum(e, axis=-1, keepdims=True)
        o = jnp.dot(e.astype(BF16), v_ref[:, sl], preferred_element_type=F32) / l
        o_ref[:, sl] = o.astype(o_ref.dtype)
    d = xo_ref.shape[1]
    tn = _tile(d, COL_CHUNK)
    o = o_ref[...]
    for c in range(d // tn):
        sl = slice(c * tn, (c + 1) * tn)
        xo_ref[:, sl] = x_ref[:, sl] + jnp.dot(o, wo_ref[:, sl], preferred_element_type=F32)
    hn = _rms(xo_ref[...], g_ref[...])
    ho_ref[...] = hn.astype(ho_ref.dtype)
    aff_ref[...] = _router_affinity(hn, wr_ref, n_experts)


def cross_block(h, w_q, kv, layer, w_o, x_res, gain, router, *, batch, seq, mem_len):
    m, d = x_res.shape
    wr, n_experts = router
    tq = _tile(seq, 512)
    nq = seq // tq
    once = pl.Buffered(1)
    vmem = 2 * (tq * d * 2 + 2 * mem_len * d * 2 + 2 * tq * d * 4 + tq * d * 2 + tq * LANES * 4) \
        + 2 * d * d * 2 + 2 * d * LANES * 2 + tq * d * 2 + 5 * tq * d * 4
    row = lambda b, i: (b * nq + i, 0)
    fixed = lambda b, i: (0, 0)
    return pl.pallas_call(
        functools.partial(_cross_block_kernel, n_heads=N_MEM_HEADS,
                          scale=(d // N_MEM_HEADS) ** -0.5, n_experts=n_experts),
        grid=(batch, nq),
        in_specs=[pl.BlockSpec((tq, d), row),
                  pl.BlockSpec((d, d), fixed, pipeline_mode=once),
                  pl.BlockSpec((None, None, mem_len, d), lambda b, i: (layer, 0, b, 0)),
                  pl.BlockSpec((None, None, mem_len, d), lambda b, i: (layer, 1, b, 0)),
                  pl.BlockSpec((d, d), fixed, pipeline_mode=once),
                  pl.BlockSpec((tq, d), row),
                  pl.BlockSpec((1, d), fixed),
                  pl.BlockSpec((d, 2 * LANES), fixed, pipeline_mode=once)],
        out_specs=[pl.BlockSpec((tq, d), row),
                   pl.BlockSpec((tq, d), row),
                   pl.BlockSpec((tq, LANES), row)],
        out_shape=[jax.ShapeDtypeStruct((m, d), F32), jax.ShapeDtypeStruct((m, d), BF16),
                   jax.ShapeDtypeStruct((m, LANES), F32)],
        scratch_shapes=[pltpu.VMEM((tq, d), BF16)],
        compiler_params=_params(("parallel", "parallel"), vmem),
        name="cross_block",
    )(h, w_q, kv, kv, w_o, x_res, gain.reshape(1, d), wr)


def _prefix_count(x, lane):
    n = x.shape[1]
    shift = 1
    while shift < n:
        x = x + jnp.where(lane >= shift, pltpu.roll(x, shift, axis=1), 0)
        shift *= 2
    return x


def _route_kernel(aff_ref, prow_ref, arow_ref, pcol_ref, *, n_experts, capacity):
    batch, e_pad, seq = prow_ref.shape
    a_t = jnp.concatenate([aff_ref[b * seq:(b + 1) * seq, :].T[:e_pad] for b in range(batch)],
                          axis=0)
    for b in range(batch):
        arow_ref[b] = a_t[b * e_pad:(b + 1) * e_pad]

    def body(i, thr_bits):
        cand = thr_bits | jnp.left_shift(jnp.int32(1), 30 - i)
        cnt = jnp.sum((a_t >= pltpu.bitcast(cand, F32)).astype(F32), axis=1, keepdims=True)
        return jnp.where(cnt >= capacity, cand, thr_bits)

    thr = pltpu.bitcast(lax.fori_loop(0, 31, body, jnp.zeros((batch * e_pad, 1), jnp.int32)), F32)
    lane = lax.broadcasted_iota(jnp.int32, a_t.shape, 1)
    gt = a_t > thr
    eq = a_t == thr
    need = capacity - jnp.sum(gt.astype(F32), axis=1, keepdims=True).astype(jnp.int32)
    sel = gt | (eq & (_prefix_count(eq.astype(jnp.int32), lane) <= need))
    pos = jnp.where(sel, _prefix_count(sel.astype(jnp.int32), lane) - 1, -1)
    row = lax.broadcasted_iota(jnp.int32, a_t.shape, 0)
    pos = jnp.where(row % e_pad < n_experts, pos, -1)
    unused = jnp.full((LANES - e_pad, seq), -1.0, F32)
    for b in range(batch):
        pos_b = pos[b * e_pad:(b + 1) * e_pad]
        prow_ref[b] = pos_b
        pcol_ref[b * seq:(b + 1) * seq, :] = jnp.concatenate([pos_b.astype(F32), unused], axis=0).T


def route(aff, *, batch, seq, n_experts, capacity):
    e_pad = max(SUBLANES, n_experts)
    return pl.pallas_call(
        functools.partial(_route_kernel, n_experts=n_experts, capacity=capacity),
        grid=(1,),
        in_specs=[pl.BlockSpec((batch * seq, LANES), lambda i: (0, 0))],
        out_specs=[pl.BlockSpec((batch, e_pad, seq), lambda i: (0, 0, 0)),
                   pl.BlockSpec((batch, e_pad, seq), lambda i: (0, 0, 0)),
                   pl.BlockSpec((batch * seq, LANES), lambda i: (0, 0))],
        out_shape=[jax.ShapeDtypeStruct((batch, e_pad, seq), jnp.int32),
                   jax.ShapeDtypeStruct((batch, e_pad, seq), F32),
                   jax.ShapeDtypeStruct((batch * seq, LANES), F32)],
        compiler_params=_params(("arbitrary",), 8 * batch * seq * LANES * 4),
        name="route",
    )(aff)


GATHER_TILES = 8
COMBINE_TILES = 4


def _window_size(capacity, n_tiles):
    return capacity // n_tiles + 2 * _window_margin(capacity)


def _window_margin(capacity):
    return capacity // 8


def _window_start(tile, capacity, n_tiles, clip=jnp.clip):
    share = capacity // n_tiles
    unit = _window_margin(capacity)
    window = _window_size(capacity, n_tiles)
    assert share % unit == 0 and (capacity - window) % unit == 0
    return clip(tile * (share // unit) - 1, 0, (capacity - window) // unit) * unit


def _static_clip(v, lo, hi):
    return max(lo, min(v, hi))


def _slots_inside_windows(prow, *, n_experts, capacity, n_tiles):
    batch, _, seq = prow.shape
    pos = prow[:, :n_experts, :].reshape(batch, n_experts, n_tiles, seq // n_tiles)
    first = _window_start(jnp.arange(n_tiles, dtype=jnp.int32), capacity, n_tiles)[None, None, :]
    lowest = jnp.min(jnp.where(pos >= 0, pos, capacity), axis=-1)
    highest = jnp.max(pos, axis=-1)
    return (lowest >= first) & (highest < first + _window_size(capacity, n_tiles))


def _gather_kernel(ok_ref, h_ref, prow_ref, o_ref, *, experts_per_step, n_tiles):
    b = pl.program_id(0)
    g = pl.program_id(1)
    cap = o_ref.shape[1]
    seq = h_ref.shape[0]
    tt = seq // n_tiles
    window = _window_size(cap, n_tiles)
    starts = [_window_start(t, cap, n_tiles, _static_clip) for t in range(n_tiles)]
    unit = _window_margin(cap)
    first = g * experts_per_step
    prows = [prow_ref[0, pl.ds(first + j, 1), :] for j in range(experts_per_step)]
    in_windows = ok_ref[b, g] != 0

    @pl.when(in_windows)
    def _():
        slot = lax.broadcasted_iota(jnp.int32, (unit, 1), 0)
        for r0 in range(0, cap, unit):
            covering = [t for t in range(n_tiles) if starts[t] <= r0 < starts[t] + window]
            assert covering == list(range(covering[0], covering[-1] + 1))
            tok = slice(covering[0] * tt, (covering[-1] + 1) * tt)
            onehot = jnp.concatenate([(slot == prow[:, tok] - r0).astype(BF16) for prow in prows],
                                     axis=0)
            rows = jnp.dot(onehot, h_ref[tok, :], preferred_element_type=F32).astype(o_ref.dtype)
            for j in range(experts_per_step):
                o_ref[j, r0:r0 + unit, :] = rows[j * unit:(j + 1) * unit]

    @pl.when(jnp.logical_not(in_windows))
    def _():
        slot = lax.broadcasted_iota(jnp.int32, (cap, seq), 0)
        for j, prow in enumerate(prows):
            o_ref[j] = jnp.dot((slot == prow).astype(BF16), h_ref[...],
                               preferred_element_type=F32).astype(o_ref.dtype)


def gather_tokens(h, prow, *, batch, seq, n_experts, capacity):
    d = h.shape[1]
    e_pad = prow.shape[1]
    ng = _tile(n_experts, 8)
    ok = jnp.all(_slots_inside_windows(prow, n_experts=n_experts, capacity=capacity,
                                       n_tiles=GATHER_TILES), axis=-1)
    ok = jnp.all(ok.reshape(batch, n_experts // ng, ng), axis=-1).astype(jnp.int32)
    vmem = 2 * (seq * d * 2 + e_pad * seq * 4 + ng * capacity * d * 2) + 2 * capacity * seq * 8 \
        + 3 * capacity * d * 4
    return pl.pallas_call(
        functools.partial(_gather_kernel, experts_per_step=ng, n_tiles=GATHER_TILES),
        grid_spec=pltpu.PrefetchScalarGridSpec(
            num_scalar_prefetch=1,
            grid=(batch, n_experts // ng),
            in_specs=[pl.BlockSpec((seq, d), lambda b, g, ok_ref: (b, 0)),
                      pl.BlockSpec((1, e_pad, seq), lambda b, g, ok_ref: (b, 0, 0))],
            out_specs=pl.BlockSpec((ng, None, capacity, d), lambda b, g, ok_ref: (g, b, 0, 0))),
        out_shape=jax.ShapeDtypeStruct((n_experts, batch, capacity, d), BF16),
        compiler_params=_params(("parallel", "arbitrary"), vmem),
        name="gather_tokens",
    )(ok, h, prow)


def _ffn_kernel(x_ref, wg_ref, wu_ref, wd_ref, prow_ref, arow_ref, o_ref, acc_ref, gate_ref,
                *, capacity, gate_steps):
    e = pl.program_id(0)
    f = pl.program_id(1)
    nf = pl.num_programs(1)
    rows = x_ref.shape[0]
    n_seq = rows // capacity

    def chunk_product():
        x = x_ref[...]
        a = jnp.dot(x, wg_ref[...].astype(BF16), preferred_element_type=F32)
        u = jnp.dot(x, wu_ref[...].astype(BF16), preferred_element_type=F32)
        hidden = (a / (1.0 + jnp.exp(-a)) * u).astype(BF16)
        return jnp.dot(hidden, wd_ref[...].astype(BF16), preferred_element_type=F32)

    def gate_share():
        seq = prow_ref.shape[2]
        slot = lax.broadcasted_iota(jnp.int32, (capacity, seq), 0)
        per_step = -(-n_seq // gate_steps)
        for j in range(per_step):
            b = jnp.minimum(f * per_step + j, n_seq - 1)
            mine = slot == prow_ref[b, pl.ds(e, 1), :]
            gate_ref[pl.ds(pl.multiple_of(b * capacity, capacity), capacity), :] = jnp.sum(
                jnp.where(mine, arow_ref[b, pl.ds(e, 1), :], 0.0), axis=1, keepdims=True)

    @pl.when(f == 0)
    def _():
        gate_share()
        acc_ref[...] = chunk_product()

    @pl.when((f > 0) & (f < nf - 1))
    def _():
        gate_share()
        acc_ref[...] += chunk_product()

    @pl.when(f == nf - 1)
    def _():
        gate_share()
        o_ref[...] = ((acc_ref[...] + chunk_product()) * gate_ref[...]).astype(o_ref.dtype)


def expert_ffn(xs, w_gate, w_up, w_down, prow, arow, layer, *, rows_per_expert, capacity):
    _, n_experts, d, ff = w_gate.shape
    tf = _tile(ff // 2, 512)
    r = rows_per_expert
    batch, e_pad, seq = prow.shape
    vmem = 2 * (r * d * 2 + 3 * d * tf * 4 + r * d * 2 + 2 * batch * e_pad * seq * 4) + r * d * 4 \
        + r * LANES * 4 + 3 * d * tf * 2 + 4 * r * tf * 4 + r * d * 4 + 2 * capacity * seq * 4
    return pl.pallas_call(
        functools.partial(_ffn_kernel, capacity=capacity, gate_steps=ff // tf),
        grid=(n_experts, ff // tf),
        in_specs=[pl.BlockSpec((r, d), lambda e, f: (e, 0)),
                  pl.BlockSpec((None, None, d, tf), lambda e, f: (layer, e, 0, f)),
                  pl.BlockSpec((None, None, d, tf), lambda e, f: (layer, e, 0, f)),
                  pl.BlockSpec((None, None, tf, d), lambda e, f: (layer, e, f, 0)),
                  pl.BlockSpec((batch, e_pad, seq), lambda e, f: (0, 0, 0)),
                  pl.BlockSpec((batch, e_pad, seq), lambda e, f: (0, 0, 0))],
        out_specs=pl.BlockSpec((r, d), lambda e, f: (e, 0)),
        out_shape=jax.ShapeDtypeStruct((n_experts * r, d), BF16),
        scratch_shapes=[pltpu.VMEM((r, d), F32), pltpu.VMEM((r, 1), F32)],
        compiler_params=_params(("parallel", "arbitrary"), vmem),
        name="expert_ffn",
    )(xs, w_gate, w_up, w_down, prow, arow)


def _combine_kernel(ok_ref, x_ref, y_ref, pcol_ref, g_ref, out_ref, *, n_group, n_tiles, final_norm):
    b = pl.program_id(0)
    t = pl.program_id(1)
    tt, d = x_ref.shape
    n_experts, cap, _ = y_ref.shape
    window = _window_size(cap, n_tiles)
    lane = lax.broadcasted_iota(jnp.int32, (tt, LANES), 1)
    pcol = pcol_ref[...]

    def scatter(rows_of, n_slots, first_slot):
        slot = (lax.broadcasted_iota(jnp.int32, (tt, n_slots), 1) + first_slot).astype(F32)
        acc = x_ref[...]
        for e0 in range(0, n_experts, n_group):
            hots = []
            for e in range(e0, e0 + n_group):
                slot_of_token = jnp.sum(jnp.where(lane == e, pcol, 0.0),
                                        axis=1, keepdims=True)
                hots.append((slot == slot_of_token).astype(BF16))
            acc = acc + jnp.dot(jnp.concatenate(hots, axis=1), rows_of(e0),
                                preferred_element_type=F32)
        out_ref[...] = _rms(acc, g_ref[...]) if final_norm else acc

    in_window = ok_ref[b, t] != 0

    @pl.when(in_window)
    def _():
        first = pl.multiple_of(_window_start(t, cap, n_tiles), _window_margin(cap))
        scatter(lambda e0: jnp.concatenate([y_ref[e, pl.ds(first, window), :]
                                            for e in range(e0, e0 + n_group)], axis=0),
                window, first)

    @pl.when(jnp.logical_not(in_window))
    def _():
        scatter(lambda e0: y_ref[e0:e0 + n_group].reshape(n_group * cap, d), cap, 0)


def combine(x_res, y, prow, pcol, gain, *, batch, seq, n_experts, capacity, final_norm):
    d = x_res.shape[1]
    nt = COMBINE_TILES
    tt = seq // nt
    ng = _tile(n_experts, 8)
    y4 = y.reshape(n_experts, batch, capacity, d)
    inside = _slots_inside_windows(prow, n_experts=n_experts, capacity=capacity, n_tiles=nt)
    ok = jnp.all(inside, axis=1).astype(jnp.int32)
    vmem = 2 * (2 * tt * d * 4 + n_experts * capacity * d * 2 + tt * LANES * 4) \
        + 3 * tt * d * 4 + ng * capacity * d * 2 + 2 * tt * ng * capacity * 4
    return pl.pallas_call(
        functools.partial(_combine_kernel, n_group=ng, n_tiles=nt, final_norm=final_norm),
        grid_spec=pltpu.PrefetchScalarGridSpec(
            num_scalar_prefetch=1,
            grid=(batch, nt),
            in_specs=[pl.BlockSpec((tt, d), lambda b, t, ok_ref: (b * nt + t, 0)),
                      pl.BlockSpec((n_experts, None, capacity, d), lambda b, t, ok_ref: (0, b, 0, 0)),
                      pl.BlockSpec((tt, LANES), lambda b, t, ok_ref: (b * nt + t, 0)),
                      pl.BlockSpec((1, d), lambda b, t, ok_ref: (0, 0))],
            out_specs=pl.BlockSpec((tt, d), lambda b, t, ok_ref: (b * nt + t, 0))),
        out_shape=jax.ShapeDtypeStruct((batch * seq, d), F32),
        compiler_params=_params(("parallel", "arbitrary"), vmem),
        name="combine",
    )(ok, x_res, y4, pcol, gain.reshape(1, d))


def _pool_kernel(x_ref, prev_ref, next_ref, ng_ref, w_ref, s_ref, g_ref, xo_ref, ho_ref, pad_ref,
                 *, seq, windows):
    i = pl.program_id(0)
    tm, d = x_ref.shape
    halo = prev_ref.shape[0]
    pg = w_ref.shape[1]
    tiles_per_seq = seq // tm
    tile_in_seq = i % tiles_per_seq
    norm_gain = ng_ref[...]
    h = _rms(x_ref[...], norm_gain)
    pad_ref[0:halo, :] = jnp.where(tile_in_seq == 0, 0.0, _rms(prev_ref[...], norm_gain))
    pad_ref[halo:halo + tm, :] = h
    pad_ref[halo + tm:2 * halo + tm, :] = jnp.where(tile_in_seq == tiles_per_seq - 1, 0.0,
                                                    _rms(next_ref[...], norm_gain))
    t = tile_in_seq * tm + lax.broadcasted_iota(jnp.int32, (tm, 1), 0)
    for gi, w in enumerate(windows):
        cols = slice(gi * pg, (gi + 1) * pg)
        acc = pad_ref[:, cols]
        span = 1
        while span < w:
            acc = acc + pltpu.roll(acc, acc.shape[0] - span, axis=0)
            span *= 2
        total = acc[halo - w // 2:halo - w // 2 + tm, :]
        count = (jnp.minimum(t + (w - w // 2), seq) - jnp.maximum(t - w // 2, 0)).astype(F32)
        pooled = (total / count - h[:, cols]).astype(BF16)
        mixed = jnp.dot(pooled, w_ref[gi], preferred_element_type=F32)
        xo_ref[:, cols] = mixed * s_ref[:, cols] + x_ref[:, cols]
    ho_ref[...] = _rms(xo_ref[...], g_ref[...]).astype(ho_ref.dtype)


def pool_mixer(x, norm_gain, w_pool, pool_scale, next_gain, *, seq):
    m, d = x.shape
    ng, pg, _ = w_pool.shape
    halo = SUBLANES
    assert all(w & (w - 1) == 0 and w // 2 <= halo for w in POOL_WINDOWS) and ng == len(POOL_WINDOWS)
    tm = _tile(seq, 1024)
    per = tm // halo
    n_halo_blocks = m // halo
    vmem = 2 * (2 * tm * d * 4 + 2 * halo * d * 4 + ng * pg * pg * 2 + tm * d * 2) \
        + (tm + 2 * halo) * d * 4 + 6 * tm * d * 4
    return pl.pallas_call(
        functools.partial(_pool_kernel, seq=seq, windows=POOL_WINDOWS),
        grid=(m // tm,),
        in_specs=[pl.BlockSpec((tm, d), lambda i: (i, 0)),
                  pl.BlockSpec((halo, d), lambda i: (jnp.maximum(i * per - 1, 0), 0)),
                  pl.BlockSpec((halo, d), lambda i: (jnp.minimum((i + 1) * per, n_halo_blocks - 1), 0)),
                  pl.BlockSpec((1, d), lambda i: (0, 0)),
                  pl.BlockSpec((ng, pg, pg), lambda i: (0, 0, 0)),
                  pl.BlockSpec((1, d), lambda i: (0, 0)),
                  pl.BlockSpec((1, d), lambda i: (0, 0))],
        out_specs=[pl.BlockSpec((tm, d), lambda i: (i, 0)),
                   pl.BlockSpec((tm, d), lambda i: (i, 0))],
        out_shape=[jax.ShapeDtypeStruct((m, d), F32), jax.ShapeDtypeStruct((m, d), BF16)],
        scratch_shapes=[pltpu.VMEM((tm + 2 * halo, d), F32)],
        compiler_params=_params(("parallel",), vmem),
        name="pool_mixer",
    )(x, x, x, norm_gain.reshape(1, d), w_pool, pool_scale.reshape(1, d), next_gain.reshape(1, d))


def _rope_tables(seq):
    half = HEAD_DIM // 2
    rows = seq // GRID_W
    row_idx = jnp.repeat(jnp.arange(rows), GRID_W).astype(F32)
    col_idx = jnp.tile(jnp.arange(GRID_W), rows).astype(F32)
    inv_freq = 1.0 / (ROPE_THETA ** (jnp.arange(0, half, 2, dtype=F32) / half))
    ang = jnp.concatenate([row_idx[:, None] * inv_freq[None, :],
                           col_idx[:, None] * inv_freq[None, :]], axis=-1)
    cos, sin = jnp.cos(ang), jnp.sin(ang)
    zero = jnp.zeros_like(sin)
    cos_full = jnp.stack([cos, cos], axis=-1).reshape(seq, HEAD_DIM)
    sin_even = jnp.stack([-sin, zero], axis=-1).reshape(seq, HEAD_DIM)
    sin_odd = jnp.stack([zero, sin], axis=-1).reshape(seq, HEAD_DIM)
    return cos_full, sin_even, sin_odd


def _split_router(w_router):
    d, n_experts = w_router.shape
    w = jnp.pad(w_router, ((0, 0), (0, LANES - n_experts)))
    hi = w.astype(BF16)
    lo = (w - hi.astype(F32)).astype(BF16)
    return jnp.concatenate([hi, lo], axis=1), n_experts


def _moe(x_res, h, aff, w_gate, w_up, w_down, layer, gain, *, batch, seq, final_norm):
    n_experts = w_gate.shape[1]
    capacity = EC_CAPACITY_FACTOR * seq // n_experts
    d = h.shape[1]
    prow, arow, pcol = route(aff, batch=batch, seq=seq, n_experts=n_experts, capacity=capacity)
    xs = gather_tokens(h, prow, batch=batch, seq=seq, n_experts=n_experts, capacity=capacity)
    y = expert_ffn(xs.reshape(n_experts * batch * capacity, d), w_gate, w_up, w_down, prow, arow,
                   layer, rows_per_expert=batch * capacity, capacity=capacity)
    return combine(x_res, y, prow, pcol, gain, batch=batch, seq=seq,
                   n_experts=n_experts, capacity=capacity, final_norm=final_norm)


def _cross_block(x, h, kv, layer, cross_w_q, cross_w_o, ffn_norm, router_w, *, batch, seq, mem_len):
    return cross_block(h, cast_weight(cross_w_q, (layer,)), kv, layer,
                       cast_weight(cross_w_o, (layer,)),
                       x, ffn_norm[layer], _split_router(router_w[layer]),
                       batch=batch, seq=seq, mem_len=mem_len)


def kernel(x, mem, mix_norm, attn_w_in, q_gain, k_gain, fourier_w, attn_w_out, pool_w, pool_scale,
           cross_norm, mem_norm, cross_w_q, cross_w_k, cross_w_v, cross_w_o, ffn_norm, router_w,
           expert_w_gate, expert_w_up, expert_w_down, final_norm):
    batch, seq, d = x.shape
    mem_len = mem.shape[1]
    depth = mix_norm.shape[0]
    attn_width = N_Q_HEADS * HEAD_DIM
    kv_width = N_KV_HEADS * HEAD_DIM
    fourier_width = N_FOURIER_GROUPS * FOURIER_GROUP
    rope_width = attn_width + kv_width
    cos_t, sin_a, sin_b = _rope_tables(seq)

    xf = x.reshape(batch * seq, d)
    kv = memory_kv(mem.reshape(batch * mem_len, d), mem_norm, cross_w_k, cross_w_v)
    cross = functools.partial(_cross_block, kv=kv, cross_w_q=cross_w_q, cross_w_o=cross_w_o,
                              ffn_norm=ffn_norm, router_w=router_w,
                              batch=batch, seq=seq, mem_len=mem_len)
    for layer in range(depth):
        i = layer // 2
        if layer % 2 == 0:
            gain_cols = jnp.concatenate([jnp.tile(q_gain[i], N_Q_HEADS), jnp.tile(k_gain[i], N_KV_HEADS),
                                         jnp.ones((kv_width + fourier_width,), F32)]).reshape(1, -1)
            proj = proj_in(xf, mix_norm[layer], cast_weight(attn_w_in, (i,)), gain_cols,
                           cos_t, sin_a, sin_b, seq=seq, rope_width=rope_width)
            o_attn = gqa_attention(proj, batch=batch, seq=seq, attn_width=attn_width, kv_width=kv_width)
            o_four = fourier_mixer(proj, fourier_w[i], batch=batch, seq=seq,
                                   col0=attn_width + 2 * kv_width)
            xf, h = matmul_residual_norm([o_attn, o_four], cast_weight(attn_w_out, (i,)), xf,
                                         cross_norm[layer])
        else:
            xf, h = pool_mixer(xf, mix_norm[layer], pool_w[i].astype(BF16), pool_scale[i],
                               cross_norm[layer], seq=seq)
        xf, h, aff = cross(xf, h, layer=layer)
        xf = _moe(xf, h, aff, expert_w_gate, expert_w_up, expert_w_down, layer, final_norm,
                  batch=batch, seq=seq, final_norm=layer == depth - 1)
    return xf.reshape(batch, seq, d)
```

```python
import functools
import math

import jax
import jax.numpy as jnp
from jax import lax
from jax.experimental import pallas as pl
from jax.experimental.pallas import tpu as pltpu

F32 = jnp.float32
BF16 = jnp.bfloat16

GRID_W = 64
HEAD_DIM = 128
N_Q_HEADS = 12
N_KV_HEADS = 4
N_FOURIER_GROUPS = 4
FOURIER_GROUP = 128
ROPE_THETA = 10000.0
POOL_WINDOWS = (2, 4, 8, 16)
N_MEM_HEADS = 4
EC_CAPACITY_FACTOR = 2
NORM_EPS = 1e-6

LANES = 128
SUBLANES = 8
V7X_VMEM_BYTES = 64 * 1024 * 1024
VMEM_HEADROOM_BYTES = 6 * 1024 * 1024
COL_CHUNK = 512
LOG2_E = math.log2(math.e)


def _params(semantics, vmem_bytes):
    limit = min(int(vmem_bytes) + VMEM_HEADROOM_BYTES, V7X_VMEM_BYTES - VMEM_HEADROOM_BYTES)
    return pltpu.CompilerParams(dimension_semantics=semantics, vmem_limit_bytes=limit)


def _tile(n, pref):
    t = min(n, pref)
    while n % t:
        t //= 2
    return t


def _rms(x, gain):
    ms = jnp.mean(x * x, axis=-1, keepdims=True)
    return x * lax.rsqrt(ms + NORM_EPS) * gain


def _softmax_numerators(scores, scale):
    c = scale * LOG2_E
    return [jnp.exp2((s - jnp.max(s, axis=-1, keepdims=True)) * c) for s in scores]


def _cast_kernel(w_ref, o_ref):
    o_ref[...] = w_ref[...].astype(o_ref.dtype)


def cast_weight(w, lead):
    k, n = w.shape[-2:]
    tk = _tile(k, 1024)
    squeezed = (None,) * len(lead)
    return pl.pallas_call(
        _cast_kernel,
        grid=(k // tk,),
        in_specs=[pl.BlockSpec(squeezed + (tk, n), lambda i: tuple(lead) + (i, 0))],
        out_specs=pl.BlockSpec((tk, n), lambda i: (i, 0)),
        out_shape=jax.ShapeDtypeStruct((k, n), BF16),
        compiler_params=_params(("parallel",), 2 * tk * n * 6),
        name="cast_weight",
    )(w)


def _memory_kv_kernel(mem_ref, g_ref, wk_ref, wv_ref, wq_ref, wo_ref, o_ref, c_ref, a_ref):
    s = pl.program_id(1)
    j = pl.program_id(2)

    @pl.when((s == 0) & (j == 0))
    def _():
        a_ref[...] = _rms(mem_ref[...], g_ref[...]).astype(a_ref.dtype)

    @pl.when(s == 0)
    def _():
        o_ref[...] = jnp.dot(a_ref[...], wk_ref[...].astype(BF16),
                             preferred_element_type=F32).astype(o_ref.dtype)
        c_ref[...] = wq_ref[...].astype(c_ref.dtype)

    @pl.when(s != 0)
    def _():
        o_ref[...] = jnp.dot(a_ref[...], wv_ref[...].astype(BF16),
                             preferred_element_type=F32).astype(o_ref.dtype)
        c_ref[...] = wo_ref[...].astype(c_ref.dtype)


def memory_kv(mem2d, mem_norm, w_k, w_v, w_q, w_o):
    m, d = mem2d.shape
    depth, k, n = w_k.shape
    tn = _tile(n, COL_CHUNK)
    nj = n // tn
    kq, nq = w_q.shape[1:]
    tr = kq // nj
    vmem = m * d * 4 + 4 * k * tn * 4 + 2 * m * tn * 2 + m * d * 2 + k * tn * 2 + m * tn * 4 \
        + 2 * m * d * 4 + 4 * tr * nq * 4 + 2 * tr * nq * 2
    return pl.pallas_call(
        _memory_kv_kernel,
        grid=(depth, 2, nj),
        in_specs=[pl.BlockSpec((m, d), lambda l, s, j: (0, 0), pipeline_mode=pl.Buffered(1)),
                  pl.BlockSpec((None, 1, d), lambda l, s, j: (l, 0, 0)),
                  pl.BlockSpec((None, k, tn), lambda l, s, j: (l, 0, jnp.where(s == 0, j, nj - 1))),
                  pl.BlockSpec((None, k, tn), lambda l, s, j: (l, 0, jnp.where(s == 0, 0, j))),
                  pl.BlockSpec((None, tr, nq), lambda l, s, j: (l, jnp.where(s == 0, j, nj - 1), 0)),
                  pl.BlockSpec((None, tr, nq), lambda l, s, j: (l, jnp.where(s == 0, 0, j), 0))],
        out_specs=[pl.BlockSpec((None, None, m, tn), lambda l, s, j: (l, s, 0, j)),
                   pl.BlockSpec((None, None, tr, nq), lambda l, s, j: (l, s, j, 0))],
        out_shape=[jax.ShapeDtypeStruct((depth, 2, m, n), BF16),
                   jax.ShapeDtypeStruct((depth, 2, kq, nq), BF16)],
        scratch_shapes=[pltpu.VMEM((m, d), BF16)],
        compiler_params=_params(("arbitrary", "arbitrary", "arbitrary"), vmem),
        name="memory_kv",
    )(mem2d, mem_norm.reshape(depth, 1, d), w_k, w_v, w_q, w_o)


def _proj_in_kernel(a_ref, ng_ref, w_ref, g_ref, cos_ref, sa_ref, sb_ref, o_ref, *, n_rope_heads):
    n = o_ref.shape[1]
    tn = _tile(n, COL_CHUNK)
    a = _rms(a_ref[...], ng_ref[...]).astype(BF16)
    cos, sin_a, sin_b = cos_ref[...], sa_ref[...], sb_ref[...]
    chunks = [jnp.dot(a, w_ref[:, c * tn:(c + 1) * tn], preferred_element_type=F32)
              for c in range(n // tn)]
    for c, acc in enumerate(chunks):
        for h in range(tn // HEAD_DIM):
            head = c * (tn // HEAD_DIM) + h
            col = slice(head * HEAD_DIM, (head + 1) * HEAD_DIM)
            xh = acc[:, h * HEAD_DIM:(h + 1) * HEAD_DIM]
            if head < n_rope_heads:
                ms = jnp.mean(xh * xh, axis=-1, keepdims=True)
                y = xh * lax.rsqrt(ms + NORM_EPS) * g_ref[:, col]
                xh = (y * cos + pltpu.roll(y, HEAD_DIM - 1, axis=1) * sin_a
                      + pltpu.roll(y, 1, axis=1) * sin_b)
            o_ref[:, col] = xh.astype(o_ref.dtype)


def proj_in(x, norm_gain, w_in, gain_cols, cos_t, sin_a, sin_b, *, seq, rope_width):
    m, k = x.shape
    n = w_in.shape[1]
    tm = _tile(seq, 512)
    n_pos_blocks = seq // tm
    vmem = 2 * (tm * k * 4 + k * n * 2 + tm * n * 2 + 3 * tm * HEAD_DIM * 4) + 6 * tm * COL_CHUNK * 4 \
        + 2 * tm * k * 4
    table = pl.BlockSpec((tm, HEAD_DIM), lambda i: (i % n_pos_blocks, 0))
    return pl.pallas_call(
        functools.partial(_proj_in_kernel, n_rope_heads=rope_width // HEAD_DIM),
        grid=(m // tm,),
        in_specs=[pl.BlockSpec((tm, k), lambda i: (i, 0)),
                  pl.BlockSpec((1, k), lambda i: (0, 0)),
                  pl.BlockSpec((k, n), lambda i: (0, 0)),
                  pl.BlockSpec((1, n), lambda i: (0, 0)),
                  table, table, table],
        out_specs=pl.BlockSpec((tm, n), lambda i: (i, 0)),
        out_shape=jax.ShapeDtypeStruct((m, n), BF16),
        compiler_params=_params(("parallel",), vmem),
        name="proj_in",
    )(x, norm_gain.reshape(1, k), w_in, gain_cols, cos_t, sin_a, sin_b)


def _attn_kernel(q_ref, k_ref, v_ref, o_ref, *, group, scale):
    k = k_ref[...]
    v = v_ref[...]
    v_ones = jnp.concatenate([v, jnp.ones_like(v)], axis=1)
    heads = [slice(g * HEAD_DIM, (g + 1) * HEAD_DIM) for g in range(group)]
    scores = [lax.dot_general(q_ref[:, sl], k, (((1,), (1,)), ((), ())),
                              preferred_element_type=F32) for sl in heads]
    for sl, e in zip(heads, _softmax_numerators(scores, scale)):
        ol = jnp.dot(e.astype(BF16), v_ones, preferred_element_type=F32)
        o_ref[:, sl] = (ol[:, :HEAD_DIM] / ol[:, HEAD_DIM:HEAD_DIM + 1]).astype(o_ref.dtype)


def gqa_attention(proj, *, batch, seq, attn_width, kv_width):
    group = N_Q_HEADS // N_KV_HEADS
    gw = group * HEAD_DIM
    tq = _tile(seq, 1024)
    nq = seq // tq
    k_col0 = attn_width // HEAD_DIM
    v_col0 = (attn_width + kv_width) // HEAD_DIM
    vmem = 2 * (tq * gw * 2 * 2 + 2 * seq * HEAD_DIM * 2) + group * tq * seq * 6
    return pl.pallas_call(
        functools.partial(_attn_kernel, group=group, scale=HEAD_DIM ** -0.5),
        grid=(batch, N_KV_HEADS, nq),
        in_specs=[pl.BlockSpec((tq, gw), lambda b, h, i: (b * nq + i, h)),
                  pl.BlockSpec((seq, HEAD_DIM), lambda b, h, i: (b, k_col0 + h)),
                  pl.BlockSpec((seq, HEAD_DIM), lambda b, h, i: (b, v_col0 + h))],
        out_specs=pl.BlockSpec((tq, gw), lambda b, h, i: (b * nq + i, h)),
        out_shape=jax.ShapeDtypeStruct((batch * seq, attn_width), BF16),
        compiler_params=_params(("parallel", "parallel", "parallel"), vmem),
        name="gqa_attention",
    )(proj, proj, proj)


def _fourier_kernel(f_ref, cc_ref, sc_ref, cs_ref, ss_ref, wf_ref, o_ref, ab_ref, *, seq, norm):
    r = pl.program_id(1)
    ng = wf_ref.shape[0]
    cg = wf_ref.shape[1]

    @pl.when(r == 0)
    def _():
        for g in range(ng):
            fg = f_ref[:, g * cg:(g + 1) * cg]
            ab_ref[0:seq, g * cg:(g + 1) * cg] = jnp.dot(
                fg, cc_ref[...], preferred_element_type=F32).astype(BF16)
            ab_ref[seq:2 * seq, g * cg:(g + 1) * cg] = jnp.dot(
                fg, sc_ref[...], preferred_element_type=F32).astype(BF16)

    z = (jnp.dot(cs_ref[...], ab_ref[0:seq, :], preferred_element_type=F32)
         - jnp.dot(ss_ref[...], ab_ref[seq:2 * seq, :], preferred_element_type=F32)) * norm
    for g in range(ng):
        zg = z[:, g * cg:(g + 1) * cg].astype(BF16)
        o_ref[:, g * cg:(g + 1) * cg] = jnp.dot(
            zg, wf_ref[g].astype(BF16), preferred_element_type=F32).astype(o_ref.dtype)


def _dft_cos_sin(n):
    lo = _tile(n, 64)
    hi = n // lo
    j = jnp.arange(n, dtype=jnp.int32)

    def table(k):
        ang = ((k[:, None] * j[None, :]) % n).astype(F32) * (2.0 * math.pi / n)
        return jnp.cos(ang), jnp.sin(ang)

    ca, sa = table(jnp.arange(hi, dtype=jnp.int32) * lo)
    cb, sb = table(jnp.arange(lo, dtype=jnp.int32))
    cos = ca[:, None, :] * cb[None, :, :] - sa[:, None, :] * sb[None, :, :]
    sin = sa[:, None, :] * cb[None, :, :] + ca[:, None, :] * sb[None, :, :]
    return cos.reshape(n, n).astype(BF16), sin.reshape(n, n).astype(BF16)


def fourier_mixer(proj, w_fourier, *, batch, seq, col0):
    ng, cg, _ = w_fourier.shape
    fw = ng * cg
    tr = _tile(seq, 1024)
    nr = seq // tr
    cc, sc = _dft_cos_sin(cg)
    cs, ss = _dft_cos_sin(seq)
    norm = 1.0 / math.sqrt(seq * cg)
    vmem = 2 * (seq * fw * 2 + 2 * cg * cg * 2 + 2 * tr * seq * 2 + ng * cg * cg * 4 + tr * fw * 2) \
        + 2 * seq * fw * 2 + 4 * tr * fw * 4
    return pl.pallas_call(
        functools.partial(_fourier_kernel, seq=seq, norm=norm),
        grid=(batch, nr),
        in_specs=[pl.BlockSpec((seq, fw), lambda b, r: (b, col0 // fw)),
                  pl.BlockSpec((cg, cg), lambda b, r: (0, 0)),
                  pl.BlockSpec((cg, cg), lambda b, r: (0, 0)),
                  pl.BlockSpec((tr, seq), lambda b, r: (r, 0)),
                  pl.BlockSpec((tr, seq), lambda b, r: (r, 0)),
                  pl.BlockSpec((ng, cg, cg), lambda b, r: (0, 0, 0))],
        out_specs=pl.BlockSpec((tr, fw), lambda b, r: (b * nr + r, 0)),
        out_shape=jax.ShapeDtypeStruct((batch * seq, fw), BF16),
        scratch_shapes=[pltpu.VMEM((2 * seq, fw), BF16)],
        compiler_params=_params(("parallel", "arbitrary"), vmem),
        name="fourier_mixer",
    )(proj, cc, sc, cs, ss, w_fourier)


def _router_affinity(hn, wr_ref, n_experts):
    hi = hn.astype(BF16)
    lo = (hn - hi.astype(F32)).astype(BF16)
    r_hi = jnp.dot(hi, wr_ref[...], preferred_element_type=F32)
    r_lo = jnp.dot(lo, wr_ref[...], preferred_element_type=F32)
    logits = r_hi[:, :LANES] + r_hi[:, LANES:] + r_lo[:, :LANES]
    lane = lax.broadcasted_iota(jnp.int32, logits.shape, 1)
    logits = jnp.where(lane < n_experts, logits, -jnp.inf)
    e = jnp.exp(logits - jnp.max(logits, axis=-1, keepdims=True))
    return e / jnp.sum(e, axis=-1, keepdims=True)


def _mm_res_norm_kernel(*refs, n_a, with_router, n_experts):
    a_refs = refs[:n_a]
    w_ref, x_ref, g_ref = refs[n_a:n_a + 3]
    pos = n_a + 3
    if with_router:
        wr_ref = refs[pos]
        pos += 1
    xo_ref, ho_ref = refs[pos:pos + 2]
    if with_router:
        aff_ref = refs[pos + 2]
    d = xo_ref.shape[1]
    tn = _tile(d, COL_CHUNK)
    for c in range(d // tn):
        sl = slice(c * tn, (c + 1) * tn)
        acc = x_ref[:, sl]
        row0 = 0
        for a_ref in a_refs:
            ka = a_ref.shape[1]
            acc = acc + jnp.dot(a_ref[...], w_ref[row0:row0 + ka, sl], preferred_element_type=F32)
            row0 += ka
        xo_ref[:, sl] = acc
    hn = _rms(xo_ref[...], g_ref[...])
    ho_ref[...] = hn.astype(ho_ref.dtype)
    if with_router:
        aff_ref[...] = _router_affinity(hn, wr_ref, n_experts)


def matmul_residual_norm(a_list, w, x_res, gain, *, router=None):
    m, d = x_res.shape
    k = w.shape[0]
    tm = _tile(m, 512)
    in_specs = [pl.BlockSpec((tm, a.shape[1]), lambda i: (i, 0)) for a in a_list]
    in_specs += [pl.BlockSpec((k, d), lambda i: (0, 0)),
                 pl.BlockSpec((tm, d), lambda i: (i, 0)),
                 pl.BlockSpec((1, d), lambda i: (0, 0))]
    args = list(a_list) + [w, x_res, gain.reshape(1, d)]
    out_specs = [pl.BlockSpec((tm, d), lambda i: (i, 0)),
                 pl.BlockSpec((tm, d), lambda i: (i, 0))]
    out_shape = [jax.ShapeDtypeStruct((m, d), F32), jax.ShapeDtypeStruct((m, d), BF16)]
    n_experts = 0
    if router is not None:
        wr, n_experts = router
        in_specs.append(pl.BlockSpec((d, 2 * LANES), lambda i: (0, 0)))
        args.append(wr)
        out_specs.append(pl.BlockSpec((tm, LANES), lambda i: (i, 0)))
        out_shape.append(jax.ShapeDtypeStruct((m, LANES), F32))
    vmem = 2 * (tm * k * 2 + k * d * 2 + tm * d * 4 + tm * d * 4 + tm * d * 2) \
        + 3 * tm * d * 4 + 4 * d * LANES * 2
    return pl.pallas_call(
        functools.partial(_mm_res_norm_kernel, n_a=len(a_list),
                          with_router=router is not None, n_experts=n_experts),
        grid=(m // tm,),
        in_specs=in_specs,
        out_specs=out_specs,
        out_shape=out_shape,
        compiler_params=_params(("parallel",), vmem),
        name="matmul_residual_norm",
    )(*args)


def _cross_block_kernel(h_ref, wq_ref, k_ref, v_ref, wo_ref, x_ref, g_ref, wr_ref,
                        xo_ref, ho_ref, aff_ref, o_ref, *, n_heads, scale, n_experts):
    dh = h_ref.shape[1] // n_heads
    h = h_ref[...]
    heads = [slice(i * dh, (i + 1) * dh) for i in range(n_heads)]
    qs = [jnp.dot(h, wq_ref[:, sl], preferred_element_type=F32).astype(BF16) for sl in heads]
    scores = [lax.dot_general(q, k_ref[:, sl], (((1,), (1,)), ((), ())),
                              preferred_element_type=F32) for q, sl in zip(qs, heads)]
    for sl, e in zip(heads, _softmax_numerators(scores, scale)):
        l = jnp.sum(e, axis=-1, keepdims=True)
        o = jnp.dot(e.astype(BF16), v_ref[:, sl], preferred_element_type=F32) / l
        o_ref[:, sl] = o.astype(o_ref.dtype)
    d = xo_ref.shape[1]
    tn = _tile(d, COL_CHUNK)
    o = o_ref[...]
    for c in range(d // tn):
        sl = slice(c * tn, (c + 1) * tn)
        xo_ref[:, sl] = x_ref[:, sl] + jnp.dot(o, wo_ref[:, sl], preferred_element_type=F32)
    hn = _rms(xo_ref[...], g_ref[...])
    ho_ref[...] = hn.astype(ho_ref.dtype)
    aff_ref[...] = _router_affinity(hn, wr_ref, n_experts)


def cross_block(h, w_qo, kv, layer, x_res, gain, router, *, batch, seq, mem_len):
    m, d = x_res.shape
    wr, n_experts = router
    tq = _tile(seq, 512)
    nq = seq // tq
    once = pl.Buffered(1)
    vmem = 2 * (tq * d * 2 + 2 * mem_len * d * 2 + 2 * tq * d * 4 + tq * d * 2 + tq * LANES * 4) \
        + 2 * d * d * 2 + 2 * d * LANES * 2 + tq * d * 2 + 5 * tq * d * 4
    row = lambda b, i: (b * nq + i, 0)
    fixed = lambda b, i: (0, 0)
    return pl.pallas_call(
        functools.partial(_cross_block_kernel, n_heads=N_MEM_HEADS,
                          scale=(d // N_MEM_HEADS) ** -0.5, n_experts=n_experts),
        grid=(batch, nq),
        in_specs=[pl.BlockSpec((tq, d), row),
                  pl.BlockSpec((None, None, d, d), lambda b, i: (layer, 0, 0, 0), pipeline_mode=once),
                  pl.BlockSpec((None, None, mem_len, d), lambda b, i: (layer, 0, b, 0)),
                  pl.BlockSpec((None, None, mem_len, d), lambda b, i: (layer, 1, b, 0)),
                  pl.BlockSpec((None, None, d, d), lambda b, i: (layer, 1, 0, 0), pipeline_mode=once),
                  pl.BlockSpec((tq, d), row),
                  pl.BlockSpec((1, d), fixed),
                  pl.BlockSpec((d, 2 * LANES), fixed, pipeline_mode=once)],
        out_specs=[pl.BlockSpec((tq, d), row),
                   pl.BlockSpec((tq, d), row),
                   pl.BlockSpec((tq, LANES), row)],
        out_shape=[jax.ShapeDtypeStruct((m, d), F32), jax.ShapeDtypeStruct((m, d), BF16),
                   jax.ShapeDtypeStruct((m, LANES), F32)],
        scratch_shapes=[pltpu.VMEM((tq, d), BF16)],
        compiler_params=_params(("parallel", "parallel"), vmem),
        name="cross_block",
    )(h, w_qo, kv, kv, w_qo, x_res, gain.reshape(1, d), wr)


def _prefix_count(x, lane):
    n = x.shape[1]
    shift = 1
    while shift < n:
        x = x + jnp.where(lane >= shift, pltpu.roll(x, shift, axis=1), 0)
        shift *= 2
    return x


def _route_kernel(aff_ref, prow_ref, arow_ref, pcol_ref, *, n_experts, capacity):
    batch, e_pad, seq = prow_ref.shape
    a_t = jnp.concatenate([aff_ref[b * seq:(b + 1) * seq, :].T[:e_pad] for b in range(batch)],
                          axis=0)
    for b in range(batch):
        arow_ref[b] = a_t[b * e_pad:(b + 1) * e_pad]

    def body(i, thr_bits):
        cand = thr_bits | jnp.left_shift(jnp.int32(1), 30 - i)
        cnt = jnp.sum((a_t >= pltpu.bitcast(cand, F32)).astype(F32), axis=1, keepdims=True)
        return jnp.where(cnt >= capacity, cand, thr_bits)

    thr = pltpu.bitcast(lax.fori_loop(0, 31, body, jnp.zeros((batch * e_pad, 1), jnp.int32)), F32)
    lane = lax.broadcasted_iota(jnp.int32, a_t.shape, 1)
    gt = a_t > thr
    eq = a_t == thr
    need = capacity - jnp.sum(gt.astype(F32), axis=1, keepdims=True).astype(jnp.int32)
    sel = gt | (eq & (_prefix_count(eq.astype(jnp.int32), lane) <= need))
    pos = jnp.where(sel, _prefix_count(sel.astype(jnp.int32), lane) - 1, -1)
    row = lax.broadcasted_iota(jnp.int32, a_t.shape, 0)
    pos = jnp.where(row % e_pad < n_experts, pos, -1)
    unused = jnp.full((LANES - e_pad, seq), -1.0, F32)
    for b in range(batch):
        pos_b = pos[b * e_pad:(b + 1) * e_pad]
        prow_ref[b] = pos_b
        pcol_ref[b * seq:(b + 1) * seq, :] = jnp.concatenate([pos_b.astype(F32), unused], axis=0).T


def route(aff, *, batch, seq, n_experts, capacity):
    e_pad = max(SUBLANES, n_experts)
    return pl.pallas_call(
        functools.partial(_route_kernel, n_experts=n_experts, capacity=capacity),
        grid=(1,),
        in_specs=[pl.BlockSpec((batch * seq, LANES), lambda i: (0, 0))],
        out_specs=[pl.BlockSpec((batch, e_pad, seq), lambda i: (0, 0, 0)),
                   pl.BlockSpec((batch, e_pad, seq), lambda i: (0, 0, 0)),
                   pl.BlockSpec((batch * seq, LANES), lambda i: (0, 0))],
        out_shape=[jax.ShapeDtypeStruct((batch, e_pad, seq), jnp.int32),
                   jax.ShapeDtypeStruct((batch, e_pad, seq), F32),
                   jax.ShapeDtypeStruct((batch * seq, LANES), F32)],
        compiler_params=_params(("arbitrary",), 8 * batch * seq * LANES * 4),
        name="route",
    )(aff)


GATHER_TILES = 8
COMBINE_TILES = 4


def _window_size(capacity, n_tiles):
    return capacity // n_tiles + 2 * _window_margin(capacity)


def _window_margin(capacity):
    return capacity // 8


def _window_start(tile, capacity, n_tiles, clip=jnp.clip):
    share = capacity // n_tiles
    unit = _window_margin(capacity)
    window = _window_size(capacity, n_tiles)
    assert share % unit == 0 and (capacity - window) % unit == 0
    return clip(tile * (share // unit) - 1, 0, (capacity - window) // unit) * unit


def _static_clip(v, lo, hi):
    return max(lo, min(v, hi))


def _slots_inside_windows(prow, *, n_experts, capacity, n_tiles):
    batch, _, seq = prow.shape
    pos = prow[:, :n_experts, :].reshape(batch, n_experts, n_tiles, seq // n_tiles)
    first = _window_start(jnp.arange(n_tiles, dtype=jnp.int32), capacity, n_tiles)[None, None, :]
    lowest = jnp.min(jnp.where(pos >= 0, pos, capacity), axis=-1)
    highest = jnp.max(pos, axis=-1)
    return (lowest >= first) & (highest < first + _window_size(capacity, n_tiles))


def _gather_kernel(ok_ref, h_ref, prow_ref, o_ref, *, experts_per_step, n_tiles):
    b = pl.program_id(0)
    g = pl.program_id(1)
    cap = o_ref.shape[1]
    seq = h_ref.shape[0]
    tt = seq // n_tiles
    window = _window_size(cap, n_tiles)
    starts = [_window_start(t, cap, n_tiles, _static_clip) for t in range(n_tiles)]
    unit = _window_margin(cap)
    first = g * experts_per_step
    prows = [prow_ref[0, pl.ds(first + j, 1), :] for j in range(experts_per_step)]
    in_windows = ok_ref[b, g] != 0

    @pl.when(in_windows)
    def _():
        slot = lax.broadcasted_iota(jnp.int32, (unit, 1), 0)
        for r0 in range(0, cap, unit):
            covering = [t for t in range(n_tiles) if starts[t] <= r0 < starts[t] + window]
            assert covering == list(range(covering[0], covering[-1] + 1))
            tok = slice(covering[0] * tt, (covering[-1] + 1) * tt)
            onehot = jnp.concatenate([(slot == prow[:, tok] - r0).astype(BF16) for prow in prows],
                                     axis=0)
            rows = jnp.dot(onehot, h_ref[tok, :], preferred_element_type=F32).astype(o_ref.dtype)
            for j in range(experts_per_step):
                o_ref[j, r0:r0 + unit, :] = rows[j * unit:(j + 1) * unit]

    @pl.when(jnp.logical_not(in_windows))
    def _():
        slot = lax.broadcasted_iota(jnp.int32, (cap, seq), 0)
        for j, prow in enumerate(prows):
            o_ref[j] = jnp.dot((slot == prow).astype(BF16), h_ref[...],
                               preferred_element_type=F32).astype(o_ref.dtype)


def gather_tokens(h, prow, *, batch, seq, n_experts, capacity):
    d = h.shape[1]
    e_pad = prow.shape[1]
    ng = _tile(n_experts, 8)
    ok = jnp.all(_slots_inside_windows(prow, n_experts=n_experts, capacity=capacity,
                                       n_tiles=GATHER_TILES), axis=-1)
    ok = jnp.all(ok.reshape(batch, n_experts // ng, ng), axis=-1).astype(jnp.int32)
    vmem = 2 * (seq * d * 2 + e_pad * seq * 4 + ng * capacity * d * 2) + 2 * capacity * seq * 8 \
        + 3 * capacity * d * 4
    return pl.pallas_call(
        functools.partial(_gather_kernel, experts_per_step=ng, n_tiles=GATHER_TILES),
        grid_spec=pltpu.PrefetchScalarGridSpec(
            num_scalar_prefetch=1,
            grid=(batch, n_experts // ng),
            in_specs=[pl.BlockSpec((seq, d), lambda b, g, ok_ref: (b, 0)),
                      pl.BlockSpec((1, e_pad, seq), lambda b, g, ok_ref: (b, 0, 0))],
            out_specs=pl.BlockSpec((ng, None, capacity, d), lambda b, g, ok_ref: (g, b, 0, 0))),
        out_shape=jax.ShapeDtypeStruct((n_experts, batch, capacity, d), BF16),
        compiler_params=_params(("parallel", "arbitrary"), vmem),
        name="gather_tokens",
    )(ok, h, prow)


def _ffn_kernel(x_ref, wg_ref, wu_ref, wd_ref, prow_ref, arow_ref, o_ref, acc_ref, gate_ref,
                *, capacity, gate_steps):
    e = pl.program_id(0)
    f = pl.program_id(1)
    nf = pl.num_programs(1)
    rows = x_ref.shape[0]
    n_seq = rows // capacity

    def chunk_product():
        x = x_ref[...]
        a = jnp.dot(x, wg_ref[...].astype(BF16), preferred_element_type=F32)
        u = jnp.dot(x, wu_ref[...].astype(BF16), preferred_element_type=F32)
        hidden = (a / (1.0 + jnp.exp(-a)) * u).astype(BF16)
        return jnp.dot(hidden, wd_ref[...].astype(BF16), preferred_element_type=F32)

    def gate_share():
        seq = prow_ref.shape[2]
        slot = lax.broadcasted_iota(jnp.int32, (capacity, seq), 0)
        per_step = -(-n_seq // gate_steps)
        for j in range(per_step):
            b = jnp.minimum(f * per_step + j, n_seq - 1)
            mine = slot == prow_ref[b, pl.ds(e, 1), :]
            gate_ref[pl.ds(pl.multiple_of(b * capacity, capacity), capacity), :] = jnp.sum(
                jnp.where(mine, arow_ref[b, pl.ds(e, 1), :], 0.0), axis=1, keepdims=True)

    @pl.when(f == 0)
    def _():
        gate_share()
        acc_ref[...] = chunk_product()

    @pl.when((f > 0) & (f < nf - 1))
    def _():
        gate_share()
        acc_ref[...] += chunk_product()

    @pl.when(f == nf - 1)
    def _():
        gate_share()
        o_ref[...] = ((acc_ref[...] + chunk_product()) * gate_ref[...]).astype(o_ref.dtype)


def expert_ffn(xs, w_gate, w_up, w_down, prow, arow, layer, *, rows_per_expert, capacity):
    _, n_experts, d, ff = w_gate.shape
    tf = _tile(ff // 2, 512)
    r = rows_per_expert
    batch, e_pad, seq = prow.shape
    vmem = 2 * (r * d * 2 + 3 * d * tf * 4 + r * d * 2 + 2 * batch * e_pad * seq * 4) + r * d * 4 \
        + r * LANES * 4 + 3 * d * tf * 2 + 4 * r * tf * 4 + r * d * 4 + 2 * capacity * seq * 4
    return pl.pallas_call(
        functools.partial(_ffn_kernel, capacity=capacity, gate_steps=ff // tf),
        grid=(n_experts, ff // tf),
        in_specs=[pl.BlockSpec((r, d), lambda e, f: (e, 0)),
                  pl.BlockSpec((None, None, d, tf), lambda e, f: (layer, e, 0, f)),
                  pl.BlockSpec((None, None, d, tf), lambda e, f: (layer, e, 0, f)),
                  pl.BlockSpec((None, None, tf, d), lambda e, f: (layer, e, f, 0)),
                  pl.BlockSpec((batch, e_pad, seq), lambda e, f: (0, 0, 0)),
                  pl.BlockSpec((batch, e_pad, seq), lambda e, f: (0, 0, 0))],
        out_specs=pl.BlockSpec((r, d), lambda e, f: (e, 0)),
        out_shape=jax.ShapeDtypeStruct((n_experts * r, d), BF16),
        scratch_shapes=[pltpu.VMEM((r, d), F32), pltpu.VMEM((r, 1), F32)],
        compiler_params=_params(("parallel", "arbitrary"), vmem),
        name="expert_ffn",
    )(xs, w_gate, w_up, w_down, prow, arow)


def _combine_kernel(ok_ref, x_ref, y_ref, pcol_ref, g_ref, out_ref, *, n_group, n_tiles, final_norm):
    b = pl.program_id(0)
    t = pl.program_id(1)
    tt, d = x_ref.shape
    n_experts, cap, _ = y_ref.shape
    window = _window_size(cap, n_tiles)
    lane = lax.broadcasted_iota(jnp.int32, (tt, LANES), 1)
    pcol = pcol_ref[...]

    def scatter(rows_of, n_slots, first_slot):
        slot = (lax.broadcasted_iota(jnp.int32, (tt, n_slots), 1) + first_slot).astype(F32)
        acc = x_ref[...]
        for e0 in range(0, n_experts, n_group):
            hots = []
            for e in range(e0, e0 + n_group):
                slot_of_token = jnp.sum(jnp.where(lane == e, pcol, 0.0),
                                        axis=1, keepdims=True)
                hots.append((slot == slot_of_token).astype(BF16))
            acc = acc + jnp.dot(jnp.concatenate(hots, axis=1), rows_of(e0),
                                preferred_element_type=F32)
        out_ref[...] = _rms(acc, g_ref[...]) if final_norm else acc

    in_window = ok_ref[b, t] != 0

    @pl.when(in_window)
    def _():
        first = pl.multiple_of(_window_start(t, cap, n_tiles), _window_margin(cap))
        scatter(lambda e0: jnp.concatenate([y_ref[e, pl.ds(first, window), :]
                                            for e in range(e0, e0 + n_group)], axis=0),
                window, first)

    @pl.when(jnp.logical_not(in_window))
    def _():
        scatter(lambda e0: y_ref[e0:e0 + n_group].reshape(n_group * cap, d), cap, 0)


def combine(x_res, y, prow, pcol, gain, *, batch, seq, n_experts, capacity, final_norm):
    d = x_res.shape[1]
    nt = COMBINE_TILES
    tt = seq // nt
    ng = _tile(n_experts, 8)
    y4 = y.reshape(n_experts, batch, capacity, d)
    inside = _slots_inside_windows(prow, n_experts=n_experts, capacity=capacity, n_tiles=nt)
    ok = jnp.all(inside, axis=1).astype(jnp.int32)
    vmem = 2 * (2 * tt * d * 4 + n_experts * capacity * d * 2 + tt * LANES * 4) \
        + 3 * tt * d * 4 + ng * capacity * d * 2 + 2 * tt * ng * capacity * 4
    return pl.pallas_call(
        functools.partial(_combine_kernel, n_group=ng, n_tiles=nt, final_norm=final_norm),
        grid_spec=pltpu.PrefetchScalarGridSpec(
            num_scalar_prefetch=1,
            grid=(batch, nt),
            in_specs=[pl.BlockSpec((tt, d), lambda b, t, ok_ref: (b * nt + t, 0)),
                      pl.BlockSpec((n_experts, None, capacity, d), lambda b, t, ok_ref: (0, b, 0, 0)),
                      pl.BlockSpec((tt, LANES), lambda b, t, ok_ref: (b * nt + t, 0)),
                      pl.BlockSpec((1, d), lambda b, t, ok_ref: (0, 0))],
            out_specs=pl.BlockSpec((tt, d), lambda b, t, ok_ref: (b * nt + t, 0))),
        out_shape=jax.ShapeDtypeStruct((batch * seq, d), F32),
        compiler_params=_params(("parallel", "arbitrary"), vmem),
        name="combine",
    )(ok, x_res, y4, pcol, gain.reshape(1, d))


def _pool_kernel(x_ref, prev_ref, next_ref, ng_ref, w_ref, s_ref, g_ref, xo_ref, ho_ref, pad_ref,
                 *, seq, windows):
    i = pl.program_id(0)
    tm, d = x_ref.shape
    halo = prev_ref.shape[0]
    pg = w_ref.shape[1]
    tiles_per_seq = seq // tm
    tile_in_seq = i % tiles_per_seq
    norm_gain = ng_ref[...]
    h = _rms(x_ref[...], norm_gain)
    pad_ref[0:halo, :] = jnp.where(tile_in_seq == 0, 0.0, _rms(prev_ref[...], norm_gain))
    pad_ref[halo:halo + tm, :] = h
    pad_ref[halo + tm:2 * halo + tm, :] = jnp.where(tile_in_seq == tiles_per_seq - 1, 0.0,
                                                    _rms(next_ref[...], norm_gain))
    t = tile_in_seq * tm + lax.broadcasted_iota(jnp.int32, (tm, 1), 0)
    for gi, w in enumerate(windows):
        cols = slice(gi * pg, (gi + 1) * pg)
        acc = pad_ref[:, cols]
        span = 1
        while span < w:
            acc = acc + pltpu.roll(acc, acc.shape[0] - span, axis=0)
            span *= 2
        total = acc[halo - w // 2:halo - w // 2 + tm, :]
        count = (jnp.minimum(t + (w - w // 2), seq) - jnp.maximum(t - w // 2, 0)).astype(F32)
        pooled = (total / count - h[:, cols]).astype(BF16)
        mixed = jnp.dot(pooled, w_ref[gi], preferred_element_type=F32)
        xo_ref[:, cols] = mixed * s_ref[:, cols] + x_ref[:, cols]
    ho_ref[...] = _rms(xo_ref[...], g_ref[...]).astype(ho_ref.dtype)


def pool_mixer(x, norm_gain, w_pool, pool_scale, next_gain, *, seq):
    m, d = x.shape
    ng, pg, _ = w_pool.shape
    halo = SUBLANES
    assert all(w & (w - 1) == 0 and w // 2 <= halo for w in POOL_WINDOWS) and ng == len(POOL_WINDOWS)
    tm = _tile(seq, 1024)
    per = tm // halo
    n_halo_blocks = m // halo
    vmem = 2 * (2 * tm * d * 4 + 2 * halo * d * 4 + ng * pg * pg * 2 + tm * d * 2) \
        + (tm + 2 * halo) * d * 4 + 6 * tm * d * 4
    return pl.pallas_call(
        functools.partial(_pool_kernel, seq=seq, windows=POOL_WINDOWS),
        grid=(m // tm,),
        in_specs=[pl.BlockSpec((tm, d), lambda i: (i, 0)),
                  pl.BlockSpec((halo, d), lambda i: (jnp.maximum(i * per - 1, 0), 0)),
                  pl.BlockSpec((halo, d), lambda i: (jnp.minimum((i + 1) * per, n_halo_blocks - 1), 0)),
                  pl.BlockSpec((1, d), lambda i: (0, 0)),
                  pl.BlockSpec((ng, pg, pg), lambda i: (0, 0, 0)),
                  pl.BlockSpec((1, d), lambda i: (0, 0)),
                  pl.BlockSpec((1, d), lambda i: (0, 0))],
        out_specs=[pl.BlockSpec((tm, d), lambda i: (i, 0)),
                   pl.BlockSpec((tm, d), lambda i: (i, 0))],
        out_shape=[jax.ShapeDtypeStruct((m, d), F32), jax.ShapeDtypeStruct((m, d), BF16)],
        scratch_shapes=[pltpu.VMEM((tm + 2 * halo, d), F32)],
        compiler_params=_params(("parallel",), vmem),
        name="pool_mixer",
    )(x, x, x, norm_gain.reshape(1, d), w_pool, pool_scale.reshape(1, d), next_gain.reshape(1, d))


def _rope_tables(seq):
    half = HEAD_DIM // 2
    rows = seq // GRID_W
    row_idx = jnp.repeat(jnp.arange(rows), GRID_W).astype(F32)
    col_idx = jnp.tile(jnp.arange(GRID_W), rows).astype(F32)
    inv_freq = 1.0 / (ROPE_THETA ** (jnp.arange(0, half, 2, dtype=F32) / half))
    ang = jnp.concatenate([row_idx[:, None] * inv_freq[None, :],
                           col_idx[:, None] * inv_freq[None, :]], axis=-1)
    cos, sin = jnp.cos(ang), jnp.sin(ang)
    zero = jnp.zeros_like(sin)
    cos_full = jnp.stack([cos, cos], axis=-1).reshape(seq, HEAD_DIM)
    sin_even = jnp.stack([-sin, zero], axis=-1).reshape(seq, HEAD_DIM)
    sin_odd = jnp.stack([zero, sin], axis=-1).reshape(seq, HEAD_DIM)
    return cos_full, sin_even, sin_odd


def _split_router(w_router):
    d, n_experts = w_router.shape
    w = jnp.pad(w_router, ((0, 0), (0, LANES - n_experts)))
    hi = w.astype(BF16)
    lo = (w - hi.astype(F32)).astype(BF16)
    return jnp.concatenate([hi, lo], axis=1), n_experts


def _moe(x_res, h, aff, w_gate, w_up, w_down, layer, gain, *, batch, seq, final_norm):
    n_experts = w_gate.shape[1]
    capacity = EC_CAPACITY_FACTOR * seq // n_experts
    d = h.shape[1]
    prow, arow, pcol = route(aff, batch=batch, seq=seq, n_experts=n_experts, capacity=capacity)
    xs = gather_tokens(h, prow, batch=batch, seq=seq, n_experts=n_experts, capacity=capacity)
    y = expert_ffn(xs.reshape(n_experts * batch * capacity, d), w_gate, w_up, w_down, prow, arow,
                   layer, rows_per_expert=batch * capacity, capacity=capacity)
    return combine(x_res, y, prow, pcol, gain, batch=batch, seq=seq,
                   n_experts=n_experts, capacity=capacity, final_norm=final_norm)


def _cross_block(x, h, kv, w_qo, layer, ffn_norm, router_w, *, batch, seq, mem_len):
    return cross_block(h, w_qo, kv, layer, x, ffn_norm[layer], _split_router(router_w[layer]),
                       batch=batch, seq=seq, mem_len=mem_len)


def kernel(x, mem, mix_norm, attn_w_in, q_gain, k_gain, fourier_w, attn_w_out, pool_w, pool_scale,
           cross_norm, mem_norm, cross_w_q, cross_w_k, cross_w_v, cross_w_o, ffn_norm, router_w,
           expert_w_gate, expert_w_up, expert_w_down, final_norm):
    batch, seq, d = x.shape
    mem_len = mem.shape[1]
    depth = mix_norm.shape[0]
    attn_width = N_Q_HEADS * HEAD_DIM
    kv_width = N_KV_HEADS * HEAD_DIM
    fourier_width = N_FOURIER_GROUPS * FOURIER_GROUP
    rope_width = attn_width + kv_width
    cos_t, sin_a, sin_b = _rope_tables(seq)

    xf = x.reshape(batch * seq, d)
    kv, w_qo = memory_kv(mem.reshape(batch * mem_len, d), mem_norm, cross_w_k, cross_w_v,
                         cross_w_q, cross_w_o)
    cross = functools.partial(_cross_block, kv=kv, w_qo=w_qo, ffn_norm=ffn_norm, router_w=router_w,
                              batch=batch, seq=seq, mem_len=mem_len)
    for layer in range(depth):
        i = layer // 2
        if layer % 2 == 0:
            gain_cols = jnp.concatenate([jnp.tile(q_gain[i], N_Q_HEADS), jnp.tile(k_gain[i], N_KV_HEADS),
                                         jnp.ones((kv_width + fourier_width,), F32)]).reshape(1, -1)
            proj = proj_in(xf, mix_norm[layer], cast_weight(attn_w_in, (i,)), gain_cols,
                           cos_t, sin_a, sin_b, seq=seq, rope_width=rope_width)
            o_attn = gqa_attention(proj, batch=batch, seq=seq, attn_width=attn_width, kv_width=kv_width)
            o_four = fourier_mixer(proj, fourier_w[i], batch=batch, seq=seq,
                                   col0=attn_width + 2 * kv_width)
            xf, h = matmul_residual_norm([o_attn, o_four], cast_weight(attn_w_out, (i,)), xf,
                                         cross_norm[layer])
        else:
            xf, h = pool_mixer(xf, mix_norm[layer], pool_w[i].astype(BF16), pool_scale[i],
                               cross_norm[layer], seq=seq)
        xf, h, aff = cross(xf, h, layer=layer)
        xf = _moe(xf, h, aff, expert_w_gate, expert_w_up, expert_w_down, layer, final_norm,
                  batch=batch, seq=seq, final_norm=layer == depth - 1)
    return xf.reshape(batch, seq, d)
```
